```python
import math
import jax, jax.numpy as jnp
from jax import lax
import numpy as np

D_MODEL = 1024
BATCH = 8
SEQ = 8192
DEPTH = 1

N_MEM = 256
D_LRU = D_MODEL // 2
D_CONF = D_MODEL // 2
D_MIX = D_LRU + D_CONF
LRU_HEADS = 8
LRU_HD = D_LRU // LRU_HEADS
LRU_CONV = 4
RG_C = 8.0
CONF_CONV = 31
CONF_GROUPS = 8
XA_HEADS = 4
XA_HD = D_MODEL // XA_HEADS
D_FF = 3 * D_MODEL
FFN_CONV = 3
EPS = 1e-6

kernel_name = "hybrid_rglru_conformer_xattn_convffn"


def rms_norm(x, g):
    xf = x.astype(jnp.float32)
    y = xf * lax.rsqrt(jnp.mean(xf * xf, axis=-1, keepdims=True) + EPS)
    return y.astype(x.dtype) * g


def layer_norm(x, g, b):
    xf = x.astype(jnp.float32)
    mu = jnp.mean(xf, axis=-1, keepdims=True)
    xc = xf - mu
    var = jnp.mean(xc * xc, axis=-1, keepdims=True)
    return (xc * lax.rsqrt(var + EPS)).astype(x.dtype) * g + b


def causal_dwconv(x, w, b):
    k = w.shape[0]
    y = lax.conv_general_dilated(
        x, w[:, None, :], window_strides=(1,), padding=[(k - 1, 0)],
        dimension_numbers=("NWC", "WIO", "NWC"), feature_group_count=x.shape[-1])
    return y + b


def rg_lru(x, w_a, b_a, w_x, b_x, lam):
    bsz, s, c = x.shape
    xh = x.reshape(bsz, s, LRU_HEADS, LRU_HD)
    r = jax.nn.sigmoid(jnp.einsum("bshi,hij->bshj", xh, w_a).reshape(bsz, s, c) + b_a)
    i = jax.nn.sigmoid(jnp.einsum("bshi,hij->bshj", xh, w_x).reshape(bsz, s, c) + b_x)
    log_a = -RG_C * r.astype(jnp.float32) * jax.nn.softplus(-lam.astype(jnp.float32))
    a = jnp.exp(log_a)
    mult = jnp.sqrt(-jnp.expm1(2.0 * log_a))
    u = mult * (i * x).astype(jnp.float32)

    def combine(left, right):
        a1, b1 = left
        a2, b2 = right
        return a1 * a2, a2 * b1 + b2

    _, h = lax.associative_scan(combine, (a, u), axis=1)
    return h.astype(x.dtype)


def _fwd_setup_inputs(seed: int = 0) -> dict:
    key = jax.random.key(seed)
    ks = iter(jax.random.split(key, 40))
    f32 = jnp.float32

    def nrm(shape, scale):
        return jax.random.normal(next(ks), shape, f32) * scale

    def gain(shape):
        return 1.0 + 0.01 * jax.random.normal(next(ks), shape, f32)

    L = DEPTH
    x = jax.random.normal(next(ks), (BATCH, SEQ, D_MODEL), f32)
    mem = jax.random.normal(next(ks), (BATCH, N_MEM, D_MODEL), f32)
    a_c = jax.random.uniform(next(ks), (L, D_LRU), f32, 0.9, 0.999)
    sig = a_c ** (1.0 / RG_C)
    lam = jnp.log(sig) - jnp.log1p(-sig)
    return {
        "x": x,
        "mem": mem,
        "mix_norm_g": gain((L, D_MODEL)),
        "w_in": nrm((L, D_MODEL, 2 * D_MIX), D_MODEL ** -0.5),
        "lru_conv_w": nrm((L, LRU_CONV, D_LRU), LRU_CONV ** -0.5),
        "lru_conv_b": nrm((L, D_LRU), 0.01),
        "lru_w_a": nrm((L, LRU_HEADS, LRU_HD, LRU_HD), LRU_HD ** -0.5),
        "lru_b_a": nrm((L, D_LRU), 0.01),
        "lru_w_x": nrm((L, LRU_HEADS, LRU_HD, LRU_HD), LRU_HD ** -0.5),
        "lru_b_x": nrm((L, D_LRU), 0.01),
        "lru_lambda": lam,
        "conf_conv_w": nrm((L, CONF_CONV, D_CONF), CONF_CONV ** -0.5),
        "conf_conv_b": nrm((L, D_CONF), 0.01),
        "conf_ln_g": gain((L, D_CONF)),
        "conf_ln_b": nrm((L, D_CONF), 0.01),
        "w_out": nrm((L, D_MIX, D_MODEL), D_MIX ** -0.5),
        "xa_norm_g": gain((L, D_MODEL)),
        "mem_norm_g": gain((L, D_MODEL)),
        "w_q": nrm((L, D_MODEL, D_MODEL), D_MODEL ** -0.5),
        "w_kv": nrm((L, D_MODEL, 2 * D_MODEL), D_MODEL ** -0.5),
        "w_o": nrm((L, D_MODEL, D_MODEL), D_MODEL ** -0.5),
        "ffn_norm_g": gain((L, D_MODEL)),
        "w_up": nrm((L, D_MODEL, 2 * D_FF), D_MODEL ** -0.5),
        "ffn_conv_w": nrm((L, FFN_CONV, D_FF), FFN_CONV ** -0.5),
        "ffn_conv_b": nrm((L, D_FF), 0.01),
        "w_down": nrm((L, D_FF, D_MODEL), D_FF ** -0.5),
        "final_norm_g": gain((D_MODEL,)),
    }


def _fwd_reference(x, mem, mix_norm_g, w_in, lru_conv_w, lru_conv_b, lru_w_a, lru_b_a,
              lru_w_x, lru_b_x, lru_lambda, conf_conv_w, conf_conv_b, conf_ln_g,
              conf_ln_b, w_out, xa_norm_g, mem_norm_g, w_q, w_kv, w_o, ffn_norm_g,
              w_up, ffn_conv_w, ffn_conv_b, w_down, final_norm_g):
    bsz, s, d = x.shape
    m_len = mem.shape[1]
    for l in range(DEPTH):
        h = rms_norm(x, mix_norm_g[l])
        z = h @ w_in[l]
        lru_x, lru_gate, conf_a, conf_b = jnp.split(
            z, [D_LRU, 2 * D_LRU, 2 * D_LRU + D_CONF], axis=-1)
        lru_x = causal_dwconv(lru_x, lru_conv_w[l], lru_conv_b[l])
        y_lru = rg_lru(lru_x, lru_w_a[l], lru_b_a[l], lru_w_x[l], lru_b_x[l],
                       lru_lambda[l]) * jax.nn.gelu(lru_gate, approximate=True)
        c = conf_a * jax.nn.sigmoid(conf_b)
        c = causal_dwconv(c, conf_conv_w[l], conf_conv_b[l])
        c = jax.nn.silu(layer_norm(c, conf_ln_g[l], conf_ln_b[l]))
        y = jnp.concatenate([y_lru, c], axis=-1) @ w_out[l]
        x = x + y

        h = rms_norm(x, xa_norm_g[l])
        m = rms_norm(mem, mem_norm_g[l])
        q = (h @ w_q[l]).reshape(bsz, s, XA_HEADS, XA_HD)
        kv = m @ w_kv[l]
        k, v = jnp.split(kv, 2, axis=-1)
        k = k.reshape(bsz, m_len, XA_HEADS, XA_HD)
        v = v.reshape(bsz, m_len, XA_HEADS, XA_HD)
        scores = jnp.einsum("bshd,bmhd->bhsm", q, k).astype(jnp.float32) * (XA_HD ** -0.5)
        p = jax.nn.softmax(scores, axis=-1).astype(v.dtype)
        o = jnp.einsum("bhsm,bmhd->bshd", p, v).reshape(bsz, s, d)
        x = x + o @ w_o[l]

        h = rms_norm(x, ffn_norm_g[l])
        gu = h @ w_up[l]
        g, u = jnp.split(gu, 2, axis=-1)
        g = causal_dwconv(g, ffn_conv_w[l], ffn_conv_b[l])
        x = x + (jax.nn.gelu(g, approximate=True) * u) @ w_down[l]
    return rms_norm(x, final_norm_g)


import jax as _jax
import jax.numpy as _jnp

TWIN_FORMAT = 'train_step'
FWD_PARAMS = ['x', 'mem', 'mix_norm_g', 'w_in', 'lru_conv_w', 'lru_conv_b', 'lru_w_a', 'lru_b_a', 'lru_w_x', 'lru_b_x', 'lru_lambda', 'conf_conv_w', 'conf_conv_b', 'conf_ln_g', 'conf_ln_b', 'w_out', 'xa_norm_g', 'mem_norm_g', 'w_q', 'w_kv', 'w_o', 'ffn_norm_g', 'w_up', 'ffn_conv_w', 'ffn_conv_b', 'w_down', 'final_norm_g']
TWIN_WEIGHTS = ['mix_norm_g', 'w_in', 'lru_conv_w', 'lru_conv_b', 'lru_w_a', 'lru_b_a', 'lru_w_x', 'lru_b_x', 'lru_lambda', 'conf_conv_w', 'conf_conv_b', 'conf_ln_g', 'conf_ln_b', 'w_out', 'xa_norm_g', 'mem_norm_g', 'w_q', 'w_kv', 'w_o', 'ffn_norm_g', 'w_up', 'ffn_conv_w', 'ffn_conv_b', 'w_down', 'final_norm_g']
TWIN_DIFF_INPUT = 'x'
TWIN_INPUTS = ['x', 'mem', 'mix_norm_g', 'w_in', 'lru_conv_w', 'lru_conv_b', 'lru_w_a', 'lru_b_a', 'lru_w_x', 'lru_b_x', 'lru_lambda', 'conf_conv_w', 'conf_conv_b', 'conf_ln_g', 'conf_ln_b', 'w_out', 'xa_norm_g', 'mem_norm_g', 'w_q', 'w_kv', 'w_o', 'ffn_norm_g', 'w_up', 'ffn_conv_w', 'ffn_conv_b', 'w_down', 'final_norm_g', 'loss_target', 'm_mix_norm_g', 'm_w_in', 'm_lru_conv_w', 'm_lru_conv_b', 'm_lru_w_a', 'm_lru_b_a', 'm_lru_w_x', 'm_lru_b_x', 'm_lru_lambda', 'm_conf_conv_w', 'm_conf_conv_b', 'm_conf_ln_g', 'm_conf_ln_b', 'm_w_out', 'm_xa_norm_g', 'm_mem_norm_g', 'm_w_q', 'm_w_kv', 'm_w_o', 'm_ffn_norm_g', 'm_w_up', 'm_ffn_conv_w', 'm_ffn_conv_b', 'm_w_down', 'm_final_norm_g', 'v_mix_norm_g', 'v_w_in', 'v_lru_conv_w', 'v_lru_conv_b', 'v_lru_w_a', 'v_lru_b_a', 'v_lru_w_x', 'v_lru_b_x', 'v_lru_lambda', 'v_conf_conv_w', 'v_conf_conv_b', 'v_conf_ln_g', 'v_conf_ln_b', 'v_w_out', 'v_xa_norm_g', 'v_mem_norm_g', 'v_w_q', 'v_w_kv', 'v_w_o', 'v_ffn_norm_g', 'v_w_up', 'v_ffn_conv_w', 'v_ffn_conv_b', 'v_w_down', 'v_final_norm_g']
TWIN_OUTPUTS = ['loss', 'grad_x', 'grad_mix_norm_g', 'grad_w_in', 'grad_lru_conv_w', 'grad_lru_conv_b', 'grad_lru_w_a', 'grad_lru_b_a', 'grad_lru_w_x', 'grad_lru_b_x', 'grad_lru_lambda', 'grad_conf_conv_w', 'grad_conf_conv_b', 'grad_conf_ln_g', 'grad_conf_ln_b', 'grad_w_out', 'grad_xa_norm_g', 'grad_mem_norm_g', 'grad_w_q', 'grad_w_kv', 'grad_w_o', 'grad_ffn_norm_g', 'grad_w_up', 'grad_ffn_conv_w', 'grad_ffn_conv_b', 'grad_w_down', 'grad_final_norm_g', 'delta_mix_norm_g', 'delta_w_in', 'delta_lru_conv_w', 'delta_lru_conv_b', 'delta_lru_w_a', 'delta_lru_b_a', 'delta_lru_w_x', 'delta_lru_b_x', 'delta_lru_lambda', 'delta_conf_conv_w', 'delta_conf_conv_b', 'delta_conf_ln_g', 'delta_conf_ln_b', 'delta_w_out', 'delta_xa_norm_g', 'delta_mem_norm_g', 'delta_w_q', 'delta_w_kv', 'delta_w_o', 'delta_ffn_norm_g', 'delta_w_up', 'delta_ffn_conv_w', 'delta_ffn_conv_b', 'delta_w_down', 'delta_final_norm_g', 'new_m_mix_norm_g', 'new_m_w_in', 'new_m_lru_conv_w', 'new_m_lru_conv_b', 'new_m_lru_w_a', 'new_m_lru_b_a', 'new_m_lru_w_x', 'new_m_lru_b_x', 'new_m_lru_lambda', 'new_m_conf_conv_w', 'new_m_conf_conv_b', 'new_m_conf_ln_g', 'new_m_conf_ln_b', 'new_m_w_out', 'new_m_xa_norm_g', 'new_m_mem_norm_g', 'new_m_w_q', 'new_m_w_kv', 'new_m_w_o', 'new_m_ffn_norm_g', 'new_m_w_up', 'new_m_ffn_conv_w', 'new_m_ffn_conv_b', 'new_m_w_down', 'new_m_final_norm_g', 'new_v_mix_norm_g', 'new_v_w_in', 'new_v_lru_conv_w', 'new_v_lru_conv_b', 'new_v_lru_w_a', 'new_v_lru_b_a', 'new_v_lru_w_x', 'new_v_lru_b_x', 'new_v_lru_lambda', 'new_v_conf_conv_w', 'new_v_conf_conv_b', 'new_v_conf_ln_g', 'new_v_conf_ln_b', 'new_v_w_out', 'new_v_xa_norm_g', 'new_v_mem_norm_g', 'new_v_w_q', 'new_v_w_kv', 'new_v_w_o', 'new_v_ffn_norm_g', 'new_v_w_up', 'new_v_ffn_conv_w', 'new_v_ffn_conv_b', 'new_v_w_down', 'new_v_final_norm_g']
TWIN_LEAF_KINDS = {'loss': 'loss', 'grad_x': 'grad_x', 'grad_mix_norm_g': 'grad_w', 'grad_w_in': 'grad_w', 'grad_lru_conv_w': 'grad_w', 'grad_lru_conv_b': 'grad_w', 'grad_lru_w_a': 'grad_w', 'grad_lru_b_a': 'grad_w', 'grad_lru_w_x': 'grad_w', 'grad_lru_b_x': 'grad_w', 'grad_lru_lambda': 'grad_w', 'grad_conf_conv_w': 'grad_w', 'grad_conf_conv_b': 'grad_w', 'grad_conf_ln_g': 'grad_w', 'grad_conf_ln_b': 'grad_w', 'grad_w_out': 'grad_w', 'grad_xa_norm_g': 'grad_w', 'grad_mem_norm_g': 'grad_w', 'grad_w_q': 'grad_w', 'grad_w_kv': 'grad_w', 'grad_w_o': 'grad_w', 'grad_ffn_norm_g': 'grad_w', 'grad_w_up': 'grad_w', 'grad_ffn_conv_w': 'grad_w', 'grad_ffn_conv_b': 'grad_w', 'grad_w_down': 'grad_w', 'grad_final_norm_g': 'grad_w', 'delta_mix_norm_g': 'delta_w', 'delta_w_in': 'delta_w', 'delta_lru_conv_w': 'delta_w', 'delta_lru_conv_b': 'delta_w', 'delta_lru_w_a': 'delta_w', 'delta_lru_b_a': 'delta_w', 'delta_lru_w_x': 'delta_w', 'delta_lru_b_x': 'delta_w', 'delta_lru_lambda': 'delta_w', 'delta_conf_conv_w': 'delta_w', 'delta_conf_conv_b': 'delta_w', 'delta_conf_ln_g': 'delta_w', 'delta_conf_ln_b': 'delta_w', 'delta_w_out': 'delta_w', 'delta_xa_norm_g': 'delta_w', 'delta_mem_norm_g': 'delta_w', 'delta_w_q': 'delta_w', 'delta_w_kv': 'delta_w', 'delta_w_o': 'delta_w', 'delta_ffn_norm_g': 'delta_w', 'delta_w_up': 'delta_w', 'delta_ffn_conv_w': 'delta_w', 'delta_ffn_conv_b': 'delta_w', 'delta_w_down': 'delta_w', 'delta_final_norm_g': 'delta_w', 'new_m_mix_norm_g': 'new_m', 'new_m_w_in': 'new_m', 'new_m_lru_conv_w': 'new_m', 'new_m_lru_conv_b': 'new_m', 'new_m_lru_w_a': 'new_m', 'new_m_lru_b_a': 'new_m', 'new_m_lru_w_x': 'new_m', 'new_m_lru_b_x': 'new_m', 'new_m_lru_lambda': 'new_m', 'new_m_conf_conv_w': 'new_m', 'new_m_conf_conv_b': 'new_m', 'new_m_conf_ln_g': 'new_m', 'new_m_conf_ln_b': 'new_m', 'new_m_w_out': 'new_m', 'new_m_xa_norm_g': 'new_m', 'new_m_mem_norm_g': 'new_m', 'new_m_w_q': 'new_m', 'new_m_w_kv': 'new_m', 'new_m_w_o': 'new_m', 'new_m_ffn_norm_g': 'new_m', 'new_m_w_up': 'new_m', 'new_m_ffn_conv_w': 'new_m', 'new_m_ffn_conv_b': 'new_m', 'new_m_w_down': 'new_m', 'new_m_final_norm_g': 'new_m', 'new_v_mix_norm_g': 'new_v', 'new_v_w_in': 'new_v', 'new_v_lru_conv_w': 'new_v', 'new_v_lru_conv_b': 'new_v', 'new_v_lru_w_a': 'new_v', 'new_v_lru_b_a': 'new_v', 'new_v_lru_w_x': 'new_v', 'new_v_lru_b_x': 'new_v', 'new_v_lru_lambda': 'new_v', 'new_v_conf_conv_w': 'new_v', 'new_v_conf_conv_b': 'new_v', 'new_v_conf_ln_g': 'new_v', 'new_v_conf_ln_b': 'new_v', 'new_v_w_out': 'new_v', 'new_v_xa_norm_g': 'new_v', 'new_v_mem_norm_g': 'new_v', 'new_v_w_q': 'new_v', 'new_v_w_kv': 'new_v', 'new_v_w_o': 'new_v', 'new_v_ffn_norm_g': 'new_v', 'new_v_w_up': 'new_v', 'new_v_ffn_conv_w': 'new_v', 'new_v_ffn_conv_b': 'new_v', 'new_v_w_down': 'new_v', 'new_v_final_norm_g': 'new_v'}


def _forward(args):
    return _fwd_reference(*[args[k] for k in FWD_PARAMS])


def _output_shape():
    def fwd():
        inp = _fwd_setup_inputs(0)
        return _fwd_reference(*[inp[k] for k in FWD_PARAMS])
    out = _jax.eval_shape(fwd)
    return out.shape, out.dtype

N_MICROBATCH = 1
ADAM_LR = 0.001
ADAM_B1 = 0.9
ADAM_B2 = 0.999
ADAM_EPS = 1e-08
ADAM_WD = 0.01
ADAM_STEP = 10
PER_EXAMPLE_BATCH_AXIS = {'x': 0, 'mem': 0, 'loss_target': 0}
SHARED_INPUTS = []
_WEIGHT_DTYPES = {'mix_norm_g': _jnp.float32, 'w_in': _jnp.float32, 'lru_conv_w': _jnp.float32, 'lru_conv_b': _jnp.float32, 'lru_w_a': _jnp.float32, 'lru_b_a': _jnp.float32, 'lru_w_x': _jnp.float32, 'lru_b_x': _jnp.float32, 'lru_lambda': _jnp.float32, 'conf_conv_w': _jnp.float32, 'conf_conv_b': _jnp.float32, 'conf_ln_g': _jnp.float32, 'conf_ln_b': _jnp.float32, 'w_out': _jnp.float32, 'xa_norm_g': _jnp.float32, 'mem_norm_g': _jnp.float32, 'w_q': _jnp.float32, 'w_kv': _jnp.float32, 'w_o': _jnp.float32, 'ffn_norm_g': _jnp.float32, 'w_up': _jnp.float32, 'ffn_conv_w': _jnp.float32, 'ffn_conv_b': _jnp.float32, 'w_down': _jnp.float32, 'final_norm_g': _jnp.float32}
MOMENT_SCALE = {'mix_norm_g': 1.901577e-01, 'w_in': 1.164546e-01, 'lru_conv_w': 1.319721e-01, 'lru_conv_b': 1.570601e+00, 'lru_w_a': 5.608336e-02, 'lru_b_a': 4.240765e-02, 'lru_w_x': 9.982213e-02, 'lru_b_x': 4.505608e-02, 'lru_lambda': 7.881792e-02, 'conf_conv_w': 1.546626e-01, 'conf_conv_b': 3.333957e-01, 'conf_ln_g': 1.888180e-01, 'conf_ln_b': 1.575796e-01, 'w_out': 1.372230e-01, 'xa_norm_g': 2.930094e-02, 'mem_norm_g': 4.146905e-02, 'w_q': 2.617063e-02, 'w_kv': 2.636946e-02, 'w_o': 2.675510e-02, 'ffn_norm_g': 1.990502e-01, 'w_up': 7.452334e-02, 'ffn_conv_w': 7.595399e-02, 'ffn_conv_b': 7.445394e-02, 'w_down': 1.269002e-01, 'final_norm_g': 6.401319e+01}


def _to_microbatches(a, axis):
    t = _jnp.moveaxis(a, axis, 0)
    t = t.reshape((N_MICROBATCH, t.shape[0] // N_MICROBATCH) + t.shape[1:])
    return _jnp.moveaxis(t, 1, axis + 1)


def setup_inputs(seed: int = 0) -> dict:
    inp = _fwd_setup_inputs(seed)
    key = _jax.random.fold_in(_jax.random.key(seed), 7919)
    shape, _ = _output_shape()
    out = dict(inp)
    out["loss_target"] = _jax.random.normal(_jax.random.fold_in(key, 0), shape, _jnp.float32)
    for i, name in enumerate(TWIN_WEIGHTS):
        w = inp[name].astype(_jnp.float32)
        if MOMENT_SCALE is None:
            s = _jnp.sqrt(_jnp.mean(_jnp.square(w)) + 1e-30)
        else:
            s = MOMENT_SCALE[name]
        km, kv = _jax.random.split(_jax.random.fold_in(key, i + 1))
        out[name] = w
        out["m_" + name] = s * _jax.random.normal(km, w.shape, _jnp.float32)
        out["v_" + name] = (s * s) * _jax.random.uniform(kv, w.shape, _jnp.float32, 0.5, 1.5)
    if N_MICROBATCH > 1:
        for name, axis in PER_EXAMPLE_BATCH_AXIS.items():
            out[name] = _to_microbatches(out[name], axis)
    return {'x': out['x'], 'mem': out['mem'], 'mix_norm_g': out['mix_norm_g'], 'w_in': out['w_in'], 'lru_conv_w': out['lru_conv_w'], 'lru_conv_b': out['lru_conv_b'], 'lru_w_a': out['lru_w_a'], 'lru_b_a': out['lru_b_a'], 'lru_w_x': out['lru_w_x'], 'lru_b_x': out['lru_b_x'], 'lru_lambda': out['lru_lambda'], 'conf_conv_w': out['conf_conv_w'], 'conf_conv_b': out['conf_conv_b'], 'conf_ln_g': out['conf_ln_g'], 'conf_ln_b': out['conf_ln_b'], 'w_out': out['w_out'], 'xa_norm_g': out['xa_norm_g'], 'mem_norm_g': out['mem_norm_g'], 'w_q': out['w_q'], 'w_kv': out['w_kv'], 'w_o': out['w_o'], 'ffn_norm_g': out['ffn_norm_g'], 'w_up': out['w_up'], 'ffn_conv_w': out['ffn_conv_w'], 'ffn_conv_b': out['ffn_conv_b'], 'w_down': out['w_down'], 'final_norm_g': out['final_norm_g'], 'loss_target': out['loss_target'], 'm_mix_norm_g': out['m_mix_norm_g'], 'm_w_in': out['m_w_in'], 'm_lru_conv_w': out['m_lru_conv_w'], 'm_lru_conv_b': out['m_lru_conv_b'], 'm_lru_w_a': out['m_lru_w_a'], 'm_lru_b_a': out['m_lru_b_a'], 'm_lru_w_x': out['m_lru_w_x'], 'm_lru_b_x': out['m_lru_b_x'], 'm_lru_lambda': out['m_lru_lambda'], 'm_conf_conv_w': out['m_conf_conv_w'], 'm_conf_conv_b': out['m_conf_conv_b'], 'm_conf_ln_g': out['m_conf_ln_g'], 'm_conf_ln_b': out['m_conf_ln_b'], 'm_w_out': out['m_w_out'], 'm_xa_norm_g': out['m_xa_norm_g'], 'm_mem_norm_g': out['m_mem_norm_g'], 'm_w_q': out['m_w_q'], 'm_w_kv': out['m_w_kv'], 'm_w_o': out['m_w_o'], 'm_ffn_norm_g': out['m_ffn_norm_g'], 'm_w_up': out['m_w_up'], 'm_ffn_conv_w': out['m_ffn_conv_w'], 'm_ffn_conv_b': out['m_ffn_conv_b'], 'm_w_down': out['m_w_down'], 'm_final_norm_g': out['m_final_norm_g'], 'v_mix_norm_g': out['v_mix_norm_g'], 'v_w_in': out['v_w_in'], 'v_lru_conv_w': out['v_lru_conv_w'], 'v_lru_conv_b': out['v_lru_conv_b'], 'v_lru_w_a': out['v_lru_w_a'], 'v_lru_b_a': out['v_lru_b_a'], 'v_lru_w_x': out['v_lru_w_x'], 'v_lru_b_x': out['v_lru_b_x'], 'v_lru_lambda': out['v_lru_lambda'], 'v_conf_conv_w': out['v_conf_conv_w'], 'v_conf_conv_b': out['v_conf_conv_b'], 'v_conf_ln_g': out['v_conf_ln_g'], 'v_conf_ln_b': out['v_conf_ln_b'], 'v_w_out': out['v_w_out'], 'v_xa_norm_g': out['v_xa_norm_g'], 'v_mem_norm_g': out['v_mem_norm_g'], 'v_w_q': out['v_w_q'], 'v_w_kv': out['v_w_kv'], 'v_w_o': out['v_w_o'], 'v_ffn_norm_g': out['v_ffn_norm_g'], 'v_w_up': out['v_w_up'], 'v_ffn_conv_w': out['v_ffn_conv_w'], 'v_ffn_conv_b': out['v_ffn_conv_b'], 'v_w_down': out['v_w_down'], 'v_final_norm_g': out['v_final_norm_g']}


def _loss(weights, diff, rest, loss_target):
    with _jax.named_scope("forward"):
        args = {**rest, TWIN_DIFF_INPUT: diff, **{k: w.astype(_WEIGHT_DTYPES[k]) for k, w in weights.items()}}
        y = _forward(args)
    with _jax.named_scope("loss_head"):
        err = _jnp.square(y.astype(_jnp.float32) - loss_target)
        return 0.5 * _jnp.sum(_jnp.mean(err, axis=-1)) if err.ndim else 0.5 * err


def _adamw(w, g, m, v):
    m = ADAM_B1 * m + (1.0 - ADAM_B1) * g
    v = ADAM_B2 * v + (1.0 - ADAM_B2) * _jnp.square(g)
    m_hat = m / (1.0 - ADAM_B1 ** ADAM_STEP)
    v_hat = v / (1.0 - ADAM_B2 ** ADAM_STEP)
    delta = -ADAM_LR * (m_hat / (_jnp.sqrt(v_hat) + ADAM_EPS) + ADAM_WD * w)
    return delta, m, v


def reference(x, mem, mix_norm_g, w_in, lru_conv_w, lru_conv_b, lru_w_a, lru_b_a, lru_w_x, lru_b_x, lru_lambda, conf_conv_w, conf_conv_b, conf_ln_g, conf_ln_b, w_out, xa_norm_g, mem_norm_g, w_q, w_kv, w_o, ffn_norm_g, w_up, ffn_conv_w, ffn_conv_b, w_down, final_norm_g, loss_target, m_mix_norm_g, m_w_in, m_lru_conv_w, m_lru_conv_b, m_lru_w_a, m_lru_b_a, m_lru_w_x, m_lru_b_x, m_lru_lambda, m_conf_conv_w, m_conf_conv_b, m_conf_ln_g, m_conf_ln_b, m_w_out, m_xa_norm_g, m_mem_norm_g, m_w_q, m_w_kv, m_w_o, m_ffn_norm_g, m_w_up, m_ffn_conv_w, m_ffn_conv_b, m_w_down, m_final_norm_g, v_mix_norm_g, v_w_in, v_lru_conv_w, v_lru_conv_b, v_lru_w_a, v_lru_b_a, v_lru_w_x, v_lru_b_x, v_lru_lambda, v_conf_conv_w, v_conf_conv_b, v_conf_ln_g, v_conf_ln_b, v_w_out, v_xa_norm_g, v_mem_norm_g, v_w_q, v_w_kv, v_w_o, v_ffn_norm_g, v_w_up, v_ffn_conv_w, v_ffn_conv_b, v_w_down, v_final_norm_g):
    given = dict(x=x, mem=mem, mix_norm_g=mix_norm_g, w_in=w_in, lru_conv_w=lru_conv_w, lru_conv_b=lru_conv_b, lru_w_a=lru_w_a, lru_b_a=lru_b_a, lru_w_x=lru_w_x, lru_b_x=lru_b_x, lru_lambda=lru_lambda, conf_conv_w=conf_conv_w, conf_conv_b=conf_conv_b, conf_ln_g=conf_ln_g, conf_ln_b=conf_ln_b, w_out=w_out, xa_norm_g=xa_norm_g, mem_norm_g=mem_norm_g, w_q=w_q, w_kv=w_kv, w_o=w_o, ffn_norm_g=ffn_norm_g, w_up=w_up, ffn_conv_w=ffn_conv_w, ffn_conv_b=ffn_conv_b, w_down=w_down, final_norm_g=final_norm_g, loss_target=loss_target, m_mix_norm_g=m_mix_norm_g, m_w_in=m_w_in, m_lru_conv_w=m_lru_conv_w, m_lru_conv_b=m_lru_conv_b, m_lru_w_a=m_lru_w_a, m_lru_b_a=m_lru_b_a, m_lru_w_x=m_lru_w_x, m_lru_b_x=m_lru_b_x, m_lru_lambda=m_lru_lambda, m_conf_conv_w=m_conf_conv_w, m_conf_conv_b=m_conf_conv_b, m_conf_ln_g=m_conf_ln_g, m_conf_ln_b=m_conf_ln_b, m_w_out=m_w_out, m_xa_norm_g=m_xa_norm_g, m_mem_norm_g=m_mem_norm_g, m_w_q=m_w_q, m_w_kv=m_w_kv, m_w_o=m_w_o, m_ffn_norm_g=m_ffn_norm_g, m_w_up=m_w_up, m_ffn_conv_w=m_ffn_conv_w, m_ffn_conv_b=m_ffn_conv_b, m_w_down=m_w_down, m_final_norm_g=m_final_norm_g, v_mix_norm_g=v_mix_norm_g, v_w_in=v_w_in, v_lru_conv_w=v_lru_conv_w, v_lru_conv_b=v_lru_conv_b, v_lru_w_a=v_lru_w_a, v_lru_b_a=v_lru_b_a, v_lru_w_x=v_lru_w_x, v_lru_b_x=v_lru_b_x, v_lru_lambda=v_lru_lambda, v_conf_conv_w=v_conf_conv_w, v_conf_conv_b=v_conf_conv_b, v_conf_ln_g=v_conf_ln_g, v_conf_ln_b=v_conf_ln_b, v_w_out=v_w_out, v_xa_norm_g=v_xa_norm_g, v_mem_norm_g=v_mem_norm_g, v_w_q=v_w_q, v_w_kv=v_w_kv, v_w_o=v_w_o, v_ffn_norm_g=v_ffn_norm_g, v_w_up=v_w_up, v_ffn_conv_w=v_ffn_conv_w, v_ffn_conv_b=v_ffn_conv_b, v_w_down=v_w_down, v_final_norm_g=v_final_norm_g)
    weights = {n: given[n] for n in TWIN_WEIGHTS}
    shared = {n: given[n] for n in SHARED_INPUTS}
    per_example = {n: given[n] for n in ['x', 'mem']}
    grad_fn = _jax.value_and_grad(_loss, argnums=(0, 1))

    def one_microbatch(ex, loss_target):
        ex = dict(ex)
        diff = ex.pop(TWIN_DIFF_INPUT)
        return grad_fn(weights, diff, {**shared, **ex}, loss_target)

    if N_MICROBATCH == 1:
        loss, (grad_w, grad_x) = one_microbatch(per_example, given["loss_target"])
    else:
        def body(carry, xs):
            loss_sum, grad_sum = carry
            l_k, (gw_k, gx_k) = one_microbatch(xs[0], xs[1])
            with _jax.named_scope("update"):
                return (loss_sum + l_k, _jax.tree.map(_jnp.add, grad_sum, gw_k)), gx_k

        init = (_jnp.zeros((), _jnp.float32), _jax.tree.map(_jnp.zeros_like, weights))
        (loss, grad_w), grad_x = _jax.lax.scan(body, init, (per_example, given["loss_target"]))
    with _jax.named_scope("update"):
        delta_w, new_m, new_v = {}, {}, {}
        for n in TWIN_WEIGHTS:
            delta_w[n], new_m[n], new_v[n] = _adamw(weights[n], grad_w[n], given["m_" + n], given["v_" + n])
    return (loss, grad_x, *[grad_w[n] for n in TWIN_WEIGHTS], *[delta_w[n] for n in TWIN_WEIGHTS],
            *[new_m[n] for n in TWIN_WEIGHTS], *[new_v[n] for n in TWIN_WEIGHTS])
```

```python
import math

import jax
import jax.numpy as jnp
from jax import lax
from jax.experimental import pallas as pl
from jax.experimental.pallas import tpu as pltpu

F32 = jnp.float32
MXU_DTYPE = jnp.bfloat16
WIRE_DTYPE = jnp.bfloat16
EPS = 1e-6
RG_C = 8.0
XA_HEADS = 4
ADAM_LR, ADAM_B1, ADAM_B2, ADAM_EPS, ADAM_WD, ADAM_STEP = 0.001, 0.9, 0.999, 1e-08, 0.01, 10
VMEM_LIMIT_BYTES = 52 * 1024 * 1024
LANES = 1024
N_CHIPS = 4
N_DEV = 8
MESH = pl.DeviceIdType.MESH
GELU_C = math.sqrt(2.0 / math.pi)
GELU_K = 0.044715

WEIGHTS = ['mix_norm_g', 'w_in', 'lru_conv_w', 'lru_conv_b', 'lru_w_a', 'lru_b_a', 'lru_w_x', 'lru_b_x',
           'lru_lambda', 'conf_conv_w', 'conf_conv_b', 'conf_ln_g', 'conf_ln_b', 'w_out', 'xa_norm_g',
           'mem_norm_g', 'w_q', 'w_kv', 'w_o', 'ffn_norm_g', 'w_up', 'ffn_conv_w', 'ffn_conv_b', 'w_down',
           'final_norm_g']
BIG = ['w_in', 'w_kv', 'w_up', 'w_out', 'w_q', 'w_o', 'w_down']
SMALL = [n for n in WEIGHTS if n not in BIG]
COL_SHARDED_SMALL = ['lru_conv_w', 'conf_conv_w', 'ffn_conv_w']


def _params(*semantics):
    return pltpu.CompilerParams(dimension_semantics=semantics, vmem_limit_bytes=VMEM_LIMIT_BYTES)


def _tile(n, want, align=8):
    if n <= want:
        return n
    for t in range(want - want % align, 0, -align):
        if n % t == 0:
            return t
    raise ValueError((n, want, align))


def _mm(a, b):
    return jnp.dot(a.astype(MXU_DTYPE), b.astype(MXU_DTYPE), preferred_element_type=F32)


def _mm_nt(a, b):
    return lax.dot_general(a.astype(MXU_DTYPE), b.astype(MXU_DTYPE), (((1,), (1,)), ((), ())),
                           preferred_element_type=F32)


def _mm_tn(a, b):
    return lax.dot_general(a.astype(MXU_DTYPE), b.astype(MXU_DTYPE), (((0,), (0,)), ((), ())),
                           preferred_element_type=F32)


def _sigmoid(v):
    return 1.0 / (1.0 + jnp.exp(-v))


def _gelu(v):
    v2 = v * v
    t = jnp.tanh(GELU_C * (v + GELU_K * v * v2))
    return 0.5 * v * (1.0 + t), 0.5 * (1.0 + t) + 0.5 * v * (1.0 - t * t) * GELU_C * (1.0 + 3.0 * GELU_K * v2)


def _softplus_neg(lam):
    e = jnp.exp(-jnp.abs(lam))
    u = 1.0 + e
    log1p_e = jnp.where(u == 1.0, e, jnp.log(u) * e / jnp.where(u == 1.0, 1.0, u - 1.0))
    return jnp.maximum(-lam, 0.0) + log1p_e


def _rms(xv):
    rinv = lax.rsqrt(jnp.mean(xv * xv, axis=-1, keepdims=True) + EPS)
    return rinv, xv * rinv


def _rms_bwd(rinv, xhat, dxhat):
    return rinv * (dxhat - xhat * jnp.mean(dxhat * xhat, axis=-1, keepdims=True))


def _colsum(v):
    return jnp.sum(v, axis=0, keepdims=True)


def _wrow(w_ref, k, wcols):
    return w_ref[pl.ds(k, 1), :] if wcols is None else w_ref[pl.ds(k, 1), wcols]


def _causal_taps(buf_ref, halo, w_ref, taps, rows, wcols=None):
    acc = None
    for s in range(taps):
        term = _wrow(w_ref, taps - 1 - s, wcols) * buf_ref[pl.ds(halo - s, rows), :]
        acc = term if acc is None else acc + term
    return acc


def _anticausal_taps(buf_ref, w_ref, taps, rows, wcols=None):
    acc = None
    for s in range(taps):
        term = _wrow(w_ref, taps - 1 - s, wcols) * buf_ref[pl.ds(s, rows), :]
        acc = term if acc is None else acc + term
    return acc


def _tap_grads(dw_ref, dy, buf_ref, halo, taps, rows, wcols=None):
    for s in range(taps):
        g = _colsum(dy * buf_ref[pl.ds(halo - s, rows), :])
        if wcols is None:
            dw_ref[pl.ds(taps - 1 - s, 1), :] += g
        else:
            dw_ref[pl.ds(taps - 1 - s, 1), wcols] += g


def _fwd_in(x, g, w_in):
    S, D = x.shape
    nb, _, C = w_in.shape
    ts = _tile(S, 512)

    def body(x_ref, g_ref, w_ref, z_ref, h_ref):
        _, xhat = _rms(x_ref[...])
        h = (xhat * g_ref[...]).astype(MXU_DTYPE)
        h_ref[...] = h
        for j in range(nb):
            z_ref[j] = jnp.dot(h, w_ref[j], preferred_element_type=F32)

    return pl.pallas_call(
        body, name="fwd_in", grid=(S // ts,),
        in_specs=[pl.BlockSpec((ts, D), lambda i: (i, 0)), pl.BlockSpec((1, D), lambda i: (0, 0)),
                  pl.BlockSpec((nb, D, C), lambda i: (0, 0, 0))],
        out_specs=[pl.BlockSpec((nb, ts, C), lambda i: (0, i, 0)), pl.BlockSpec((ts, D), lambda i: (i, 0))],
        out_shape=[jax.ShapeDtypeStruct((nb, S, C), F32), jax.ShapeDtypeStruct((S, D), MXU_DTYPE)],
        compiler_params=_params("parallel"))(x, g, w_in)


def _lru_gates(xc, wa_ref, ba_ref, wx_ref, bx_ref, sp):
    xb = xc.astype(MXU_DTYPE)
    r = _sigmoid(jnp.dot(xb, wa_ref[...], preferred_element_type=F32) + ba_ref[...])
    ig = _sigmoid(jnp.dot(xb, wx_ref[...], preferred_element_type=F32) + bx_ref[...])
    log_a = -RG_C * r * sp
    a = jnp.exp(log_a)
    mult = jnp.sqrt(jnp.tanh(-log_a) * (a * a + 1.0))
    return r, ig, a, mult


def _lru_fwd(z, conv_w, conv_b, wa, ba, wx, bx, lam):
    _, S, C = z.shape
    ts = _tile(S, 256)
    taps = conv_w.shape[0]
    halo = 8

    def body(zx_ref, zg_ref, cw_ref, cb_ref, wa_ref, ba_ref, wx_ref, bx_ref, lam_ref,
             h_ref, y_ref, xbuf, a_s, u_s, hc):
        i = pl.program_id(0)

        @pl.when(i == 0)
        def _():
            xbuf[pl.ds(0, halo), :] = jnp.zeros((halo, C), F32)
            hc[...] = jnp.zeros_like(hc)

        xbuf[pl.ds(halo, ts), :] = zx_ref[0]
        xc = _causal_taps(xbuf, halo, cw_ref, taps, ts) + cb_ref[...]
        sp = _softplus_neg(lam_ref[...])
        _, ig, a, mult = _lru_gates(xc, wa_ref, ba_ref, wx_ref, bx_ref, sp)
        a_s[...] = a
        u_s[...] = mult * (ig * xc)
        row = lax.broadcasted_iota(jnp.int32, (8, C), 0)

        def step(k, carry):
            off = pl.multiple_of(k * 8, 8)
            av = a_s[pl.ds(off, 8), :]
            uv = u_s[pl.ds(off, 8), :]
            for d in (1, 2, 4):
                m = row >= d
                a_sh = jnp.where(m, pltpu.roll(av, d, 0), 1.0)
                u_sh = jnp.where(m, pltpu.roll(uv, d, 0), 0.0)
                uv = uv + av * u_sh
                av = av * a_sh
            hv = uv + av * carry
            h_ref[pl.ds(off, 8), :] = hv
            return jnp.broadcast_to(hv[7:8, :], (8, C))

        hc[...] = lax.fori_loop(0, ts // 8, step, hc[...])
        ge, _ = _gelu(zg_ref[0])
        y_ref[...] = (h_ref[...] * ge).astype(MXU_DTYPE)
        xbuf[pl.ds(0, halo), :] = xbuf[pl.ds(ts, halo), :]

    vec = pl.BlockSpec((1, C), lambda i: (0, 0))
    mat = pl.BlockSpec((C, C), lambda i: (0, 0))
    return pl.pallas_call(
        body, name="lru_fwd", grid=(S // ts,),
        in_specs=[pl.BlockSpec((1, ts, C), lambda i: (0, i, 0)), pl.BlockSpec((1, ts, C), lambda i: (1, i, 0)),
                  pl.BlockSpec((taps, C), lambda i: (0, 0)), vec, mat, vec, mat, vec, vec],
        out_specs=[pl.BlockSpec((ts, C), lambda i: (i, 0)), pl.BlockSpec((ts, C), lambda i: (i, 0))],
        out_shape=[jax.ShapeDtypeStruct((S, C), F32), jax.ShapeDtypeStruct((S, C), MXU_DTYPE)],
        scratch_shapes=[pltpu.VMEM((ts + halo, C), F32), pltpu.VMEM((ts, C), F32), pltpu.VMEM((ts, C), F32),
                        pltpu.VMEM((8, C), F32)],
        compiler_params=_params("arbitrary"))(z, z, conv_w, conv_b, wa, ba, wx, bx, lam)


def _layer_norm_stats(c1):
    mu = jnp.mean(c1, axis=-1, keepdims=True)
    xc = c1 - mu
    rstd = lax.rsqrt(jnp.mean(xc * xc, axis=-1, keepdims=True) + EPS)
    return rstd, xc * rstd


def _conf_fwd(z, conv_w, conv_b, ln_g, ln_b):
    _, S, C = z.shape
    ts = _tile(S, 256)
    taps = conv_w.shape[0]
    halo = 32

    def body(za_ref, zb_ref, cw_ref, cb_ref, g_ref, b_ref, c1_ref, c3_ref, cbuf):
        i = pl.program_id(0)

        @pl.when(i == 0)
        def _():
            cbuf[pl.ds(0, halo), :] = jnp.zeros((halo, C), F32)

        cbuf[pl.ds(halo, ts), :] = za_ref[0] * _sigmoid(zb_ref[0])
        c1 = _causal_taps(cbuf, halo, cw_ref, taps, ts) + cb_ref[...]
        c1_ref[...] = c1
        _, xhat = _layer_norm_stats(c1)
        c2 = xhat * g_ref[...] + b_ref[...]
        c3_ref[...] = (c2 * _sigmoid(c2)).astype(MXU_DTYPE)
        cbuf[pl.ds(0, halo), :] = cbuf[pl.ds(ts, halo), :]

    vec = pl.BlockSpec((1, C), lambda i: (0, 0))
    return pl.pallas_call(
        body, name="conf_fwd", grid=(S // ts,),
        in_specs=[pl.BlockSpec((1, ts, C), lambda i: (2, i, 0)), pl.BlockSpec((1, ts, C), lambda i: (3, i, 0)),
                  pl.BlockSpec((taps, C), lambda i: (0, 0)), vec, vec, vec],
        out_specs=[pl.BlockSpec((ts, C), lambda i: (i, 0)), pl.BlockSpec((ts, C), lambda i: (i, 0))],
        out_shape=[jax.ShapeDtypeStruct((S, C), F32), jax.ShapeDtypeStruct((S, C), MXU_DTYPE)],
        scratch_shapes=[pltpu.VMEM((ts + halo, C), F32)],
        compiler_params=_params("arbitrary"))(z, z, conv_w, conv_b, ln_g, ln_b)


def _fwd_out_q(x, y_lru, c3, w_out, g_xa, w_q):
    S, D = x.shape
    C = y_lru.shape[1]
    ts = _tile(S, 512)

    def body(x_ref, yl_ref, c3_ref, wo_ref, g_ref, wq_ref, x1_ref, h2_ref, q_ref):
        x1 = (x_ref[...] + jnp.dot(yl_ref[...], wo_ref[0], preferred_element_type=F32)
              + jnp.dot(c3_ref[...], wo_ref[1], preferred_element_type=F32))
        x1_ref[...] = x1
        _, xhat = _rms(x1)
        h2 = (xhat * g_ref[...]).astype(MXU_DTYPE)
        h2_ref[...] = h2
        q_ref[...] = jnp.dot(h2, wq_ref[...], preferred_element_type=F32).astype(MXU_DTYPE)

    row = lambda w: pl.BlockSpec((ts, w), lambda i: (i, 0))
    return pl.pallas_call(
        body, name="fwd_out_q", grid=(S // ts,),
        in_specs=[row(D), row(C), row(C), pl.BlockSpec((2, C, D), lambda i: (0, 0, 0)),
                  pl.BlockSpec((1, D), lambda i: (0, 0)), pl.BlockSpec((D, D), lambda i: (0, 0))],
        out_specs=[row(D), row(D), row(D)],
        out_shape=[jax.ShapeDtypeStruct((S, D), F32), jax.ShapeDtypeStruct((S, D), MXU_DTYPE),
                   jax.ShapeDtypeStruct((S, D), MXU_DTYPE)],
        compiler_params=_params("parallel"))(x, y_lru, c3, w_out, g_xa, w_q)


def _kv_fwd(mem, g, w_kv):
    M, D = mem.shape
    nb, _, C = w_kv.shape

    def body(mem_ref, g_ref, w_ref, m_ref, kv_ref):
        _, xhat = _rms(mem_ref[...])
        m = (xhat * g_ref[...]).astype(MXU_DTYPE)
        m_ref[...] = m
        for j in range(nb):
            kv_ref[:, pl.ds(j * C, C)] = jnp.dot(m, w_ref[j], preferred_element_type=F32).astype(MXU_DTYPE)

    return pl.pallas_call(
        body, name="kv_fwd", grid=(1,),
        in_specs=[pl.BlockSpec((M, D), lambda i: (0, 0)), pl.BlockSpec((1, D), lambda i: (0, 0)),
                  pl.BlockSpec((nb, D, C), lambda i: (0, 0, 0))],
        out_specs=[pl.BlockSpec((M, D), lambda i: (0, 0)), pl.BlockSpec((M, nb * C), lambda i: (0, 0))],
        out_shape=[jax.ShapeDtypeStruct((M, D), MXU_DTYPE), jax.ShapeDtypeStruct((M, nb * C), MXU_DTYPE)],
        compiler_params=_params("arbitrary"))(mem, g, w_kv)


def _softmax_rows(s):
    e = jnp.exp(s - jnp.max(s, axis=-1, keepdims=True))
    return e / jnp.sum(e, axis=-1, keepdims=True)


def _attn_fwd(q, kv, x1, w_o, g_ffn):
    S, D = x1.shape
    M = kv.shape[0]
    hd = D // XA_HEADS
    scale = hd ** -0.5
    ts = _tile(S, 512)

    def body(q_ref, kv_ref, x1_ref, wo_ref, g_ref, o_ref, x2_ref, h3_ref):
        for h in range(XA_HEADS):
            cols = pl.ds(h * hd, hd)
            p = _softmax_rows(_mm_nt(q_ref[:, cols], kv_ref[:, cols]) * scale)
            o_ref[:, cols] = _mm(p, kv_ref[:, pl.ds(D + h * hd, hd)]).astype(MXU_DTYPE)
        x2 = x1_ref[...] + jnp.dot(o_ref[...], wo_ref[...], preferred_element_type=F32)
        x2_ref[...] = x2
        _, xhat = _rms(x2)
        h3_ref[...] = (xhat * g_ref[...]).astype(MXU_DTYPE)

    row = pl.BlockSpec((ts, D), lambda i: (i, 0))
    return pl.pallas_call(
        body, name="attn_fwd", grid=(S // ts,),
        in_specs=[row, pl.BlockSpec((M, 2 * D), lambda i: (0, 0)), row, pl.BlockSpec((D, D), lambda i: (0, 0)),
                  pl.BlockSpec((1, D), lambda i: (0, 0))],
        out_specs=[row, row, row],
        out_shape=[jax.ShapeDtypeStruct((S, D), MXU_DTYPE), jax.ShapeDtypeStruct((S, D), F32),
                   jax.ShapeDtypeStruct((S, D), MXU_DTYPE)],
        compiler_params=_params("parallel"))(q, kv, x1, w_o, g_ffn)


def _fwd_up(h3, w_up):
    S, D = h3.shape
    nb, _, C = w_up.shape
    ts = _tile(S, 512)

    def body(h_ref, w_ref, o_ref):
        o_ref[0] = jnp.dot(h_ref[...], w_ref[0], preferred_element_type=F32)

    return pl.pallas_call(
        body, name="fwd_up", grid=(nb, S // ts),
        in_specs=[pl.BlockSpec((ts, D), lambda j, i: (i, 0)), pl.BlockSpec((1, D, C), lambda j, i: (j, 0, 0))],
        out_specs=pl.BlockSpec((1, ts, C), lambda j, i: (j, i, 0)),
        out_shape=jax.ShapeDtypeStruct((nb, S, C), F32),
        compiler_params=_params("parallel", "parallel"))(h3, w_up)


def _ffn_down_loss(gu, conv_w, conv_b, w_down, x2, g_final, target):
    nb, S, C = gu.shape
    half = nb // 2
    D = x2.shape[1]
    ts = _tile(S, 256)
    taps = conv_w.shape[0]
    halo = 8
    hb = ts // halo

    def body(g_ref, gh_ref, u_ref, cw_ref, cb_ref, wd_ref, x2_ref, gf_ref, t_ref,
             act_ref, dx3_ref, loss_ref, dgf_ref, gbuf):
        i = pl.program_id(0)

        @pl.when(i == 0)
        def _():
            loss_ref[...] = jnp.zeros_like(loss_ref)
            dgf_ref[...] = jnp.zeros_like(dgf_ref)

        x3 = x2_ref[...]
        for j in range(half):
            cols = pl.ds(j * C, C)
            gbuf[pl.ds(0, halo), :] = jnp.where(i > 0, gh_ref[j], 0.0)
            gbuf[pl.ds(halo, ts), :] = g_ref[j]
            gc = _causal_taps(gbuf, halo, cw_ref, taps, ts, wcols=cols) + cb_ref[:, cols]
            ge, _ = _gelu(gc)
            act = (ge * u_ref[j]).astype(MXU_DTYPE)
            act_ref[j] = act
            x3 = x3 + jnp.dot(act, wd_ref[j], preferred_element_type=F32)
        rinv, xhat = _rms(x3)
        gf = gf_ref[...]
        diff = xhat * gf - t_ref[...]
        loss_ref[...] += _colsum(diff * diff) * (0.5 / D)
        dy = diff * (1.0 / D)
        dgf_ref[...] += _colsum(dy * xhat)
        dx3_ref[...] = _rms_bwd(rinv, xhat, dy * gf)

    row = pl.BlockSpec((ts, D), lambda i: (i, 0))
    vecd = pl.BlockSpec((1, D), lambda i: (0, 0))
    return pl.pallas_call(
        body, name="ffn_down_loss", grid=(S // ts,),
        in_specs=[pl.BlockSpec((half, ts, C), lambda i: (0, i, 0)),
                  pl.BlockSpec((half, halo, C), lambda i: (0, jnp.maximum(i * hb - 1, 0), 0)),
                  pl.BlockSpec((half, ts, C), lambda i: (1, i, 0)),
                  pl.BlockSpec((taps, half * C), lambda i: (0, 0)), pl.BlockSpec((1, half * C), lambda i: (0, 0)),
                  pl.BlockSpec((half, C, D), lambda i: (0, 0, 0)), row, vecd, row],
        out_specs=[pl.BlockSpec((half, ts, C), lambda i: (0, i, 0)), row, vecd, vecd],
        out_shape=[jax.ShapeDtypeStruct((half, S, C), MXU_DTYPE), jax.ShapeDtypeStruct((S, D), F32),
                   jax.ShapeDtypeStruct((1, D), F32), jax.ShapeDtypeStruct((1, D), F32)],
        scratch_shapes=[pltpu.VMEM((ts + halo, C), F32)],
        compiler_params=_params("arbitrary"))(gu, gu, gu, conv_w, conv_b, w_down, x2, g_final, target)


def _bwd_down(dx3, w_down, gu, conv_w, conv_b):
    nb, S, C = gu.shape
    half = nb // 2
    D = dx3.shape[1]
    ts = _tile(S, 256)
    n = S // ts
    taps = conv_w.shape[0]
    halo = 8
    hb = ts // halo

    def body(dx_ref, wd_ref, g_ref, gh_ref, u_ref, cw_ref, cb_ref, dgu_ref, dcw_ref, dcb_ref, gbuf, dbuf):
        i = pl.program_id(0)
        r = n - 1 - i

        @pl.when(i == 0)
        def _():
            dcw_ref[...] = jnp.zeros_like(dcw_ref)
            dcb_ref[...] = jnp.zeros_like(dcb_ref)
            dbuf[...] = jnp.zeros_like(dbuf)

        dxb = dx_ref[...].astype(MXU_DTYPE)
        for j in range(half):
            cols = pl.ds(j * C, C)
            dact = _mm_nt(dxb, wd_ref[j])
            gbuf[pl.ds(0, halo), :] = jnp.where(r > 0, gh_ref[j], 0.0)
            gbuf[pl.ds(halo, ts), :] = g_ref[j]
            gc = _causal_taps(gbuf, halo, cw_ref, taps, ts, wcols=cols) + cb_ref[:, cols]
            ge, dge = _gelu(gc)
            dgu_ref[half + j] = (dact * ge).astype(MXU_DTYPE)
            dgc = dact * u_ref[j] * dge
            dcb_ref[:, cols] += _colsum(dgc)
            dbuf[j, pl.ds(0, ts), :] = dgc
            _tap_grads(dcw_ref, dgc, gbuf, halo, taps, ts, wcols=cols)
            dgu_ref[j] = _anticausal_taps(dbuf.at[j], cw_ref, taps, ts, wcols=cols).astype(MXU_DTYPE)
            dbuf[j, pl.ds(ts, halo), :] = dbuf[j, pl.ds(0, halo), :]

    row = pl.BlockSpec((ts, D), lambda i: (n - 1 - i, 0))
    wide = pl.BlockSpec((taps, half * C), lambda i: (0, 0))
    wide1 = pl.BlockSpec((1, half * C), lambda i: (0, 0))
    return pl.pallas_call(
        body, name="bwd_down", grid=(n,),
        in_specs=[row, pl.BlockSpec((half, C, D), lambda i: (0, 0, 0)),
                  pl.BlockSpec((half, ts, C), lambda i: (0, n - 1 - i, 0)),
                  pl.BlockSpec((half, halo, C), lambda i: (0, jnp.maximum((n - 1 - i) * hb - 1, 0), 0)),
                  pl.BlockSpec((half, ts, C), lambda i: (1, n - 1 - i, 0)), wide, wide1],
        out_specs=[pl.BlockSpec((nb, ts, C), lambda i: (0, n - 1 - i, 0)), wide, wide1],
        out_shape=[jax.ShapeDtypeStruct((nb, S, C), MXU_DTYPE), jax.ShapeDtypeStruct((taps, half * C), F32),
                   jax.ShapeDtypeStruct((1, half * C), F32)],
        scratch_shapes=[pltpu.VMEM((ts + halo, C), F32), pltpu.VMEM((half, ts + halo, C), F32)],
        compiler_params=_params("arbitrary"))(dx3, w_down, gu, gu, gu, conv_w, conv_b)


def _bwd_up(dgu, w_up, x2, g_ffn, dx3):
    nb, S, C = dgu.shape
    D = x2.shape[1]
    ts = _tile(S, 512)

    def body(d_ref, w_ref, x2_ref, g_ref, dx3_ref, dx2_ref, dg_ref, acc):
        i, j = pl.program_id(0), pl.program_id(1)

        @pl.when((i == 0) & (j == 0))
        def _():
            dg_ref[...] = jnp.zeros_like(dg_ref)

        part = _mm_nt(d_ref[0], w_ref[0])

        @pl.when(j == 0)
        def _():
            acc[...] = part

        @pl.when(j > 0)
        def _():
            acc[...] += part

        @pl.when(j == nb - 1)
        def _():
            rinv, xhat = _rms(x2_ref[...])
            dh = acc[...]
            dg_ref[...] += _colsum(dh * xhat)
            dx2_ref[...] = dx3_ref[...] + _rms_bwd(rinv, xhat, dh * g_ref[...])

    row = pl.BlockSpec((ts, D), lambda i, j: (i, 0))
    vecd = pl.BlockSpec((1, D), lambda i, j: (0, 0))
    return pl.pallas_call(
        body, name="bwd_up", grid=(S // ts, nb),
        in_specs=[pl.BlockSpec((1, ts, C), lambda i, j: (j, i, 0)), pl.BlockSpec((1, D, C), lambda i, j: (j, 0, 0)),
                  row, vecd, row],
        out_specs=[row, vecd],
        out_shape=[jax.ShapeDtypeStruct((S, D), F32), jax.ShapeDtypeStruct((1, D), F32)],
        scratch_shapes=[pltpu.VMEM((ts, D), F32)],
        compiler_params=_params("arbitrary", "arbitrary"))(dgu, w_up, x2, g_ffn, dx3)


def _attn_bwd(dx2, w_o, q, kv, x1, g_xa, w_q):
    S, D = x1.shape
    M = kv.shape[0]
    hd = D // XA_HEADS
    scale = hd ** -0.5
    ts = _tile(S, 512)

    def body(dx2_ref, wo_ref, q_ref, kv_ref, x1_ref, g_ref, wq_ref, dq_ref, dx1_ref, dkv_ref, dg_ref):
        i = pl.program_id(0)

        @pl.when(i == 0)
        def _():
            dkv_ref[...] = jnp.zeros_like(dkv_ref)
            dg_ref[...] = jnp.zeros_like(dg_ref)

        dx2 = dx2_ref[...]
        do = _mm_nt(dx2, wo_ref[...]).astype(MXU_DTYPE)
        for h in range(XA_HEADS):
            cols = pl.ds(h * hd, hd)
            vcols = pl.ds(D + h * hd, hd)
            qh, kh, doh = q_ref[:, cols], kv_ref[:, cols], do[:, h * hd:(h + 1) * hd]
            p = _softmax_rows(_mm_nt(qh, kh) * scale)
            dp = _mm_nt(doh, kv_ref[:, vcols])
            dkv_ref[:, vcols] += _mm_tn(p, doh)
            ds = (p * (dp - jnp.sum(dp * p, axis=-1, keepdims=True)) * scale).astype(MXU_DTYPE)
            dq_ref[:, cols] = _mm(ds, kh).astype(MXU_DTYPE)
            dkv_ref[:, cols] += _mm_tn(ds, qh)
        dh2 = _mm_nt(dq_ref[...], wq_ref[...])
        rinv, xhat = _rms(x1_ref[...])
        dg_ref[...] += _colsum(dh2 * xhat)
        dx1_ref[...] = dx2 + _rms_bwd(rinv, xhat, dh2 * g_ref[...])

    row = pl.BlockSpec((ts, D), lambda i: (i, 0))
    mat = pl.BlockSpec((D, D), lambda i: (0, 0))
    vecd = pl.BlockSpec((1, D), lambda i: (0, 0))
    kvs = pl.BlockSpec((M, 2 * D), lambda i: (0, 0))
    return pl.pallas_call(
        body, name="attn_bwd", grid=(S // ts,),
        in_specs=[row, mat, row, kvs, row, vecd, mat],
        out_specs=[row, row, kvs, vecd],
        out_shape=[jax.ShapeDtypeStruct((S, D), MXU_DTYPE), jax.ShapeDtypeStruct((S, D), F32),
                   jax.ShapeDtypeStruct((M, 2 * D), F32), jax.ShapeDtypeStruct((1, D), F32)],
        compiler_params=_params("arbitrary"))(dx2, w_o, q, kv, x1, g_xa, w_q)


def _kv_bwd(dkv, w_kv, mem, g, m):
    M, D = mem.shape
    nb, _, C = w_kv.shape

    def body(dkv_ref, w_ref, mem_ref, m_ref, dw_ref, dg_ref):
        dm = jnp.zeros((M, D), F32)
        for j in range(nb):
            dj = dkv_ref[:, pl.ds(j * C, C)].astype(MXU_DTYPE)
            dw_ref[j] = _mm_tn(m_ref[...], dj).astype(dw_ref.dtype)
            dm = dm + _mm_nt(dj, w_ref[j])
        _, xhat = _rms(mem_ref[...])
        dg_ref[...] = _colsum(dm * xhat)

    full = lambda *s: pl.BlockSpec(s, lambda i: (0,) * len(s))
    return pl.pallas_call(
        body, name="kv_bwd", grid=(1,),
        in_specs=[full(M, nb * C), full(nb, D, C), full(M, D), full(M, D)],
        out_specs=[full(nb, D, C), full(1, D)],
        out_shape=[jax.ShapeDtypeStruct((nb, D, C), WIRE_DTYPE), jax.ShapeDtypeStruct((1, D), F32)],
        compiler_params=_params("arbitrary"))(dkv, w_kv, mem, m)


def _conf_bwd(dx1, w_out_c, z, c1, conv_w, ln_g, ln_b):
    _, S, C = z.shape
    D = dx1.shape[1]
    ts = _tile(S, 256)
    n = S // ts
    taps = conv_w.shape[0]
    halo = 32
    hb = ts // halo

    def body(dx_ref, wo_ref, za_ref, zb_ref, zah_ref, zbh_ref, c1_ref, cw_ref, g_ref, b_ref,
             dz_ref, dcw_ref, dcb_ref, dlg_ref, dlb_ref, c0buf, dbuf):
        i = pl.program_id(0)
        r = n - 1 - i

        @pl.when(i == 0)
        def _():
            for ref in (dcw_ref, dcb_ref, dlg_ref, dlb_ref):
                ref[...] = jnp.zeros_like(ref)
            dbuf[pl.ds(ts, halo), :] = jnp.zeros((halo, C), F32)

        za = za_ref[0]
        sb = _sigmoid(zb_ref[0])
        c0buf[pl.ds(0, halo), :] = jnp.where(r > 0, zah_ref[0] * _sigmoid(zbh_ref[0]), 0.0)
        c0buf[pl.ds(halo, ts), :] = za * sb
        dc3 = _mm_nt(dx_ref[...], wo_ref[...])
        rstd, xhat = _layer_norm_stats(c1_ref[...])
        g = g_ref[...]
        c2 = xhat * g + b_ref[...]
        sg = _sigmoid(c2)
        dc2 = dc3 * sg * (1.0 + c2 * (1.0 - sg))
        dlg_ref[...] += _colsum(dc2 * xhat)
        dlb_ref[...] += _colsum(dc2)
        dxh = dc2 * g
        dc1 = rstd * (dxh - jnp.mean(dxh, axis=-1, keepdims=True)
                      - xhat * jnp.mean(dxh * xhat, axis=-1, keepdims=True))
        dcb_ref[...] += _colsum(dc1)
        dbuf[pl.ds(0, ts), :] = dc1
        _tap_grads(dcw_ref, dc1, c0buf, halo, taps, ts)
        dc0 = _anticausal_taps(dbuf, cw_ref, taps, ts)
        dz_ref[0] = (dc0 * sb).astype(MXU_DTYPE)
        dz_ref[1] = (dc0 * za * sb * (1.0 - sb)).astype(MXU_DTYPE)
        dbuf[pl.ds(ts, halo), :] = dbuf[pl.ds(0, halo), :]

    vec = pl.BlockSpec((1, C), lambda i: (0, 0))
    tapw = pl.BlockSpec((taps, C), lambda i: (0, 0))
    tile = lambda b: pl.BlockSpec((1, ts, C), lambda i: (b, n - 1 - i, 0))
    prev = lambda b: pl.BlockSpec((1, halo, C), lambda i: (b, jnp.maximum((n - 1 - i) * hb - 1, 0), 0))
    return pl.pallas_call(
        body, name="conf_bwd", grid=(n,),
        in_specs=[pl.BlockSpec((ts, D), lambda i: (n - 1 - i, 0)), pl.BlockSpec((C, D), lambda i: (0, 0)),
                  tile(2), tile(3), prev(2), prev(3), pl.BlockSpec((ts, C), lambda i: (n - 1 - i, 0)),
                  tapw, vec, vec],
        out_specs=[pl.BlockSpec((2, ts, C), lambda i: (0, n - 1 - i, 0)), tapw, vec, vec, vec],
        out_shape=[jax.ShapeDtypeStruct((2, S, C), MXU_DTYPE), jax.ShapeDtypeStruct((taps, C), F32),
                   jax.ShapeDtypeStruct((1, C), F32), jax.ShapeDtypeStruct((1, C), F32),
                   jax.ShapeDtypeStruct((1, C), F32)],
        scratch_shapes=[pltpu.VMEM((ts + halo, C), F32), pltpu.VMEM((ts + halo, C), F32)],
        compiler_params=_params("arbitrary"))(dx1, w_out_c, z, z, z, z, c1, conv_w, ln_g, ln_b)


def _lru_bwd(dx1, w_out_l, z, h, conv_w, conv_b, wa, ba, wx, bx, lam):
    _, S, C = z.shape
    D = dx1.shape[1]
    ts = _tile(S, 256)
    n = S // ts
    taps = conv_w.shape[0]
    halo = 8
    hb = ts // halo

    def body(dx_ref, wo_ref, zx_ref, zxh_ref, zg_ref, h_ref, hh_ref, cw_ref, cb_ref, wa_ref, ba_ref,
             wx_ref, bx_ref, lam_ref,
             dz_ref, dwa_ref, dwx_ref, dba_ref, dbx_ref, dlam_ref, dcw_ref, dcb_ref,
             xbuf, hbuf, a_s, w_s, dh_s, g_s, dbuf, pc):
        i = pl.program_id(0)
        r = n - 1 - i

        @pl.when(i == 0)
        def _():
            for ref in (dwa_ref, dwx_ref, dba_ref, dbx_ref, dlam_ref, dcw_ref, dcb_ref, pc):
                ref[...] = jnp.zeros_like(ref)
            dbuf[pl.ds(ts, halo), :] = jnp.zeros((halo, C), F32)

        xbuf[pl.ds(0, halo), :] = jnp.where(r > 0, zxh_ref[0], 0.0)
        xbuf[pl.ds(halo, ts), :] = zx_ref[0]
        hbuf[pl.ds(0, halo), :] = jnp.where(r > 0, hh_ref[...], 0.0)
        hbuf[pl.ds(halo, ts), :] = h_ref[...]
        xc = _causal_taps(xbuf, halo, cw_ref, taps, ts) + cb_ref[...]
        lam_v = lam_ref[...]
        sp = _softplus_neg(lam_v)
        rg, ig, a, mult = _lru_gates(xc, wa_ref, ba_ref, wx_ref, bx_ref, sp)

        dy = _mm_nt(dx_ref[...], wo_ref[...])
        ge, dge = _gelu(zg_ref[0])
        dh = dy * ge
        dz_ref[1] = (dy * h_ref[...] * dge).astype(MXU_DTYPE)
        a_s[...] = a
        w_s[...] = a * dh
        dh_s[...] = dh
        row = lax.broadcasted_iota(jnp.int32, (8, C), 0)

        def step(kk, carry):
            off = pl.multiple_of((ts // 8 - 1 - kk) * 8, 8)
            av = a_s[pl.ds(off, 8), :]
            wv = w_s[pl.ds(off, 8), :]
            for d in (1, 2, 4):
                m = row < 8 - d
                a_sh = jnp.where(m, pltpu.roll(av, 8 - d, 0), 1.0)
                w_sh = jnp.where(m, pltpu.roll(wv, 8 - d, 0), 0.0)
                wv = wv + av * w_sh
                av = av * a_sh
            pv = wv + av * carry
            g_s[pl.ds(off, 8), :] = dh_s[pl.ds(off, 8), :] + jnp.where(row < 7, pltpu.roll(pv, 7, 0), carry)
            return jnp.broadcast_to(pv[0:1, :], (8, C))

        pc[...] = lax.fori_loop(0, ts // 8, step, pc[...])
        gt = g_s[...]
        da = gt * hbuf[pl.ds(halo - 1, ts), :]
        gm = gt * mult
        dlog_a = da * a - (gt * ig * xc) * (a * a) / mult
        dlam_ref[...] += _colsum(dlog_a * rg) * (RG_C * _sigmoid(-lam_v))
        dpa = (dlog_a * (-RG_C * sp)) * rg * (1.0 - rg)
        dpx = (gm * xc) * ig * (1.0 - ig)
        dba_ref[...] += _colsum(dpa)
        dbx_ref[...] += _colsum(dpx)
        xb = xc.astype(MXU_DTYPE)
        dpab, dpxb = dpa.astype(MXU_DTYPE), dpx.astype(MXU_DTYPE)
        dwa_ref[...] += _mm_tn(xb, dpab)
        dwx_ref[...] += _mm_tn(xb, dpxb)
        dxc = gm * ig + _mm_nt(dpab, wa_ref[...]) + _mm_nt(dpxb, wx_ref[...])
        dcb_ref[...] += _colsum(dxc)
        dbuf[pl.ds(0, ts), :] = dxc
        _tap_grads(dcw_ref, dxc, xbuf, halo, taps, ts)
        dz_ref[0] = _anticausal_taps(dbuf, cw_ref, taps, ts).astype(MXU_DTYPE)
        dbuf[pl.ds(ts, halo), :] = dbuf[pl.ds(0, halo), :]

    vec = pl.BlockSpec((1, C), lambda i: (0, 0))
    mat = pl.BlockSpec((C, C), lambda i: (0, 0))
    tapw = pl.BlockSpec((taps, C), lambda i: (0, 0))
    prev_rows = lambda i: jnp.maximum((n - 1 - i) * hb - 1, 0)
    sds = jax.ShapeDtypeStruct
    return pl.pallas_call(
        body, name="lru_bwd", grid=(n,),
        in_specs=[pl.BlockSpec((ts, D), lambda i: (n - 1 - i, 0)), pl.BlockSpec((C, D), lambda i: (0, 0)),
                  pl.BlockSpec((1, ts, C), lambda i: (0, n - 1 - i, 0)),
                  pl.BlockSpec((1, halo, C), lambda i: (0, prev_rows(i), 0)),
                  pl.BlockSpec((1, ts, C), lambda i: (1, n - 1 - i, 0)),
                  pl.BlockSpec((ts, C), lambda i: (n - 1 - i, 0)),
                  pl.BlockSpec((halo, C), lambda i: (prev_rows(i), 0)),
                  tapw, vec, mat, vec, mat, vec, vec],
        out_specs=[pl.BlockSpec((2, ts, C), lambda i: (0, n - 1 - i, 0)), mat, mat, vec, vec, vec, tapw, vec],
        out_shape=[sds((2, S, C), MXU_DTYPE), sds((C, C), F32), sds((C, C), F32), sds((1, C), F32),
                   sds((1, C), F32), sds((1, C), F32), sds((taps, C), F32), sds((1, C), F32)],
        scratch_shapes=[pltpu.VMEM((ts + halo, C), F32), pltpu.VMEM((ts + halo, C), F32)]
        + [pltpu.VMEM((ts, C), F32)] * 4 + [pltpu.VMEM((ts + halo, C), F32), pltpu.VMEM((8, C), F32)],
        compiler_params=_params("arbitrary"))(dx1, w_out_l, z, z, z, h, h, conv_w, conv_b, wa, ba, wx, bx, lam)


def _bwd_in(dz_l, dz_c, w_in, x, g, dx1):
    S, D = x.shape
    nb, _, C = w_in.shape
    ts = _tile(S, 512)

    def body(dl_ref, dc_ref, w_ref, x_ref, g_ref, dx1_ref, dx_ref, dg_ref):
        i = pl.program_id(0)

        @pl.when(i == 0)
        def _():
            dg_ref[...] = jnp.zeros_like(dg_ref)

        dh = (_mm_nt(dl_ref[0], w_ref[0]) + _mm_nt(dl_ref[1], w_ref[1])
              + _mm_nt(dc_ref[0], w_ref[2]) + _mm_nt(dc_ref[1], w_ref[3]))
        rinv, xhat = _rms(x_ref[...])
        dg_ref[...] += _colsum(dh * xhat)
        dx_ref[...] = dx1_ref[...] + _rms_bwd(rinv, xhat, dh * g_ref[...])

    row = pl.BlockSpec((ts, D), lambda i: (i, 0))
    pair = pl.BlockSpec((2, ts, C), lambda i: (0, i, 0))
    vecd = pl.BlockSpec((1, D), lambda i: (0, 0))
    return pl.pallas_call(
        body, name="bwd_in", grid=(S // ts,),
        in_specs=[pair, pair, pl.BlockSpec((nb, D, C), lambda i: (0, 0, 0)), row, vecd, row],
        out_specs=[row, vecd],
        out_shape=[jax.ShapeDtypeStruct((S, D), F32), jax.ShapeDtypeStruct((1, D), F32)],
        compiler_params=_params("arbitrary"))(dz_l, dz_c, w_in, x, g, dx1)


def _wgrad(a, b, name):
    na, S, K = a.shape
    nb, _, N = b.shape
    nj = max(na, nb)
    assert min(na, nb) == 1
    ts = _tile(S, 512)
    ns = S // ts

    def body(a_ref, b_ref, o_ref, acc):
        s = pl.program_id(1)
        part = _mm_tn(a_ref[0], b_ref[0])

        @pl.when(s == 0)
        def _():
            acc[...] = part

        @pl.when(s > 0)
        def _():
            acc[...] += part

        @pl.when(s == ns - 1)
        def _():
            o_ref[0] = acc[...].astype(o_ref.dtype)

    return pl.pallas_call(
        body, name=name, grid=(nj, ns),
        in_specs=[pl.BlockSpec((1, ts, K), (lambda j, s: (j, s, 0)) if na > 1 else (lambda j, s: (0, s, 0))),
                  pl.BlockSpec((1, ts, N), (lambda j, s: (j, s, 0)) if nb > 1 else (lambda j, s: (0, s, 0)))],
        out_specs=pl.BlockSpec((1, K, N), lambda j, s: (j, 0, 0)),
        out_shape=jax.ShapeDtypeStruct((nj, K, N), WIRE_DTYPE),
        scratch_shapes=[pltpu.VMEM((K, N), F32)],
        compiler_params=_params("parallel", "arbitrary"))(a, b)


def _place():
    x, y, c = lax.axis_index("x"), lax.axis_index("y"), lax.axis_index("c")
    other_chips = [(1 - x, y), (x, 1 - y), (1 - x, 1 - y)]
    return x, y, c, other_chips


ANY = pl.BlockSpec(memory_space=pl.ANY)


def _gather_weights(shards):
    nt = len(shards)

    def body(*refs):
        src, dst = refs[:nt], refs[nt:2 * nt]
        ici_send, ici_recv, d2d_send, d2d_recv, loc = refs[2 * nt:]
        x, y, c, chips = _place()
        mine = 2 * x + y
        local = [pltpu.make_async_copy(src[t], dst[t].at[mine], loc.at[t]) for t in range(nt)]
        for cp in local:
            cp.start()

        def half(t, pc):
            hr = src[t].shape[0] // 2
            return pl.ds(pc * hr, hr)

        def ici(t, k, block, to):
            cx, cy = block
            ref = dst[t].at[2 * cx + cy, half(t, c)]
            return pltpu.make_async_remote_copy(
                src_ref=src[t].at[half(t, c)] if to is not None else ref, dst_ref=ref,
                send_sem=ici_send.at[t, k], recv_sem=ici_recv.at[t, k],
                device_id=(*to, c) if to is not None else (x, y, c), device_id_type=MESH)

        def d2d(t, k, block, pc):
            cx, cy = block
            ref = dst[t].at[2 * cx + cy, half(t, pc)]
            return pltpu.make_async_remote_copy(
                src_ref=ref, dst_ref=ref, send_sem=d2d_send.at[t, k], recv_sem=d2d_recv.at[t, k],
                device_id=(x, y, 1 - c), device_id_type=MESH)

        sends = [ici(t, k, (x, y), chip) for t in range(nt) for k, chip in enumerate(chips)]
        for cp in sends:
            cp.start()
        passed = []
        for t in range(nt):
            for k, chip in enumerate(chips):
                ici(t, k, chip, None).wait_recv()
                fw = d2d(t, k, chip, c)
                fw.start()
                passed.append(fw)
        for t in range(nt):
            for k, chip in enumerate(chips):
                d2d(t, k, chip, 1 - c).wait_recv()
        for cp in sends + passed:
            cp.wait_send()
        for cp in local:
            cp.wait()

    return pl.pallas_call(
        body, name="gather_weights",
        in_specs=[ANY] * nt, out_specs=[ANY] * nt,
        out_shape=[jax.ShapeDtypeStruct((N_CHIPS,) + s.shape, s.dtype) for s in shards],
        scratch_shapes=[pltpu.SemaphoreType.DMA((nt, 3))] * 4 + [pltpu.SemaphoreType.DMA((nt,))],
        compiler_params=pltpu.CompilerParams(has_side_effects=True))(*shards)


def _exchange_halves(grads):
    nt = len(grads)

    def body(*refs):
        src, dst = refs[:nt], refs[nt:2 * nt]
        send, recv = refs[2 * nt:]
        x, y, c, _ = _place()
        copies = []
        for t in range(nt):
            hr = src[t].shape[1] // 2
            copies.append(pltpu.make_async_remote_copy(
                src_ref=src[t].at[:, pl.ds((1 - c) * hr, hr)], dst_ref=dst[t],
                send_sem=send.at[t], recv_sem=recv.at[t], device_id=(x, y, 1 - c), device_id_type=MESH))
        for cp in copies:
            cp.start()
        for cp in copies:
            cp.wait()

    return pl.pallas_call(
        body, name="rs_exchange_halves",
        in_specs=[ANY] * nt, out_specs=[ANY] * nt,
        out_shape=[jax.ShapeDtypeStruct((g.shape[0], g.shape[1] // 2, g.shape[2]), g.dtype) for g in grads],
        scratch_shapes=[pltpu.SemaphoreType.DMA((nt,))] * 2,
        compiler_params=pltpu.CompilerParams(has_side_effects=True))(*grads)


def _add_halves(grad, other, name):
    nb, R, C = grad.shape
    hr = R // 2
    tr = _tile(hr, 256, 16)
    steps = hr // tr
    c = lax.axis_index("c").astype(jnp.int32).reshape((1,))

    def body(c_ref, a_ref, b_ref, o_ref):
        o_ref[...] = (a_ref[...].astype(F32) + b_ref[...].astype(F32)).astype(o_ref.dtype)

    return pl.pallas_call(
        body, name=name,
        grid_spec=pltpu.PrefetchScalarGridSpec(
            num_scalar_prefetch=1, grid=(nb, steps),
            in_specs=[pl.BlockSpec((1, tr, C), lambda j, i, c_ref: (j, c_ref[0] * steps + i, 0)),
                      pl.BlockSpec((1, tr, C), lambda j, i, c_ref: (j, i, 0))],
            out_specs=pl.BlockSpec((1, tr, C), lambda j, i, c_ref: (j, i, 0))),
        out_shape=jax.ShapeDtypeStruct((nb, hr, C), grad.dtype),
        compiler_params=_params("parallel", "parallel"))(c, grad, other)


def _scatter_chip_sums(parts):
    nt = len(parts)

    def body(*refs):
        src, dst = refs[:nt], refs[nt:2 * nt]
        send, recv = refs[2 * nt:]
        x, y, c, chips = _place()
        copies = []
        for t in range(nt):
            for k, (cx, cy) in enumerate(chips):
                copies.append(pltpu.make_async_remote_copy(
                    src_ref=src[t].at[2 * cx + cy], dst_ref=dst[t].at[k],
                    send_sem=send.at[t, k], recv_sem=recv.at[t, k], device_id=(cx, cy, c), device_id_type=MESH))
        for cp in copies:
            cp.start()
        for cp in copies:
            cp.wait()

    return pl.pallas_call(
        body, name="rs_scatter_chip_sums",
        in_specs=[ANY] * nt, out_specs=[ANY] * nt,
        out_shape=[jax.ShapeDtypeStruct((3,) + p.shape[1:], p.dtype) for p in parts],
        scratch_shapes=[pltpu.SemaphoreType.DMA((nt, 3))] * 2,
        compiler_params=pltpu.CompilerParams(has_side_effects=True))(*parts)


def _sum_chips(part, recv, name):
    _, hr, C = part.shape
    tr = _tile(hr, 256, 16)
    mine = (2 * lax.axis_index("x") + lax.axis_index("y")).astype(jnp.int32).reshape((1,))

    def body(m_ref, a_ref, b_ref, o_ref):
        acc = a_ref[0].astype(F32)
        for k in range(3):
            acc = acc + b_ref[k].astype(F32)
        o_ref[...] = acc

    return pl.pallas_call(
        body, name=name,
        grid_spec=pltpu.PrefetchScalarGridSpec(
            num_scalar_prefetch=1, grid=(hr // tr,),
            in_specs=[pl.BlockSpec((1, tr, C), lambda i, m_ref: (m_ref[0], i, 0)),
                      pl.BlockSpec((3, tr, C), lambda i, m_ref: (0, i, 0))],
            out_specs=pl.BlockSpec((tr, C), lambda i, m_ref: (i, 0))),
        out_shape=jax.ShapeDtypeStruct((hr, C), F32),
        compiler_params=_params("parallel"))(mine, part, recv)


def _join_halves(halves):
    nt = len(halves)

    def body(*refs):
        src, dst = refs[:nt], refs[nt:2 * nt]
        send, recv, loc = refs[2 * nt:]
        x, y, c, _ = _place()
        local, remote = [], []
        for t in range(nt):
            hr = src[t].shape[0]
            rows = dst[t].at[pl.ds(c * hr, hr)]
            local.append(pltpu.make_async_copy(src[t], rows, loc.at[t]))
            remote.append(pltpu.make_async_remote_copy(
                src_ref=src[t], dst_ref=rows, send_sem=send.at[t], recv_sem=recv.at[t],
                device_id=(x, y, 1 - c), device_id_type=MESH))
        for cp in local + remote:
            cp.start()
        for t in range(nt):
            hr = src[t].shape[0]
            arrived = dst[t].at[pl.ds((1 - c) * hr, hr)]
            pltpu.make_async_remote_copy(src_ref=src[t], dst_ref=arrived, send_sem=send.at[t], recv_sem=recv.at[t],
                                         device_id=(x, y, 1 - c), device_id_type=MESH).wait_recv()
        for cp in remote:
            cp.wait_send()
        for cp in local:
            cp.wait()

    return pl.pallas_call(
        body, name="rs_join_halves",
        in_specs=[ANY] * nt, out_specs=[ANY] * nt,
        out_shape=[jax.ShapeDtypeStruct((2 * h.shape[0], h.shape[1]), h.dtype) for h in halves],
        scratch_shapes=[pltpu.SemaphoreType.DMA((nt,))] * 3,
        compiler_params=pltpu.CompilerParams(has_side_effects=True))(*halves)


def _all_reduce_rows(buf, name, loss_row=None):
    R, L = buf.shape

    def body(in_ref, *rest):
        if loss_row is None:
            out_ref, gath, send, recv = rest
        else:
            out_ref, loss_ref, gath, send, recv = rest
        x, y, c, _ = _place()
        me = 4 * x + 2 * y + c
        gath[0] = in_ref[...]
        copies = []
        for k in range(1, N_DEV):
            peer = (x ^ ((k >> 2) & 1), y ^ ((k >> 1) & 1), c ^ (k & 1))
            copies.append(pltpu.make_async_remote_copy(
                src_ref=in_ref, dst_ref=gath.at[k], send_sem=send.at[k - 1], recv_sem=recv.at[k - 1],
                device_id=peer, device_id_type=MESH))
        for cp in copies:
            cp.start()
        for cp in copies:
            cp.wait()
        total = gath[me]
        for d in range(1, N_DEV):
            total = total + gath[d ^ me]
        out_ref[...] = total
        if loss_row is not None:
            loss_ref[...] = jnp.sum(total[loss_row:loss_row + 1, :], axis=1, keepdims=True)

    vm = pl.BlockSpec(memory_space=pltpu.VMEM)
    out_shape = [jax.ShapeDtypeStruct((R, L), F32)]
    if loss_row is not None:
        out_shape.append(jax.ShapeDtypeStruct((1, 1), F32))
    return pl.pallas_call(
        body, name=name, in_specs=[vm], out_specs=[vm] * len(out_shape), out_shape=out_shape,
        scratch_shapes=[pltpu.VMEM((N_DEV, R, L), F32), pltpu.SemaphoreType.DMA((N_DEV - 1,)),
                        pltpu.SemaphoreType.DMA((N_DEV - 1,))],
        compiler_params=pltpu.CompilerParams(has_side_effects=True, vmem_limit_bytes=VMEM_LIMIT_BYTES))(buf)


def _adamw(w, g, m, v, name):
    R, C = w.shape
    tr = _tile(R, 256)
    c1 = 1.0 - ADAM_B1 ** ADAM_STEP
    c2 = 1.0 - ADAM_B2 ** ADAM_STEP

    def body(w_ref, g_ref, m_ref, v_ref, d_ref, nm_ref, nv_ref):
        gv = g_ref[...]
        nm = ADAM_B1 * m_ref[...] + (1.0 - ADAM_B1) * gv
        nv = ADAM_B2 * v_ref[...] + (1.0 - ADAM_B2) * (gv * gv)
        nm_ref[...] = nm
        nv_ref[...] = nv
        d_ref[...] = -ADAM_LR * ((nm / c1) / (jnp.sqrt(nv / c2) + ADAM_EPS) + ADAM_WD * w_ref[...])

    blk = pl.BlockSpec((tr, C), lambda i: (i, 0))
    return pl.pallas_call(
        body, name=name, grid=(R // tr,), in_specs=[blk] * 4, out_specs=[blk] * 3,
        out_shape=[jax.ShapeDtypeStruct((R, C), F32)] * 3,
        compiler_params=_params("parallel"))(w, g, m, v)


def _pack_rows(arrays):
    rows = []
    for a in arrays:
        flat = a.reshape(-1).astype(F32)
        pad = (-flat.shape[0]) % LANES
        rows.append(jnp.pad(flat, (0, pad)).reshape(-1, LANES))
    buf = jnp.concatenate(rows, axis=0)
    return jnp.pad(buf, ((0, (-buf.shape[0]) % 8), (0, 0)))


def _unpack_rows(buf, shapes):
    out, r = [], 0
    for s in shapes:
        n = math.prod(s)
        nr = -(-n // LANES)
        out.append(buf[r:r + nr].reshape(-1)[:n].reshape(s))
        r += nr
    return out


def _block_diag(w):
    H, a, b = w.shape
    eye = jnp.eye(H, dtype=w.dtype)
    return (eye[:, None, :, None] * w[:, :, None, :]).reshape(H * a, H * b)


def _block_diag_parts(d, H):
    a, b = d.shape[0] // H, d.shape[1] // H
    d4 = d.reshape(H, a, H, b)
    return jnp.stack([d4[h, :, h, :] for h in range(H)])


def _local_step(x, mem, target, wf, small):
    D = x.shape[1]
    p = small
    wa_d = _block_diag(p['lru_w_a']).astype(MXU_DTYPE)
    wx_d = _block_diag(p['lru_w_x']).astype(MXU_DTYPE)
    heads = p['lru_w_a'].shape[0]
    w_out2 = wf['w_out'].reshape(2, -1, D)
    w_q = wf['w_q'].reshape(D, D)
    w_o = wf['w_o'].reshape(D, D)
    n_up = wf['w_up'].shape[0]
    w_down2 = wf['w_down'].reshape(n_up // 2, -1, D)

    z, h1 = _fwd_in(x, p['mix_norm_g'], wf['w_in'])
    h, y_lru = _lru_fwd(z, p['lru_conv_w'], p['lru_conv_b'], wa_d, p['lru_b_a'], wx_d, p['lru_b_x'],
                        p['lru_lambda'])
    c1, c3 = _conf_fwd(z, p['conf_conv_w'], p['conf_conv_b'], p['conf_ln_g'], p['conf_ln_b'])
    x1, h2, q = _fwd_out_q(x, y_lru, c3, w_out2, p['xa_norm_g'], w_q)
    m, kv = _kv_fwd(mem, p['mem_norm_g'], wf['w_kv'])
    o, x2, h3 = _attn_fwd(q, kv, x1, w_o, p['ffn_norm_g'])
    gu = _fwd_up(h3, wf['w_up'])
    act, dx3, loss_lanes, d_final_g = _ffn_down_loss(gu, p['ffn_conv_w'], p['ffn_conv_b'], w_down2, x2,
                                                     p['final_norm_g'], target)

    dgu, d_ffn_cw, d_ffn_cb = _bwd_down(dx3, w_down2, gu, p['ffn_conv_w'], p['ffn_conv_b'])
    g_down = _wgrad(act, dx3[None], "wgrad_down")
    dx2, d_ffn_g = _bwd_up(dgu, wf['w_up'], x2, p['ffn_norm_g'], dx3)
    g_up = _wgrad(h3[None], dgu, "wgrad_up")
    dq, dx1, dkv, d_xa_g = _attn_bwd(dx2, w_o, q, kv, x1, p['xa_norm_g'], w_q)
    g_o = _wgrad(o[None], dx2[None], "wgrad_o")
    g_q = _wgrad(h2[None], dq[None], "wgrad_q")
    g_kv, d_mem_g = _kv_bwd(dkv, wf['w_kv'], mem, p['mem_norm_g'], m)
    dz_c, d_conf_cw, d_conf_cb, d_ln_g, d_ln_b = _conf_bwd(dx1, w_out2[1], z, c1, p['conf_conv_w'],
                                                           p['conf_ln_g'], p['conf_ln_b'])
    (dz_l, d_wa, d_wx, d_ba, d_bx, d_lam, d_lru_cw, d_lru_cb) = _lru_bwd(
        dx1, w_out2[0], z, h, p['lru_conv_w'], p['lru_conv_b'], wa_d, p['lru_b_a'], wx_d, p['lru_b_x'],
        p['lru_lambda'])
    g_out = jnp.concatenate([_wgrad(y_lru[None], dx1[None], "wgrad_out_lru"),
                             _wgrad(c3[None], dx1[None], "wgrad_out_conf")], axis=0)
    grad_x, d_mix_g = _bwd_in(dz_l, dz_c, wf['w_in'], x, p['mix_norm_g'], dx1)
    g_in = _wgrad(h1[None], jnp.concatenate([dz_l, dz_c], axis=0), "wgrad_in")

    nch = N_CHIPS
    big = {'w_in': g_in, 'w_kv': g_kv, 'w_up': g_up,
           'w_out': g_out.reshape(nch, -1, D), 'w_q': g_q.reshape(nch, -1, D), 'w_o': g_o.reshape(nch, -1, D),
           'w_down': g_down.reshape(nch, -1, D)}
    small_g = {'mix_norm_g': d_mix_g, 'lru_conv_w': d_lru_cw, 'lru_conv_b': d_lru_cb,
               'lru_w_a': _block_diag_parts(d_wa, heads), 'lru_b_a': d_ba,
               'lru_w_x': _block_diag_parts(d_wx, heads), 'lru_b_x': d_bx, 'lru_lambda': d_lam,
               'conf_conv_w': d_conf_cw, 'conf_conv_b': d_conf_cb, 'conf_ln_g': d_ln_g, 'conf_ln_b': d_ln_b,
               'xa_norm_g': d_xa_g, 'mem_norm_g': d_mem_g, 'ffn_norm_g': d_ffn_g,
               'ffn_conv_w': d_ffn_cw, 'ffn_conv_b': d_ffn_cb, 'final_norm_g': d_final_g}
    return grad_x, big, small_g, loss_lanes


def _reduce_scatter(big_grads):
    names = list(big_grads)
    grads = [big_grads[n] for n in names]
    others = _exchange_halves(grads)
    parts = [_add_halves(g, o, "rs_add_halves_" + n) for n, g, o in zip(names, grads, others)]
    recvs = _scatter_chip_sums(parts)
    halves = [_sum_chips(p, r, "rs_sum_chips_" + n) for n, p, r in zip(names, parts, recvs)]
    return dict(zip(names, _join_halves(halves)))


def kernel(x, mem, mix_norm_g, w_in, lru_conv_w, lru_conv_b, lru_w_a, lru_b_a, lru_w_x, lru_b_x, lru_lambda, conf_conv_w, conf_conv_b, conf_ln_g, conf_ln_b, w_out, xa_norm_g, mem_norm_g, w_q, w_kv, w_o, ffn_norm_g, w_up, ffn_conv_w, ffn_conv_b, w_down, final_norm_g, loss_target, m_mix_norm_g, m_w_in, m_lru_conv_w, m_lru_conv_b, m_lru_w_a, m_lru_b_a, m_lru_w_x, m_lru_b_x, m_lru_lambda, m_conf_conv_w, m_conf_conv_b, m_conf_ln_g, m_conf_ln_b, m_w_out, m_xa_norm_g, m_mem_norm_g, m_w_q, m_w_kv, m_w_o, m_ffn_norm_g, m_w_up, m_ffn_conv_w, m_ffn_conv_b, m_w_down, m_final_norm_g, v_mix_norm_g, v_w_in, v_lru_conv_w, v_lru_conv_b, v_lru_w_a, v_lru_b_a, v_lru_w_x, v_lru_b_x, v_lru_lambda, v_conf_conv_w, v_conf_conv_b, v_conf_ln_g, v_conf_ln_b, v_w_out, v_xa_norm_g, v_mem_norm_g, v_w_q, v_w_kv, v_w_o, v_ffn_norm_g, v_w_up, v_ffn_conv_w, v_ffn_conv_b, v_w_down, v_final_norm_g):
    given = dict(locals())
    w = {n: given[n] for n in WEIGHTS}
    mom = {n: given["m_" + n] for n in WEIGHTS}
    var = {n: given["v_" + n] for n in WEIGHTS}
    xi, yi, ci = lax.axis_index("x"), lax.axis_index("y"), lax.axis_index("c")
    chip = 2 * xi + yi

    shards = [w[n][0].astype(WIRE_DTYPE) for n in BIG]
    wf = dict(zip(BIG, _gather_weights(shards)))
    tap_full = []
    for n in COL_SHARDED_SMALL:
        s = w[n][0]
        full = jnp.zeros((s.shape[0], N_CHIPS * s.shape[1]), F32)
        s = jnp.where(ci == 0, s, jnp.zeros_like(s))
        tap_full.append(lax.dynamic_update_slice(full, s, (0, chip * s.shape[1])))
    tap_shapes = [t.shape for t in tap_full]
    taps = _unpack_rows(_all_reduce_rows(_pack_rows(tap_full), "gather_conv_taps")[0], tap_shapes)
    small = {n: (w[n] if w[n].ndim == 1 else w[n][0]) for n in SMALL}
    small.update(dict(zip(COL_SHARDED_SMALL, taps)))
    small = {n: (a.reshape(1, -1) if a.ndim == 1 else a) for n, a in small.items()}

    grad_x, big_g, small_g, loss_lanes = _local_step(x[0], mem[0], loss_target[0], wf, small)

    big_g = _reduce_scatter(big_g)
    small_names = list(small_g)
    small_shapes = [small_g[n].shape for n in small_names]
    packed = _pack_rows([loss_lanes] + [small_g[n] for n in small_names])
    summed, loss = _all_reduce_rows(packed, "all_reduce_small", loss_row=0)
    small_sum = dict(zip(small_names, _unpack_rows(summed, [loss_lanes.shape] + small_shapes)[1:]))

    grads = {}
    for n in WEIGHTS:
        if n in BIG:
            g = big_g[n]
        elif n in COL_SHARDED_SMALL:
            width = w[n].shape[-1]
            g = lax.dynamic_slice_in_dim(small_sum[n], chip * width, width, axis=1)
        else:
            g = small_sum[n]
        grads[n] = g.reshape(w[n].shape)

    delta, new_m, new_v = {}, {}, {}
    for n in BIG:
        d, nm, nv = _adamw(w[n][0], grads[n][0], mom[n][0], var[n][0], "adamw_" + n)
        delta[n], new_m[n], new_v[n] = d[None], nm[None], nv[None]
    shapes = [w[n].shape for n in SMALL]
    d, nm, nv = _adamw(_pack_rows([w[n] for n in SMALL]), _pack_rows([grads[n] for n in SMALL]),
                       _pack_rows([mom[n] for n in SMALL]), _pack_rows([var[n] for n in SMALL]), "adamw_small")
    for out, buf in ((delta, d), (new_m, nm), (new_v, nv)):
        out.update(dict(zip(SMALL, _unpack_rows(buf, shapes))))

    return (loss[0, 0], grad_x[None], *[grads[n] for n in WEIGHTS], *[delta[n] for n in WEIGHTS],
            *[new_m[n] for n in WEIGHTS], *[new_v[n] for n in WEIGHTS])
```

```python
import math

import jax
import jax.numpy as jnp
from jax import lax
from jax.experimental import pallas as pl
from jax.experimental.pallas import tpu as pltpu

F32 = jnp.float32
MXU_DTYPE = jnp.bfloat16
WIRE_DTYPE = jnp.bfloat16
EPS = 1e-6
RG_C = 8.0
XA_HEADS = 4
ADAM_LR, ADAM_B1, ADAM_B2, ADAM_EPS, ADAM_WD, ADAM_STEP = 0.001, 0.9, 0.999, 1e-08, 0.01, 10
VMEM_LIMIT_BYTES = 52 * 1024 * 1024
LANES = 1024
N_CHIPS = 4
N_DEV = 8
MESH = pl.DeviceIdType.MESH
GELU_C = math.sqrt(2.0 / math.pi)
GELU_K = 0.044715

WEIGHTS = ['mix_norm_g', 'w_in', 'lru_conv_w', 'lru_conv_b', 'lru_w_a', 'lru_b_a', 'lru_w_x', 'lru_b_x',
           'lru_lambda', 'conf_conv_w', 'conf_conv_b', 'conf_ln_g', 'conf_ln_b', 'w_out', 'xa_norm_g',
           'mem_norm_g', 'w_q', 'w_kv', 'w_o', 'ffn_norm_g', 'w_up', 'ffn_conv_w', 'ffn_conv_b', 'w_down',
           'final_norm_g']
BIG = ['w_in', 'w_kv', 'w_up', 'w_out', 'w_q', 'w_o', 'w_down']
SMALL = [n for n in WEIGHTS if n not in BIG]
COL_SHARDED_SMALL = ['lru_conv_w', 'conf_conv_w', 'ffn_conv_w']


def _params(*semantics):
    return pltpu.CompilerParams(dimension_semantics=semantics, vmem_limit_bytes=VMEM_LIMIT_BYTES)


def _tile(n, want, align=8):
    if n <= want:
        return n
    for t in range(want - want % align, 0, -align):
        if n % t == 0:
            return t
    raise ValueError((n, want, align))


def _mm(a, b):
    return jnp.dot(a.astype(MXU_DTYPE), b.astype(MXU_DTYPE), preferred_element_type=F32)


def _mm_nt(a, b):
    return lax.dot_general(a.astype(MXU_DTYPE), b.astype(MXU_DTYPE), (((1,), (1,)), ((), ())),
                           preferred_element_type=F32)


def _mm_tn(a, b):
    return lax.dot_general(a.astype(MXU_DTYPE), b.astype(MXU_DTYPE), (((0,), (0,)), ((), ())),
                           preferred_element_type=F32)


def _sigmoid(v):
    return 1.0 / (1.0 + jnp.exp(-v))


def _gelu(v):
    v2 = v * v
    t = jnp.tanh(GELU_C * (v + GELU_K * v * v2))
    return 0.5 * v * (1.0 + t), 0.5 * (1.0 + t) + 0.5 * v * (1.0 - t * t) * GELU_C * (1.0 + 3.0 * GELU_K * v2)


def _softplus_neg(lam):
    e = jnp.exp(-jnp.abs(lam))
    u = 1.0 + e
    log1p_e = jnp.where(u == 1.0, e, jnp.log(u) * e / jnp.where(u == 1.0, 1.0, u - 1.0))
    return jnp.maximum(-lam, 0.0) + log1p_e


def _rms(xv):
    rinv = lax.rsqrt(jnp.mean(xv * xv, axis=-1, keepdims=True) + EPS)
    return rinv, xv * rinv


def _rms_bwd(rinv, xhat, dxhat):
    return rinv * (dxhat - xhat * jnp.mean(dxhat * xhat, axis=-1, keepdims=True))


def _colsum(v):
    return jnp.sum(v, axis=0, keepdims=True)


def _wrow(w_ref, k, wcols):
    return w_ref[pl.ds(k, 1), :] if wcols is None else w_ref[pl.ds(k, 1), wcols]


def _causal_taps(buf_ref, halo, w_ref, taps, rows, wcols=None):
    acc = None
    for s in range(taps):
        term = _wrow(w_ref, taps - 1 - s, wcols) * buf_ref[pl.ds(halo - s, rows), :]
        acc = term if acc is None else acc + term
    return acc


def _anticausal_taps(buf_ref, w_ref, taps, rows, wcols=None):
    acc = None
    for s in range(taps):
        term = _wrow(w_ref, taps - 1 - s, wcols) * buf_ref[pl.ds(s, rows), :]
        acc = term if acc is None else acc + term
    return acc


def _tap_grads(dw_ref, dy, buf_ref, halo, taps, rows, wcols=None):
    for s in range(taps):
        g = _colsum(dy * buf_ref[pl.ds(halo - s, rows), :])
        if wcols is None:
            dw_ref[pl.ds(taps - 1 - s, 1), :] += g
        else:
            dw_ref[pl.ds(taps - 1 - s, 1), wcols] += g


def _fwd_in(x, g, w_in):
    S, D = x.shape
    nb, _, C = w_in.shape
    ts = _tile(S, 512)

    def body(x_ref, g_ref, w_ref, z_ref, h_ref):
        _, xhat = _rms(x_ref[...])
        h = (xhat * g_ref[...]).astype(MXU_DTYPE)
        h_ref[...] = h
        for j in range(nb):
            z_ref[j] = jnp.dot(h, w_ref[j], preferred_element_type=F32)

    return pl.pallas_call(
        body, name="fwd_in", grid=(S // ts,),
        in_specs=[pl.BlockSpec((ts, D), lambda i: (i, 0)), pl.BlockSpec((1, D), lambda i: (0, 0)),
                  pl.BlockSpec((nb, D, C), lambda i: (0, 0, 0))],
        out_specs=[pl.BlockSpec((nb, ts, C), lambda i: (0, i, 0)), pl.BlockSpec((ts, D), lambda i: (i, 0))],
        out_shape=[jax.ShapeDtypeStruct((nb, S, C), F32), jax.ShapeDtypeStruct((S, D), MXU_DTYPE)],
        compiler_params=_params("parallel"))(x, g, w_in)


def _lru_gates(xc, wa_ref, ba_ref, wx_ref, bx_ref, sp):
    xb = xc.astype(MXU_DTYPE)
    r = _sigmoid(jnp.dot(xb, wa_ref[...], preferred_element_type=F32) + ba_ref[...])
    ig = _sigmoid(jnp.dot(xb, wx_ref[...], preferred_element_type=F32) + bx_ref[...])
    log_a = -RG_C * r * sp
    a = jnp.exp(log_a)
    mult = jnp.sqrt(jnp.tanh(-log_a) * (a * a + 1.0))
    return r, ig, a, mult


def _lru_fwd(z, conv_w, conv_b, wa, ba, wx, bx, lam):
    _, S, C = z.shape
    ts = _tile(S, 256)
    taps = conv_w.shape[0]
    halo = 8

    def body(zx_ref, zg_ref, cw_ref, cb_ref, wa_ref, ba_ref, wx_ref, bx_ref, lam_ref,
             h_ref, y_ref, xbuf, a_s, u_s, hc):
        i = pl.program_id(0)

        @pl.when(i == 0)
        def _():
            xbuf[pl.ds(0, halo), :] = jnp.zeros((halo, C), F32)
            hc[...] = jnp.zeros_like(hc)

        xbuf[pl.ds(halo, ts), :] = zx_ref[0]
        xc = _causal_taps(xbuf, halo, cw_ref, taps, ts) + cb_ref[...]
        sp = _softplus_neg(lam_ref[...])
        _, ig, a, mult = _lru_gates(xc, wa_ref, ba_ref, wx_ref, bx_ref, sp)
        a_s[...] = a
        u_s[...] = mult * (ig * xc)
        row = lax.broadcasted_iota(jnp.int32, (8, C), 0)

        def step(k, carry):
            off = pl.multiple_of(k * 8, 8)
            av = a_s[pl.ds(off, 8), :]
            uv = u_s[pl.ds(off, 8), :]
            for d in (1, 2, 4):
                m = row >= d
                a_sh = jnp.where(m, pltpu.roll(av, d, 0), 1.0)
                u_sh = jnp.where(m, pltpu.roll(uv, d, 0), 0.0)
                uv = uv + av * u_sh
                av = av * a_sh
            hv = uv + av * carry
            h_ref[pl.ds(off, 8), :] = hv
            return jnp.broadcast_to(hv[7:8, :], (8, C))

        hc[...] = lax.fori_loop(0, ts // 8, step, hc[...])
        ge, _ = _gelu(zg_ref[0])
        y_ref[...] = (h_ref[...] * ge).astype(MXU_DTYPE)
        xbuf[pl.ds(0, halo), :] = xbuf[pl.ds(ts, halo), :]

    vec = pl.BlockSpec((1, C), lambda i: (0, 0))
    mat = pl.BlockSpec((C, C), lambda i: (0, 0))
    return pl.pallas_call(
        body, name="lru_fwd", grid=(S // ts,),
        in_specs=[pl.BlockSpec((1, ts, C), lambda i: (0, i, 0)), pl.BlockSpec((1, ts, C), lambda i: (1, i, 0)),
                  pl.BlockSpec((taps, C), lambda i: (0, 0)), vec, mat, vec, mat, vec, vec],
        out_specs=[pl.BlockSpec((ts, C), lambda i: (i, 0)), pl.BlockSpec((ts, C), lambda i: (i, 0))],
        out_shape=[jax.ShapeDtypeStruct((S, C), F32), jax.ShapeDtypeStruct((S, C), MXU_DTYPE)],
        scratch_shapes=[pltpu.VMEM((ts + halo, C), F32), pltpu.VMEM((ts, C), F32), pltpu.VMEM((ts, C), F32),
                        pltpu.VMEM((8, C), F32)],
        compiler_params=_params("arbitrary"))(z, z, conv_w, conv_b, wa, ba, wx, bx, lam)


def _layer_norm_stats(c1):
    mu = jnp.mean(c1, axis=-1, keepdims=True)
    xc = c1 - mu
    rstd = lax.rsqrt(jnp.mean(xc * xc, axis=-1, keepdims=True) + EPS)
    return rstd, xc * rstd


def _conf_fwd(z, conv_w, conv_b, ln_g, ln_b):
    _, S, C = z.shape
    ts = _tile(S, 256)
    taps = conv_w.shape[0]
    halo = 32

    def body(za_ref, zb_ref, cw_ref, cb_ref, g_ref, b_ref, c1_ref, c3_ref, cbuf):
        i = pl.program_id(0)

        @pl.when(i == 0)
        def _():
            cbuf[pl.ds(0, halo), :] = jnp.zeros((halo, C), F32)

        cbuf[pl.ds(halo, ts), :] = za_ref[0] * _sigmoid(zb_ref[0])
        c1 = _causal_taps(cbuf, halo, cw_ref, taps, ts) + cb_ref[...]
        c1_ref[...] = c1
        _, xhat = _layer_norm_stats(c1)
        c2 = xhat * g_ref[...] + b_ref[...]
        c3_ref[...] = (c2 * _sigmoid(c2)).astype(MXU_DTYPE)
        cbuf[pl.ds(0, halo), :] = cbuf[pl.ds(ts, halo), :]

    vec = pl.BlockSpec((1, C), lambda i: (0, 0))
    return pl.pallas_call(
        body, name="conf_fwd", grid=(S // ts,),
        in_specs=[pl.BlockSpec((1, ts, C), lambda i: (2, i, 0)), pl.BlockSpec((1, ts, C), lambda i: (3, i, 0)),
                  pl.BlockSpec((taps, C), lambda i: (0, 0)), vec, vec, vec],
        out_specs=[pl.BlockSpec((ts, C), lambda i: (i, 0)), pl.BlockSpec((ts, C), lambda i: (i, 0))],
        out_shape=[jax.ShapeDtypeStruct((S, C), F32), jax.ShapeDtypeStruct((S, C), MXU_DTYPE)],
        scratch_shapes=[pltpu.VMEM((ts + halo, C), F32)],
        compiler_params=_params("arbitrary"))(z, z, conv_w, conv_b, ln_g, ln_b)


def _fwd_out_q(x, y_lru, c3, w_out, g_xa, w_q):
    S, D = x.shape
    C = y_lru.shape[1]
    ts = _tile(S, 512)

    def body(x_ref, yl_ref, c3_ref, wo_ref, g_ref, wq_ref, x1_ref, h2_ref, q_ref):
        x1 = (x_ref[...] + jnp.dot(yl_ref[...], wo_ref[0], preferred_element_type=F32)
              + jnp.dot(c3_ref[...], wo_ref[1], preferred_element_type=F32))
        x1_ref[...] = x1
        _, xhat = _rms(x1)
        h2 = (xhat * g_ref[...]).astype(MXU_DTYPE)
        h2_ref[...] = h2
        q_ref[...] = jnp.dot(h2, wq_ref[...], preferred_element_type=F32).astype(MXU_DTYPE)

    row = lambda w: pl.BlockSpec((ts, w), lambda i: (i, 0))
    return pl.pallas_call(
        body, name="fwd_out_q", grid=(S // ts,),
        in_specs=[row(D), row(C), row(C), pl.BlockSpec((2, C, D), lambda i: (0, 0, 0)),
                  pl.BlockSpec((1, D), lambda i: (0, 0)), pl.BlockSpec((D, D), lambda i: (0, 0))],
        out_specs=[row(D), row(D), row(D)],
        out_shape=[jax.ShapeDtypeStruct((S, D), F32), jax.ShapeDtypeStruct((S, D), MXU_DTYPE),
                   jax.ShapeDtypeStruct((S, D), MXU_DTYPE)],
        compiler_params=_params("parallel"))(x, y_lru, c3, w_out, g_xa, w_q)


def _kv_fwd(mem, g, w_kv):
    M, D = mem.shape
    nb, _, C = w_kv.shape

    def body(mem_ref, g_ref, w_ref, m_ref, kv_ref):
        _, xhat = _rms(mem_ref[...])
        m = (xhat * g_ref[...]).astype(MXU_DTYPE)
        m_ref[...] = m
        for j in range(nb):
            kv_ref[:, pl.ds(j * C, C)] = jnp.dot(m, w_ref[j], preferred_element_type=F32).astype(MXU_DTYPE)

    return pl.pallas_call(
        body, name="kv_fwd", grid=(1,),
        in_specs=[pl.BlockSpec((M, D), lambda i: (0, 0)), pl.BlockSpec((1, D), lambda i: (0, 0)),
                  pl.BlockSpec((nb, D, C), lambda i: (0, 0, 0))],
        out_specs=[pl.BlockSpec((M, D), lambda i: (0, 0)), pl.BlockSpec((M, nb * C), lambda i: (0, 0))],
        out_shape=[jax.ShapeDtypeStruct((M, D), MXU_DTYPE), jax.ShapeDtypeStruct((M, nb * C), MXU_DTYPE)],
        compiler_params=_params("arbitrary"))(mem, g, w_kv)


def _softmax_rows(s):
    e = jnp.exp(s - jnp.max(s, axis=-1, keepdims=True))
    return e / jnp.sum(e, axis=-1, keepdims=True)


def _attn_fwd(q, kv, x1, w_o, g_ffn):
    S, D = x1.shape
    M = kv.shape[0]
    hd = D // XA_HEADS
    scale = hd ** -0.5
    ts = _tile(S, 512)

    def body(q_ref, kv_ref, x1_ref, wo_ref, g_ref, o_ref, x2_ref, h3_ref):
        for h in range(XA_HEADS):
            cols = pl.ds(h * hd, hd)
            p = _softmax_rows(_mm_nt(q_ref[:, cols], kv_ref[:, cols]) * scale)
            o_ref[:, cols] = _mm(p, kv_ref[:, pl.ds(D + h * hd, hd)]).astype(MXU_DTYPE)
        x2 = x1_ref[...] + jnp.dot(o_ref[...], wo_ref[...], preferred_element_type=F32)
        x2_ref[...] = x2
        _, xhat = _rms(x2)
        h3_ref[...] = (xhat * g_ref[...]).astype(MXU_DTYPE)

    row = pl.BlockSpec((ts, D), lambda i: (i, 0))
    return pl.pallas_call(
        body, name="attn_fwd", grid=(S // ts,),
        in_specs=[row, pl.BlockSpec((M, 2 * D), lambda i: (0, 0)), row, pl.BlockSpec((D, D), lambda i: (0, 0)),
                  pl.BlockSpec((1, D), lambda i: (0, 0))],
        out_specs=[row, row, row],
        out_shape=[jax.ShapeDtypeStruct((S, D), MXU_DTYPE), jax.ShapeDtypeStruct((S, D), F32),
                   jax.ShapeDtypeStruct((S, D), MXU_DTYPE)],
        compiler_params=_params("parallel"))(q, kv, x1, w_o, g_ffn)


def _fwd_up(h3, w_up):
    S, D = h3.shape
    nb, _, C = w_up.shape
    ts = _tile(S, 512)

    def body(h_ref, w_ref, o_ref):
        o_ref[0] = jnp.dot(h_ref[...], w_ref[0], preferred_element_type=F32)

    return pl.pallas_call(
        body, name="fwd_up", grid=(nb, S // ts),
        in_specs=[pl.BlockSpec((ts, D), lambda j, i: (i, 0)), pl.BlockSpec((1, D, C), lambda j, i: (j, 0, 0))],
        out_specs=pl.BlockSpec((1, ts, C), lambda j, i: (j, i, 0)),
        out_shape=jax.ShapeDtypeStruct((nb, S, C), F32),
        compiler_params=_params("parallel", "parallel"))(h3, w_up)


def _ffn_down_loss(gu, conv_w, conv_b, w_down, x2, g_final, target):
    nb, S, C = gu.shape
    half = nb // 2
    D = x2.shape[1]
    ts = _tile(S, 256)
    taps = conv_w.shape[0]
    halo = 8
    hb = ts // halo

    def body(g_ref, gh_ref, u_ref, cw_ref, cb_ref, wd_ref, x2_ref, gf_ref, t_ref,
             act_ref, dx3_ref, loss_ref, dgf_ref, gbuf):
        i = pl.program_id(0)

        @pl.when(i == 0)
        def _():
            loss_ref[...] = jnp.zeros_like(loss_ref)
            dgf_ref[...] = jnp.zeros_like(dgf_ref)

        x3 = x2_ref[...]
        for j in range(half):
            cols = pl.ds(j * C, C)
            gbuf[pl.ds(0, halo), :] = jnp.where(i > 0, gh_ref[j], 0.0)
            gbuf[pl.ds(halo, ts), :] = g_ref[j]
            gc = _causal_taps(gbuf, halo, cw_ref, taps, ts, wcols=cols) + cb_ref[:, cols]
            ge, _ = _gelu(gc)
            act = (ge * u_ref[j]).astype(MXU_DTYPE)
            act_ref[j] = act
            x3 = x3 + jnp.dot(act, wd_ref[j], preferred_element_type=F32)
        rinv, xhat = _rms(x3)
        gf = gf_ref[...]
        diff = xhat * gf - t_ref[...]
        loss_ref[...] += _colsum(diff * diff) * (0.5 / D)
        dy = diff * (1.0 / D)
        dgf_ref[...] += _colsum(dy * xhat)
        dx3_ref[...] = _rms_bwd(rinv, xhat, dy * gf)

    row = pl.BlockSpec((ts, D), lambda i: (i, 0))
    vecd = pl.BlockSpec((1, D), lambda i: (0, 0))
    return pl.pallas_call(
        body, name="ffn_down_loss", grid=(S // ts,),
        in_specs=[pl.BlockSpec((half, ts, C), lambda i: (0, i, 0)),
                  pl.BlockSpec((half, halo, C), lambda i: (0, jnp.maximum(i * hb - 1, 0), 0)),
                  pl.BlockSpec((half, ts, C), lambda i: (1, i, 0)),
                  pl.BlockSpec((taps, half * C), lambda i: (0, 0)), pl.BlockSpec((1, half * C), lambda i: (0, 0)),
                  pl.BlockSpec((half, C, D), lambda i: (0, 0, 0)), row, vecd, row],
        out_specs=[pl.BlockSpec((half, ts, C), lambda i: (0, i, 0)), row, vecd, vecd],
        out_shape=[jax.ShapeDtypeStruct((half, S, C), MXU_DTYPE), jax.ShapeDtypeStruct((S, D), F32),
                   jax.ShapeDtypeStruct((1, D), F32), jax.ShapeDtypeStruct((1, D), F32)],
        scratch_shapes=[pltpu.VMEM((ts + halo, C), F32)],
        compiler_params=_params("arbitrary"))(gu, gu, gu, conv_w, conv_b, w_down, x2, g_final, target)


def _bwd_down(dx3, w_down, gu, conv_w, conv_b):
    nb, S, C = gu.shape
    half = nb // 2
    D = dx3.shape[1]
    ts = _tile(S, 256)
    n = S // ts
    taps = conv_w.shape[0]
    halo = 8
    hb = ts // halo

    def body(dx_ref, wd_ref, g_ref, gh_ref, u_ref, cw_ref, cb_ref, dgu_ref, dcw_ref, dcb_ref, gbuf, dbuf):
        i = pl.program_id(0)
        r = n - 1 - i

        @pl.when(i == 0)
        def _():
            dcw_ref[...] = jnp.zeros_like(dcw_ref)
            dcb_ref[...] = jnp.zeros_like(dcb_ref)
            dbuf[...] = jnp.zeros_like(dbuf)

        dxb = dx_ref[...].astype(MXU_DTYPE)
        for j in range(half):
            cols = pl.ds(j * C, C)
            dact = _mm_nt(dxb, wd_ref[j])
            gbuf[pl.ds(0, halo), :] = jnp.where(r > 0, gh_ref[j], 0.0)
            gbuf[pl.ds(halo, ts), :] = g_ref[j]
            gc = _causal_taps(gbuf, halo, cw_ref, taps, ts, wcols=cols) + cb_ref[:, cols]
            ge, dge = _gelu(gc)
            dgu_ref[half + j] = (dact * ge).astype(MXU_DTYPE)
            dgc = dact * u_ref[j] * dge
            dcb_ref[:, cols] += _colsum(dgc)
            dbuf[j, pl.ds(0, ts), :] = dgc
            _tap_grads(dcw_ref, dgc, gbuf, halo, taps, ts, wcols=cols)
            dgu_ref[j] = _anticausal_taps(dbuf.at[j], cw_ref, taps, ts, wcols=cols).astype(MXU_DTYPE)
            dbuf[j, pl.ds(ts, halo), :] = dbuf[j, pl.ds(0, halo), :]

    row = pl.BlockSpec((ts, D), lambda i: (n - 1 - i, 0))
    wide = pl.BlockSpec((taps, half * C), lambda i: (0, 0))
    wide1 = pl.BlockSpec((1, half * C), lambda i: (0, 0))
    return pl.pallas_call(
        body, name="bwd_down", grid=(n,),
        in_specs=[row, pl.BlockSpec((half, C, D), lambda i: (0, 0, 0)),
                  pl.BlockSpec((half, ts, C), lambda i: (0, n - 1 - i, 0)),
                  pl.BlockSpec((half, halo, C), lambda i: (0, jnp.maximum((n - 1 - i) * hb - 1, 0), 0)),
                  pl.BlockSpec((half, ts, C), lambda i: (1, n - 1 - i, 0)), wide, wide1],
        out_specs=[pl.BlockSpec((nb, ts, C), lambda i: (0, n - 1 - i, 0)), wide, wide1],
        out_shape=[jax.ShapeDtypeStruct((nb, S, C), MXU_DTYPE), jax.ShapeDtypeStruct((taps, half * C), F32),
                   jax.ShapeDtypeStruct((1, half * C), F32)],
        scratch_shapes=[pltpu.VMEM((ts + halo, C), F32), pltpu.VMEM((half, ts + halo, C), F32)],
        compiler_params=_params("arbitrary"))(dx3, w_down, gu, gu, gu, conv_w, conv_b)


def _bwd_up(dgu, w_up, x2, g_ffn, dx3):
    nb, S, C = dgu.shape
    D = x2.shape[1]
    ts = _tile(S, 512)

    def body(d_ref, w_ref, x2_ref, g_ref, dx3_ref, dx2_ref, dg_ref, acc):
        i, j = pl.program_id(0), pl.program_id(1)

        @pl.when((i == 0) & (j == 0))
        def _():
            dg_ref[...] = jnp.zeros_like(dg_ref)

        part = _mm_nt(d_ref[0], w_ref[0])

        @pl.when(j == 0)
        def _():
            acc[...] = part

        @pl.when(j > 0)
        def _():
            acc[...] += part

        @pl.when(j == nb - 1)
        def _():
            rinv, xhat = _rms(x2_ref[...])
            dh = acc[...]
            dg_ref[...] += _colsum(dh * xhat)
            dx2_ref[...] = dx3_ref[...] + _rms_bwd(rinv, xhat, dh * g_ref[...])

    row = pl.BlockSpec((ts, D), lambda i, j: (i, 0))
    vecd = pl.BlockSpec((1, D), lambda i, j: (0, 0))
    return pl.pallas_call(
        body, name="bwd_up", grid=(S // ts, nb),
        in_specs=[pl.BlockSpec((1, ts, C), lambda i, j: (j, i, 0)), pl.BlockSpec((1, D, C), lambda i, j: (j, 0, 0)),
                  row, vecd, row],
        out_specs=[row, vecd],
        out_shape=[jax.ShapeDtypeStruct((S, D), F32), jax.ShapeDtypeStruct((1, D), F32)],
        scratch_shapes=[pltpu.VMEM((ts, D), F32)],
        compiler_params=_params("arbitrary", "arbitrary"))(dgu, w_up, x2, g_ffn, dx3)


def _attn_bwd(dx2, w_o, q, kv, x1, g_xa, w_q):
    S, D = x1.shape
    M = kv.shape[0]
    hd = D // XA_HEADS
    scale = hd ** -0.5
    ts = _tile(S, 512)

    def body(dx2_ref, wo_ref, q_ref, kv_ref, x1_ref, g_ref, wq_ref, dq_ref, dx1_ref, dkv_ref, dg_ref):
        i = pl.program_id(0)

        @pl.when(i == 0)
        def _():
            dkv_ref[...] = jnp.zeros_like(dkv_ref)
            dg_ref[...] = jnp.zeros_like(dg_ref)

        dx2 = dx2_ref[...]
        do = _mm_nt(dx2, wo_ref[...]).astype(MXU_DTYPE)
        for h in range(XA_HEADS):
            cols = pl.ds(h * hd, hd)
            vcols = pl.ds(D + h * hd, hd)
            qh, kh, doh = q_ref[:, cols], kv_ref[:, cols], do[:, h * hd:(h + 1) * hd]
            p = _softmax_rows(_mm_nt(qh, kh) * scale)
            dp = _mm_nt(doh, kv_ref[:, vcols])
            dkv_ref[:, vcols] += _mm_tn(p, doh)
            ds = (p * (dp - jnp.sum(dp * p, axis=-1, keepdims=True)) * scale).astype(MXU_DTYPE)
            dq_ref[:, cols] = _mm(ds, kh).astype(MXU_DTYPE)
            dkv_ref[:, cols] += _mm_tn(ds, qh)
        dh2 = _mm_nt(dq_ref[...], wq_ref[...])
        rinv, xhat = _rms(x1_ref[...])
        dg_ref[...] += _colsum(dh2 * xhat)
        dx1_ref[...] = dx2 + _rms_bwd(rinv, xhat, dh2 * g_ref[...])

    row = pl.BlockSpec((ts, D), lambda i: (i, 0))
    mat = pl.BlockSpec((D, D), lambda i: (0, 0))
    vecd = pl.BlockSpec((1, D), lambda i: (0, 0))
    kvs = pl.BlockSpec((M, 2 * D), lambda i: (0, 0))
    return pl.pallas_call(
        body, name="attn_bwd", grid=(S // ts,),
        in_specs=[row, mat, row, kvs, row, vecd, mat],
        out_specs=[row, row, kvs, vecd],
        out_shape=[jax.ShapeDtypeStruct((S, D), MXU_DTYPE), jax.ShapeDtypeStruct((S, D), F32),
                   jax.ShapeDtypeStruct((M, 2 * D), F32), jax.ShapeDtypeStruct((1, D), F32)],
        compiler_params=_params("arbitrary"))(dx2, w_o, q, kv, x1, g_xa, w_q)


def _kv_bwd(dkv, w_kv, mem, g, m):
    M, D = mem.shape
    nb, _, C = w_kv.shape

    def body(dkv_ref, w_ref, mem_ref, m_ref, dw_ref, dg_ref):
        dm = jnp.zeros((M, D), F32)
        for j in range(nb):
            dj = dkv_ref[:, pl.ds(j * C, C)].astype(MXU_DTYPE)
            dw_ref[j] = _mm_tn(m_ref[...], dj).astype(dw_ref.dtype)
            dm = dm + _mm_nt(dj, w_ref[j])
        _, xhat = _rms(mem_ref[...])
        dg_ref[...] = _colsum(dm * xhat)

    full = lambda *s: pl.BlockSpec(s, lambda i: (0,) * len(s))
    return pl.pallas_call(
        body, name="kv_bwd", grid=(1,),
        in_specs=[full(M, nb * C), full(nb, D, C), full(M, D), full(M, D)],
        out_specs=[full(nb, D, C), full(1, D)],
        out_shape=[jax.ShapeDtypeStruct((nb, D, C), WIRE_DTYPE), jax.ShapeDtypeStruct((1, D), F32)],
        compiler_params=_params("arbitrary"))(dkv, w_kv, mem, m)


def _conf_bwd(dx1, w_out_c, z, c1, conv_w, ln_g, ln_b):
    _, S, C = z.shape
    D = dx1.shape[1]
    ts = _tile(S, 256)
    n = S // ts
    taps = conv_w.shape[0]
    halo = 32
    hb = ts // halo

    def body(dx_ref, wo_ref, za_ref, zb_ref, zah_ref, zbh_ref, c1_ref, cw_ref, g_ref, b_ref,
             dz_ref, dcw_ref, dcb_ref, dlg_ref, dlb_ref, c0buf, dbuf):
        i = pl.program_id(0)
        r = n - 1 - i

        @pl.when(i == 0)
        def _():
            for ref in (dcw_ref, dcb_ref, dlg_ref, dlb_ref):
                ref[...] = jnp.zeros_like(ref)
            dbuf[pl.ds(ts, halo), :] = jnp.zeros((halo, C), F32)

        za = za_ref[0]
        sb = _sigmoid(zb_ref[0])
        c0buf[pl.ds(0, halo), :] = jnp.where(r > 0, zah_ref[0] * _sigmoid(zbh_ref[0]), 0.0)
        c0buf[pl.ds(halo, ts), :] = za * sb
        dc3 = _mm_nt(dx_ref[...], wo_ref[...])
        rstd, xhat = _layer_norm_stats(c1_ref[...])
        g = g_ref[...]
        c2 = xhat * g + b_ref[...]
        sg = _sigmoid(c2)
        dc2 = dc3 * sg * (1.0 + c2 * (1.0 - sg))
        dlg_ref[...] += _colsum(dc2 * xhat)
        dlb_ref[...] += _colsum(dc2)
        dxh = dc2 * g
        dc1 = rstd * (dxh - jnp.mean(dxh, axis=-1, keepdims=True)
                      - xhat * jnp.mean(dxh * xhat, axis=-1, keepdims=True))
        dcb_ref[...] += _colsum(dc1)
        dbuf[pl.ds(0, ts), :] = dc1
        _tap_grads(dcw_ref, dc1, c0buf, halo, taps, ts)
        dc0 = _anticausal_taps(dbuf, cw_ref, taps, ts)
        dz_ref[0] = (dc0 * sb).astype(MXU_DTYPE)
        dz_ref[1] = (dc0 * za * sb * (1.0 - sb)).astype(MXU_DTYPE)
        dbuf[pl.ds(ts, halo), :] = dbuf[pl.ds(0, halo), :]

    vec = pl.BlockSpec((1, C), lambda i: (0, 0))
    tapw = pl.BlockSpec((taps, C), lambda i: (0, 0))
    tile = lambda b: pl.BlockSpec((1, ts, C), lambda i: (b, n - 1 - i, 0))
    prev = lambda b: pl.BlockSpec((1, halo, C), lambda i: (b, jnp.maximum((n - 1 - i) * hb - 1, 0), 0))
    return pl.pallas_call(
        body, name="conf_bwd", grid=(n,),
        in_specs=[pl.BlockSpec((ts, D), lambda i: (n - 1 - i, 0)), pl.BlockSpec((C, D), lambda i: (0, 0)),
                  tile(2), tile(3), prev(2), prev(3), pl.BlockSpec((ts, C), lambda i: (n - 1 - i, 0)),
                  tapw, vec, vec],
        out_specs=[pl.BlockSpec((2, ts, C), lambda i: (0, n - 1 - i, 0)), tapw, vec, vec, vec],
        out_shape=[jax.ShapeDtypeStruct((2, S, C), MXU_DTYPE), jax.ShapeDtypeStruct((taps, C), F32),
                   jax.ShapeDtypeStruct((1, C), F32), jax.ShapeDtypeStruct((1, C), F32),
                   jax.ShapeDtypeStruct((1, C), F32)],
        scratch_shapes=[pltpu.VMEM((ts + halo, C), F32), pltpu.VMEM((ts + halo, C), F32)],
        compiler_params=_params("arbitrary"))(dx1, w_out_c, z, z, z, z, c1, conv_w, ln_g, ln_b)


def _lru_bwd(dx1, w_out_l, z, h, conv_w, conv_b, wa, ba, wx, bx, lam):
    _, S, C = z.shape
    D = dx1.shape[1]
    ts = _tile(S, 256)
    n = S // ts
    taps = conv_w.shape[0]
    halo = 8
    hb = ts // halo

    def body(dx_ref, wo_ref, zx_ref, zxh_ref, zg_ref, h_ref, hh_ref, cw_ref, cb_ref, wa_ref, ba_ref,
             wx_ref, bx_ref, lam_ref,
             dz_ref, dwa_ref, dwx_ref, dba_ref, dbx_ref, dlam_ref, dcw_ref, dcb_ref,
             xbuf, hbuf, a_s, w_s, dh_s, g_s, dbuf, pc):
        i = pl.program_id(0)
        r = n - 1 - i

        @pl.when(i == 0)
        def _():
            for ref in (dwa_ref, dwx_ref, dba_ref, dbx_ref, dlam_ref, dcw_ref, dcb_ref, pc):
                ref[...] = jnp.zeros_like(ref)
            dbuf[pl.ds(ts, halo), :] = jnp.zeros((halo, C), F32)

        xbuf[pl.ds(0, halo), :] = jnp.where(r > 0, zxh_ref[0], 0.0)
        xbuf[pl.ds(halo, ts), :] = zx_ref[0]
        hbuf[pl.ds(0, halo), :] = jnp.where(r > 0, hh_ref[...], 0.0)
        hbuf[pl.ds(halo, ts), :] = h_ref[...]
        xc = _causal_taps(xbuf, halo, cw_ref, taps, ts) + cb_ref[...]
        lam_v = lam_ref[...]
        sp = _softplus_neg(lam_v)
        rg, ig, a, mult = _lru_gates(xc, wa_ref, ba_ref, wx_ref, bx_ref, sp)

        dy = _mm_nt(dx_ref[...], wo_ref[...])
        ge, dge = _gelu(zg_ref[0])
        dh = dy * ge
        dz_ref[1] = (dy * h_ref[...] * dge).astype(MXU_DTYPE)
        a_s[...] = a
        w_s[...] = a * dh
        dh_s[...] = dh
        row = lax.broadcasted_iota(jnp.int32, (8, C), 0)

        def step(kk, carry):
            off = pl.multiple_of((ts // 8 - 1 - kk) * 8, 8)
            av = a_s[pl.ds(off, 8), :]
            wv = w_s[pl.ds(off, 8), :]
            for d in (1, 2, 4):
                m = row < 8 - d
                a_sh = jnp.where(m, pltpu.roll(av, 8 - d, 0), 1.0)
                w_sh = jnp.where(m, pltpu.roll(wv, 8 - d, 0), 0.0)
                wv = wv + av * w_sh
                av = av * a_sh
            pv = wv + av * carry
            g_s[pl.ds(off, 8), :] = dh_s[pl.ds(off, 8), :] + jnp.where(row < 7, pltpu.roll(pv, 7, 0), carry)
            return jnp.broadcast_to(pv[0:1, :], (8, C))

        pc[...] = lax.fori_loop(0, ts // 8, step, pc[...])
        gt = g_s[...]
        da = gt * hbuf[pl.ds(halo - 1, ts), :]
        gm = gt * mult
        dlog_a = da * a - (gt * ig * xc) * (a * a) / mult
        dlam_ref[...] += _colsum(dlog_a * rg) * (RG_C * _sigmoid(-lam_v))
        dpa = (dlog_a * (-RG_C * sp)) * rg * (1.0 - rg)
        dpx = (gm * xc) * ig * (1.0 - ig)
        dba_ref[...] += _colsum(dpa)
        dbx_ref[...] += _colsum(dpx)
        xb = xc.astype(MXU_DTYPE)
        dpab, dpxb = dpa.astype(MXU_DTYPE), dpx.astype(MXU_DTYPE)
        dwa_ref[...] += _mm_tn(xb, dpab)
        dwx_ref[...] += _mm_tn(xb, dpxb)
        dxc = gm * ig + _mm_nt(dpab, wa_ref[...]) + _mm_nt(dpxb, wx_ref[...])
        dcb_ref[...] += _colsum(dxc)
        dbuf[pl.ds(0, ts), :] = dxc
        _tap_grads(dcw_ref, dxc, xbuf, halo, taps, ts)
        dz_ref[0] = _anticausal_taps(dbuf, cw_ref, taps, ts).astype(MXU_DTYPE)
        dbuf[pl.ds(ts, halo), :] = dbuf[pl.ds(0, halo), :]

    vec = pl.BlockSpec((1, C), lambda i: (0, 0))
    mat = pl.BlockSpec((C, C), lambda i: (0, 0))
    tapw = pl.BlockSpec((taps, C), lambda i: (0, 0))
    prev_rows = lambda i: jnp.maximum((n - 1 - i) * hb - 1, 0)
    sds = jax.ShapeDtypeStruct
    return pl.pallas_call(
        body, name="lru_bwd", grid=(n,),
        in_specs=[pl.BlockSpec((ts, D), lambda i: (n - 1 - i, 0)), pl.BlockSpec((C, D), lambda i: (0, 0)),
                  pl.BlockSpec((1, ts, C), lambda i: (0, n - 1 - i, 0)),
                  pl.BlockSpec((1, halo, C), lambda i: (0, prev_rows(i), 0)),
                  pl.BlockSpec((1, ts, C), lambda i: (1, n - 1 - i, 0)),
                  pl.BlockSpec((ts, C), lambda i: (n - 1 - i, 0)),
                  pl.BlockSpec((halo, C), lambda i: (prev_rows(i), 0)),
                  tapw, vec, mat, vec, mat, vec, vec],
        out_specs=[pl.BlockSpec((2, ts, C), lambda i: (0, n - 1 - i, 0)), mat, mat, vec, vec, vec, tapw, vec],
        out_shape=[sds((2, S, C), MXU_DTYPE), sds((C, C), F32), sds((C, C), F32), sds((1, C), F32),
                   sds((1, C), F32), sds((1, C), F32), sds((taps, C), F32), sds((1, C), F32)],
        scratch_shapes=[pltpu.VMEM((ts + halo, C), F32), pltpu.VMEM((ts + halo, C), F32)]
        + [pltpu.VMEM((ts, C), F32)] * 4 + [pltpu.VMEM((ts + halo, C), F32), pltpu.VMEM((8, C), F32)],
        compiler_params=_params("arbitrary"))(dx1, w_out_l, z, z, z, h, h, conv_w, conv_b, wa, ba, wx, bx, lam)


def _bwd_in(dz_l, dz_c, w_in, x, g, dx1):
    S, D = x.shape
    nb, _, C = w_in.shape
    ts = _tile(S, 512)

    def body(dl_ref, dc_ref, w_ref, x_ref, g_ref, dx1_ref, dx_ref, dg_ref):
        i = pl.program_id(0)

        @pl.when(i == 0)
        def _():
            dg_ref[...] = jnp.zeros_like(dg_ref)

        dh = (_mm_nt(dl_ref[0], w_ref[0]) + _mm_nt(dl_ref[1], w_ref[1])
              + _mm_nt(dc_ref[0], w_ref[2]) + _mm_nt(dc_ref[1], w_ref[3]))
        rinv, xhat = _rms(x_ref[...])
        dg_ref[...] += _colsum(dh * xhat)
        dx_ref[...] = dx1_ref[...] + _rms_bwd(rinv, xhat, dh * g_ref[...])

    row = pl.BlockSpec((ts, D), lambda i: (i, 0))
    pair = pl.BlockSpec((2, ts, C), lambda i: (0, i, 0))
    vecd = pl.BlockSpec((1, D), lambda i: (0, 0))
    return pl.pallas_call(
        body, name="bwd_in", grid=(S // ts,),
        in_specs=[pair, pair, pl.BlockSpec((nb, D, C), lambda i: (0, 0, 0)), row, vecd, row],
        out_specs=[row, vecd],
        out_shape=[jax.ShapeDtypeStruct((S, D), F32), jax.ShapeDtypeStruct((1, D), F32)],
        compiler_params=_params("arbitrary"))(dz_l, dz_c, w_in, x, g, dx1)


def _wgrad(a, b, name):
    na, S, K = a.shape
    nb, _, N = b.shape
    nj = max(na, nb)
    assert min(na, nb) == 1
    ts = _tile(S, 1024)
    ns = S // ts

    def body(a_ref, b_ref, o_ref, acc):
        s = pl.program_id(1)
        part = _mm_tn(a_ref[0], b_ref[0])

        @pl.when(s == 0)
        def _():
            acc[...] = part

        @pl.when(s > 0)
        def _():
            acc[...] += part

        @pl.when(s == ns - 1)
        def _():
            o_ref[0] = acc[...].astype(o_ref.dtype)

    return pl.pallas_call(
        body, name=name, grid=(nj, ns),
        in_specs=[pl.BlockSpec((1, ts, K), (lambda j, s: (j, s, 0)) if na > 1 else (lambda j, s: (0, s, 0))),
                  pl.BlockSpec((1, ts, N), (lambda j, s: (j, s, 0)) if nb > 1 else (lambda j, s: (0, s, 0)))],
        out_specs=pl.BlockSpec((1, K, N), lambda j, s: (j, 0, 0)),
        out_shape=jax.ShapeDtypeStruct((nj, K, N), WIRE_DTYPE),
        scratch_shapes=[pltpu.VMEM((K, N), F32)],
        compiler_params=_params("parallel", "arbitrary"))(a, b)


def _place():
    x, y, c = lax.axis_index("x"), lax.axis_index("y"), lax.axis_index("c")
    other_chips = [(1 - x, y), (x, 1 - y), (1 - x, 1 - y)]
    return x, y, c, other_chips


ANY = pl.BlockSpec(memory_space=pl.ANY)


def _gather_weights(shards):
    nt = len(shards)

    def body(*refs):
        src, dst = refs[:nt], refs[nt:2 * nt]
        ici_send, ici_recv, d2d_send, d2d_recv, own_send, own_recv = refs[2 * nt:]
        x, y, c, chips = _place()
        mine = 2 * x + y

        def half(t, pc):
            hr = src[t].shape[0] // 2
            return pl.ds(pc * hr, hr)

        def own(t):
            return pltpu.make_async_remote_copy(
                src_ref=src[t], dst_ref=dst[t].at[mine], send_sem=own_send.at[t], recv_sem=own_recv.at[t],
                device_id=(x, y, 1 - c), device_id_type=MESH)

        def ici(t, k, block, to):
            cx, cy = block
            ref = dst[t].at[2 * cx + cy, half(t, c)]
            return pltpu.make_async_remote_copy(
                src_ref=src[t].at[half(t, c)] if to is not None else ref, dst_ref=ref,
                send_sem=ici_send.at[t, k], recv_sem=ici_recv.at[t, k],
                device_id=(*to, c) if to is not None else (x, y, c), device_id_type=MESH)

        def d2d(t, k, block, pc):
            cx, cy = block
            ref = dst[t].at[2 * cx + cy, half(t, pc)]
            return pltpu.make_async_remote_copy(
                src_ref=ref, dst_ref=ref, send_sem=d2d_send.at[t, k], recv_sem=d2d_recv.at[t, k],
                device_id=(x, y, 1 - c), device_id_type=MESH)

        sends = [ici(t, k, (x, y), chip) for t in range(nt) for k, chip in enumerate(chips)]
        sends += [own(t) for t in range(nt)]
        for cp in sends:
            cp.start()
        passed = []
        for t in range(nt):
            for k, chip in enumerate(chips):
                ici(t, k, chip, None).wait_recv()
                fw = d2d(t, k, chip, c)
                fw.start()
                passed.append(fw)
        for t in range(nt):
            own(t).wait_recv()
            for k, chip in enumerate(chips):
                d2d(t, k, chip, 1 - c).wait_recv()
        for cp in sends + passed:
            cp.wait_send()

    return pl.pallas_call(
        body, name="gather_weights",
        in_specs=[ANY] * nt, out_specs=[ANY] * nt,
        out_shape=[jax.ShapeDtypeStruct((N_CHIPS,) + s.shape, s.dtype) for s in shards],
        scratch_shapes=[pltpu.SemaphoreType.DMA((nt, 3))] * 4 + [pltpu.SemaphoreType.DMA((nt,))] * 2,
        compiler_params=pltpu.CompilerParams(has_side_effects=True))(*shards)


def _exchange_halves(grads):
    nt = len(grads)

    def body(*refs):
        src, dst = refs[:nt], refs[nt:2 * nt]
        send, recv = refs[2 * nt:]
        x, y, c, _ = _place()
        copies = []
        for t in range(nt):
            hr = src[t].shape[1] // 2
            copies.append(pltpu.make_async_remote_copy(
                src_ref=src[t].at[:, pl.ds((1 - c) * hr, hr)], dst_ref=dst[t],
                send_sem=send.at[t], recv_sem=recv.at[t], device_id=(x, y, 1 - c), device_id_type=MESH))
        for cp in copies:
            cp.start()
        for cp in copies:
            cp.wait()

    return pl.pallas_call(
        body, name="rs_exchange_halves",
        in_specs=[ANY] * nt, out_specs=[ANY] * nt,
        out_shape=[jax.ShapeDtypeStruct((g.shape[0], g.shape[1] // 2, g.shape[2]), g.dtype) for g in grads],
        scratch_shapes=[pltpu.SemaphoreType.DMA((nt,))] * 2,
        compiler_params=pltpu.CompilerParams(has_side_effects=True))(*grads)


def _add_halves(grad, other, name):
    nb, R, C = grad.shape
    hr = R // 2
    tr = _tile(hr, 256, 16)
    steps = hr // tr
    c = lax.axis_index("c").astype(jnp.int32).reshape((1,))

    def body(c_ref, a_ref, b_ref, o_ref):
        o_ref[...] = (a_ref[...].astype(F32) + b_ref[...].astype(F32)).astype(o_ref.dtype)

    return pl.pallas_call(
        body, name=name,
        grid_spec=pltpu.PrefetchScalarGridSpec(
            num_scalar_prefetch=1, grid=(nb, steps),
            in_specs=[pl.BlockSpec((1, tr, C), lambda j, i, c_ref: (j, c_ref[0] * steps + i, 0)),
                      pl.BlockSpec((1, tr, C), lambda j, i, c_ref: (j, i, 0))],
            out_specs=pl.BlockSpec((1, tr, C), lambda j, i, c_ref: (j, i, 0))),
        out_shape=jax.ShapeDtypeStruct((nb, hr, C), grad.dtype),
        compiler_params=_params("parallel", "parallel"))(c, grad, other)


def _scatter_chip_sums(parts):
    nt = len(parts)

    def body(*refs):
        src, dst = refs[:nt], refs[nt:2 * nt]
        send, recv = refs[2 * nt:]
        x, y, c, chips = _place()
        copies = []
        for t in range(nt):
            for k, (cx, cy) in enumerate(chips):
                copies.append(pltpu.make_async_remote_copy(
                    src_ref=src[t].at[2 * cx + cy], dst_ref=dst[t].at[k],
                    send_sem=send.at[t, k], recv_sem=recv.at[t, k], device_id=(cx, cy, c), device_id_type=MESH))
        for cp in copies:
            cp.start()
        for cp in copies:
            cp.wait()

    return pl.pallas_call(
        body, name="rs_scatter_chip_sums",
        in_specs=[ANY] * nt, out_specs=[ANY] * nt,
        out_shape=[jax.ShapeDtypeStruct((3,) + p.shape[1:], p.dtype) for p in parts],
        scratch_shapes=[pltpu.SemaphoreType.DMA((nt, 3))] * 2,
        compiler_params=pltpu.CompilerParams(has_side_effects=True))(*parts)


def _sum_chips(part, recv, name):
    _, hr, C = part.shape
    tr = _tile(hr, 256, 16)
    steps = hr // tr
    where = jnp.stack([2 * lax.axis_index("x") + lax.axis_index("y"), lax.axis_index("c")]).astype(jnp.int32)

    def body(w_ref, a_ref, b_ref, o_ref):
        acc = a_ref[0].astype(F32)
        for k in range(3):
            acc = acc + b_ref[k].astype(F32)
        o_ref[...] = acc

    return pl.pallas_call(
        body, name=name,
        grid_spec=pltpu.PrefetchScalarGridSpec(
            num_scalar_prefetch=1, grid=(steps,),
            in_specs=[pl.BlockSpec((1, tr, C), lambda i, w_ref: (w_ref[0], i, 0)),
                      pl.BlockSpec((3, tr, C), lambda i, w_ref: (0, i, 0))],
            out_specs=pl.BlockSpec((tr, C), lambda i, w_ref: (w_ref[1] * steps + i, 0))),
        out_shape=jax.ShapeDtypeStruct((2 * hr, C), F32),
        compiler_params=_params("parallel"))(where, part, recv)


def _join_halves(bufs):
    nt = len(bufs)

    def body(*refs):
        dst = refs[nt:2 * nt]
        send, recv = refs[2 * nt:]
        x, y, c, _ = _place()

        def swap(t, pc):
            hr = dst[t].shape[0] // 2
            rows = dst[t].at[pl.ds(pc * hr, hr)]
            return pltpu.make_async_remote_copy(src_ref=rows, dst_ref=rows, send_sem=send.at[t], recv_sem=recv.at[t],
                                                device_id=(x, y, 1 - c), device_id_type=MESH)

        for t in range(nt):
            swap(t, c).start()
        for t in range(nt):
            swap(t, 1 - c).wait_recv()
        for t in range(nt):
            swap(t, c).wait_send()

    return pl.pallas_call(
        body, name="rs_join_halves",
        in_specs=[ANY] * nt, out_specs=[ANY] * nt,
        out_shape=[jax.ShapeDtypeStruct(b.shape, b.dtype) for b in bufs],
        input_output_aliases={t: t for t in range(nt)},
        scratch_shapes=[pltpu.SemaphoreType.DMA((nt,))] * 2,
        compiler_params=pltpu.CompilerParams(has_side_effects=True))(*bufs)


def _all_reduce_rows(buf, name, loss_row=None):
    R, L = buf.shape

    def body(in_ref, *rest):
        if loss_row is None:
            out_ref, gath, send, recv = rest
        else:
            out_ref, loss_ref, gath, send, recv = rest
        x, y, c, _ = _place()
        me = 4 * x + 2 * y + c
        gath[0] = in_ref[...]
        copies = []
        for k in range(1, N_DEV):
            peer = (x ^ ((k >> 2) & 1), y ^ ((k >> 1) & 1), c ^ (k & 1))
            copies.append(pltpu.make_async_remote_copy(
                src_ref=in_ref, dst_ref=gath.at[k], send_sem=send.at[k - 1], recv_sem=recv.at[k - 1],
                device_id=peer, device_id_type=MESH))
        for cp in copies:
            cp.start()
        for cp in copies:
            cp.wait()
        total = gath[me]
        for d in range(1, N_DEV):
            total = total + gath[d ^ me]
        out_ref[...] = total
        if loss_row is not None:
            loss_ref[...] = jnp.sum(total[loss_row:loss_row + 1, :], axis=1, keepdims=True)

    vm = pl.BlockSpec(memory_space=pltpu.VMEM)
    out_shape = [jax.ShapeDtypeStruct((R, L), F32)]
    if loss_row is not None:
        out_shape.append(jax.ShapeDtypeStruct((1, 1), F32))
    return pl.pallas_call(
        body, name=name, in_specs=[vm], out_specs=[vm] * len(out_shape), out_shape=out_shape,
        scratch_shapes=[pltpu.VMEM((N_DEV, R, L), F32), pltpu.SemaphoreType.DMA((N_DEV - 1,)),
                        pltpu.SemaphoreType.DMA((N_DEV - 1,))],
        compiler_params=pltpu.CompilerParams(has_side_effects=True, vmem_limit_bytes=VMEM_LIMIT_BYTES))(buf)


def _adamw(w, g, m, v, name):
    R, C = w.shape
    tr = _tile(R, 256)
    c1 = 1.0 - ADAM_B1 ** ADAM_STEP
    c2 = 1.0 - ADAM_B2 ** ADAM_STEP

    def body(w_ref, g_ref, m_ref, v_ref, d_ref, nm_ref, nv_ref):
        gv = g_ref[...]
        nm = ADAM_B1 * m_ref[...] + (1.0 - ADAM_B1) * gv
        nv = ADAM_B2 * v_ref[...] + (1.0 - ADAM_B2) * (gv * gv)
        nm_ref[...] = nm
        nv_ref[...] = nv
        d_ref[...] = -ADAM_LR * ((nm / c1) / (jnp.sqrt(nv / c2) + ADAM_EPS) + ADAM_WD * w_ref[...])

    blk = pl.BlockSpec((tr, C), lambda i: (i, 0))
    return pl.pallas_call(
        body, name=name, grid=(R // tr,), in_specs=[blk] * 4, out_specs=[blk] * 3,
        out_shape=[jax.ShapeDtypeStruct((R, C), F32)] * 3,
        compiler_params=_params("parallel"))(w, g, m, v)


def _pack_rows(arrays):
    rows = []
    for a in arrays:
        flat = a.reshape(-1).astype(F32)
        pad = (-flat.shape[0]) % LANES
        rows.append(jnp.pad(flat, (0, pad)).reshape(-1, LANES))
    buf = jnp.concatenate(rows, axis=0)
    return jnp.pad(buf, ((0, (-buf.shape[0]) % 8), (0, 0)))


def _unpack_rows(buf, shapes):
    out, r = [], 0
    for s in shapes:
        n = math.prod(s)
        nr = -(-n // LANES)
        out.append(buf[r:r + nr].reshape(-1)[:n].reshape(s))
        r += nr
    return out


def _block_diag(w):
    H, a, b = w.shape
    eye = jnp.eye(H, dtype=w.dtype)
    return (eye[:, None, :, None] * w[:, :, None, :]).reshape(H * a, H * b)


def _block_diag_parts(d, H):
    a, b = d.shape[0] // H, d.shape[1] // H
    d4 = d.reshape(H, a, H, b)
    return jnp.stack([d4[h, :, h, :] for h in range(H)])


def _local_step(x, mem, target, wf, small):
    D = x.shape[1]
    p = small
    wa_d = _block_diag(p['lru_w_a']).astype(MXU_DTYPE)
    wx_d = _block_diag(p['lru_w_x']).astype(MXU_DTYPE)
    heads = p['lru_w_a'].shape[0]
    w_out2 = wf['w_out'].reshape(2, -1, D)
    w_q = wf['w_q'].reshape(D, D)
    w_o = wf['w_o'].reshape(D, D)
    n_up = wf['w_up'].shape[0]
    w_down2 = wf['w_down'].reshape(n_up // 2, -1, D)

    z, h1 = _fwd_in(x, p['mix_norm_g'], wf['w_in'])
    h, y_lru = _lru_fwd(z, p['lru_conv_w'], p['lru_conv_b'], wa_d, p['lru_b_a'], wx_d, p['lru_b_x'],
                        p['lru_lambda'])
    c1, c3 = _conf_fwd(z, p['conf_conv_w'], p['conf_conv_b'], p['conf_ln_g'], p['conf_ln_b'])
    x1, h2, q = _fwd_out_q(x, y_lru, c3, w_out2, p['xa_norm_g'], w_q)
    m, kv = _kv_fwd(mem, p['mem_norm_g'], wf['w_kv'])
    o, x2, h3 = _attn_fwd(q, kv, x1, w_o, p['ffn_norm_g'])
    gu = _fwd_up(h3, wf['w_up'])
    act, dx3, loss_lanes, d_final_g = _ffn_down_loss(gu, p['ffn_conv_w'], p['ffn_conv_b'], w_down2, x2,
                                                     p['final_norm_g'], target)

    dgu, d_ffn_cw, d_ffn_cb = _bwd_down(dx3, w_down2, gu, p['ffn_conv_w'], p['ffn_conv_b'])
    g_down = _wgrad(act, dx3[None], "wgrad_down")
    dx2, d_ffn_g = _bwd_up(dgu, wf['w_up'], x2, p['ffn_norm_g'], dx3)
    g_up = _wgrad(h3[None], dgu, "wgrad_up")
    dq, dx1, dkv, d_xa_g = _attn_bwd(dx2, w_o, q, kv, x1, p['xa_norm_g'], w_q)
    g_o = _wgrad(o[None], dx2[None], "wgrad_o")
    g_q = _wgrad(h2[None], dq[None], "wgrad_q")
    g_kv, d_mem_g = _kv_bwd(dkv, wf['w_kv'], mem, p['mem_norm_g'], m)
    dz_c, d_conf_cw, d_conf_cb, d_ln_g, d_ln_b = _conf_bwd(dx1, w_out2[1], z, c1, p['conf_conv_w'],
                                                           p['conf_ln_g'], p['conf_ln_b'])
    (dz_l, d_wa, d_wx, d_ba, d_bx, d_lam, d_lru_cw, d_lru_cb) = _lru_bwd(
        dx1, w_out2[0], z, h, p['lru_conv_w'], p['lru_conv_b'], wa_d, p['lru_b_a'], wx_d, p['lru_b_x'],
        p['lru_lambda'])
    g_out = jnp.concatenate([_wgrad(y_lru[None], dx1[None], "wgrad_out_lru"),
                             _wgrad(c3[None], dx1[None], "wgrad_out_conf")], axis=0)
    grad_x, d_mix_g = _bwd_in(dz_l, dz_c, wf['w_in'], x, p['mix_norm_g'], dx1)
    g_in = _wgrad(h1[None], jnp.concatenate([dz_l, dz_c], axis=0), "wgrad_in")

    nch = N_CHIPS
    big = {'w_in': g_in, 'w_kv': g_kv, 'w_up': g_up,
           'w_out': g_out.reshape(nch, -1, D), 'w_q': g_q.reshape(nch, -1, D), 'w_o': g_o.reshape(nch, -1, D),
           'w_down': g_down.reshape(nch, -1, D)}
    small_g = {'mix_norm_g': d_mix_g, 'lru_conv_w': d_lru_cw, 'lru_conv_b': d_lru_cb,
               'lru_w_a': _block_diag_parts(d_wa, heads), 'lru_b_a': d_ba,
               'lru_w_x': _block_diag_parts(d_wx, heads), 'lru_b_x': d_bx, 'lru_lambda': d_lam,
               'conf_conv_w': d_conf_cw, 'conf_conv_b': d_conf_cb, 'conf_ln_g': d_ln_g, 'conf_ln_b': d_ln_b,
               'xa_norm_g': d_xa_g, 'mem_norm_g': d_mem_g, 'ffn_norm_g': d_ffn_g,
               'ffn_conv_w': d_ffn_cw, 'ffn_conv_b': d_ffn_cb, 'final_norm_g': d_final_g}
    return grad_x, big, small_g, loss_lanes


def _reduce_scatter(big_grads):
    names = list(big_grads)
    grads = [big_grads[n] for n in names]
    others = _exchange_halves(grads)
    parts = [_add_halves(g, o, "rs_add_halves_" + n) for n, g, o in zip(names, grads, others)]
    recvs = _scatter_chip_sums(parts)
    halves = [_sum_chips(p, r, "rs_sum_chips_" + n) for n, p, r in zip(names, parts, recvs)]
    return dict(zip(names, _join_halves(halves)))


def kernel(x, mem, mix_norm_g, w_in, lru_conv_w, lru_conv_b, lru_w_a, lru_b_a, lru_w_x, lru_b_x, lru_lambda, conf_conv_w, conf_conv_b, conf_ln_g, conf_ln_b, w_out, xa_norm_g, mem_norm_g, w_q, w_kv, w_o, ffn_norm_g, w_up, ffn_conv_w, ffn_conv_b, w_down, final_norm_g, loss_target, m_mix_norm_g, m_w_in, m_lru_conv_w, m_lru_conv_b, m_lru_w_a, m_lru_b_a, m_lru_w_x, m_lru_b_x, m_lru_lambda, m_conf_conv_w, m_conf_conv_b, m_conf_ln_g, m_conf_ln_b, m_w_out, m_xa_norm_g, m_mem_norm_g, m_w_q, m_w_kv, m_w_o, m_ffn_norm_g, m_w_up, m_ffn_conv_w, m_ffn_conv_b, m_w_down, m_final_norm_g, v_mix_norm_g, v_w_in, v_lru_conv_w, v_lru_conv_b, v_lru_w_a, v_lru_b_a, v_lru_w_x, v_lru_b_x, v_lru_lambda, v_conf_conv_w, v_conf_conv_b, v_conf_ln_g, v_conf_ln_b, v_w_out, v_xa_norm_g, v_mem_norm_g, v_w_q, v_w_kv, v_w_o, v_ffn_norm_g, v_w_up, v_ffn_conv_w, v_ffn_conv_b, v_w_down, v_final_norm_g):
    given = dict(locals())
    w = {n: given[n] for n in WEIGHTS}
    mom = {n: given["m_" + n] for n in WEIGHTS}
    var = {n: given["v_" + n] for n in WEIGHTS}
    xi, yi, ci = lax.axis_index("x"), lax.axis_index("y"), lax.axis_index("c")
    chip = 2 * xi + yi

    shards = [w[n][0].astype(WIRE_DTYPE) for n in BIG]
    wf = dict(zip(BIG, _gather_weights(shards)))
    tap_full = []
    for n in COL_SHARDED_SMALL:
        s = w[n][0]
        full = jnp.zeros((s.shape[0], N_CHIPS * s.shape[1]), F32)
        s = jnp.where(ci == 0, s, jnp.zeros_like(s))
        tap_full.append(lax.dynamic_update_slice(full, s, (0, chip * s.shape[1])))
    tap_shapes = [t.shape for t in tap_full]
    taps = _unpack_rows(_all_reduce_rows(_pack_rows(tap_full), "gather_conv_taps")[0], tap_shapes)
    small = {n: (w[n] if w[n].ndim == 1 else w[n][0]) for n in SMALL}
    small.update(dict(zip(COL_SHARDED_SMALL, taps)))
    small = {n: (a.reshape(1, -1) if a.ndim == 1 else a) for n, a in small.items()}

    grad_x, big_g, small_g, loss_lanes = _local_step(x[0], mem[0], loss_target[0], wf, small)

    big_g = _reduce_scatter(big_g)
    small_names = list(small_g)
    small_shapes = [small_g[n].shape for n in small_names]
    packed = _pack_rows([loss_lanes] + [small_g[n] for n in small_names])
    summed, loss = _all_reduce_rows(packed, "all_reduce_small", loss_row=0)
    small_sum = dict(zip(small_names, _unpack_rows(summed, [loss_lanes.shape] + small_shapes)[1:]))

    grads = {}
    for n in WEIGHTS:
        if n in BIG:
            g = big_g[n]
        elif n in COL_SHARDED_SMALL:
            width = w[n].shape[-1]
            g = lax.dynamic_slice_in_dim(small_sum[n], chip * width, width, axis=1)
        else:
            g = small_sum[n]
        grads[n] = g.reshape(w[n].shape)

    delta, new_m, new_v = {}, {}, {}
    for n in BIG:
        d, nm, nv = _adamw(w[n][0], grads[n][0], mom[n][0], var[n][0], "adamw_" + n)
        delta[n], new_m[n], new_v[n] = d[None], nm[None], nv[None]
    shapes = [w[n].shape for n in SMALL]
    d, nm, nv = _adamw(_pack_rows([w[n] for n in SMALL]), _pack_rows([grads[n] for n in SMALL]),
                       _pack_rows([mom[n] for n in SMALL]), _pack_rows([var[n] for n in SMALL]), "adamw_small")
    for out, buf in ((delta, d), (new_m, nm), (new_v, nv)):
        out.update(dict(zip(SMALL, _unpack_rows(buf, shapes))))

    return (loss[0, 0], grad_x[None], *[grads[n] for n in WEIGHTS], *[delta[n] for n in WEIGHTS],
            *[new_m[n] for n in WEIGHTS], *[new_v[n] for n in WEIGHTS])
```

```python
import math

import jax
import jax.numpy as jnp
from jax import lax
from jax.experimental import pallas as pl
from jax.experimental.pallas import tpu as pltpu

F32 = jnp.float32
MXU_DTYPE = jnp.bfloat16
WIRE_DTYPE = jnp.bfloat16
EPS = 1e-6
RG_C = 8.0
XA_HEADS = 4
ADAM_LR, ADAM_B1, ADAM_B2, ADAM_EPS, ADAM_WD, ADAM_STEP = 0.001, 0.9, 0.999, 1e-08, 0.01, 10
VMEM_LIMIT_BYTES = 52 * 1024 * 1024
LANES = 1024
N_CHIPS = 4
N_DEV = 8
MESH = pl.DeviceIdType.MESH
GELU_C = math.sqrt(2.0 / math.pi)
GELU_K = 0.044715

WEIGHTS = ['mix_norm_g', 'w_in', 'lru_conv_w', 'lru_conv_b', 'lru_w_a', 'lru_b_a', 'lru_w_x', 'lru_b_x',
           'lru_lambda', 'conf_conv_w', 'conf_conv_b', 'conf_ln_g', 'conf_ln_b', 'w_out', 'xa_norm_g',
           'mem_norm_g', 'w_q', 'w_kv', 'w_o', 'ffn_norm_g', 'w_up', 'ffn_conv_w', 'ffn_conv_b', 'w_down',
           'final_norm_g']
BIG = ['w_in', 'w_kv', 'w_up', 'w_out', 'w_q', 'w_o', 'w_down']
SMALL = [n for n in WEIGHTS if n not in BIG]
COL_SHARDED_SMALL = ['lru_conv_w', 'conf_conv_w', 'ffn_conv_w']


def _params(*semantics):
    return pltpu.CompilerParams(dimension_semantics=semantics, vmem_limit_bytes=VMEM_LIMIT_BYTES)


ANY = pl.BlockSpec(memory_space=pl.ANY)
WHOLE_VMEM = pl.BlockSpec(memory_space=pltpu.VMEM)


class _Comm:
    def __init__(self, arrays, out_shapes, scratch, start, finish, aliases=None, in_specs=None, out_specs=None):
        self.arrays, self.out_shapes, self.scratch = list(arrays), list(out_shapes), list(scratch)
        self.start, self.finish = start, finish
        self.aliases = dict(aliases or {})
        self.in_specs = list(in_specs) if in_specs is not None else [ANY] * len(self.arrays)
        self.out_specs = list(out_specs) if out_specs is not None else [ANY] * len(self.out_shapes)


def _merge(*comms):
    comms = [c for c in comms if c is not None]
    if not comms:
        return None
    ai = [0]
    for c in comms:
        ai.append(ai[-1] + len(c.arrays))
    oi = [0]
    for c in comms:
        oi.append(oi[-1] + len(c.out_shapes))
    si = [0]
    for c in comms:
        si.append(si[-1] + len(c.scratch))

    def each(which):
        def run(ins, outs, scr):
            for k, c in enumerate(comms):
                getattr(c, which)(ins[ai[k]:ai[k + 1]], outs[oi[k]:oi[k + 1]], scr[si[k]:si[k + 1]])
        return run

    aliases = {ai[k] + i: oi[k] + o for k, c in enumerate(comms) for i, o in c.aliases.items()}
    return _Comm(sum((c.arrays for c in comms), []), sum((c.out_shapes for c in comms), []),
                 sum((c.scratch for c in comms), []), each("start"), each("finish"), aliases,
                 sum((c.in_specs for c in comms), []), sum((c.out_specs for c in comms), []))


def _split(outs, *comms):
    parts, at = [], 0
    for c in comms:
        parts.append(outs[at:at + len(c.out_shapes)])
        at += len(c.out_shapes)
    return parts


def _pcall(comm, body, *, name, grid, in_specs, out_specs, out_shape, semantics, scratch_shapes=()):
    single = not isinstance(out_shape, (list, tuple))
    out_shape = [out_shape] if single else list(out_shape)
    out_specs = [out_specs] if single else list(out_specs)
    in_specs, scratch_shapes = list(in_specs), list(scratch_shapes)

    if comm is None:
        def plain(*args):
            return list(pl.pallas_call(body, name=name, grid=grid, in_specs=in_specs, out_specs=out_specs,
                                       out_shape=out_shape, scratch_shapes=scratch_shapes,
                                       compiler_params=_params(*semantics))(*args))
        return plain

    def hosted(*args):
        n_in, n_out, n_scr = len(args), len(out_shape), len(scratch_shapes)
        c_in, c_out = len(comm.arrays), len(comm.out_shapes)

        def wrapped(*refs):
            ins, cins = refs[:n_in], refs[n_in:n_in + c_in]
            o0 = n_in + c_in
            outs, couts = refs[o0:o0 + n_out], refs[o0 + n_out:o0 + n_out + c_out]
            s0 = o0 + n_out + c_out
            scr, cscr = refs[s0:s0 + n_scr], refs[s0 + n_scr:]
            first = last = None
            for axis, size in enumerate(grid):
                at_start, at_end = pl.program_id(axis) == 0, pl.program_id(axis) == size - 1
                first = at_start if first is None else first & at_start
                last = at_end if last is None else last & at_end
            if first is None:
                comm.start(cins, couts, cscr)
                body(*ins, *outs, *scr)
                comm.finish(cins, couts, cscr)
                return
            pl.when(first)(lambda: comm.start(cins, couts, cscr))
            body(*ins, *outs, *scr)
            pl.when(last)(lambda: comm.finish(cins, couts, cscr))

        res = pl.pallas_call(
            wrapped, name=name, grid=grid, in_specs=in_specs + comm.in_specs, out_specs=out_specs + comm.out_specs,
            out_shape=out_shape + comm.out_shapes, scratch_shapes=scratch_shapes + comm.scratch,
            input_output_aliases={n_in + i: n_out + o for i, o in comm.aliases.items()},
            compiler_params=pltpu.CompilerParams(dimension_semantics=("arbitrary",) * len(grid),
                                                 vmem_limit_bytes=VMEM_LIMIT_BYTES, has_side_effects=True),
        )(*args, *comm.arrays)
        return list(res[:n_out]), list(res[n_out:])

    return hosted


def _run_comm(comm, name):
    return _pcall(comm, lambda: None, name=name, grid=(), in_specs=[], out_specs=[], out_shape=[], semantics=())()[1]


def _tile(n, want, align=8):
    if n <= want:
        return n
    for t in range(want - want % align, 0, -align):
        if n % t == 0:
            return t
    raise ValueError((n, want, align))


def _mm(a, b):
    return jnp.dot(a.astype(MXU_DTYPE), b.astype(MXU_DTYPE), preferred_element_type=F32)


def _mm_nt(a, b):
    return lax.dot_general(a.astype(MXU_DTYPE), b.astype(MXU_DTYPE), (((1,), (1,)), ((), ())),
                           preferred_element_type=F32)


def _mm_tn(a, b):
    return lax.dot_general(a.astype(MXU_DTYPE), b.astype(MXU_DTYPE), (((0,), (0,)), ((), ())),
                           preferred_element_type=F32)


def _sigmoid(v):
    return 1.0 / (1.0 + jnp.exp(-v))


def _gelu(v):
    v2 = v * v
    t = jnp.tanh(GELU_C * (v + GELU_K * v * v2))
    return 0.5 * v * (1.0 + t), 0.5 * (1.0 + t) + 0.5 * v * (1.0 - t * t) * GELU_C * (1.0 + 3.0 * GELU_K * v2)


def _softplus_neg(lam):
    e = jnp.exp(-jnp.abs(lam))
    u = 1.0 + e
    log1p_e = jnp.where(u == 1.0, e, jnp.log(u) * e / jnp.where(u == 1.0, 1.0, u - 1.0))
    return jnp.maximum(-lam, 0.0) + log1p_e


def _rms(xv):
    rinv = lax.rsqrt(jnp.mean(xv * xv, axis=-1, keepdims=True) + EPS)
    return rinv, xv * rinv


def _rms_bwd(rinv, xhat, dxhat):
    return rinv * (dxhat - xhat * jnp.mean(dxhat * xhat, axis=-1, keepdims=True))


def _colsum(v):
    return jnp.sum(v, axis=0, keepdims=True)


def _wrow(w_ref, k, wcols):
    return w_ref[pl.ds(k, 1), :] if wcols is None else w_ref[pl.ds(k, 1), wcols]


def _causal_taps(buf_ref, halo, w_ref, taps, rows, wcols=None):
    acc = None
    for s in range(taps):
        term = _wrow(w_ref, taps - 1 - s, wcols) * buf_ref[pl.ds(halo - s, rows), :]
        acc = term if acc is None else acc + term
    return acc


def _anticausal_taps(buf_ref, w_ref, taps, rows, wcols=None):
    acc = None
    for s in range(taps):
        term = _wrow(w_ref, taps - 1 - s, wcols) * buf_ref[pl.ds(s, rows), :]
        acc = term if acc is None else acc + term
    return acc


def _tap_grads(dw_ref, dy, buf_ref, halo, taps, rows, wcols=None):
    for s in range(taps):
        g = _colsum(dy * buf_ref[pl.ds(halo - s, rows), :])
        if wcols is None:
            dw_ref[pl.ds(taps - 1 - s, 1), :] += g
        else:
            dw_ref[pl.ds(taps - 1 - s, 1), wcols] += g


def _fwd_in(x, g, w_in, comm=None):
    S, D = x.shape
    nb, _, C = w_in.shape
    ts = _tile(S, 512)

    def body(x_ref, g_ref, w_ref, z_ref, h_ref):
        _, xhat = _rms(x_ref[...])
        h = (xhat * g_ref[...]).astype(MXU_DTYPE)
        h_ref[...] = h
        for j in range(nb):
            z_ref[j] = jnp.dot(h, w_ref[j], preferred_element_type=F32)

    return _pcall(
        comm, body, name="fwd_in", grid=(S // ts,),
        in_specs=[pl.BlockSpec((ts, D), lambda i: (i, 0)), pl.BlockSpec((1, D), lambda i: (0, 0)),
                  pl.BlockSpec((nb, D, C), lambda i: (0, 0, 0))],
        out_specs=[pl.BlockSpec((nb, ts, C), lambda i: (0, i, 0)), pl.BlockSpec((ts, D), lambda i: (i, 0))],
        out_shape=[jax.ShapeDtypeStruct((nb, S, C), F32), jax.ShapeDtypeStruct((S, D), MXU_DTYPE)],
        semantics=("parallel",))(x, g, w_in)


def _lru_gates(xc, wa_ref, ba_ref, wx_ref, bx_ref, sp):
    xb = xc.astype(MXU_DTYPE)
    r = _sigmoid(jnp.dot(xb, wa_ref[...], preferred_element_type=F32) + ba_ref[...])
    ig = _sigmoid(jnp.dot(xb, wx_ref[...], preferred_element_type=F32) + bx_ref[...])
    log_a = -RG_C * r * sp
    a = jnp.exp(log_a)
    mult = jnp.sqrt(jnp.tanh(-log_a) * (a * a + 1.0))
    return r, ig, a, mult


def _lru_fwd(z, conv_w, conv_b, wa, ba, wx, bx, lam, comm=None):
    _, S, C = z.shape
    ts = _tile(S, 256)
    taps = conv_w.shape[0]
    halo = 8

    def body(zx_ref, zg_ref, cw_ref, cb_ref, wa_ref, ba_ref, wx_ref, bx_ref, lam_ref,
             h_ref, y_ref, xbuf, a_s, u_s, hc):
        i = pl.program_id(0)

        @pl.when(i == 0)
        def _():
            xbuf[pl.ds(0, halo), :] = jnp.zeros((halo, C), F32)
            hc[...] = jnp.zeros_like(hc)

        xbuf[pl.ds(halo, ts), :] = zx_ref[0]
        xc = _causal_taps(xbuf, halo, cw_ref, taps, ts) + cb_ref[...]
        sp = _softplus_neg(lam_ref[...])
        _, ig, a, mult = _lru_gates(xc, wa_ref, ba_ref, wx_ref, bx_ref, sp)
        a_s[...] = a
        u_s[...] = mult * (ig * xc)
        row = lax.broadcasted_iota(jnp.int32, (8, C), 0)

        def step(k, carry):
            off = pl.multiple_of(k * 8, 8)
            av = a_s[pl.ds(off, 8), :]
            uv = u_s[pl.ds(off, 8), :]
            for d in (1, 2, 4):
                m = row >= d
                a_sh = jnp.where(m, pltpu.roll(av, d, 0), 1.0)
                u_sh = jnp.where(m, pltpu.roll(uv, d, 0), 0.0)
                uv = uv + av * u_sh
                av = av * a_sh
            hv = uv + av * carry
            h_ref[pl.ds(off, 8), :] = hv
            return jnp.broadcast_to(hv[7:8, :], (8, C))

        hc[...] = lax.fori_loop(0, ts // 8, step, hc[...])
        ge, _ = _gelu(zg_ref[0])
        y_ref[...] = (h_ref[...] * ge).astype(MXU_DTYPE)
        xbuf[pl.ds(0, halo), :] = xbuf[pl.ds(ts, halo), :]

    vec = pl.BlockSpec((1, C), lambda i: (0, 0))
    mat = pl.BlockSpec((C, C), lambda i: (0, 0))
    return _pcall(
        comm, body, name="lru_fwd", grid=(S // ts,),
        in_specs=[pl.BlockSpec((1, ts, C), lambda i: (0, i, 0)), pl.BlockSpec((1, ts, C), lambda i: (1, i, 0)),
                  pl.BlockSpec((taps, C), lambda i: (0, 0)), vec, mat, vec, mat, vec, vec],
        out_specs=[pl.BlockSpec((ts, C), lambda i: (i, 0)), pl.BlockSpec((ts, C), lambda i: (i, 0))],
        out_shape=[jax.ShapeDtypeStruct((S, C), F32), jax.ShapeDtypeStruct((S, C), MXU_DTYPE)],
        scratch_shapes=[pltpu.VMEM((ts + halo, C), F32), pltpu.VMEM((ts, C), F32), pltpu.VMEM((ts, C), F32),
                        pltpu.VMEM((8, C), F32)],
        semantics=("arbitrary",))(z, z, conv_w, conv_b, wa, ba, wx, bx, lam)


def _layer_norm_stats(c1):
    mu = jnp.mean(c1, axis=-1, keepdims=True)
    xc = c1 - mu
    rstd = lax.rsqrt(jnp.mean(xc * xc, axis=-1, keepdims=True) + EPS)
    return rstd, xc * rstd


def _conf_fwd(z, conv_w, conv_b, ln_g, ln_b, comm=None):
    _, S, C = z.shape
    ts = _tile(S, 256)
    taps = conv_w.shape[0]
    halo = 32

    def body(za_ref, zb_ref, cw_ref, cb_ref, g_ref, b_ref, c1_ref, c3_ref, cbuf):
        i = pl.program_id(0)

        @pl.when(i == 0)
        def _():
            cbuf[pl.ds(0, halo), :] = jnp.zeros((halo, C), F32)

        cbuf[pl.ds(halo, ts), :] = za_ref[0] * _sigmoid(zb_ref[0])
        c1 = _causal_taps(cbuf, halo, cw_ref, taps, ts) + cb_ref[...]
        c1_ref[...] = c1
        _, xhat = _layer_norm_stats(c1)
        c2 = xhat * g_ref[...] + b_ref[...]
        c3_ref[...] = (c2 * _sigmoid(c2)).astype(MXU_DTYPE)
        cbuf[pl.ds(0, halo), :] = cbuf[pl.ds(ts, halo), :]

    vec = pl.BlockSpec((1, C), lambda i: (0, 0))
    return _pcall(
        comm, body, name="conf_fwd", grid=(S // ts,),
        in_specs=[pl.BlockSpec((1, ts, C), lambda i: (2, i, 0)), pl.BlockSpec((1, ts, C), lambda i: (3, i, 0)),
                  pl.BlockSpec((taps, C), lambda i: (0, 0)), vec, vec, vec],
        out_specs=[pl.BlockSpec((ts, C), lambda i: (i, 0)), pl.BlockSpec((ts, C), lambda i: (i, 0))],
        out_shape=[jax.ShapeDtypeStruct((S, C), F32), jax.ShapeDtypeStruct((S, C), MXU_DTYPE)],
        scratch_shapes=[pltpu.VMEM((ts + halo, C), F32)],
        semantics=("arbitrary",))(z, z, conv_w, conv_b, ln_g, ln_b)


def _fwd_out_q(x, y_lru, c3, w_out, g_xa, w_q, comm=None):
    S, D = x.shape
    C = y_lru.shape[1]
    ts = _tile(S, 512)

    def body(x_ref, yl_ref, c3_ref, wo_ref, g_ref, wq_ref, x1_ref, h2_ref, q_ref):
        x1 = (x_ref[...] + jnp.dot(yl_ref[...], wo_ref[0], preferred_element_type=F32)
              + jnp.dot(c3_ref[...], wo_ref[1], preferred_element_type=F32))
        x1_ref[...] = x1
        _, xhat = _rms(x1)
        h2 = (xhat * g_ref[...]).astype(MXU_DTYPE)
        h2_ref[...] = h2
        q_ref[...] = jnp.dot(h2, wq_ref[...], preferred_element_type=F32).astype(MXU_DTYPE)

    row = lambda w: pl.BlockSpec((ts, w), lambda i: (i, 0))
    return _pcall(
        comm, body, name="fwd_out_q", grid=(S // ts,),
        in_specs=[row(D), row(C), row(C), pl.BlockSpec((2, C, D), lambda i: (0, 0, 0)),
                  pl.BlockSpec((1, D), lambda i: (0, 0)), pl.BlockSpec((D, D), lambda i: (0, 0))],
        out_specs=[row(D), row(D), row(D)],
        out_shape=[jax.ShapeDtypeStruct((S, D), F32), jax.ShapeDtypeStruct((S, D), MXU_DTYPE),
                   jax.ShapeDtypeStruct((S, D), MXU_DTYPE)],
        semantics=("parallel",))(x, y_lru, c3, w_out, g_xa, w_q)


def _kv_fwd(mem, g, w_kv):
    M, D = mem.shape
    nb, _, C = w_kv.shape

    def body(mem_ref, g_ref, w_ref, m_ref, kv_ref):
        _, xhat = _rms(mem_ref[...])
        m = (xhat * g_ref[...]).astype(MXU_DTYPE)
        m_ref[...] = m
        for j in range(nb):
            kv_ref[:, pl.ds(j * C, C)] = jnp.dot(m, w_ref[j], preferred_element_type=F32).astype(MXU_DTYPE)

    return pl.pallas_call(
        body, name="kv_fwd", grid=(1,),
        in_specs=[pl.BlockSpec((M, D), lambda i: (0, 0)), pl.BlockSpec((1, D), lambda i: (0, 0)),
                  pl.BlockSpec((nb, D, C), lambda i: (0, 0, 0))],
        out_specs=[pl.BlockSpec((M, D), lambda i: (0, 0)), pl.BlockSpec((M, nb * C), lambda i: (0, 0))],
        out_shape=[jax.ShapeDtypeStruct((M, D), MXU_DTYPE), jax.ShapeDtypeStruct((M, nb * C), MXU_DTYPE)],
        compiler_params=_params("arbitrary"))(mem, g, w_kv)


def _softmax_rows(s):
    e = jnp.exp(s - jnp.max(s, axis=-1, keepdims=True))
    return e / jnp.sum(e, axis=-1, keepdims=True)


def _attn_fwd(q, kv, x1, w_o, g_ffn):
    S, D = x1.shape
    M = kv.shape[0]
    hd = D // XA_HEADS
    scale = hd ** -0.5
    ts = _tile(S, 512)

    def body(q_ref, kv_ref, x1_ref, wo_ref, g_ref, o_ref, x2_ref, h3_ref):
        for h in range(XA_HEADS):
            cols = pl.ds(h * hd, hd)
            p = _softmax_rows(_mm_nt(q_ref[:, cols], kv_ref[:, cols]) * scale)
            o_ref[:, cols] = _mm(p, kv_ref[:, pl.ds(D + h * hd, hd)]).astype(MXU_DTYPE)
        x2 = x1_ref[...] + jnp.dot(o_ref[...], wo_ref[...], preferred_element_type=F32)
        x2_ref[...] = x2
        _, xhat = _rms(x2)
        h3_ref[...] = (xhat * g_ref[...]).astype(MXU_DTYPE)

    row = pl.BlockSpec((ts, D), lambda i: (i, 0))
    return pl.pallas_call(
        body, name="attn_fwd", grid=(S // ts,),
        in_specs=[row, pl.BlockSpec((M, 2 * D), lambda i: (0, 0)), row, pl.BlockSpec((D, D), lambda i: (0, 0)),
                  pl.BlockSpec((1, D), lambda i: (0, 0))],
        out_specs=[row, row, row],
        out_shape=[jax.ShapeDtypeStruct((S, D), MXU_DTYPE), jax.ShapeDtypeStruct((S, D), F32),
                   jax.ShapeDtypeStruct((S, D), MXU_DTYPE)],
        compiler_params=_params("parallel"))(q, kv, x1, w_o, g_ffn)


def _fwd_up(h3, w_up):
    S, D = h3.shape
    nb, _, C = w_up.shape
    ts = _tile(S, 512)

    def body(h_ref, w_ref, o_ref):
        o_ref[0] = jnp.dot(h_ref[...], w_ref[0], preferred_element_type=F32)

    return pl.pallas_call(
        body, name="fwd_up", grid=(nb, S // ts),
        in_specs=[pl.BlockSpec((ts, D), lambda j, i: (i, 0)), pl.BlockSpec((1, D, C), lambda j, i: (j, 0, 0))],
        out_specs=pl.BlockSpec((1, ts, C), lambda j, i: (j, i, 0)),
        out_shape=jax.ShapeDtypeStruct((nb, S, C), F32),
        compiler_params=_params("parallel", "parallel"))(h3, w_up)


def _ffn_down_loss(gu, conv_w, conv_b, w_down, x2, g_final, target):
    nb, S, C = gu.shape
    half = nb // 2
    D = x2.shape[1]
    ts = _tile(S, 256)
    taps = conv_w.shape[0]
    halo = 8
    hb = ts // halo

    def body(g_ref, gh_ref, u_ref, cw_ref, cb_ref, wd_ref, x2_ref, gf_ref, t_ref,
             act_ref, dx3_ref, loss_ref, dgf_ref, gbuf):
        i = pl.program_id(0)

        @pl.when(i == 0)
        def _():
            loss_ref[...] = jnp.zeros_like(loss_ref)
            dgf_ref[...] = jnp.zeros_like(dgf_ref)

        x3 = x2_ref[...]
        for j in range(half):
            cols = pl.ds(j * C, C)
            gbuf[pl.ds(0, halo), :] = jnp.where(i > 0, gh_ref[j], 0.0)
            gbuf[pl.ds(halo, ts), :] = g_ref[j]
            gc = _causal_taps(gbuf, halo, cw_ref, taps, ts, wcols=cols) + cb_ref[:, cols]
            ge, _ = _gelu(gc)
            act = (ge * u_ref[j]).astype(MXU_DTYPE)
            act_ref[j] = act
            x3 = x3 + jnp.dot(act, wd_ref[j], preferred_element_type=F32)
        rinv, xhat = _rms(x3)
        gf = gf_ref[...]
        diff = xhat * gf - t_ref[...]
        loss_ref[...] += _colsum(diff * diff) * (0.5 / D)
        dy = diff * (1.0 / D)
        dgf_ref[...] += _colsum(dy * xhat)
        dx3_ref[...] = _rms_bwd(rinv, xhat, dy * gf)

    row = pl.BlockSpec((ts, D), lambda i: (i, 0))
    vecd = pl.BlockSpec((1, D), lambda i: (0, 0))
    return pl.pallas_call(
        body, name="ffn_down_loss", grid=(S // ts,),
        in_specs=[pl.BlockSpec((half, ts, C), lambda i: (0, i, 0)),
                  pl.BlockSpec((half, halo, C), lambda i: (0, jnp.maximum(i * hb - 1, 0), 0)),
                  pl.BlockSpec((half, ts, C), lambda i: (1, i, 0)),
                  pl.BlockSpec((taps, half * C), lambda i: (0, 0)), pl.BlockSpec((1, half * C), lambda i: (0, 0)),
                  pl.BlockSpec((half, C, D), lambda i: (0, 0, 0)), row, vecd, row],
        out_specs=[pl.BlockSpec((half, ts, C), lambda i: (0, i, 0)), row, vecd, vecd],
        out_shape=[jax.ShapeDtypeStruct((half, S, C), MXU_DTYPE), jax.ShapeDtypeStruct((S, D), F32),
                   jax.ShapeDtypeStruct((1, D), F32), jax.ShapeDtypeStruct((1, D), F32)],
        scratch_shapes=[pltpu.VMEM((ts + halo, C), F32)],
        compiler_params=_params("arbitrary"))(gu, gu, gu, conv_w, conv_b, w_down, x2, g_final, target)


def _bwd_down(dx3, w_down, gu, conv_w, conv_b):
    nb, S, C = gu.shape
    half = nb // 2
    D = dx3.shape[1]
    ts = _tile(S, 256)
    n = S // ts
    taps = conv_w.shape[0]
    halo = 8
    hb = ts // halo

    def body(dx_ref, wd_ref, g_ref, gh_ref, u_ref, cw_ref, cb_ref, dgu_ref, dcw_ref, dcb_ref, gbuf, dbuf):
        i = pl.program_id(0)
        r = n - 1 - i

        @pl.when(i == 0)
        def _():
            dcw_ref[...] = jnp.zeros_like(dcw_ref)
            dcb_ref[...] = jnp.zeros_like(dcb_ref)
            dbuf[...] = jnp.zeros_like(dbuf)

        dxb = dx_ref[...].astype(MXU_DTYPE)
        for j in range(half):
            cols = pl.ds(j * C, C)
            dact = _mm_nt(dxb, wd_ref[j])
            gbuf[pl.ds(0, halo), :] = jnp.where(r > 0, gh_ref[j], 0.0)
            gbuf[pl.ds(halo, ts), :] = g_ref[j]
            gc = _causal_taps(gbuf, halo, cw_ref, taps, ts, wcols=cols) + cb_ref[:, cols]
            ge, dge = _gelu(gc)
            dgu_ref[half + j] = (dact * ge).astype(MXU_DTYPE)
            dgc = dact * u_ref[j] * dge
            dcb_ref[:, cols] += _colsum(dgc)
            dbuf[j, pl.ds(0, ts), :] = dgc
            _tap_grads(dcw_ref, dgc, gbuf, halo, taps, ts, wcols=cols)
            dgu_ref[j] = _anticausal_taps(dbuf.at[j], cw_ref, taps, ts, wcols=cols).astype(MXU_DTYPE)
            dbuf[j, pl.ds(ts, halo), :] = dbuf[j, pl.ds(0, halo), :]

    row = pl.BlockSpec((ts, D), lambda i: (n - 1 - i, 0))
    wide = pl.BlockSpec((taps, half * C), lambda i: (0, 0))
    wide1 = pl.BlockSpec((1, half * C), lambda i: (0, 0))
    return pl.pallas_call(
        body, name="bwd_down", grid=(n,),
        in_specs=[row, pl.BlockSpec((half, C, D), lambda i: (0, 0, 0)),
                  pl.BlockSpec((half, ts, C), lambda i: (0, n - 1 - i, 0)),
                  pl.BlockSpec((half, halo, C), lambda i: (0, jnp.maximum((n - 1 - i) * hb - 1, 0), 0)),
                  pl.BlockSpec((half, ts, C), lambda i: (1, n - 1 - i, 0)), wide, wide1],
        out_specs=[pl.BlockSpec((nb, ts, C), lambda i: (0, n - 1 - i, 0)), wide, wide1],
        out_shape=[jax.ShapeDtypeStruct((nb, S, C), MXU_DTYPE), jax.ShapeDtypeStruct((taps, half * C), F32),
                   jax.ShapeDtypeStruct((1, half * C), F32)],
        scratch_shapes=[pltpu.VMEM((ts + halo, C), F32), pltpu.VMEM((half, ts + halo, C), F32)],
        compiler_params=_params("arbitrary"))(dx3, w_down, gu, gu, gu, conv_w, conv_b)


def _bwd_up(dgu, w_up, x2, g_ffn, dx3, comm=None):
    nb, S, C = dgu.shape
    D = x2.shape[1]
    ts = _tile(S, 512)

    def body(d_ref, w_ref, x2_ref, g_ref, dx3_ref, dx2_ref, dg_ref, acc):
        i, j = pl.program_id(0), pl.program_id(1)

        @pl.when((i == 0) & (j == 0))
        def _():
            dg_ref[...] = jnp.zeros_like(dg_ref)

        part = _mm_nt(d_ref[0], w_ref[0])

        @pl.when(j == 0)
        def _():
            acc[...] = part

        @pl.when(j > 0)
        def _():
            acc[...] += part

        @pl.when(j == nb - 1)
        def _():
            rinv, xhat = _rms(x2_ref[...])
            dh = acc[...]
            dg_ref[...] += _colsum(dh * xhat)
            dx2_ref[...] = dx3_ref[...] + _rms_bwd(rinv, xhat, dh * g_ref[...])

    row = pl.BlockSpec((ts, D), lambda i, j: (i, 0))
    vecd = pl.BlockSpec((1, D), lambda i, j: (0, 0))
    return _pcall(
        comm, body, name="bwd_up", grid=(S // ts, nb),
        in_specs=[pl.BlockSpec((1, ts, C), lambda i, j: (j, i, 0)), pl.BlockSpec((1, D, C), lambda i, j: (j, 0, 0)),
                  row, vecd, row],
        out_specs=[row, vecd],
        out_shape=[jax.ShapeDtypeStruct((S, D), F32), jax.ShapeDtypeStruct((1, D), F32)],
        scratch_shapes=[pltpu.VMEM((ts, D), F32)],
        semantics=("arbitrary", "arbitrary"))(dgu, w_up, x2, g_ffn, dx3)


def _attn_bwd(dx2, w_o, q, kv, x1, g_xa, w_q, comm=None):
    S, D = x1.shape
    M = kv.shape[0]
    hd = D // XA_HEADS
    scale = hd ** -0.5
    ts = _tile(S, 512)

    def body(dx2_ref, wo_ref, q_ref, kv_ref, x1_ref, g_ref, wq_ref, dq_ref, dx1_ref, dkv_ref, dg_ref):
        i = pl.program_id(0)

        @pl.when(i == 0)
        def _():
            dkv_ref[...] = jnp.zeros_like(dkv_ref)
            dg_ref[...] = jnp.zeros_like(dg_ref)

        dx2 = dx2_ref[...]
        do = _mm_nt(dx2, wo_ref[...]).astype(MXU_DTYPE)
        for h in range(XA_HEADS):
            cols = pl.ds(h * hd, hd)
            vcols = pl.ds(D + h * hd, hd)
            qh, kh, doh = q_ref[:, cols], kv_ref[:, cols], do[:, h * hd:(h + 1) * hd]
            p = _softmax_rows(_mm_nt(qh, kh) * scale)
            dp = _mm_nt(doh, kv_ref[:, vcols])
            dkv_ref[:, vcols] += _mm_tn(p, doh)
            ds = (p * (dp - jnp.sum(dp * p, axis=-1, keepdims=True)) * scale).astype(MXU_DTYPE)
            dq_ref[:, cols] = _mm(ds, kh).astype(MXU_DTYPE)
            dkv_ref[:, cols] += _mm_tn(ds, qh)
        dh2 = _mm_nt(dq_ref[...], wq_ref[...])
        rinv, xhat = _rms(x1_ref[...])
        dg_ref[...] += _colsum(dh2 * xhat)
        dx1_ref[...] = dx2 + _rms_bwd(rinv, xhat, dh2 * g_ref[...])

    row = pl.BlockSpec((ts, D), lambda i: (i, 0))
    mat = pl.BlockSpec((D, D), lambda i: (0, 0))
    vecd = pl.BlockSpec((1, D), lambda i: (0, 0))
    kvs = pl.BlockSpec((M, 2 * D), lambda i: (0, 0))
    return _pcall(
        comm, body, name="attn_bwd", grid=(S // ts,),
        in_specs=[row, mat, row, kvs, row, vecd, mat],
        out_specs=[row, row, kvs, vecd],
        out_shape=[jax.ShapeDtypeStruct((S, D), MXU_DTYPE), jax.ShapeDtypeStruct((S, D), F32),
                   jax.ShapeDtypeStruct((M, 2 * D), F32), jax.ShapeDtypeStruct((1, D), F32)],
        semantics=("arbitrary",))(dx2, w_o, q, kv, x1, g_xa, w_q)


def _kv_bwd(dkv, w_kv, mem, g, m):
    M, D = mem.shape
    nb, _, C = w_kv.shape

    def body(dkv_ref, w_ref, mem_ref, m_ref, dw_ref, dg_ref):
        dm = jnp.zeros((M, D), F32)
        for j in range(nb):
            dj = dkv_ref[:, pl.ds(j * C, C)].astype(MXU_DTYPE)
            dw_ref[j] = _mm_tn(m_ref[...], dj).astype(dw_ref.dtype)
            dm = dm + _mm_nt(dj, w_ref[j])
        _, xhat = _rms(mem_ref[...])
        dg_ref[...] = _colsum(dm * xhat)

    full = lambda *s: pl.BlockSpec(s, lambda i: (0,) * len(s))
    return pl.pallas_call(
        body, name="kv_bwd", grid=(1,),
        in_specs=[full(M, nb * C), full(nb, D, C), full(M, D), full(M, D)],
        out_specs=[full(nb, D, C), full(1, D)],
        out_shape=[jax.ShapeDtypeStruct((nb, D, C), WIRE_DTYPE), jax.ShapeDtypeStruct((1, D), F32)],
        compiler_params=_params("arbitrary"))(dkv, w_kv, mem, m)


def _conf_bwd(dx1, w_out_c, z, c1, conv_w, ln_g, ln_b, comm=None):
    _, S, C = z.shape
    D = dx1.shape[1]
    ts = _tile(S, 256)
    n = S // ts
    taps = conv_w.shape[0]
    halo = 32
    hb = ts // halo

    def body(dx_ref, wo_ref, za_ref, zb_ref, zah_ref, zbh_ref, c1_ref, cw_ref, g_ref, b_ref,
             dz_ref, dcw_ref, dcb_ref, dlg_ref, dlb_ref, c0buf, dbuf):
        i = pl.program_id(0)
        r = n - 1 - i

        @pl.when(i == 0)
        def _():
            for ref in (dcw_ref, dcb_ref, dlg_ref, dlb_ref):
                ref[...] = jnp.zeros_like(ref)
            dbuf[pl.ds(ts, halo), :] = jnp.zeros((halo, C), F32)

        za = za_ref[0]
        sb = _sigmoid(zb_ref[0])
        c0buf[pl.ds(0, halo), :] = jnp.where(r > 0, zah_ref[0] * _sigmoid(zbh_ref[0]), 0.0)
        c0buf[pl.ds(halo, ts), :] = za * sb
        dc3 = _mm_nt(dx_ref[...], wo_ref[...])
        rstd, xhat = _layer_norm_stats(c1_ref[...])
        g = g_ref[...]
        c2 = xhat * g + b_ref[...]
        sg = _sigmoid(c2)
        dc2 = dc3 * sg * (1.0 + c2 * (1.0 - sg))
        dlg_ref[...] += _colsum(dc2 * xhat)
        dlb_ref[...] += _colsum(dc2)
        dxh = dc2 * g
        dc1 = rstd * (dxh - jnp.mean(dxh, axis=-1, keepdims=True)
                      - xhat * jnp.mean(dxh * xhat, axis=-1, keepdims=True))
        dcb_ref[...] += _colsum(dc1)
        dbuf[pl.ds(0, ts), :] = dc1
        _tap_grads(dcw_ref, dc1, c0buf, halo, taps, ts)
        dc0 = _anticausal_taps(dbuf, cw_ref, taps, ts)
        dz_ref[0] = (dc0 * sb).astype(MXU_DTYPE)
        dz_ref[1] = (dc0 * za * sb * (1.0 - sb)).astype(MXU_DTYPE)
        dbuf[pl.ds(ts, halo), :] = dbuf[pl.ds(0, halo), :]

    vec = pl.BlockSpec((1, C), lambda i: (0, 0))
    tapw = pl.BlockSpec((taps, C), lambda i: (0, 0))
    tile = lambda b: pl.BlockSpec((1, ts, C), lambda i: (b, n - 1 - i, 0))
    prev = lambda b: pl.BlockSpec((1, halo, C), lambda i: (b, jnp.maximum((n - 1 - i) * hb - 1, 0), 0))
    return _pcall(
        comm, body, name="conf_bwd", grid=(n,),
        in_specs=[pl.BlockSpec((ts, D), lambda i: (n - 1 - i, 0)), pl.BlockSpec((C, D), lambda i: (0, 0)),
                  tile(2), tile(3), prev(2), prev(3), pl.BlockSpec((ts, C), lambda i: (n - 1 - i, 0)),
                  tapw, vec, vec],
        out_specs=[pl.BlockSpec((2, ts, C), lambda i: (0, n - 1 - i, 0)), tapw, vec, vec, vec],
        out_shape=[jax.ShapeDtypeStruct((2, S, C), MXU_DTYPE), jax.ShapeDtypeStruct((taps, C), F32),
                   jax.ShapeDtypeStruct((1, C), F32), jax.ShapeDtypeStruct((1, C), F32),
                   jax.ShapeDtypeStruct((1, C), F32)],
        scratch_shapes=[pltpu.VMEM((ts + halo, C), F32), pltpu.VMEM((ts + halo, C), F32)],
        semantics=("arbitrary",))(dx1, w_out_c, z, z, z, z, c1, conv_w, ln_g, ln_b)


def _lru_bwd(dx1, w_out_l, z, h, conv_w, conv_b, wa, ba, wx, bx, lam, comm=None):
    _, S, C = z.shape
    D = dx1.shape[1]
    ts = _tile(S, 256)
    n = S // ts
    taps = conv_w.shape[0]
    halo = 8
    hb = ts // halo

    def body(dx_ref, wo_ref, zx_ref, zxh_ref, zg_ref, h_ref, hh_ref, cw_ref, cb_ref, wa_ref, ba_ref,
             wx_ref, bx_ref, lam_ref,
             dz_ref, dwa_ref, dwx_ref, dba_ref, dbx_ref, dlam_ref, dcw_ref, dcb_ref,
             xbuf, hbuf, a_s, w_s, dh_s, g_s, dbuf, pc):
        i = pl.program_id(0)
        r = n - 1 - i

        @pl.when(i == 0)
        def _():
            for ref in (dwa_ref, dwx_ref, dba_ref, dbx_ref, dlam_ref, dcw_ref, dcb_ref, pc):
                ref[...] = jnp.zeros_like(ref)
            dbuf[pl.ds(ts, halo), :] = jnp.zeros((halo, C), F32)

        xbuf[pl.ds(0, halo), :] = jnp.where(r > 0, zxh_ref[0], 0.0)
        xbuf[pl.ds(halo, ts), :] = zx_ref[0]
        hbuf[pl.ds(0, halo), :] = jnp.where(r > 0, hh_ref[...], 0.0)
        hbuf[pl.ds(halo, ts), :] = h_ref[...]
        xc = _causal_taps(xbuf, halo, cw_ref, taps, ts) + cb_ref[...]
        lam_v = lam_ref[...]
        sp = _softplus_neg(lam_v)
        rg, ig, a, mult = _lru_gates(xc, wa_ref, ba_ref, wx_ref, bx_ref, sp)

        dy = _mm_nt(dx_ref[...], wo_ref[...])
        ge, dge = _gelu(zg_ref[0])
        dh = dy * ge
        dz_ref[1] = (dy * h_ref[...] * dge).astype(MXU_DTYPE)
        a_s[...] = a
        w_s[...] = a * dh
        dh_s[...] = dh
        row = lax.broadcasted_iota(jnp.int32, (8, C), 0)

        def step(kk, carry):
            off = pl.multiple_of((ts // 8 - 1 - kk) * 8, 8)
            av = a_s[pl.ds(off, 8), :]
            wv = w_s[pl.ds(off, 8), :]
            for d in (1, 2, 4):
                m = row < 8 - d
                a_sh = jnp.where(m, pltpu.roll(av, 8 - d, 0), 1.0)
                w_sh = jnp.where(m, pltpu.roll(wv, 8 - d, 0), 0.0)
                wv = wv + av * w_sh
                av = av * a_sh
            pv = wv + av * carry
            g_s[pl.ds(off, 8), :] = dh_s[pl.ds(off, 8), :] + jnp.where(row < 7, pltpu.roll(pv, 7, 0), carry)
            return jnp.broadcast_to(pv[0:1, :], (8, C))

        pc[...] = lax.fori_loop(0, ts // 8, step, pc[...])
        gt = g_s[...]
        da = gt * hbuf[pl.ds(halo - 1, ts), :]
        gm = gt * mult
        dlog_a = da * a - (gt * ig * xc) * (a * a) / mult
        dlam_ref[...] += _colsum(dlog_a * rg) * (RG_C * _sigmoid(-lam_v))
        dpa = (dlog_a * (-RG_C * sp)) * rg * (1.0 - rg)
        dpx = (gm * xc) * ig * (1.0 - ig)
        dba_ref[...] += _colsum(dpa)
        dbx_ref[...] += _colsum(dpx)
        xb = xc.astype(MXU_DTYPE)
        dpab, dpxb = dpa.astype(MXU_DTYPE), dpx.astype(MXU_DTYPE)
        dwa_ref[...] += _mm_tn(xb, dpab)
        dwx_ref[...] += _mm_tn(xb, dpxb)
        dxc = gm * ig + _mm_nt(dpab, wa_ref[...]) + _mm_nt(dpxb, wx_ref[...])
        dcb_ref[...] += _colsum(dxc)
        dbuf[pl.ds(0, ts), :] = dxc
        _tap_grads(dcw_ref, dxc, xbuf, halo, taps, ts)
        dz_ref[0] = _anticausal_taps(dbuf, cw_ref, taps, ts).astype(MXU_DTYPE)
        dbuf[pl.ds(ts, halo), :] = dbuf[pl.ds(0, halo), :]

    vec = pl.BlockSpec((1, C), lambda i: (0, 0))
    mat = pl.BlockSpec((C, C), lambda i: (0, 0))
    tapw = pl.BlockSpec((taps, C), lambda i: (0, 0))
    prev_rows = lambda i: jnp.maximum((n - 1 - i) * hb - 1, 0)
    sds = jax.ShapeDtypeStruct
    return _pcall(
        comm, body, name="lru_bwd", grid=(n,),
        in_specs=[pl.BlockSpec((ts, D), lambda i: (n - 1 - i, 0)), pl.BlockSpec((C, D), lambda i: (0, 0)),
                  pl.BlockSpec((1, ts, C), lambda i: (0, n - 1 - i, 0)),
                  pl.BlockSpec((1, halo, C), lambda i: (0, prev_rows(i), 0)),
                  pl.BlockSpec((1, ts, C), lambda i: (1, n - 1 - i, 0)),
                  pl.BlockSpec((ts, C), lambda i: (n - 1 - i, 0)),
                  pl.BlockSpec((halo, C), lambda i: (prev_rows(i), 0)),
                  tapw, vec, mat, vec, mat, vec, vec],
        out_specs=[pl.BlockSpec((2, ts, C), lambda i: (0, n - 1 - i, 0)), mat, mat, vec, vec, vec, tapw, vec],
        out_shape=[sds((2, S, C), MXU_DTYPE), sds((C, C), F32), sds((C, C), F32), sds((1, C), F32),
                   sds((1, C), F32), sds((1, C), F32), sds((taps, C), F32), sds((1, C), F32)],
        scratch_shapes=[pltpu.VMEM((ts + halo, C), F32), pltpu.VMEM((ts + halo, C), F32)]
        + [pltpu.VMEM((ts, C), F32)] * 4 + [pltpu.VMEM((ts + halo, C), F32), pltpu.VMEM((8, C), F32)],
        semantics=("arbitrary",))(dx1, w_out_l, z, z, z, h, h, conv_w, conv_b, wa, ba, wx, bx, lam)


def _bwd_in(dz_l, dz_c, w_in, x, g, dx1):
    S, D = x.shape
    nb, _, C = w_in.shape
    ts = _tile(S, 512)

    def body(dl_ref, dc_ref, w_ref, x_ref, g_ref, dx1_ref, dx_ref, dg_ref):
        i = pl.program_id(0)

        @pl.when(i == 0)
        def _():
            dg_ref[...] = jnp.zeros_like(dg_ref)

        dh = (_mm_nt(dl_ref[0], w_ref[0]) + _mm_nt(dl_ref[1], w_ref[1])
              + _mm_nt(dc_ref[0], w_ref[2]) + _mm_nt(dc_ref[1], w_ref[3]))
        rinv, xhat = _rms(x_ref[...])
        dg_ref[...] += _colsum(dh * xhat)
        dx_ref[...] = dx1_ref[...] + _rms_bwd(rinv, xhat, dh * g_ref[...])

    row = pl.BlockSpec((ts, D), lambda i: (i, 0))
    pair = pl.BlockSpec((2, ts, C), lambda i: (0, i, 0))
    vecd = pl.BlockSpec((1, D), lambda i: (0, 0))
    return pl.pallas_call(
        body, name="bwd_in", grid=(S // ts,),
        in_specs=[pair, pair, pl.BlockSpec((nb, D, C), lambda i: (0, 0, 0)), row, vecd, row],
        out_specs=[row, vecd],
        out_shape=[jax.ShapeDtypeStruct((S, D), F32), jax.ShapeDtypeStruct((1, D), F32)],
        compiler_params=_params("arbitrary"))(dz_l, dz_c, w_in, x, g, dx1)


def _wgrad(a, b, name, comm=None):
    na, S, K = a.shape
    nb, _, N = b.shape
    nj = max(na, nb)
    assert min(na, nb) == 1
    ts = _tile(S, 1024)
    ns = S // ts

    def body(a_ref, b_ref, o_ref, acc):
        s = pl.program_id(1)
        part = _mm_tn(a_ref[0], b_ref[0])

        @pl.when(s == 0)
        def _():
            acc[...] = part

        @pl.when(s > 0)
        def _():
            acc[...] += part

        @pl.when(s == ns - 1)
        def _():
            o_ref[0] = acc[...].astype(o_ref.dtype)

    res = _pcall(
        comm, body, name=name, grid=(nj, ns),
        in_specs=[pl.BlockSpec((1, ts, K), (lambda j, s: (j, s, 0)) if na > 1 else (lambda j, s: (0, s, 0))),
                  pl.BlockSpec((1, ts, N), (lambda j, s: (j, s, 0)) if nb > 1 else (lambda j, s: (0, s, 0)))],
        out_specs=pl.BlockSpec((1, K, N), lambda j, s: (j, 0, 0)),
        out_shape=jax.ShapeDtypeStruct((nj, K, N), WIRE_DTYPE),
        scratch_shapes=[pltpu.VMEM((K, N), F32)],
        semantics=("parallel", "arbitrary"))(a, b)
    return res[0] if comm is None else (res[0][0], res[1])


def _place():
    x, y, c = lax.axis_index("x"), lax.axis_index("y"), lax.axis_index("c")
    other_chips = [(1 - x, y), (x, 1 - y), (1 - x, 1 - y)]
    return x, y, c, other_chips


def _gather_weights(shards):
    nt = len(shards)

    def body(*refs):
        src, dst = refs[:nt], refs[nt:2 * nt]
        ici_send, ici_recv, d2d_send, d2d_recv, own_send, own_recv = refs[2 * nt:]
        x, y, c, chips = _place()
        mine = 2 * x + y

        def half(t, pc):
            hr = src[t].shape[0] // 2
            return pl.ds(pc * hr, hr)

        def own(t):
            return pltpu.make_async_remote_copy(
                src_ref=src[t], dst_ref=dst[t].at[mine], send_sem=own_send.at[t], recv_sem=own_recv.at[t],
                device_id=(x, y, 1 - c), device_id_type=MESH)

        def ici(t, k, block, to):
            cx, cy = block
            ref = dst[t].at[2 * cx + cy, half(t, c)]
            return pltpu.make_async_remote_copy(
                src_ref=src[t].at[half(t, c)] if to is not None else ref, dst_ref=ref,
                send_sem=ici_send.at[t, k], recv_sem=ici_recv.at[t, k],
                device_id=(*to, c) if to is not None else (x, y, c), device_id_type=MESH)

        def d2d(t, k, block, pc):
            cx, cy = block
            ref = dst[t].at[2 * cx + cy, half(t, pc)]
            return pltpu.make_async_remote_copy(
                src_ref=ref, dst_ref=ref, send_sem=d2d_send.at[t, k], recv_sem=d2d_recv.at[t, k],
                device_id=(x, y, 1 - c), device_id_type=MESH)

        sends = [ici(t, k, (x, y), chip) for t in range(nt) for k, chip in enumerate(chips)]
        sends += [own(t) for t in range(nt)]
        for cp in sends:
            cp.start()
        passed = []
        for t in range(nt):
            for k, chip in enumerate(chips):
                ici(t, k, chip, None).wait_recv()
                fw = d2d(t, k, chip, c)
                fw.start()
                passed.append(fw)
        for t in range(nt):
            own(t).wait_recv()
            for k, chip in enumerate(chips):
                d2d(t, k, chip, 1 - c).wait_recv()
        for cp in sends + passed:
            cp.wait_send()

    return pl.pallas_call(
        body, name="gather_weights",
        in_specs=[ANY] * nt, out_specs=[ANY] * nt,
        out_shape=[jax.ShapeDtypeStruct((N_CHIPS,) + s.shape, s.dtype) for s in shards],
        scratch_shapes=[pltpu.SemaphoreType.DMA((nt, 3))] * 4 + [pltpu.SemaphoreType.DMA((nt,))] * 2,
        compiler_params=pltpu.CompilerParams(has_side_effects=True))(*shards)


def _gather_over_ici(shards):
    nt = len(shards)

    def copies(src, dst, scr, arriving):
        ici_send, ici_recv, own_send, own_recv = scr
        x, y, c, chips = _place()
        out = []
        for t in range(nt):
            hr = src[t].shape[0] // 2
            rows = pl.ds(c * hr, hr)
            for k, (cx, cy) in enumerate(chips):
                block = 2 * cx + cy if arriving else 2 * x + y
                out.append(pltpu.make_async_remote_copy(
                    src_ref=src[t].at[rows], dst_ref=dst[t].at[block, rows],
                    send_sem=ici_send.at[t, k], recv_sem=ici_recv.at[t, k],
                    device_id=(cx, cy, c), device_id_type=MESH))
            out.append(pltpu.make_async_remote_copy(
                src_ref=src[t], dst_ref=dst[t].at[2 * x + y], send_sem=own_send.at[t], recv_sem=own_recv.at[t],
                device_id=(x, y, 1 - c), device_id_type=MESH))
        return out

    def start(src, dst, scr):
        for cp in copies(src, dst, scr, False):
            cp.start()

    def finish(src, dst, scr):
        for cp in copies(src, dst, scr, True):
            cp.wait_recv()
        for cp in copies(src, dst, scr, False):
            cp.wait_send()

    return _Comm(shards, [jax.ShapeDtypeStruct((N_CHIPS,) + s.shape, s.dtype) for s in shards],
                 [pltpu.SemaphoreType.DMA((nt, 3))] * 2 + [pltpu.SemaphoreType.DMA((nt,))] * 2, start, finish)


def _gather_pass_on(bufs):
    nt = len(bufs)

    def passed(dst, scr, t, k, block, pc):
        send, recv = scr
        x, y, c, _ = _place()
        cx, cy = block
        hr = dst[t].shape[1] // 2
        ref = dst[t].at[2 * cx + cy, pl.ds(pc * hr, hr)]
        return pltpu.make_async_remote_copy(src_ref=ref, dst_ref=ref, send_sem=send.at[t, k], recv_sem=recv.at[t, k],
                                            device_id=(x, y, 1 - c), device_id_type=MESH)

    def start(src, dst, scr):
        _, _, c, chips = _place()
        for t in range(nt):
            for k, chip in enumerate(chips):
                passed(dst, scr, t, k, chip, c).start()

    def finish(src, dst, scr):
        _, _, c, chips = _place()
        for t in range(nt):
            for k, chip in enumerate(chips):
                passed(dst, scr, t, k, chip, 1 - c).wait_recv()
        for t in range(nt):
            for k, chip in enumerate(chips):
                passed(dst, scr, t, k, chip, c).wait_send()

    return _Comm(bufs, [jax.ShapeDtypeStruct(b.shape, b.dtype) for b in bufs],
                 [pltpu.SemaphoreType.DMA((nt, 3))] * 2, start, finish, aliases={t: t for t in range(nt)})


def _exchange_halves(grads):
    nt = len(grads)

    def copies(src, dst, scr):
        send, recv = scr
        x, y, c, _ = _place()
        out = []
        for t in range(nt):
            hr = src[t].shape[1] // 2
            out.append(pltpu.make_async_remote_copy(
                src_ref=src[t].at[:, pl.ds((1 - c) * hr, hr)], dst_ref=dst[t],
                send_sem=send.at[t], recv_sem=recv.at[t], device_id=(x, y, 1 - c), device_id_type=MESH))
        return out

    def start(src, dst, scr):
        for cp in copies(src, dst, scr):
            cp.start()

    def finish(src, dst, scr):
        for cp in copies(src, dst, scr):
            cp.wait()

    return _Comm(grads, [jax.ShapeDtypeStruct((g.shape[0], g.shape[1] // 2, g.shape[2]), g.dtype) for g in grads],
                 [pltpu.SemaphoreType.DMA((nt,))] * 2, start, finish)


def _add_halves(grad, other, name):
    nb, R, C = grad.shape
    hr = R // 2
    tr = _tile(hr, 256, 16)
    steps = hr // tr
    c = lax.axis_index("c").astype(jnp.int32).reshape((1,))

    def body(c_ref, a_ref, b_ref, o_ref):
        o_ref[...] = (a_ref[...].astype(F32) + b_ref[...].astype(F32)).astype(o_ref.dtype)

    return pl.pallas_call(
        body, name=name,
        grid_spec=pltpu.PrefetchScalarGridSpec(
            num_scalar_prefetch=1, grid=(nb, steps),
            in_specs=[pl.BlockSpec((1, tr, C), lambda j, i, c_ref: (j, c_ref[0] * steps + i, 0)),
                      pl.BlockSpec((1, tr, C), lambda j, i, c_ref: (j, i, 0))],
            out_specs=pl.BlockSpec((1, tr, C), lambda j, i, c_ref: (j, i, 0))),
        out_shape=jax.ShapeDtypeStruct((nb, hr, C), grad.dtype),
        compiler_params=_params("parallel", "parallel"))(c, grad, other)


def _scatter_chip_sums(parts):
    nt = len(parts)

    def copies(src, dst, scr):
        send, recv = scr
        x, y, c, chips = _place()
        out = []
        for t in range(nt):
            for k, (cx, cy) in enumerate(chips):
                out.append(pltpu.make_async_remote_copy(
                    src_ref=src[t].at[2 * cx + cy], dst_ref=dst[t].at[k],
                    send_sem=send.at[t, k], recv_sem=recv.at[t, k], device_id=(cx, cy, c), device_id_type=MESH))
        return out

    def start(src, dst, scr):
        for cp in copies(src, dst, scr):
            cp.start()

    def finish(src, dst, scr):
        for cp in copies(src, dst, scr):
            cp.wait()

    return _Comm(parts, [jax.ShapeDtypeStruct((3,) + p.shape[1:], p.dtype) for p in parts],
                 [pltpu.SemaphoreType.DMA((nt, 3))] * 2, start, finish)


def _sum_chips(part, recv, name):
    _, hr, C = part.shape
    tr = _tile(hr, 256, 16)
    steps = hr // tr
    where = jnp.stack([2 * lax.axis_index("x") + lax.axis_index("y"), lax.axis_index("c")]).astype(jnp.int32)

    def body(w_ref, a_ref, b_ref, o_ref):
        acc = a_ref[0].astype(F32)
        for k in range(3):
            acc = acc + b_ref[k].astype(F32)
        o_ref[...] = acc

    return pl.pallas_call(
        body, name=name,
        grid_spec=pltpu.PrefetchScalarGridSpec(
            num_scalar_prefetch=1, grid=(steps,),
            in_specs=[pl.BlockSpec((1, tr, C), lambda i, w_ref: (w_ref[0], i, 0)),
                      pl.BlockSpec((3, tr, C), lambda i, w_ref: (0, i, 0))],
            out_specs=pl.BlockSpec((tr, C), lambda i, w_ref: (w_ref[1] * steps + i, 0))),
        out_shape=jax.ShapeDtypeStruct((2 * hr, C), F32),
        compiler_params=_params("parallel"))(where, part, recv)


def _join_halves(bufs):
    nt = len(bufs)

    def swap(dst, scr, t, pc):
        send, recv = scr
        x, y, c, _ = _place()
        hr = dst[t].shape[0] // 2
        rows = dst[t].at[pl.ds(pc * hr, hr)]
        return pltpu.make_async_remote_copy(src_ref=rows, dst_ref=rows, send_sem=send.at[t], recv_sem=recv.at[t],
                                            device_id=(x, y, 1 - c), device_id_type=MESH)

    def start(src, dst, scr):
        c = lax.axis_index("c")
        for t in range(nt):
            swap(dst, scr, t, c).start()

    def finish(src, dst, scr):
        c = lax.axis_index("c")
        for t in range(nt):
            swap(dst, scr, t, 1 - c).wait_recv()
        for t in range(nt):
            swap(dst, scr, t, c).wait_send()

    return _Comm(bufs, [jax.ShapeDtypeStruct(b.shape, b.dtype) for b in bufs],
                 [pltpu.SemaphoreType.DMA((nt,))] * 2, start, finish, aliases={t: t for t in range(nt)})


def _all_reduce_rows(buf, loss_row=None):
    R, L = buf.shape

    def copies(in_ref, gath, send, recv):
        x, y, c, _ = _place()
        out = []
        for k in range(1, N_DEV):
            peer = (x ^ ((k >> 2) & 1), y ^ ((k >> 1) & 1), c ^ (k & 1))
            out.append(pltpu.make_async_remote_copy(
                src_ref=in_ref, dst_ref=gath.at[k], send_sem=send.at[k - 1], recv_sem=recv.at[k - 1],
                device_id=peer, device_id_type=MESH))
        return out

    def start(ins, outs, scr):
        gath, send, recv = scr
        gath[0] = ins[0][...]
        for cp in copies(ins[0], gath, send, recv):
            cp.start()

    def finish(ins, outs, scr):
        gath, send, recv = scr
        for cp in copies(ins[0], gath, send, recv):
            cp.wait()
        x, y, c, _ = _place()
        me = 4 * x + 2 * y + c
        total = gath[me]
        for d in range(1, N_DEV):
            total = total + gath[d ^ me]
        outs[0][...] = total
        if loss_row is not None:
            outs[1][...] = jnp.sum(total[loss_row:loss_row + 1, :], axis=1, keepdims=True)

    out_shape = [jax.ShapeDtypeStruct((R, L), F32)]
    if loss_row is not None:
        out_shape.append(jax.ShapeDtypeStruct((1, 1), F32))
    return _Comm([buf], out_shape,
                 [pltpu.VMEM((N_DEV, R, L), F32), pltpu.SemaphoreType.DMA((N_DEV - 1,)),
                  pltpu.SemaphoreType.DMA((N_DEV - 1,))],
                 start, finish, in_specs=[WHOLE_VMEM], out_specs=[WHOLE_VMEM] * len(out_shape))


def _adamw(w, g, m, v, name):
    R, C = w.shape
    tr = _tile(R, 256)
    c1 = 1.0 - ADAM_B1 ** ADAM_STEP
    c2 = 1.0 - ADAM_B2 ** ADAM_STEP

    def body(w_ref, g_ref, m_ref, v_ref, d_ref, nm_ref, nv_ref):
        gv = g_ref[...]
        nm = ADAM_B1 * m_ref[...] + (1.0 - ADAM_B1) * gv
        nv = ADAM_B2 * v_ref[...] + (1.0 - ADAM_B2) * (gv * gv)
        nm_ref[...] = nm
        nv_ref[...] = nv
        d_ref[...] = -ADAM_LR * ((nm / c1) / (jnp.sqrt(nv / c2) + ADAM_EPS) + ADAM_WD * w_ref[...])

    blk = pl.BlockSpec((tr, C), lambda i: (i, 0))
    return pl.pallas_call(
        body, name=name, grid=(R // tr,), in_specs=[blk] * 4, out_specs=[blk] * 3,
        out_shape=[jax.ShapeDtypeStruct((R, C), F32)] * 3,
        compiler_params=_params("parallel"))(w, g, m, v)


def _pack_rows(arrays):
    rows = []
    for a in arrays:
        flat = a.reshape(-1).astype(F32)
        pad = (-flat.shape[0]) % LANES
        rows.append(jnp.pad(flat, (0, pad)).reshape(-1, LANES))
    buf = jnp.concatenate(rows, axis=0)
    return jnp.pad(buf, ((0, (-buf.shape[0]) % 8), (0, 0)))


def _unpack_rows(buf, shapes):
    out, r = [], 0
    for s in shapes:
        n = math.prod(s)
        nr = -(-n // LANES)
        out.append(buf[r:r + nr].reshape(-1)[:n].reshape(s))
        r += nr
    return out


def _block_diag(w):
    H, a, b = w.shape
    eye = jnp.eye(H, dtype=w.dtype)
    return (eye[:, None, :, None] * w[:, :, None, :]).reshape(H * a, H * b)


def _block_diag_parts(d, H):
    a, b = d.shape[0] // H, d.shape[1] // H
    d4 = d.reshape(H, a, H, b)
    return jnp.stack([d4[h, :, h, :] for h in range(H)])


def _rs_add(names, grads, others):
    return [_add_halves(g, o, "rs_add_halves_" + n) for n, g, o in zip(names, grads, others)]


def _rs_sum(names, parts, recvs):
    return [_sum_chips(p, r, "rs_sum_chips_" + n) for n, p, r in zip(names, parts, recvs)]


def _step(x, mem, target, shards, small, tap_rows, tap_shapes):
    D = x.shape[1]
    nch = N_CHIPS
    p = dict(small)

    (w_in_f,) = _gather_weights([shards['w_in']])
    early, late = ['w_out', 'w_q', 'w_kv', 'w_o'], ['w_up', 'w_down']
    ici_early, taps_sum = _gather_over_ici([shards[n] for n in early]), _all_reduce_rows(tap_rows)
    (z, h1), couts = _fwd_in(x, p['mix_norm_g'], w_in_f, comm=_merge(ici_early, taps_sum))
    early_bufs, (taps,) = _split(couts, ici_early, taps_sum)
    p.update(zip(COL_SHARDED_SMALL, _unpack_rows(taps, tap_shapes)))
    wa_d = _block_diag(p['lru_w_a']).astype(MXU_DTYPE)
    wx_d = _block_diag(p['lru_w_x']).astype(MXU_DTYPE)
    heads = p['lru_w_a'].shape[0]
    (h, y_lru), early_full = _lru_fwd(z, p['lru_conv_w'], p['lru_conv_b'], wa_d, p['lru_b_a'], wx_d, p['lru_b_x'],
                                      p['lru_lambda'], comm=_gather_pass_on(early_bufs))
    wf = dict(zip(early, early_full))
    (c1, c3), late_bufs = _conf_fwd(z, p['conf_conv_w'], p['conf_conv_b'], p['conf_ln_g'], p['conf_ln_b'],
                                    comm=_gather_over_ici([shards[n] for n in late]))
    w_out2 = wf['w_out'].reshape(2, -1, D)
    w_q = wf['w_q'].reshape(D, D)
    w_o = wf['w_o'].reshape(D, D)
    (x1, h2, q), late_full = _fwd_out_q(x, y_lru, c3, w_out2, p['xa_norm_g'], w_q, comm=_gather_pass_on(late_bufs))
    wf.update(zip(late, late_full))
    w_down2 = wf['w_down'].reshape(wf['w_up'].shape[0] // 2, -1, D)

    m, kv = _kv_fwd(mem, p['mem_norm_g'], wf['w_kv'])
    o, x2, h3 = _attn_fwd(q, kv, x1, w_o, p['ffn_norm_g'])
    gu = _fwd_up(h3, wf['w_up'])
    act, dx3, loss_lanes, d_final_g = _ffn_down_loss(gu, p['ffn_conv_w'], p['ffn_conv_b'], w_down2, x2,
                                                     p['final_norm_g'], target)

    dgu, d_ffn_cw, d_ffn_cb = _bwd_down(dx3, w_down2, gu, p['ffn_conv_w'], p['ffn_conv_b'])
    g_down = _wgrad(act, dx3[None], "wgrad_down").reshape(nch, -1, D)
    (dx2, d_ffn_g), other = _bwd_up(dgu, wf['w_up'], x2, p['ffn_norm_g'], dx3, comm=_exchange_halves([g_down]))
    p_down = _rs_add(['w_down'], [g_down], other)
    g_up, recv = _wgrad(h3[None], dgu, "wgrad_up", comm=_scatter_chip_sums(p_down))
    join_down, ex_up = _join_halves(_rs_sum(['w_down'], p_down, recv)), _exchange_halves([g_up])
    (dq, dx1, dkv, d_xa_g), couts = _attn_bwd(dx2, w_o, q, kv, x1, p['xa_norm_g'], w_q,
                                              comm=_merge(join_down, ex_up))
    (r_down,), other = _split(couts, join_down, ex_up)
    p_up = _rs_add(['w_up'], [g_up], other)
    mid = ['w_o', 'w_q', 'w_kv']
    g_o = _wgrad(o[None], dx2[None], "wgrad_o").reshape(nch, -1, D)
    g_q = _wgrad(h2[None], dq[None], "wgrad_q").reshape(nch, -1, D)
    g_kv, d_mem_g = _kv_bwd(dkv, wf['w_kv'], mem, p['mem_norm_g'], m)
    sc_up, ex_mid = _scatter_chip_sums(p_up), _exchange_halves([g_o, g_q, g_kv])
    (dz_c, d_conf_cw, d_conf_cb, d_ln_g, d_ln_b), couts = _conf_bwd(
        dx1, w_out2[1], z, c1, p['conf_conv_w'], p['conf_ln_g'], p['conf_ln_b'], comm=_merge(sc_up, ex_mid))
    recv, other = _split(couts, sc_up, ex_mid)
    p_mid = _rs_add(mid, [g_o, g_q, g_kv], other)
    join_up, sc_mid = _join_halves(_rs_sum(['w_up'], p_up, recv)), _scatter_chip_sums(p_mid)
    (dz_l, d_wa, d_wx, d_ba, d_bx, d_lam, d_lru_cw, d_lru_cb), couts = _lru_bwd(
        dx1, w_out2[0], z, h, p['lru_conv_w'], p['lru_conv_b'], wa_d, p['lru_b_a'], wx_d, p['lru_b_x'],
        p['lru_lambda'], comm=_merge(join_up, sc_mid))
    (r_up,), recv = _split(couts, join_up, sc_mid)
    f_mid = _rs_sum(mid, p_mid, recv)
    grad_x, d_mix_g = _bwd_in(dz_l, dz_c, w_in_f, x, p['mix_norm_g'], dx1)
    g_out = jnp.concatenate([_wgrad(y_lru[None], dx1[None], "wgrad_out_lru"),
                             _wgrad(c3[None], dx1[None], "wgrad_out_conf")], axis=0).reshape(nch, -1, D)

    small_g = {'mix_norm_g': d_mix_g, 'lru_conv_w': d_lru_cw, 'lru_conv_b': d_lru_cb,
               'lru_w_a': _block_diag_parts(d_wa, heads), 'lru_b_a': d_ba,
               'lru_w_x': _block_diag_parts(d_wx, heads), 'lru_b_x': d_bx, 'lru_lambda': d_lam,
               'conf_conv_w': d_conf_cw, 'conf_conv_b': d_conf_cb, 'conf_ln_g': d_ln_g, 'conf_ln_b': d_ln_b,
               'xa_norm_g': d_xa_g, 'mem_norm_g': d_mem_g, 'ffn_norm_g': d_ffn_g,
               'ffn_conv_w': d_ffn_cw, 'ffn_conv_b': d_ffn_cb, 'final_norm_g': d_final_g}
    names = list(small_g)
    shapes = [small_g[n].shape for n in names]
    join_mid, ex_out = _join_halves(f_mid), _exchange_halves([g_out])
    small_sum = _all_reduce_rows(_pack_rows([loss_lanes] + [small_g[n] for n in names]), loss_row=0)
    g_in, couts = _wgrad(h1[None], jnp.concatenate([dz_l, dz_c], axis=0), "wgrad_in",
                         comm=_merge(join_mid, ex_out, small_sum))
    r_mid, other, (summed, loss) = _split(couts, join_mid, ex_out, small_sum)

    last = ['w_out', 'w_in']
    p_last = _rs_add(['w_out'], [g_out], other)
    p_last += _rs_add(['w_in'], [g_in], _run_comm(_exchange_halves([g_in]), "rs_exchange_w_in"))
    recv = _run_comm(_scatter_chip_sums(p_last), "rs_scatter_last")
    r_last = _run_comm(_join_halves(_rs_sum(last, p_last, recv)), "rs_join_last")
    big = dict(zip(['w_down', 'w_up'] + mid + last, [r_down, r_up] + r_mid + r_last))
    return grad_x, big, summed, loss, names, [loss_lanes.shape] + shapes


def kernel(x, mem, mix_norm_g, w_in, lru_conv_w, lru_conv_b, lru_w_a, lru_b_a, lru_w_x, lru_b_x, lru_lambda, conf_conv_w, conf_conv_b, conf_ln_g, conf_ln_b, w_out, xa_norm_g, mem_norm_g, w_q, w_kv, w_o, ffn_norm_g, w_up, ffn_conv_w, ffn_conv_b, w_down, final_norm_g, loss_target, m_mix_norm_g, m_w_in, m_lru_conv_w, m_lru_conv_b, m_lru_w_a, m_lru_b_a, m_lru_w_x, m_lru_b_x, m_lru_lambda, m_conf_conv_w, m_conf_conv_b, m_conf_ln_g, m_conf_ln_b, m_w_out, m_xa_norm_g, m_mem_norm_g, m_w_q, m_w_kv, m_w_o, m_ffn_norm_g, m_w_up, m_ffn_conv_w, m_ffn_conv_b, m_w_down, m_final_norm_g, v_mix_norm_g, v_w_in, v_lru_conv_w, v_lru_conv_b, v_lru_w_a, v_lru_b_a, v_lru_w_x, v_lru_b_x, v_lru_lambda, v_conf_conv_w, v_conf_conv_b, v_conf_ln_g, v_conf_ln_b, v_w_out, v_xa_norm_g, v_mem_norm_g, v_w_q, v_w_kv, v_w_o, v_ffn_norm_g, v_w_up, v_ffn_conv_w, v_ffn_conv_b, v_w_down, v_final_norm_g):
    given = dict(locals())
    w = {n: given[n] for n in WEIGHTS}
    mom = {n: given["m_" + n] for n in WEIGHTS}
    var = {n: given["v_" + n] for n in WEIGHTS}
    xi, yi, ci = lax.axis_index("x"), lax.axis_index("y"), lax.axis_index("c")
    chip = 2 * xi + yi

    shards = {n: w[n][0].astype(WIRE_DTYPE) for n in BIG}
    tap_full = []
    for n in COL_SHARDED_SMALL:
        s = w[n][0]
        full = jnp.zeros((s.shape[0], N_CHIPS * s.shape[1]), F32)
        s = jnp.where(ci == 0, s, jnp.zeros_like(s))
        tap_full.append(lax.dynamic_update_slice(full, s, (0, chip * s.shape[1])))
    small = {n: (w[n] if w[n].ndim == 1 else w[n][0]) for n in SMALL if n not in COL_SHARDED_SMALL}
    small = {n: (a.reshape(1, -1) if a.ndim == 1 else a) for n, a in small.items()}

    grad_x, big_g, summed, loss, small_names, packed_shapes = _step(
        x[0], mem[0], loss_target[0], shards, small, _pack_rows(tap_full), [t.shape for t in tap_full])
    small_sum = dict(zip(small_names, _unpack_rows(summed, packed_shapes)[1:]))

    grads = {}
    for n in WEIGHTS:
        if n in BIG:
            g = big_g[n]
        elif n in COL_SHARDED_SMALL:
            width = w[n].shape[-1]
            g = lax.dynamic_slice_in_dim(small_sum[n], chip * width, width, axis=1)
        else:
            g = small_sum[n]
        grads[n] = g.reshape(w[n].shape)

    delta, new_m, new_v = {}, {}, {}
    for n in BIG:
        d, nm, nv = _adamw(w[n][0], grads[n][0], mom[n][0], var[n][0], "adamw_" + n)
        delta[n], new_m[n], new_v[n] = d[None], nm[None], nv[None]
    shapes = [w[n].shape for n in SMALL]
    d, nm, nv = _adamw(_pack_rows([w[n] for n in SMALL]), _pack_rows([grads[n] for n in SMALL]),
                       _pack_rows([mom[n] for n in SMALL]), _pack_rows([var[n] for n in SMALL]), "adamw_small")
    for out, buf in ((delta, d), (new_m, nm), (new_v, nv)):
        out.update(dict(zip(SMALL, _unpack_rows(buf, shapes))))

    return (loss[0, 0], grad_x[None], *[grads[n] for n in WEIGHTS], *[delta[n] for n in WEIGHTS],
            *[new_m[n] for n in WEIGHTS], *[new_v[n] for n in WEIGHTS])
```

```python
import math

import jax
import jax.numpy as jnp
from jax import lax
from jax.experimental import pallas as pl
from jax.experimental.pallas import tpu as pltpu

F32 = jnp.float32
MXU_DTYPE = jnp.bfloat16
WIRE_DTYPE = jnp.bfloat16
EPS = 1e-6
RG_C = 8.0
XA_HEADS = 4
ADAM_LR, ADAM_B1, ADAM_B2, ADAM_EPS, ADAM_WD, ADAM_STEP = 0.001, 0.9, 0.999, 1e-08, 0.01, 10
VMEM_LIMIT_BYTES = 52 * 1024 * 1024
LANES = 1024
N_CHIPS = 4
N_DEV = 8
MESH = pl.DeviceIdType.MESH
GELU_C = math.sqrt(2.0 / math.pi)
GELU_K = 0.044715

WEIGHTS = ['mix_norm_g', 'w_in', 'lru_conv_w', 'lru_conv_b', 'lru_w_a', 'lru_b_a', 'lru_w_x', 'lru_b_x',
           'lru_lambda', 'conf_conv_w', 'conf_conv_b', 'conf_ln_g', 'conf_ln_b', 'w_out', 'xa_norm_g',
           'mem_norm_g', 'w_q', 'w_kv', 'w_o', 'ffn_norm_g', 'w_up', 'ffn_conv_w', 'ffn_conv_b', 'w_down',
           'final_norm_g']
BIG = ['w_in', 'w_kv', 'w_up', 'w_out', 'w_q', 'w_o', 'w_down']
SMALL = [n for n in WEIGHTS if n not in BIG]
COL_SHARDED_SMALL = ['lru_conv_w', 'conf_conv_w', 'ffn_conv_w']


def _params(*semantics):
    return pltpu.CompilerParams(dimension_semantics=semantics, vmem_limit_bytes=VMEM_LIMIT_BYTES)


ANY = pl.BlockSpec(memory_space=pl.ANY)
WHOLE_VMEM = pl.BlockSpec(memory_space=pltpu.VMEM)


class _Comm:
    def __init__(self, arrays, out_shapes, scratch, start, finish, aliases=None, in_specs=None, out_specs=None):
        self.arrays, self.out_shapes, self.scratch = list(arrays), list(out_shapes), list(scratch)
        self.start, self.finish = start, finish
        self.aliases = dict(aliases or {})
        self.in_specs = list(in_specs) if in_specs is not None else [ANY] * len(self.arrays)
        self.out_specs = list(out_specs) if out_specs is not None else [ANY] * len(self.out_shapes)


def _merge(*comms):
    comms = [c for c in comms if c is not None]
    if not comms:
        return None
    ai = [0]
    for c in comms:
        ai.append(ai[-1] + len(c.arrays))
    oi = [0]
    for c in comms:
        oi.append(oi[-1] + len(c.out_shapes))
    si = [0]
    for c in comms:
        si.append(si[-1] + len(c.scratch))

    def each(which):
        def run(ins, outs, scr):
            for k, c in enumerate(comms):
                getattr(c, which)(ins[ai[k]:ai[k + 1]], outs[oi[k]:oi[k + 1]], scr[si[k]:si[k + 1]])
        return run

    aliases = {ai[k] + i: oi[k] + o for k, c in enumerate(comms) for i, o in c.aliases.items()}
    return _Comm(sum((c.arrays for c in comms), []), sum((c.out_shapes for c in comms), []),
                 sum((c.scratch for c in comms), []), each("start"), each("finish"), aliases,
                 sum((c.in_specs for c in comms), []), sum((c.out_specs for c in comms), []))


def _split(outs, *comms):
    parts, at = [], 0
    for c in comms:
        parts.append(outs[at:at + len(c.out_shapes)])
        at += len(c.out_shapes)
    return parts


def _pcall(comm, body, *, name, grid, in_specs, out_specs, out_shape, semantics, scratch_shapes=()):
    single = not isinstance(out_shape, (list, tuple))
    out_shape = [out_shape] if single else list(out_shape)
    out_specs = [out_specs] if single else list(out_specs)
    in_specs, scratch_shapes = list(in_specs), list(scratch_shapes)

    if comm is None:
        def plain(*args):
            return list(pl.pallas_call(body, name=name, grid=grid, in_specs=in_specs, out_specs=out_specs,
                                       out_shape=out_shape, scratch_shapes=scratch_shapes,
                                       compiler_params=_params(*semantics))(*args))
        return plain

    def hosted(*args):
        n_in, n_out, n_scr = len(args), len(out_shape), len(scratch_shapes)
        c_in, c_out = len(comm.arrays), len(comm.out_shapes)

        def wrapped(*refs):
            ins, cins = refs[:n_in], refs[n_in:n_in + c_in]
            o0 = n_in + c_in
            outs, couts = refs[o0:o0 + n_out], refs[o0 + n_out:o0 + n_out + c_out]
            s0 = o0 + n_out + c_out
            scr, cscr = refs[s0:s0 + n_scr], refs[s0 + n_scr:]
            first = last = None
            for axis, size in enumerate(grid):
                at_start, at_end = pl.program_id(axis) == 0, pl.program_id(axis) == size - 1
                first = at_start if first is None else first & at_start
                last = at_end if last is None else last & at_end
            if first is None:
                comm.start(cins, couts, cscr)
                body(*ins, *outs, *scr)
                comm.finish(cins, couts, cscr)
                return
            pl.when(first)(lambda: comm.start(cins, couts, cscr))
            body(*ins, *outs, *scr)
            pl.when(last)(lambda: comm.finish(cins, couts, cscr))

        res = pl.pallas_call(
            wrapped, name=name, grid=grid, in_specs=in_specs + comm.in_specs, out_specs=out_specs + comm.out_specs,
            out_shape=out_shape + comm.out_shapes, scratch_shapes=scratch_shapes + comm.scratch,
            input_output_aliases={n_in + i: n_out + o for i, o in comm.aliases.items()},
            compiler_params=pltpu.CompilerParams(dimension_semantics=("arbitrary",) * len(grid),
                                                 vmem_limit_bytes=VMEM_LIMIT_BYTES, has_side_effects=True),
        )(*args, *comm.arrays)
        return list(res[:n_out]), list(res[n_out:])

    return hosted


def _run_comm(comm, name):
    return _pcall(comm, lambda: None, name=name, grid=(), in_specs=[], out_specs=[], out_shape=[], semantics=())()[1]


def _tile(n, want, align=8):
    if n <= want:
        return n
    for t in range(want - want % align, 0, -align):
        if n % t == 0:
            return t
    raise ValueError((n, want, align))


def _mm(a, b):
    return jnp.dot(a.astype(MXU_DTYPE), b.astype(MXU_DTYPE), preferred_element_type=F32)


def _mm_nt(a, b):
    return lax.dot_general(a.astype(MXU_DTYPE), b.astype(MXU_DTYPE), (((1,), (1,)), ((), ())),
                           preferred_element_type=F32)


def _mm_tn(a, b):
    return lax.dot_general(a.astype(MXU_DTYPE), b.astype(MXU_DTYPE), (((0,), (0,)), ((), ())),
                           preferred_element_type=F32)


def _sigmoid(v):
    return 1.0 / (1.0 + jnp.exp(-v))


def _gelu(v):
    v2 = v * v
    t = jnp.tanh(GELU_C * (v + GELU_K * v * v2))
    return 0.5 * v * (1.0 + t), 0.5 * (1.0 + t) + 0.5 * v * (1.0 - t * t) * GELU_C * (1.0 + 3.0 * GELU_K * v2)


def _softplus_neg(lam):
    e = jnp.exp(-jnp.abs(lam))
    u = 1.0 + e
    log1p_e = jnp.where(u == 1.0, e, jnp.log(u) * e / jnp.where(u == 1.0, 1.0, u - 1.0))
    return jnp.maximum(-lam, 0.0) + log1p_e


def _rms(xv):
    rinv = lax.rsqrt(jnp.mean(xv * xv, axis=-1, keepdims=True) + EPS)
    return rinv, xv * rinv


def _rms_bwd(rinv, xhat, dxhat):
    return rinv * (dxhat - xhat * jnp.mean(dxhat * xhat, axis=-1, keepdims=True))


def _colsum(v):
    return jnp.sum(v, axis=0, keepdims=True)


def _wrow(w_ref, k, wcols):
    return w_ref[pl.ds(k, 1), :] if wcols is None else w_ref[pl.ds(k, 1), wcols]


def _causal_taps(buf_ref, halo, w_ref, taps, rows, wcols=None):
    acc = None
    for s in range(taps):
        term = _wrow(w_ref, taps - 1 - s, wcols) * buf_ref[pl.ds(halo - s, rows), :]
        acc = term if acc is None else acc + term
    return acc


def _anticausal_taps(buf_ref, w_ref, taps, rows, wcols=None):
    acc = None
    for s in range(taps):
        term = _wrow(w_ref, taps - 1 - s, wcols) * buf_ref[pl.ds(s, rows), :]
        acc = term if acc is None else acc + term
    return acc


def _tap_grads(dw_ref, dy, buf_ref, halo, taps, rows, wcols=None):
    for s in range(taps):
        g = _colsum(dy * buf_ref[pl.ds(halo - s, rows), :])
        if wcols is None:
            dw_ref[pl.ds(taps - 1 - s, 1), :] += g
        else:
            dw_ref[pl.ds(taps - 1 - s, 1), wcols] += g


def _fwd_in(x, g, w_in, comm=None):
    S, D = x.shape
    nb, _, C = w_in.shape
    ts = _tile(S, 512)

    def body(x_ref, g_ref, w_ref, z_ref, h_ref):
        _, xhat = _rms(x_ref[...])
        h = (xhat * g_ref[...]).astype(MXU_DTYPE)
        h_ref[...] = h
        for j in range(nb):
            z_ref[j] = jnp.dot(h, w_ref[j], preferred_element_type=F32)

    return _pcall(
        comm, body, name="fwd_in", grid=(S // ts,),
        in_specs=[pl.BlockSpec((ts, D), lambda i: (i, 0)), pl.BlockSpec((1, D), lambda i: (0, 0)),
                  pl.BlockSpec((nb, D, C), lambda i: (0, 0, 0))],
        out_specs=[pl.BlockSpec((nb, ts, C), lambda i: (0, i, 0)), pl.BlockSpec((ts, D), lambda i: (i, 0))],
        out_shape=[jax.ShapeDtypeStruct((nb, S, C), F32), jax.ShapeDtypeStruct((S, D), MXU_DTYPE)],
        semantics=("parallel",))(x, g, w_in)


def _lru_gates(xc, wa_ref, ba_ref, wx_ref, bx_ref, sp):
    xb = xc.astype(MXU_DTYPE)
    r = _sigmoid(jnp.dot(xb, wa_ref[...], preferred_element_type=F32) + ba_ref[...])
    ig = _sigmoid(jnp.dot(xb, wx_ref[...], preferred_element_type=F32) + bx_ref[...])
    log_a = -RG_C * r * sp
    a = jnp.exp(log_a)
    mult = jnp.sqrt(jnp.tanh(-log_a) * (a * a + 1.0))
    return r, ig, a, mult


def _lru_fwd(z, conv_w, conv_b, wa, ba, wx, bx, lam, comm=None):
    _, S, C = z.shape
    ts = _tile(S, 256)
    taps = conv_w.shape[0]
    halo = 8

    def body(zx_ref, zg_ref, cw_ref, cb_ref, wa_ref, ba_ref, wx_ref, bx_ref, lam_ref,
             h_ref, y_ref, xbuf, a_s, u_s, hc):
        i = pl.program_id(0)

        @pl.when(i == 0)
        def _():
            xbuf[pl.ds(0, halo), :] = jnp.zeros((halo, C), F32)
            hc[...] = jnp.zeros_like(hc)

        xbuf[pl.ds(halo, ts), :] = zx_ref[0]
        xc = _causal_taps(xbuf, halo, cw_ref, taps, ts) + cb_ref[...]
        sp = _softplus_neg(lam_ref[...])
        _, ig, a, mult = _lru_gates(xc, wa_ref, ba_ref, wx_ref, bx_ref, sp)
        a_s[...] = a
        u_s[...] = mult * (ig * xc)
        row = lax.broadcasted_iota(jnp.int32, (8, C), 0)

        def step(k, carry):
            off = pl.multiple_of(k * 8, 8)
            av = a_s[pl.ds(off, 8), :]
            uv = u_s[pl.ds(off, 8), :]
            for d in (1, 2, 4):
                m = row >= d
                a_sh = jnp.where(m, pltpu.roll(av, d, 0), 1.0)
                u_sh = jnp.where(m, pltpu.roll(uv, d, 0), 0.0)
                uv = uv + av * u_sh
                av = av * a_sh
            hv = uv + av * carry
            h_ref[pl.ds(off, 8), :] = hv
            return jnp.broadcast_to(hv[7:8, :], (8, C))

        hc[...] = lax.fori_loop(0, ts // 8, step, hc[...])
        ge, _ = _gelu(zg_ref[0])
        y_ref[...] = (h_ref[...] * ge).astype(MXU_DTYPE)
        xbuf[pl.ds(0, halo), :] = xbuf[pl.ds(ts, halo), :]

    vec = pl.BlockSpec((1, C), lambda i: (0, 0))
    mat = pl.BlockSpec((C, C), lambda i: (0, 0))
    return _pcall(
        comm, body, name="lru_fwd", grid=(S // ts,),
        in_specs=[pl.BlockSpec((1, ts, C), lambda i: (0, i, 0)), pl.BlockSpec((1, ts, C), lambda i: (1, i, 0)),
                  pl.BlockSpec((taps, C), lambda i: (0, 0)), vec, mat, vec, mat, vec, vec],
        out_specs=[pl.BlockSpec((ts, C), lambda i: (i, 0)), pl.BlockSpec((ts, C), lambda i: (i, 0))],
        out_shape=[jax.ShapeDtypeStruct((S, C), F32), jax.ShapeDtypeStruct((S, C), MXU_DTYPE)],
        scratch_shapes=[pltpu.VMEM((ts + halo, C), F32), pltpu.VMEM((ts, C), F32), pltpu.VMEM((ts, C), F32),
                        pltpu.VMEM((8, C), F32)],
        semantics=("arbitrary",))(z, z, conv_w, conv_b, wa, ba, wx, bx, lam)


def _layer_norm_stats(c1):
    mu = jnp.mean(c1, axis=-1, keepdims=True)
    xc = c1 - mu
    rstd = lax.rsqrt(jnp.mean(xc * xc, axis=-1, keepdims=True) + EPS)
    return rstd, xc * rstd


def _conf_fwd(z, conv_w, conv_b, ln_g, ln_b, comm=None):
    _, S, C = z.shape
    ts = _tile(S, 256)
    taps = conv_w.shape[0]
    halo = 32

    def body(za_ref, zb_ref, cw_ref, cb_ref, g_ref, b_ref, c1_ref, c3_ref, cbuf):
        i = pl.program_id(0)

        @pl.when(i == 0)
        def _():
            cbuf[pl.ds(0, halo), :] = jnp.zeros((halo, C), F32)

        cbuf[pl.ds(halo, ts), :] = za_ref[0] * _sigmoid(zb_ref[0])
        c1 = _causal_taps(cbuf, halo, cw_ref, taps, ts) + cb_ref[...]
        c1_ref[...] = c1
        _, xhat = _layer_norm_stats(c1)
        c2 = xhat * g_ref[...] + b_ref[...]
        c3_ref[...] = (c2 * _sigmoid(c2)).astype(MXU_DTYPE)
        cbuf[pl.ds(0, halo), :] = cbuf[pl.ds(ts, halo), :]

    vec = pl.BlockSpec((1, C), lambda i: (0, 0))
    return _pcall(
        comm, body, name="conf_fwd", grid=(S // ts,),
        in_specs=[pl.BlockSpec((1, ts, C), lambda i: (2, i, 0)), pl.BlockSpec((1, ts, C), lambda i: (3, i, 0)),
                  pl.BlockSpec((taps, C), lambda i: (0, 0)), vec, vec, vec],
        out_specs=[pl.BlockSpec((ts, C), lambda i: (i, 0)), pl.BlockSpec((ts, C), lambda i: (i, 0))],
        out_shape=[jax.ShapeDtypeStruct((S, C), F32), jax.ShapeDtypeStruct((S, C), MXU_DTYPE)],
        scratch_shapes=[pltpu.VMEM((ts + halo, C), F32)],
        semantics=("arbitrary",))(z, z, conv_w, conv_b, ln_g, ln_b)


def _fwd_out_q(x, y_lru, c3, w_out, g_xa, w_q, comm=None):
    S, D = x.shape
    C = y_lru.shape[1]
    ts = _tile(S, 512)

    def body(x_ref, yl_ref, c3_ref, wo_ref, g_ref, wq_ref, x1_ref, h2_ref, q_ref):
        x1 = (x_ref[...] + jnp.dot(yl_ref[...], wo_ref[0], preferred_element_type=F32)
              + jnp.dot(c3_ref[...], wo_ref[1], preferred_element_type=F32))
        x1_ref[...] = x1
        _, xhat = _rms(x1)
        h2 = (xhat * g_ref[...]).astype(MXU_DTYPE)
        h2_ref[...] = h2
        q_ref[...] = jnp.dot(h2, wq_ref[...], preferred_element_type=F32).astype(MXU_DTYPE)

    row = lambda w: pl.BlockSpec((ts, w), lambda i: (i, 0))
    return _pcall(
        comm, body, name="fwd_out_q", grid=(S // ts,),
        in_specs=[row(D), row(C), row(C), pl.BlockSpec((2, C, D), lambda i: (0, 0, 0)),
                  pl.BlockSpec((1, D), lambda i: (0, 0)), pl.BlockSpec((D, D), lambda i: (0, 0))],
        out_specs=[row(D), row(D), row(D)],
        out_shape=[jax.ShapeDtypeStruct((S, D), F32), jax.ShapeDtypeStruct((S, D), MXU_DTYPE),
                   jax.ShapeDtypeStruct((S, D), MXU_DTYPE)],
        semantics=("parallel",))(x, y_lru, c3, w_out, g_xa, w_q)


def _kv_fwd(mem, g, w_kv):
    M, D = mem.shape
    nb, _, C = w_kv.shape

    def body(mem_ref, g_ref, w_ref, m_ref, kv_ref):
        _, xhat = _rms(mem_ref[...])
        m = (xhat * g_ref[...]).astype(MXU_DTYPE)
        m_ref[...] = m
        for j in range(nb):
            kv_ref[:, pl.ds(j * C, C)] = jnp.dot(m, w_ref[j], preferred_element_type=F32).astype(MXU_DTYPE)

    return pl.pallas_call(
        body, name="kv_fwd", grid=(1,),
        in_specs=[pl.BlockSpec((M, D), lambda i: (0, 0)), pl.BlockSpec((1, D), lambda i: (0, 0)),
                  pl.BlockSpec((nb, D, C), lambda i: (0, 0, 0))],
        out_specs=[pl.BlockSpec((M, D), lambda i: (0, 0)), pl.BlockSpec((M, nb * C), lambda i: (0, 0))],
        out_shape=[jax.ShapeDtypeStruct((M, D), MXU_DTYPE), jax.ShapeDtypeStruct((M, nb * C), MXU_DTYPE)],
        compiler_params=_params("arbitrary"))(mem, g, w_kv)


def _softmax_rows(s):
    e = jnp.exp(s - jnp.max(s, axis=-1, keepdims=True))
    return e / jnp.sum(e, axis=-1, keepdims=True)


def _attn_fwd(q, kv, x1, w_o, g_ffn):
    S, D = x1.shape
    M = kv.shape[0]
    hd = D // XA_HEADS
    scale = hd ** -0.5
    ts = _tile(S, 512)

    def body(q_ref, kv_ref, x1_ref, wo_ref, g_ref, o_ref, x2_ref, h3_ref):
        for h in range(XA_HEADS):
            cols = pl.ds(h * hd, hd)
            p = _softmax_rows(_mm_nt(q_ref[:, cols], kv_ref[:, cols]) * scale)
            o_ref[:, cols] = _mm(p, kv_ref[:, pl.ds(D + h * hd, hd)]).astype(MXU_DTYPE)
        x2 = x1_ref[...] + jnp.dot(o_ref[...], wo_ref[...], preferred_element_type=F32)
        x2_ref[...] = x2
        _, xhat = _rms(x2)
        h3_ref[...] = (xhat * g_ref[...]).astype(MXU_DTYPE)

    row = pl.BlockSpec((ts, D), lambda i: (i, 0))
    return pl.pallas_call(
        body, name="attn_fwd", grid=(S // ts,),
        in_specs=[row, pl.BlockSpec((M, 2 * D), lambda i: (0, 0)), row, pl.BlockSpec((D, D), lambda i: (0, 0)),
                  pl.BlockSpec((1, D), lambda i: (0, 0))],
        out_specs=[row, row, row],
        out_shape=[jax.ShapeDtypeStruct((S, D), MXU_DTYPE), jax.ShapeDtypeStruct((S, D), F32),
                   jax.ShapeDtypeStruct((S, D), MXU_DTYPE)],
        compiler_params=_params("parallel"))(q, kv, x1, w_o, g_ffn)


def _fwd_up(h3, w_up):
    S, D = h3.shape
    nb, _, C = w_up.shape
    ts = _tile(S, 512)

    def body(h_ref, w_ref, o_ref):
        o_ref[0] = jnp.dot(h_ref[...], w_ref[0], preferred_element_type=F32)

    return pl.pallas_call(
        body, name="fwd_up", grid=(nb, S // ts),
        in_specs=[pl.BlockSpec((ts, D), lambda j, i: (i, 0)), pl.BlockSpec((1, D, C), lambda j, i: (j, 0, 0))],
        out_specs=pl.BlockSpec((1, ts, C), lambda j, i: (j, i, 0)),
        out_shape=jax.ShapeDtypeStruct((nb, S, C), F32),
        compiler_params=_params("parallel", "parallel"))(h3, w_up)


def _ffn_down_loss(gu, conv_w, conv_b, w_down, x2, g_final, target):
    nb, S, C = gu.shape
    half = nb // 2
    D = x2.shape[1]
    ts = _tile(S, 256)
    taps = conv_w.shape[0]
    halo = 8
    hb = ts // halo

    def body(g_ref, gh_ref, u_ref, cw_ref, cb_ref, wd_ref, x2_ref, gf_ref, t_ref,
             dx3_ref, loss_ref, dgf_ref, gbuf):
        i = pl.program_id(0)

        @pl.when(i == 0)
        def _():
            loss_ref[...] = jnp.zeros_like(loss_ref)
            dgf_ref[...] = jnp.zeros_like(dgf_ref)

        x3 = x2_ref[...]
        for j in range(half):
            cols = pl.ds(j * C, C)
            gbuf[pl.ds(0, halo), :] = jnp.where(i > 0, gh_ref[j], 0.0)
            gbuf[pl.ds(halo, ts), :] = g_ref[j]
            gc = _causal_taps(gbuf, halo, cw_ref, taps, ts, wcols=cols) + cb_ref[:, cols]
            ge, _ = _gelu(gc)
            act = (ge * u_ref[j]).astype(MXU_DTYPE)
            x3 = x3 + jnp.dot(act, wd_ref[j], preferred_element_type=F32)
        rinv, xhat = _rms(x3)
        gf = gf_ref[...]
        diff = xhat * gf - t_ref[...]
        loss_ref[...] += _colsum(diff * diff) * (0.5 / D)
        dy = diff * (1.0 / D)
        dgf_ref[...] += _colsum(dy * xhat)
        dx3_ref[...] = _rms_bwd(rinv, xhat, dy * gf)

    row = pl.BlockSpec((ts, D), lambda i: (i, 0))
    vecd = pl.BlockSpec((1, D), lambda i: (0, 0))
    return pl.pallas_call(
        body, name="ffn_down_loss", grid=(S // ts,),
        in_specs=[pl.BlockSpec((half, ts, C), lambda i: (0, i, 0)),
                  pl.BlockSpec((half, halo, C), lambda i: (0, jnp.maximum(i * hb - 1, 0), 0)),
                  pl.BlockSpec((half, ts, C), lambda i: (1, i, 0)),
                  pl.BlockSpec((taps, half * C), lambda i: (0, 0)), pl.BlockSpec((1, half * C), lambda i: (0, 0)),
                  pl.BlockSpec((half, C, D), lambda i: (0, 0, 0)), row, vecd, row],
        out_specs=[row, vecd, vecd],
        out_shape=[jax.ShapeDtypeStruct((S, D), F32), jax.ShapeDtypeStruct((1, D), F32),
                   jax.ShapeDtypeStruct((1, D), F32)],
        scratch_shapes=[pltpu.VMEM((ts + halo, C), F32)],
        compiler_params=_params("arbitrary"))(gu, gu, gu, conv_w, conv_b, w_down, x2, g_final, target)


def _ffn_bwd(dx3, w_down, gu, h3, conv_w, conv_b, comm=None):
    nb, S, CW = gu.shape
    half = nb // 2
    D = dx3.shape[1]
    cb = 768
    per = CW // cb
    J = half * per
    ts = _tile(S, 256)
    n = S // ts
    taps = conv_w.shape[0]
    halo = 8
    hb = ts // halo

    def body(dx_ref, h3_ref, wd_ref, gu_ref, gh_ref, cw_ref, cb_ref,
             dgu_ref, dwd_ref, dwup_ref, dcw_ref, dcb_ref, gbuf, dbuf, acc_d, acc_g, acc_u):
        i = pl.program_id(1)
        r = n - 1 - i

        @pl.when(i == 0)
        def _():
            dcw_ref[...] = jnp.zeros_like(dcw_ref)
            dcb_ref[...] = jnp.zeros_like(dcb_ref)
            dbuf[pl.ds(ts, halo), :] = jnp.zeros((halo, cb), F32)

        dxb = dx_ref[...].astype(MXU_DTYPE)
        dact = _mm_nt(dxb, wd_ref[0])
        gbuf[pl.ds(0, halo), :] = jnp.where(r > 0, gh_ref[0, 0], 0.0)
        gbuf[pl.ds(halo, ts), :] = gu_ref[0, 0]
        gc = _causal_taps(gbuf, halo, cw_ref, taps, ts) + cb_ref[...]
        ge, dge = _gelu(gc)
        u = gu_ref[1, 0]
        act = (ge * u).astype(MXU_DTYPE)
        dub = (dact * ge).astype(MXU_DTYPE)
        dgc = dact * u * dge
        dcb_ref[...] += _colsum(dgc)
        dbuf[pl.ds(0, ts), :] = dgc
        _tap_grads(dcw_ref, dgc, gbuf, halo, taps, ts)
        dgb = _anticausal_taps(dbuf, cw_ref, taps, ts).astype(MXU_DTYPE)
        dbuf[pl.ds(ts, halo), :] = dbuf[pl.ds(0, halo), :]
        dgu_ref[0, 0] = dgb
        dgu_ref[1, 0] = dub
        h3v = h3_ref[...]
        parts = (_mm_tn(act, dxb), _mm_tn(h3v, dgb), _mm_tn(h3v, dub))

        @pl.when(i == 0)
        def _():
            for acc, part in zip((acc_d, acc_g, acc_u), parts):
                acc[...] = part

        @pl.when(i > 0)
        def _():
            for acc, part in zip((acc_d, acc_g, acc_u), parts):
                acc[...] += part

        @pl.when(i == n - 1)
        def _():
            dwd_ref[0] = acc_d[...].astype(dwd_ref.dtype)
            dwup_ref[0, 0] = acc_g[...].astype(dwup_ref.dtype)
            dwup_ref[1, 0] = acc_u[...].astype(dwup_ref.dtype)

    gu2 = gu.reshape(2, half, S, CW)
    rev = lambda j, i: (n - 1 - i, 0)
    pair = pl.BlockSpec((2, 1, ts, cb), lambda j, i: (0, j // per, n - 1 - i, j % per))
    g_prev = pl.BlockSpec((1, 1, halo, cb),
                          lambda j, i: (0, j // per, jnp.maximum((n - 1 - i) * hb - 1, 0), j % per))
    tapw = pl.BlockSpec((taps, cb), lambda j, i: (0, j))
    vec = pl.BlockSpec((1, cb), lambda j, i: (0, j))
    wd_blk = pl.BlockSpec((1, cb, D), lambda j, i: (j, 0, 0))
    sds = jax.ShapeDtypeStruct
    res = _pcall(
        comm, body, name="ffn_bwd", grid=(J, n),
        in_specs=[pl.BlockSpec((ts, D), rev), pl.BlockSpec((ts, D), rev), wd_blk, pair, g_prev, tapw, vec],
        out_specs=[pair, wd_blk, pl.BlockSpec((2, 1, D, cb), lambda j, i: (0, j // per, 0, j % per)), tapw, vec],
        out_shape=[sds((2, half, S, CW), MXU_DTYPE), sds((J, cb, D), WIRE_DTYPE),
                   sds((2, half, D, CW), WIRE_DTYPE), sds((taps, half * CW), F32), sds((1, half * CW), F32)],
        scratch_shapes=[pltpu.VMEM((ts + halo, cb), F32), pltpu.VMEM((ts + halo, cb), F32),
                        pltpu.VMEM((cb, D), F32), pltpu.VMEM((D, cb), F32), pltpu.VMEM((D, cb), F32)],
        semantics=("arbitrary", "arbitrary"))(dx3, h3, w_down.reshape(J, cb, D), gu2, gu2, conv_w, conv_b)
    outs = res if comm is None else res[0]
    outs = [outs[0].reshape(nb, S, CW), outs[1], outs[2].reshape(nb, D, CW), outs[3], outs[4]]
    return outs if comm is None else (outs, res[1])


def _bwd_up(dgu, w_up, x2, g_ffn, dx3, comm=None):
    nb, S, C = dgu.shape
    D = x2.shape[1]
    ts = _tile(S, 512)

    def body(d_ref, w_ref, x2_ref, g_ref, dx3_ref, dx2_ref, dg_ref, acc):
        i, j = pl.program_id(0), pl.program_id(1)

        @pl.when((i == 0) & (j == 0))
        def _():
            dg_ref[...] = jnp.zeros_like(dg_ref)

        part = _mm_nt(d_ref[0], w_ref[0])

        @pl.when(j == 0)
        def _():
            acc[...] = part

        @pl.when(j > 0)
        def _():
            acc[...] += part

        @pl.when(j == nb - 1)
        def _():
            rinv, xhat = _rms(x2_ref[...])
            dh = acc[...]
            dg_ref[...] += _colsum(dh * xhat)
            dx2_ref[...] = dx3_ref[...] + _rms_bwd(rinv, xhat, dh * g_ref[...])

    row = pl.BlockSpec((ts, D), lambda i, j: (i, 0))
    vecd = pl.BlockSpec((1, D), lambda i, j: (0, 0))
    return _pcall(
        comm, body, name="bwd_up", grid=(S // ts, nb),
        in_specs=[pl.BlockSpec((1, ts, C), lambda i, j: (j, i, 0)), pl.BlockSpec((1, D, C), lambda i, j: (j, 0, 0)),
                  row, vecd, row],
        out_specs=[row, vecd],
        out_shape=[jax.ShapeDtypeStruct((S, D), F32), jax.ShapeDtypeStruct((1, D), F32)],
        scratch_shapes=[pltpu.VMEM((ts, D), F32)],
        semantics=("arbitrary", "arbitrary"))(dgu, w_up, x2, g_ffn, dx3)


def _attn_bwd(dx2, w_o, q, kv, x1, g_xa, w_q, comm=None):
    S, D = x1.shape
    M = kv.shape[0]
    hd = D // XA_HEADS
    scale = hd ** -0.5
    ts = _tile(S, 512)

    def body(dx2_ref, wo_ref, q_ref, kv_ref, x1_ref, g_ref, wq_ref, dq_ref, dx1_ref, dkv_ref, dg_ref):
        i = pl.program_id(0)

        @pl.when(i == 0)
        def _():
            dkv_ref[...] = jnp.zeros_like(dkv_ref)
            dg_ref[...] = jnp.zeros_like(dg_ref)

        dx2 = dx2_ref[...]
        do = _mm_nt(dx2, wo_ref[...]).astype(MXU_DTYPE)
        for h in range(XA_HEADS):
            cols = pl.ds(h * hd, hd)
            vcols = pl.ds(D + h * hd, hd)
            qh, kh, doh = q_ref[:, cols], kv_ref[:, cols], do[:, h * hd:(h + 1) * hd]
            p = _softmax_rows(_mm_nt(qh, kh) * scale)
            dp = _mm_nt(doh, kv_ref[:, vcols])
            dkv_ref[:, vcols] += _mm_tn(p, doh)
            ds = (p * (dp - jnp.sum(dp * p, axis=-1, keepdims=True)) * scale).astype(MXU_DTYPE)
            dq_ref[:, cols] = _mm(ds, kh).astype(MXU_DTYPE)
            dkv_ref[:, cols] += _mm_tn(ds, qh)
        dh2 = _mm_nt(dq_ref[...], wq_ref[...])
        rinv, xhat = _rms(x1_ref[...])
        dg_ref[...] += _colsum(dh2 * xhat)
        dx1_ref[...] = dx2 + _rms_bwd(rinv, xhat, dh2 * g_ref[...])

    row = pl.BlockSpec((ts, D), lambda i: (i, 0))
    mat = pl.BlockSpec((D, D), lambda i: (0, 0))
    vecd = pl.BlockSpec((1, D), lambda i: (0, 0))
    kvs = pl.BlockSpec((M, 2 * D), lambda i: (0, 0))
    return _pcall(
        comm, body, name="attn_bwd", grid=(S // ts,),
        in_specs=[row, mat, row, kvs, row, vecd, mat],
        out_specs=[row, row, kvs, vecd],
        out_shape=[jax.ShapeDtypeStruct((S, D), MXU_DTYPE), jax.ShapeDtypeStruct((S, D), F32),
                   jax.ShapeDtypeStruct((M, 2 * D), F32), jax.ShapeDtypeStruct((1, D), F32)],
        semantics=("arbitrary",))(dx2, w_o, q, kv, x1, g_xa, w_q)


def _kv_bwd(dkv, w_kv, mem, g, m):
    M, D = mem.shape
    nb, _, C = w_kv.shape

    def body(dkv_ref, w_ref, mem_ref, m_ref, dw_ref, dg_ref):
        dm = jnp.zeros((M, D), F32)
        for j in range(nb):
            dj = dkv_ref[:, pl.ds(j * C, C)].astype(MXU_DTYPE)
            dw_ref[j] = _mm_tn(m_ref[...], dj).astype(dw_ref.dtype)
            dm = dm + _mm_nt(dj, w_ref[j])
        _, xhat = _rms(mem_ref[...])
        dg_ref[...] = _colsum(dm * xhat)

    full = lambda *s: pl.BlockSpec(s, lambda i: (0,) * len(s))
    return pl.pallas_call(
        body, name="kv_bwd", grid=(1,),
        in_specs=[full(M, nb * C), full(nb, D, C), full(M, D), full(M, D)],
        out_specs=[full(nb, D, C), full(1, D)],
        out_shape=[jax.ShapeDtypeStruct((nb, D, C), WIRE_DTYPE), jax.ShapeDtypeStruct((1, D), F32)],
        compiler_params=_params("arbitrary"))(dkv, w_kv, mem, m)


def _conf_bwd(dx1, w_out_c, z, c1, conv_w, ln_g, ln_b, comm=None):
    _, S, C = z.shape
    D = dx1.shape[1]
    ts = _tile(S, 256)
    n = S // ts
    taps = conv_w.shape[0]
    halo = 32
    hb = ts // halo

    def body(dx_ref, wo_ref, za_ref, zb_ref, zah_ref, zbh_ref, c1_ref, cw_ref, g_ref, b_ref,
             dz_ref, dcw_ref, dcb_ref, dlg_ref, dlb_ref, c0buf, dbuf):
        i = pl.program_id(0)
        r = n - 1 - i

        @pl.when(i == 0)
        def _():
            for ref in (dcw_ref, dcb_ref, dlg_ref, dlb_ref):
                ref[...] = jnp.zeros_like(ref)
            dbuf[pl.ds(ts, halo), :] = jnp.zeros((halo, C), F32)

        za = za_ref[0]
        sb = _sigmoid(zb_ref[0])
        c0buf[pl.ds(0, halo), :] = jnp.where(r > 0, zah_ref[0] * _sigmoid(zbh_ref[0]), 0.0)
        c0buf[pl.ds(halo, ts), :] = za * sb
        dc3 = _mm_nt(dx_ref[...], wo_ref[...])
        rstd, xhat = _layer_norm_stats(c1_ref[...])
        g = g_ref[...]
        c2 = xhat * g + b_ref[...]
        sg = _sigmoid(c2)
        dc2 = dc3 * sg * (1.0 + c2 * (1.0 - sg))
        dlg_ref[...] += _colsum(dc2 * xhat)
        dlb_ref[...] += _colsum(dc2)
        dxh = dc2 * g
        dc1 = rstd * (dxh - jnp.mean(dxh, axis=-1, keepdims=True)
                      - xhat * jnp.mean(dxh * xhat, axis=-1, keepdims=True))
        dcb_ref[...] += _colsum(dc1)
        dbuf[pl.ds(0, ts), :] = dc1
        _tap_grads(dcw_ref, dc1, c0buf, halo, taps, ts)
        dc0 = _anticausal_taps(dbuf, cw_ref, taps, ts)
        dz_ref[0] = (dc0 * sb).astype(MXU_DTYPE)
        dz_ref[1] = (dc0 * za * sb * (1.0 - sb)).astype(MXU_DTYPE)
        dbuf[pl.ds(ts, halo), :] = dbuf[pl.ds(0, halo), :]

    vec = pl.BlockSpec((1, C), lambda i: (0, 0))
    tapw = pl.BlockSpec((taps, C), lambda i: (0, 0))
    tile = lambda b: pl.BlockSpec((1, ts, C), lambda i: (b, n - 1 - i, 0))
    prev = lambda b: pl.BlockSpec((1, halo, C), lambda i: (b, jnp.maximum((n - 1 - i) * hb - 1, 0), 0))
    return _pcall(
        comm, body, name="conf_bwd", grid=(n,),
        in_specs=[pl.BlockSpec((ts, D), lambda i: (n - 1 - i, 0)), pl.BlockSpec((C, D), lambda i: (0, 0)),
                  tile(2), tile(3), prev(2), prev(3), pl.BlockSpec((ts, C), lambda i: (n - 1 - i, 0)),
                  tapw, vec, vec],
        out_specs=[pl.BlockSpec((2, ts, C), lambda i: (0, n - 1 - i, 0)), tapw, vec, vec, vec],
        out_shape=[jax.ShapeDtypeStruct((2, S, C), MXU_DTYPE), jax.ShapeDtypeStruct((taps, C), F32),
                   jax.ShapeDtypeStruct((1, C), F32), jax.ShapeDtypeStruct((1, C), F32),
                   jax.ShapeDtypeStruct((1, C), F32)],
        scratch_shapes=[pltpu.VMEM((ts + halo, C), F32), pltpu.VMEM((ts + halo, C), F32)],
        semantics=("arbitrary",))(dx1, w_out_c, z, z, z, z, c1, conv_w, ln_g, ln_b)


def _lru_bwd(dx1, w_out_l, z, h, conv_w, conv_b, wa, ba, wx, bx, lam, comm=None):
    _, S, C = z.shape
    D = dx1.shape[1]
    ts = _tile(S, 256)
    n = S // ts
    taps = conv_w.shape[0]
    halo = 8
    hb = ts // halo

    def body(dx_ref, wo_ref, zx_ref, zxh_ref, zg_ref, h_ref, hh_ref, cw_ref, cb_ref, wa_ref, ba_ref,
             wx_ref, bx_ref, lam_ref,
             dz_ref, dwa_ref, dwx_ref, dba_ref, dbx_ref, dlam_ref, dcw_ref, dcb_ref,
             xbuf, hbuf, a_s, w_s, dh_s, g_s, dbuf, pc):
        i = pl.program_id(0)
        r = n - 1 - i

        @pl.when(i == 0)
        def _():
            for ref in (dwa_ref, dwx_ref, dba_ref, dbx_ref, dlam_ref, dcw_ref, dcb_ref, pc):
                ref[...] = jnp.zeros_like(ref)
            dbuf[pl.ds(ts, halo), :] = jnp.zeros((halo, C), F32)

        xbuf[pl.ds(0, halo), :] = jnp.where(r > 0, zxh_ref[0], 0.0)
        xbuf[pl.ds(halo, ts), :] = zx_ref[0]
        hbuf[pl.ds(0, halo), :] = jnp.where(r > 0, hh_ref[...], 0.0)
        hbuf[pl.ds(halo, ts), :] = h_ref[...]
        xc = _causal_taps(xbuf, halo, cw_ref, taps, ts) + cb_ref[...]
        lam_v = lam_ref[...]
        sp = _softplus_neg(lam_v)
        rg, ig, a, mult = _lru_gates(xc, wa_ref, ba_ref, wx_ref, bx_ref, sp)

        dy = _mm_nt(dx_ref[...], wo_ref[...])
        ge, dge = _gelu(zg_ref[0])
        dh = dy * ge
        dz_ref[1] = (dy * h_ref[...] * dge).astype(MXU_DTYPE)
        a_s[...] = a
        w_s[...] = a * dh
        dh_s[...] = dh
        row = lax.broadcasted_iota(jnp.int32, (8, C), 0)

        def step(kk, carry):
            off = pl.multiple_of((ts // 8 - 1 - kk) * 8, 8)
            av = a_s[pl.ds(off, 8), :]
            wv = w_s[pl.ds(off, 8), :]
            for d in (1, 2, 4):
                m = row < 8 - d
                a_sh = jnp.where(m, pltpu.roll(av, 8 - d, 0), 1.0)
                w_sh = jnp.where(m, pltpu.roll(wv, 8 - d, 0), 0.0)
                wv = wv + av * w_sh
                av = av * a_sh
            pv = wv + av * carry
            g_s[pl.ds(off, 8), :] = dh_s[pl.ds(off, 8), :] + jnp.where(row < 7, pltpu.roll(pv, 7, 0), carry)
            return jnp.broadcast_to(pv[0:1, :], (8, C))

        pc[...] = lax.fori_loop(0, ts // 8, step, pc[...])
        gt = g_s[...]
        da = gt * hbuf[pl.ds(halo - 1, ts), :]
        gm = gt * mult
        dlog_a = da * a - (gt * ig * xc) * (a * a) / mult
        dlam_ref[...] += _colsum(dlog_a * rg) * (RG_C * _sigmoid(-lam_v))
        dpa = (dlog_a * (-RG_C * sp)) * rg * (1.0 - rg)
        dpx = (gm * xc) * ig * (1.0 - ig)
        dba_ref[...] += _colsum(dpa)
        dbx_ref[...] += _colsum(dpx)
        xb = xc.astype(MXU_DTYPE)
        dpab, dpxb = dpa.astype(MXU_DTYPE), dpx.astype(MXU_DTYPE)
        dwa_ref[...] += _mm_tn(xb, dpab)
        dwx_ref[...] += _mm_tn(xb, dpxb)
        dxc = gm * ig + _mm_nt(dpab, wa_ref[...]) + _mm_nt(dpxb, wx_ref[...])
        dcb_ref[...] += _colsum(dxc)
        dbuf[pl.ds(0, ts), :] = dxc
        _tap_grads(dcw_ref, dxc, xbuf, halo, taps, ts)
        dz_ref[0] = _anticausal_taps(dbuf, cw_ref, taps, ts).astype(MXU_DTYPE)
        dbuf[pl.ds(ts, halo), :] = dbuf[pl.ds(0, halo), :]

    vec = pl.BlockSpec((1, C), lambda i: (0, 0))
    mat = pl.BlockSpec((C, C), lambda i: (0, 0))
    tapw = pl.BlockSpec((taps, C), lambda i: (0, 0))
    prev_rows = lambda i: jnp.maximum((n - 1 - i) * hb - 1, 0)
    sds = jax.ShapeDtypeStruct
    return _pcall(
        comm, body, name="lru_bwd", grid=(n,),
        in_specs=[pl.BlockSpec((ts, D), lambda i: (n - 1 - i, 0)), pl.BlockSpec((C, D), lambda i: (0, 0)),
                  pl.BlockSpec((1, ts, C), lambda i: (0, n - 1 - i, 0)),
                  pl.BlockSpec((1, halo, C), lambda i: (0, prev_rows(i), 0)),
                  pl.BlockSpec((1, ts, C), lambda i: (1, n - 1 - i, 0)),
                  pl.BlockSpec((ts, C), lambda i: (n - 1 - i, 0)),
                  pl.BlockSpec((halo, C), lambda i: (prev_rows(i), 0)),
                  tapw, vec, mat, vec, mat, vec, vec],
        out_specs=[pl.BlockSpec((2, ts, C), lambda i: (0, n - 1 - i, 0)), mat, mat, vec, vec, vec, tapw, vec],
        out_shape=[sds((2, S, C), MXU_DTYPE), sds((C, C), F32), sds((C, C), F32), sds((1, C), F32),
                   sds((1, C), F32), sds((1, C), F32), sds((taps, C), F32), sds((1, C), F32)],
        scratch_shapes=[pltpu.VMEM((ts + halo, C), F32), pltpu.VMEM((ts + halo, C), F32)]
        + [pltpu.VMEM((ts, C), F32)] * 4 + [pltpu.VMEM((ts + halo, C), F32), pltpu.VMEM((8, C), F32)],
        semantics=("arbitrary",))(dx1, w_out_l, z, z, z, h, h, conv_w, conv_b, wa, ba, wx, bx, lam)


def _bwd_in(dz_l, dz_c, w_in, x, g, dx1):
    S, D = x.shape
    nb, _, C = w_in.shape
    ts = _tile(S, 512)

    def body(dl_ref, dc_ref, w_ref, x_ref, g_ref, dx1_ref, dx_ref, dg_ref):
        i = pl.program_id(0)

        @pl.when(i == 0)
        def _():
            dg_ref[...] = jnp.zeros_like(dg_ref)

        dh = (_mm_nt(dl_ref[0], w_ref[0]) + _mm_nt(dl_ref[1], w_ref[1])
              + _mm_nt(dc_ref[0], w_ref[2]) + _mm_nt(dc_ref[1], w_ref[3]))
        rinv, xhat = _rms(x_ref[...])
        dg_ref[...] += _colsum(dh * xhat)
        dx_ref[...] = dx1_ref[...] + _rms_bwd(rinv, xhat, dh * g_ref[...])

    row = pl.BlockSpec((ts, D), lambda i: (i, 0))
    pair = pl.BlockSpec((2, ts, C), lambda i: (0, i, 0))
    vecd = pl.BlockSpec((1, D), lambda i: (0, 0))
    return pl.pallas_call(
        body, name="bwd_in", grid=(S // ts,),
        in_specs=[pair, pair, pl.BlockSpec((nb, D, C), lambda i: (0, 0, 0)), row, vecd, row],
        out_specs=[row, vecd],
        out_shape=[jax.ShapeDtypeStruct((S, D), F32), jax.ShapeDtypeStruct((1, D), F32)],
        compiler_params=_params("arbitrary"))(dz_l, dz_c, w_in, x, g, dx1)


def _wgrad(a, b, name, comm=None):
    na, S, K = a.shape
    nb, _, N = b.shape
    nj = max(na, nb)
    assert min(na, nb) == 1
    ts = _tile(S, 1024)
    ns = S // ts

    def body(a_ref, b_ref, o_ref, acc):
        s = pl.program_id(1)
        part = _mm_tn(a_ref[0], b_ref[0])

        @pl.when(s == 0)
        def _():
            acc[...] = part

        @pl.when(s > 0)
        def _():
            acc[...] += part

        @pl.when(s == ns - 1)
        def _():
            o_ref[0] = acc[...].astype(o_ref.dtype)

    res = _pcall(
        comm, body, name=name, grid=(nj, ns),
        in_specs=[pl.BlockSpec((1, ts, K), (lambda j, s: (j, s, 0)) if na > 1 else (lambda j, s: (0, s, 0))),
                  pl.BlockSpec((1, ts, N), (lambda j, s: (j, s, 0)) if nb > 1 else (lambda j, s: (0, s, 0)))],
        out_specs=pl.BlockSpec((1, K, N), lambda j, s: (j, 0, 0)),
        out_shape=jax.ShapeDtypeStruct((nj, K, N), WIRE_DTYPE),
        scratch_shapes=[pltpu.VMEM((K, N), F32)],
        semantics=("parallel", "arbitrary"))(a, b)
    return res[0] if comm is None else (res[0][0], res[1])


def _place():
    x, y, c = lax.axis_index("x"), lax.axis_index("y"), lax.axis_index("c")
    other_chips = [(1 - x, y), (x, 1 - y), (1 - x, 1 - y)]
    return x, y, c, other_chips


def _gather_weights(shards):
    nt = len(shards)

    def body(*refs):
        src, dst = refs[:nt], refs[nt:2 * nt]
        ici_send, ici_recv, d2d_send, d2d_recv, own_send, own_recv = refs[2 * nt:]
        x, y, c, chips = _place()
        mine = 2 * x + y

        def half(t, pc):
            hr = src[t].shape[0] // 2
            return pl.ds(pc * hr, hr)

        def own(t):
            return pltpu.make_async_remote_copy(
                src_ref=src[t], dst_ref=dst[t].at[mine], send_sem=own_send.at[t], recv_sem=own_recv.at[t],
                device_id=(x, y, 1 - c), device_id_type=MESH)

        def ici(t, k, block, to):
            cx, cy = block
            ref = dst[t].at[2 * cx + cy, half(t, c)]
            return pltpu.make_async_remote_copy(
                src_ref=src[t].at[half(t, c)] if to is not None else ref, dst_ref=ref,
                send_sem=ici_send.at[t, k], recv_sem=ici_recv.at[t, k],
                device_id=(*to, c) if to is not None else (x, y, c), device_id_type=MESH)

        def d2d(t, k, block, pc):
            cx, cy = block
            ref = dst[t].at[2 * cx + cy, half(t, pc)]
            return pltpu.make_async_remote_copy(
                src_ref=ref, dst_ref=ref, send_sem=d2d_send.at[t, k], recv_sem=d2d_recv.at[t, k],
                device_id=(x, y, 1 - c), device_id_type=MESH)

        sends = [ici(t, k, (x, y), chip) for t in range(nt) for k, chip in enumerate(chips)]
        sends += [own(t) for t in range(nt)]
        for cp in sends:
            cp.start()
        passed = []
        for t in range(nt):
            for k, chip in enumerate(chips):
                ici(t, k, chip, None).wait_recv()
                fw = d2d(t, k, chip, c)
                fw.start()
                passed.append(fw)
        for t in range(nt):
            own(t).wait_recv()
            for k, chip in enumerate(chips):
                d2d(t, k, chip, 1 - c).wait_recv()
        for cp in sends + passed:
            cp.wait_send()

    return pl.pallas_call(
        body, name="gather_weights",
        in_specs=[ANY] * nt, out_specs=[ANY] * nt,
        out_shape=[jax.ShapeDtypeStruct((N_CHIPS,) + s.shape, s.dtype) for s in shards],
        scratch_shapes=[pltpu.SemaphoreType.DMA((nt, 3))] * 4 + [pltpu.SemaphoreType.DMA((nt,))] * 2,
        compiler_params=pltpu.CompilerParams(has_side_effects=True))(*shards)


def _gather_over_ici(shards):
    nt = len(shards)

    def copies(src, dst, scr, arriving):
        ici_send, ici_recv, own_send, own_recv = scr
        x, y, c, chips = _place()
        out = []
        for t in range(nt):
            hr = src[t].shape[0] // 2
            rows = pl.ds(c * hr, hr)
            for k, (cx, cy) in enumerate(chips):
                block = 2 * cx + cy if arriving else 2 * x + y
                out.append(pltpu.make_async_remote_copy(
                    src_ref=src[t].at[rows], dst_ref=dst[t].at[block, rows],
                    send_sem=ici_send.at[t, k], recv_sem=ici_recv.at[t, k],
                    device_id=(cx, cy, c), device_id_type=MESH))
            out.append(pltpu.make_async_remote_copy(
                src_ref=src[t], dst_ref=dst[t].at[2 * x + y], send_sem=own_send.at[t], recv_sem=own_recv.at[t],
                device_id=(x, y, 1 - c), device_id_type=MESH))
        return out

    def start(src, dst, scr):
        for cp in copies(src, dst, scr, False):
            cp.start()

    def finish(src, dst, scr):
        for cp in copies(src, dst, scr, True):
            cp.wait_recv()
        for cp in copies(src, dst, scr, False):
            cp.wait_send()

    return _Comm(shards, [jax.ShapeDtypeStruct((N_CHIPS,) + s.shape, s.dtype) for s in shards],
                 [pltpu.SemaphoreType.DMA((nt, 3))] * 2 + [pltpu.SemaphoreType.DMA((nt,))] * 2, start, finish)


def _gather_pass_on(bufs):
    nt = len(bufs)

    def passed(dst, scr, t, k, block, pc):
        send, recv = scr
        x, y, c, _ = _place()
        cx, cy = block
        hr = dst[t].shape[1] // 2
        ref = dst[t].at[2 * cx + cy, pl.ds(pc * hr, hr)]
        return pltpu.make_async_remote_copy(src_ref=ref, dst_ref=ref, send_sem=send.at[t, k], recv_sem=recv.at[t, k],
                                            device_id=(x, y, 1 - c), device_id_type=MESH)

    def start(src, dst, scr):
        _, _, c, chips = _place()
        for t in range(nt):
            for k, chip in enumerate(chips):
                passed(dst, scr, t, k, chip, c).start()

    def finish(src, dst, scr):
        _, _, c, chips = _place()
        for t in range(nt):
            for k, chip in enumerate(chips):
                passed(dst, scr, t, k, chip, 1 - c).wait_recv()
        for t in range(nt):
            for k, chip in enumerate(chips):
                passed(dst, scr, t, k, chip, c).wait_send()

    return _Comm(bufs, [jax.ShapeDtypeStruct(b.shape, b.dtype) for b in bufs],
                 [pltpu.SemaphoreType.DMA((nt, 3))] * 2, start, finish, aliases={t: t for t in range(nt)})


def _exchange_halves(grads):
    nt = len(grads)

    def copies(src, dst, scr):
        send, recv = scr
        x, y, c, _ = _place()
        out = []
        for t in range(nt):
            hr = src[t].shape[1] // 2
            out.append(pltpu.make_async_remote_copy(
                src_ref=src[t].at[:, pl.ds((1 - c) * hr, hr)], dst_ref=dst[t],
                send_sem=send.at[t], recv_sem=recv.at[t], device_id=(x, y, 1 - c), device_id_type=MESH))
        return out

    def start(src, dst, scr):
        for cp in copies(src, dst, scr):
            cp.start()

    def finish(src, dst, scr):
        for cp in copies(src, dst, scr):
            cp.wait()

    return _Comm(grads, [jax.ShapeDtypeStruct((g.shape[0], g.shape[1] // 2, g.shape[2]), g.dtype) for g in grads],
                 [pltpu.SemaphoreType.DMA((nt,))] * 2, start, finish)


def _add_halves(grad, other, name):
    nb, R, C = grad.shape
    hr = R // 2
    tr = _tile(hr, 256, 16)
    steps = hr // tr
    c = lax.axis_index("c").astype(jnp.int32).reshape((1,))

    def body(c_ref, a_ref, b_ref, o_ref):
        o_ref[...] = (a_ref[...].astype(F32) + b_ref[...].astype(F32)).astype(o_ref.dtype)

    return pl.pallas_call(
        body, name=name,
        grid_spec=pltpu.PrefetchScalarGridSpec(
            num_scalar_prefetch=1, grid=(nb, steps),
            in_specs=[pl.BlockSpec((1, tr, C), lambda j, i, c_ref: (j, c_ref[0] * steps + i, 0)),
                      pl.BlockSpec((1, tr, C), lambda j, i, c_ref: (j, i, 0))],
            out_specs=pl.BlockSpec((1, tr, C), lambda j, i, c_ref: (j, i, 0))),
        out_shape=jax.ShapeDtypeStruct((nb, hr, C), grad.dtype),
        compiler_params=_params("parallel", "parallel"))(c, grad, other)


def _scatter_chip_sums(parts):
    nt = len(parts)

    def copies(src, dst, scr):
        send, recv = scr
        x, y, c, chips = _place()
        out = []
        for t in range(nt):
            for k, (cx, cy) in enumerate(chips):
                out.append(pltpu.make_async_remote_copy(
                    src_ref=src[t].at[2 * cx + cy], dst_ref=dst[t].at[k],
                    send_sem=send.at[t, k], recv_sem=recv.at[t, k], device_id=(cx, cy, c), device_id_type=MESH))
        return out

    def start(src, dst, scr):
        for cp in copies(src, dst, scr):
            cp.start()

    def finish(src, dst, scr):
        for cp in copies(src, dst, scr):
            cp.wait()

    return _Comm(parts, [jax.ShapeDtypeStruct((3,) + p.shape[1:], p.dtype) for p in parts],
                 [pltpu.SemaphoreType.DMA((nt, 3))] * 2, start, finish)


def _sum_chips(part, recv, name):
    _, hr, C = part.shape
    tr = _tile(hr, 256, 16)
    steps = hr // tr
    where = jnp.stack([2 * lax.axis_index("x") + lax.axis_index("y"), lax.axis_index("c")]).astype(jnp.int32)

    def body(w_ref, a_ref, b_ref, o_ref):
        acc = a_ref[0].astype(F32)
        for k in range(3):
            acc = acc + b_ref[k].astype(F32)
        o_ref[...] = acc

    return pl.pallas_call(
        body, name=name,
        grid_spec=pltpu.PrefetchScalarGridSpec(
            num_scalar_prefetch=1, grid=(steps,),
            in_specs=[pl.BlockSpec((1, tr, C), lambda i, w_ref: (w_ref[0], i, 0)),
                      pl.BlockSpec((3, tr, C), lambda i, w_ref: (0, i, 0))],
            out_specs=pl.BlockSpec((tr, C), lambda i, w_ref: (w_ref[1] * steps + i, 0))),
        out_shape=jax.ShapeDtypeStruct((2 * hr, C), F32),
        compiler_params=_params("parallel"))(where, part, recv)


def _join_halves(bufs):
    nt = len(bufs)

    def swap(dst, scr, t, pc):
        send, recv = scr
        x, y, c, _ = _place()
        hr = dst[t].shape[0] // 2
        rows = dst[t].at[pl.ds(pc * hr, hr)]
        return pltpu.make_async_remote_copy(src_ref=rows, dst_ref=rows, send_sem=send.at[t], recv_sem=recv.at[t],
                                            device_id=(x, y, 1 - c), device_id_type=MESH)

    def start(src, dst, scr):
        c = lax.axis_index("c")
        for t in range(nt):
            swap(dst, scr, t, c).start()

    def finish(src, dst, scr):
        c = lax.axis_index("c")
        for t in range(nt):
            swap(dst, scr, t, 1 - c).wait_recv()
        for t in range(nt):
            swap(dst, scr, t, c).wait_send()

    return _Comm(bufs, [jax.ShapeDtypeStruct(b.shape, b.dtype) for b in bufs],
                 [pltpu.SemaphoreType.DMA((nt,))] * 2, start, finish, aliases={t: t for t in range(nt)})


def _all_reduce_rows(buf, loss_row=None):
    R, L = buf.shape

    def copies(in_ref, gath, send, recv):
        x, y, c, _ = _place()
        out = []
        for k in range(1, N_DEV):
            peer = (x ^ ((k >> 2) & 1), y ^ ((k >> 1) & 1), c ^ (k & 1))
            out.append(pltpu.make_async_remote_copy(
                src_ref=in_ref, dst_ref=gath.at[k], send_sem=send.at[k - 1], recv_sem=recv.at[k - 1],
                device_id=peer, device_id_type=MESH))
        return out

    def start(ins, outs, scr):
        gath, send, recv = scr
        gath[0] = ins[0][...]
        for cp in copies(ins[0], gath, send, recv):
            cp.start()

    def finish(ins, outs, scr):
        gath, send, recv = scr
        for cp in copies(ins[0], gath, send, recv):
            cp.wait()
        x, y, c, _ = _place()
        me = 4 * x + 2 * y + c
        total = gath[me]
        for d in range(1, N_DEV):
            total = total + gath[d ^ me]
        outs[0][...] = total
        if loss_row is not None:
            outs[1][...] = jnp.sum(total[loss_row:loss_row + 1, :], axis=1, keepdims=True)

    out_shape = [jax.ShapeDtypeStruct((R, L), F32)]
    if loss_row is not None:
        out_shape.append(jax.ShapeDtypeStruct((1, 1), F32))
    return _Comm([buf], out_shape,
                 [pltpu.VMEM((N_DEV, R, L), F32), pltpu.SemaphoreType.DMA((N_DEV - 1,)),
                  pltpu.SemaphoreType.DMA((N_DEV - 1,))],
                 start, finish, in_specs=[WHOLE_VMEM], out_specs=[WHOLE_VMEM] * len(out_shape))


def _adamw(w, g, m, v, name):
    R, C = w.shape
    tr = _tile(R, 256)
    c1 = 1.0 - ADAM_B1 ** ADAM_STEP
    c2 = 1.0 - ADAM_B2 ** ADAM_STEP

    def body(w_ref, g_ref, m_ref, v_ref, d_ref, nm_ref, nv_ref):
        gv = g_ref[...]
        nm = ADAM_B1 * m_ref[...] + (1.0 - ADAM_B1) * gv
        nv = ADAM_B2 * v_ref[...] + (1.0 - ADAM_B2) * (gv * gv)
        nm_ref[...] = nm
        nv_ref[...] = nv
        d_ref[...] = -ADAM_LR * ((nm / c1) / (jnp.sqrt(nv / c2) + ADAM_EPS) + ADAM_WD * w_ref[...])

    blk = pl.BlockSpec((tr, C), lambda i: (i, 0))
    return pl.pallas_call(
        body, name=name, grid=(R // tr,), in_specs=[blk] * 4, out_specs=[blk] * 3,
        out_shape=[jax.ShapeDtypeStruct((R, C), F32)] * 3,
        compiler_params=_params("parallel"))(w, g, m, v)


def _pack_rows(arrays):
    rows = []
    for a in arrays:
        flat = a.reshape(-1).astype(F32)
        pad = (-flat.shape[0]) % LANES
        rows.append(jnp.pad(flat, (0, pad)).reshape(-1, LANES))
    buf = jnp.concatenate(rows, axis=0)
    return jnp.pad(buf, ((0, (-buf.shape[0]) % 8), (0, 0)))


def _unpack_rows(buf, shapes):
    out, r = [], 0
    for s in shapes:
        n = math.prod(s)
        nr = -(-n // LANES)
        out.append(buf[r:r + nr].reshape(-1)[:n].reshape(s))
        r += nr
    return out


def _block_diag(w):
    H, a, b = w.shape
    eye = jnp.eye(H, dtype=w.dtype)
    return (eye[:, None, :, None] * w[:, :, None, :]).reshape(H * a, H * b)


def _block_diag_parts(d, H):
    a, b = d.shape[0] // H, d.shape[1] // H
    d4 = d.reshape(H, a, H, b)
    return jnp.stack([d4[h, :, h, :] for h in range(H)])


def _rs_add(names, grads, others):
    return [_add_halves(g, o, "rs_add_halves_" + n) for n, g, o in zip(names, grads, others)]


def _rs_sum(names, parts, recvs):
    return [_sum_chips(p, r, "rs_sum_chips_" + n) for n, p, r in zip(names, parts, recvs)]


def _step(x, mem, target, shards, small, tap_rows, tap_shapes):
    D = x.shape[1]
    nch = N_CHIPS
    p = dict(small)

    (w_in_f,) = _gather_weights([shards['w_in']])
    early, late = ['w_out', 'w_q', 'w_kv', 'w_o'], ['w_up', 'w_down']
    ici_early, taps_sum = _gather_over_ici([shards[n] for n in early]), _all_reduce_rows(tap_rows)
    (z, h1), couts = _fwd_in(x, p['mix_norm_g'], w_in_f, comm=_merge(ici_early, taps_sum))
    early_bufs, (taps,) = _split(couts, ici_early, taps_sum)
    p.update(zip(COL_SHARDED_SMALL, _unpack_rows(taps, tap_shapes)))
    wa_d = _block_diag(p['lru_w_a']).astype(MXU_DTYPE)
    wx_d = _block_diag(p['lru_w_x']).astype(MXU_DTYPE)
    heads = p['lru_w_a'].shape[0]
    (h, y_lru), early_full = _lru_fwd(z, p['lru_conv_w'], p['lru_conv_b'], wa_d, p['lru_b_a'], wx_d, p['lru_b_x'],
                                      p['lru_lambda'], comm=_gather_pass_on(early_bufs))
    wf = dict(zip(early, early_full))
    (c1, c3), late_bufs = _conf_fwd(z, p['conf_conv_w'], p['conf_conv_b'], p['conf_ln_g'], p['conf_ln_b'],
                                    comm=_gather_over_ici([shards[n] for n in late]))
    w_out2 = wf['w_out'].reshape(2, -1, D)
    w_q = wf['w_q'].reshape(D, D)
    w_o = wf['w_o'].reshape(D, D)
    (x1, h2, q), late_full = _fwd_out_q(x, y_lru, c3, w_out2, p['xa_norm_g'], w_q, comm=_gather_pass_on(late_bufs))
    wf.update(zip(late, late_full))
    w_down2 = wf['w_down'].reshape(wf['w_up'].shape[0] // 2, -1, D)

    m, kv = _kv_fwd(mem, p['mem_norm_g'], wf['w_kv'])
    o, x2, h3 = _attn_fwd(q, kv, x1, w_o, p['ffn_norm_g'])
    gu = _fwd_up(h3, wf['w_up'])
    dx3, loss_lanes, d_final_g = _ffn_down_loss(gu, p['ffn_conv_w'], p['ffn_conv_b'], w_down2, x2,
                                                     p['final_norm_g'], target)

    dgu, g_down, g_up, d_ffn_cw, d_ffn_cb = _ffn_bwd(dx3, wf['w_down'], gu, h3, p['ffn_conv_w'], p['ffn_conv_b'])
    (dx2, d_ffn_g), other = _bwd_up(dgu, wf['w_up'], x2, p['ffn_norm_g'], dx3,
                                    comm=_exchange_halves([g_down, g_up]))
    p_down, p_up = _rs_add(['w_down', 'w_up'], [g_down, g_up], other)
    (dq, dx1, dkv, d_xa_g), recv = _attn_bwd(dx2, w_o, q, kv, x1, p['xa_norm_g'], w_q,
                                             comm=_scatter_chip_sums([p_down]))
    f_down = _rs_sum(['w_down'], [p_down], recv)
    mid = ['w_o', 'w_q', 'w_kv']
    g_o = _wgrad(o[None], dx2[None], "wgrad_o").reshape(nch, -1, D)
    g_q = _wgrad(h2[None], dq[None], "wgrad_q").reshape(nch, -1, D)
    g_kv, d_mem_g = _kv_bwd(dkv, wf['w_kv'], mem, p['mem_norm_g'], m)
    join_down, sc_up, ex_mid = _join_halves(f_down), _scatter_chip_sums([p_up]), _exchange_halves([g_o, g_q, g_kv])
    (dz_c, d_conf_cw, d_conf_cb, d_ln_g, d_ln_b), couts = _conf_bwd(
        dx1, w_out2[1], z, c1, p['conf_conv_w'], p['conf_ln_g'], p['conf_ln_b'],
        comm=_merge(join_down, sc_up, ex_mid))
    (r_down,), recv, other = _split(couts, join_down, sc_up, ex_mid)
    p_up = [p_up]
    p_mid = _rs_add(mid, [g_o, g_q, g_kv], other)
    join_up, sc_mid = _join_halves(_rs_sum(['w_up'], p_up, recv)), _scatter_chip_sums(p_mid)
    (dz_l, d_wa, d_wx, d_ba, d_bx, d_lam, d_lru_cw, d_lru_cb), couts = _lru_bwd(
        dx1, w_out2[0], z, h, p['lru_conv_w'], p['lru_conv_b'], wa_d, p['lru_b_a'], wx_d, p['lru_b_x'],
        p['lru_lambda'], comm=_merge(join_up, sc_mid))
    (r_up,), recv = _split(couts, join_up, sc_mid)
    f_mid = _rs_sum(mid, p_mid, recv)
    grad_x, d_mix_g = _bwd_in(dz_l, dz_c, w_in_f, x, p['mix_norm_g'], dx1)
    g_out = jnp.concatenate([_wgrad(y_lru[None], dx1[None], "wgrad_out_lru"),
                             _wgrad(c3[None], dx1[None], "wgrad_out_conf")], axis=0).reshape(nch, -1, D)

    small_g = {'mix_norm_g': d_mix_g, 'lru_conv_w': d_lru_cw, 'lru_conv_b': d_lru_cb,
               'lru_w_a': _block_diag_parts(d_wa, heads), 'lru_b_a': d_ba,
               'lru_w_x': _block_diag_parts(d_wx, heads), 'lru_b_x': d_bx, 'lru_lambda': d_lam,
               'conf_conv_w': d_conf_cw, 'conf_conv_b': d_conf_cb, 'conf_ln_g': d_ln_g, 'conf_ln_b': d_ln_b,
               'xa_norm_g': d_xa_g, 'mem_norm_g': d_mem_g, 'ffn_norm_g': d_ffn_g,
               'ffn_conv_w': d_ffn_cw, 'ffn_conv_b': d_ffn_cb, 'final_norm_g': d_final_g}
    names = list(small_g)
    shapes = [small_g[n].shape for n in names]
    join_mid, ex_out = _join_halves(f_mid), _exchange_halves([g_out])
    small_sum = _all_reduce_rows(_pack_rows([loss_lanes] + [small_g[n] for n in names]), loss_row=0)
    g_in, couts = _wgrad(h1[None], jnp.concatenate([dz_l, dz_c], axis=0), "wgrad_in",
                         comm=_merge(join_mid, ex_out, small_sum))
    r_mid, other, (summed, loss) = _split(couts, join_mid, ex_out, small_sum)

    last = ['w_out', 'w_in']
    p_last = _rs_add(['w_out'], [g_out], other)
    p_last += _rs_add(['w_in'], [g_in], _run_comm(_exchange_halves([g_in]), "rs_exchange_w_in"))
    recv = _run_comm(_scatter_chip_sums(p_last), "rs_scatter_last")
    r_last = _run_comm(_join_halves(_rs_sum(last, p_last, recv)), "rs_join_last")
    big = dict(zip(['w_down', 'w_up'] + mid + last, [r_down, r_up] + r_mid + r_last))
    return grad_x, big, summed, loss, names, [loss_lanes.shape] + shapes


def kernel(x, mem, mix_norm_g, w_in, lru_conv_w, lru_conv_b, lru_w_a, lru_b_a, lru_w_x, lru_b_x, lru_lambda, conf_conv_w, conf_conv_b, conf_ln_g, conf_ln_b, w_out, xa_norm_g, mem_norm_g, w_q, w_kv, w_o, ffn_norm_g, w_up, ffn_conv_w, ffn_conv_b, w_down, final_norm_g, loss_target, m_mix_norm_g, m_w_in, m_lru_conv_w, m_lru_conv_b, m_lru_w_a, m_lru_b_a, m_lru_w_x, m_lru_b_x, m_lru_lambda, m_conf_conv_w, m_conf_conv_b, m_conf_ln_g, m_conf_ln_b, m_w_out, m_xa_norm_g, m_mem_norm_g, m_w_q, m_w_kv, m_w_o, m_ffn_norm_g, m_w_up, m_ffn_conv_w, m_ffn_conv_b, m_w_down, m_final_norm_g, v_mix_norm_g, v_w_in, v_lru_conv_w, v_lru_conv_b, v_lru_w_a, v_lru_b_a, v_lru_w_x, v_lru_b_x, v_lru_lambda, v_conf_conv_w, v_conf_conv_b, v_conf_ln_g, v_conf_ln_b, v_w_out, v_xa_norm_g, v_mem_norm_g, v_w_q, v_w_kv, v_w_o, v_ffn_norm_g, v_w_up, v_ffn_conv_w, v_ffn_conv_b, v_w_down, v_final_norm_g):
    given = dict(locals())
    w = {n: given[n] for n in WEIGHTS}
    mom = {n: given["m_" + n] for n in WEIGHTS}
    var = {n: given["v_" + n] for n in WEIGHTS}
    xi, yi, ci = lax.axis_index("x"), lax.axis_index("y"), lax.axis_index("c")
    chip = 2 * xi + yi

    shards = {n: w[n][0].astype(WIRE_DTYPE) for n in BIG}
    tap_full = []
    for n in COL_SHARDED_SMALL:
        s = w[n][0]
        full = jnp.zeros((s.shape[0], N_CHIPS * s.shape[1]), F32)
        s = jnp.where(ci == 0, s, jnp.zeros_like(s))
        tap_full.append(lax.dynamic_update_slice(full, s, (0, chip * s.shape[1])))
    small = {n: (w[n] if w[n].ndim == 1 else w[n][0]) for n in SMALL if n not in COL_SHARDED_SMALL}
    small = {n: (a.reshape(1, -1) if a.ndim == 1 else a) for n, a in small.items()}

    grad_x, big_g, summed, loss, small_names, packed_shapes = _step(
        x[0], mem[0], loss_target[0], shards, small, _pack_rows(tap_full), [t.shape for t in tap_full])
    small_sum = dict(zip(small_names, _unpack_rows(summed, packed_shapes)[1:]))

    grads = {}
    for n in WEIGHTS:
        if n in BIG:
            g = big_g[n]
        elif n in COL_SHARDED_SMALL:
            width = w[n].shape[-1]
            g = lax.dynamic_slice_in_dim(small_sum[n], chip * width, width, axis=1)
        else:
            g = small_sum[n]
        grads[n] = g.reshape(w[n].shape)

    delta, new_m, new_v = {}, {}, {}
    for n in BIG:
        d, nm, nv = _adamw(w[n][0], grads[n][0], mom[n][0], var[n][0], "adamw_" + n)
        delta[n], new_m[n], new_v[n] = d[None], nm[None], nv[None]
    shapes = [w[n].shape for n in SMALL]
    d, nm, nv = _adamw(_pack_rows([w[n] for n in SMALL]), _pack_rows([grads[n] for n in SMALL]),
                       _pack_rows([mom[n] for n in SMALL]), _pack_rows([var[n] for n in SMALL]), "adamw_small")
    for out, buf in ((delta, d), (new_m, nm), (new_v, nv)):
        out.update(dict(zip(SMALL, _unpack_rows(buf, shapes))))

    return (loss[0, 0], grad_x[None], *[grads[n] for n in WEIGHTS], *[delta[n] for n in WEIGHTS],
            *[new_m[n] for n in WEIGHTS], *[new_v[n] for n in WEIGHTS])
```

```python
import math

import jax
import jax.numpy as jnp
from jax import lax
from jax.experimental import pallas as pl
from jax.experimental.pallas import tpu as pltpu

F32 = jnp.float32
MXU_DTYPE = jnp.bfloat16
WIRE_DTYPE = jnp.bfloat16
EPS = 1e-6
RG_C = 8.0
XA_HEADS = 4
ADAM_LR, ADAM_B1, ADAM_B2, ADAM_EPS, ADAM_WD, ADAM_STEP = 0.001, 0.9, 0.999, 1e-08, 0.01, 10
VMEM_LIMIT_BYTES = 52 * 1024 * 1024
LANES = 1024
N_CHIPS = 4
N_DEV = 8
MESH = pl.DeviceIdType.MESH
GELU_C = math.sqrt(2.0 / math.pi)
GELU_K = 0.044715

WEIGHTS = ['mix_norm_g', 'w_in', 'lru_conv_w', 'lru_conv_b', 'lru_w_a', 'lru_b_a', 'lru_w_x', 'lru_b_x',
           'lru_lambda', 'conf_conv_w', 'conf_conv_b', 'conf_ln_g', 'conf_ln_b', 'w_out', 'xa_norm_g',
           'mem_norm_g', 'w_q', 'w_kv', 'w_o', 'ffn_norm_g', 'w_up', 'ffn_conv_w', 'ffn_conv_b', 'w_down',
           'final_norm_g']
BIG = ['w_in', 'w_kv', 'w_up', 'w_out', 'w_q', 'w_o', 'w_down']
SMALL = [n for n in WEIGHTS if n not in BIG]
COL_SHARDED_SMALL = ['lru_conv_w', 'conf_conv_w', 'ffn_conv_w']


def _params(*semantics):
    return pltpu.CompilerParams(dimension_semantics=semantics, vmem_limit_bytes=VMEM_LIMIT_BYTES)


ANY = pl.BlockSpec(memory_space=pl.ANY)
WHOLE_VMEM = pl.BlockSpec(memory_space=pltpu.VMEM)


class _Comm:
    def __init__(self, arrays, out_shapes, scratch, start, finish, aliases=None, in_specs=None, out_specs=None):
        self.arrays, self.out_shapes, self.scratch = list(arrays), list(out_shapes), list(scratch)
        self.start, self.finish = start, finish
        self.aliases = dict(aliases or {})
        self.in_specs = list(in_specs) if in_specs is not None else [ANY] * len(self.arrays)
        self.out_specs = list(out_specs) if out_specs is not None else [ANY] * len(self.out_shapes)


def _merge(*comms):
    comms = [c for c in comms if c is not None]
    if not comms:
        return None
    ai = [0]
    for c in comms:
        ai.append(ai[-1] + len(c.arrays))
    oi = [0]
    for c in comms:
        oi.append(oi[-1] + len(c.out_shapes))
    si = [0]
    for c in comms:
        si.append(si[-1] + len(c.scratch))

    def each(which):
        def run(ins, outs, scr):
            for k, c in enumerate(comms):
                getattr(c, which)(ins[ai[k]:ai[k + 1]], outs[oi[k]:oi[k + 1]], scr[si[k]:si[k + 1]])
        return run

    aliases = {ai[k] + i: oi[k] + o for k, c in enumerate(comms) for i, o in c.aliases.items()}
    return _Comm(sum((c.arrays for c in comms), []), sum((c.out_shapes for c in comms), []),
                 sum((c.scratch for c in comms), []), each("start"), each("finish"), aliases,
                 sum((c.in_specs for c in comms), []), sum((c.out_specs for c in comms), []))


def _split(outs, *comms):
    parts, at = [], 0
    for c in comms:
        parts.append(outs[at:at + len(c.out_shapes)])
        at += len(c.out_shapes)
    return parts


def _pcall(comm, body, *, name, grid, in_specs, out_specs, out_shape, semantics, scratch_shapes=()):
    single = not isinstance(out_shape, (list, tuple))
    out_shape = [out_shape] if single else list(out_shape)
    out_specs = [out_specs] if single else list(out_specs)
    in_specs, scratch_shapes = list(in_specs), list(scratch_shapes)

    if comm is None:
        def plain(*args):
            return list(pl.pallas_call(body, name=name, grid=grid, in_specs=in_specs, out_specs=out_specs,
                                       out_shape=out_shape, scratch_shapes=scratch_shapes,
                                       compiler_params=_params(*semantics))(*args))
        return plain

    def hosted(*args):
        n_in, n_out, n_scr = len(args), len(out_shape), len(scratch_shapes)
        c_in, c_out = len(comm.arrays), len(comm.out_shapes)

        def wrapped(*refs):
            ins, cins = refs[:n_in], refs[n_in:n_in + c_in]
            o0 = n_in + c_in
            outs, couts = refs[o0:o0 + n_out], refs[o0 + n_out:o0 + n_out + c_out]
            s0 = o0 + n_out + c_out
            scr, cscr = refs[s0:s0 + n_scr], refs[s0 + n_scr:]
            first = last = None
            for axis, size in enumerate(grid):
                at_start, at_end = pl.program_id(axis) == 0, pl.program_id(axis) == size - 1
                first = at_start if first is None else first & at_start
                last = at_end if last is None else last & at_end
            if first is None:
                comm.start(cins, couts, cscr)
                body(*ins, *outs, *scr)
                comm.finish(cins, couts, cscr)
                return
            pl.when(first)(lambda: comm.start(cins, couts, cscr))
            body(*ins, *outs, *scr)
            pl.when(last)(lambda: comm.finish(cins, couts, cscr))

        res = pl.pallas_call(
            wrapped, name=name, grid=grid, in_specs=in_specs + comm.in_specs, out_specs=out_specs + comm.out_specs,
            out_shape=out_shape + comm.out_shapes, scratch_shapes=scratch_shapes + comm.scratch,
            input_output_aliases={n_in + i: n_out + o for i, o in comm.aliases.items()},
            compiler_params=pltpu.CompilerParams(dimension_semantics=("arbitrary",) * len(grid),
                                                 vmem_limit_bytes=VMEM_LIMIT_BYTES, has_side_effects=True),
        )(*args, *comm.arrays)
        return list(res[:n_out]), list(res[n_out:])

    return hosted


def _run_comm(comm, name):
    return _pcall(comm, lambda: None, name=name, grid=(), in_specs=[], out_specs=[], out_shape=[], semantics=())()[1]


def _tile(n, want, align=8):
    if n <= want:
        return n
    for t in range(want - want % align, 0, -align):
        if n % t == 0:
            return t
    raise ValueError((n, want, align))


def _mm(a, b):
    return jnp.dot(a.astype(MXU_DTYPE), b.astype(MXU_DTYPE), preferred_element_type=F32)


def _mm_nt(a, b):
    return lax.dot_general(a.astype(MXU_DTYPE), b.astype(MXU_DTYPE), (((1,), (1,)), ((), ())),
                           preferred_element_type=F32)


def _mm_tn(a, b):
    return lax.dot_general(a.astype(MXU_DTYPE), b.astype(MXU_DTYPE), (((0,), (0,)), ((), ())),
                           preferred_element_type=F32)


def _sigmoid(v):
    return 1.0 / (1.0 + jnp.exp(-v))


def _gelu(v):
    v2 = v * v
    t = jnp.tanh(GELU_C * (v + GELU_K * v * v2))
    return 0.5 * v * (1.0 + t), 0.5 * (1.0 + t) + 0.5 * v * (1.0 - t * t) * GELU_C * (1.0 + 3.0 * GELU_K * v2)


def _softplus_neg(lam):
    e = jnp.exp(-jnp.abs(lam))
    u = 1.0 + e
    log1p_e = jnp.where(u == 1.0, e, jnp.log(u) * e / jnp.where(u == 1.0, 1.0, u - 1.0))
    return jnp.maximum(-lam, 0.0) + log1p_e


def _rms(xv):
    rinv = lax.rsqrt(jnp.mean(xv * xv, axis=-1, keepdims=True) + EPS)
    return rinv, xv * rinv


def _rms_bwd(rinv, xhat, dxhat):
    return rinv * (dxhat - xhat * jnp.mean(dxhat * xhat, axis=-1, keepdims=True))


def _colsum(v):
    return jnp.sum(v, axis=0, keepdims=True)


def _wrow(w_ref, k, wcols):
    return w_ref[pl.ds(k, 1), :] if wcols is None else w_ref[pl.ds(k, 1), wcols]


def _causal_taps(buf_ref, halo, w_ref, taps, rows, wcols=None):
    acc = None
    for s in range(taps):
        term = _wrow(w_ref, taps - 1 - s, wcols) * buf_ref[pl.ds(halo - s, rows), :]
        acc = term if acc is None else acc + term
    return acc


def _anticausal_taps(buf_ref, w_ref, taps, rows, wcols=None):
    acc = None
    for s in range(taps):
        term = _wrow(w_ref, taps - 1 - s, wcols) * buf_ref[pl.ds(s, rows), :]
        acc = term if acc is None else acc + term
    return acc


def _tap_grads(dw_ref, dy, buf_ref, halo, taps, rows, wcols=None):
    for s in range(taps):
        g = _colsum(dy * buf_ref[pl.ds(halo - s, rows), :])
        if wcols is None:
            dw_ref[pl.ds(taps - 1 - s, 1), :] += g
        else:
            dw_ref[pl.ds(taps - 1 - s, 1), wcols] += g


def _shift_copies(dst_ref, buf_ref, rows, up):
    for r in range(8):
        dst_ref[r] = buf_ref[pl.ds(r if up else 8 - r, rows), :]


def _causal_taps8(sh_ref, halo, w_ref, taps, rows):
    acc = None
    for s in range(taps):
        term = _wrow(w_ref, taps - 1 - s, None) * sh_ref[s % 8, pl.ds(halo - 8 - 8 * (s // 8), rows), :]
        acc = term if acc is None else acc + term
    return acc


def _anticausal_taps8(sh_ref, w_ref, taps, rows):
    acc = None
    for s in range(taps):
        term = _wrow(w_ref, taps - 1 - s, None) * sh_ref[s % 8, pl.ds(8 * (s // 8), rows), :]
        acc = term if acc is None else acc + term
    return acc


def _tap_grads8(dw_ref, dy, sh_ref, halo, taps, rows):
    for s in range(taps):
        dw_ref[pl.ds(taps - 1 - s, 1), :] += _colsum(dy * sh_ref[s % 8, pl.ds(halo - 8 - 8 * (s // 8), rows), :])


def _fwd_in(x, g, w_in, comm=None):
    S, D = x.shape
    nb, _, C = w_in.shape
    ts = _tile(S, 512)

    def body(x_ref, g_ref, w_ref, z_ref, h_ref):
        _, xhat = _rms(x_ref[...])
        h = (xhat * g_ref[...]).astype(MXU_DTYPE)
        h_ref[...] = h
        for j in range(nb):
            z_ref[j] = jnp.dot(h, w_ref[j], preferred_element_type=F32)

    return _pcall(
        comm, body, name="fwd_in", grid=(S // ts,),
        in_specs=[pl.BlockSpec((ts, D), lambda i: (i, 0)), pl.BlockSpec((1, D), lambda i: (0, 0)),
                  pl.BlockSpec((nb, D, C), lambda i: (0, 0, 0))],
        out_specs=[pl.BlockSpec((nb, ts, C), lambda i: (0, i, 0)), pl.BlockSpec((ts, D), lambda i: (i, 0))],
        out_shape=[jax.ShapeDtypeStruct((nb, S, C), F32), jax.ShapeDtypeStruct((S, D), MXU_DTYPE)],
        semantics=("parallel",))(x, g, w_in)


def _lru_gates(xc, wa_ref, ba_ref, wx_ref, bx_ref, sp):
    xb = xc.astype(MXU_DTYPE)
    r = _sigmoid(jnp.dot(xb, wa_ref[...], preferred_element_type=F32) + ba_ref[...])
    ig = _sigmoid(jnp.dot(xb, wx_ref[...], preferred_element_type=F32) + bx_ref[...])
    log_a = -RG_C * r * sp
    a = jnp.exp(log_a)
    mult = jnp.sqrt(jnp.tanh(-log_a) * (a * a + 1.0))
    return r, ig, a, mult


def _lru_fwd(z, conv_w, conv_b, wa, ba, wx, bx, lam, comm=None):
    _, S, C = z.shape
    ts = _tile(S, 256)
    taps = conv_w.shape[0]
    halo = 8

    def body(zx_ref, zg_ref, cw_ref, cb_ref, wa_ref, ba_ref, wx_ref, bx_ref, lam_ref,
             h_ref, y_ref, xbuf, a_s, u_s, hc):
        i = pl.program_id(0)

        @pl.when(i == 0)
        def _():
            xbuf[pl.ds(0, halo), :] = jnp.zeros((halo, C), F32)
            hc[...] = jnp.zeros_like(hc)

        xbuf[pl.ds(halo, ts), :] = zx_ref[0]
        xc = _causal_taps(xbuf, halo, cw_ref, taps, ts) + cb_ref[...]
        sp = _softplus_neg(lam_ref[...])
        _, ig, a, mult = _lru_gates(xc, wa_ref, ba_ref, wx_ref, bx_ref, sp)
        a_s[...] = a
        u_s[...] = mult * (ig * xc)
        row = lax.broadcasted_iota(jnp.int32, (8, C), 0)

        def step(k, carry):
            off = pl.multiple_of(k * 8, 8)
            av = a_s[pl.ds(off, 8), :]
            uv = u_s[pl.ds(off, 8), :]
            for d in (1, 2, 4):
                m = row >= d
                a_sh = jnp.where(m, pltpu.roll(av, d, 0), 1.0)
                u_sh = jnp.where(m, pltpu.roll(uv, d, 0), 0.0)
                uv = uv + av * u_sh
                av = av * a_sh
            hv = uv + av * carry
            h_ref[pl.ds(off, 8), :] = hv
            return jnp.broadcast_to(hv[7:8, :], (8, C))

        hc[...] = lax.fori_loop(0, ts // 8, step, hc[...])
        ge, _ = _gelu(zg_ref[0])
        y_ref[...] = (h_ref[...] * ge).astype(MXU_DTYPE)
        xbuf[pl.ds(0, halo), :] = xbuf[pl.ds(ts, halo), :]

    vec = pl.BlockSpec((1, C), lambda i: (0, 0))
    mat = pl.BlockSpec((C, C), lambda i: (0, 0))
    return _pcall(
        comm, body, name="lru_fwd", grid=(S // ts,),
        in_specs=[pl.BlockSpec((1, ts, C), lambda i: (0, i, 0)), pl.BlockSpec((1, ts, C), lambda i: (1, i, 0)),
                  pl.BlockSpec((taps, C), lambda i: (0, 0)), vec, mat, vec, mat, vec, vec],
        out_specs=[pl.BlockSpec((ts, C), lambda i: (i, 0)), pl.BlockSpec((ts, C), lambda i: (i, 0))],
        out_shape=[jax.ShapeDtypeStruct((S, C), F32), jax.ShapeDtypeStruct((S, C), MXU_DTYPE)],
        scratch_shapes=[pltpu.VMEM((ts + halo, C), F32), pltpu.VMEM((ts, C), F32), pltpu.VMEM((ts, C), F32),
                        pltpu.VMEM((8, C), F32)],
        semantics=("arbitrary",))(z, z, conv_w, conv_b, wa, ba, wx, bx, lam)


def _layer_norm_stats(c1):
    mu = jnp.mean(c1, axis=-1, keepdims=True)
    xc = c1 - mu
    rstd = lax.rsqrt(jnp.mean(xc * xc, axis=-1, keepdims=True) + EPS)
    return rstd, xc * rstd


def _conf_fwd(z, conv_w, conv_b, ln_g, ln_b, comm=None):
    _, S, C = z.shape
    ts = _tile(S, 256)
    taps = conv_w.shape[0]
    halo = 32

    def body(za_ref, zb_ref, cw_ref, cb_ref, g_ref, b_ref, c1_ref, c3_ref, cbuf, shifted):
        i = pl.program_id(0)

        @pl.when(i == 0)
        def _():
            cbuf[pl.ds(0, halo), :] = jnp.zeros((halo, C), F32)

        cbuf[pl.ds(halo, ts), :] = za_ref[0] * _sigmoid(zb_ref[0])
        _shift_copies(shifted, cbuf, ts + halo - 8, up=False)
        c1 = _causal_taps8(shifted, halo, cw_ref, taps, ts) + cb_ref[...]
        c1_ref[...] = c1
        _, xhat = _layer_norm_stats(c1)
        c2 = xhat * g_ref[...] + b_ref[...]
        c3_ref[...] = (c2 * _sigmoid(c2)).astype(MXU_DTYPE)
        cbuf[pl.ds(0, halo), :] = cbuf[pl.ds(ts, halo), :]

    vec = pl.BlockSpec((1, C), lambda i: (0, 0))
    return _pcall(
        comm, body, name="conf_fwd", grid=(S // ts,),
        in_specs=[pl.BlockSpec((1, ts, C), lambda i: (2, i, 0)), pl.BlockSpec((1, ts, C), lambda i: (3, i, 0)),
                  pl.BlockSpec((taps, C), lambda i: (0, 0)), vec, vec, vec],
        out_specs=[pl.BlockSpec((ts, C), lambda i: (i, 0)), pl.BlockSpec((ts, C), lambda i: (i, 0))],
        out_shape=[jax.ShapeDtypeStruct((S, C), F32), jax.ShapeDtypeStruct((S, C), MXU_DTYPE)],
        scratch_shapes=[pltpu.VMEM((ts + halo, C), F32), pltpu.VMEM((8, ts + halo - 8, C), F32)],
        semantics=("arbitrary",))(z, z, conv_w, conv_b, ln_g, ln_b)


def _fwd_out_q(x, y_lru, c3, w_out, g_xa, w_q, comm=None):
    S, D = x.shape
    C = y_lru.shape[1]
    ts = _tile(S, 512)

    def body(x_ref, yl_ref, c3_ref, wo_ref, g_ref, wq_ref, x1_ref, h2_ref, q_ref):
        x1 = (x_ref[...] + jnp.dot(yl_ref[...], wo_ref[0], preferred_element_type=F32)
              + jnp.dot(c3_ref[...], wo_ref[1], preferred_element_type=F32))
        x1_ref[...] = x1
        _, xhat = _rms(x1)
        h2 = (xhat * g_ref[...]).astype(MXU_DTYPE)
        h2_ref[...] = h2
        q_ref[...] = jnp.dot(h2, wq_ref[...], preferred_element_type=F32).astype(MXU_DTYPE)

    row = lambda w: pl.BlockSpec((ts, w), lambda i: (i, 0))
    return _pcall(
        comm, body, name="fwd_out_q", grid=(S // ts,),
        in_specs=[row(D), row(C), row(C), pl.BlockSpec((2, C, D), lambda i: (0, 0, 0)),
                  pl.BlockSpec((1, D), lambda i: (0, 0)), pl.BlockSpec((D, D), lambda i: (0, 0))],
        out_specs=[row(D), row(D), row(D)],
        out_shape=[jax.ShapeDtypeStruct((S, D), F32), jax.ShapeDtypeStruct((S, D), MXU_DTYPE),
                   jax.ShapeDtypeStruct((S, D), MXU_DTYPE)],
        semantics=("parallel",))(x, y_lru, c3, w_out, g_xa, w_q)


def _kv_fwd(mem, g, w_kv):
    M, D = mem.shape
    nb, _, C = w_kv.shape

    def body(mem_ref, g_ref, w_ref, m_ref, kv_ref):
        _, xhat = _rms(mem_ref[...])
        m = (xhat * g_ref[...]).astype(MXU_DTYPE)
        m_ref[...] = m
        for j in range(nb):
            kv_ref[:, pl.ds(j * C, C)] = jnp.dot(m, w_ref[j], preferred_element_type=F32).astype(MXU_DTYPE)

    return pl.pallas_call(
        body, name="kv_fwd", grid=(1,),
        in_specs=[pl.BlockSpec((M, D), lambda i: (0, 0)), pl.BlockSpec((1, D), lambda i: (0, 0)),
                  pl.BlockSpec((nb, D, C), lambda i: (0, 0, 0))],
        out_specs=[pl.BlockSpec((M, D), lambda i: (0, 0)), pl.BlockSpec((M, nb * C), lambda i: (0, 0))],
        out_shape=[jax.ShapeDtypeStruct((M, D), MXU_DTYPE), jax.ShapeDtypeStruct((M, nb * C), MXU_DTYPE)],
        compiler_params=_params("arbitrary"))(mem, g, w_kv)


def _softmax_rows(s):
    e = jnp.exp(s - jnp.max(s, axis=-1, keepdims=True))
    return e / jnp.sum(e, axis=-1, keepdims=True)


def _attn_fwd(q, kv, x1, w_o, g_ffn):
    S, D = x1.shape
    M = kv.shape[0]
    hd = D // XA_HEADS
    scale = hd ** -0.5
    ts = _tile(S, 512)

    def body(q_ref, kv_ref, x1_ref, wo_ref, g_ref, o_ref, x2_ref, h3_ref):
        for h in range(XA_HEADS):
            cols = pl.ds(h * hd, hd)
            p = _softmax_rows(_mm_nt(q_ref[:, cols], kv_ref[:, cols]) * scale)
            o_ref[:, cols] = _mm(p, kv_ref[:, pl.ds(D + h * hd, hd)]).astype(MXU_DTYPE)
        x2 = x1_ref[...] + jnp.dot(o_ref[...], wo_ref[...], preferred_element_type=F32)
        x2_ref[...] = x2
        _, xhat = _rms(x2)
        h3_ref[...] = (xhat * g_ref[...]).astype(MXU_DTYPE)

    row = pl.BlockSpec((ts, D), lambda i: (i, 0))
    return pl.pallas_call(
        body, name="attn_fwd", grid=(S // ts,),
        in_specs=[row, pl.BlockSpec((M, 2 * D), lambda i: (0, 0)), row, pl.BlockSpec((D, D), lambda i: (0, 0)),
                  pl.BlockSpec((1, D), lambda i: (0, 0))],
        out_specs=[row, row, row],
        out_shape=[jax.ShapeDtypeStruct((S, D), MXU_DTYPE), jax.ShapeDtypeStruct((S, D), F32),
                   jax.ShapeDtypeStruct((S, D), MXU_DTYPE)],
        compiler_params=_params("parallel"))(q, kv, x1, w_o, g_ffn)


def _fwd_up(h3, w_up):
    S, D = h3.shape
    nb, _, C = w_up.shape
    ts = _tile(S, 512)

    def body(h_ref, w_ref, o_ref):
        o_ref[0] = jnp.dot(h_ref[...], w_ref[0], preferred_element_type=F32)

    return pl.pallas_call(
        body, name="fwd_up", grid=(nb, S // ts),
        in_specs=[pl.BlockSpec((ts, D), lambda j, i: (i, 0)), pl.BlockSpec((1, D, C), lambda j, i: (j, 0, 0))],
        out_specs=pl.BlockSpec((1, ts, C), lambda j, i: (j, i, 0)),
        out_shape=jax.ShapeDtypeStruct((nb, S, C), F32),
        compiler_params=_params("parallel", "parallel"))(h3, w_up)


def _ffn_down_loss(gu, conv_w, conv_b, w_down, x2, g_final, target):
    nb, S, C = gu.shape
    half = nb // 2
    D = x2.shape[1]
    ts = _tile(S, 256)
    taps = conv_w.shape[0]
    halo = 8
    hb = ts // halo

    def body(g_ref, gh_ref, u_ref, cw_ref, cb_ref, wd_ref, x2_ref, gf_ref, t_ref,
             act_ref, dx3_ref, loss_ref, dgf_ref, gbuf):
        i = pl.program_id(0)

        @pl.when(i == 0)
        def _():
            loss_ref[...] = jnp.zeros_like(loss_ref)
            dgf_ref[...] = jnp.zeros_like(dgf_ref)

        x3 = x2_ref[...]
        for j in range(half):
            cols = pl.ds(j * C, C)
            gbuf[pl.ds(0, halo), :] = jnp.where(i > 0, gh_ref[j], 0.0)
            gbuf[pl.ds(halo, ts), :] = g_ref[j]
            gc = _causal_taps(gbuf, halo, cw_ref, taps, ts, wcols=cols) + cb_ref[:, cols]
            ge, _ = _gelu(gc)
            act = (ge * u_ref[j]).astype(MXU_DTYPE)
            act_ref[j] = act
            x3 = x3 + jnp.dot(act, wd_ref[j], preferred_element_type=F32)
        rinv, xhat = _rms(x3)
        gf = gf_ref[...]
        diff = xhat * gf - t_ref[...]
        loss_ref[...] += _colsum(diff * diff) * (0.5 / D)
        dy = diff * (1.0 / D)
        dgf_ref[...] += _colsum(dy * xhat)
        dx3_ref[...] = _rms_bwd(rinv, xhat, dy * gf)

    row = pl.BlockSpec((ts, D), lambda i: (i, 0))
    vecd = pl.BlockSpec((1, D), lambda i: (0, 0))
    return pl.pallas_call(
        body, name="ffn_down_loss", grid=(S // ts,),
        in_specs=[pl.BlockSpec((half, ts, C), lambda i: (0, i, 0)),
                  pl.BlockSpec((half, halo, C), lambda i: (0, jnp.maximum(i * hb - 1, 0), 0)),
                  pl.BlockSpec((half, ts, C), lambda i: (1, i, 0)),
                  pl.BlockSpec((taps, half * C), lambda i: (0, 0)), pl.BlockSpec((1, half * C), lambda i: (0, 0)),
                  pl.BlockSpec((half, C, D), lambda i: (0, 0, 0)), row, vecd, row],
        out_specs=[pl.BlockSpec((half, ts, C), lambda i: (0, i, 0)), row, vecd, vecd],
        out_shape=[jax.ShapeDtypeStruct((half, S, C), MXU_DTYPE), jax.ShapeDtypeStruct((S, D), F32),
                   jax.ShapeDtypeStruct((1, D), F32), jax.ShapeDtypeStruct((1, D), F32)],
        scratch_shapes=[pltpu.VMEM((ts + halo, C), F32)],
        compiler_params=_params("arbitrary"))(gu, gu, gu, conv_w, conv_b, w_down, x2, g_final, target)


def _ffn_bwd(dx3, w_down, w_up, gu, x2, g_ffn, conv_w, conv_b, comm=None):
    nb, S, CW = gu.shape
    half = nb // 2
    D = dx3.shape[1]
    cb = 768
    per = CW // cb
    J = half * per
    ts = _tile(S, 256)
    n = S // ts
    taps = conv_w.shape[0]
    halo = 8
    hb = ts // halo

    def body(dx_ref, x2_ref, gf_ref, wd_ref, wup_ref, gu_ref, gh_ref, cw_ref, cb_ref,
             dgu_ref, dx2_ref, dgf_ref, dcw_ref, dcb_ref, gbuf, dbuf):
        i = pl.program_id(0)
        r = n - 1 - i

        @pl.when(i == 0)
        def _():
            for ref in (dgf_ref, dcw_ref, dcb_ref, dbuf):
                ref[...] = jnp.zeros_like(ref)

        dx3v = dx_ref[...]
        dxb = dx3v.astype(MXU_DTYPE)
        dh = None
        for j in range(J):
            b, cols, wcols = j // per, pl.ds((j % per) * cb, cb), pl.ds(j * cb, cb)
            dact = _mm_nt(dxb, wd_ref[j])
            gbuf[pl.ds(0, halo), :] = jnp.where(r > 0, gh_ref[0, b, :, cols], 0.0)
            gbuf[pl.ds(halo, ts), :] = gu_ref[0, b, :, cols]
            gc = _causal_taps(gbuf, halo, cw_ref, taps, ts, wcols=wcols) + cb_ref[:, wcols]
            ge, dge = _gelu(gc)
            dub = (dact * ge).astype(MXU_DTYPE)
            dgc = dact * gu_ref[1, b, :, cols] * dge
            dcb_ref[:, wcols] += _colsum(dgc)
            dbuf[j, pl.ds(0, ts), :] = dgc
            _tap_grads(dcw_ref, dgc, gbuf, halo, taps, ts, wcols=wcols)
            dgb = _anticausal_taps(dbuf.at[j], cw_ref, taps, ts, wcols=wcols).astype(MXU_DTYPE)
            dbuf[j, pl.ds(ts, halo), :] = dbuf[j, pl.ds(0, halo), :]
            dgu_ref[0, b, :, cols] = dgb
            dgu_ref[1, b, :, cols] = dub
            part = _mm_nt(dgb, wup_ref[b, :, cols]) + _mm_nt(dub, wup_ref[half + b, :, cols])
            dh = part if dh is None else dh + part
        rinv, xhat = _rms(x2_ref[...])
        dgf_ref[...] += _colsum(dh * xhat)
        dx2_ref[...] = dx3v + _rms_bwd(rinv, xhat, dh * gf_ref[...])

    gu2 = gu.reshape(2, half, S, CW)
    row = pl.BlockSpec((ts, D), lambda i: (n - 1 - i, 0))
    vecd = pl.BlockSpec((1, D), lambda i: (0, 0))
    pair = pl.BlockSpec((2, half, ts, CW), lambda i: (0, 0, n - 1 - i, 0))
    g_prev = pl.BlockSpec((1, half, halo, CW), lambda i: (0, 0, jnp.maximum((n - 1 - i) * hb - 1, 0), 0))
    tapw = pl.BlockSpec((taps, half * CW), lambda i: (0, 0))
    vec = pl.BlockSpec((1, half * CW), lambda i: (0, 0))
    once = pl.Buffered(1)
    sds = jax.ShapeDtypeStruct
    res = _pcall(
        comm, body, name="ffn_bwd", grid=(n,),
        in_specs=[row, row, vecd, pl.BlockSpec((J, cb, D), lambda i: (0, 0, 0), pipeline_mode=once),
                  pl.BlockSpec((nb, D, CW), lambda i: (0, 0, 0), pipeline_mode=once), pair, g_prev, tapw, vec],
        out_specs=[pair, row, vecd, tapw, vec],
        out_shape=[sds((2, half, S, CW), MXU_DTYPE), sds((S, D), F32), sds((1, D), F32),
                   sds((taps, half * CW), F32), sds((1, half * CW), F32)],
        scratch_shapes=[pltpu.VMEM((ts + halo, cb), F32), pltpu.VMEM((J, ts + halo, cb), F32)],
        semantics=("arbitrary",))(dx3, x2, g_ffn, w_down.reshape(J, cb, D), w_up, gu2, gu2, conv_w, conv_b)
    outs = res if comm is None else res[0]
    outs = [outs[0].reshape(nb, S, CW)] + list(outs[1:])
    return outs if comm is None else (outs, res[1])


def _attn_bwd(dx2, w_o, q, kv, x1, g_xa, w_q, comm=None):
    S, D = x1.shape
    M = kv.shape[0]
    hd = D // XA_HEADS
    scale = hd ** -0.5
    ts = _tile(S, 512)

    def body(dx2_ref, wo_ref, q_ref, kv_ref, x1_ref, g_ref, wq_ref, dq_ref, dx1_ref, dkv_ref, dg_ref):
        i = pl.program_id(0)

        @pl.when(i == 0)
        def _():
            dkv_ref[...] = jnp.zeros_like(dkv_ref)
            dg_ref[...] = jnp.zeros_like(dg_ref)

        dx2 = dx2_ref[...]
        do = _mm_nt(dx2, wo_ref[...]).astype(MXU_DTYPE)
        for h in range(XA_HEADS):
            cols = pl.ds(h * hd, hd)
            vcols = pl.ds(D + h * hd, hd)
            qh, kh, doh = q_ref[:, cols], kv_ref[:, cols], do[:, h * hd:(h + 1) * hd]
            p = _softmax_rows(_mm_nt(qh, kh) * scale)
            dp = _mm_nt(doh, kv_ref[:, vcols])
            dkv_ref[:, vcols] += _mm_tn(p, doh)
            ds = (p * (dp - jnp.sum(dp * p, axis=-1, keepdims=True)) * scale).astype(MXU_DTYPE)
            dq_ref[:, cols] = _mm(ds, kh).astype(MXU_DTYPE)
            dkv_ref[:, cols] += _mm_tn(ds, qh)
        dh2 = _mm_nt(dq_ref[...], wq_ref[...])
        rinv, xhat = _rms(x1_ref[...])
        dg_ref[...] += _colsum(dh2 * xhat)
        dx1_ref[...] = dx2 + _rms_bwd(rinv, xhat, dh2 * g_ref[...])

    row = pl.BlockSpec((ts, D), lambda i: (i, 0))
    mat = pl.BlockSpec((D, D), lambda i: (0, 0))
    vecd = pl.BlockSpec((1, D), lambda i: (0, 0))
    kvs = pl.BlockSpec((M, 2 * D), lambda i: (0, 0))
    return _pcall(
        comm, body, name="attn_bwd", grid=(S // ts,),
        in_specs=[row, mat, row, kvs, row, vecd, mat],
        out_specs=[row, row, kvs, vecd],
        out_shape=[jax.ShapeDtypeStruct((S, D), MXU_DTYPE), jax.ShapeDtypeStruct((S, D), F32),
                   jax.ShapeDtypeStruct((M, 2 * D), F32), jax.ShapeDtypeStruct((1, D), F32)],
        semantics=("arbitrary",))(dx2, w_o, q, kv, x1, g_xa, w_q)


def _kv_bwd(dkv, w_kv, mem, g, m):
    M, D = mem.shape
    nb, _, C = w_kv.shape

    def body(dkv_ref, w_ref, mem_ref, m_ref, dw_ref, dg_ref):
        dm = jnp.zeros((M, D), F32)
        for j in range(nb):
            dj = dkv_ref[:, pl.ds(j * C, C)].astype(MXU_DTYPE)
            dw_ref[j] = _mm_tn(m_ref[...], dj).astype(dw_ref.dtype)
            dm = dm + _mm_nt(dj, w_ref[j])
        _, xhat = _rms(mem_ref[...])
        dg_ref[...] = _colsum(dm * xhat)

    full = lambda *s: pl.BlockSpec(s, lambda i: (0,) * len(s))
    return pl.pallas_call(
        body, name="kv_bwd", grid=(1,),
        in_specs=[full(M, nb * C), full(nb, D, C), full(M, D), full(M, D)],
        out_specs=[full(nb, D, C), full(1, D)],
        out_shape=[jax.ShapeDtypeStruct((nb, D, C), WIRE_DTYPE), jax.ShapeDtypeStruct((1, D), F32)],
        compiler_params=_params("arbitrary"))(dkv, w_kv, mem, m)


def _conf_bwd(dx1, w_out_c, z, c1, conv_w, ln_g, ln_b, comm=None):
    _, S, C = z.shape
    D = dx1.shape[1]
    ts = _tile(S, 256)
    n = S // ts
    taps = conv_w.shape[0]
    halo = 32
    hb = ts // halo

    def body(dx_ref, wo_ref, za_ref, zb_ref, zah_ref, zbh_ref, c1_ref, cw_ref, g_ref, b_ref,
             dz_ref, dcw_ref, dcb_ref, dlg_ref, dlb_ref, c0buf, dbuf, shifted):
        i = pl.program_id(0)
        r = n - 1 - i

        @pl.when(i == 0)
        def _():
            for ref in (dcw_ref, dcb_ref, dlg_ref, dlb_ref):
                ref[...] = jnp.zeros_like(ref)
            dbuf[pl.ds(ts, halo), :] = jnp.zeros((halo, C), F32)

        za = za_ref[0]
        sb = _sigmoid(zb_ref[0])
        c0buf[pl.ds(0, halo), :] = jnp.where(r > 0, zah_ref[0] * _sigmoid(zbh_ref[0]), 0.0)
        c0buf[pl.ds(halo, ts), :] = za * sb
        dc3 = _mm_nt(dx_ref[...], wo_ref[...])
        rstd, xhat = _layer_norm_stats(c1_ref[...])
        g = g_ref[...]
        c2 = xhat * g + b_ref[...]
        sg = _sigmoid(c2)
        dc2 = dc3 * sg * (1.0 + c2 * (1.0 - sg))
        dlg_ref[...] += _colsum(dc2 * xhat)
        dlb_ref[...] += _colsum(dc2)
        dxh = dc2 * g
        dc1 = rstd * (dxh - jnp.mean(dxh, axis=-1, keepdims=True)
                      - xhat * jnp.mean(dxh * xhat, axis=-1, keepdims=True))
        dcb_ref[...] += _colsum(dc1)
        dbuf[pl.ds(0, ts), :] = dc1
        _shift_copies(shifted, c0buf, ts + halo - 8, up=False)
        _tap_grads8(dcw_ref, dc1, shifted, halo, taps, ts)
        _shift_copies(shifted, dbuf, ts + halo - 8, up=True)
        dc0 = _anticausal_taps8(shifted, cw_ref, taps, ts)
        dz_ref[0] = (dc0 * sb).astype(MXU_DTYPE)
        dz_ref[1] = (dc0 * za * sb * (1.0 - sb)).astype(MXU_DTYPE)
        dbuf[pl.ds(ts, halo), :] = dbuf[pl.ds(0, halo), :]

    vec = pl.BlockSpec((1, C), lambda i: (0, 0))
    tapw = pl.BlockSpec((taps, C), lambda i: (0, 0))
    tile = lambda b: pl.BlockSpec((1, ts, C), lambda i: (b, n - 1 - i, 0))
    prev = lambda b: pl.BlockSpec((1, halo, C), lambda i: (b, jnp.maximum((n - 1 - i) * hb - 1, 0), 0))
    return _pcall(
        comm, body, name="conf_bwd", grid=(n,),
        in_specs=[pl.BlockSpec((ts, D), lambda i: (n - 1 - i, 0)), pl.BlockSpec((C, D), lambda i: (0, 0)),
                  tile(2), tile(3), prev(2), prev(3), pl.BlockSpec((ts, C), lambda i: (n - 1 - i, 0)),
                  tapw, vec, vec],
        out_specs=[pl.BlockSpec((2, ts, C), lambda i: (0, n - 1 - i, 0)), tapw, vec, vec, vec],
        out_shape=[jax.ShapeDtypeStruct((2, S, C), MXU_DTYPE), jax.ShapeDtypeStruct((taps, C), F32),
                   jax.ShapeDtypeStruct((1, C), F32), jax.ShapeDtypeStruct((1, C), F32),
                   jax.ShapeDtypeStruct((1, C), F32)],
        scratch_shapes=[pltpu.VMEM((ts + halo, C), F32), pltpu.VMEM((ts + halo, C), F32),
                        pltpu.VMEM((8, ts + halo - 8, C), F32)],
        semantics=("arbitrary",))(dx1, w_out_c, z, z, z, z, c1, conv_w, ln_g, ln_b)


def _lru_bwd(dx1, w_out_l, z, h, conv_w, conv_b, wa, ba, wx, bx, lam, comm=None):
    _, S, C = z.shape
    D = dx1.shape[1]
    ts = _tile(S, 256)
    n = S // ts
    taps = conv_w.shape[0]
    halo = 8
    hb = ts // halo

    def body(dx_ref, wo_ref, zx_ref, zxh_ref, zg_ref, h_ref, hh_ref, cw_ref, cb_ref, wa_ref, ba_ref,
             wx_ref, bx_ref, lam_ref,
             dz_ref, dwa_ref, dwx_ref, dba_ref, dbx_ref, dlam_ref, dcw_ref, dcb_ref,
             xbuf, hbuf, a_s, w_s, dh_s, g_s, dbuf, pc):
        i = pl.program_id(0)
        r = n - 1 - i

        @pl.when(i == 0)
        def _():
            for ref in (dwa_ref, dwx_ref, dba_ref, dbx_ref, dlam_ref, dcw_ref, dcb_ref, pc):
                ref[...] = jnp.zeros_like(ref)
            dbuf[pl.ds(ts, halo), :] = jnp.zeros((halo, C), F32)

        xbuf[pl.ds(0, halo), :] = jnp.where(r > 0, zxh_ref[0], 0.0)
        xbuf[pl.ds(halo, ts), :] = zx_ref[0]
        hbuf[pl.ds(0, halo), :] = jnp.where(r > 0, hh_ref[...], 0.0)
        hbuf[pl.ds(halo, ts), :] = h_ref[...]
        xc = _causal_taps(xbuf, halo, cw_ref, taps, ts) + cb_ref[...]
        lam_v = lam_ref[...]
        sp = _softplus_neg(lam_v)
        rg, ig, a, mult = _lru_gates(xc, wa_ref, ba_ref, wx_ref, bx_ref, sp)

        dy = _mm_nt(dx_ref[...], wo_ref[...])
        ge, dge = _gelu(zg_ref[0])
        dh = dy * ge
        dz_ref[1] = (dy * h_ref[...] * dge).astype(MXU_DTYPE)
        a_s[...] = a
        w_s[...] = a * dh
        dh_s[...] = dh
        row = lax.broadcasted_iota(jnp.int32, (8, C), 0)

        def step(kk, carry):
            off = pl.multiple_of((ts // 8 - 1 - kk) * 8, 8)
            av = a_s[pl.ds(off, 8), :]
            wv = w_s[pl.ds(off, 8), :]
            for d in (1, 2, 4):
                m = row < 8 - d
                a_sh = jnp.where(m, pltpu.roll(av, 8 - d, 0), 1.0)
                w_sh = jnp.where(m, pltpu.roll(wv, 8 - d, 0), 0.0)
                wv = wv + av * w_sh
                av = av * a_sh
            pv = wv + av * carry
            g_s[pl.ds(off, 8), :] = dh_s[pl.ds(off, 8), :] + jnp.where(row < 7, pltpu.roll(pv, 7, 0), carry)
            return jnp.broadcast_to(pv[0:1, :], (8, C))

        pc[...] = lax.fori_loop(0, ts // 8, step, pc[...])
        gt = g_s[...]
        da = gt * hbuf[pl.ds(halo - 1, ts), :]
        gm = gt * mult
        dlog_a = da * a - (gt * ig * xc) * (a * a) / mult
        dlam_ref[...] += _colsum(dlog_a * rg) * (RG_C * _sigmoid(-lam_v))
        dpa = (dlog_a * (-RG_C * sp)) * rg * (1.0 - rg)
        dpx = (gm * xc) * ig * (1.0 - ig)
        dba_ref[...] += _colsum(dpa)
        dbx_ref[...] += _colsum(dpx)
        xb = xc.astype(MXU_DTYPE)
        dpab, dpxb = dpa.astype(MXU_DTYPE), dpx.astype(MXU_DTYPE)
        dwa_ref[...] += _mm_tn(xb, dpab)
        dwx_ref[...] += _mm_tn(xb, dpxb)
        dxc = gm * ig + _mm_nt(dpab, wa_ref[...]) + _mm_nt(dpxb, wx_ref[...])
        dcb_ref[...] += _colsum(dxc)
        dbuf[pl.ds(0, ts), :] = dxc
        _tap_grads(dcw_ref, dxc, xbuf, halo, taps, ts)
        dz_ref[0] = _anticausal_taps(dbuf, cw_ref, taps, ts).astype(MXU_DTYPE)
        dbuf[pl.ds(ts, halo), :] = dbuf[pl.ds(0, halo), :]

    vec = pl.BlockSpec((1, C), lambda i: (0, 0))
    mat = pl.BlockSpec((C, C), lambda i: (0, 0))
    tapw = pl.BlockSpec((taps, C), lambda i: (0, 0))
    prev_rows = lambda i: jnp.maximum((n - 1 - i) * hb - 1, 0)
    sds = jax.ShapeDtypeStruct
    return _pcall(
        comm, body, name="lru_bwd", grid=(n,),
        in_specs=[pl.BlockSpec((ts, D), lambda i: (n - 1 - i, 0)), pl.BlockSpec((C, D), lambda i: (0, 0)),
                  pl.BlockSpec((1, ts, C), lambda i: (0, n - 1 - i, 0)),
                  pl.BlockSpec((1, halo, C), lambda i: (0, prev_rows(i), 0)),
                  pl.BlockSpec((1, ts, C), lambda i: (1, n - 1 - i, 0)),
                  pl.BlockSpec((ts, C), lambda i: (n - 1 - i, 0)),
                  pl.BlockSpec((halo, C), lambda i: (prev_rows(i), 0)),
                  tapw, vec, mat, vec, mat, vec, vec],
        out_specs=[pl.BlockSpec((2, ts, C), lambda i: (0, n - 1 - i, 0)), mat, mat, vec, vec, vec, tapw, vec],
        out_shape=[sds((2, S, C), MXU_DTYPE), sds((C, C), F32), sds((C, C), F32), sds((1, C), F32),
                   sds((1, C), F32), sds((1, C), F32), sds((taps, C), F32), sds((1, C), F32)],
        scratch_shapes=[pltpu.VMEM((ts + halo, C), F32), pltpu.VMEM((ts + halo, C), F32)]
        + [pltpu.VMEM((ts, C), F32)] * 4 + [pltpu.VMEM((ts + halo, C), F32), pltpu.VMEM((8, C), F32)],
        semantics=("arbitrary",))(dx1, w_out_l, z, z, z, h, h, conv_w, conv_b, wa, ba, wx, bx, lam)


def _bwd_in(dz_l, dz_c, w_in, x, g, dx1):
    S, D = x.shape
    nb, _, C = w_in.shape
    ts = _tile(S, 512)

    def body(dl_ref, dc_ref, w_ref, x_ref, g_ref, dx1_ref, dx_ref, dg_ref):
        i = pl.program_id(0)

        @pl.when(i == 0)
        def _():
            dg_ref[...] = jnp.zeros_like(dg_ref)

        dh = (_mm_nt(dl_ref[0], w_ref[0]) + _mm_nt(dl_ref[1], w_ref[1])
              + _mm_nt(dc_ref[0], w_ref[2]) + _mm_nt(dc_ref[1], w_ref[3]))
        rinv, xhat = _rms(x_ref[...])
        dg_ref[...] += _colsum(dh * xhat)
        dx_ref[...] = dx1_ref[...] + _rms_bwd(rinv, xhat, dh * g_ref[...])

    row = pl.BlockSpec((ts, D), lambda i: (i, 0))
    pair = pl.BlockSpec((2, ts, C), lambda i: (0, i, 0))
    vecd = pl.BlockSpec((1, D), lambda i: (0, 0))
    return pl.pallas_call(
        body, name="bwd_in", grid=(S // ts,),
        in_specs=[pair, pair, pl.BlockSpec((nb, D, C), lambda i: (0, 0, 0)), row, vecd, row],
        out_specs=[row, vecd],
        out_shape=[jax.ShapeDtypeStruct((S, D), F32), jax.ShapeDtypeStruct((1, D), F32)],
        compiler_params=_params("arbitrary"))(dz_l, dz_c, w_in, x, g, dx1)


def _wgrad(a, b, name, comm=None):
    na, S, K = a.shape
    nb, _, N = b.shape
    nj = max(na, nb)
    assert min(na, nb) == 1
    ts = _tile(S, 1024)
    ns = S // ts

    def body(a_ref, b_ref, o_ref, acc):
        s = pl.program_id(1)
        part = _mm_tn(a_ref[0], b_ref[0])

        @pl.when(s == 0)
        def _():
            acc[...] = part

        @pl.when(s > 0)
        def _():
            acc[...] += part

        @pl.when(s == ns - 1)
        def _():
            o_ref[0] = acc[...].astype(o_ref.dtype)

    res = _pcall(
        comm, body, name=name, grid=(nj, ns),
        in_specs=[pl.BlockSpec((1, ts, K), (lambda j, s: (j, s, 0)) if na > 1 else (lambda j, s: (0, s, 0))),
                  pl.BlockSpec((1, ts, N), (lambda j, s: (j, s, 0)) if nb > 1 else (lambda j, s: (0, s, 0)))],
        out_specs=pl.BlockSpec((1, K, N), lambda j, s: (j, 0, 0)),
        out_shape=jax.ShapeDtypeStruct((nj, K, N), WIRE_DTYPE),
        scratch_shapes=[pltpu.VMEM((K, N), F32)],
        semantics=("parallel", "arbitrary"))(a, b)
    return res[0] if comm is None else (res[0][0], res[1])


def _place():
    x, y, c = lax.axis_index("x"), lax.axis_index("y"), lax.axis_index("c")
    other_chips = [(1 - x, y), (x, 1 - y), (1 - x, 1 - y)]
    return x, y, c, other_chips


def _gather_weights(shards):
    nt = len(shards)

    def body(*refs):
        src, dst = refs[:nt], refs[nt:2 * nt]
        ici_send, ici_recv, d2d_send, d2d_recv, own_send, own_recv = refs[2 * nt:]
        x, y, c, chips = _place()
        mine = 2 * x + y

        def half(t, pc):
            hr = src[t].shape[0] // 2
            return pl.ds(pc * hr, hr)

        def own(t):
            return pltpu.make_async_remote_copy(
                src_ref=src[t], dst_ref=dst[t].at[mine], send_sem=own_send.at[t], recv_sem=own_recv.at[t],
                device_id=(x, y, 1 - c), device_id_type=MESH)

        def ici(t, k, block, to):
            cx, cy = block
            ref = dst[t].at[2 * cx + cy, half(t, c)]
            return pltpu.make_async_remote_copy(
                src_ref=src[t].at[half(t, c)] if to is not None else ref, dst_ref=ref,
                send_sem=ici_send.at[t, k], recv_sem=ici_recv.at[t, k],
                device_id=(*to, c) if to is not None else (x, y, c), device_id_type=MESH)

        def d2d(t, k, block, pc):
            cx, cy = block
            ref = dst[t].at[2 * cx + cy, half(t, pc)]
            return pltpu.make_async_remote_copy(
                src_ref=ref, dst_ref=ref, send_sem=d2d_send.at[t, k], recv_sem=d2d_recv.at[t, k],
                device_id=(x, y, 1 - c), device_id_type=MESH)

        sends = [ici(t, k, (x, y), chip) for t in range(nt) for k, chip in enumerate(chips)]
        sends += [own(t) for t in range(nt)]
        for cp in sends:
            cp.start()
        passed = []
        for t in range(nt):
            for k, chip in enumerate(chips):
                ici(t, k, chip, None).wait_recv()
                fw = d2d(t, k, chip, c)
                fw.start()
                passed.append(fw)
        for t in range(nt):
            own(t).wait_recv()
            for k, chip in enumerate(chips):
                d2d(t, k, chip, 1 - c).wait_recv()
        for cp in sends + passed:
            cp.wait_send()

    return pl.pallas_call(
        body, name="gather_weights",
        in_specs=[ANY] * nt, out_specs=[ANY] * nt,
        out_shape=[jax.ShapeDtypeStruct((N_CHIPS,) + s.shape, s.dtype) for s in shards],
        scratch_shapes=[pltpu.SemaphoreType.DMA((nt, 3))] * 4 + [pltpu.SemaphoreType.DMA((nt,))] * 2,
        compiler_params=pltpu.CompilerParams(has_side_effects=True))(*shards)


def _gather_over_ici(shards):
    nt = len(shards)

    def copies(src, dst, scr, arriving):
        ici_send, ici_recv, own_send, own_recv = scr
        x, y, c, chips = _place()
        out = []
        for t in range(nt):
            hr = src[t].shape[0] // 2
            rows = pl.ds(c * hr, hr)
            for k, (cx, cy) in enumerate(chips):
                block = 2 * cx + cy if arriving else 2 * x + y
                out.append(pltpu.make_async_remote_copy(
                    src_ref=src[t].at[rows], dst_ref=dst[t].at[block, rows],
                    send_sem=ici_send.at[t, k], recv_sem=ici_recv.at[t, k],
                    device_id=(cx, cy, c), device_id_type=MESH))
            out.append(pltpu.make_async_remote_copy(
                src_ref=src[t], dst_ref=dst[t].at[2 * x + y], send_sem=own_send.at[t], recv_sem=own_recv.at[t],
                device_id=(x, y, 1 - c), device_id_type=MESH))
        return out

    def start(src, dst, scr):
        for cp in copies(src, dst, scr, False):
            cp.start()

    def finish(src, dst, scr):
        for cp in copies(src, dst, scr, True):
            cp.wait_recv()
        for cp in copies(src, dst, scr, False):
            cp.wait_send()

    return _Comm(shards, [jax.ShapeDtypeStruct((N_CHIPS,) + s.shape, s.dtype) for s in shards],
                 [pltpu.SemaphoreType.DMA((nt, 3))] * 2 + [pltpu.SemaphoreType.DMA((nt,))] * 2, start, finish)


def _gather_pass_on(bufs):
    nt = len(bufs)

    def passed(dst, scr, t, k, block, pc):
        send, recv = scr
        x, y, c, _ = _place()
        cx, cy = block
        hr = dst[t].shape[1] // 2
        ref = dst[t].at[2 * cx + cy, pl.ds(pc * hr, hr)]
        return pltpu.make_async_remote_copy(src_ref=ref, dst_ref=ref, send_sem=send.at[t, k], recv_sem=recv.at[t, k],
                                            device_id=(x, y, 1 - c), device_id_type=MESH)

    def start(src, dst, scr):
        _, _, c, chips = _place()
        for t in range(nt):
            for k, chip in enumerate(chips):
                passed(dst, scr, t, k, chip, c).start()

    def finish(src, dst, scr):
        _, _, c, chips = _place()
        for t in range(nt):
            for k, chip in enumerate(chips):
                passed(dst, scr, t, k, chip, 1 - c).wait_recv()
        for t in range(nt):
            for k, chip in enumerate(chips):
                passed(dst, scr, t, k, chip, c).wait_send()

    return _Comm(bufs, [jax.ShapeDtypeStruct(b.shape, b.dtype) for b in bufs],
                 [pltpu.SemaphoreType.DMA((nt, 3))] * 2, start, finish, aliases={t: t for t in range(nt)})


def _exchange_halves(grads):
    nt = len(grads)

    def copies(src, dst, scr):
        send, recv = scr
        x, y, c, _ = _place()
        out = []
        for t in range(nt):
            hr = src[t].shape[1] // 2
            out.append(pltpu.make_async_remote_copy(
                src_ref=src[t].at[:, pl.ds((1 - c) * hr, hr)], dst_ref=dst[t],
                send_sem=send.at[t], recv_sem=recv.at[t], device_id=(x, y, 1 - c), device_id_type=MESH))
        return out

    def start(src, dst, scr):
        for cp in copies(src, dst, scr):
            cp.start()

    def finish(src, dst, scr):
        for cp in copies(src, dst, scr):
            cp.wait()

    return _Comm(grads, [jax.ShapeDtypeStruct((g.shape[0], g.shape[1] // 2, g.shape[2]), g.dtype) for g in grads],
                 [pltpu.SemaphoreType.DMA((nt,))] * 2, start, finish)


def _add_halves(grad, other, name):
    nb, R, C = grad.shape
    hr = R // 2
    tr = _tile(hr, 256, 16)
    steps = hr // tr
    c = lax.axis_index("c").astype(jnp.int32).reshape((1,))

    def body(c_ref, a_ref, b_ref, o_ref):
        o_ref[...] = (a_ref[...].astype(F32) + b_ref[...].astype(F32)).astype(o_ref.dtype)

    return pl.pallas_call(
        body, name=name,
        grid_spec=pltpu.PrefetchScalarGridSpec(
            num_scalar_prefetch=1, grid=(nb, steps),
            in_specs=[pl.BlockSpec((1, tr, C), lambda j, i, c_ref: (j, c_ref[0] * steps + i, 0)),
                      pl.BlockSpec((1, tr, C), lambda j, i, c_ref: (j, i, 0))],
            out_specs=pl.BlockSpec((1, tr, C), lambda j, i, c_ref: (j, i, 0))),
        out_shape=jax.ShapeDtypeStruct((nb, hr, C), grad.dtype),
        compiler_params=_params("parallel", "parallel"))(c, grad, other)


def _scatter_chip_sums(parts):
    nt = len(parts)

    def copies(src, dst, scr):
        send, recv = scr
        x, y, c, chips = _place()
        out = []
        for t in range(nt):
            for k, (cx, cy) in enumerate(chips):
                out.append(pltpu.make_async_remote_copy(
                    src_ref=src[t].at[2 * cx + cy], dst_ref=dst[t].at[k],
                    send_sem=send.at[t, k], recv_sem=recv.at[t, k], device_id=(cx, cy, c), device_id_type=MESH))
        return out

    def start(src, dst, scr):
        for cp in copies(src, dst, scr):
            cp.start()

    def finish(src, dst, scr):
        for cp in copies(src, dst, scr):
            cp.wait()

    return _Comm(parts, [jax.ShapeDtypeStruct((3,) + p.shape[1:], p.dtype) for p in parts],
                 [pltpu.SemaphoreType.DMA((nt, 3))] * 2, start, finish)


def _sum_chips(part, recv, name):
    _, hr, C = part.shape
    tr = _tile(hr, 256, 16)
    steps = hr // tr
    where = jnp.stack([2 * lax.axis_index("x") + lax.axis_index("y"), lax.axis_index("c")]).astype(jnp.int32)

    def body(w_ref, a_ref, b_ref, o_ref):
        acc = a_ref[0].astype(F32)
        for k in range(3):
            acc = acc + b_ref[k].astype(F32)
        o_ref[...] = acc

    return pl.pallas_call(
        body, name=name,
        grid_spec=pltpu.PrefetchScalarGridSpec(
            num_scalar_prefetch=1, grid=(steps,),
            in_specs=[pl.BlockSpec((1, tr, C), lambda i, w_ref: (w_ref[0], i, 0)),
                      pl.BlockSpec((3, tr, C), lambda i, w_ref: (0, i, 0))],
            out_specs=pl.BlockSpec((tr, C), lambda i, w_ref: (w_ref[1] * steps + i, 0))),
        out_shape=jax.ShapeDtypeStruct((2 * hr, C), F32),
        compiler_params=_params("parallel"))(where, part, recv)


def _join_halves(bufs):
    nt = len(bufs)

    def swap(dst, scr, t, pc):
        send, recv = scr
        x, y, c, _ = _place()
        hr = dst[t].shape[0] // 2
        rows = dst[t].at[pl.ds(pc * hr, hr)]
        return pltpu.make_async_remote_copy(src_ref=rows, dst_ref=rows, send_sem=send.at[t], recv_sem=recv.at[t],
                                            device_id=(x, y, 1 - c), device_id_type=MESH)

    def start(src, dst, scr):
        c = lax.axis_index("c")
        for t in range(nt):
            swap(dst, scr, t, c).start()

    def finish(src, dst, scr):
        c = lax.axis_index("c")
        for t in range(nt):
            swap(dst, scr, t, 1 - c).wait_recv()
        for t in range(nt):
            swap(dst, scr, t, c).wait_send()

    return _Comm(bufs, [jax.ShapeDtypeStruct(b.shape, b.dtype) for b in bufs],
                 [pltpu.SemaphoreType.DMA((nt,))] * 2, start, finish, aliases={t: t for t in range(nt)})


def _all_reduce_rows(buf, loss_row=None):
    R, L = buf.shape

    def copies(in_ref, gath, send, recv):
        x, y, c, _ = _place()
        out = []
        for k in range(1, N_DEV):
            peer = (x ^ ((k >> 2) & 1), y ^ ((k >> 1) & 1), c ^ (k & 1))
            out.append(pltpu.make_async_remote_copy(
                src_ref=in_ref, dst_ref=gath.at[k], send_sem=send.at[k - 1], recv_sem=recv.at[k - 1],
                device_id=peer, device_id_type=MESH))
        return out

    def start(ins, outs, scr):
        gath, send, recv = scr
        gath[0] = ins[0][...]
        for cp in copies(ins[0], gath, send, recv):
            cp.start()

    def finish(ins, outs, scr):
        gath, send, recv = scr
        for cp in copies(ins[0], gath, send, recv):
            cp.wait()
        x, y, c, _ = _place()
        me = 4 * x + 2 * y + c
        total = gath[me]
        for d in range(1, N_DEV):
            total = total + gath[d ^ me]
        outs[0][...] = total
        if loss_row is not None:
            outs[1][...] = jnp.sum(total[loss_row:loss_row + 1, :], axis=1, keepdims=True)

    out_shape = [jax.ShapeDtypeStruct((R, L), F32)]
    if loss_row is not None:
        out_shape.append(jax.ShapeDtypeStruct((1, 1), F32))
    return _Comm([buf], out_shape,
                 [pltpu.VMEM((N_DEV, R, L), F32), pltpu.SemaphoreType.DMA((N_DEV - 1,)),
                  pltpu.SemaphoreType.DMA((N_DEV - 1,))],
                 start, finish, in_specs=[WHOLE_VMEM], out_specs=[WHOLE_VMEM] * len(out_shape))


def _adamw(w, g, m, v, name):
    R, C = w.shape
    tr = _tile(R, 256)
    c1 = 1.0 - ADAM_B1 ** ADAM_STEP
    c2 = 1.0 - ADAM_B2 ** ADAM_STEP

    def body(w_ref, g_ref, m_ref, v_ref, d_ref, nm_ref, nv_ref):
        gv = g_ref[...]
        nm = ADAM_B1 * m_ref[...] + (1.0 - ADAM_B1) * gv
        nv = ADAM_B2 * v_ref[...] + (1.0 - ADAM_B2) * (gv * gv)
        nm_ref[...] = nm
        nv_ref[...] = nv
        d_ref[...] = -ADAM_LR * ((nm / c1) / (jnp.sqrt(nv / c2) + ADAM_EPS) + ADAM_WD * w_ref[...])

    blk = pl.BlockSpec((tr, C), lambda i: (i, 0))
    return pl.pallas_call(
        body, name=name, grid=(R // tr,), in_specs=[blk] * 4, out_specs=[blk] * 3,
        out_shape=[jax.ShapeDtypeStruct((R, C), F32)] * 3,
        compiler_params=_params("parallel"))(w, g, m, v)


def _pack_rows(arrays):
    rows = []
    for a in arrays:
        flat = a.reshape(-1).astype(F32)
        pad = (-flat.shape[0]) % LANES
        rows.append(jnp.pad(flat, (0, pad)).reshape(-1, LANES))
    buf = jnp.concatenate(rows, axis=0)
    return jnp.pad(buf, ((0, (-buf.shape[0]) % 8), (0, 0)))


def _unpack_rows(buf, shapes):
    out, r = [], 0
    for s in shapes:
        n = math.prod(s)
        nr = -(-n // LANES)
        out.append(buf[r:r + nr].reshape(-1)[:n].reshape(s))
        r += nr
    return out


def _block_diag(w):
    H, a, b = w.shape
    eye = jnp.eye(H, dtype=w.dtype)
    return (eye[:, None, :, None] * w[:, :, None, :]).reshape(H * a, H * b)


def _block_diag_parts(d, H):
    a, b = d.shape[0] // H, d.shape[1] // H
    d4 = d.reshape(H, a, H, b)
    return jnp.stack([d4[h, :, h, :] for h in range(H)])


def _rs_add(names, grads, others):
    return [_add_halves(g, o, "rs_add_halves_" + n) for n, g, o in zip(names, grads, others)]


def _rs_sum(names, parts, recvs):
    return [_sum_chips(p, r, "rs_sum_chips_" + n) for n, p, r in zip(names, parts, recvs)]


def _step(x, mem, target, shards, small, tap_rows, tap_shapes):
    D = x.shape[1]
    nch = N_CHIPS
    p = dict(small)

    (w_in_f,) = _gather_weights([shards['w_in']])
    early, late = ['w_out', 'w_q', 'w_kv', 'w_o'], ['w_up', 'w_down']
    ici_early, taps_sum = _gather_over_ici([shards[n] for n in early]), _all_reduce_rows(tap_rows)
    (z, h1), couts = _fwd_in(x, p['mix_norm_g'], w_in_f, comm=_merge(ici_early, taps_sum))
    early_bufs, (taps,) = _split(couts, ici_early, taps_sum)
    p.update(zip(COL_SHARDED_SMALL, _unpack_rows(taps, tap_shapes)))
    wa_d = _block_diag(p['lru_w_a']).astype(MXU_DTYPE)
    wx_d = _block_diag(p['lru_w_x']).astype(MXU_DTYPE)
    heads = p['lru_w_a'].shape[0]
    (h, y_lru), early_full = _lru_fwd(z, p['lru_conv_w'], p['lru_conv_b'], wa_d, p['lru_b_a'], wx_d, p['lru_b_x'],
                                      p['lru_lambda'], comm=_gather_pass_on(early_bufs))
    wf = dict(zip(early, early_full))
    (c1, c3), late_bufs = _conf_fwd(z, p['conf_conv_w'], p['conf_conv_b'], p['conf_ln_g'], p['conf_ln_b'],
                                    comm=_gather_over_ici([shards[n] for n in late]))
    w_out2 = wf['w_out'].reshape(2, -1, D)
    w_q = wf['w_q'].reshape(D, D)
    w_o = wf['w_o'].reshape(D, D)
    (x1, h2, q), late_full = _fwd_out_q(x, y_lru, c3, w_out2, p['xa_norm_g'], w_q, comm=_gather_pass_on(late_bufs))
    wf.update(zip(late, late_full))
    w_down2 = wf['w_down'].reshape(wf['w_up'].shape[0] // 2, -1, D)

    m, kv = _kv_fwd(mem, p['mem_norm_g'], wf['w_kv'])
    o, x2, h3 = _attn_fwd(q, kv, x1, w_o, p['ffn_norm_g'])
    gu = _fwd_up(h3, wf['w_up'])
    act, dx3, loss_lanes, d_final_g = _ffn_down_loss(gu, p['ffn_conv_w'], p['ffn_conv_b'], w_down2, x2,
                                                     p['final_norm_g'], target)

    dgu, dx2, d_ffn_g, d_ffn_cw, d_ffn_cb = _ffn_bwd(dx3, wf['w_down'], wf['w_up'], gu, x2, p['ffn_norm_g'],
                                                     p['ffn_conv_w'], p['ffn_conv_b'])
    g_down = _wgrad(act, dx3[None], "wgrad_down").reshape(nch, -1, D)
    g_up, other = _wgrad(h3[None], dgu, "wgrad_up", comm=_exchange_halves([g_down]))
    (p_down,) = _rs_add(['w_down'], [g_down], other)
    sc_down, ex_up = _scatter_chip_sums([p_down]), _exchange_halves([g_up])
    (dq, dx1, dkv, d_xa_g), couts = _attn_bwd(dx2, w_o, q, kv, x1, p['xa_norm_g'], w_q, comm=_merge(sc_down, ex_up))
    recv, other = _split(couts, sc_down, ex_up)
    f_down = _rs_sum(['w_down'], [p_down], recv)
    (p_up,) = _rs_add(['w_up'], [g_up], other)
    mid = ['w_o', 'w_q', 'w_kv']
    g_o = _wgrad(o[None], dx2[None], "wgrad_o").reshape(nch, -1, D)
    g_q = _wgrad(h2[None], dq[None], "wgrad_q").reshape(nch, -1, D)
    g_kv, d_mem_g = _kv_bwd(dkv, wf['w_kv'], mem, p['mem_norm_g'], m)
    join_down, sc_up, ex_mid = _join_halves(f_down), _scatter_chip_sums([p_up]), _exchange_halves([g_o, g_q, g_kv])
    (dz_c, d_conf_cw, d_conf_cb, d_ln_g, d_ln_b), couts = _conf_bwd(
        dx1, w_out2[1], z, c1, p['conf_conv_w'], p['conf_ln_g'], p['conf_ln_b'],
        comm=_merge(join_down, sc_up, ex_mid))
    (r_down,), recv, other = _split(couts, join_down, sc_up, ex_mid)
    p_up = [p_up]
    p_mid = _rs_add(mid, [g_o, g_q, g_kv], other)
    join_up, sc_mid = _join_halves(_rs_sum(['w_up'], p_up, recv)), _scatter_chip_sums(p_mid)
    (dz_l, d_wa, d_wx, d_ba, d_bx, d_lam, d_lru_cw, d_lru_cb), couts = _lru_bwd(
        dx1, w_out2[0], z, h, p['lru_conv_w'], p['lru_conv_b'], wa_d, p['lru_b_a'], wx_d, p['lru_b_x'],
        p['lru_lambda'], comm=_merge(join_up, sc_mid))
    (r_up,), recv = _split(couts, join_up, sc_mid)
    f_mid = _rs_sum(mid, p_mid, recv)
    grad_x, d_mix_g = _bwd_in(dz_l, dz_c, w_in_f, x, p['mix_norm_g'], dx1)
    g_out = jnp.concatenate([_wgrad(y_lru[None], dx1[None], "wgrad_out_lru"),
                             _wgrad(c3[None], dx1[None], "wgrad_out_conf")], axis=0).reshape(nch, -1, D)

    small_g = {'mix_norm_g': d_mix_g, 'lru_conv_w': d_lru_cw, 'lru_conv_b': d_lru_cb,
               'lru_w_a': _block_diag_parts(d_wa, heads), 'lru_b_a': d_ba,
               'lru_w_x': _block_diag_parts(d_wx, heads), 'lru_b_x': d_bx, 'lru_lambda': d_lam,
               'conf_conv_w': d_conf_cw, 'conf_conv_b': d_conf_cb, 'conf_ln_g': d_ln_g, 'conf_ln_b': d_ln_b,
               'xa_norm_g': d_xa_g, 'mem_norm_g': d_mem_g, 'ffn_norm_g': d_ffn_g,
               'ffn_conv_w': d_ffn_cw, 'ffn_conv_b': d_ffn_cb, 'final_norm_g': d_final_g}
    names = list(small_g)
    shapes = [small_g[n].shape for n in names]
    join_mid, ex_out = _join_halves(f_mid), _exchange_halves([g_out])
    small_sum = _all_reduce_rows(_pack_rows([loss_lanes] + [small_g[n] for n in names]), loss_row=0)
    g_in, couts = _wgrad(h1[None], jnp.concatenate([dz_l, dz_c], axis=0), "wgrad_in",
                         comm=_merge(join_mid, ex_out, small_sum))
    r_mid, other, (summed, loss) = _split(couts, join_mid, ex_out, small_sum)

    last = ['w_out', 'w_in']
    p_last = _rs_add(['w_out'], [g_out], other)
    p_last += _rs_add(['w_in'], [g_in], _run_comm(_exchange_halves([g_in]), "rs_exchange_w_in"))
    recv = _run_comm(_scatter_chip_sums(p_last), "rs_scatter_last")
    r_last = _run_comm(_join_halves(_rs_sum(last, p_last, recv)), "rs_join_last")
    big = dict(zip(['w_down', 'w_up'] + mid + last, [r_down, r_up] + r_mid + r_last))
    return grad_x, big, summed, loss, names, [loss_lanes.shape] + shapes


def kernel(x, mem, mix_norm_g, w_in, lru_conv_w, lru_conv_b, lru_w_a, lru_b_a, lru_w_x, lru_b_x, lru_lambda, conf_conv_w, conf_conv_b, conf_ln_g, conf_ln_b, w_out, xa_norm_g, mem_norm_g, w_q, w_kv, w_o, ffn_norm_g, w_up, ffn_conv_w, ffn_conv_b, w_down, final_norm_g, loss_target, m_mix_norm_g, m_w_in, m_lru_conv_w, m_lru_conv_b, m_lru_w_a, m_lru_b_a, m_lru_w_x, m_lru_b_x, m_lru_lambda, m_conf_conv_w, m_conf_conv_b, m_conf_ln_g, m_conf_ln_b, m_w_out, m_xa_norm_g, m_mem_norm_g, m_w_q, m_w_kv, m_w_o, m_ffn_norm_g, m_w_up, m_ffn_conv_w, m_ffn_conv_b, m_w_down, m_final_norm_g, v_mix_norm_g, v_w_in, v_lru_conv_w, v_lru_conv_b, v_lru_w_a, v_lru_b_a, v_lru_w_x, v_lru_b_x, v_lru_lambda, v_conf_conv_w, v_conf_conv_b, v_conf_ln_g, v_conf_ln_b, v_w_out, v_xa_norm_g, v_mem_norm_g, v_w_q, v_w_kv, v_w_o, v_ffn_norm_g, v_w_up, v_ffn_conv_w, v_ffn_conv_b, v_w_down, v_final_norm_g):
    given = dict(locals())
    w = {n: given[n] for n in WEIGHTS}
    mom = {n: given["m_" + n] for n in WEIGHTS}
    var = {n: given["v_" + n] for n in WEIGHTS}
    xi, yi, ci = lax.axis_index("x"), lax.axis_index("y"), lax.axis_index("c")
    chip = 2 * xi + yi

    shards = {n: w[n][0].astype(WIRE_DTYPE) for n in BIG}
    tap_full = []
    for n in COL_SHARDED_SMALL:
        s = w[n][0]
        full = jnp.zeros((s.shape[0], N_CHIPS * s.shape[1]), F32)
        s = jnp.where(ci == 0, s, jnp.zeros_like(s))
        tap_full.append(lax.dynamic_update_slice(full, s, (0, chip * s.shape[1])))
    small = {n: (w[n] if w[n].ndim == 1 else w[n][0]) for n in SMALL if n not in COL_SHARDED_SMALL}
    small = {n: (a.reshape(1, -1) if a.ndim == 1 else a) for n, a in small.items()}

    grad_x, big_g, summed, loss, small_names, packed_shapes = _step(
        x[0], mem[0], loss_target[0], shards, small, _pack_rows(tap_full), [t.shape for t in tap_full])
    small_sum = dict(zip(small_names, _unpack_rows(summed, packed_shapes)[1:]))

    grads = {}
    for n in WEIGHTS:
        if n in BIG:
            g = big_g[n]
        elif n in COL_SHARDED_SMALL:
            width = w[n].shape[-1]
            g = lax.dynamic_slice_in_dim(small_sum[n], chip * width, width, axis=1)
        else:
            g = small_sum[n]
        grads[n] = g.reshape(w[n].shape)

    delta, new_m, new_v = {}, {}, {}
    for n in BIG:
        d, nm, nv = _adamw(w[n][0], grads[n][0], mom[n][0], var[n][0], "adamw_" + n)
        delta[n], new_m[n], new_v[n] = d[None], nm[None], nv[None]
    shapes = [w[n].shape for n in SMALL]
    d, nm, nv = _adamw(_pack_rows([w[n] for n in SMALL]), _pack_rows([grads[n] for n in SMALL]),
                       _pack_rows([mom[n] for n in SMALL]), _pack_rows([var[n] for n in SMALL]), "adamw_small")
    for out, buf in ((delta, d), (new_m, nm), (new_v, nv)):
        out.update(dict(zip(SMALL, _unpack_rows(buf, shapes))))

    return (loss[0, 0], grad_x[None], *[grads[n] for n in WEIGHTS], *[delta[n] for n in WEIGHTS],
            *[new_m[n] for n in WEIGHTS], *[new_v[n] for n in WEIGHTS])
```

```python
import math

import jax
import jax.numpy as jnp
from jax import lax
from jax.experimental import pallas as pl
from jax.experimental.pallas import tpu as pltpu

F32 = jnp.float32
MXU_DTYPE = jnp.bfloat16
WIRE_DTYPE = jnp.bfloat16
EPS = 1e-6
RG_C = 8.0
XA_HEADS = 4
ADAM_LR, ADAM_B1, ADAM_B2, ADAM_EPS, ADAM_WD, ADAM_STEP = 0.001, 0.9, 0.999, 1e-08, 0.01, 10
VMEM_LIMIT_BYTES = 52 * 1024 * 1024
LANES = 1024
N_CHIPS = 4
N_DEV = 8
MESH = pl.DeviceIdType.MESH
GELU_C = math.sqrt(2.0 / math.pi)
GELU_K = 0.044715

WEIGHTS = ['mix_norm_g', 'w_in', 'lru_conv_w', 'lru_conv_b', 'lru_w_a', 'lru_b_a', 'lru_w_x', 'lru_b_x',
           'lru_lambda', 'conf_conv_w', 'conf_conv_b', 'conf_ln_g', 'conf_ln_b', 'w_out', 'xa_norm_g',
           'mem_norm_g', 'w_q', 'w_kv', 'w_o', 'ffn_norm_g', 'w_up', 'ffn_conv_w', 'ffn_conv_b', 'w_down',
           'final_norm_g']
BIG = ['w_in', 'w_kv', 'w_up', 'w_out', 'w_q', 'w_o', 'w_down']
SMALL = [n for n in WEIGHTS if n not in BIG]
COL_SHARDED_SMALL = ['lru_conv_w', 'conf_conv_w', 'ffn_conv_w']


def _params(*semantics):
    return pltpu.CompilerParams(dimension_semantics=semantics, vmem_limit_bytes=VMEM_LIMIT_BYTES)


ANY = pl.BlockSpec(memory_space=pl.ANY)
WHOLE_VMEM = pl.BlockSpec(memory_space=pltpu.VMEM)


class _Comm:
    def __init__(self, arrays, out_shapes, scratch, start, finish, aliases=None, in_specs=None, out_specs=None):
        self.arrays, self.out_shapes, self.scratch = list(arrays), list(out_shapes), list(scratch)
        self.start, self.finish = start, finish
        self.aliases = dict(aliases or {})
        self.in_specs = list(in_specs) if in_specs is not None else [ANY] * len(self.arrays)
        self.out_specs = list(out_specs) if out_specs is not None else [ANY] * len(self.out_shapes)


def _merge(*comms):
    comms = [c for c in comms if c is not None]
    if not comms:
        return None
    ai = [0]
    for c in comms:
        ai.append(ai[-1] + len(c.arrays))
    oi = [0]
    for c in comms:
        oi.append(oi[-1] + len(c.out_shapes))
    si = [0]
    for c in comms:
        si.append(si[-1] + len(c.scratch))

    def each(which):
        def run(ins, outs, scr):
            for k, c in enumerate(comms):
                getattr(c, which)(ins[ai[k]:ai[k + 1]], outs[oi[k]:oi[k + 1]], scr[si[k]:si[k + 1]])
        return run

    aliases = {ai[k] + i: oi[k] + o for k, c in enumerate(comms) for i, o in c.aliases.items()}
    return _Comm(sum((c.arrays for c in comms), []), sum((c.out_shapes for c in comms), []),
                 sum((c.scratch for c in comms), []), each("start"), each("finish"), aliases,
                 sum((c.in_specs for c in comms), []), sum((c.out_specs for c in comms), []))


def _split(outs, *comms):
    parts, at = [], 0
    for c in comms:
        parts.append(outs[at:at + len(c.out_shapes)])
        at += len(c.out_shapes)
    return parts


def _pcall(comm, body, *, name, grid, in_specs, out_specs, out_shape, semantics, scratch_shapes=(), aliases=None):
    single = not isinstance(out_shape, (list, tuple))
    out_shape = [out_shape] if single else list(out_shape)
    out_specs = [out_specs] if single else list(out_specs)
    in_specs, scratch_shapes = list(in_specs), list(scratch_shapes)
    aliases = dict(aliases or {})

    if comm is None:
        def plain(*args):
            return list(pl.pallas_call(body, name=name, grid=grid, in_specs=in_specs, out_specs=out_specs,
                                       out_shape=out_shape, scratch_shapes=scratch_shapes,
                                       input_output_aliases=aliases,
                                       compiler_params=_params(*semantics))(*args))
        return plain

    def hosted(*args):
        n_in, n_out, n_scr = len(args), len(out_shape), len(scratch_shapes)
        c_in, c_out = len(comm.arrays), len(comm.out_shapes)

        def wrapped(*refs):
            ins, cins = refs[:n_in], refs[n_in:n_in + c_in]
            o0 = n_in + c_in
            outs, couts = refs[o0:o0 + n_out], refs[o0 + n_out:o0 + n_out + c_out]
            s0 = o0 + n_out + c_out
            scr, cscr = refs[s0:s0 + n_scr], refs[s0 + n_scr:]
            first = last = None
            for axis, size in enumerate(grid):
                at_start, at_end = pl.program_id(axis) == 0, pl.program_id(axis) == size - 1
                first = at_start if first is None else first & at_start
                last = at_end if last is None else last & at_end
            if first is None:
                comm.start(cins, couts, cscr)
                body(*ins, *outs, *scr)
                comm.finish(cins, couts, cscr)
                return
            pl.when(first)(lambda: comm.start(cins, couts, cscr))
            body(*ins, *outs, *scr)
            pl.when(last)(lambda: comm.finish(cins, couts, cscr))

        res = pl.pallas_call(
            wrapped, name=name, grid=grid, in_specs=in_specs + comm.in_specs, out_specs=out_specs + comm.out_specs,
            out_shape=out_shape + comm.out_shapes, scratch_shapes=scratch_shapes + comm.scratch,
            input_output_aliases={**aliases, **{n_in + i: n_out + o for i, o in comm.aliases.items()}},
            compiler_params=pltpu.CompilerParams(dimension_semantics=("arbitrary",) * len(grid),
                                                 vmem_limit_bytes=VMEM_LIMIT_BYTES, has_side_effects=True),
        )(*args, *comm.arrays)
        return list(res[:n_out]), list(res[n_out:])

    return hosted


def _run_comm(comm, name):
    return _pcall(comm, lambda: None, name=name, grid=(), in_specs=[], out_specs=[], out_shape=[], semantics=())()[1]


def _tile(n, want, align=8):
    if n <= want:
        return n
    for t in range(want - want % align, 0, -align):
        if n % t == 0:
            return t
    raise ValueError((n, want, align))


def _mm(a, b):
    return jnp.dot(a.astype(MXU_DTYPE), b.astype(MXU_DTYPE), preferred_element_type=F32)


def _mm_nt(a, b):
    return lax.dot_general(a.astype(MXU_DTYPE), b.astype(MXU_DTYPE), (((1,), (1,)), ((), ())),
                           preferred_element_type=F32)


def _mm_tn(a, b):
    return lax.dot_general(a.astype(MXU_DTYPE), b.astype(MXU_DTYPE), (((0,), (0,)), ((), ())),
                           preferred_element_type=F32)


def _sigmoid(v):
    return 1.0 / (1.0 + jnp.exp(-v))


def _gelu(v):
    v2 = v * v
    t = jnp.tanh(GELU_C * (v + GELU_K * v * v2))
    return 0.5 * v * (1.0 + t), 0.5 * (1.0 + t) + 0.5 * v * (1.0 - t * t) * GELU_C * (1.0 + 3.0 * GELU_K * v2)


def _softplus_neg(lam):
    e = jnp.exp(-jnp.abs(lam))
    u = 1.0 + e
    log1p_e = jnp.where(u == 1.0, e, jnp.log(u) * e / jnp.where(u == 1.0, 1.0, u - 1.0))
    return jnp.maximum(-lam, 0.0) + log1p_e


def _rms(xv):
    rinv = lax.rsqrt(jnp.mean(xv * xv, axis=-1, keepdims=True) + EPS)
    return rinv, xv * rinv


def _rms_bwd(rinv, xhat, dxhat):
    return rinv * (dxhat - xhat * jnp.mean(dxhat * xhat, axis=-1, keepdims=True))


def _colsum(v):
    return jnp.sum(v, axis=0, keepdims=True)


def _wrow(w_ref, k, wcols):
    return w_ref[pl.ds(k, 1), :] if wcols is None else w_ref[pl.ds(k, 1), wcols]


def _causal_taps(buf_ref, halo, w_ref, taps, rows, wcols=None):
    acc = None
    for s in range(taps):
        term = _wrow(w_ref, taps - 1 - s, wcols) * buf_ref[pl.ds(halo - s, rows), :]
        acc = term if acc is None else acc + term
    return acc


def _anticausal_taps(buf_ref, w_ref, taps, rows, wcols=None):
    acc = None
    for s in range(taps):
        term = _wrow(w_ref, taps - 1 - s, wcols) * buf_ref[pl.ds(s, rows), :]
        acc = term if acc is None else acc + term
    return acc


def _tap_grads(dw_ref, dy, buf_ref, halo, taps, rows, wcols=None):
    for s in range(taps):
        g = _colsum(dy * buf_ref[pl.ds(halo - s, rows), :])
        if wcols is None:
            dw_ref[pl.ds(taps - 1 - s, 1), :] += g
        else:
            dw_ref[pl.ds(taps - 1 - s, 1), wcols] += g


def _shift_copies(dst_ref, buf_ref, rows, up):
    for r in range(8):
        dst_ref[r] = buf_ref[pl.ds(r if up else 8 - r, rows), :]


def _causal_taps8(sh_ref, halo, w_ref, taps, rows):
    acc = None
    for s in range(taps):
        term = _wrow(w_ref, taps - 1 - s, None) * sh_ref[s % 8, pl.ds(halo - 8 - 8 * (s // 8), rows), :]
        acc = term if acc is None else acc + term
    return acc


def _anticausal_taps8(sh_ref, w_ref, taps, rows):
    acc = None
    for s in range(taps):
        term = _wrow(w_ref, taps - 1 - s, None) * sh_ref[s % 8, pl.ds(8 * (s // 8), rows), :]
        acc = term if acc is None else acc + term
    return acc


def _tap_grads8(dw_ref, dy, sh_ref, halo, taps, rows):
    for s in range(taps):
        dw_ref[pl.ds(taps - 1 - s, 1), :] += _colsum(dy * sh_ref[s % 8, pl.ds(halo - 8 - 8 * (s // 8), rows), :])


def _fwd_in(x, g, w_in, comm=None):
    S, D = x.shape
    nb, _, C = w_in.shape
    ts = _tile(S, 512)

    def body(x_ref, g_ref, w_ref, z_ref, h_ref):
        _, xhat = _rms(x_ref[...])
        h = (xhat * g_ref[...]).astype(MXU_DTYPE)
        h_ref[...] = h
        for j in range(nb):
            z_ref[j] = jnp.dot(h, w_ref[j], preferred_element_type=F32)

    return _pcall(
        comm, body, name="fwd_in", grid=(S // ts,),
        in_specs=[pl.BlockSpec((ts, D), lambda i: (i, 0)), pl.BlockSpec((1, D), lambda i: (0, 0)),
                  pl.BlockSpec((nb, D, C), lambda i: (0, 0, 0))],
        out_specs=[pl.BlockSpec((nb, ts, C), lambda i: (0, i, 0)), pl.BlockSpec((ts, D), lambda i: (i, 0))],
        out_shape=[jax.ShapeDtypeStruct((nb, S, C), F32), jax.ShapeDtypeStruct((S, D), MXU_DTYPE)],
        semantics=("parallel",))(x, g, w_in)


def _lru_gates(xc, wa_ref, ba_ref, wx_ref, bx_ref, sp):
    xb = xc.astype(MXU_DTYPE)
    r = _sigmoid(jnp.dot(xb, wa_ref[...], preferred_element_type=F32) + ba_ref[...])
    ig = _sigmoid(jnp.dot(xb, wx_ref[...], preferred_element_type=F32) + bx_ref[...])
    log_a = -RG_C * r * sp
    a = jnp.exp(log_a)
    mult = jnp.sqrt(jnp.tanh(-log_a) * (a * a + 1.0))
    return r, ig, a, mult


def _lru_fwd(z, conv_w, conv_b, wa, ba, wx, bx, lam, comm=None):
    _, S, C = z.shape
    ts = _tile(S, 256)
    taps = conv_w.shape[0]
    halo = 8

    def body(zx_ref, zg_ref, cw_ref, cb_ref, wa_ref, ba_ref, wx_ref, bx_ref, lam_ref,
             h_ref, y_ref, xbuf, a_s, u_s, hc):
        i = pl.program_id(0)

        @pl.when(i == 0)
        def _():
            xbuf[pl.ds(0, halo), :] = jnp.zeros((halo, C), F32)
            hc[...] = jnp.zeros_like(hc)

        xbuf[pl.ds(halo, ts), :] = zx_ref[0]
        xc = _causal_taps(xbuf, halo, cw_ref, taps, ts) + cb_ref[...]
        sp = _softplus_neg(lam_ref[...])
        _, ig, a, mult = _lru_gates(xc, wa_ref, ba_ref, wx_ref, bx_ref, sp)
        a_s[...] = a
        u_s[...] = mult * (ig * xc)
        row = lax.broadcasted_iota(jnp.int32, (8, C), 0)

        def step(k, carry):
            off = pl.multiple_of(k * 8, 8)
            av = a_s[pl.ds(off, 8), :]
            uv = u_s[pl.ds(off, 8), :]
            for d in (1, 2, 4):
                m = row >= d
                a_sh = jnp.where(m, pltpu.roll(av, d, 0), 1.0)
                u_sh = jnp.where(m, pltpu.roll(uv, d, 0), 0.0)
                uv = uv + av * u_sh
                av = av * a_sh
            hv = uv + av * carry
            h_ref[pl.ds(off, 8), :] = hv
            return jnp.broadcast_to(hv[7:8, :], (8, C))

        hc[...] = lax.fori_loop(0, ts // 8, step, hc[...])
        ge, _ = _gelu(zg_ref[0])
        y_ref[...] = (h_ref[...] * ge).astype(MXU_DTYPE)
        xbuf[pl.ds(0, halo), :] = xbuf[pl.ds(ts, halo), :]

    vec = pl.BlockSpec((1, C), lambda i: (0, 0))
    mat = pl.BlockSpec((C, C), lambda i: (0, 0))
    return _pcall(
        comm, body, name="lru_fwd", grid=(S // ts,),
        in_specs=[pl.BlockSpec((1, ts, C), lambda i: (0, i, 0)), pl.BlockSpec((1, ts, C), lambda i: (1, i, 0)),
                  pl.BlockSpec((taps, C), lambda i: (0, 0)), vec, mat, vec, mat, vec, vec],
        out_specs=[pl.BlockSpec((ts, C), lambda i: (i, 0)), pl.BlockSpec((ts, C), lambda i: (i, 0))],
        out_shape=[jax.ShapeDtypeStruct((S, C), F32), jax.ShapeDtypeStruct((S, C), MXU_DTYPE)],
        scratch_shapes=[pltpu.VMEM((ts + halo, C), F32), pltpu.VMEM((ts, C), F32), pltpu.VMEM((ts, C), F32),
                        pltpu.VMEM((8, C), F32)],
        semantics=("arbitrary",))(z, z, conv_w, conv_b, wa, ba, wx, bx, lam)


def _layer_norm_stats(c1):
    mu = jnp.mean(c1, axis=-1, keepdims=True)
    xc = c1 - mu
    rstd = lax.rsqrt(jnp.mean(xc * xc, axis=-1, keepdims=True) + EPS)
    return rstd, xc * rstd


def _conf_fwd(z, conv_w, conv_b, ln_g, ln_b, comm=None):
    _, S, C = z.shape
    ts = _tile(S, 256)
    taps = conv_w.shape[0]
    halo = 32

    def body(za_ref, zb_ref, cw_ref, cb_ref, g_ref, b_ref, c1_ref, c3_ref, cbuf, shifted):
        i = pl.program_id(0)

        @pl.when(i == 0)
        def _():
            cbuf[pl.ds(0, halo), :] = jnp.zeros((halo, C), F32)

        cbuf[pl.ds(halo, ts), :] = za_ref[0] * _sigmoid(zb_ref[0])
        _shift_copies(shifted, cbuf, ts + halo - 8, up=False)
        c1 = _causal_taps8(shifted, halo, cw_ref, taps, ts) + cb_ref[...]
        c1_ref[...] = c1
        _, xhat = _layer_norm_stats(c1)
        c2 = xhat * g_ref[...] + b_ref[...]
        c3_ref[...] = (c2 * _sigmoid(c2)).astype(MXU_DTYPE)
        cbuf[pl.ds(0, halo), :] = cbuf[pl.ds(ts, halo), :]

    vec = pl.BlockSpec((1, C), lambda i: (0, 0))
    return _pcall(
        comm, body, name="conf_fwd", grid=(S // ts,),
        in_specs=[pl.BlockSpec((1, ts, C), lambda i: (2, i, 0)), pl.BlockSpec((1, ts, C), lambda i: (3, i, 0)),
                  pl.BlockSpec((taps, C), lambda i: (0, 0)), vec, vec, vec],
        out_specs=[pl.BlockSpec((ts, C), lambda i: (i, 0)), pl.BlockSpec((ts, C), lambda i: (i, 0))],
        out_shape=[jax.ShapeDtypeStruct((S, C), F32), jax.ShapeDtypeStruct((S, C), MXU_DTYPE)],
        scratch_shapes=[pltpu.VMEM((ts + halo, C), F32), pltpu.VMEM((8, ts + halo - 8, C), F32)],
        semantics=("arbitrary",))(z, z, conv_w, conv_b, ln_g, ln_b)


def _fwd_out_q(x, y_lru, c3, w_out, g_xa, w_q, comm=None):
    S, D = x.shape
    C = y_lru.shape[1]
    ts = _tile(S, 512)

    def body(x_ref, yl_ref, c3_ref, wo_ref, g_ref, wq_ref, x1_ref, h2_ref, q_ref):
        x1 = (x_ref[...] + jnp.dot(yl_ref[...], wo_ref[0], preferred_element_type=F32)
              + jnp.dot(c3_ref[...], wo_ref[1], preferred_element_type=F32))
        x1_ref[...] = x1
        _, xhat = _rms(x1)
        h2 = (xhat * g_ref[...]).astype(MXU_DTYPE)
        h2_ref[...] = h2
        q_ref[...] = jnp.dot(h2, wq_ref[...], preferred_element_type=F32).astype(MXU_DTYPE)

    row = lambda w: pl.BlockSpec((ts, w), lambda i: (i, 0))
    return _pcall(
        comm, body, name="fwd_out_q", grid=(S // ts,),
        in_specs=[row(D), row(C), row(C), pl.BlockSpec((2, C, D), lambda i: (0, 0, 0)),
                  pl.BlockSpec((1, D), lambda i: (0, 0)), pl.BlockSpec((D, D), lambda i: (0, 0))],
        out_specs=[row(D), row(D), row(D)],
        out_shape=[jax.ShapeDtypeStruct((S, D), F32), jax.ShapeDtypeStruct((S, D), MXU_DTYPE),
                   jax.ShapeDtypeStruct((S, D), MXU_DTYPE)],
        semantics=("parallel",))(x, y_lru, c3, w_out, g_xa, w_q)


def _kv_fwd(mem, g, w_kv):
    M, D = mem.shape
    nb, _, C = w_kv.shape

    def body(mem_ref, g_ref, w_ref, m_ref, kv_ref):
        _, xhat = _rms(mem_ref[...])
        m = (xhat * g_ref[...]).astype(MXU_DTYPE)
        m_ref[...] = m
        for j in range(nb):
            kv_ref[:, pl.ds(j * C, C)] = jnp.dot(m, w_ref[j], preferred_element_type=F32).astype(MXU_DTYPE)

    return pl.pallas_call(
        body, name="kv_fwd", grid=(1,),
        in_specs=[pl.BlockSpec((M, D), lambda i: (0, 0)), pl.BlockSpec((1, D), lambda i: (0, 0)),
                  pl.BlockSpec((nb, D, C), lambda i: (0, 0, 0))],
        out_specs=[pl.BlockSpec((M, D), lambda i: (0, 0)), pl.BlockSpec((M, nb * C), lambda i: (0, 0))],
        out_shape=[jax.ShapeDtypeStruct((M, D), MXU_DTYPE), jax.ShapeDtypeStruct((M, nb * C), MXU_DTYPE)],
        compiler_params=_params("arbitrary"))(mem, g, w_kv)


def _softmax_rows(s):
    e = jnp.exp(s - jnp.max(s, axis=-1, keepdims=True))
    return e / jnp.sum(e, axis=-1, keepdims=True)


def _attn_fwd(q, kv, x1, w_o, g_ffn, comm=None):
    S, D = x1.shape
    M = kv.shape[0]
    hd = D // XA_HEADS
    scale = hd ** -0.5
    ts = _tile(S, 512)

    def body(q_ref, kv_ref, x1_ref, wo_ref, g_ref, o_ref, x2_ref, h3_ref):
        for h in range(XA_HEADS):
            cols = pl.ds(h * hd, hd)
            p = _softmax_rows(_mm_nt(q_ref[:, cols], kv_ref[:, cols]) * scale)
            o_ref[:, cols] = _mm(p, kv_ref[:, pl.ds(D + h * hd, hd)]).astype(MXU_DTYPE)
        x2 = x1_ref[...] + jnp.dot(o_ref[...], wo_ref[...], preferred_element_type=F32)
        x2_ref[...] = x2
        _, xhat = _rms(x2)
        h3_ref[...] = (xhat * g_ref[...]).astype(MXU_DTYPE)

    row = pl.BlockSpec((ts, D), lambda i: (i, 0))
    return _pcall(
        comm, body, name="attn_fwd", grid=(S // ts,),
        in_specs=[row, pl.BlockSpec((M, 2 * D), lambda i: (0, 0)), row, pl.BlockSpec((D, D), lambda i: (0, 0)),
                  pl.BlockSpec((1, D), lambda i: (0, 0))],
        out_specs=[row, row, row],
        out_shape=[jax.ShapeDtypeStruct((S, D), MXU_DTYPE), jax.ShapeDtypeStruct((S, D), F32),
                   jax.ShapeDtypeStruct((S, D), MXU_DTYPE)],
        semantics=("parallel",))(q, kv, x1, w_o, g_ffn)


def _ffn_fwd(h3, w_up, conv_w, conv_b, w_down, x2, g_final, target, comm=None):
    S, D = h3.shape
    nb, _, CW = w_up.shape
    half = nb // 2
    cb = 768
    per = CW // cb
    J = half * per
    ts = _tile(S, 256)
    taps = conv_w.shape[0]
    halo = 8

    def body(h_ref, wup_ref, cw_ref, cb_ref, wd_ref, x2_ref, gf_ref, t_ref,
             gu_ref, act_ref, dx3_ref, loss_ref, dgf_ref, gbuf):
        i = pl.program_id(0)

        @pl.when(i == 0)
        def _():
            for ref in (loss_ref, dgf_ref, gbuf):
                ref[...] = jnp.zeros_like(ref)

        hv = h_ref[...]
        x3 = x2_ref[...]
        for j in range(J):
            b, cols, wcols = j // per, pl.ds((j % per) * cb, cb), pl.ds(j * cb, cb)
            g = jnp.dot(hv, wup_ref[b, :, cols], preferred_element_type=F32)
            u = jnp.dot(hv, wup_ref[half + b, :, cols], preferred_element_type=F32)
            gu_ref[0, b, :, cols] = g
            gu_ref[1, b, :, cols] = u
            gbuf[j, pl.ds(halo, ts), :] = g
            gc = _causal_taps(gbuf.at[j], halo, cw_ref, taps, ts, wcols=wcols) + cb_ref[:, wcols]
            gbuf[j, pl.ds(0, halo), :] = gbuf[j, pl.ds(ts, halo), :]
            ge, _ = _gelu(gc)
            act = (ge * u).astype(MXU_DTYPE)
            act_ref[j] = act
            x3 = x3 + jnp.dot(act, wd_ref[j], preferred_element_type=F32)
        rinv, xhat = _rms(x3)
        gf = gf_ref[...]
        diff = xhat * gf - t_ref[...]
        loss_ref[...] += _colsum(diff * diff) * (0.5 / D)
        dy = diff * (1.0 / D)
        dgf_ref[...] += _colsum(dy * xhat)
        dx3_ref[...] = _rms_bwd(rinv, xhat, dy * gf)

    row = pl.BlockSpec((ts, D), lambda i: (i, 0))
    vecd = pl.BlockSpec((1, D), lambda i: (0, 0))
    once = pl.Buffered(1)
    sds = jax.ShapeDtypeStruct
    res = _pcall(
        comm, body, name="ffn_fwd", grid=(S // ts,),
        in_specs=[row, pl.BlockSpec((nb, D, CW), lambda i: (0, 0, 0), pipeline_mode=once),
                  pl.BlockSpec((taps, half * CW), lambda i: (0, 0)), pl.BlockSpec((1, half * CW), lambda i: (0, 0)),
                  pl.BlockSpec((J, cb, D), lambda i: (0, 0, 0), pipeline_mode=once), row, vecd, row],
        out_specs=[pl.BlockSpec((2, half, ts, CW), lambda i: (0, 0, i, 0)),
                   pl.BlockSpec((J, ts, cb), lambda i: (0, i, 0)), row, vecd, vecd],
        out_shape=[sds((2, half, S, CW), F32), sds((J, S, cb), MXU_DTYPE), sds((S, D), F32),
                   sds((1, D), F32), sds((1, D), F32)],
        scratch_shapes=[pltpu.VMEM((J, ts + halo, cb), F32)],
        semantics=("arbitrary",))(h3, w_up, conv_w, conv_b, w_down.reshape(J, cb, D), x2, g_final, target)
    outs = res if comm is None else res[0]
    outs = [outs[0].reshape(nb, S, CW)] + list(outs[1:])
    return outs if comm is None else (outs, res[1])


def _ffn_bwd(dx3, w_down, w_up, gu, x2, g_ffn, conv_w, conv_b, comm=None):
    nb, S, CW = gu.shape
    half = nb // 2
    D = dx3.shape[1]
    cb = 768
    per = CW // cb
    J = half * per
    ts = _tile(S, 256)
    n = S // ts
    taps = conv_w.shape[0]
    halo = 8
    hb = ts // halo

    def body(dx_ref, x2_ref, gf_ref, wd_ref, wup_ref, gu_ref, gh_ref, cw_ref, cb_ref,
             dgu_ref, dx2_ref, dgf_ref, dcw_ref, dcb_ref, gbuf, dbuf):
        i = pl.program_id(0)
        r = n - 1 - i

        @pl.when(i == 0)
        def _():
            for ref in (dgf_ref, dcw_ref, dcb_ref, dbuf):
                ref[...] = jnp.zeros_like(ref)

        dx3v = dx_ref[...]
        dxb = dx3v.astype(MXU_DTYPE)
        dh = None
        for j in range(J):
            b, cols, wcols = j // per, pl.ds((j % per) * cb, cb), pl.ds(j * cb, cb)
            dact = _mm_nt(dxb, wd_ref[j])
            gbuf[pl.ds(0, halo), :] = jnp.where(r > 0, gh_ref[0, b, :, cols], 0.0)
            gbuf[pl.ds(halo, ts), :] = gu_ref[0, b, :, cols]
            gc = _causal_taps(gbuf, halo, cw_ref, taps, ts, wcols=wcols) + cb_ref[:, wcols]
            ge, dge = _gelu(gc)
            dub = (dact * ge).astype(MXU_DTYPE)
            dgc = dact * gu_ref[1, b, :, cols] * dge
            dcb_ref[:, wcols] += _colsum(dgc)
            dbuf[j, pl.ds(0, ts), :] = dgc
            _tap_grads(dcw_ref, dgc, gbuf, halo, taps, ts, wcols=wcols)
            dgb = _anticausal_taps(dbuf.at[j], cw_ref, taps, ts, wcols=wcols).astype(MXU_DTYPE)
            dbuf[j, pl.ds(ts, halo), :] = dbuf[j, pl.ds(0, halo), :]
            dgu_ref[0, b, :, cols] = dgb
            dgu_ref[1, b, :, cols] = dub
            part = _mm_nt(dgb, wup_ref[b, :, cols]) + _mm_nt(dub, wup_ref[half + b, :, cols])
            dh = part if dh is None else dh + part
        rinv, xhat = _rms(x2_ref[...])
        dgf_ref[...] += _colsum(dh * xhat)
        dx2_ref[...] = dx3v + _rms_bwd(rinv, xhat, dh * gf_ref[...])

    gu2 = gu.reshape(2, half, S, CW)
    row = pl.BlockSpec((ts, D), lambda i: (n - 1 - i, 0))
    vecd = pl.BlockSpec((1, D), lambda i: (0, 0))
    pair = pl.BlockSpec((2, half, ts, CW), lambda i: (0, 0, n - 1 - i, 0))
    g_prev = pl.BlockSpec((1, half, halo, CW), lambda i: (0, 0, jnp.maximum((n - 1 - i) * hb - 1, 0), 0))
    tapw = pl.BlockSpec((taps, half * CW), lambda i: (0, 0))
    vec = pl.BlockSpec((1, half * CW), lambda i: (0, 0))
    once = pl.Buffered(1)
    sds = jax.ShapeDtypeStruct
    res = _pcall(
        comm, body, name="ffn_bwd", grid=(n,),
        in_specs=[row, row, vecd, pl.BlockSpec((J, cb, D), lambda i: (0, 0, 0), pipeline_mode=once),
                  pl.BlockSpec((nb, D, CW), lambda i: (0, 0, 0), pipeline_mode=once), pair, g_prev, tapw, vec],
        out_specs=[pair, row, vecd, tapw, vec],
        out_shape=[sds((2, half, S, CW), MXU_DTYPE), sds((S, D), F32), sds((1, D), F32),
                   sds((taps, half * CW), F32), sds((1, half * CW), F32)],
        scratch_shapes=[pltpu.VMEM((ts + halo, cb), F32), pltpu.VMEM((J, ts + halo, cb), F32)],
        semantics=("arbitrary",))(dx3, x2, g_ffn, w_down.reshape(J, cb, D), w_up, gu2, gu2, conv_w, conv_b)
    outs = res if comm is None else res[0]
    outs = [outs[0].reshape(nb, S, CW)] + list(outs[1:])
    return outs if comm is None else (outs, res[1])


def _attn_bwd(dx2, w_o, q, kv, x1, g_xa, w_q, comm=None):
    S, D = x1.shape
    M = kv.shape[0]
    hd = D // XA_HEADS
    scale = hd ** -0.5
    ts = _tile(S, 512)

    def body(dx2_ref, wo_ref, q_ref, kv_ref, x1_ref, g_ref, wq_ref, dq_ref, dx1_ref, dkv_ref, dg_ref):
        i = pl.program_id(0)

        @pl.when(i == 0)
        def _():
            dkv_ref[...] = jnp.zeros_like(dkv_ref)
            dg_ref[...] = jnp.zeros_like(dg_ref)

        dx2 = dx2_ref[...]
        do = _mm_nt(dx2, wo_ref[...]).astype(MXU_DTYPE)
        for h in range(XA_HEADS):
            cols = pl.ds(h * hd, hd)
            vcols = pl.ds(D + h * hd, hd)
            qh, kh, doh = q_ref[:, cols], kv_ref[:, cols], do[:, h * hd:(h + 1) * hd]
            p = _softmax_rows(_mm_nt(qh, kh) * scale)
            dp = _mm_nt(doh, kv_ref[:, vcols])
            dkv_ref[:, vcols] += _mm_tn(p, doh)
            ds = (p * (dp - jnp.sum(dp * p, axis=-1, keepdims=True)) * scale).astype(MXU_DTYPE)
            dq_ref[:, cols] = _mm(ds, kh).astype(MXU_DTYPE)
            dkv_ref[:, cols] += _mm_tn(ds, qh)
        dh2 = _mm_nt(dq_ref[...], wq_ref[...])
        rinv, xhat = _rms(x1_ref[...])
        dg_ref[...] += _colsum(dh2 * xhat)
        dx1_ref[...] = dx2 + _rms_bwd(rinv, xhat, dh2 * g_ref[...])

    row = pl.BlockSpec((ts, D), lambda i: (i, 0))
    mat = pl.BlockSpec((D, D), lambda i: (0, 0))
    vecd = pl.BlockSpec((1, D), lambda i: (0, 0))
    kvs = pl.BlockSpec((M, 2 * D), lambda i: (0, 0))
    return _pcall(
        comm, body, name="attn_bwd", grid=(S // ts,),
        in_specs=[row, mat, row, kvs, row, vecd, mat],
        out_specs=[row, row, kvs, vecd],
        out_shape=[jax.ShapeDtypeStruct((S, D), MXU_DTYPE), jax.ShapeDtypeStruct((S, D), F32),
                   jax.ShapeDtypeStruct((M, 2 * D), F32), jax.ShapeDtypeStruct((1, D), F32)],
        semantics=("arbitrary",))(dx2, w_o, q, kv, x1, g_xa, w_q)


def _kv_bwd(dkv, w_kv, mem, g, m):
    M, D = mem.shape
    nb, _, C = w_kv.shape

    def body(dkv_ref, w_ref, mem_ref, m_ref, dw_ref, dg_ref):
        dm = jnp.zeros((M, D), F32)
        for j in range(nb):
            dj = dkv_ref[:, pl.ds(j * C, C)].astype(MXU_DTYPE)
            dw_ref[j] = _mm_tn(m_ref[...], dj).astype(dw_ref.dtype)
            dm = dm + _mm_nt(dj, w_ref[j])
        _, xhat = _rms(mem_ref[...])
        dg_ref[...] = _colsum(dm * xhat)

    full = lambda *s: pl.BlockSpec(s, lambda i: (0,) * len(s))
    return pl.pallas_call(
        body, name="kv_bwd", grid=(1,),
        in_specs=[full(M, nb * C), full(nb, D, C), full(M, D), full(M, D)],
        out_specs=[full(nb, D, C), full(1, D)],
        out_shape=[jax.ShapeDtypeStruct((nb, D, C), WIRE_DTYPE), jax.ShapeDtypeStruct((1, D), F32)],
        compiler_params=_params("arbitrary"))(dkv, w_kv, mem, m)


def _conf_bwd(dx1, w_out_c, z, c1, conv_w, ln_g, ln_b, comm=None):
    _, S, C = z.shape
    D = dx1.shape[1]
    ts = _tile(S, 256)
    n = S // ts
    taps = conv_w.shape[0]
    halo = 32
    hb = ts // halo

    def body(dx_ref, wo_ref, za_ref, zb_ref, zah_ref, zbh_ref, c1_ref, cw_ref, g_ref, b_ref,
             dz_ref, dcw_ref, dcb_ref, dlg_ref, dlb_ref, c0buf, dbuf, shifted):
        i = pl.program_id(0)
        r = n - 1 - i

        @pl.when(i == 0)
        def _():
            for ref in (dcw_ref, dcb_ref, dlg_ref, dlb_ref):
                ref[...] = jnp.zeros_like(ref)
            dbuf[pl.ds(ts, halo), :] = jnp.zeros((halo, C), F32)

        za = za_ref[0]
        sb = _sigmoid(zb_ref[0])
        c0buf[pl.ds(0, halo), :] = jnp.where(r > 0, zah_ref[0] * _sigmoid(zbh_ref[0]), 0.0)
        c0buf[pl.ds(halo, ts), :] = za * sb
        dc3 = _mm_nt(dx_ref[...], wo_ref[...])
        rstd, xhat = _layer_norm_stats(c1_ref[...])
        g = g_ref[...]
        c2 = xhat * g + b_ref[...]
        sg = _sigmoid(c2)
        dc2 = dc3 * sg * (1.0 + c2 * (1.0 - sg))
        dlg_ref[...] += _colsum(dc2 * xhat)
        dlb_ref[...] += _colsum(dc2)
        dxh = dc2 * g
        dc1 = rstd * (dxh - jnp.mean(dxh, axis=-1, keepdims=True)
                      - xhat * jnp.mean(dxh * xhat, axis=-1, keepdims=True))
        dcb_ref[...] += _colsum(dc1)
        dbuf[pl.ds(0, ts), :] = dc1
        _shift_copies(shifted, c0buf, ts + halo - 8, up=False)
        _tap_grads8(dcw_ref, dc1, shifted, halo, taps, ts)
        _shift_copies(shifted, dbuf, ts + halo - 8, up=True)
        dc0 = _anticausal_taps8(shifted, cw_ref, taps, ts)
        dz_ref[0] = (dc0 * sb).astype(MXU_DTYPE)
        dz_ref[1] = (dc0 * za * sb * (1.0 - sb)).astype(MXU_DTYPE)
        dbuf[pl.ds(ts, halo), :] = dbuf[pl.ds(0, halo), :]

    vec = pl.BlockSpec((1, C), lambda i: (0, 0))
    tapw = pl.BlockSpec((taps, C), lambda i: (0, 0))
    tile = lambda b: pl.BlockSpec((1, ts, C), lambda i: (b, n - 1 - i, 0))
    prev = lambda b: pl.BlockSpec((1, halo, C), lambda i: (b, jnp.maximum((n - 1 - i) * hb - 1, 0), 0))
    return _pcall(
        comm, body, name="conf_bwd", grid=(n,),
        in_specs=[pl.BlockSpec((ts, D), lambda i: (n - 1 - i, 0)), pl.BlockSpec((C, D), lambda i: (0, 0)),
                  tile(2), tile(3), prev(2), prev(3), pl.BlockSpec((ts, C), lambda i: (n - 1 - i, 0)),
                  tapw, vec, vec],
        out_specs=[pl.BlockSpec((2, ts, C), lambda i: (1, n - 1 - i, 0)), tapw, vec, vec, vec],
        out_shape=[jax.ShapeDtypeStruct((4, S, C), MXU_DTYPE), jax.ShapeDtypeStruct((taps, C), F32),
                   jax.ShapeDtypeStruct((1, C), F32), jax.ShapeDtypeStruct((1, C), F32),
                   jax.ShapeDtypeStruct((1, C), F32)],
        scratch_shapes=[pltpu.VMEM((ts + halo, C), F32), pltpu.VMEM((ts + halo, C), F32),
                        pltpu.VMEM((8, ts + halo - 8, C), F32)],
        semantics=("arbitrary",))(dx1, w_out_c, z, z, z, z, c1, conv_w, ln_g, ln_b)


def _lru_bwd(dx1, w_out_l, z, h, conv_w, conv_b, wa, ba, wx, bx, lam, dz, comm=None):
    _, S, C = z.shape
    D = dx1.shape[1]
    ts = _tile(S, 256)
    n = S // ts
    taps = conv_w.shape[0]
    halo = 8
    hb = ts // halo

    def body(dx_ref, wo_ref, zx_ref, zxh_ref, zg_ref, h_ref, hh_ref, cw_ref, cb_ref, wa_ref, ba_ref,
             wx_ref, bx_ref, lam_ref, dz_in,
             dz_ref, dwa_ref, dwx_ref, dba_ref, dbx_ref, dlam_ref, dcw_ref, dcb_ref,
             xbuf, hbuf, a_s, w_s, dh_s, g_s, dbuf, pc):
        i = pl.program_id(0)
        r = n - 1 - i

        @pl.when(i == 0)
        def _():
            for ref in (dwa_ref, dwx_ref, dba_ref, dbx_ref, dlam_ref, dcw_ref, dcb_ref, pc):
                ref[...] = jnp.zeros_like(ref)
            dbuf[pl.ds(ts, halo), :] = jnp.zeros((halo, C), F32)

        xbuf[pl.ds(0, halo), :] = jnp.where(r > 0, zxh_ref[0], 0.0)
        xbuf[pl.ds(halo, ts), :] = zx_ref[0]
        hbuf[pl.ds(0, halo), :] = jnp.where(r > 0, hh_ref[...], 0.0)
        hbuf[pl.ds(halo, ts), :] = h_ref[...]
        xc = _causal_taps(xbuf, halo, cw_ref, taps, ts) + cb_ref[...]
        lam_v = lam_ref[...]
        sp = _softplus_neg(lam_v)
        rg, ig, a, mult = _lru_gates(xc, wa_ref, ba_ref, wx_ref, bx_ref, sp)

        dy = _mm_nt(dx_ref[...], wo_ref[...])
        ge, dge = _gelu(zg_ref[0])
        dh = dy * ge
        dz_ref[1] = (dy * h_ref[...] * dge).astype(MXU_DTYPE)
        a_s[...] = a
        w_s[...] = a * dh
        dh_s[...] = dh
        row = lax.broadcasted_iota(jnp.int32, (8, C), 0)

        def step(kk, carry):
            off = pl.multiple_of((ts // 8 - 1 - kk) * 8, 8)
            av = a_s[pl.ds(off, 8), :]
            wv = w_s[pl.ds(off, 8), :]
            for d in (1, 2, 4):
                m = row < 8 - d
                a_sh = jnp.where(m, pltpu.roll(av, 8 - d, 0), 1.0)
                w_sh = jnp.where(m, pltpu.roll(wv, 8 - d, 0), 0.0)
                wv = wv + av * w_sh
                av = av * a_sh
            pv = wv + av * carry
            g_s[pl.ds(off, 8), :] = dh_s[pl.ds(off, 8), :] + jnp.where(row < 7, pltpu.roll(pv, 7, 0), carry)
            return jnp.broadcast_to(pv[0:1, :], (8, C))

        pc[...] = lax.fori_loop(0, ts // 8, step, pc[...])
        gt = g_s[...]
        da = gt * hbuf[pl.ds(halo - 1, ts), :]
        gm = gt * mult
        dlog_a = da * a - (gt * ig * xc) * (a * a) / mult
        dlam_ref[...] += _colsum(dlog_a * rg) * (RG_C * _sigmoid(-lam_v))
        dpa = (dlog_a * (-RG_C * sp)) * rg * (1.0 - rg)
        dpx = (gm * xc) * ig * (1.0 - ig)
        dba_ref[...] += _colsum(dpa)
        dbx_ref[...] += _colsum(dpx)
        xb = xc.astype(MXU_DTYPE)
        dpab, dpxb = dpa.astype(MXU_DTYPE), dpx.astype(MXU_DTYPE)
        dwa_ref[...] += _mm_tn(xb, dpab)
        dwx_ref[...] += _mm_tn(xb, dpxb)
        dxc = gm * ig + _mm_nt(dpab, wa_ref[...]) + _mm_nt(dpxb, wx_ref[...])
        dcb_ref[...] += _colsum(dxc)
        dbuf[pl.ds(0, ts), :] = dxc
        _tap_grads(dcw_ref, dxc, xbuf, halo, taps, ts)
        dz_ref[0] = _anticausal_taps(dbuf, cw_ref, taps, ts).astype(MXU_DTYPE)
        dbuf[pl.ds(ts, halo), :] = dbuf[pl.ds(0, halo), :]

    vec = pl.BlockSpec((1, C), lambda i: (0, 0))
    mat = pl.BlockSpec((C, C), lambda i: (0, 0))
    tapw = pl.BlockSpec((taps, C), lambda i: (0, 0))
    prev_rows = lambda i: jnp.maximum((n - 1 - i) * hb - 1, 0)
    sds = jax.ShapeDtypeStruct
    return _pcall(
        comm, body, name="lru_bwd", grid=(n,),
        in_specs=[pl.BlockSpec((ts, D), lambda i: (n - 1 - i, 0)), pl.BlockSpec((C, D), lambda i: (0, 0)),
                  pl.BlockSpec((1, ts, C), lambda i: (0, n - 1 - i, 0)),
                  pl.BlockSpec((1, halo, C), lambda i: (0, prev_rows(i), 0)),
                  pl.BlockSpec((1, ts, C), lambda i: (1, n - 1 - i, 0)),
                  pl.BlockSpec((ts, C), lambda i: (n - 1 - i, 0)),
                  pl.BlockSpec((halo, C), lambda i: (prev_rows(i), 0)),
                  tapw, vec, mat, vec, mat, vec, vec, ANY],
        out_specs=[pl.BlockSpec((2, ts, C), lambda i: (0, n - 1 - i, 0)), mat, mat, vec, vec, vec, tapw, vec],
        out_shape=[sds(dz.shape, MXU_DTYPE), sds((C, C), F32), sds((C, C), F32), sds((1, C), F32),
                   sds((1, C), F32), sds((1, C), F32), sds((taps, C), F32), sds((1, C), F32)],
        scratch_shapes=[pltpu.VMEM((ts + halo, C), F32), pltpu.VMEM((ts + halo, C), F32)]
        + [pltpu.VMEM((ts, C), F32)] * 4 + [pltpu.VMEM((ts + halo, C), F32), pltpu.VMEM((8, C), F32)],
        aliases={14: 0},
        semantics=("arbitrary",))(dx1, w_out_l, z, z, z, h, h, conv_w, conv_b, wa, ba, wx, bx, lam, dz)


def _bwd_in(dz, w_in, x, g, dx1):
    S, D = x.shape
    nb, _, C = w_in.shape
    ts = _tile(S, 512)

    def body(dz_ref, w_ref, x_ref, g_ref, dx1_ref, dx_ref, dg_ref):
        i = pl.program_id(0)

        @pl.when(i == 0)
        def _():
            dg_ref[...] = jnp.zeros_like(dg_ref)

        dh = _mm_nt(dz_ref[0], w_ref[0])
        for j in range(1, nb):
            dh = dh + _mm_nt(dz_ref[j], w_ref[j])
        rinv, xhat = _rms(x_ref[...])
        dg_ref[...] += _colsum(dh * xhat)
        dx_ref[...] = dx1_ref[...] + _rms_bwd(rinv, xhat, dh * g_ref[...])

    row = pl.BlockSpec((ts, D), lambda i: (i, 0))
    vecd = pl.BlockSpec((1, D), lambda i: (0, 0))
    return pl.pallas_call(
        body, name="bwd_in", grid=(S // ts,),
        in_specs=[pl.BlockSpec((nb, ts, C), lambda i: (0, i, 0)), pl.BlockSpec((nb, D, C), lambda i: (0, 0, 0)),
                  row, vecd, row],
        out_specs=[row, vecd],
        out_shape=[jax.ShapeDtypeStruct((S, D), F32), jax.ShapeDtypeStruct((1, D), F32)],
        compiler_params=_params("arbitrary"))(dz, w_in, x, g, dx1)


def _wgrad(a, b, name, comm=None):
    na, S, K = a.shape
    nb, _, N = b.shape
    nj = max(na, nb)
    assert min(na, nb) == 1
    ts = _tile(S, 1024)
    ns = S // ts

    def body(a_ref, b_ref, o_ref, acc):
        s = pl.program_id(1)
        part = _mm_tn(a_ref[0], b_ref[0])

        @pl.when(s == 0)
        def _():
            acc[...] = part

        @pl.when(s > 0)
        def _():
            acc[...] += part

        @pl.when(s == ns - 1)
        def _():
            o_ref[0] = acc[...].astype(o_ref.dtype)

    res = _pcall(
        comm, body, name=name, grid=(nj, ns),
        in_specs=[pl.BlockSpec((1, ts, K), (lambda j, s: (j, s, 0)) if na > 1 else (lambda j, s: (0, s, 0))),
                  pl.BlockSpec((1, ts, N), (lambda j, s: (j, s, 0)) if nb > 1 else (lambda j, s: (0, s, 0)))],
        out_specs=pl.BlockSpec((1, K, N), lambda j, s: (j, 0, 0)),
        out_shape=jax.ShapeDtypeStruct((nj, K, N), WIRE_DTYPE),
        scratch_shapes=[pltpu.VMEM((K, N), F32)],
        semantics=("parallel", "arbitrary"))(a, b)
    return res[0] if comm is None else (res[0][0], res[1])


def _place():
    x, y, c = lax.axis_index("x"), lax.axis_index("y"), lax.axis_index("c")
    other_chips = [(1 - x, y), (x, 1 - y), (1 - x, 1 - y)]
    return x, y, c, other_chips


def _gather_weights(shards):
    nt = len(shards)

    def body(*refs):
        src, dst = refs[:nt], refs[nt:2 * nt]
        ici_send, ici_recv, d2d_send, d2d_recv, own_send, own_recv = refs[2 * nt:]
        x, y, c, chips = _place()
        mine = 2 * x + y

        def half(t, pc):
            hr = src[t].shape[0] // 2
            return pl.ds(pc * hr, hr)

        def own(t):
            return pltpu.make_async_remote_copy(
                src_ref=src[t], dst_ref=dst[t].at[mine], send_sem=own_send.at[t], recv_sem=own_recv.at[t],
                device_id=(x, y, 1 - c), device_id_type=MESH)

        def ici(t, k, block, to):
            cx, cy = block
            ref = dst[t].at[2 * cx + cy, half(t, c)]
            return pltpu.make_async_remote_copy(
                src_ref=src[t].at[half(t, c)] if to is not None else ref, dst_ref=ref,
                send_sem=ici_send.at[t, k], recv_sem=ici_recv.at[t, k],
                device_id=(*to, c) if to is not None else (x, y, c), device_id_type=MESH)

        def d2d(t, k, block, pc):
            cx, cy = block
            ref = dst[t].at[2 * cx + cy, half(t, pc)]
            return pltpu.make_async_remote_copy(
                src_ref=ref, dst_ref=ref, send_sem=d2d_send.at[t, k], recv_sem=d2d_recv.at[t, k],
                device_id=(x, y, 1 - c), device_id_type=MESH)

        sends = [ici(t, k, (x, y), chip) for t in range(nt) for k, chip in enumerate(chips)]
        sends += [own(t) for t in range(nt)]
        for cp in sends:
            cp.start()
        passed = []
        for t in range(nt):
            for k, chip in enumerate(chips):
                ici(t, k, chip, None).wait_recv()
                fw = d2d(t, k, chip, c)
                fw.start()
                passed.append(fw)
        for t in range(nt):
            own(t).wait_recv()
            for k, chip in enumerate(chips):
                d2d(t, k, chip, 1 - c).wait_recv()
        for cp in sends + passed:
            cp.wait_send()

    return pl.pallas_call(
        body, name="gather_weights",
        in_specs=[ANY] * nt, out_specs=[ANY] * nt,
        out_shape=[jax.ShapeDtypeStruct((N_CHIPS,) + s.shape, s.dtype) for s in shards],
        scratch_shapes=[pltpu.SemaphoreType.DMA((nt, 3))] * 4 + [pltpu.SemaphoreType.DMA((nt,))] * 2,
        compiler_params=pltpu.CompilerParams(has_side_effects=True))(*shards)


def _gather_over_ici(shards):
    nt = len(shards)

    def copies(src, dst, scr, arriving):
        ici_send, ici_recv, own_send, own_recv = scr
        x, y, c, chips = _place()
        out = []
        for t in range(nt):
            hr = src[t].shape[0] // 2
            rows = pl.ds(c * hr, hr)
            for k, (cx, cy) in enumerate(chips):
                block = 2 * cx + cy if arriving else 2 * x + y
                out.append(pltpu.make_async_remote_copy(
                    src_ref=src[t].at[rows], dst_ref=dst[t].at[block, rows],
                    send_sem=ici_send.at[t, k], recv_sem=ici_recv.at[t, k],
                    device_id=(cx, cy, c), device_id_type=MESH))
            out.append(pltpu.make_async_remote_copy(
                src_ref=src[t], dst_ref=dst[t].at[2 * x + y], send_sem=own_send.at[t], recv_sem=own_recv.at[t],
                device_id=(x, y, 1 - c), device_id_type=MESH))
        return out

    def start(src, dst, scr):
        for cp in copies(src, dst, scr, False):
            cp.start()

    def finish(src, dst, scr):
        for cp in copies(src, dst, scr, True):
            cp.wait_recv()
        for cp in copies(src, dst, scr, False):
            cp.wait_send()

    return _Comm(shards, [jax.ShapeDtypeStruct((N_CHIPS,) + s.shape, s.dtype) for s in shards],
                 [pltpu.SemaphoreType.DMA((nt, 3))] * 2 + [pltpu.SemaphoreType.DMA((nt,))] * 2, start, finish)


def _gather_pass_on(bufs):
    nt = len(bufs)

    def passed(dst, scr, t, k, block, pc):
        send, recv = scr
        x, y, c, _ = _place()
        cx, cy = block
        hr = dst[t].shape[1] // 2
        ref = dst[t].at[2 * cx + cy, pl.ds(pc * hr, hr)]
        return pltpu.make_async_remote_copy(src_ref=ref, dst_ref=ref, send_sem=send.at[t, k], recv_sem=recv.at[t, k],
                                            device_id=(x, y, 1 - c), device_id_type=MESH)

    def start(src, dst, scr):
        _, _, c, chips = _place()
        for t in range(nt):
            for k, chip in enumerate(chips):
                passed(dst, scr, t, k, chip, c).start()

    def finish(src, dst, scr):
        _, _, c, chips = _place()
        for t in range(nt):
            for k, chip in enumerate(chips):
                passed(dst, scr, t, k, chip, 1 - c).wait_recv()
        for t in range(nt):
            for k, chip in enumerate(chips):
                passed(dst, scr, t, k, chip, c).wait_send()

    return _Comm(bufs, [jax.ShapeDtypeStruct(b.shape, b.dtype) for b in bufs],
                 [pltpu.SemaphoreType.DMA((nt, 3))] * 2, start, finish, aliases={t: t for t in range(nt)})


def _exchange_halves(grads):
    nt = len(grads)

    def copies(src, dst, scr):
        send, recv = scr
        x, y, c, _ = _place()
        out = []
        for t in range(nt):
            hr = src[t].shape[1] // 2
            out.append(pltpu.make_async_remote_copy(
                src_ref=src[t].at[:, pl.ds((1 - c) * hr, hr)], dst_ref=dst[t],
                send_sem=send.at[t], recv_sem=recv.at[t], device_id=(x, y, 1 - c), device_id_type=MESH))
        return out

    def start(src, dst, scr):
        for cp in copies(src, dst, scr):
            cp.start()

    def finish(src, dst, scr):
        for cp in copies(src, dst, scr):
            cp.wait()

    return _Comm(grads, [jax.ShapeDtypeStruct((g.shape[0], g.shape[1] // 2, g.shape[2]), g.dtype) for g in grads],
                 [pltpu.SemaphoreType.DMA((nt,))] * 2, start, finish)


def _add_halves(grad, other, name):
    nb, R, C = grad.shape
    hr = R // 2
    tr = _tile(hr, 256, 16)
    steps = hr // tr
    c = lax.axis_index("c").astype(jnp.int32).reshape((1,))

    def body(c_ref, a_ref, b_ref, o_ref):
        o_ref[...] = (a_ref[...].astype(F32) + b_ref[...].astype(F32)).astype(o_ref.dtype)

    return pl.pallas_call(
        body, name=name,
        grid_spec=pltpu.PrefetchScalarGridSpec(
            num_scalar_prefetch=1, grid=(nb, steps),
            in_specs=[pl.BlockSpec((1, tr, C), lambda j, i, c_ref: (j, c_ref[0] * steps + i, 0)),
                      pl.BlockSpec((1, tr, C), lambda j, i, c_ref: (j, i, 0))],
            out_specs=pl.BlockSpec((1, tr, C), lambda j, i, c_ref: (j, i, 0))),
        out_shape=jax.ShapeDtypeStruct((nb, hr, C), grad.dtype),
        compiler_params=_params("parallel", "parallel"))(c, grad, other)


def _scatter_chip_sums(parts):
    nt = len(parts)

    def copies(src, dst, scr):
        send, recv = scr
        x, y, c, chips = _place()
        out = []
        for t in range(nt):
            for k, (cx, cy) in enumerate(chips):
                out.append(pltpu.make_async_remote_copy(
                    src_ref=src[t].at[2 * cx + cy], dst_ref=dst[t].at[k],
                    send_sem=send.at[t, k], recv_sem=recv.at[t, k], device_id=(cx, cy, c), device_id_type=MESH))
        return out

    def start(src, dst, scr):
        for cp in copies(src, dst, scr):
            cp.start()

    def finish(src, dst, scr):
        for cp in copies(src, dst, scr):
            cp.wait()

    return _Comm(parts, [jax.ShapeDtypeStruct((3,) + p.shape[1:], p.dtype) for p in parts],
                 [pltpu.SemaphoreType.DMA((nt, 3))] * 2, start, finish)


def _sum_chips(part, recv, name):
    _, hr, C = part.shape
    tr = _tile(hr, 256, 16)
    steps = hr // tr
    where = jnp.stack([2 * lax.axis_index("x") + lax.axis_index("y"), lax.axis_index("c")]).astype(jnp.int32)

    def body(w_ref, a_ref, b_ref, o_ref):
        acc = a_ref[0].astype(F32)
        for k in range(3):
            acc = acc + b_ref[k].astype(F32)
        o_ref[...] = acc

    return pl.pallas_call(
        body, name=name,
        grid_spec=pltpu.PrefetchScalarGridSpec(
            num_scalar_prefetch=1, grid=(steps,),
            in_specs=[pl.BlockSpec((1, tr, C), lambda i, w_ref: (w_ref[0], i, 0)),
                      pl.BlockSpec((3, tr, C), lambda i, w_ref: (0, i, 0))],
            out_specs=pl.BlockSpec((tr, C), lambda i, w_ref: (w_ref[1] * steps + i, 0))),
        out_shape=jax.ShapeDtypeStruct((2 * hr, C), F32),
        compiler_params=_params("parallel"))(where, part, recv)


def _join_halves(bufs):
    nt = len(bufs)

    def swap(dst, scr, t, pc):
        send, recv = scr
        x, y, c, _ = _place()
        hr = dst[t].shape[0] // 2
        rows = dst[t].at[pl.ds(pc * hr, hr)]
        return pltpu.make_async_remote_copy(src_ref=rows, dst_ref=rows, send_sem=send.at[t], recv_sem=recv.at[t],
                                            device_id=(x, y, 1 - c), device_id_type=MESH)

    def start(src, dst, scr):
        c = lax.axis_index("c")
        for t in range(nt):
            swap(dst, scr, t, c).start()

    def finish(src, dst, scr):
        c = lax.axis_index("c")
        for t in range(nt):
            swap(dst, scr, t, 1 - c).wait_recv()
        for t in range(nt):
            swap(dst, scr, t, c).wait_send()

    return _Comm(bufs, [jax.ShapeDtypeStruct(b.shape, b.dtype) for b in bufs],
                 [pltpu.SemaphoreType.DMA((nt,))] * 2, start, finish, aliases={t: t for t in range(nt)})


def _all_reduce_rows(buf, loss_row=None):
    R, L = buf.shape

    def copies(in_ref, gath, send, recv):
        x, y, c, _ = _place()
        out = []
        for k in range(1, N_DEV):
            peer = (x ^ ((k >> 2) & 1), y ^ ((k >> 1) & 1), c ^ (k & 1))
            out.append(pltpu.make_async_remote_copy(
                src_ref=in_ref, dst_ref=gath.at[k], send_sem=send.at[k - 1], recv_sem=recv.at[k - 1],
                device_id=peer, device_id_type=MESH))
        return out

    def start(ins, outs, scr):
        gath, send, recv = scr
        gath[0] = ins[0][...]
        for cp in copies(ins[0], gath, send, recv):
            cp.start()

    def finish(ins, outs, scr):
        gath, send, recv = scr
        for cp in copies(ins[0], gath, send, recv):
            cp.wait()
        x, y, c, _ = _place()
        me = 4 * x + 2 * y + c
        total = gath[me]
        for d in range(1, N_DEV):
            total = total + gath[d ^ me]
        outs[0][...] = total
        if loss_row is not None:
            outs[1][...] = jnp.sum(total[loss_row:loss_row + 1, :], axis=1, keepdims=True)

    out_shape = [jax.ShapeDtypeStruct((R, L), F32)]
    if loss_row is not None:
        out_shape.append(jax.ShapeDtypeStruct((1, 1), F32))
    return _Comm([buf], out_shape,
                 [pltpu.VMEM((N_DEV, R, L), F32), pltpu.SemaphoreType.DMA((N_DEV - 1,)),
                  pltpu.SemaphoreType.DMA((N_DEV - 1,))],
                 start, finish, in_specs=[WHOLE_VMEM], out_specs=[WHOLE_VMEM] * len(out_shape))


def _adamw(w, g, m, v, name):
    R, C = w.shape
    tr = _tile(R, 256)
    c1 = 1.0 - ADAM_B1 ** ADAM_STEP
    c2 = 1.0 - ADAM_B2 ** ADAM_STEP

    def body(w_ref, g_ref, m_ref, v_ref, d_ref, nm_ref, nv_ref):
        gv = g_ref[...]
        nm = ADAM_B1 * m_ref[...] + (1.0 - ADAM_B1) * gv
        nv = ADAM_B2 * v_ref[...] + (1.0 - ADAM_B2) * (gv * gv)
        nm_ref[...] = nm
        nv_ref[...] = nv
        d_ref[...] = -ADAM_LR * ((nm / c1) / (jnp.sqrt(nv / c2) + ADAM_EPS) + ADAM_WD * w_ref[...])

    blk = pl.BlockSpec((tr, C), lambda i: (i, 0))
    return pl.pallas_call(
        body, name=name, grid=(R // tr,), in_specs=[blk] * 4, out_specs=[blk] * 3,
        out_shape=[jax.ShapeDtypeStruct((R, C), F32)] * 3,
        compiler_params=_params("parallel"))(w, g, m, v)


def _pack_rows(arrays):
    rows = []
    for a in arrays:
        flat = a.reshape(-1).astype(F32)
        pad = (-flat.shape[0]) % LANES
        rows.append(jnp.pad(flat, (0, pad)).reshape(-1, LANES))
    buf = jnp.concatenate(rows, axis=0)
    return jnp.pad(buf, ((0, (-buf.shape[0]) % 8), (0, 0)))


def _unpack_rows(buf, shapes):
    out, r = [], 0
    for s in shapes:
        n = math.prod(s)
        nr = -(-n // LANES)
        out.append(buf[r:r + nr].reshape(-1)[:n].reshape(s))
        r += nr
    return out


def _block_diag(w):
    H, a, b = w.shape
    eye = jnp.eye(H, dtype=w.dtype)
    return (eye[:, None, :, None] * w[:, :, None, :]).reshape(H * a, H * b)


def _block_diag_parts(d, H):
    a, b = d.shape[0] // H, d.shape[1] // H
    d4 = d.reshape(H, a, H, b)
    return jnp.stack([d4[h, :, h, :] for h in range(H)])


def _rs_add(names, grads, others):
    return [_add_halves(g, o, "rs_add_halves_" + n) for n, g, o in zip(names, grads, others)]


def _rs_sum(names, parts, recvs):
    return [_sum_chips(p, r, "rs_sum_chips_" + n) for n, p, r in zip(names, parts, recvs)]


def _step(x, mem, target, shards, small, tap_rows, tap_shapes):
    D = x.shape[1]
    nch = N_CHIPS
    p = dict(small)

    (w_in_f,) = _gather_weights([shards['w_in']])
    wf = {}

    def ici(names):
        return _gather_over_ici([shards[n] for n in names])

    ici_a, taps_sum = ici(['w_out', 'w_q']), _all_reduce_rows(tap_rows)
    (z, h1), couts = _fwd_in(x, p['mix_norm_g'], w_in_f, comm=_merge(ici_a, taps_sum))
    bufs_a, (taps,) = _split(couts, ici_a, taps_sum)
    p.update(zip(COL_SHARDED_SMALL, _unpack_rows(taps, tap_shapes)))
    wa_d = _block_diag(p['lru_w_a']).astype(MXU_DTYPE)
    wx_d = _block_diag(p['lru_w_x']).astype(MXU_DTYPE)
    heads = p['lru_w_a'].shape[0]
    pass_a, ici_b = _gather_pass_on(bufs_a), ici(['w_kv', 'w_o'])
    (h, y_lru), couts = _lru_fwd(z, p['lru_conv_w'], p['lru_conv_b'], wa_d, p['lru_b_a'], wx_d, p['lru_b_x'],
                                 p['lru_lambda'], comm=_merge(pass_a, ici_b))
    (wf['w_out'], wf['w_q']), bufs_b = _split(couts, pass_a, ici_b)
    pass_b, ici_c = _gather_pass_on(bufs_b), ici(['w_up'])
    (c1, c3), couts = _conf_fwd(z, p['conf_conv_w'], p['conf_conv_b'], p['conf_ln_g'], p['conf_ln_b'],
                                comm=_merge(pass_b, ici_c))
    (wf['w_kv'], wf['w_o']), bufs_c = _split(couts, pass_b, ici_c)
    w_out2 = wf['w_out'].reshape(2, -1, D)
    w_q = wf['w_q'].reshape(D, D)
    w_o = wf['w_o'].reshape(D, D)
    pass_c, ici_d = _gather_pass_on(bufs_c), ici(['w_down'])
    (x1, h2, q), couts = _fwd_out_q(x, y_lru, c3, w_out2, p['xa_norm_g'], w_q, comm=_merge(pass_c, ici_d))
    (wf['w_up'],), bufs_d = _split(couts, pass_c, ici_d)
    m, kv = _kv_fwd(mem, p['mem_norm_g'], wf['w_kv'])
    (o, x2, h3), (wf['w_down'],) = _attn_fwd(q, kv, x1, w_o, p['ffn_norm_g'], comm=_gather_pass_on(bufs_d))
    gu, act, dx3, loss_lanes, d_final_g = _ffn_fwd(h3, wf['w_up'], p['ffn_conv_w'], p['ffn_conv_b'], wf['w_down'],
                                                   x2, p['final_norm_g'], target)

    dgu, dx2, d_ffn_g, d_ffn_cw, d_ffn_cb = _ffn_bwd(dx3, wf['w_down'], wf['w_up'], gu, x2, p['ffn_norm_g'],
                                                     p['ffn_conv_w'], p['ffn_conv_b'])
    g_down = _wgrad(act, dx3[None], "wgrad_down").reshape(nch, -1, D)
    g_up, other = _wgrad(h3[None], dgu, "wgrad_up", comm=_exchange_halves([g_down]))
    (p_down,) = _rs_add(['w_down'], [g_down], other)
    sc_down, ex_up = _scatter_chip_sums([p_down]), _exchange_halves([g_up])
    (dq, dx1, dkv, d_xa_g), couts = _attn_bwd(dx2, w_o, q, kv, x1, p['xa_norm_g'], w_q, comm=_merge(sc_down, ex_up))
    recv, other = _split(couts, sc_down, ex_up)
    f_down = _rs_sum(['w_down'], [p_down], recv)
    (p_up,) = _rs_add(['w_up'], [g_up], other)
    mid = ['w_o', 'w_q', 'w_kv']
    g_o = _wgrad(o[None], dx2[None], "wgrad_o").reshape(nch, -1, D)
    g_q = _wgrad(h2[None], dq[None], "wgrad_q").reshape(nch, -1, D)
    g_kv, d_mem_g = _kv_bwd(dkv, wf['w_kv'], mem, p['mem_norm_g'], m)
    join_down, sc_up, ex_mid = _join_halves(f_down), _scatter_chip_sums([p_up]), _exchange_halves([g_o, g_q, g_kv])
    (dz_c, d_conf_cw, d_conf_cb, d_ln_g, d_ln_b), couts = _conf_bwd(
        dx1, w_out2[1], z, c1, p['conf_conv_w'], p['conf_ln_g'], p['conf_ln_b'],
        comm=_merge(join_down, sc_up, ex_mid))
    (r_down,), recv, other = _split(couts, join_down, sc_up, ex_mid)
    p_up = [p_up]
    p_mid = _rs_add(mid, [g_o, g_q, g_kv], other)
    join_up, sc_mid = _join_halves(_rs_sum(['w_up'], p_up, recv)), _scatter_chip_sums(p_mid)
    (dz, d_wa, d_wx, d_ba, d_bx, d_lam, d_lru_cw, d_lru_cb), couts = _lru_bwd(
        dx1, w_out2[0], z, h, p['lru_conv_w'], p['lru_conv_b'], wa_d, p['lru_b_a'], wx_d, p['lru_b_x'],
        p['lru_lambda'], dz_c, comm=_merge(join_up, sc_mid))
    (r_up,), recv = _split(couts, join_up, sc_mid)
    f_mid = _rs_sum(mid, p_mid, recv)
    grad_x, d_mix_g = _bwd_in(dz, w_in_f, x, p['mix_norm_g'], dx1)
    g_out = jnp.concatenate([_wgrad(y_lru[None], dx1[None], "wgrad_out_lru"),
                             _wgrad(c3[None], dx1[None], "wgrad_out_conf")], axis=0).reshape(nch, -1, D)

    small_g = {'mix_norm_g': d_mix_g, 'lru_conv_w': d_lru_cw, 'lru_conv_b': d_lru_cb,
               'lru_w_a': _block_diag_parts(d_wa, heads), 'lru_b_a': d_ba,
               'lru_w_x': _block_diag_parts(d_wx, heads), 'lru_b_x': d_bx, 'lru_lambda': d_lam,
               'conf_conv_w': d_conf_cw, 'conf_conv_b': d_conf_cb, 'conf_ln_g': d_ln_g, 'conf_ln_b': d_ln_b,
               'xa_norm_g': d_xa_g, 'mem_norm_g': d_mem_g, 'ffn_norm_g': d_ffn_g,
               'ffn_conv_w': d_ffn_cw, 'ffn_conv_b': d_ffn_cb, 'final_norm_g': d_final_g}
    names = list(small_g)
    shapes = [small_g[n].shape for n in names]
    join_mid, ex_out = _join_halves(f_mid), _exchange_halves([g_out])
    small_sum = _all_reduce_rows(_pack_rows([loss_lanes] + [small_g[n] for n in names]), loss_row=0)
    g_in, couts = _wgrad(h1[None], dz, "wgrad_in", comm=_merge(join_mid, ex_out, small_sum))
    r_mid, other, (summed, loss) = _split(couts, join_mid, ex_out, small_sum)

    last = ['w_out', 'w_in']
    p_last = _rs_add(['w_out'], [g_out], other)
    p_last += _rs_add(['w_in'], [g_in], _run_comm(_exchange_halves([g_in]), "rs_exchange_w_in"))
    recv = _run_comm(_scatter_chip_sums(p_last), "rs_scatter_last")
    r_last = _run_comm(_join_halves(_rs_sum(last, p_last, recv)), "rs_join_last")
    big = dict(zip(['w_down', 'w_up'] + mid + last, [r_down, r_up] + r_mid + r_last))
    return grad_x, big, summed, loss, names, [loss_lanes.shape] + shapes


def kernel(x, mem, mix_norm_g, w_in, lru_conv_w, lru_conv_b, lru_w_a, lru_b_a, lru_w_x, lru_b_x, lru_lambda, conf_conv_w, conf_conv_b, conf_ln_g, conf_ln_b, w_out, xa_norm_g, mem_norm_g, w_q, w_kv, w_o, ffn_norm_g, w_up, ffn_conv_w, ffn_conv_b, w_down, final_norm_g, loss_target, m_mix_norm_g, m_w_in, m_lru_conv_w, m_lru_conv_b, m_lru_w_a, m_lru_b_a, m_lru_w_x, m_lru_b_x, m_lru_lambda, m_conf_conv_w, m_conf_conv_b, m_conf_ln_g, m_conf_ln_b, m_w_out, m_xa_norm_g, m_mem_norm_g, m_w_q, m_w_kv, m_w_o, m_ffn_norm_g, m_w_up, m_ffn_conv_w, m_ffn_conv_b, m_w_down, m_final_norm_g, v_mix_norm_g, v_w_in, v_lru_conv_w, v_lru_conv_b, v_lru_w_a, v_lru_b_a, v_lru_w_x, v_lru_b_x, v_lru_lambda, v_conf_conv_w, v_conf_conv_b, v_conf_ln_g, v_conf_ln_b, v_w_out, v_xa_norm_g, v_mem_norm_g, v_w_q, v_w_kv, v_w_o, v_ffn_norm_g, v_w_up, v_ffn_conv_w, v_ffn_conv_b, v_w_down, v_final_norm_g):
    given = dict(locals())
    w = {n: given[n] for n in WEIGHTS}
    mom = {n: given["m_" + n] for n in WEIGHTS}
    var = {n: given["v_" + n] for n in WEIGHTS}
    xi, yi, ci = lax.axis_index("x"), lax.axis_index("y"), lax.axis_index("c")
    chip = 2 * xi + yi

    shards = {n: w[n][0].astype(WIRE_DTYPE) for n in BIG}
    tap_full = []
    for n in COL_SHARDED_SMALL:
        s = w[n][0]
        full = jnp.zeros((s.shape[0], N_CHIPS * s.shape[1]), F32)
        s = jnp.where(ci == 0, s, jnp.zeros_like(s))
        tap_full.append(lax.dynamic_update_slice(full, s, (0, chip * s.shape[1])))
    small = {n: (w[n] if w[n].ndim == 1 else w[n][0]) for n in SMALL if n not in COL_SHARDED_SMALL}
    small = {n: (a.reshape(1, -1) if a.ndim == 1 else a) for n, a in small.items()}

    grad_x, big_g, summed, loss, small_names, packed_shapes = _step(
        x[0], mem[0], loss_target[0], shards, small, _pack_rows(tap_full), [t.shape for t in tap_full])
    small_sum = dict(zip(small_names, _unpack_rows(summed, packed_shapes)[1:]))

    grads = {}
    for n in WEIGHTS:
        if n in BIG:
            g = big_g[n]
        elif n in COL_SHARDED_SMALL:
            width = w[n].shape[-1]
            g = lax.dynamic_slice_in_dim(small_sum[n], chip * width, width, axis=1)
        else:
            g = small_sum[n]
        grads[n] = g.reshape(w[n].shape)

    delta, new_m, new_v = {}, {}, {}
    for n in BIG:
        d, nm, nv = _adamw(w[n][0], grads[n][0], mom[n][0], var[n][0], "adamw_" + n)
        delta[n], new_m[n], new_v[n] = d[None], nm[None], nv[None]
    shapes = [w[n].shape for n in SMALL]
    d, nm, nv = _adamw(_pack_rows([w[n] for n in SMALL]), _pack_rows([grads[n] for n in SMALL]),
                       _pack_rows([mom[n] for n in SMALL]), _pack_rows([var[n] for n in SMALL]), "adamw_small")
    for out, buf in ((delta, d), (new_m, nm), (new_v, nv)):
        out.update(dict(zip(SMALL, _unpack_rows(buf, shapes))))

    return (loss[0, 0], grad_x[None], *[grads[n] for n in WEIGHTS], *[delta[n] for n in WEIGHTS],
            *[new_m[n] for n in WEIGHTS], *[new_v[n] for n in WEIGHTS])
```

```python
import math

import jax
import jax.numpy as jnp
from jax import lax
from jax.experimental import pallas as pl
from jax.experimental.pallas import tpu as pltpu

F32 = jnp.float32
MXU_DTYPE = jnp.bfloat16
WIRE_DTYPE = jnp.bfloat16
EPS = 1e-6
RG_C = 8.0
XA_HEADS = 4
ADAM_LR, ADAM_B1, ADAM_B2, ADAM_EPS, ADAM_WD, ADAM_STEP = 0.001, 0.9, 0.999, 1e-08, 0.01, 10
VMEM_LIMIT_BYTES = 52 * 1024 * 1024
WGRAD_ACC_BYTES = 8 * 1024 * 1024
LANES = 1024
N_CHIPS = 4
N_DEV = 8
MESH = pl.DeviceIdType.MESH
GELU_C = math.sqrt(2.0 / math.pi)
GELU_K = 0.044715

WEIGHTS = ['mix_norm_g', 'w_in', 'lru_conv_w', 'lru_conv_b', 'lru_w_a', 'lru_b_a', 'lru_w_x', 'lru_b_x',
           'lru_lambda', 'conf_conv_w', 'conf_conv_b', 'conf_ln_g', 'conf_ln_b', 'w_out', 'xa_norm_g',
           'mem_norm_g', 'w_q', 'w_kv', 'w_o', 'ffn_norm_g', 'w_up', 'ffn_conv_w', 'ffn_conv_b', 'w_down',
           'final_norm_g']
BIG = ['w_in', 'w_kv', 'w_up', 'w_out', 'w_q', 'w_o', 'w_down']
SMALL = [n for n in WEIGHTS if n not in BIG]
COL_SHARDED_SMALL = ['lru_conv_w', 'conf_conv_w', 'ffn_conv_w']


def _params(*semantics):
    return pltpu.CompilerParams(dimension_semantics=semantics, vmem_limit_bytes=VMEM_LIMIT_BYTES)


ANY = pl.BlockSpec(memory_space=pl.ANY)
WHOLE_VMEM = pl.BlockSpec(memory_space=pltpu.VMEM)


class _Comm:
    def __init__(self, arrays, out_shapes, scratch, start, finish, aliases=None, in_specs=None, out_specs=None):
        self.arrays, self.out_shapes, self.scratch = list(arrays), list(out_shapes), list(scratch)
        self.start, self.finish = start, finish
        self.aliases = dict(aliases or {})
        self.in_specs = list(in_specs) if in_specs is not None else [ANY] * len(self.arrays)
        self.out_specs = list(out_specs) if out_specs is not None else [ANY] * len(self.out_shapes)


def _merge(*comms):
    comms = [c for c in comms if c is not None]
    if not comms:
        return None
    ai = [0]
    for c in comms:
        ai.append(ai[-1] + len(c.arrays))
    oi = [0]
    for c in comms:
        oi.append(oi[-1] + len(c.out_shapes))
    si = [0]
    for c in comms:
        si.append(si[-1] + len(c.scratch))

    def each(which):
        def run(ins, outs, scr):
            for k, c in enumerate(comms):
                getattr(c, which)(ins[ai[k]:ai[k + 1]], outs[oi[k]:oi[k + 1]], scr[si[k]:si[k + 1]])
        return run

    aliases = {ai[k] + i: oi[k] + o for k, c in enumerate(comms) for i, o in c.aliases.items()}
    return _Comm(sum((c.arrays for c in comms), []), sum((c.out_shapes for c in comms), []),
                 sum((c.scratch for c in comms), []), each("start"), each("finish"), aliases,
                 sum((c.in_specs for c in comms), []), sum((c.out_specs for c in comms), []))


def _split(outs, *comms):
    parts, at = [], 0
    for c in comms:
        parts.append(outs[at:at + len(c.out_shapes)])
        at += len(c.out_shapes)
    return parts


def _pcall(comm, body, *, name, grid, in_specs, out_specs, out_shape, semantics, scratch_shapes=(), aliases=None):
    single = not isinstance(out_shape, (list, tuple))
    out_shape = [out_shape] if single else list(out_shape)
    out_specs = [out_specs] if single else list(out_specs)
    in_specs, scratch_shapes = list(in_specs), list(scratch_shapes)
    aliases = dict(aliases or {})

    if comm is None:
        def plain(*args):
            return list(pl.pallas_call(body, name=name, grid=grid, in_specs=in_specs, out_specs=out_specs,
                                       out_shape=out_shape, scratch_shapes=scratch_shapes,
                                       input_output_aliases=aliases,
                                       compiler_params=_params(*semantics))(*args))
        return plain

    def hosted(*args):
        n_in, n_out, n_scr = len(args), len(out_shape), len(scratch_shapes)
        c_in, c_out = len(comm.arrays), len(comm.out_shapes)

        def wrapped(*refs):
            ins, cins = refs[:n_in], refs[n_in:n_in + c_in]
            o0 = n_in + c_in
            outs, couts = refs[o0:o0 + n_out], refs[o0 + n_out:o0 + n_out + c_out]
            s0 = o0 + n_out + c_out
            scr, cscr = refs[s0:s0 + n_scr], refs[s0 + n_scr:]
            first = last = None
            for axis, size in enumerate(grid):
                at_start, at_end = pl.program_id(axis) == 0, pl.program_id(axis) == size - 1
                first = at_start if first is None else first & at_start
                last = at_end if last is None else last & at_end
            if first is None:
                comm.start(cins, couts, cscr)
                body(*ins, *outs, *scr)
                comm.finish(cins, couts, cscr)
                return
            pl.when(first)(lambda: comm.start(cins, couts, cscr))
            body(*ins, *outs, *scr)
            pl.when(last)(lambda: comm.finish(cins, couts, cscr))

        res = pl.pallas_call(
            wrapped, name=name, grid=grid, in_specs=in_specs + comm.in_specs, out_specs=out_specs + comm.out_specs,
            out_shape=out_shape + comm.out_shapes, scratch_shapes=scratch_shapes + comm.scratch,
            input_output_aliases={**aliases, **{n_in + i: n_out + o for i, o in comm.aliases.items()}},
            compiler_params=pltpu.CompilerParams(dimension_semantics=("arbitrary",) * len(grid),
                                                 vmem_limit_bytes=VMEM_LIMIT_BYTES, has_side_effects=True),
        )(*args, *comm.arrays)
        return list(res[:n_out]), list(res[n_out:])

    return hosted


def _run_comm(comm, name):
    return _pcall(comm, lambda: None, name=name, grid=(), in_specs=[], out_specs=[], out_shape=[], semantics=())()[1]


def _tile(n, want, align=8):
    if n <= want:
        return n
    for t in range(want - want % align, 0, -align):
        if n % t == 0:
            return t
    raise ValueError((n, want, align))


def _mm(a, b):
    return jnp.dot(a.astype(MXU_DTYPE), b.astype(MXU_DTYPE), preferred_element_type=F32)


def _mm_nt(a, b):
    return lax.dot_general(a.astype(MXU_DTYPE), b.astype(MXU_DTYPE), (((1,), (1,)), ((), ())),
                           preferred_element_type=F32)


def _mm_tn(a, b):
    return lax.dot_general(a.astype(MXU_DTYPE), b.astype(MXU_DTYPE), (((0,), (0,)), ((), ())),
                           preferred_element_type=F32)


def _sigmoid(v):
    return 0.5 * jnp.tanh(0.5 * v) + 0.5


def _gelu(v):
    v2 = v * v
    t = jnp.tanh(v * (GELU_C + (GELU_C * GELU_K) * v2))
    hv = 0.5 * v
    dt = (1.0 - t * t) * (GELU_C + (3.0 * GELU_C * GELU_K) * v2)
    return hv + hv * t, (0.5 + 0.5 * t) + hv * dt


def _softplus_neg(lam):
    e = jnp.exp(-jnp.abs(lam))
    u = 1.0 + e
    log1p_e = jnp.where(u == 1.0, e, jnp.log(u) * e / jnp.where(u == 1.0, 1.0, u - 1.0))
    return jnp.maximum(-lam, 0.0) + log1p_e


def _rms(xv):
    rinv = lax.rsqrt(jnp.mean(xv * xv, axis=-1, keepdims=True) + EPS)
    return rinv, xv * rinv


def _rms_bwd(rinv, xhat, dxhat):
    return rinv * (dxhat - xhat * jnp.mean(dxhat * xhat, axis=-1, keepdims=True))


def _colsum(v):
    return jnp.sum(v, axis=0, keepdims=True)


def _wrow(w_ref, k, wcols):
    return w_ref[pl.ds(k, 1), :] if wcols is None else w_ref[pl.ds(k, 1), wcols]


def _windows(buf_ref, halo, taps, rows):
    return [buf_ref[pl.ds(halo - s, rows), :] for s in range(taps)]


def _causal_from(xs, w_ref, wcols=None):
    taps = len(xs)
    acc = None
    for s in range(taps):
        term = _wrow(w_ref, taps - 1 - s, wcols) * xs[s]
        acc = term if acc is None else acc + term
    return acc


def _tap_grads_from(dw_ref, dy, xs, wcols=None):
    taps = len(xs)
    for s in range(taps):
        g = _colsum(dy * xs[s])
        if wcols is None:
            dw_ref[pl.ds(taps - 1 - s, 1), :] += g
        else:
            dw_ref[pl.ds(taps - 1 - s, 1), wcols] += g


def _causal_taps(buf_ref, halo, w_ref, taps, rows, wcols=None):
    return _causal_from(_windows(buf_ref, halo, taps, rows), w_ref, wcols)


def _anticausal_taps(buf_ref, w_ref, taps, rows, wcols=None):
    acc = None
    for s in range(taps):
        term = _wrow(w_ref, taps - 1 - s, wcols) * buf_ref[pl.ds(s, rows), :]
        acc = term if acc is None else acc + term
    return acc


def _shift_copies(dst_ref, buf_ref, rows, up):
    for r in range(8):
        dst_ref[r] = buf_ref[pl.ds(r if up else 8 - r, rows), :]


def _causal_taps8(sh_ref, halo, w_ref, taps, rows):
    acc = None
    for s in range(taps):
        term = _wrow(w_ref, taps - 1 - s, None) * sh_ref[s % 8, pl.ds(halo - 8 - 8 * (s // 8), rows), :]
        acc = term if acc is None else acc + term
    return acc


def _anticausal_taps8(sh_ref, w_ref, taps, rows):
    acc = None
    for s in range(taps):
        term = _wrow(w_ref, taps - 1 - s, None) * sh_ref[s % 8, pl.ds(8 * (s // 8), rows), :]
        acc = term if acc is None else acc + term
    return acc


def _tap_grads8(dw_ref, dy, sh_ref, halo, taps, rows):
    for s in range(taps):
        dw_ref[pl.ds(taps - 1 - s, 1), :] += _colsum(dy * sh_ref[s % 8, pl.ds(halo - 8 - 8 * (s // 8), rows), :])


def _fwd_in(x, g, w_in, comm=None):
    S, D = x.shape
    nb, _, C = w_in.shape
    ts = _tile(S, 512)

    def body(x_ref, g_ref, w_ref, z_ref, h_ref):
        _, xhat = _rms(x_ref[...])
        h = (xhat * g_ref[...]).astype(MXU_DTYPE)
        h_ref[...] = h
        for j in range(nb):
            z_ref[j] = jnp.dot(h, w_ref[j], preferred_element_type=F32)

    return _pcall(
        comm, body, name="fwd_in", grid=(S // ts,),
        in_specs=[pl.BlockSpec((ts, D), lambda i: (i, 0)), pl.BlockSpec((1, D), lambda i: (0, 0)),
                  pl.BlockSpec((nb, D, C), lambda i: (0, 0, 0))],
        out_specs=[pl.BlockSpec((nb, ts, C), lambda i: (0, i, 0)), pl.BlockSpec((ts, D), lambda i: (i, 0))],
        out_shape=[jax.ShapeDtypeStruct((nb, S, C), F32), jax.ShapeDtypeStruct((S, D), MXU_DTYPE)],
        semantics=("parallel",))(x, g, w_in)


def _lru_gates(xc, wa_ref, ba_ref, wx_ref, bx_ref, sp):
    xb = xc.astype(MXU_DTYPE)
    r = _sigmoid(jnp.dot(xb, wa_ref[...], preferred_element_type=F32) + ba_ref[...])
    ig = _sigmoid(jnp.dot(xb, wx_ref[...], preferred_element_type=F32) + bx_ref[...])
    log_a = -RG_C * r * sp
    a = jnp.exp(log_a)
    one_minus_a2 = jnp.tanh(-log_a) * (a * a + 1.0)
    inv_mult = lax.rsqrt(one_minus_a2)
    mult = jnp.where(one_minus_a2 > 0.0, one_minus_a2 * inv_mult, 0.0)
    return r, ig, a, mult, inv_mult


def _lru_fwd(z, conv_w, conv_b, wa, ba, wx, bx, lam, comm=None):
    _, S, C = z.shape
    ts = _tile(S, 256)
    taps = conv_w.shape[0]
    halo = 8

    def body(zx_ref, zg_ref, cw_ref, cb_ref, wa_ref, ba_ref, wx_ref, bx_ref, lam_ref,
             h_ref, y_ref, xbuf, a_s, u_s, hc):
        i = pl.program_id(0)

        @pl.when(i == 0)
        def _():
            xbuf[pl.ds(0, halo), :] = jnp.zeros((halo, C), F32)
            hc[...] = jnp.zeros_like(hc)

        xbuf[pl.ds(halo, ts), :] = zx_ref[0]
        xc = _causal_taps(xbuf, halo, cw_ref, taps, ts) + cb_ref[...]
        sp = _softplus_neg(lam_ref[...])
        _, ig, a, mult, _ = _lru_gates(xc, wa_ref, ba_ref, wx_ref, bx_ref, sp)
        a_s[...] = a
        u_s[...] = mult * (ig * xc)
        row = lax.broadcasted_iota(jnp.int32, (8, C), 0)

        def step(k, carry):
            off = pl.multiple_of(k * 8, 8)
            av = a_s[pl.ds(off, 8), :]
            uv = u_s[pl.ds(off, 8), :]
            for d in (1, 2, 4):
                m = row >= d
                a_sh = jnp.where(m, pltpu.roll(av, d, 0), 1.0)
                u_sh = jnp.where(m, pltpu.roll(uv, d, 0), 0.0)
                uv = uv + av * u_sh
                av = av * a_sh
            hv = uv + av * carry
            h_ref[pl.ds(off, 8), :] = hv
            return jnp.broadcast_to(hv[7:8, :], (8, C))

        hc[...] = lax.fori_loop(0, ts // 8, step, hc[...])
        ge, _ = _gelu(zg_ref[0])
        y_ref[...] = (h_ref[...] * ge).astype(MXU_DTYPE)
        xbuf[pl.ds(0, halo), :] = xbuf[pl.ds(ts, halo), :]

    vec = pl.BlockSpec((1, C), lambda i: (0, 0))
    mat = pl.BlockSpec((C, C), lambda i: (0, 0))
    return _pcall(
        comm, body, name="lru_fwd", grid=(S // ts,),
        in_specs=[pl.BlockSpec((1, ts, C), lambda i: (0, i, 0)), pl.BlockSpec((1, ts, C), lambda i: (1, i, 0)),
                  pl.BlockSpec((taps, C), lambda i: (0, 0)), vec, mat, vec, mat, vec, vec],
        out_specs=[pl.BlockSpec((ts, C), lambda i: (i, 0)), pl.BlockSpec((ts, C), lambda i: (i, 0))],
        out_shape=[jax.ShapeDtypeStruct((S, C), F32), jax.ShapeDtypeStruct((S, C), MXU_DTYPE)],
        scratch_shapes=[pltpu.VMEM((ts + halo, C), F32), pltpu.VMEM((ts, C), F32), pltpu.VMEM((ts, C), F32),
                        pltpu.VMEM((8, C), F32)],
        semantics=("arbitrary",))(z, z, conv_w, conv_b, wa, ba, wx, bx, lam)


def _layer_norm_stats(c1):
    mu = jnp.mean(c1, axis=-1, keepdims=True)
    xc = c1 - mu
    rstd = lax.rsqrt(jnp.mean(xc * xc, axis=-1, keepdims=True) + EPS)
    return rstd, xc * rstd


def _conf_fwd(z, conv_w, conv_b, ln_g, ln_b, comm=None):
    _, S, C = z.shape
    ts = _tile(S, 256)
    taps = conv_w.shape[0]
    halo = 32

    def body(za_ref, zb_ref, cw_ref, cb_ref, g_ref, b_ref, c1_ref, c3_ref, cbuf, shifted):
        i = pl.program_id(0)

        @pl.when(i == 0)
        def _():
            cbuf[pl.ds(0, halo), :] = jnp.zeros((halo, C), F32)

        cbuf[pl.ds(halo, ts), :] = za_ref[0] * _sigmoid(zb_ref[0])
        _shift_copies(shifted, cbuf, ts + halo - 8, up=False)
        c1 = _causal_taps8(shifted, halo, cw_ref, taps, ts) + cb_ref[...]
        c1_ref[...] = c1
        _, xhat = _layer_norm_stats(c1)
        c2 = xhat * g_ref[...] + b_ref[...]
        c3_ref[...] = (c2 * _sigmoid(c2)).astype(MXU_DTYPE)
        cbuf[pl.ds(0, halo), :] = cbuf[pl.ds(ts, halo), :]

    vec = pl.BlockSpec((1, C), lambda i: (0, 0))
    return _pcall(
        comm, body, name="conf_fwd", grid=(S // ts,),
        in_specs=[pl.BlockSpec((1, ts, C), lambda i: (2, i, 0)), pl.BlockSpec((1, ts, C), lambda i: (3, i, 0)),
                  pl.BlockSpec((taps, C), lambda i: (0, 0)), vec, vec, vec],
        out_specs=[pl.BlockSpec((ts, C), lambda i: (i, 0)), pl.BlockSpec((ts, C), lambda i: (i, 0))],
        out_shape=[jax.ShapeDtypeStruct((S, C), F32), jax.ShapeDtypeStruct((S, C), MXU_DTYPE)],
        scratch_shapes=[pltpu.VMEM((ts + halo, C), F32), pltpu.VMEM((8, ts + halo - 8, C), F32)],
        semantics=("arbitrary",))(z, z, conv_w, conv_b, ln_g, ln_b)


def _fwd_out_q(x, y_lru, c3, w_out, g_xa, w_q, comm=None):
    S, D = x.shape
    C = y_lru.shape[1]
    ts = _tile(S, 512)

    def body(x_ref, yl_ref, c3_ref, wo_ref, g_ref, wq_ref, x1_ref, h2_ref, q_ref):
        x1 = (x_ref[...] + jnp.dot(yl_ref[...], wo_ref[0], preferred_element_type=F32)
              + jnp.dot(c3_ref[...], wo_ref[1], preferred_element_type=F32))
        x1_ref[...] = x1
        _, xhat = _rms(x1)
        h2 = (xhat * g_ref[...]).astype(MXU_DTYPE)
        h2_ref[...] = h2
        q_ref[...] = jnp.dot(h2, wq_ref[...], preferred_element_type=F32).astype(MXU_DTYPE)

    row = lambda w: pl.BlockSpec((ts, w), lambda i: (i, 0))
    return _pcall(
        comm, body, name="fwd_out_q", grid=(S // ts,),
        in_specs=[row(D), row(C), row(C), pl.BlockSpec((2, C, D), lambda i: (0, 0, 0)),
                  pl.BlockSpec((1, D), lambda i: (0, 0)), pl.BlockSpec((D, D), lambda i: (0, 0))],
        out_specs=[row(D), row(D), row(D)],
        out_shape=[jax.ShapeDtypeStruct((S, D), F32), jax.ShapeDtypeStruct((S, D), MXU_DTYPE),
                   jax.ShapeDtypeStruct((S, D), MXU_DTYPE)],
        semantics=("parallel",))(x, y_lru, c3, w_out, g_xa, w_q)


def _kv_fwd(mem, g, w_kv):
    M, D = mem.shape
    nb, _, C = w_kv.shape

    def body(mem_ref, g_ref, w_ref, m_ref, kv_ref):
        _, xhat = _rms(mem_ref[...])
        m = (xhat * g_ref[...]).astype(MXU_DTYPE)
        m_ref[...] = m
        for j in range(nb):
            kv_ref[:, pl.ds(j * C, C)] = jnp.dot(m, w_ref[j], preferred_element_type=F32).astype(MXU_DTYPE)

    return pl.pallas_call(
        body, name="kv_fwd", grid=(1,),
        in_specs=[pl.BlockSpec((M, D), lambda i: (0, 0)), pl.BlockSpec((1, D), lambda i: (0, 0)),
                  pl.BlockSpec((nb, D, C), lambda i: (0, 0, 0))],
        out_specs=[pl.BlockSpec((M, D), lambda i: (0, 0)), pl.BlockSpec((M, nb * C), lambda i: (0, 0))],
        out_shape=[jax.ShapeDtypeStruct((M, D), MXU_DTYPE), jax.ShapeDtypeStruct((M, nb * C), MXU_DTYPE)],
        compiler_params=_params("arbitrary"))(mem, g, w_kv)


def _softmax_rows(s):
    e = jnp.exp(s - jnp.max(s, axis=-1, keepdims=True))
    return e / jnp.sum(e, axis=-1, keepdims=True)


def _attn_fwd(q, kv, x1, w_o, g_ffn, comm=None):
    S, D = x1.shape
    M = kv.shape[0]
    hd = D // XA_HEADS
    scale = hd ** -0.5
    ts = _tile(S, 512)

    def body(q_ref, kv_ref, x1_ref, wo_ref, g_ref, o_ref, x2_ref, h3_ref):
        for h in range(XA_HEADS):
            cols = pl.ds(h * hd, hd)
            p = _softmax_rows(_mm_nt(q_ref[:, cols], kv_ref[:, cols]) * scale)
            o_ref[:, cols] = _mm(p, kv_ref[:, pl.ds(D + h * hd, hd)]).astype(MXU_DTYPE)
        x2 = x1_ref[...] + jnp.dot(o_ref[...], wo_ref[...], preferred_element_type=F32)
        x2_ref[...] = x2
        _, xhat = _rms(x2)
        h3_ref[...] = (xhat * g_ref[...]).astype(MXU_DTYPE)

    row = pl.BlockSpec((ts, D), lambda i: (i, 0))
    return _pcall(
        comm, body, name="attn_fwd", grid=(S // ts,),
        in_specs=[row, pl.BlockSpec((M, 2 * D), lambda i: (0, 0)), row, pl.BlockSpec((D, D), lambda i: (0, 0)),
                  pl.BlockSpec((1, D), lambda i: (0, 0))],
        out_specs=[row, row, row],
        out_shape=[jax.ShapeDtypeStruct((S, D), MXU_DTYPE), jax.ShapeDtypeStruct((S, D), F32),
                   jax.ShapeDtypeStruct((S, D), MXU_DTYPE)],
        semantics=("parallel",))(q, kv, x1, w_o, g_ffn)


def _ffn_fwd(h3, w_up, conv_w, conv_b, w_down, x2, g_final, target, comm=None):
    S, D = h3.shape
    nb, _, CW = w_up.shape
    half = nb // 2
    cb = 768
    per = CW // cb
    J = half * per
    ts = _tile(S, 256)
    taps = conv_w.shape[0]
    halo = 8

    def body(h_ref, wup_ref, cw_ref, cb_ref, wd_ref, x2_ref, gf_ref, t_ref,
             gu_ref, act_ref, dx3_ref, loss_ref, dgf_ref, gbuf):
        i = pl.program_id(0)

        @pl.when(i == 0)
        def _():
            for ref in (loss_ref, dgf_ref, gbuf):
                ref[...] = jnp.zeros_like(ref)

        hv = h_ref[...]
        x3 = x2_ref[...]
        for j in range(J):
            b, cols, wcols = j // per, pl.ds((j % per) * cb, cb), pl.ds(j * cb, cb)
            g = jnp.dot(hv, wup_ref[b, :, cols], preferred_element_type=F32)
            u = jnp.dot(hv, wup_ref[half + b, :, cols], preferred_element_type=F32)
            gu_ref[0, b, :, cols] = g
            gu_ref[1, b, :, cols] = u
            gbuf[j, pl.ds(halo, ts), :] = g
            gc = _causal_taps(gbuf.at[j], halo, cw_ref, taps, ts, wcols=wcols) + cb_ref[:, wcols]
            gbuf[j, pl.ds(0, halo), :] = gbuf[j, pl.ds(ts, halo), :]
            ge, _ = _gelu(gc)
            act = (ge * u).astype(MXU_DTYPE)
            act_ref[j] = act
            x3 = x3 + jnp.dot(act, wd_ref[j], preferred_element_type=F32)
        rinv, xhat = _rms(x3)
        gf = gf_ref[...]
        diff = xhat * gf - t_ref[...]
        loss_ref[...] += _colsum(diff * diff) * (0.5 / D)
        dy = diff * (1.0 / D)
        dgf_ref[...] += _colsum(dy * xhat)
        dx3_ref[...] = _rms_bwd(rinv, xhat, dy * gf)

    row = pl.BlockSpec((ts, D), lambda i: (i, 0))
    vecd = pl.BlockSpec((1, D), lambda i: (0, 0))
    once = pl.Buffered(1)
    sds = jax.ShapeDtypeStruct
    res = _pcall(
        comm, body, name="ffn_fwd", grid=(S // ts,),
        in_specs=[row, pl.BlockSpec((nb, D, CW), lambda i: (0, 0, 0), pipeline_mode=once),
                  pl.BlockSpec((taps, half * CW), lambda i: (0, 0)), pl.BlockSpec((1, half * CW), lambda i: (0, 0)),
                  pl.BlockSpec((J, cb, D), lambda i: (0, 0, 0), pipeline_mode=once), row, vecd, row],
        out_specs=[pl.BlockSpec((2, half, ts, CW), lambda i: (0, 0, i, 0)),
                   pl.BlockSpec((J, ts, cb), lambda i: (0, i, 0)), row, vecd, vecd],
        out_shape=[sds((2, half, S, CW), F32), sds((J, S, cb), MXU_DTYPE), sds((S, D), F32),
                   sds((1, D), F32), sds((1, D), F32)],
        scratch_shapes=[pltpu.VMEM((J, ts + halo, cb), F32)],
        semantics=("arbitrary",))(h3, w_up, conv_w, conv_b, w_down.reshape(J, cb, D), x2, g_final, target)
    outs = res if comm is None else res[0]
    outs = [outs[0].reshape(nb, S, CW)] + list(outs[1:])
    return outs if comm is None else (outs, res[1])


def _ffn_bwd(dx3, w_down, w_up, gu, x2, g_ffn, conv_w, conv_b, comm=None):
    nb, S, CW = gu.shape
    half = nb // 2
    D = dx3.shape[1]
    cb = 768
    per = CW // cb
    J = half * per
    ts = _tile(S, 256)
    n = S // ts
    taps = conv_w.shape[0]
    halo = 8
    hb = ts // halo

    def body(dx_ref, x2_ref, gf_ref, wd_ref, wup_ref, gu_ref, gh_ref, cw_ref, cb_ref,
             dgu_ref, dx2_ref, dgf_ref, dcw_ref, dcb_ref, gbuf, dbuf):
        i = pl.program_id(0)
        r = n - 1 - i

        @pl.when(i == 0)
        def _():
            for ref in (dgf_ref, dcw_ref, dcb_ref, dbuf):
                ref[...] = jnp.zeros_like(ref)

        dx3v = dx_ref[...]
        dxb = dx3v.astype(MXU_DTYPE)
        dh = None
        for j in range(J):
            b, cols, wcols = j // per, pl.ds((j % per) * cb, cb), pl.ds(j * cb, cb)
            dact = _mm_nt(dxb, wd_ref[j])
            gbuf[pl.ds(0, halo), :] = jnp.where(r > 0, gh_ref[0, b, :, cols], 0.0)
            gbuf[pl.ds(halo, ts), :] = gu_ref[0, b, :, cols]
            gs = _windows(gbuf, halo, taps, ts)
            gc = _causal_from(gs, cw_ref, wcols) + cb_ref[:, wcols]
            ge, dge = _gelu(gc)
            dub = (dact * ge).astype(MXU_DTYPE)
            dgc = dact * gu_ref[1, b, :, cols] * dge
            dcb_ref[:, wcols] += _colsum(dgc)
            dbuf[j, pl.ds(0, ts), :] = dgc
            _tap_grads_from(dcw_ref, dgc, gs, wcols)
            dgb = _anticausal_taps(dbuf.at[j], cw_ref, taps, ts, wcols=wcols).astype(MXU_DTYPE)
            dbuf[j, pl.ds(ts, halo), :] = dbuf[j, pl.ds(0, halo), :]
            dgu_ref[0, b, :, cols] = dgb
            dgu_ref[1, b, :, cols] = dub
            part = _mm_nt(dgb, wup_ref[b, :, cols]) + _mm_nt(dub, wup_ref[half + b, :, cols])
            dh = part if dh is None else dh + part
        rinv, xhat = _rms(x2_ref[...])
        dgf_ref[...] += _colsum(dh * xhat)
        dx2_ref[...] = dx3v + _rms_bwd(rinv, xhat, dh * gf_ref[...])

    gu2 = gu.reshape(2, half, S, CW)
    row = pl.BlockSpec((ts, D), lambda i: (n - 1 - i, 0))
    vecd = pl.BlockSpec((1, D), lambda i: (0, 0))
    pair = pl.BlockSpec((2, half, ts, CW), lambda i: (0, 0, n - 1 - i, 0))
    g_prev = pl.BlockSpec((1, half, halo, CW), lambda i: (0, 0, jnp.maximum((n - 1 - i) * hb - 1, 0), 0))
    tapw = pl.BlockSpec((taps, half * CW), lambda i: (0, 0))
    vec = pl.BlockSpec((1, half * CW), lambda i: (0, 0))
    once = pl.Buffered(1)
    sds = jax.ShapeDtypeStruct
    res = _pcall(
        comm, body, name="ffn_bwd", grid=(n,),
        in_specs=[row, row, vecd, pl.BlockSpec((J, cb, D), lambda i: (0, 0, 0), pipeline_mode=once),
                  pl.BlockSpec((nb, D, CW), lambda i: (0, 0, 0), pipeline_mode=once), pair, g_prev, tapw, vec],
        out_specs=[pair, row, vecd, tapw, vec],
        out_shape=[sds((2, half, S, CW), MXU_DTYPE), sds((S, D), F32), sds((1, D), F32),
                   sds((taps, half * CW), F32), sds((1, half * CW), F32)],
        scratch_shapes=[pltpu.VMEM((ts + halo, cb), F32), pltpu.VMEM((J, ts + halo, cb), F32)],
        semantics=("arbitrary",))(dx3, x2, g_ffn, w_down.reshape(J, cb, D), w_up, gu2, gu2, conv_w, conv_b)
    outs = res if comm is None else res[0]
    outs = [outs[0].reshape(nb, S, CW)] + list(outs[1:])
    return outs if comm is None else (outs, res[1])


def _attn_bwd(dx2, w_o, q, kv, x1, g_xa, w_q, comm=None):
    S, D = x1.shape
    M = kv.shape[0]
    hd = D // XA_HEADS
    scale = hd ** -0.5
    ts = _tile(S, 512)

    def body(dx2_ref, wo_ref, q_ref, kv_ref, x1_ref, g_ref, wq_ref, dq_ref, dx1_ref, dkv_ref, dg_ref):
        i = pl.program_id(0)

        @pl.when(i == 0)
        def _():
            dkv_ref[...] = jnp.zeros_like(dkv_ref)
            dg_ref[...] = jnp.zeros_like(dg_ref)

        dx2 = dx2_ref[...]
        do = _mm_nt(dx2, wo_ref[...]).astype(MXU_DTYPE)
        for h in range(XA_HEADS):
            cols = pl.ds(h * hd, hd)
            vcols = pl.ds(D + h * hd, hd)
            qh, kh, doh = q_ref[:, cols], kv_ref[:, cols], do[:, h * hd:(h + 1) * hd]
            p = _softmax_rows(_mm_nt(qh, kh) * scale)
            dp = _mm_nt(doh, kv_ref[:, vcols])
            dkv_ref[:, vcols] += _mm_tn(p, doh)
            ds = (p * (dp - jnp.sum(dp * p, axis=-1, keepdims=True)) * scale).astype(MXU_DTYPE)
            dq_ref[:, cols] = _mm(ds, kh).astype(MXU_DTYPE)
            dkv_ref[:, cols] += _mm_tn(ds, qh)
        dh2 = _mm_nt(dq_ref[...], wq_ref[...])
        rinv, xhat = _rms(x1_ref[...])
        dg_ref[...] += _colsum(dh2 * xhat)
        dx1_ref[...] = dx2 + _rms_bwd(rinv, xhat, dh2 * g_ref[...])

    row = pl.BlockSpec((ts, D), lambda i: (i, 0))
    mat = pl.BlockSpec((D, D), lambda i: (0, 0))
    vecd = pl.BlockSpec((1, D), lambda i: (0, 0))
    kvs = pl.BlockSpec((M, 2 * D), lambda i: (0, 0))
    return _pcall(
        comm, body, name="attn_bwd", grid=(S // ts,),
        in_specs=[row, mat, row, kvs, row, vecd, mat],
        out_specs=[row, row, kvs, vecd],
        out_shape=[jax.ShapeDtypeStruct((S, D), MXU_DTYPE), jax.ShapeDtypeStruct((S, D), F32),
                   jax.ShapeDtypeStruct((M, 2 * D), F32), jax.ShapeDtypeStruct((1, D), F32)],
        semantics=("arbitrary",))(dx2, w_o, q, kv, x1, g_xa, w_q)


def _kv_bwd(dkv, w_kv, mem, g, m):
    M, D = mem.shape
    nb, _, C = w_kv.shape

    def body(dkv_ref, w_ref, mem_ref, m_ref, dw_ref, dg_ref):
        dm = jnp.zeros((M, D), F32)
        for j in range(nb):
            dj = dkv_ref[:, pl.ds(j * C, C)].astype(MXU_DTYPE)
            dw_ref[j] = _mm_tn(m_ref[...], dj).astype(dw_ref.dtype)
            dm = dm + _mm_nt(dj, w_ref[j])
        _, xhat = _rms(mem_ref[...])
        dg_ref[...] = _colsum(dm * xhat)

    full = lambda *s: pl.BlockSpec(s, lambda i: (0,) * len(s))
    return pl.pallas_call(
        body, name="kv_bwd", grid=(1,),
        in_specs=[full(M, nb * C), full(nb, D, C), full(M, D), full(M, D)],
        out_specs=[full(nb, D, C), full(1, D)],
        out_shape=[jax.ShapeDtypeStruct((nb, D, C), WIRE_DTYPE), jax.ShapeDtypeStruct((1, D), F32)],
        compiler_params=_params("arbitrary"))(dkv, w_kv, mem, m)


def _conf_bwd(dx1, w_out_c, z, c1, conv_w, ln_g, ln_b, comm=None):
    _, S, C = z.shape
    D = dx1.shape[1]
    ts = _tile(S, 256)
    n = S // ts
    taps = conv_w.shape[0]
    halo = 32
    hb = ts // halo

    def body(dx_ref, wo_ref, za_ref, zb_ref, zah_ref, zbh_ref, c1_ref, cw_ref, g_ref, b_ref,
             dz_ref, dcw_ref, dcb_ref, dlg_ref, dlb_ref, c0buf, dbuf, shifted):
        i = pl.program_id(0)
        r = n - 1 - i

        @pl.when(i == 0)
        def _():
            for ref in (dcw_ref, dcb_ref, dlg_ref, dlb_ref):
                ref[...] = jnp.zeros_like(ref)
            dbuf[pl.ds(ts, halo), :] = jnp.zeros((halo, C), F32)

        za = za_ref[0]
        sb = _sigmoid(zb_ref[0])
        c0buf[pl.ds(0, halo), :] = jnp.where(r > 0, zah_ref[0] * _sigmoid(zbh_ref[0]), 0.0)
        c0buf[pl.ds(halo, ts), :] = za * sb
        dc3 = _mm_nt(dx_ref[...], wo_ref[...])
        rstd, xhat = _layer_norm_stats(c1_ref[...])
        g = g_ref[...]
        c2 = xhat * g + b_ref[...]
        sg = _sigmoid(c2)
        dc2 = dc3 * sg * (1.0 + c2 * (1.0 - sg))
        dlg_ref[...] += _colsum(dc2 * xhat)
        dlb_ref[...] += _colsum(dc2)
        dxh = dc2 * g
        dc1 = rstd * (dxh - jnp.mean(dxh, axis=-1, keepdims=True)
                      - xhat * jnp.mean(dxh * xhat, axis=-1, keepdims=True))
        dcb_ref[...] += _colsum(dc1)
        dbuf[pl.ds(0, ts), :] = dc1
        _shift_copies(shifted, c0buf, ts + halo - 8, up=False)
        _tap_grads8(dcw_ref, dc1, shifted, halo, taps, ts)
        _shift_copies(shifted, dbuf, ts + halo - 8, up=True)
        dc0 = _anticausal_taps8(shifted, cw_ref, taps, ts)
        dz_ref[0] = (dc0 * sb).astype(MXU_DTYPE)
        dz_ref[1] = (dc0 * za * sb * (1.0 - sb)).astype(MXU_DTYPE)
        dbuf[pl.ds(ts, halo), :] = dbuf[pl.ds(0, halo), :]

    vec = pl.BlockSpec((1, C), lambda i: (0, 0))
    tapw = pl.BlockSpec((taps, C), lambda i: (0, 0))
    tile = lambda b: pl.BlockSpec((1, ts, C), lambda i: (b, n - 1 - i, 0))
    prev = lambda b: pl.BlockSpec((1, halo, C), lambda i: (b, jnp.maximum((n - 1 - i) * hb - 1, 0), 0))
    return _pcall(
        comm, body, name="conf_bwd", grid=(n,),
        in_specs=[pl.BlockSpec((ts, D), lambda i: (n - 1 - i, 0)), pl.BlockSpec((C, D), lambda i: (0, 0)),
                  tile(2), tile(3), prev(2), prev(3), pl.BlockSpec((ts, C), lambda i: (n - 1 - i, 0)),
                  tapw, vec, vec],
        out_specs=[pl.BlockSpec((2, ts, C), lambda i: (1, n - 1 - i, 0)), tapw, vec, vec, vec],
        out_shape=[jax.ShapeDtypeStruct((4, S, C), MXU_DTYPE), jax.ShapeDtypeStruct((taps, C), F32),
                   jax.ShapeDtypeStruct((1, C), F32), jax.ShapeDtypeStruct((1, C), F32),
                   jax.ShapeDtypeStruct((1, C), F32)],
        scratch_shapes=[pltpu.VMEM((ts + halo, C), F32), pltpu.VMEM((ts + halo, C), F32),
                        pltpu.VMEM((8, ts + halo - 8, C), F32)],
        semantics=("arbitrary",))(dx1, w_out_c, z, z, z, z, c1, conv_w, ln_g, ln_b)


def _lru_bwd(dx1, w_out_l, z, h, conv_w, conv_b, wa, ba, wx, bx, lam, dz, comm=None):
    _, S, C = z.shape
    D = dx1.shape[1]
    ts = _tile(S, 256)
    n = S // ts
    taps = conv_w.shape[0]
    halo = 8
    hb = ts // halo

    def body(dx_ref, wo_ref, zx_ref, zxh_ref, zg_ref, h_ref, hh_ref, cw_ref, cb_ref, wa_ref, ba_ref,
             wx_ref, bx_ref, lam_ref, dz_in,
             dz_ref, dwa_ref, dwx_ref, dba_ref, dbx_ref, dlam_ref, dcw_ref, dcb_ref,
             xbuf, hbuf, a_s, w_s, dh_s, g_s, dbuf, pc):
        i = pl.program_id(0)
        r = n - 1 - i

        @pl.when(i == 0)
        def _():
            for ref in (dwa_ref, dwx_ref, dba_ref, dbx_ref, dlam_ref, dcw_ref, dcb_ref, pc):
                ref[...] = jnp.zeros_like(ref)
            dbuf[pl.ds(ts, halo), :] = jnp.zeros((halo, C), F32)

        xbuf[pl.ds(0, halo), :] = jnp.where(r > 0, zxh_ref[0], 0.0)
        xbuf[pl.ds(halo, ts), :] = zx_ref[0]
        hbuf[pl.ds(0, halo), :] = jnp.where(r > 0, hh_ref[...], 0.0)
        hbuf[pl.ds(halo, ts), :] = h_ref[...]
        xs = _windows(xbuf, halo, taps, ts)
        xc = _causal_from(xs, cw_ref) + cb_ref[...]
        lam_v = lam_ref[...]
        sp = _softplus_neg(lam_v)
        rg, ig, a, mult, inv_mult = _lru_gates(xc, wa_ref, ba_ref, wx_ref, bx_ref, sp)

        dy = _mm_nt(dx_ref[...], wo_ref[...])
        ge, dge = _gelu(zg_ref[0])
        dh = dy * ge
        dz_ref[1] = (dy * h_ref[...] * dge).astype(MXU_DTYPE)
        a_s[...] = a
        w_s[...] = a * dh
        dh_s[...] = dh
        row = lax.broadcasted_iota(jnp.int32, (8, C), 0)

        def step(kk, carry):
            off = pl.multiple_of((ts // 8 - 1 - kk) * 8, 8)
            av = a_s[pl.ds(off, 8), :]
            wv = w_s[pl.ds(off, 8), :]
            for d in (1, 2, 4):
                m = row < 8 - d
                a_sh = jnp.where(m, pltpu.roll(av, 8 - d, 0), 1.0)
                w_sh = jnp.where(m, pltpu.roll(wv, 8 - d, 0), 0.0)
                wv = wv + av * w_sh
                av = av * a_sh
            pv = wv + av * carry
            g_s[pl.ds(off, 8), :] = dh_s[pl.ds(off, 8), :] + jnp.where(row < 7, pltpu.roll(pv, 7, 0), carry)
            return jnp.broadcast_to(pv[0:1, :], (8, C))

        pc[...] = lax.fori_loop(0, ts // 8, step, pc[...])
        gt = g_s[...]
        da = gt * hbuf[pl.ds(halo - 1, ts), :]
        gm = gt * mult
        dlog_a = da * a - (gt * ig * xc) * (a * a) * inv_mult
        dlam_ref[...] += _colsum(dlog_a * rg) * (RG_C / (1.0 + jnp.exp(lam_v)))
        dpa = (dlog_a * (-RG_C * sp)) * rg * (1.0 - rg)
        dpx = (gm * xc) * ig * (1.0 - ig)
        dba_ref[...] += _colsum(dpa)
        dbx_ref[...] += _colsum(dpx)
        xb = xc.astype(MXU_DTYPE)
        dpab, dpxb = dpa.astype(MXU_DTYPE), dpx.astype(MXU_DTYPE)
        dwa_ref[...] += _mm_tn(xb, dpab)
        dwx_ref[...] += _mm_tn(xb, dpxb)
        dxc = gm * ig + _mm_nt(dpab, wa_ref[...]) + _mm_nt(dpxb, wx_ref[...])
        dcb_ref[...] += _colsum(dxc)
        dbuf[pl.ds(0, ts), :] = dxc
        _tap_grads_from(dcw_ref, dxc, xs)
        dz_ref[0] = _anticausal_taps(dbuf, cw_ref, taps, ts).astype(MXU_DTYPE)
        dbuf[pl.ds(ts, halo), :] = dbuf[pl.ds(0, halo), :]

    vec = pl.BlockSpec((1, C), lambda i: (0, 0))
    mat = pl.BlockSpec((C, C), lambda i: (0, 0))
    tapw = pl.BlockSpec((taps, C), lambda i: (0, 0))
    prev_rows = lambda i: jnp.maximum((n - 1 - i) * hb - 1, 0)
    sds = jax.ShapeDtypeStruct
    return _pcall(
        comm, body, name="lru_bwd", grid=(n,),
        in_specs=[pl.BlockSpec((ts, D), lambda i: (n - 1 - i, 0)), pl.BlockSpec((C, D), lambda i: (0, 0)),
                  pl.BlockSpec((1, ts, C), lambda i: (0, n - 1 - i, 0)),
                  pl.BlockSpec((1, halo, C), lambda i: (0, prev_rows(i), 0)),
                  pl.BlockSpec((1, ts, C), lambda i: (1, n - 1 - i, 0)),
                  pl.BlockSpec((ts, C), lambda i: (n - 1 - i, 0)),
                  pl.BlockSpec((halo, C), lambda i: (prev_rows(i), 0)),
                  tapw, vec, mat, vec, mat, vec, vec, ANY],
        out_specs=[pl.BlockSpec((2, ts, C), lambda i: (0, n - 1 - i, 0)), mat, mat, vec, vec, vec, tapw, vec],
        out_shape=[sds(dz.shape, MXU_DTYPE), sds((C, C), F32), sds((C, C), F32), sds((1, C), F32),
                   sds((1, C), F32), sds((1, C), F32), sds((taps, C), F32), sds((1, C), F32)],
        scratch_shapes=[pltpu.VMEM((ts + halo, C), F32), pltpu.VMEM((ts + halo, C), F32)]
        + [pltpu.VMEM((ts, C), F32)] * 4 + [pltpu.VMEM((ts + halo, C), F32), pltpu.VMEM((8, C), F32)],
        aliases={14: 0},
        semantics=("arbitrary",))(dx1, w_out_l, z, z, z, h, h, conv_w, conv_b, wa, ba, wx, bx, lam, dz)


def _bwd_in(dz, w_in, x, g, dx1):
    S, D = x.shape
    nb, _, C = w_in.shape
    ts = _tile(S, 512)

    def body(dz_ref, w_ref, x_ref, g_ref, dx1_ref, dx_ref, dg_ref):
        i = pl.program_id(0)

        @pl.when(i == 0)
        def _():
            dg_ref[...] = jnp.zeros_like(dg_ref)

        dh = _mm_nt(dz_ref[0], w_ref[0])
        for j in range(1, nb):
            dh = dh + _mm_nt(dz_ref[j], w_ref[j])
        rinv, xhat = _rms(x_ref[...])
        dg_ref[...] += _colsum(dh * xhat)
        dx_ref[...] = dx1_ref[...] + _rms_bwd(rinv, xhat, dh * g_ref[...])

    row = pl.BlockSpec((ts, D), lambda i: (i, 0))
    vecd = pl.BlockSpec((1, D), lambda i: (0, 0))
    return pl.pallas_call(
        body, name="bwd_in", grid=(S // ts,),
        in_specs=[pl.BlockSpec((nb, ts, C), lambda i: (0, i, 0)), pl.BlockSpec((nb, D, C), lambda i: (0, 0, 0)),
                  row, vecd, row],
        out_specs=[row, vecd],
        out_shape=[jax.ShapeDtypeStruct((S, D), F32), jax.ShapeDtypeStruct((1, D), F32)],
        compiler_params=_params("arbitrary"))(dz, w_in, x, g, dx1)


def _wgrad(a, b, name, comm=None):
    na, S, K = a.shape
    nb, _, N = b.shape
    nj = max(na, nb)
    assert min(na, nb) == 1
    ts = _tile(S, 1024)
    ns = S // ts
    grp = max(g for g in range(1, nj + 1) if nj % g == 0 and g * K * N * 4 <= WGRAD_ACC_BYTES)
    ga, gb = (grp if na > 1 else 1), (grp if nb > 1 else 1)

    def body(a_ref, b_ref, o_ref, acc):
        s = pl.program_id(1)

        @pl.when(s == 0)
        def _():
            acc[...] = jnp.zeros_like(acc)

        for k in range(grp):
            acc[k] += _mm_tn(a_ref[k if na > 1 else 0], b_ref[k if nb > 1 else 0])

        @pl.when(s == ns - 1)
        def _():
            o_ref[...] = acc[...].astype(o_ref.dtype)

    res = _pcall(
        comm, body, name=name, grid=(nj // grp, ns),
        in_specs=[pl.BlockSpec((ga, ts, K), (lambda j, s: (j, s, 0)) if na > 1 else (lambda j, s: (0, s, 0))),
                  pl.BlockSpec((gb, ts, N), (lambda j, s: (j, s, 0)) if nb > 1 else (lambda j, s: (0, s, 0)))],
        out_specs=pl.BlockSpec((grp, K, N), lambda j, s: (j, 0, 0)),
        out_shape=jax.ShapeDtypeStruct((nj, K, N), WIRE_DTYPE),
        scratch_shapes=[pltpu.VMEM((grp, K, N), F32)],
        semantics=("parallel", "arbitrary"))(a, b)
    return res[0] if comm is None else (res[0][0], res[1])


def _place():
    x, y, c = lax.axis_index("x"), lax.axis_index("y"), lax.axis_index("c")
    other_chips = [(1 - x, y), (x, 1 - y), (1 - x, 1 - y)]
    return x, y, c, other_chips


def _gather_weights(shards):
    nt = len(shards)

    def body(*refs):
        src, dst = refs[:nt], refs[nt:2 * nt]
        ici_send, ici_recv, d2d_send, d2d_recv, own_send, own_recv = refs[2 * nt:]
        x, y, c, chips = _place()
        mine = 2 * x + y

        def half(t, pc):
            hr = src[t].shape[0] // 2
            return pl.ds(pc * hr, hr)

        def own(t):
            return pltpu.make_async_remote_copy(
                src_ref=src[t], dst_ref=dst[t].at[mine], send_sem=own_send.at[t], recv_sem=own_recv.at[t],
                device_id=(x, y, 1 - c), device_id_type=MESH)

        def ici(t, k, block, to):
            cx, cy = block
            ref = dst[t].at[2 * cx + cy, half(t, c)]
            return pltpu.make_async_remote_copy(
                src_ref=src[t].at[half(t, c)] if to is not None else ref, dst_ref=ref,
                send_sem=ici_send.at[t, k], recv_sem=ici_recv.at[t, k],
                device_id=(*to, c) if to is not None else (x, y, c), device_id_type=MESH)

        def d2d(t, k, block, pc):
            cx, cy = block
            ref = dst[t].at[2 * cx + cy, half(t, pc)]
            return pltpu.make_async_remote_copy(
                src_ref=ref, dst_ref=ref, send_sem=d2d_send.at[t, k], recv_sem=d2d_recv.at[t, k],
                device_id=(x, y, 1 - c), device_id_type=MESH)

        sends = [ici(t, k, (x, y), chip) for t in range(nt) for k, chip in enumerate(chips)]
        sends += [own(t) for t in range(nt)]
        for cp in sends:
            cp.start()
        passed = []
        for t in range(nt):
            for k, chip in enumerate(chips):
                ici(t, k, chip, None).wait_recv()
                fw = d2d(t, k, chip, c)
                fw.start()
                passed.append(fw)
        for t in range(nt):
            own(t).wait_recv()
            for k, chip in enumerate(chips):
                d2d(t, k, chip, 1 - c).wait_recv()
        for cp in sends + passed:
            cp.wait_send()

    return pl.pallas_call(
        body, name="gather_weights",
        in_specs=[ANY] * nt, out_specs=[ANY] * nt,
        out_shape=[jax.ShapeDtypeStruct((N_CHIPS,) + s.shape, s.dtype) for s in shards],
        scratch_shapes=[pltpu.SemaphoreType.DMA((nt, 3))] * 4 + [pltpu.SemaphoreType.DMA((nt,))] * 2,
        compiler_params=pltpu.CompilerParams(has_side_effects=True))(*shards)


def _gather_over_ici(shards):
    nt = len(shards)

    def copies(src, dst, scr, arriving):
        ici_send, ici_recv, own_send, own_recv = scr
        x, y, c, chips = _place()
        out = []
        for t in range(nt):
            hr = src[t].shape[0] // 2
            rows = pl.ds(c * hr, hr)
            for k, (cx, cy) in enumerate(chips):
                block = 2 * cx + cy if arriving else 2 * x + y
                out.append(pltpu.make_async_remote_copy(
                    src_ref=src[t].at[rows], dst_ref=dst[t].at[block, rows],
                    send_sem=ici_send.at[t, k], recv_sem=ici_recv.at[t, k],
                    device_id=(cx, cy, c), device_id_type=MESH))
            out.append(pltpu.make_async_remote_copy(
                src_ref=src[t], dst_ref=dst[t].at[2 * x + y], send_sem=own_send.at[t], recv_sem=own_recv.at[t],
                device_id=(x, y, 1 - c), device_id_type=MESH))
        return out

    def start(src, dst, scr):
        for cp in copies(src, dst, scr, False):
            cp.start()

    def finish(src, dst, scr):
        for cp in copies(src, dst, scr, True):
            cp.wait_recv()
        for cp in copies(src, dst, scr, False):
            cp.wait_send()

    return _Comm(shards, [jax.ShapeDtypeStruct((N_CHIPS,) + s.shape, s.dtype) for s in shards],
                 [pltpu.SemaphoreType.DMA((nt, 3))] * 2 + [pltpu.SemaphoreType.DMA((nt,))] * 2, start, finish)


def _gather_pass_on(bufs):
    nt = len(bufs)

    def passed(dst, scr, t, k, block, pc):
        send, recv = scr
        x, y, c, _ = _place()
        cx, cy = block
        hr = dst[t].shape[1] // 2
        ref = dst[t].at[2 * cx + cy, pl.ds(pc * hr, hr)]
        return pltpu.make_async_remote_copy(src_ref=ref, dst_ref=ref, send_sem=send.at[t, k], recv_sem=recv.at[t, k],
                                            device_id=(x, y, 1 - c), device_id_type=MESH)

    def start(src, dst, scr):
        _, _, c, chips = _place()
        for t in range(nt):
            for k, chip in enumerate(chips):
                passed(dst, scr, t, k, chip, c).start()

    def finish(src, dst, scr):
        _, _, c, chips = _place()
        for t in range(nt):
            for k, chip in enumerate(chips):
                passed(dst, scr, t, k, chip, 1 - c).wait_recv()
        for t in range(nt):
            for k, chip in enumerate(chips):
                passed(dst, scr, t, k, chip, c).wait_send()

    return _Comm(bufs, [jax.ShapeDtypeStruct(b.shape, b.dtype) for b in bufs],
                 [pltpu.SemaphoreType.DMA((nt, 3))] * 2, start, finish, aliases={t: t for t in range(nt)})


def _exchange_halves(grads):
    nt = len(grads)

    def copies(src, dst, scr):
        send, recv = scr
        x, y, c, _ = _place()
        out = []
        for t in range(nt):
            hr = src[t].shape[1] // 2
            out.append(pltpu.make_async_remote_copy(
                src_ref=src[t].at[:, pl.ds((1 - c) * hr, hr)], dst_ref=dst[t],
                send_sem=send.at[t], recv_sem=recv.at[t], device_id=(x, y, 1 - c), device_id_type=MESH))
        return out

    def start(src, dst, scr):
        for cp in copies(src, dst, scr):
            cp.start()

    def finish(src, dst, scr):
        for cp in copies(src, dst, scr):
            cp.wait()

    return _Comm(grads, [jax.ShapeDtypeStruct((g.shape[0], g.shape[1] // 2, g.shape[2]), g.dtype) for g in grads],
                 [pltpu.SemaphoreType.DMA((nt,))] * 2, start, finish)


def _add_halves(grad, other, name):
    nb, R, C = grad.shape
    hr = R // 2
    tr = _tile(hr, 256, 16)
    steps = hr // tr
    c = lax.axis_index("c").astype(jnp.int32).reshape((1,))

    def body(c_ref, a_ref, b_ref, o_ref):
        o_ref[...] = (a_ref[...].astype(F32) + b_ref[...].astype(F32)).astype(o_ref.dtype)

    return pl.pallas_call(
        body, name=name,
        grid_spec=pltpu.PrefetchScalarGridSpec(
            num_scalar_prefetch=1, grid=(nb, steps),
            in_specs=[pl.BlockSpec((1, tr, C), lambda j, i, c_ref: (j, c_ref[0] * steps + i, 0)),
                      pl.BlockSpec((1, tr, C), lambda j, i, c_ref: (j, i, 0))],
            out_specs=pl.BlockSpec((1, tr, C), lambda j, i, c_ref: (j, i, 0))),
        out_shape=jax.ShapeDtypeStruct((nb, hr, C), grad.dtype),
        compiler_params=_params("parallel", "parallel"))(c, grad, other)


def _scatter_chip_sums(parts):
    nt = len(parts)

    def copies(src, dst, scr):
        send, recv = scr
        x, y, c, chips = _place()
        out = []
        for t in range(nt):
            for k, (cx, cy) in enumerate(chips):
                out.append(pltpu.make_async_remote_copy(
                    src_ref=src[t].at[2 * cx + cy], dst_ref=dst[t].at[k],
                    send_sem=send.at[t, k], recv_sem=recv.at[t, k], device_id=(cx, cy, c), device_id_type=MESH))
        return out

    def start(src, dst, scr):
        for cp in copies(src, dst, scr):
            cp.start()

    def finish(src, dst, scr):
        for cp in copies(src, dst, scr):
            cp.wait()

    return _Comm(parts, [jax.ShapeDtypeStruct((3,) + p.shape[1:], p.dtype) for p in parts],
                 [pltpu.SemaphoreType.DMA((nt, 3))] * 2, start, finish)


def _sum_chips(part, recv, name):
    _, hr, C = part.shape
    tr = _tile(hr, 256, 16)
    steps = hr // tr
    where = jnp.stack([2 * lax.axis_index("x") + lax.axis_index("y"), lax.axis_index("c")]).astype(jnp.int32)

    def body(w_ref, a_ref, b_ref, o_ref):
        acc = a_ref[0].astype(F32)
        for k in range(3):
            acc = acc + b_ref[k].astype(F32)
        o_ref[...] = acc

    return pl.pallas_call(
        body, name=name,
        grid_spec=pltpu.PrefetchScalarGridSpec(
            num_scalar_prefetch=1, grid=(steps,),
            in_specs=[pl.BlockSpec((1, tr, C), lambda i, w_ref: (w_ref[0], i, 0)),
                      pl.BlockSpec((3, tr, C), lambda i, w_ref: (0, i, 0))],
            out_specs=pl.BlockSpec((tr, C), lambda i, w_ref: (w_ref[1] * steps + i, 0))),
        out_shape=jax.ShapeDtypeStruct((2 * hr, C), F32),
        compiler_params=_params("parallel"))(where, part, recv)


def _join_halves(bufs):
    nt = len(bufs)

    def swap(dst, scr, t, pc):
        send, recv = scr
        x, y, c, _ = _place()
        hr = dst[t].shape[0] // 2
        rows = dst[t].at[pl.ds(pc * hr, hr)]
        return pltpu.make_async_remote_copy(src_ref=rows, dst_ref=rows, send_sem=send.at[t], recv_sem=recv.at[t],
                                            device_id=(x, y, 1 - c), device_id_type=MESH)

    def start(src, dst, scr):
        c = lax.axis_index("c")
        for t in range(nt):
            swap(dst, scr, t, c).start()

    def finish(src, dst, scr):
        c = lax.axis_index("c")
        for t in range(nt):
            swap(dst, scr, t, 1 - c).wait_recv()
        for t in range(nt):
            swap(dst, scr, t, c).wait_send()

    return _Comm(bufs, [jax.ShapeDtypeStruct(b.shape, b.dtype) for b in bufs],
                 [pltpu.SemaphoreType.DMA((nt,))] * 2, start, finish, aliases={t: t for t in range(nt)})


def _all_reduce_rows(buf, loss_row=None):
    R, L = buf.shape

    def copies(in_ref, gath, send, recv):
        x, y, c, _ = _place()
        out = []
        for k in range(1, N_DEV):
            peer = (x ^ ((k >> 2) & 1), y ^ ((k >> 1) & 1), c ^ (k & 1))
            out.append(pltpu.make_async_remote_copy(
                src_ref=in_ref, dst_ref=gath.at[k], send_sem=send.at[k - 1], recv_sem=recv.at[k - 1],
                device_id=peer, device_id_type=MESH))
        return out

    def start(ins, outs, scr):
        gath, send, recv = scr
        gath[0] = ins[0][...]
        for cp in copies(ins[0], gath, send, recv):
            cp.start()

    def finish(ins, outs, scr):
        gath, send, recv = scr
        for cp in copies(ins[0], gath, send, recv):
            cp.wait()
        x, y, c, _ = _place()
        me = 4 * x + 2 * y + c
        total = gath[me]
        for d in range(1, N_DEV):
            total = total + gath[d ^ me]
        outs[0][...] = total
        if loss_row is not None:
            outs[1][...] = jnp.sum(total[loss_row:loss_row + 1, :], axis=1, keepdims=True)

    out_shape = [jax.ShapeDtypeStruct((R, L), F32)]
    if loss_row is not None:
        out_shape.append(jax.ShapeDtypeStruct((1, 1), F32))
    return _Comm([buf], out_shape,
                 [pltpu.VMEM((N_DEV, R, L), F32), pltpu.SemaphoreType.DMA((N_DEV - 1,)),
                  pltpu.SemaphoreType.DMA((N_DEV - 1,))],
                 start, finish, in_specs=[WHOLE_VMEM], out_specs=[WHOLE_VMEM] * len(out_shape))


def _adamw(w, g, m, v, name):
    R, C = w.shape
    tr = _tile(R, 256)
    c1 = 1.0 - ADAM_B1 ** ADAM_STEP
    c2 = 1.0 - ADAM_B2 ** ADAM_STEP

    def body(w_ref, g_ref, m_ref, v_ref, d_ref, nm_ref, nv_ref):
        gv = g_ref[...]
        nm = ADAM_B1 * m_ref[...] + (1.0 - ADAM_B1) * gv
        nv = ADAM_B2 * v_ref[...] + (1.0 - ADAM_B2) * (gv * gv)
        nm_ref[...] = nm
        nv_ref[...] = nv
        d_ref[...] = -ADAM_LR * ((nm / c1) / (jnp.sqrt(nv / c2) + ADAM_EPS) + ADAM_WD * w_ref[...])

    blk = pl.BlockSpec((tr, C), lambda i: (i, 0))
    return pl.pallas_call(
        body, name=name, grid=(R // tr,), in_specs=[blk] * 4, out_specs=[blk] * 3,
        out_shape=[jax.ShapeDtypeStruct((R, C), F32)] * 3,
        compiler_params=_params("parallel"))(w, g, m, v)


def _pack_rows(arrays):
    rows = []
    for a in arrays:
        flat = a.reshape(-1).astype(F32)
        pad = (-flat.shape[0]) % LANES
        rows.append(jnp.pad(flat, (0, pad)).reshape(-1, LANES))
    buf = jnp.concatenate(rows, axis=0)
    return jnp.pad(buf, ((0, (-buf.shape[0]) % 8), (0, 0)))


def _unpack_rows(buf, shapes):
    out, r = [], 0
    for s in shapes:
        n = math.prod(s)
        nr = -(-n // LANES)
        out.append(buf[r:r + nr].reshape(-1)[:n].reshape(s))
        r += nr
    return out


def _block_diag(w):
    H, a, b = w.shape
    eye = jnp.eye(H, dtype=w.dtype)
    return (eye[:, None, :, None] * w[:, :, None, :]).reshape(H * a, H * b)


def _block_diag_parts(d, H):
    a, b = d.shape[0] // H, d.shape[1] // H
    d4 = d.reshape(H, a, H, b)
    return jnp.stack([d4[h, :, h, :] for h in range(H)])


def _rs_add(names, grads, others):
    return [_add_halves(g, o, "rs_add_halves_" + n) for n, g, o in zip(names, grads, others)]


def _rs_sum(names, parts, recvs):
    return [_sum_chips(p, r, "rs_sum_chips_" + n) for n, p, r in zip(names, parts, recvs)]


def _step(x, mem, target, shards, small, tap_rows, tap_shapes):
    D = x.shape[1]
    nch = N_CHIPS
    p = dict(small)

    (w_in_f,) = _gather_weights([shards['w_in']])
    wf = {}

    def ici(names):
        return _gather_over_ici([shards[n] for n in names])

    ici_a, taps_sum = ici(['w_out', 'w_q']), _all_reduce_rows(tap_rows)
    (z, h1), couts = _fwd_in(x, p['mix_norm_g'], w_in_f, comm=_merge(ici_a, taps_sum))
    bufs_a, (taps,) = _split(couts, ici_a, taps_sum)
    p.update(zip(COL_SHARDED_SMALL, _unpack_rows(taps, tap_shapes)))
    wa_d = _block_diag(p['lru_w_a']).astype(MXU_DTYPE)
    wx_d = _block_diag(p['lru_w_x']).astype(MXU_DTYPE)
    heads = p['lru_w_a'].shape[0]
    pass_a, ici_b = _gather_pass_on(bufs_a), ici(['w_kv', 'w_o'])
    (h, y_lru), couts = _lru_fwd(z, p['lru_conv_w'], p['lru_conv_b'], wa_d, p['lru_b_a'], wx_d, p['lru_b_x'],
                                 p['lru_lambda'], comm=_merge(pass_a, ici_b))
    (wf['w_out'], wf['w_q']), bufs_b = _split(couts, pass_a, ici_b)
    pass_b, ici_c = _gather_pass_on(bufs_b), ici(['w_up'])
    (c1, c3), couts = _conf_fwd(z, p['conf_conv_w'], p['conf_conv_b'], p['conf_ln_g'], p['conf_ln_b'],
                                comm=_merge(pass_b, ici_c))
    (wf['w_kv'], wf['w_o']), bufs_c = _split(couts, pass_b, ici_c)
    w_out2 = wf['w_out'].reshape(2, -1, D)
    w_q = wf['w_q'].reshape(D, D)
    w_o = wf['w_o'].reshape(D, D)
    pass_c, ici_d = _gather_pass_on(bufs_c), ici(['w_down'])
    (x1, h2, q), couts = _fwd_out_q(x, y_lru, c3, w_out2, p['xa_norm_g'], w_q, comm=_merge(pass_c, ici_d))
    (wf['w_up'],), bufs_d = _split(couts, pass_c, ici_d)
    m, kv = _kv_fwd(mem, p['mem_norm_g'], wf['w_kv'])
    (o, x2, h3), (wf['w_down'],) = _attn_fwd(q, kv, x1, w_o, p['ffn_norm_g'], comm=_gather_pass_on(bufs_d))
    gu, act, dx3, loss_lanes, d_final_g = _ffn_fwd(h3, wf['w_up'], p['ffn_conv_w'], p['ffn_conv_b'], wf['w_down'],
                                                   x2, p['final_norm_g'], target)

    dgu, dx2, d_ffn_g, d_ffn_cw, d_ffn_cb = _ffn_bwd(dx3, wf['w_down'], wf['w_up'], gu, x2, p['ffn_norm_g'],
                                                     p['ffn_conv_w'], p['ffn_conv_b'])
    g_down = _wgrad(act, dx3[None], "wgrad_down").reshape(nch, -1, D)
    g_up, other = _wgrad(h3[None], dgu, "wgrad_up", comm=_exchange_halves([g_down]))
    (p_down,) = _rs_add(['w_down'], [g_down], other)
    sc_down, ex_up = _scatter_chip_sums([p_down]), _exchange_halves([g_up])
    (dq, dx1, dkv, d_xa_g), couts = _attn_bwd(dx2, w_o, q, kv, x1, p['xa_norm_g'], w_q, comm=_merge(sc_down, ex_up))
    recv, other = _split(couts, sc_down, ex_up)
    f_down = _rs_sum(['w_down'], [p_down], recv)
    (p_up,) = _rs_add(['w_up'], [g_up], other)
    mid = ['w_o', 'w_q', 'w_kv']
    g_o = _wgrad(o[None], dx2[None], "wgrad_o").reshape(nch, -1, D)
    g_q = _wgrad(h2[None], dq[None], "wgrad_q").reshape(nch, -1, D)
    g_kv, d_mem_g = _kv_bwd(dkv, wf['w_kv'], mem, p['mem_norm_g'], m)
    join_down, sc_up, ex_mid = _join_halves(f_down), _scatter_chip_sums([p_up]), _exchange_halves([g_o, g_q, g_kv])
    (dz_c, d_conf_cw, d_conf_cb, d_ln_g, d_ln_b), couts = _conf_bwd(
        dx1, w_out2[1], z, c1, p['conf_conv_w'], p['conf_ln_g'], p['conf_ln_b'],
        comm=_merge(join_down, sc_up, ex_mid))
    (r_down,), recv, other = _split(couts, join_down, sc_up, ex_mid)
    p_up = [p_up]
    p_mid = _rs_add(mid, [g_o, g_q, g_kv], other)
    join_up, sc_mid = _join_halves(_rs_sum(['w_up'], p_up, recv)), _scatter_chip_sums(p_mid)
    (dz, d_wa, d_wx, d_ba, d_bx, d_lam, d_lru_cw, d_lru_cb), couts = _lru_bwd(
        dx1, w_out2[0], z, h, p['lru_conv_w'], p['lru_conv_b'], wa_d, p['lru_b_a'], wx_d, p['lru_b_x'],
        p['lru_lambda'], dz_c, comm=_merge(join_up, sc_mid))
    (r_up,), recv = _split(couts, join_up, sc_mid)
    f_mid = _rs_sum(mid, p_mid, recv)
    grad_x, d_mix_g = _bwd_in(dz, w_in_f, x, p['mix_norm_g'], dx1)
    g_out = jnp.concatenate([_wgrad(y_lru[None], dx1[None], "wgrad_out_lru"),
                             _wgrad(c3[None], dx1[None], "wgrad_out_conf")], axis=0).reshape(nch, -1, D)

    small_g = {'mix_norm_g': d_mix_g, 'lru_conv_w': d_lru_cw, 'lru_conv_b': d_lru_cb,
               'lru_w_a': _block_diag_parts(d_wa, heads), 'lru_b_a': d_ba,
               'lru_w_x': _block_diag_parts(d_wx, heads), 'lru_b_x': d_bx, 'lru_lambda': d_lam,
               'conf_conv_w': d_conf_cw, 'conf_conv_b': d_conf_cb, 'conf_ln_g': d_ln_g, 'conf_ln_b': d_ln_b,
               'xa_norm_g': d_xa_g, 'mem_norm_g': d_mem_g, 'ffn_norm_g': d_ffn_g,
               'ffn_conv_w': d_ffn_cw, 'ffn_conv_b': d_ffn_cb, 'final_norm_g': d_final_g}
    names = list(small_g)
    shapes = [small_g[n].shape for n in names]
    join_mid, ex_out = _join_halves(f_mid), _exchange_halves([g_out])
    small_sum = _all_reduce_rows(_pack_rows([loss_lanes] + [small_g[n] for n in names]), loss_row=0)
    g_in, couts = _wgrad(h1[None], dz, "wgrad_in", comm=_merge(join_mid, ex_out, small_sum))
    r_mid, other, (summed, loss) = _split(couts, join_mid, ex_out, small_sum)

    last = ['w_out', 'w_in']
    p_last = _rs_add(['w_out'], [g_out], other)
    p_last += _rs_add(['w_in'], [g_in], _run_comm(_exchange_halves([g_in]), "rs_exchange_w_in"))
    recv = _run_comm(_scatter_chip_sums(p_last), "rs_scatter_last")
    r_last = _run_comm(_join_halves(_rs_sum(last, p_last, recv)), "rs_join_last")
    big = dict(zip(['w_down', 'w_up'] + mid + last, [r_down, r_up] + r_mid + r_last))
    return grad_x, big, summed, loss, names, [loss_lanes.shape] + shapes


def kernel(x, mem, mix_norm_g, w_in, lru_conv_w, lru_conv_b, lru_w_a, lru_b_a, lru_w_x, lru_b_x, lru_lambda, conf_conv_w, conf_conv_b, conf_ln_g, conf_ln_b, w_out, xa_norm_g, mem_norm_g, w_q, w_kv, w_o, ffn_norm_g, w_up, ffn_conv_w, ffn_conv_b, w_down, final_norm_g, loss_target, m_mix_norm_g, m_w_in, m_lru_conv_w, m_lru_conv_b, m_lru_w_a, m_lru_b_a, m_lru_w_x, m_lru_b_x, m_lru_lambda, m_conf_conv_w, m_conf_conv_b, m_conf_ln_g, m_conf_ln_b, m_w_out, m_xa_norm_g, m_mem_norm_g, m_w_q, m_w_kv, m_w_o, m_ffn_norm_g, m_w_up, m_ffn_conv_w, m_ffn_conv_b, m_w_down, m_final_norm_g, v_mix_norm_g, v_w_in, v_lru_conv_w, v_lru_conv_b, v_lru_w_a, v_lru_b_a, v_lru_w_x, v_lru_b_x, v_lru_lambda, v_conf_conv_w, v_conf_conv_b, v_conf_ln_g, v_conf_ln_b, v_w_out, v_xa_norm_g, v_mem_norm_g, v_w_q, v_w_kv, v_w_o, v_ffn_norm_g, v_w_up, v_ffn_conv_w, v_ffn_conv_b, v_w_down, v_final_norm_g):
    given = dict(locals())
    w = {n: given[n] for n in WEIGHTS}
    mom = {n: given["m_" + n] for n in WEIGHTS}
    var = {n: given["v_" + n] for n in WEIGHTS}
    xi, yi, ci = lax.axis_index("x"), lax.axis_index("y"), lax.axis_index("c")
    chip = 2 * xi + yi

    shards = {n: w[n][0].astype(WIRE_DTYPE) for n in BIG}
    tap_full = []
    for n in COL_SHARDED_SMALL:
        s = w[n][0]
        full = jnp.zeros((s.shape[0], N_CHIPS * s.shape[1]), F32)
        s = jnp.where(ci == 0, s, jnp.zeros_like(s))
        tap_full.append(lax.dynamic_update_slice(full, s, (0, chip * s.shape[1])))
    small = {n: (w[n] if w[n].ndim == 1 else w[n][0]) for n in SMALL if n not in COL_SHARDED_SMALL}
    small = {n: (a.reshape(1, -1) if a.ndim == 1 else a) for n, a in small.items()}

    grad_x, big_g, summed, loss, small_names, packed_shapes = _step(
        x[0], mem[0], loss_target[0], shards, small, _pack_rows(tap_full), [t.shape for t in tap_full])
    small_sum = dict(zip(small_names, _unpack_rows(summed, packed_shapes)[1:]))

    grads = {}
    for n in WEIGHTS:
        if n in BIG:
            g = big_g[n]
        elif n in COL_SHARDED_SMALL:
            width = w[n].shape[-1]
            g = lax.dynamic_slice_in_dim(small_sum[n], chip * width, width, axis=1)
        else:
            g = small_sum[n]
        grads[n] = g.reshape(w[n].shape)

    delta, new_m, new_v = {}, {}, {}
    for n in BIG:
        d, nm, nv = _adamw(w[n][0], grads[n][0], mom[n][0], var[n][0], "adamw_" + n)
        delta[n], new_m[n], new_v[n] = d[None], nm[None], nv[None]
    shapes = [w[n].shape for n in SMALL]
    d, nm, nv = _adamw(_pack_rows([w[n] for n in SMALL]), _pack_rows([grads[n] for n in SMALL]),
                       _pack_rows([mom[n] for n in SMALL]), _pack_rows([var[n] for n in SMALL]), "adamw_small")
    for out, buf in ((delta, d), (new_m, nm), (new_v, nv)):
        out.update(dict(zip(SMALL, _unpack_rows(buf, shapes))))

    return (loss[0, 0], grad_x[None], *[grads[n] for n in WEIGHTS], *[delta[n] for n in WEIGHTS],
            *[new_m[n] for n in WEIGHTS], *[new_v[n] for n in WEIGHTS])
```

```python
import math

import jax
import jax.numpy as jnp
from jax import lax
from jax.experimental import pallas as pl
from jax.experimental.pallas import tpu as pltpu

F32 = jnp.float32
MXU_DTYPE = jnp.bfloat16
WIRE_DTYPE = jnp.bfloat16
EPS = 1e-6
RG_C = 8.0
XA_HEADS = 4
ADAM_LR, ADAM_B1, ADAM_B2, ADAM_EPS, ADAM_WD, ADAM_STEP = 0.001, 0.9, 0.999, 1e-08, 0.01, 10
VMEM_LIMIT_BYTES = 52 * 1024 * 1024
WGRAD_ACC_BYTES = 8 * 1024 * 1024
LANES = 1024
N_CHIPS = 4
N_DEV = 8
MESH = pl.DeviceIdType.MESH
GELU_C = math.sqrt(2.0 / math.pi)
GELU_K = 0.044715

WEIGHTS = ['mix_norm_g', 'w_in', 'lru_conv_w', 'lru_conv_b', 'lru_w_a', 'lru_b_a', 'lru_w_x', 'lru_b_x',
           'lru_lambda', 'conf_conv_w', 'conf_conv_b', 'conf_ln_g', 'conf_ln_b', 'w_out', 'xa_norm_g',
           'mem_norm_g', 'w_q', 'w_kv', 'w_o', 'ffn_norm_g', 'w_up', 'ffn_conv_w', 'ffn_conv_b', 'w_down',
           'final_norm_g']
BIG = ['w_in', 'w_kv', 'w_up', 'w_out', 'w_q', 'w_o', 'w_down']
SMALL = [n for n in WEIGHTS if n not in BIG]
COL_SHARDED_SMALL = ['lru_conv_w', 'conf_conv_w', 'ffn_conv_w']


def _params(*semantics):
    return pltpu.CompilerParams(dimension_semantics=semantics, vmem_limit_bytes=VMEM_LIMIT_BYTES)


ANY = pl.BlockSpec(memory_space=pl.ANY)
WHOLE_VMEM = pl.BlockSpec(memory_space=pltpu.VMEM)


class _Comm:
    def __init__(self, arrays, out_shapes, scratch, start, finish, aliases=None, in_specs=None, out_specs=None):
        self.arrays, self.out_shapes, self.scratch = list(arrays), list(out_shapes), list(scratch)
        self.start, self.finish = start, finish
        self.aliases = dict(aliases or {})
        self.in_specs = list(in_specs) if in_specs is not None else [ANY] * len(self.arrays)
        self.out_specs = list(out_specs) if out_specs is not None else [ANY] * len(self.out_shapes)


def _merge(*comms):
    comms = [c for c in comms if c is not None]
    if not comms:
        return None
    ai = [0]
    for c in comms:
        ai.append(ai[-1] + len(c.arrays))
    oi = [0]
    for c in comms:
        oi.append(oi[-1] + len(c.out_shapes))
    si = [0]
    for c in comms:
        si.append(si[-1] + len(c.scratch))

    def each(which):
        def run(ins, outs, scr):
            for k, c in enumerate(comms):
                getattr(c, which)(ins[ai[k]:ai[k + 1]], outs[oi[k]:oi[k + 1]], scr[si[k]:si[k + 1]])
        return run

    aliases = {ai[k] + i: oi[k] + o for k, c in enumerate(comms) for i, o in c.aliases.items()}
    return _Comm(sum((c.arrays for c in comms), []), sum((c.out_shapes for c in comms), []),
                 sum((c.scratch for c in comms), []), each("start"), each("finish"), aliases,
                 sum((c.in_specs for c in comms), []), sum((c.out_specs for c in comms), []))


def _split(outs, *comms):
    parts, at = [], 0
    for c in comms:
        parts.append(outs[at:at + len(c.out_shapes)])
        at += len(c.out_shapes)
    return parts


def _pcall(comm, body, *, name, grid, in_specs, out_specs, out_shape, semantics, scratch_shapes=(), aliases=None):
    single = not isinstance(out_shape, (list, tuple))
    out_shape = [out_shape] if single else list(out_shape)
    out_specs = [out_specs] if single else list(out_specs)
    in_specs, scratch_shapes = list(in_specs), list(scratch_shapes)
    aliases = dict(aliases or {})

    if comm is None:
        def plain(*args):
            return list(pl.pallas_call(body, name=name, grid=grid, in_specs=in_specs, out_specs=out_specs,
                                       out_shape=out_shape, scratch_shapes=scratch_shapes,
                                       input_output_aliases=aliases,
                                       compiler_params=_params(*semantics))(*args))
        return plain

    def hosted(*args):
        n_in, n_out, n_scr = len(args), len(out_shape), len(scratch_shapes)
        c_in, c_out = len(comm.arrays), len(comm.out_shapes)

        def wrapped(*refs):
            ins, cins = refs[:n_in], refs[n_in:n_in + c_in]
            o0 = n_in + c_in
            outs, couts = refs[o0:o0 + n_out], refs[o0 + n_out:o0 + n_out + c_out]
            s0 = o0 + n_out + c_out
            scr, cscr = refs[s0:s0 + n_scr], refs[s0 + n_scr:]
            first = last = None
            for axis, size in enumerate(grid):
                at_start, at_end = pl.program_id(axis) == 0, pl.program_id(axis) == size - 1
                first = at_start if first is None else first & at_start
                last = at_end if last is None else last & at_end
            if first is None:
                comm.start(cins, couts, cscr)
                body(*ins, *outs, *scr)
                comm.finish(cins, couts, cscr)
                return
            pl.when(first)(lambda: comm.start(cins, couts, cscr))
            body(*ins, *outs, *scr)
            pl.when(last)(lambda: comm.finish(cins, couts, cscr))

        res = pl.pallas_call(
            wrapped, name=name, grid=grid, in_specs=in_specs + comm.in_specs, out_specs=out_specs + comm.out_specs,
            out_shape=out_shape + comm.out_shapes, scratch_shapes=scratch_shapes + comm.scratch,
            input_output_aliases={**aliases, **{n_in + i: n_out + o for i, o in comm.aliases.items()}},
            compiler_params=pltpu.CompilerParams(dimension_semantics=("arbitrary",) * len(grid),
                                                 vmem_limit_bytes=VMEM_LIMIT_BYTES, has_side_effects=True),
        )(*args, *comm.arrays)
        return list(res[:n_out]), list(res[n_out:])

    return hosted


def _run_comm(comm, name):
    return _pcall(comm, lambda: None, name=name, grid=(), in_specs=[], out_specs=[], out_shape=[], semantics=())()[1]


def _tile(n, want, align=8):
    if n <= want:
        return n
    for t in range(want - want % align, 0, -align):
        if n % t == 0:
            return t
    raise ValueError((n, want, align))


def _mm(a, b):
    return jnp.dot(a.astype(MXU_DTYPE), b.astype(MXU_DTYPE), preferred_element_type=F32)


def _mm_nt(a, b):
    return lax.dot_general(a.astype(MXU_DTYPE), b.astype(MXU_DTYPE), (((1,), (1,)), ((), ())),
                           preferred_element_type=F32)


def _mm_tn(a, b):
    return lax.dot_general(a.astype(MXU_DTYPE), b.astype(MXU_DTYPE), (((0,), (0,)), ((), ())),
                           preferred_element_type=F32)


def _sigmoid(v):
    return 0.5 * jnp.tanh(0.5 * v) + 0.5


def _gelu(v):
    v2 = v * v
    t = jnp.tanh(v * (GELU_C + (GELU_C * GELU_K) * v2))
    hv = 0.5 * v
    dt = (1.0 - t * t) * (GELU_C + (3.0 * GELU_C * GELU_K) * v2)
    return hv + hv * t, (0.5 + 0.5 * t) + hv * dt


def _softplus_neg(lam):
    e = jnp.exp(-jnp.abs(lam))
    u = 1.0 + e
    log1p_e = jnp.where(u == 1.0, e, jnp.log(u) * e / jnp.where(u == 1.0, 1.0, u - 1.0))
    return jnp.maximum(-lam, 0.0) + log1p_e


def _rms(xv):
    rinv = lax.rsqrt(jnp.mean(xv * xv, axis=-1, keepdims=True) + EPS)
    return rinv, xv * rinv


def _rms_bwd(rinv, xhat, dxhat):
    return rinv * (dxhat - xhat * jnp.mean(dxhat * xhat, axis=-1, keepdims=True))


def _colsum(v):
    return jnp.sum(v, axis=0, keepdims=True)


def _wrow(w_ref, k, wcols):
    return w_ref[pl.ds(k, 1), :] if wcols is None else w_ref[pl.ds(k, 1), wcols]


def _windows(buf_ref, halo, taps, rows):
    assert taps <= 8 <= halo
    x = buf_ref[pl.ds(halo - 8, rows + 8), :]
    return [x[8:] if s == 0 else pltpu.roll(x, s, 0)[8:] for s in range(taps)]


def _causal_from(xs, w_ref, wcols=None):
    taps = len(xs)
    acc = None
    for s in range(taps):
        term = _wrow(w_ref, taps - 1 - s, wcols) * xs[s]
        acc = term if acc is None else acc + term
    return acc


def _tap_grads_from(dw_ref, dy, xs, wcols=None):
    taps = len(xs)
    for s in range(taps):
        g = _colsum(dy * xs[s])
        if wcols is None:
            dw_ref[pl.ds(taps - 1 - s, 1), :] += g
        else:
            dw_ref[pl.ds(taps - 1 - s, 1), wcols] += g


def _causal_taps(buf_ref, halo, w_ref, taps, rows, wcols=None):
    return _causal_from(_windows(buf_ref, halo, taps, rows), w_ref, wcols)


def _anticausal_taps(buf_ref, w_ref, taps, rows, wcols=None):
    assert taps <= 8
    x = buf_ref[pl.ds(0, rows + 8), :]
    acc = None
    for s in range(taps):
        win = x[:rows] if s == 0 else pltpu.roll(x, rows + 8 - s, 0)[:rows]
        term = _wrow(w_ref, taps - 1 - s, wcols) * win
        acc = term if acc is None else acc + term
    return acc


def _shift_copies(dst_ref, buf_ref, rows, up):
    x = buf_ref[pl.ds(0, rows + 8), :]
    for r in range(8):
        if up:
            dst_ref[r] = x[:rows] if r == 0 else pltpu.roll(x, rows + 8 - r, 0)[:rows]
        else:
            dst_ref[r] = x[8:] if r == 0 else pltpu.roll(x, r, 0)[8:]


def _causal_taps8(sh_ref, halo, w_ref, taps, rows):
    acc = None
    for s in range(taps):
        term = _wrow(w_ref, taps - 1 - s, None) * sh_ref[s % 8, pl.ds(halo - 8 - 8 * (s // 8), rows), :]
        acc = term if acc is None else acc + term
    return acc


def _anticausal_taps8(sh_ref, w_ref, taps, rows):
    acc = None
    for s in range(taps):
        term = _wrow(w_ref, taps - 1 - s, None) * sh_ref[s % 8, pl.ds(8 * (s // 8), rows), :]
        acc = term if acc is None else acc + term
    return acc


def _tap_grads8(dw_ref, dy, sh_ref, halo, taps, rows):
    for s in range(taps):
        dw_ref[pl.ds(taps - 1 - s, 1), :] += _colsum(dy * sh_ref[s % 8, pl.ds(halo - 8 - 8 * (s // 8), rows), :])


def _fwd_in(x, g, w_in, comm=None):
    S, D = x.shape
    nb, _, C = w_in.shape
    ts = _tile(S, 512)

    def body(x_ref, g_ref, w_ref, z_ref, h_ref):
        _, xhat = _rms(x_ref[...])
        h = (xhat * g_ref[...]).astype(MXU_DTYPE)
        h_ref[...] = h
        for j in range(nb):
            z_ref[j] = jnp.dot(h, w_ref[j], preferred_element_type=F32)

    return _pcall(
        comm, body, name="fwd_in", grid=(S // ts,),
        in_specs=[pl.BlockSpec((ts, D), lambda i: (i, 0)), pl.BlockSpec((1, D), lambda i: (0, 0)),
                  pl.BlockSpec((nb, D, C), lambda i: (0, 0, 0))],
        out_specs=[pl.BlockSpec((nb, ts, C), lambda i: (0, i, 0)), pl.BlockSpec((ts, D), lambda i: (i, 0))],
        out_shape=[jax.ShapeDtypeStruct((nb, S, C), F32), jax.ShapeDtypeStruct((S, D), MXU_DTYPE)],
        semantics=("parallel",))(x, g, w_in)


def _lru_gates(xc, wa_ref, ba_ref, wx_ref, bx_ref, sp):
    xb = xc.astype(MXU_DTYPE)
    r = _sigmoid(jnp.dot(xb, wa_ref[...], preferred_element_type=F32) + ba_ref[...])
    ig = _sigmoid(jnp.dot(xb, wx_ref[...], preferred_element_type=F32) + bx_ref[...])
    log_a = -RG_C * r * sp
    a = jnp.exp(log_a)
    one_minus_a2 = jnp.tanh(-log_a) * (a * a + 1.0)
    inv_mult = lax.rsqrt(one_minus_a2)
    mult = jnp.where(one_minus_a2 > 0.0, one_minus_a2 * inv_mult, 0.0)
    return r, ig, a, mult, inv_mult


def _lru_fwd(z, conv_w, conv_b, wa, ba, wx, bx, lam, comm=None):
    _, S, C = z.shape
    ts = _tile(S, 256)
    taps = conv_w.shape[0]
    halo = 8

    def body(zx_ref, zg_ref, cw_ref, cb_ref, wa_ref, ba_ref, wx_ref, bx_ref, lam_ref,
             h_ref, y_ref, xbuf, a_s, u_s, hc):
        i = pl.program_id(0)

        @pl.when(i == 0)
        def _():
            xbuf[pl.ds(0, halo), :] = jnp.zeros((halo, C), F32)
            hc[...] = jnp.zeros_like(hc)

        xbuf[pl.ds(halo, ts), :] = zx_ref[0]
        xc = _causal_taps(xbuf, halo, cw_ref, taps, ts) + cb_ref[...]
        sp = _softplus_neg(lam_ref[...])
        _, ig, a, mult, _ = _lru_gates(xc, wa_ref, ba_ref, wx_ref, bx_ref, sp)
        a_s[...] = a
        u_s[...] = mult * (ig * xc)
        row = lax.broadcasted_iota(jnp.int32, (8, C), 0)

        def step(k, carry):
            off = pl.multiple_of(k * 8, 8)
            av = a_s[pl.ds(off, 8), :]
            uv = u_s[pl.ds(off, 8), :]
            for d in (1, 2, 4):
                m = row >= d
                a_sh = jnp.where(m, pltpu.roll(av, d, 0), 1.0)
                u_sh = jnp.where(m, pltpu.roll(uv, d, 0), 0.0)
                uv = uv + av * u_sh
                av = av * a_sh
            hv = uv + av * carry
            h_ref[pl.ds(off, 8), :] = hv
            return jnp.broadcast_to(hv[7:8, :], (8, C))

        hc[...] = lax.fori_loop(0, ts // 8, step, hc[...], unroll=4)
        ge, _ = _gelu(zg_ref[0])
        y_ref[...] = (h_ref[...] * ge).astype(MXU_DTYPE)
        xbuf[pl.ds(0, halo), :] = xbuf[pl.ds(ts, halo), :]

    vec = pl.BlockSpec((1, C), lambda i: (0, 0))
    mat = pl.BlockSpec((C, C), lambda i: (0, 0))
    return _pcall(
        comm, body, name="lru_fwd", grid=(S // ts,),
        in_specs=[pl.BlockSpec((1, ts, C), lambda i: (0, i, 0)), pl.BlockSpec((1, ts, C), lambda i: (1, i, 0)),
                  pl.BlockSpec((taps, C), lambda i: (0, 0)), vec, mat, vec, mat, vec, vec],
        out_specs=[pl.BlockSpec((ts, C), lambda i: (i, 0)), pl.BlockSpec((ts, C), lambda i: (i, 0))],
        out_shape=[jax.ShapeDtypeStruct((S, C), F32), jax.ShapeDtypeStruct((S, C), MXU_DTYPE)],
        scratch_shapes=[pltpu.VMEM((ts + halo, C), F32), pltpu.VMEM((ts, C), F32), pltpu.VMEM((ts, C), F32),
                        pltpu.VMEM((8, C), F32)],
        semantics=("arbitrary",))(z, z, conv_w, conv_b, wa, ba, wx, bx, lam)


def _layer_norm_stats(c1):
    mu = jnp.mean(c1, axis=-1, keepdims=True)
    xc = c1 - mu
    rstd = lax.rsqrt(jnp.mean(xc * xc, axis=-1, keepdims=True) + EPS)
    return rstd, xc * rstd


def _conf_fwd(z, conv_w, conv_b, ln_g, ln_b, comm=None):
    _, S, C = z.shape
    ts = _tile(S, 256)
    taps = conv_w.shape[0]
    halo = 32

    def body(za_ref, zb_ref, cw_ref, cb_ref, g_ref, b_ref, c1_ref, c3_ref, cbuf, shifted):
        i = pl.program_id(0)

        @pl.when(i == 0)
        def _():
            cbuf[pl.ds(0, halo), :] = jnp.zeros((halo, C), F32)

        cbuf[pl.ds(halo, ts), :] = za_ref[0] * _sigmoid(zb_ref[0])
        _shift_copies(shifted, cbuf, ts + halo - 8, up=False)
        c1 = _causal_taps8(shifted, halo, cw_ref, taps, ts) + cb_ref[...]
        c1_ref[...] = c1
        _, xhat = _layer_norm_stats(c1)
        c2 = xhat * g_ref[...] + b_ref[...]
        c3_ref[...] = (c2 * _sigmoid(c2)).astype(MXU_DTYPE)
        cbuf[pl.ds(0, halo), :] = cbuf[pl.ds(ts, halo), :]

    vec = pl.BlockSpec((1, C), lambda i: (0, 0))
    return _pcall(
        comm, body, name="conf_fwd", grid=(S // ts,),
        in_specs=[pl.BlockSpec((1, ts, C), lambda i: (2, i, 0)), pl.BlockSpec((1, ts, C), lambda i: (3, i, 0)),
                  pl.BlockSpec((taps, C), lambda i: (0, 0)), vec, vec, vec],
        out_specs=[pl.BlockSpec((ts, C), lambda i: (i, 0)), pl.BlockSpec((ts, C), lambda i: (i, 0))],
        out_shape=[jax.ShapeDtypeStruct((S, C), F32), jax.ShapeDtypeStruct((S, C), MXU_DTYPE)],
        scratch_shapes=[pltpu.VMEM((ts + halo, C), F32), pltpu.VMEM((8, ts + halo - 8, C), F32)],
        semantics=("arbitrary",))(z, z, conv_w, conv_b, ln_g, ln_b)


def _fwd_out_q(x, y_lru, c3, w_out, g_xa, w_q, comm=None):
    S, D = x.shape
    C = y_lru.shape[1]
    ts = _tile(S, 512)

    def body(x_ref, yl_ref, c3_ref, wo_ref, g_ref, wq_ref, x1_ref, h2_ref, q_ref):
        x1 = (x_ref[...] + jnp.dot(yl_ref[...], wo_ref[0], preferred_element_type=F32)
              + jnp.dot(c3_ref[...], wo_ref[1], preferred_element_type=F32))
        x1_ref[...] = x1
        _, xhat = _rms(x1)
        h2 = (xhat * g_ref[...]).astype(MXU_DTYPE)
        h2_ref[...] = h2
        q_ref[...] = jnp.dot(h2, wq_ref[...], preferred_element_type=F32).astype(MXU_DTYPE)

    row = lambda w: pl.BlockSpec((ts, w), lambda i: (i, 0))
    return _pcall(
        comm, body, name="fwd_out_q", grid=(S // ts,),
        in_specs=[row(D), row(C), row(C), pl.BlockSpec((2, C, D), lambda i: (0, 0, 0)),
                  pl.BlockSpec((1, D), lambda i: (0, 0)), pl.BlockSpec((D, D), lambda i: (0, 0))],
        out_specs=[row(D), row(D), row(D)],
        out_shape=[jax.ShapeDtypeStruct((S, D), F32), jax.ShapeDtypeStruct((S, D), MXU_DTYPE),
                   jax.ShapeDtypeStruct((S, D), MXU_DTYPE)],
        semantics=("parallel",))(x, y_lru, c3, w_out, g_xa, w_q)


def _kv_fwd(mem, g, w_kv):
    M, D = mem.shape
    nb, _, C = w_kv.shape

    def body(mem_ref, g_ref, w_ref, m_ref, kv_ref):
        _, xhat = _rms(mem_ref[...])
        m = (xhat * g_ref[...]).astype(MXU_DTYPE)
        m_ref[...] = m
        for j in range(nb):
            kv_ref[:, pl.ds(j * C, C)] = jnp.dot(m, w_ref[j], preferred_element_type=F32).astype(MXU_DTYPE)

    return pl.pallas_call(
        body, name="kv_fwd", grid=(1,),
        in_specs=[pl.BlockSpec((M, D), lambda i: (0, 0)), pl.BlockSpec((1, D), lambda i: (0, 0)),
                  pl.BlockSpec((nb, D, C), lambda i: (0, 0, 0))],
        out_specs=[pl.BlockSpec((M, D), lambda i: (0, 0)), pl.BlockSpec((M, nb * C), lambda i: (0, 0))],
        out_shape=[jax.ShapeDtypeStruct((M, D), MXU_DTYPE), jax.ShapeDtypeStruct((M, nb * C), MXU_DTYPE)],
        compiler_params=_params("arbitrary"))(mem, g, w_kv)


def _softmax_rows(s):
    e = jnp.exp(s - jnp.max(s, axis=-1, keepdims=True))
    return e / jnp.sum(e, axis=-1, keepdims=True)


def _attn_fwd(q, kv, x1, w_o, g_ffn, comm=None):
    S, D = x1.shape
    M = kv.shape[0]
    hd = D // XA_HEADS
    scale = hd ** -0.5
    ts = _tile(S, 512)

    def body(q_ref, kv_ref, x1_ref, wo_ref, g_ref, o_ref, x2_ref, h3_ref):
        for h in range(XA_HEADS):
            cols = pl.ds(h * hd, hd)
            p = _softmax_rows(_mm_nt(q_ref[:, cols], kv_ref[:, cols]) * scale)
            o_ref[:, cols] = _mm(p, kv_ref[:, pl.ds(D + h * hd, hd)]).astype(MXU_DTYPE)
        x2 = x1_ref[...] + jnp.dot(o_ref[...], wo_ref[...], preferred_element_type=F32)
        x2_ref[...] = x2
        _, xhat = _rms(x2)
        h3_ref[...] = (xhat * g_ref[...]).astype(MXU_DTYPE)

    row = pl.BlockSpec((ts, D), lambda i: (i, 0))
    return _pcall(
        comm, body, name="attn_fwd", grid=(S // ts,),
        in_specs=[row, pl.BlockSpec((M, 2 * D), lambda i: (0, 0)), row, pl.BlockSpec((D, D), lambda i: (0, 0)),
                  pl.BlockSpec((1, D), lambda i: (0, 0))],
        out_specs=[row, row, row],
        out_shape=[jax.ShapeDtypeStruct((S, D), MXU_DTYPE), jax.ShapeDtypeStruct((S, D), F32),
                   jax.ShapeDtypeStruct((S, D), MXU_DTYPE)],
        semantics=("parallel",))(q, kv, x1, w_o, g_ffn)


def _ffn_fwd(h3, w_up, conv_w, conv_b, w_down, x2, g_final, target, comm=None):
    S, D = h3.shape
    nb, _, CW = w_up.shape
    half = nb // 2
    cb = 768
    per = CW // cb
    J = half * per
    ts = _tile(S, 256)
    taps = conv_w.shape[0]
    halo = 8

    def body(h_ref, wup_ref, cw_ref, cb_ref, wd_ref, x2_ref, gf_ref, t_ref,
             gu_ref, act_ref, dx3_ref, loss_ref, dgf_ref, gbuf):
        i = pl.program_id(0)

        @pl.when(i == 0)
        def _():
            for ref in (loss_ref, dgf_ref, gbuf):
                ref[...] = jnp.zeros_like(ref)

        hv = h_ref[...]
        x3 = x2_ref[...]
        for j in range(J):
            b, cols, wcols = j // per, pl.ds((j % per) * cb, cb), pl.ds(j * cb, cb)
            g = jnp.dot(hv, wup_ref[b, :, cols], preferred_element_type=F32)
            u = jnp.dot(hv, wup_ref[half + b, :, cols], preferred_element_type=F32)
            gu_ref[0, b, :, cols] = g
            gu_ref[1, b, :, cols] = u
            gbuf[j, pl.ds(halo, ts), :] = g
            gc = _causal_taps(gbuf.at[j], halo, cw_ref, taps, ts, wcols=wcols) + cb_ref[:, wcols]
            gbuf[j, pl.ds(0, halo), :] = gbuf[j, pl.ds(ts, halo), :]
            ge, _ = _gelu(gc)
            act = (ge * u).astype(MXU_DTYPE)
            act_ref[j] = act
            x3 = x3 + jnp.dot(act, wd_ref[j], preferred_element_type=F32)
        rinv, xhat = _rms(x3)
        gf = gf_ref[...]
        diff = xhat * gf - t_ref[...]
        loss_ref[...] += _colsum(diff * diff) * (0.5 / D)
        dy = diff * (1.0 / D)
        dgf_ref[...] += _colsum(dy * xhat)
        dx3_ref[...] = _rms_bwd(rinv, xhat, dy * gf)

    row = pl.BlockSpec((ts, D), lambda i: (i, 0))
    vecd = pl.BlockSpec((1, D), lambda i: (0, 0))
    once = pl.Buffered(1)
    sds = jax.ShapeDtypeStruct
    res = _pcall(
        comm, body, name="ffn_fwd", grid=(S // ts,),
        in_specs=[row, pl.BlockSpec((nb, D, CW), lambda i: (0, 0, 0), pipeline_mode=once),
                  pl.BlockSpec((taps, half * CW), lambda i: (0, 0)), pl.BlockSpec((1, half * CW), lambda i: (0, 0)),
                  pl.BlockSpec((J, cb, D), lambda i: (0, 0, 0), pipeline_mode=once), row, vecd, row],
        out_specs=[pl.BlockSpec((2, half, ts, CW), lambda i: (0, 0, i, 0)),
                   pl.BlockSpec((J, ts, cb), lambda i: (0, i, 0)), row, vecd, vecd],
        out_shape=[sds((2, half, S, CW), F32), sds((J, S, cb), MXU_DTYPE), sds((S, D), F32),
                   sds((1, D), F32), sds((1, D), F32)],
        scratch_shapes=[pltpu.VMEM((J, ts + halo, cb), F32)],
        semantics=("arbitrary",))(h3, w_up, conv_w, conv_b, w_down.reshape(J, cb, D), x2, g_final, target)
    outs = res if comm is None else res[0]
    outs = [outs[0].reshape(nb, S, CW)] + list(outs[1:])
    return outs if comm is None else (outs, res[1])


def _ffn_bwd(dx3, w_down, w_up, gu, x2, g_ffn, conv_w, conv_b, comm=None):
    nb, S, CW = gu.shape
    half = nb // 2
    D = dx3.shape[1]
    cb = 768
    per = CW // cb
    J = half * per
    ts = _tile(S, 256)
    n = S // ts
    taps = conv_w.shape[0]
    halo = 8
    hb = ts // halo

    def body(dx_ref, x2_ref, gf_ref, wd_ref, wup_ref, gu_ref, gh_ref, cw_ref, cb_ref,
             dgu_ref, dx2_ref, dgf_ref, dcw_ref, dcb_ref, gbuf, dbuf):
        i = pl.program_id(0)
        r = n - 1 - i

        @pl.when(i == 0)
        def _():
            for ref in (dgf_ref, dcw_ref, dcb_ref, dbuf):
                ref[...] = jnp.zeros_like(ref)

        dx3v = dx_ref[...]
        dxb = dx3v.astype(MXU_DTYPE)
        dacts = [_mm_nt(dxb, wd_ref[j]) for j in range(J)]
        dh = None
        for j in range(J):
            b, cols, wcols = j // per, pl.ds((j % per) * cb, cb), pl.ds(j * cb, cb)
            dact = dacts[j]
            gbuf[pl.ds(0, halo), :] = jnp.where(r > 0, gh_ref[0, b, :, cols], 0.0)
            gbuf[pl.ds(halo, ts), :] = gu_ref[0, b, :, cols]
            gs = _windows(gbuf, halo, taps, ts)
            gc = _causal_from(gs, cw_ref, wcols) + cb_ref[:, wcols]
            ge, dge = _gelu(gc)
            dub = (dact * ge).astype(MXU_DTYPE)
            dgc = dact * gu_ref[1, b, :, cols] * dge
            dcb_ref[:, wcols] += _colsum(dgc)
            dbuf[j, pl.ds(0, ts), :] = dgc
            _tap_grads_from(dcw_ref, dgc, gs, wcols)
            dgb = _anticausal_taps(dbuf.at[j], cw_ref, taps, ts, wcols=wcols).astype(MXU_DTYPE)
            dbuf[j, pl.ds(ts, halo), :] = dbuf[j, pl.ds(0, halo), :]
            dgu_ref[0, b, :, cols] = dgb
            dgu_ref[1, b, :, cols] = dub
            part = _mm_nt(dgb, wup_ref[b, :, cols]) + _mm_nt(dub, wup_ref[half + b, :, cols])
            dh = part if dh is None else dh + part
        rinv, xhat = _rms(x2_ref[...])
        dgf_ref[...] += _colsum(dh * xhat)
        dx2_ref[...] = dx3v + _rms_bwd(rinv, xhat, dh * gf_ref[...])

    gu2 = gu.reshape(2, half, S, CW)
    row = pl.BlockSpec((ts, D), lambda i: (n - 1 - i, 0))
    vecd = pl.BlockSpec((1, D), lambda i: (0, 0))
    pair = pl.BlockSpec((2, half, ts, CW), lambda i: (0, 0, n - 1 - i, 0))
    g_prev = pl.BlockSpec((1, half, halo, CW), lambda i: (0, 0, jnp.maximum((n - 1 - i) * hb - 1, 0), 0))
    tapw = pl.BlockSpec((taps, half * CW), lambda i: (0, 0))
    vec = pl.BlockSpec((1, half * CW), lambda i: (0, 0))
    once = pl.Buffered(1)
    sds = jax.ShapeDtypeStruct
    res = _pcall(
        comm, body, name="ffn_bwd", grid=(n,),
        in_specs=[row, row, vecd, pl.BlockSpec((J, cb, D), lambda i: (0, 0, 0), pipeline_mode=once),
                  pl.BlockSpec((nb, D, CW), lambda i: (0, 0, 0), pipeline_mode=once), pair, g_prev, tapw, vec],
        out_specs=[pair, row, vecd, tapw, vec],
        out_shape=[sds((2, half, S, CW), MXU_DTYPE), sds((S, D), F32), sds((1, D), F32),
                   sds((taps, half * CW), F32), sds((1, half * CW), F32)],
        scratch_shapes=[pltpu.VMEM((ts + halo, cb), F32), pltpu.VMEM((J, ts + halo, cb), F32)],
        semantics=("arbitrary",))(dx3, x2, g_ffn, w_down.reshape(J, cb, D), w_up, gu2, gu2, conv_w, conv_b)
    outs = res if comm is None else res[0]
    outs = [outs[0].reshape(nb, S, CW)] + list(outs[1:])
    return outs if comm is None else (outs, res[1])


def _attn_bwd(dx2, w_o, q, kv, x1, g_xa, w_q, comm=None):
    S, D = x1.shape
    M = kv.shape[0]
    hd = D // XA_HEADS
    scale = hd ** -0.5
    ts = _tile(S, 512)

    def body(dx2_ref, wo_ref, q_ref, kv_ref, x1_ref, g_ref, wq_ref, dq_ref, dx1_ref, dkv_ref, dg_ref):
        i = pl.program_id(0)

        @pl.when(i == 0)
        def _():
            dkv_ref[...] = jnp.zeros_like(dkv_ref)
            dg_ref[...] = jnp.zeros_like(dg_ref)

        dx2 = dx2_ref[...]
        do = _mm_nt(dx2, wo_ref[...]).astype(MXU_DTYPE)
        for h in range(XA_HEADS):
            cols = pl.ds(h * hd, hd)
            vcols = pl.ds(D + h * hd, hd)
            qh, kh, doh = q_ref[:, cols], kv_ref[:, cols], do[:, h * hd:(h + 1) * hd]
            p = _softmax_rows(_mm_nt(qh, kh) * scale)
            dp = _mm_nt(doh, kv_ref[:, vcols])
            dkv_ref[:, vcols] += _mm_tn(p, doh)
            ds = (p * (dp - jnp.sum(dp * p, axis=-1, keepdims=True)) * scale).astype(MXU_DTYPE)
            dq_ref[:, cols] = _mm(ds, kh).astype(MXU_DTYPE)
            dkv_ref[:, cols] += _mm_tn(ds, qh)
        dh2 = _mm_nt(dq_ref[...], wq_ref[...])
        rinv, xhat = _rms(x1_ref[...])
        dg_ref[...] += _colsum(dh2 * xhat)
        dx1_ref[...] = dx2 + _rms_bwd(rinv, xhat, dh2 * g_ref[...])

    row = pl.BlockSpec((ts, D), lambda i: (i, 0))
    mat = pl.BlockSpec((D, D), lambda i: (0, 0))
    vecd = pl.BlockSpec((1, D), lambda i: (0, 0))
    kvs = pl.BlockSpec((M, 2 * D), lambda i: (0, 0))
    return _pcall(
        comm, body, name="attn_bwd", grid=(S // ts,),
        in_specs=[row, mat, row, kvs, row, vecd, mat],
        out_specs=[row, row, kvs, vecd],
        out_shape=[jax.ShapeDtypeStruct((S, D), MXU_DTYPE), jax.ShapeDtypeStruct((S, D), F32),
                   jax.ShapeDtypeStruct((M, 2 * D), F32), jax.ShapeDtypeStruct((1, D), F32)],
        semantics=("arbitrary",))(dx2, w_o, q, kv, x1, g_xa, w_q)


def _kv_bwd(dkv, w_kv, mem, g, m):
    M, D = mem.shape
    nb, _, C = w_kv.shape

    def body(dkv_ref, w_ref, mem_ref, m_ref, dw_ref, dg_ref):
        dm = jnp.zeros((M, D), F32)
        for j in range(nb):
            dj = dkv_ref[:, pl.ds(j * C, C)].astype(MXU_DTYPE)
            dw_ref[j] = _mm_tn(m_ref[...], dj).astype(dw_ref.dtype)
            dm = dm + _mm_nt(dj, w_ref[j])
        _, xhat = _rms(mem_ref[...])
        dg_ref[...] = _colsum(dm * xhat)

    full = lambda *s: pl.BlockSpec(s, lambda i: (0,) * len(s))
    return pl.pallas_call(
        body, name="kv_bwd", grid=(1,),
        in_specs=[full(M, nb * C), full(nb, D, C), full(M, D), full(M, D)],
        out_specs=[full(nb, D, C), full(1, D)],
        out_shape=[jax.ShapeDtypeStruct((nb, D, C), WIRE_DTYPE), jax.ShapeDtypeStruct((1, D), F32)],
        compiler_params=_params("arbitrary"))(dkv, w_kv, mem, m)


def _conf_bwd(dx1, w_out_c, z, c1, conv_w, ln_g, ln_b, comm=None):
    _, S, C = z.shape
    D = dx1.shape[1]
    ts = _tile(S, 256)
    n = S // ts
    taps = conv_w.shape[0]
    halo = 32
    hb = ts // halo

    def body(dx_ref, wo_ref, za_ref, zb_ref, zah_ref, zbh_ref, c1_ref, cw_ref, g_ref, b_ref,
             dz_ref, dcw_ref, dcb_ref, dlg_ref, dlb_ref, c0buf, dbuf, shifted):
        i = pl.program_id(0)
        r = n - 1 - i

        @pl.when(i == 0)
        def _():
            for ref in (dcw_ref, dcb_ref, dlg_ref, dlb_ref):
                ref[...] = jnp.zeros_like(ref)
            dbuf[pl.ds(ts, halo), :] = jnp.zeros((halo, C), F32)

        za = za_ref[0]
        sb = _sigmoid(zb_ref[0])
        c0buf[pl.ds(0, halo), :] = jnp.where(r > 0, zah_ref[0] * _sigmoid(zbh_ref[0]), 0.0)
        c0buf[pl.ds(halo, ts), :] = za * sb
        dc3 = _mm_nt(dx_ref[...], wo_ref[...])
        rstd, xhat = _layer_norm_stats(c1_ref[...])
        g = g_ref[...]
        c2 = xhat * g + b_ref[...]
        sg = _sigmoid(c2)
        dc2 = dc3 * sg * (1.0 + c2 * (1.0 - sg))
        dlg_ref[...] += _colsum(dc2 * xhat)
        dlb_ref[...] += _colsum(dc2)
        dxh = dc2 * g
        dc1 = rstd * (dxh - jnp.mean(dxh, axis=-1, keepdims=True)
                      - xhat * jnp.mean(dxh * xhat, axis=-1, keepdims=True))
        dcb_ref[...] += _colsum(dc1)
        dbuf[pl.ds(0, ts), :] = dc1
        _shift_copies(shifted, c0buf, ts + halo - 8, up=False)
        _tap_grads8(dcw_ref, dc1, shifted, halo, taps, ts)
        _shift_copies(shifted, dbuf, ts + halo - 8, up=True)
        dc0 = _anticausal_taps8(shifted, cw_ref, taps, ts)
        dz_ref[0] = (dc0 * sb).astype(MXU_DTYPE)
        dz_ref[1] = (dc0 * za * sb * (1.0 - sb)).astype(MXU_DTYPE)
        dbuf[pl.ds(ts, halo), :] = dbuf[pl.ds(0, halo), :]

    vec = pl.BlockSpec((1, C), lambda i: (0, 0))
    tapw = pl.BlockSpec((taps, C), lambda i: (0, 0))
    tile = lambda b: pl.BlockSpec((1, ts, C), lambda i: (b, n - 1 - i, 0))
    prev = lambda b: pl.BlockSpec((1, halo, C), lambda i: (b, jnp.maximum((n - 1 - i) * hb - 1, 0), 0))
    return _pcall(
        comm, body, name="conf_bwd", grid=(n,),
        in_specs=[pl.BlockSpec((ts, D), lambda i: (n - 1 - i, 0)), pl.BlockSpec((C, D), lambda i: (0, 0)),
                  tile(2), tile(3), prev(2), prev(3), pl.BlockSpec((ts, C), lambda i: (n - 1 - i, 0)),
                  tapw, vec, vec],
        out_specs=[pl.BlockSpec((2, ts, C), lambda i: (1, n - 1 - i, 0)), tapw, vec, vec, vec],
        out_shape=[jax.ShapeDtypeStruct((4, S, C), MXU_DTYPE), jax.ShapeDtypeStruct((taps, C), F32),
                   jax.ShapeDtypeStruct((1, C), F32), jax.ShapeDtypeStruct((1, C), F32),
                   jax.ShapeDtypeStruct((1, C), F32)],
        scratch_shapes=[pltpu.VMEM((ts + halo, C), F32), pltpu.VMEM((ts + halo, C), F32),
                        pltpu.VMEM((8, ts + halo - 8, C), F32)],
        semantics=("arbitrary",))(dx1, w_out_c, z, z, z, z, c1, conv_w, ln_g, ln_b)


def _lru_bwd(dx1, w_out_l, z, h, conv_w, conv_b, wa, ba, wx, bx, lam, dz, comm=None):
    _, S, C = z.shape
    D = dx1.shape[1]
    ts = _tile(S, 256)
    n = S // ts
    taps = conv_w.shape[0]
    halo = 8
    hb = ts // halo

    def body(dx_ref, wo_ref, zx_ref, zxh_ref, zg_ref, h_ref, hh_ref, cw_ref, cb_ref, wa_ref, ba_ref,
             wx_ref, bx_ref, lam_ref, dz_in,
             dz_ref, dwa_ref, dwx_ref, dba_ref, dbx_ref, dlam_ref, dcw_ref, dcb_ref,
             xbuf, hbuf, a_s, w_s, dh_s, g_s, dbuf, pc):
        i = pl.program_id(0)
        r = n - 1 - i

        @pl.when(i == 0)
        def _():
            for ref in (dwa_ref, dwx_ref, dba_ref, dbx_ref, dlam_ref, dcw_ref, dcb_ref, pc):
                ref[...] = jnp.zeros_like(ref)
            dbuf[pl.ds(ts, halo), :] = jnp.zeros((halo, C), F32)

        xbuf[pl.ds(0, halo), :] = jnp.where(r > 0, zxh_ref[0], 0.0)
        xbuf[pl.ds(halo, ts), :] = zx_ref[0]
        hbuf[pl.ds(0, halo), :] = jnp.where(r > 0, hh_ref[...], 0.0)
        hbuf[pl.ds(halo, ts), :] = h_ref[...]
        xs = _windows(xbuf, halo, taps, ts)
        xc = _causal_from(xs, cw_ref) + cb_ref[...]
        lam_v = lam_ref[...]
        sp = _softplus_neg(lam_v)
        rg, ig, a, mult, inv_mult = _lru_gates(xc, wa_ref, ba_ref, wx_ref, bx_ref, sp)

        dy = _mm_nt(dx_ref[...], wo_ref[...])
        ge, dge = _gelu(zg_ref[0])
        dh = dy * ge
        dz_ref[1] = (dy * h_ref[...] * dge).astype(MXU_DTYPE)
        a_s[...] = a
        w_s[...] = a * dh
        dh_s[...] = dh
        row = lax.broadcasted_iota(jnp.int32, (8, C), 0)

        def step(kk, carry):
            off = pl.multiple_of((ts // 8 - 1 - kk) * 8, 8)
            av = a_s[pl.ds(off, 8), :]
            wv = w_s[pl.ds(off, 8), :]
            for d in (1, 2, 4):
                m = row < 8 - d
                a_sh = jnp.where(m, pltpu.roll(av, 8 - d, 0), 1.0)
                w_sh = jnp.where(m, pltpu.roll(wv, 8 - d, 0), 0.0)
                wv = wv + av * w_sh
                av = av * a_sh
            pv = wv + av * carry
            g_s[pl.ds(off, 8), :] = dh_s[pl.ds(off, 8), :] + jnp.where(row < 7, pltpu.roll(pv, 7, 0), carry)
            return jnp.broadcast_to(pv[0:1, :], (8, C))

        pc[...] = lax.fori_loop(0, ts // 8, step, pc[...], unroll=4)
        gt = g_s[...]
        da = gt * hbuf[pl.ds(halo - 1, ts), :]
        gm = gt * mult
        dlog_a = da * a - (gt * ig * xc) * (a * a) * inv_mult
        dlam_ref[...] += _colsum(dlog_a * rg) * (RG_C / (1.0 + jnp.exp(lam_v)))
        dpa = (dlog_a * (-RG_C * sp)) * rg * (1.0 - rg)
        dpx = (gm * xc) * ig * (1.0 - ig)
        dba_ref[...] += _colsum(dpa)
        dbx_ref[...] += _colsum(dpx)
        xb = xc.astype(MXU_DTYPE)
        dpab, dpxb = dpa.astype(MXU_DTYPE), dpx.astype(MXU_DTYPE)
        dwa_ref[...] += _mm_tn(xb, dpab)
        dwx_ref[...] += _mm_tn(xb, dpxb)
        dxc = gm * ig + _mm_nt(dpab, wa_ref[...]) + _mm_nt(dpxb, wx_ref[...])
        dcb_ref[...] += _colsum(dxc)
        dbuf[pl.ds(0, ts), :] = dxc
        _tap_grads_from(dcw_ref, dxc, xs)
        dz_ref[0] = _anticausal_taps(dbuf, cw_ref, taps, ts).astype(MXU_DTYPE)
        dbuf[pl.ds(ts, halo), :] = dbuf[pl.ds(0, halo), :]

    vec = pl.BlockSpec((1, C), lambda i: (0, 0))
    mat = pl.BlockSpec((C, C), lambda i: (0, 0))
    tapw = pl.BlockSpec((taps, C), lambda i: (0, 0))
    prev_rows = lambda i: jnp.maximum((n - 1 - i) * hb - 1, 0)
    sds = jax.ShapeDtypeStruct
    return _pcall(
        comm, body, name="lru_bwd", grid=(n,),
        in_specs=[pl.BlockSpec((ts, D), lambda i: (n - 1 - i, 0)), pl.BlockSpec((C, D), lambda i: (0, 0)),
                  pl.BlockSpec((1, ts, C), lambda i: (0, n - 1 - i, 0)),
                  pl.BlockSpec((1, halo, C), lambda i: (0, prev_rows(i), 0)),
                  pl.BlockSpec((1, ts, C), lambda i: (1, n - 1 - i, 0)),
                  pl.BlockSpec((ts, C), lambda i: (n - 1 - i, 0)),
                  pl.BlockSpec((halo, C), lambda i: (prev_rows(i), 0)),
                  tapw, vec, mat, vec, mat, vec, vec, ANY],
        out_specs=[pl.BlockSpec((2, ts, C), lambda i: (0, n - 1 - i, 0)), mat, mat, vec, vec, vec, tapw, vec],
        out_shape=[sds(dz.shape, MXU_DTYPE), sds((C, C), F32), sds((C, C), F32), sds((1, C), F32),
                   sds((1, C), F32), sds((1, C), F32), sds((taps, C), F32), sds((1, C), F32)],
        scratch_shapes=[pltpu.VMEM((ts + halo, C), F32), pltpu.VMEM((ts + halo, C), F32)]
        + [pltpu.VMEM((ts, C), F32)] * 4 + [pltpu.VMEM((ts + halo, C), F32), pltpu.VMEM((8, C), F32)],
        aliases={14: 0},
        semantics=("arbitrary",))(dx1, w_out_l, z, z, z, h, h, conv_w, conv_b, wa, ba, wx, bx, lam, dz)


def _bwd_in(dz, w_in, x, g, dx1):
    S, D = x.shape
    nb, _, C = w_in.shape
    ts = _tile(S, 512)

    def body(dz_ref, w_ref, x_ref, g_ref, dx1_ref, dx_ref, dg_ref):
        i = pl.program_id(0)

        @pl.when(i == 0)
        def _():
            dg_ref[...] = jnp.zeros_like(dg_ref)

        dh = _mm_nt(dz_ref[0], w_ref[0])
        for j in range(1, nb):
            dh = dh + _mm_nt(dz_ref[j], w_ref[j])
        rinv, xhat = _rms(x_ref[...])
        dg_ref[...] += _colsum(dh * xhat)
        dx_ref[...] = dx1_ref[...] + _rms_bwd(rinv, xhat, dh * g_ref[...])

    row = pl.BlockSpec((ts, D), lambda i: (i, 0))
    vecd = pl.BlockSpec((1, D), lambda i: (0, 0))
    return pl.pallas_call(
        body, name="bwd_in", grid=(S // ts,),
        in_specs=[pl.BlockSpec((nb, ts, C), lambda i: (0, i, 0)), pl.BlockSpec((nb, D, C), lambda i: (0, 0, 0)),
                  row, vecd, row],
        out_specs=[row, vecd],
        out_shape=[jax.ShapeDtypeStruct((S, D), F32), jax.ShapeDtypeStruct((1, D), F32)],
        compiler_params=_params("arbitrary"))(dz, w_in, x, g, dx1)


def _wgrad(a, b, name, comm=None):
    na, S, K = a.shape
    nb, _, N = b.shape
    nj = max(na, nb)
    assert min(na, nb) == 1
    ts = _tile(S, 1024)
    ns = S // ts
    grp = max(g for g in range(1, nj + 1) if nj % g == 0 and g * K * N * 4 <= WGRAD_ACC_BYTES)
    ga, gb = (grp if na > 1 else 1), (grp if nb > 1 else 1)

    def body(a_ref, b_ref, o_ref, acc):
        s = pl.program_id(1)

        @pl.when(s == 0)
        def _():
            acc[...] = jnp.zeros_like(acc)

        for k in range(grp):
            acc[k] += _mm_tn(a_ref[k if na > 1 else 0], b_ref[k if nb > 1 else 0])

        @pl.when(s == ns - 1)
        def _():
            o_ref[...] = acc[...].astype(o_ref.dtype)

    res = _pcall(
        comm, body, name=name, grid=(nj // grp, ns),
        in_specs=[pl.BlockSpec((ga, ts, K), (lambda j, s: (j, s, 0)) if na > 1 else (lambda j, s: (0, s, 0))),
                  pl.BlockSpec((gb, ts, N), (lambda j, s: (j, s, 0)) if nb > 1 else (lambda j, s: (0, s, 0)))],
        out_specs=pl.BlockSpec((grp, K, N), lambda j, s: (j, 0, 0)),
        out_shape=jax.ShapeDtypeStruct((nj, K, N), WIRE_DTYPE),
        scratch_shapes=[pltpu.VMEM((grp, K, N), F32)],
        semantics=("parallel", "arbitrary"))(a, b)
    return res[0] if comm is None else (res[0][0], res[1])


def _place():
    x, y, c = lax.axis_index("x"), lax.axis_index("y"), lax.axis_index("c")
    other_chips = [(1 - x, y), (x, 1 - y), (1 - x, 1 - y)]
    return x, y, c, other_chips


def _gather_weights(shards):
    nt = len(shards)

    def body(*refs):
        src, dst = refs[:nt], refs[nt:2 * nt]
        ici_send, ici_recv, d2d_send, d2d_recv, own_send, own_recv = refs[2 * nt:]
        x, y, c, chips = _place()
        mine = 2 * x + y

        def half(t, pc):
            hr = src[t].shape[0] // 2
            return pl.ds(pc * hr, hr)

        def own(t):
            return pltpu.make_async_remote_copy(
                src_ref=src[t], dst_ref=dst[t].at[mine], send_sem=own_send.at[t], recv_sem=own_recv.at[t],
                device_id=(x, y, 1 - c), device_id_type=MESH)

        def ici(t, k, block, to):
            cx, cy = block
            ref = dst[t].at[2 * cx + cy, half(t, c)]
            return pltpu.make_async_remote_copy(
                src_ref=src[t].at[half(t, c)] if to is not None else ref, dst_ref=ref,
                send_sem=ici_send.at[t, k], recv_sem=ici_recv.at[t, k],
                device_id=(*to, c) if to is not None else (x, y, c), device_id_type=MESH)

        def d2d(t, k, block, pc):
            cx, cy = block
            ref = dst[t].at[2 * cx + cy, half(t, pc)]
            return pltpu.make_async_remote_copy(
                src_ref=ref, dst_ref=ref, send_sem=d2d_send.at[t, k], recv_sem=d2d_recv.at[t, k],
                device_id=(x, y, 1 - c), device_id_type=MESH)

        sends = [ici(t, k, (x, y), chip) for t in range(nt) for k, chip in enumerate(chips)]
        sends += [own(t) for t in range(nt)]
        for cp in sends:
            cp.start()
        passed = []
        for t in range(nt):
            for k, chip in enumerate(chips):
                ici(t, k, chip, None).wait_recv()
                fw = d2d(t, k, chip, c)
                fw.start()
                passed.append(fw)
        for t in range(nt):
            own(t).wait_recv()
            for k, chip in enumerate(chips):
                d2d(t, k, chip, 1 - c).wait_recv()
        for cp in sends + passed:
            cp.wait_send()

    return pl.pallas_call(
        body, name="gather_weights",
        in_specs=[ANY] * nt, out_specs=[ANY] * nt,
        out_shape=[jax.ShapeDtypeStruct((N_CHIPS,) + s.shape, s.dtype) for s in shards],
        scratch_shapes=[pltpu.SemaphoreType.DMA((nt, 3))] * 4 + [pltpu.SemaphoreType.DMA((nt,))] * 2,
        compiler_params=pltpu.CompilerParams(has_side_effects=True))(*shards)


def _gather_over_ici(shards):
    nt = len(shards)

    def copies(src, dst, scr, arriving):
        ici_send, ici_recv, own_send, own_recv = scr
        x, y, c, chips = _place()
        out = []
        for t in range(nt):
            hr = src[t].shape[0] // 2
            rows = pl.ds(c * hr, hr)
            for k, (cx, cy) in enumerate(chips):
                block = 2 * cx + cy if arriving else 2 * x + y
                out.append(pltpu.make_async_remote_copy(
                    src_ref=src[t].at[rows], dst_ref=dst[t].at[block, rows],
                    send_sem=ici_send.at[t, k], recv_sem=ici_recv.at[t, k],
                    device_id=(cx, cy, c), device_id_type=MESH))
            out.append(pltpu.make_async_remote_copy(
                src_ref=src[t], dst_ref=dst[t].at[2 * x + y], send_sem=own_send.at[t], recv_sem=own_recv.at[t],
                device_id=(x, y, 1 - c), device_id_type=MESH))
        return out

    def start(src, dst, scr):
        for cp in copies(src, dst, scr, False):
            cp.start()

    def finish(src, dst, scr):
        for cp in copies(src, dst, scr, True):
            cp.wait_recv()
        for cp in copies(src, dst, scr, False):
            cp.wait_send()

    return _Comm(shards, [jax.ShapeDtypeStruct((N_CHIPS,) + s.shape, s.dtype) for s in shards],
                 [pltpu.SemaphoreType.DMA((nt, 3))] * 2 + [pltpu.SemaphoreType.DMA((nt,))] * 2, start, finish)


def _gather_pass_on(bufs):
    nt = len(bufs)

    def passed(dst, scr, t, k, block, pc):
        send, recv = scr
        x, y, c, _ = _place()
        cx, cy = block
        hr = dst[t].shape[1] // 2
        ref = dst[t].at[2 * cx + cy, pl.ds(pc * hr, hr)]
        return pltpu.make_async_remote_copy(src_ref=ref, dst_ref=ref, send_sem=send.at[t, k], recv_sem=recv.at[t, k],
                                            device_id=(x, y, 1 - c), device_id_type=MESH)

    def start(src, dst, scr):
        _, _, c, chips = _place()
        for t in range(nt):
            for k, chip in enumerate(chips):
                passed(dst, scr, t, k, chip, c).start()

    def finish(src, dst, scr):
        _, _, c, chips = _place()
        for t in range(nt):
            for k, chip in enumerate(chips):
                passed(dst, scr, t, k, chip, 1 - c).wait_recv()
        for t in range(nt):
            for k, chip in enumerate(chips):
                passed(dst, scr, t, k, chip, c).wait_send()

    return _Comm(bufs, [jax.ShapeDtypeStruct(b.shape, b.dtype) for b in bufs],
                 [pltpu.SemaphoreType.DMA((nt, 3))] * 2, start, finish, aliases={t: t for t in range(nt)})


def _exchange_halves(grads):
    nt = len(grads)

    def copies(src, dst, scr):
        send, recv = scr
        x, y, c, _ = _place()
        out = []
        for t in range(nt):
            hr = src[t].shape[1] // 2
            out.append(pltpu.make_async_remote_copy(
                src_ref=src[t].at[:, pl.ds((1 - c) * hr, hr)], dst_ref=dst[t],
                send_sem=send.at[t], recv_sem=recv.at[t], device_id=(x, y, 1 - c), device_id_type=MESH))
        return out

    def start(src, dst, scr):
        for cp in copies(src, dst, scr):
            cp.start()

    def finish(src, dst, scr):
        for cp in copies(src, dst, scr):
            cp.wait()

    return _Comm(grads, [jax.ShapeDtypeStruct((g.shape[0], g.shape[1] // 2, g.shape[2]), g.dtype) for g in grads],
                 [pltpu.SemaphoreType.DMA((nt,))] * 2, start, finish)


def _add_halves(grad, other, name):
    nb, R, C = grad.shape
    hr = R // 2
    tr = _tile(hr, 256, 16)
    steps = hr // tr
    c = lax.axis_index("c").astype(jnp.int32).reshape((1,))

    def body(c_ref, a_ref, b_ref, o_ref):
        o_ref[...] = (a_ref[...].astype(F32) + b_ref[...].astype(F32)).astype(o_ref.dtype)

    return pl.pallas_call(
        body, name=name,
        grid_spec=pltpu.PrefetchScalarGridSpec(
            num_scalar_prefetch=1, grid=(nb, steps),
            in_specs=[pl.BlockSpec((1, tr, C), lambda j, i, c_ref: (j, c_ref[0] * steps + i, 0)),
                      pl.BlockSpec((1, tr, C), lambda j, i, c_ref: (j, i, 0))],
            out_specs=pl.BlockSpec((1, tr, C), lambda j, i, c_ref: (j, i, 0))),
        out_shape=jax.ShapeDtypeStruct((nb, hr, C), grad.dtype),
        compiler_params=_params("parallel", "parallel"))(c, grad, other)


def _scatter_chip_sums(parts):
    nt = len(parts)

    def copies(src, dst, scr):
        send, recv = scr
        x, y, c, chips = _place()
        out = []
        for t in range(nt):
            for k, (cx, cy) in enumerate(chips):
                out.append(pltpu.make_async_remote_copy(
                    src_ref=src[t].at[2 * cx + cy], dst_ref=dst[t].at[k],
                    send_sem=send.at[t, k], recv_sem=recv.at[t, k], device_id=(cx, cy, c), device_id_type=MESH))
        return out

    def start(src, dst, scr):
        for cp in copies(src, dst, scr):
            cp.start()

    def finish(src, dst, scr):
        for cp in copies(src, dst, scr):
            cp.wait()

    return _Comm(parts, [jax.ShapeDtypeStruct((3,) + p.shape[1:], p.dtype) for p in parts],
                 [pltpu.SemaphoreType.DMA((nt, 3))] * 2, start, finish)


def _sum_chips(part, recv, name):
    _, hr, C = part.shape
    tr = _tile(hr, 256, 16)
    steps = hr // tr
    where = jnp.stack([2 * lax.axis_index("x") + lax.axis_index("y"), lax.axis_index("c")]).astype(jnp.int32)

    def body(w_ref, a_ref, b_ref, o_ref):
        acc = a_ref[0].astype(F32)
        for k in range(3):
            acc = acc + b_ref[k].astype(F32)
        o_ref[...] = acc

    return pl.pallas_call(
        body, name=name,
        grid_spec=pltpu.PrefetchScalarGridSpec(
            num_scalar_prefetch=1, grid=(steps,),
            in_specs=[pl.BlockSpec((1, tr, C), lambda i, w_ref: (w_ref[0], i, 0)),
                      pl.BlockSpec((3, tr, C), lambda i, w_ref: (0, i, 0))],
            out_specs=pl.BlockSpec((tr, C), lambda i, w_ref: (w_ref[1] * steps + i, 0))),
        out_shape=jax.ShapeDtypeStruct((2 * hr, C), F32),
        compiler_params=_params("parallel"))(where, part, recv)


def _join_halves(bufs):
    nt = len(bufs)

    def swap(dst, scr, t, pc):
        send, recv = scr
        x, y, c, _ = _place()
        hr = dst[t].shape[0] // 2
        rows = dst[t].at[pl.ds(pc * hr, hr)]
        return pltpu.make_async_remote_copy(src_ref=rows, dst_ref=rows, send_sem=send.at[t], recv_sem=recv.at[t],
                                            device_id=(x, y, 1 - c), device_id_type=MESH)

    def start(src, dst, scr):
        c = lax.axis_index("c")
        for t in range(nt):
            swap(dst, scr, t, c).start()

    def finish(src, dst, scr):
        c = lax.axis_index("c")
        for t in range(nt):
            swap(dst, scr, t, 1 - c).wait_recv()
        for t in range(nt):
            swap(dst, scr, t, c).wait_send()

    return _Comm(bufs, [jax.ShapeDtypeStruct(b.shape, b.dtype) for b in bufs],
                 [pltpu.SemaphoreType.DMA((nt,))] * 2, start, finish, aliases={t: t for t in range(nt)})


def _all_reduce_rows(buf, loss_row=None):
    R, L = buf.shape

    def copies(in_ref, gath, send, recv):
        x, y, c, _ = _place()
        out = []
        for k in range(1, N_DEV):
            peer = (x ^ ((k >> 2) & 1), y ^ ((k >> 1) & 1), c ^ (k & 1))
            out.append(pltpu.make_async_remote_copy(
                src_ref=in_ref, dst_ref=gath.at[k], send_sem=send.at[k - 1], recv_sem=recv.at[k - 1],
                device_id=peer, device_id_type=MESH))
        return out

    def start(ins, outs, scr):
        gath, send, recv = scr
        gath[0] = ins[0][...]
        for cp in copies(ins[0], gath, send, recv):
            cp.start()

    def finish(ins, outs, scr):
        gath, send, recv = scr
        for cp in copies(ins[0], gath, send, recv):
            cp.wait()
        x, y, c, _ = _place()
        me = 4 * x + 2 * y + c
        total = gath[me]
        for d in range(1, N_DEV):
            total = total + gath[d ^ me]
        outs[0][...] = total
        if loss_row is not None:
            outs[1][...] = jnp.sum(total[loss_row:loss_row + 1, :], axis=1, keepdims=True)

    out_shape = [jax.ShapeDtypeStruct((R, L), F32)]
    if loss_row is not None:
        out_shape.append(jax.ShapeDtypeStruct((1, 1), F32))
    return _Comm([buf], out_shape,
                 [pltpu.VMEM((N_DEV, R, L), F32), pltpu.SemaphoreType.DMA((N_DEV - 1,)),
                  pltpu.SemaphoreType.DMA((N_DEV - 1,))],
                 start, finish, in_specs=[WHOLE_VMEM], out_specs=[WHOLE_VMEM] * len(out_shape))


def _adamw(w, g, m, v, name):
    R, C = w.shape
    tr = _tile(R, 256)
    c1 = 1.0 - ADAM_B1 ** ADAM_STEP
    c2 = 1.0 - ADAM_B2 ** ADAM_STEP

    def body(w_ref, g_ref, m_ref, v_ref, d_ref, nm_ref, nv_ref):
        gv = g_ref[...]
        nm = ADAM_B1 * m_ref[...] + (1.0 - ADAM_B1) * gv
        nv = ADAM_B2 * v_ref[...] + (1.0 - ADAM_B2) * (gv * gv)
        nm_ref[...] = nm
        nv_ref[...] = nv
        d_ref[...] = -ADAM_LR * ((nm / c1) / (jnp.sqrt(nv / c2) + ADAM_EPS) + ADAM_WD * w_ref[...])

    blk = pl.BlockSpec((tr, C), lambda i: (i, 0))
    return pl.pallas_call(
        body, name=name, grid=(R // tr,), in_specs=[blk] * 4, out_specs=[blk] * 3,
        out_shape=[jax.ShapeDtypeStruct((R, C), F32)] * 3,
        compiler_params=_params("parallel"))(w, g, m, v)


def _pack_rows(arrays):
    rows = []
    for a in arrays:
        flat = a.reshape(-1).astype(F32)
        pad = (-flat.shape[0]) % LANES
        rows.append(jnp.pad(flat, (0, pad)).reshape(-1, LANES))
    buf = jnp.concatenate(rows, axis=0)
    return jnp.pad(buf, ((0, (-buf.shape[0]) % 8), (0, 0)))


def _unpack_rows(buf, shapes):
    out, r = [], 0
    for s in shapes:
        n = math.prod(s)
        nr = -(-n // LANES)
        out.append(buf[r:r + nr].reshape(-1)[:n].reshape(s))
        r += nr
    return out


def _block_diag(w):
    H, a, b = w.shape
    eye = jnp.eye(H, dtype=w.dtype)
    return (eye[:, None, :, None] * w[:, :, None, :]).reshape(H * a, H * b)


def _block_diag_parts(d, H):
    a, b = d.shape[0] // H, d.shape[1] // H
    d4 = d.reshape(H, a, H, b)
    return jnp.stack([d4[h, :, h, :] for h in range(H)])


def _rs_add(names, grads, others):
    return [_add_halves(g, o, "rs_add_halves_" + n) for n, g, o in zip(names, grads, others)]


def _rs_sum(names, parts, recvs):
    return [_sum_chips(p, r, "rs_sum_chips_" + n) for n, p, r in zip(names, parts, recvs)]


def _step(x, mem, target, shards, small, tap_rows, tap_shapes):
    D = x.shape[1]
    nch = N_CHIPS
    p = dict(small)

    (w_in_f,) = _gather_weights([shards['w_in']])
    wf = {}

    def ici(names):
        return _gather_over_ici([shards[n] for n in names])

    ici_a, taps_sum = ici(['w_out', 'w_q']), _all_reduce_rows(tap_rows)
    (z, h1), couts = _fwd_in(x, p['mix_norm_g'], w_in_f, comm=_merge(ici_a, taps_sum))
    bufs_a, (taps,) = _split(couts, ici_a, taps_sum)
    p.update(zip(COL_SHARDED_SMALL, _unpack_rows(taps, tap_shapes)))
    wa_d = _block_diag(p['lru_w_a']).astype(MXU_DTYPE)
    wx_d = _block_diag(p['lru_w_x']).astype(MXU_DTYPE)
    heads = p['lru_w_a'].shape[0]
    pass_a, ici_b = _gather_pass_on(bufs_a), ici(['w_kv', 'w_o'])
    (h, y_lru), couts = _lru_fwd(z, p['lru_conv_w'], p['lru_conv_b'], wa_d, p['lru_b_a'], wx_d, p['lru_b_x'],
                                 p['lru_lambda'], comm=_merge(pass_a, ici_b))
    (wf['w_out'], wf['w_q']), bufs_b = _split(couts, pass_a, ici_b)
    pass_b, ici_c = _gather_pass_on(bufs_b), ici(['w_up'])
    (c1, c3), couts = _conf_fwd(z, p['conf_conv_w'], p['conf_conv_b'], p['conf_ln_g'], p['conf_ln_b'],
                                comm=_merge(pass_b, ici_c))
    (wf['w_kv'], wf['w_o']), bufs_c = _split(couts, pass_b, ici_c)
    w_out2 = wf['w_out'].reshape(2, -1, D)
    w_q = wf['w_q'].reshape(D, D)
    w_o = wf['w_o'].reshape(D, D)
    pass_c, ici_d = _gather_pass_on(bufs_c), ici(['w_down'])
    (x1, h2, q), couts = _fwd_out_q(x, y_lru, c3, w_out2, p['xa_norm_g'], w_q, comm=_merge(pass_c, ici_d))
    (wf['w_up'],), bufs_d = _split(couts, pass_c, ici_d)
    m, kv = _kv_fwd(mem, p['mem_norm_g'], wf['w_kv'])
    (o, x2, h3), (wf['w_down'],) = _attn_fwd(q, kv, x1, w_o, p['ffn_norm_g'], comm=_gather_pass_on(bufs_d))
    gu, act, dx3, loss_lanes, d_final_g = _ffn_fwd(h3, wf['w_up'], p['ffn_conv_w'], p['ffn_conv_b'], wf['w_down'],
                                                   x2, p['final_norm_g'], target)

    dgu, dx2, d_ffn_g, d_ffn_cw, d_ffn_cb = _ffn_bwd(dx3, wf['w_down'], wf['w_up'], gu, x2, p['ffn_norm_g'],
                                                     p['ffn_conv_w'], p['ffn_conv_b'])
    g_down = _wgrad(act, dx3[None], "wgrad_down").reshape(nch, -1, D)
    g_up, other = _wgrad(h3[None], dgu, "wgrad_up", comm=_exchange_halves([g_down]))
    (p_down,) = _rs_add(['w_down'], [g_down], other)
    sc_down, ex_up = _scatter_chip_sums([p_down]), _exchange_halves([g_up])
    (dq, dx1, dkv, d_xa_g), couts = _attn_bwd(dx2, w_o, q, kv, x1, p['xa_norm_g'], w_q, comm=_merge(sc_down, ex_up))
    recv, other = _split(couts, sc_down, ex_up)
    f_down = _rs_sum(['w_down'], [p_down], recv)
    (p_up,) = _rs_add(['w_up'], [g_up], other)
    mid = ['w_o', 'w_q', 'w_kv']
    g_o = _wgrad(o[None], dx2[None], "wgrad_o").reshape(nch, -1, D)
    g_q = _wgrad(h2[None], dq[None], "wgrad_q").reshape(nch, -1, D)
    g_kv, d_mem_g = _kv_bwd(dkv, wf['w_kv'], mem, p['mem_norm_g'], m)
    join_down, sc_up, ex_mid = _join_halves(f_down), _scatter_chip_sums([p_up]), _exchange_halves([g_o, g_q, g_kv])
    (dz_c, d_conf_cw, d_conf_cb, d_ln_g, d_ln_b), couts = _conf_bwd(
        dx1, w_out2[1], z, c1, p['conf_conv_w'], p['conf_ln_g'], p['conf_ln_b'],
        comm=_merge(join_down, sc_up, ex_mid))
    (r_down,), recv, other = _split(couts, join_down, sc_up, ex_mid)
    p_up = [p_up]
    p_mid = _rs_add(mid, [g_o, g_q, g_kv], other)
    join_up, sc_mid = _join_halves(_rs_sum(['w_up'], p_up, recv)), _scatter_chip_sums(p_mid)
    (dz, d_wa, d_wx, d_ba, d_bx, d_lam, d_lru_cw, d_lru_cb), couts = _lru_bwd(
        dx1, w_out2[0], z, h, p['lru_conv_w'], p['lru_conv_b'], wa_d, p['lru_b_a'], wx_d, p['lru_b_x'],
        p['lru_lambda'], dz_c, comm=_merge(join_up, sc_mid))
    (r_up,), recv = _split(couts, join_up, sc_mid)
    f_mid = _rs_sum(mid, p_mid, recv)
    grad_x, d_mix_g = _bwd_in(dz, w_in_f, x, p['mix_norm_g'], dx1)
    g_out = jnp.concatenate([_wgrad(y_lru[None], dx1[None], "wgrad_out_lru"),
                             _wgrad(c3[None], dx1[None], "wgrad_out_conf")], axis=0).reshape(nch, -1, D)

    small_g = {'mix_norm_g': d_mix_g, 'lru_conv_w': d_lru_cw, 'lru_conv_b': d_lru_cb,
               'lru_w_a': _block_diag_parts(d_wa, heads), 'lru_b_a': d_ba,
               'lru_w_x': _block_diag_parts(d_wx, heads), 'lru_b_x': d_bx, 'lru_lambda': d_lam,
               'conf_conv_w': d_conf_cw, 'conf_conv_b': d_conf_cb, 'conf_ln_g': d_ln_g, 'conf_ln_b': d_ln_b,
               'xa_norm_g': d_xa_g, 'mem_norm_g': d_mem_g, 'ffn_norm_g': d_ffn_g,
               'ffn_conv_w': d_ffn_cw, 'ffn_conv_b': d_ffn_cb, 'final_norm_g': d_final_g}
    names = list(small_g)
    shapes = [small_g[n].shape for n in names]
    join_mid, ex_out = _join_halves(f_mid), _exchange_halves([g_out])
    small_sum = _all_reduce_rows(_pack_rows([loss_lanes] + [small_g[n] for n in names]), loss_row=0)
    g_in, couts = _wgrad(h1[None], dz, "wgrad_in", comm=_merge(join_mid, ex_out, small_sum))
    r_mid, other, (summed, loss) = _split(couts, join_mid, ex_out, small_sum)

    last = ['w_out', 'w_in']
    p_last = _rs_add(['w_out'], [g_out], other)
    p_last += _rs_add(['w_in'], [g_in], _run_comm(_exchange_halves([g_in]), "rs_exchange_w_in"))
    recv = _run_comm(_scatter_chip_sums(p_last), "rs_scatter_last")
    r_last = _run_comm(_join_halves(_rs_sum(last, p_last, recv)), "rs_join_last")
    big = dict(zip(['w_down', 'w_up'] + mid + last, [r_down, r_up] + r_mid + r_last))
    return grad_x, big, summed, loss, names, [loss_lanes.shape] + shapes


def kernel(x, mem, mix_norm_g, w_in, lru_conv_w, lru_conv_b, lru_w_a, lru_b_a, lru_w_x, lru_b_x, lru_lambda, conf_conv_w, conf_conv_b, conf_ln_g, conf_ln_b, w_out, xa_norm_g, mem_norm_g, w_q, w_kv, w_o, ffn_norm_g, w_up, ffn_conv_w, ffn_conv_b, w_down, final_norm_g, loss_target, m_mix_norm_g, m_w_in, m_lru_conv_w, m_lru_conv_b, m_lru_w_a, m_lru_b_a, m_lru_w_x, m_lru_b_x, m_lru_lambda, m_conf_conv_w, m_conf_conv_b, m_conf_ln_g, m_conf_ln_b, m_w_out, m_xa_norm_g, m_mem_norm_g, m_w_q, m_w_kv, m_w_o, m_ffn_norm_g, m_w_up, m_ffn_conv_w, m_ffn_conv_b, m_w_down, m_final_norm_g, v_mix_norm_g, v_w_in, v_lru_conv_w, v_lru_conv_b, v_lru_w_a, v_lru_b_a, v_lru_w_x, v_lru_b_x, v_lru_lambda, v_conf_conv_w, v_conf_conv_b, v_conf_ln_g, v_conf_ln_b, v_w_out, v_xa_norm_g, v_mem_norm_g, v_w_q, v_w_kv, v_w_o, v_ffn_norm_g, v_w_up, v_ffn_conv_w, v_ffn_conv_b, v_w_down, v_final_norm_g):
    given = dict(locals())
    w = {n: given[n] for n in WEIGHTS}
    mom = {n: given["m_" + n] for n in WEIGHTS}
    var = {n: given["v_" + n] for n in WEIGHTS}
    xi, yi, ci = lax.axis_index("x"), lax.axis_index("y"), lax.axis_index("c")
    chip = 2 * xi + yi

    shards = {n: w[n][0].astype(WIRE_DTYPE) for n in BIG}
    tap_full = []
    for n in COL_SHARDED_SMALL:
        s = w[n][0]
        full = jnp.zeros((s.shape[0], N_CHIPS * s.shape[1]), F32)
        s = jnp.where(ci == 0, s, jnp.zeros_like(s))
        tap_full.append(lax.dynamic_update_slice(full, s, (0, chip * s.shape[1])))
    small = {n: (w[n] if w[n].ndim == 1 else w[n][0]) for n in SMALL if n not in COL_SHARDED_SMALL}
    small = {n: (a.reshape(1, -1) if a.ndim == 1 else a) for n, a in small.items()}

    grad_x, big_g, summed, loss, small_names, packed_shapes = _step(
        x[0], mem[0], loss_target[0], shards, small, _pack_rows(tap_full), [t.shape for t in tap_full])
    small_sum = dict(zip(small_names, _unpack_rows(summed, packed_shapes)[1:]))

    grads = {}
    for n in WEIGHTS:
        if n in BIG:
            g = big_g[n]
        elif n in COL_SHARDED_SMALL:
            width = w[n].shape[-1]
            g = lax.dynamic_slice_in_dim(small_sum[n], chip * width, width, axis=1)
        else:
            g = small_sum[n]
        grads[n] = g.reshape(w[n].shape)

    delta, new_m, new_v = {}, {}, {}
    for n in BIG:
        d, nm, nv = _adamw(w[n][0], grads[n][0], mom[n][0], var[n][0], "adamw_" + n)
        delta[n], new_m[n], new_v[n] = d[None], nm[None], nv[None]
    shapes = [w[n].shape for n in SMALL]
    d, nm, nv = _adamw(_pack_rows([w[n] for n in SMALL]), _pack_rows([grads[n] for n in SMALL]),
                       _pack_rows([mom[n] for n in SMALL]), _pack_rows([var[n] for n in SMALL]), "adamw_small")
    for out, buf in ((delta, d), (new_m, nm), (new_v, nv)):
        out.update(dict(zip(SMALL, _unpack_rows(buf, shapes))))

    return (loss[0, 0], grad_x[None], *[grads[n] for n in WEIGHTS], *[delta[n] for n in WEIGHTS],
            *[new_m[n] for n in WEIGHTS], *[new_v[n] for n in WEIGHTS])
```

```python
import math

import jax
import jax.numpy as jnp
from jax import lax
from jax.experimental import pallas as pl
from jax.experimental.pallas import tpu as pltpu

F32 = jnp.float32
MXU_DTYPE = jnp.bfloat16
WIRE_DTYPE = jnp.bfloat16
EPS = 1e-6
RG_C = 8.0
XA_HEADS = 4
ADAM_LR, ADAM_B1, ADAM_B2, ADAM_EPS, ADAM_WD, ADAM_STEP = 0.001, 0.9, 0.999, 1e-08, 0.01, 10
VMEM_LIMIT_BYTES = 52 * 1024 * 1024
WGRAD_ACC_BYTES = 8 * 1024 * 1024
LANES = 1024
N_CHIPS = 4
N_DEV = 8
MESH = pl.DeviceIdType.MESH
GELU_C = math.sqrt(2.0 / math.pi)
GELU_K = 0.044715

WEIGHTS = ['mix_norm_g', 'w_in', 'lru_conv_w', 'lru_conv_b', 'lru_w_a', 'lru_b_a', 'lru_w_x', 'lru_b_x',
           'lru_lambda', 'conf_conv_w', 'conf_conv_b', 'conf_ln_g', 'conf_ln_b', 'w_out', 'xa_norm_g',
           'mem_norm_g', 'w_q', 'w_kv', 'w_o', 'ffn_norm_g', 'w_up', 'ffn_conv_w', 'ffn_conv_b', 'w_down',
           'final_norm_g']
BIG = ['w_in', 'w_kv', 'w_up', 'w_out', 'w_q', 'w_o', 'w_down']
SMALL = [n for n in WEIGHTS if n not in BIG]
COL_SHARDED_SMALL = ['lru_conv_w', 'conf_conv_w', 'ffn_conv_w']


def _params(*semantics):
    return pltpu.CompilerParams(dimension_semantics=semantics, vmem_limit_bytes=VMEM_LIMIT_BYTES)


ANY = pl.BlockSpec(memory_space=pl.ANY)
WHOLE_VMEM = pl.BlockSpec(memory_space=pltpu.VMEM)


class _Comm:
    def __init__(self, arrays, out_shapes, scratch, start, finish, aliases=None, in_specs=None, out_specs=None):
        self.arrays, self.out_shapes, self.scratch = list(arrays), list(out_shapes), list(scratch)
        self.start, self.finish = start, finish
        self.aliases = dict(aliases or {})
        self.in_specs = list(in_specs) if in_specs is not None else [ANY] * len(self.arrays)
        self.out_specs = list(out_specs) if out_specs is not None else [ANY] * len(self.out_shapes)


def _merge(*comms):
    comms = [c for c in comms if c is not None]
    if not comms:
        return None
    ai = [0]
    for c in comms:
        ai.append(ai[-1] + len(c.arrays))
    oi = [0]
    for c in comms:
        oi.append(oi[-1] + len(c.out_shapes))
    si = [0]
    for c in comms:
        si.append(si[-1] + len(c.scratch))

    def each(which):
        def run(ins, outs, scr):
            for k, c in enumerate(comms):
                getattr(c, which)(ins[ai[k]:ai[k + 1]], outs[oi[k]:oi[k + 1]], scr[si[k]:si[k + 1]])
        return run

    aliases = {ai[k] + i: oi[k] + o for k, c in enumerate(comms) for i, o in c.aliases.items()}
    return _Comm(sum((c.arrays for c in comms), []), sum((c.out_shapes for c in comms), []),
                 sum((c.scratch for c in comms), []), each("start"), each("finish"), aliases,
                 sum((c.in_specs for c in comms), []), sum((c.out_specs for c in comms), []))


def _split(outs, *comms):
    parts, at = [], 0
    for c in comms:
        parts.append(outs[at:at + len(c.out_shapes)])
        at += len(c.out_shapes)
    return parts


def _pcall(comm, body, *, name, grid, in_specs, out_specs, out_shape, semantics, scratch_shapes=(), aliases=None):
    single = not isinstance(out_shape, (list, tuple))
    out_shape = [out_shape] if single else list(out_shape)
    out_specs = [out_specs] if single else list(out_specs)
    in_specs, scratch_shapes = list(in_specs), list(scratch_shapes)
    aliases = dict(aliases or {})

    if comm is None:
        def plain(*args):
            return list(pl.pallas_call(body, name=name, grid=grid, in_specs=in_specs, out_specs=out_specs,
                                       out_shape=out_shape, scratch_shapes=scratch_shapes,
                                       input_output_aliases=aliases,
                                       compiler_params=_params(*semantics))(*args))
        return plain

    def hosted(*args):
        n_in, n_out, n_scr = len(args), len(out_shape), len(scratch_shapes)
        c_in, c_out = len(comm.arrays), len(comm.out_shapes)

        def wrapped(*refs):
            ins, cins = refs[:n_in], refs[n_in:n_in + c_in]
            o0 = n_in + c_in
            outs, couts = refs[o0:o0 + n_out], refs[o0 + n_out:o0 + n_out + c_out]
            s0 = o0 + n_out + c_out
            scr, cscr = refs[s0:s0 + n_scr], refs[s0 + n_scr:]
            first = last = None
            for axis, size in enumerate(grid):
                at_start, at_end = pl.program_id(axis) == 0, pl.program_id(axis) == size - 1
                first = at_start if first is None else first & at_start
                last = at_end if last is None else last & at_end
            if first is None:
                comm.start(cins, couts, cscr)
                body(*ins, *outs, *scr)
                comm.finish(cins, couts, cscr)
                return
            pl.when(first)(lambda: comm.start(cins, couts, cscr))
            body(*ins, *outs, *scr)
            pl.when(last)(lambda: comm.finish(cins, couts, cscr))

        res = pl.pallas_call(
            wrapped, name=name, grid=grid, in_specs=in_specs + comm.in_specs, out_specs=out_specs + comm.out_specs,
            out_shape=out_shape + comm.out_shapes, scratch_shapes=scratch_shapes + comm.scratch,
            input_output_aliases={**aliases, **{n_in + i: n_out + o for i, o in comm.aliases.items()}},
            compiler_params=pltpu.CompilerParams(dimension_semantics=("arbitrary",) * len(grid),
                                                 vmem_limit_bytes=VMEM_LIMIT_BYTES, has_side_effects=True),
        )(*args, *comm.arrays)
        return list(res[:n_out]), list(res[n_out:])

    return hosted


def _run_comm(comm, name):
    return _pcall(comm, lambda: None, name=name, grid=(), in_specs=[], out_specs=[], out_shape=[], semantics=())()[1]


def _tile(n, want, align=8):
    if n <= want:
        return n
    for t in range(want - want % align, 0, -align):
        if n % t == 0:
            return t
    raise ValueError((n, want, align))


def _mm(a, b):
    return jnp.dot(a.astype(MXU_DTYPE), b.astype(MXU_DTYPE), preferred_element_type=F32)


def _mm_nt(a, b):
    return lax.dot_general(a.astype(MXU_DTYPE), b.astype(MXU_DTYPE), (((1,), (1,)), ((), ())),
                           preferred_element_type=F32)


def _mm_tn(a, b):
    return lax.dot_general(a.astype(MXU_DTYPE), b.astype(MXU_DTYPE), (((0,), (0,)), ((), ())),
                           preferred_element_type=F32)


def _sigmoid(v):
    return 0.5 * jnp.tanh(0.5 * v) + 0.5


def _gelu(v):
    v2 = v * v
    t = jnp.tanh(v * (GELU_C + (GELU_C * GELU_K) * v2))
    hv = 0.5 * v
    dt = (1.0 - t * t) * (GELU_C + (3.0 * GELU_C * GELU_K) * v2)
    return hv + hv * t, (0.5 + 0.5 * t) + hv * dt


def _softplus_neg(lam):
    e = jnp.exp(-jnp.abs(lam))
    u = 1.0 + e
    log1p_e = jnp.where(u == 1.0, e, jnp.log(u) * e / jnp.where(u == 1.0, 1.0, u - 1.0))
    return jnp.maximum(-lam, 0.0) + log1p_e


def _rms(xv):
    rinv = lax.rsqrt(jnp.mean(xv * xv, axis=-1, keepdims=True) + EPS)
    return rinv, xv * rinv


def _rms_bwd(rinv, xhat, dxhat):
    return rinv * (dxhat - xhat * jnp.mean(dxhat * xhat, axis=-1, keepdims=True))


def _colsum(v):
    return jnp.sum(v, axis=0, keepdims=True)


def _wrow(w_ref, k, wcols):
    return w_ref[pl.ds(k, 1), :] if wcols is None else w_ref[pl.ds(k, 1), wcols]


def _windows(buf_ref, halo, taps, rows):
    assert taps <= 8 <= halo
    x = buf_ref[pl.ds(halo - 8, rows + 8), :]
    return [x[8:] if s == 0 else pltpu.roll(x, s, 0)[8:] for s in range(taps)]


def _causal_from(xs, w_ref, wcols=None):
    taps = len(xs)
    acc = None
    for s in range(taps):
        term = _wrow(w_ref, taps - 1 - s, wcols) * xs[s]
        acc = term if acc is None else acc + term
    return acc


def _tap_grads_from(dw_ref, dy, xs, wcols=None):
    taps = len(xs)
    for s in range(taps):
        g = _colsum(dy * xs[s])
        if wcols is None:
            dw_ref[pl.ds(taps - 1 - s, 1), :] += g
        else:
            dw_ref[pl.ds(taps - 1 - s, 1), wcols] += g


def _causal_taps(buf_ref, halo, w_ref, taps, rows, wcols=None):
    return _causal_from(_windows(buf_ref, halo, taps, rows), w_ref, wcols)


def _anticausal_taps(buf_ref, w_ref, taps, rows, wcols=None):
    assert taps <= 8
    x = buf_ref[pl.ds(0, rows + 8), :]
    acc = None
    for s in range(taps):
        win = x[:rows] if s == 0 else pltpu.roll(x, rows + 8 - s, 0)[:rows]
        term = _wrow(w_ref, taps - 1 - s, wcols) * win
        acc = term if acc is None else acc + term
    return acc


def _shift_copies(dst_ref, buf_ref, rows, up):
    x = buf_ref[pl.ds(0, rows + 8), :]
    for r in range(8):
        if up:
            dst_ref[r] = x[:rows] if r == 0 else pltpu.roll(x, rows + 8 - r, 0)[:rows]
        else:
            dst_ref[r] = x[8:] if r == 0 else pltpu.roll(x, r, 0)[8:]


def _causal_taps8(sh_ref, halo, w_ref, taps, rows):
    acc = None
    for s in range(taps):
        term = _wrow(w_ref, taps - 1 - s, None) * sh_ref[s % 8, pl.ds(halo - 8 - 8 * (s // 8), rows), :]
        acc = term if acc is None else acc + term
    return acc


def _anticausal_taps8(sh_ref, w_ref, taps, rows):
    acc = None
    for s in range(taps):
        term = _wrow(w_ref, taps - 1 - s, None) * sh_ref[s % 8, pl.ds(8 * (s // 8), rows), :]
        acc = term if acc is None else acc + term
    return acc


def _tap_grads8(dw_ref, dy, sh_ref, halo, taps, rows):
    for s in range(taps):
        dw_ref[pl.ds(taps - 1 - s, 1), :] += _colsum(dy * sh_ref[s % 8, pl.ds(halo - 8 - 8 * (s // 8), rows), :])


def _fwd_in(x, g, w_in, comm=None):
    S, D = x.shape
    nb, _, C = w_in.shape
    ts = _tile(S, 1024)

    def body(x_ref, g_ref, w_ref, z_ref, h_ref):
        _, xhat = _rms(x_ref[...])
        h = (xhat * g_ref[...]).astype(MXU_DTYPE)
        h_ref[...] = h
        for j in range(nb):
            z_ref[j] = jnp.dot(h, w_ref[j], preferred_element_type=F32)

    return _pcall(
        comm, body, name="fwd_in", grid=(S // ts,),
        in_specs=[pl.BlockSpec((ts, D), lambda i: (i, 0)), pl.BlockSpec((1, D), lambda i: (0, 0)),
                  pl.BlockSpec((nb, D, C), lambda i: (0, 0, 0))],
        out_specs=[pl.BlockSpec((nb, ts, C), lambda i: (0, i, 0)), pl.BlockSpec((ts, D), lambda i: (i, 0))],
        out_shape=[jax.ShapeDtypeStruct((nb, S, C), F32), jax.ShapeDtypeStruct((S, D), MXU_DTYPE)],
        semantics=("parallel",))(x, g, w_in)


def _lru_gates(xc, wa_ref, ba_ref, wx_ref, bx_ref, sp):
    xb = xc.astype(MXU_DTYPE)
    r = _sigmoid(jnp.dot(xb, wa_ref[...], preferred_element_type=F32) + ba_ref[...])
    ig = _sigmoid(jnp.dot(xb, wx_ref[...], preferred_element_type=F32) + bx_ref[...])
    log_a = -RG_C * r * sp
    a = jnp.exp(log_a)
    one_minus_a2 = jnp.tanh(-log_a) * (a * a + 1.0)
    inv_mult = lax.rsqrt(one_minus_a2)
    mult = jnp.where(one_minus_a2 > 0.0, one_minus_a2 * inv_mult, 0.0)
    return r, ig, a, mult, inv_mult


def _lru_fwd(z, conv_w, conv_b, wa, ba, wx, bx, lam, comm=None):
    _, S, C = z.shape
    ts = _tile(S, 256)
    taps = conv_w.shape[0]
    halo = 8

    def body(zx_ref, zg_ref, cw_ref, cb_ref, wa_ref, ba_ref, wx_ref, bx_ref, lam_ref,
             h_ref, y_ref, xbuf, a_s, u_s, hc):
        i = pl.program_id(0)

        @pl.when(i == 0)
        def _():
            xbuf[pl.ds(0, halo), :] = jnp.zeros((halo, C), F32)
            hc[...] = jnp.zeros_like(hc)

        xbuf[pl.ds(halo, ts), :] = zx_ref[0]
        xc = _causal_taps(xbuf, halo, cw_ref, taps, ts) + cb_ref[...]
        sp = _softplus_neg(lam_ref[...])
        _, ig, a, mult, _ = _lru_gates(xc, wa_ref, ba_ref, wx_ref, bx_ref, sp)
        a_s[...] = a
        u_s[...] = mult * (ig * xc)
        row = lax.broadcasted_iota(jnp.int32, (8, C), 0)

        def step(k, carry):
            off = pl.multiple_of(k * 8, 8)
            av = a_s[pl.ds(off, 8), :]
            uv = u_s[pl.ds(off, 8), :]
            for d in (1, 2, 4):
                m = row >= d
                a_sh = jnp.where(m, pltpu.roll(av, d, 0), 1.0)
                u_sh = jnp.where(m, pltpu.roll(uv, d, 0), 0.0)
                uv = uv + av * u_sh
                av = av * a_sh
            hv = uv + av * carry
            h_ref[pl.ds(off, 8), :] = hv
            return jnp.broadcast_to(hv[7:8, :], (8, C))

        hc[...] = lax.fori_loop(0, ts // 8, step, hc[...], unroll=4)
        ge, _ = _gelu(zg_ref[0])
        y_ref[...] = (h_ref[...] * ge).astype(MXU_DTYPE)
        xbuf[pl.ds(0, halo), :] = xbuf[pl.ds(ts, halo), :]

    vec = pl.BlockSpec((1, C), lambda i: (0, 0))
    mat = pl.BlockSpec((C, C), lambda i: (0, 0))
    return _pcall(
        comm, body, name="lru_fwd", grid=(S // ts,),
        in_specs=[pl.BlockSpec((1, ts, C), lambda i: (0, i, 0)), pl.BlockSpec((1, ts, C), lambda i: (1, i, 0)),
                  pl.BlockSpec((taps, C), lambda i: (0, 0)), vec, mat, vec, mat, vec, vec],
        out_specs=[pl.BlockSpec((ts, C), lambda i: (i, 0)), pl.BlockSpec((ts, C), lambda i: (i, 0))],
        out_shape=[jax.ShapeDtypeStruct((S, C), F32), jax.ShapeDtypeStruct((S, C), MXU_DTYPE)],
        scratch_shapes=[pltpu.VMEM((ts + halo, C), F32), pltpu.VMEM((ts, C), F32), pltpu.VMEM((ts, C), F32),
                        pltpu.VMEM((8, C), F32)],
        semantics=("arbitrary",))(z, z, conv_w, conv_b, wa, ba, wx, bx, lam)


def _layer_norm_stats(c1):
    mu = jnp.mean(c1, axis=-1, keepdims=True)
    xc = c1 - mu
    rstd = lax.rsqrt(jnp.mean(xc * xc, axis=-1, keepdims=True) + EPS)
    return rstd, xc * rstd


def _conf_fwd(z, conv_w, conv_b, ln_g, ln_b, comm=None):
    _, S, C = z.shape
    ts = _tile(S, 256)
    taps = conv_w.shape[0]
    halo = 32

    def body(za_ref, zb_ref, cw_ref, cb_ref, g_ref, b_ref, c1_ref, c3_ref, cbuf, shifted):
        i = pl.program_id(0)

        @pl.when(i == 0)
        def _():
            cbuf[pl.ds(0, halo), :] = jnp.zeros((halo, C), F32)

        cbuf[pl.ds(halo, ts), :] = za_ref[0] * _sigmoid(zb_ref[0])
        _shift_copies(shifted, cbuf, ts + halo - 8, up=False)
        c1 = _causal_taps8(shifted, halo, cw_ref, taps, ts) + cb_ref[...]
        c1_ref[...] = c1
        _, xhat = _layer_norm_stats(c1)
        c2 = xhat * g_ref[...] + b_ref[...]
        c3_ref[...] = (c2 * _sigmoid(c2)).astype(MXU_DTYPE)
        cbuf[pl.ds(0, halo), :] = cbuf[pl.ds(ts, halo), :]

    vec = pl.BlockSpec((1, C), lambda i: (0, 0))
    return _pcall(
        comm, body, name="conf_fwd", grid=(S // ts,),
        in_specs=[pl.BlockSpec((1, ts, C), lambda i: (2, i, 0)), pl.BlockSpec((1, ts, C), lambda i: (3, i, 0)),
                  pl.BlockSpec((taps, C), lambda i: (0, 0)), vec, vec, vec],
        out_specs=[pl.BlockSpec((ts, C), lambda i: (i, 0)), pl.BlockSpec((ts, C), lambda i: (i, 0))],
        out_shape=[jax.ShapeDtypeStruct((S, C), F32), jax.ShapeDtypeStruct((S, C), MXU_DTYPE)],
        scratch_shapes=[pltpu.VMEM((ts + halo, C), F32), pltpu.VMEM((8, ts + halo - 8, C), F32)],
        semantics=("arbitrary",))(z, z, conv_w, conv_b, ln_g, ln_b)


def _fwd_out_q(x, y_lru, c3, w_out, g_xa, w_q, comm=None):
    S, D = x.shape
    C = y_lru.shape[1]
    ts = _tile(S, 1024)

    def body(x_ref, yl_ref, c3_ref, wo_ref, g_ref, wq_ref, x1_ref, h2_ref, q_ref):
        x1 = (x_ref[...] + jnp.dot(yl_ref[...], wo_ref[0], preferred_element_type=F32)
              + jnp.dot(c3_ref[...], wo_ref[1], preferred_element_type=F32))
        x1_ref[...] = x1
        _, xhat = _rms(x1)
        h2 = (xhat * g_ref[...]).astype(MXU_DTYPE)
        h2_ref[...] = h2
        q_ref[...] = jnp.dot(h2, wq_ref[...], preferred_element_type=F32).astype(MXU_DTYPE)

    row = lambda w: pl.BlockSpec((ts, w), lambda i: (i, 0))
    return _pcall(
        comm, body, name="fwd_out_q", grid=(S // ts,),
        in_specs=[row(D), row(C), row(C), pl.BlockSpec((2, C, D), lambda i: (0, 0, 0)),
                  pl.BlockSpec((1, D), lambda i: (0, 0)), pl.BlockSpec((D, D), lambda i: (0, 0))],
        out_specs=[row(D), row(D), row(D)],
        out_shape=[jax.ShapeDtypeStruct((S, D), F32), jax.ShapeDtypeStruct((S, D), MXU_DTYPE),
                   jax.ShapeDtypeStruct((S, D), MXU_DTYPE)],
        semantics=("parallel",))(x, y_lru, c3, w_out, g_xa, w_q)


def _kv_fwd(mem, g, w_kv):
    M, D = mem.shape
    nb, _, C = w_kv.shape

    def body(mem_ref, g_ref, w_ref, m_ref, kv_ref):
        _, xhat = _rms(mem_ref[...])
        m = (xhat * g_ref[...]).astype(MXU_DTYPE)
        m_ref[...] = m
        for j in range(nb):
            kv_ref[:, pl.ds(j * C, C)] = jnp.dot(m, w_ref[j], preferred_element_type=F32).astype(MXU_DTYPE)

    return pl.pallas_call(
        body, name="kv_fwd", grid=(1,),
        in_specs=[pl.BlockSpec((M, D), lambda i: (0, 0)), pl.BlockSpec((1, D), lambda i: (0, 0)),
                  pl.BlockSpec((nb, D, C), lambda i: (0, 0, 0))],
        out_specs=[pl.BlockSpec((M, D), lambda i: (0, 0)), pl.BlockSpec((M, nb * C), lambda i: (0, 0))],
        out_shape=[jax.ShapeDtypeStruct((M, D), MXU_DTYPE), jax.ShapeDtypeStruct((M, nb * C), MXU_DTYPE)],
        compiler_params=_params("arbitrary"))(mem, g, w_kv)


def _softmax_rows(s):
    e = jnp.exp(s - jnp.max(s, axis=-1, keepdims=True))
    return e / jnp.sum(e, axis=-1, keepdims=True)


def _attn_fwd(q, kv, x1, w_o, g_ffn, comm=None):
    S, D = x1.shape
    M = kv.shape[0]
    hd = D // XA_HEADS
    scale = hd ** -0.5
    ts = _tile(S, 1024)

    def body(q_ref, kv_ref, x1_ref, wo_ref, g_ref, o_ref, x2_ref, h3_ref):
        for h in range(XA_HEADS):
            cols = pl.ds(h * hd, hd)
            p = _softmax_rows(_mm_nt(q_ref[:, cols], kv_ref[:, cols]) * scale)
            o_ref[:, cols] = _mm(p, kv_ref[:, pl.ds(D + h * hd, hd)]).astype(MXU_DTYPE)
        x2 = x1_ref[...] + jnp.dot(o_ref[...], wo_ref[...], preferred_element_type=F32)
        x2_ref[...] = x2
        _, xhat = _rms(x2)
        h3_ref[...] = (xhat * g_ref[...]).astype(MXU_DTYPE)

    row = pl.BlockSpec((ts, D), lambda i: (i, 0))
    return _pcall(
        comm, body, name="attn_fwd", grid=(S // ts,),
        in_specs=[row, pl.BlockSpec((M, 2 * D), lambda i: (0, 0)), row, pl.BlockSpec((D, D), lambda i: (0, 0)),
                  pl.BlockSpec((1, D), lambda i: (0, 0))],
        out_specs=[row, row, row],
        out_shape=[jax.ShapeDtypeStruct((S, D), MXU_DTYPE), jax.ShapeDtypeStruct((S, D), F32),
                   jax.ShapeDtypeStruct((S, D), MXU_DTYPE)],
        semantics=("parallel",))(q, kv, x1, w_o, g_ffn)


def _ffn_fwd(h3, w_up, conv_w, conv_b, w_down, x2, g_final, target, comm=None):
    S, D = h3.shape
    nb, _, CW = w_up.shape
    half = nb // 2
    cb = 768
    per = CW // cb
    J = half * per
    ts = _tile(S, 256)
    taps = conv_w.shape[0]
    halo = 8

    def body(h_ref, wup_ref, cw_ref, cb_ref, wd_ref, x2_ref, gf_ref, t_ref,
             gu_ref, act_ref, dx3_ref, loss_ref, dgf_ref, gbuf):
        i = pl.program_id(0)

        @pl.when(i == 0)
        def _():
            for ref in (loss_ref, dgf_ref, gbuf):
                ref[...] = jnp.zeros_like(ref)

        hv = h_ref[...]
        x3 = x2_ref[...]
        for j in range(J):
            b, cols, wcols = j // per, pl.ds((j % per) * cb, cb), pl.ds(j * cb, cb)
            g = jnp.dot(hv, wup_ref[b, :, cols], preferred_element_type=F32)
            u = jnp.dot(hv, wup_ref[half + b, :, cols], preferred_element_type=F32)
            gu_ref[0, b, :, cols] = g
            gu_ref[1, b, :, cols] = u
            gbuf[j, pl.ds(halo, ts), :] = g
            gc = _causal_taps(gbuf.at[j], halo, cw_ref, taps, ts, wcols=wcols) + cb_ref[:, wcols]
            gbuf[j, pl.ds(0, halo), :] = gbuf[j, pl.ds(ts, halo), :]
            ge, _ = _gelu(gc)
            act = (ge * u).astype(MXU_DTYPE)
            act_ref[j] = act
            x3 = x3 + jnp.dot(act, wd_ref[j], preferred_element_type=F32)
        rinv, xhat = _rms(x3)
        gf = gf_ref[...]
        diff = xhat * gf - t_ref[...]
        loss_ref[...] += _colsum(diff * diff) * (0.5 / D)
        dy = diff * (1.0 / D)
        dgf_ref[...] += _colsum(dy * xhat)
        dx3_ref[...] = _rms_bwd(rinv, xhat, dy * gf)

    row = pl.BlockSpec((ts, D), lambda i: (i, 0))
    vecd = pl.BlockSpec((1, D), lambda i: (0, 0))
    once = pl.Buffered(1)
    sds = jax.ShapeDtypeStruct
    res = _pcall(
        comm, body, name="ffn_fwd", grid=(S // ts,),
        in_specs=[row, pl.BlockSpec((nb, D, CW), lambda i: (0, 0, 0), pipeline_mode=once),
                  pl.BlockSpec((taps, half * CW), lambda i: (0, 0)), pl.BlockSpec((1, half * CW), lambda i: (0, 0)),
                  pl.BlockSpec((J, cb, D), lambda i: (0, 0, 0), pipeline_mode=once), row, vecd, row],
        out_specs=[pl.BlockSpec((2, half, ts, CW), lambda i: (0, 0, i, 0)),
                   pl.BlockSpec((J, ts, cb), lambda i: (0, i, 0)), row, vecd, vecd],
        out_shape=[sds((2, half, S, CW), F32), sds((J, S, cb), MXU_DTYPE), sds((S, D), F32),
                   sds((1, D), F32), sds((1, D), F32)],
        scratch_shapes=[pltpu.VMEM((J, ts + halo, cb), F32)],
        semantics=("arbitrary",))(h3, w_up, conv_w, conv_b, w_down.reshape(J, cb, D), x2, g_final, target)
    outs = res if comm is None else res[0]
    outs = [outs[0].reshape(nb, S, CW)] + list(outs[1:])
    return outs if comm is None else (outs, res[1])


def _ffn_bwd(dx3, w_down, w_up, gu, x2, g_ffn, conv_w, conv_b, comm=None):
    nb, S, CW = gu.shape
    half = nb // 2
    D = dx3.shape[1]
    cb = 768
    per = CW // cb
    J = half * per
    ts = _tile(S, 256)
    n = S // ts
    taps = conv_w.shape[0]
    halo = 8
    hb = ts // halo

    def body(dx_ref, x2_ref, gf_ref, wd_ref, wup_ref, gu_ref, gh_ref, cw_ref, cb_ref,
             dgu_ref, dx2_ref, dgf_ref, dcw_ref, dcb_ref, gbuf, dbuf):
        i = pl.program_id(0)
        r = n - 1 - i

        @pl.when(i == 0)
        def _():
            for ref in (dgf_ref, dcw_ref, dcb_ref, dbuf):
                ref[...] = jnp.zeros_like(ref)

        dx3v = dx_ref[...]
        dxb = dx3v.astype(MXU_DTYPE)
        dacts = [_mm_nt(dxb, wd_ref[j]) for j in range(J)]
        dh = None
        for j in range(J):
            b, cols, wcols = j // per, pl.ds((j % per) * cb, cb), pl.ds(j * cb, cb)
            dact = dacts[j]
            gbuf[pl.ds(0, halo), :] = jnp.where(r > 0, gh_ref[0, b, :, cols], 0.0)
            gbuf[pl.ds(halo, ts), :] = gu_ref[0, b, :, cols]
            gs = _windows(gbuf, halo, taps, ts)
            gc = _causal_from(gs, cw_ref, wcols) + cb_ref[:, wcols]
            ge, dge = _gelu(gc)
            dub = (dact * ge).astype(MXU_DTYPE)
            dgc = dact * gu_ref[1, b, :, cols] * dge
            dcb_ref[:, wcols] += _colsum(dgc)
            dbuf[j, pl.ds(0, ts), :] = dgc
            _tap_grads_from(dcw_ref, dgc, gs, wcols)
            dgb = _anticausal_taps(dbuf.at[j], cw_ref, taps, ts, wcols=wcols).astype(MXU_DTYPE)
            dbuf[j, pl.ds(ts, halo), :] = dbuf[j, pl.ds(0, halo), :]
            dgu_ref[0, b, :, cols] = dgb
            dgu_ref[1, b, :, cols] = dub
            part = _mm_nt(dgb, wup_ref[b, :, cols]) + _mm_nt(dub, wup_ref[half + b, :, cols])
            dh = part if dh is None else dh + part
        rinv, xhat = _rms(x2_ref[...])
        dgf_ref[...] += _colsum(dh * xhat)
        dx2_ref[...] = dx3v + _rms_bwd(rinv, xhat, dh * gf_ref[...])

    gu2 = gu.reshape(2, half, S, CW)
    row = pl.BlockSpec((ts, D), lambda i: (n - 1 - i, 0))
    vecd = pl.BlockSpec((1, D), lambda i: (0, 0))
    pair = pl.BlockSpec((2, half, ts, CW), lambda i: (0, 0, n - 1 - i, 0))
    g_prev = pl.BlockSpec((1, half, halo, CW), lambda i: (0, 0, jnp.maximum((n - 1 - i) * hb - 1, 0), 0))
    tapw = pl.BlockSpec((taps, half * CW), lambda i: (0, 0))
    vec = pl.BlockSpec((1, half * CW), lambda i: (0, 0))
    once = pl.Buffered(1)
    sds = jax.ShapeDtypeStruct
    res = _pcall(
        comm, body, name="ffn_bwd", grid=(n,),
        in_specs=[row, row, vecd, pl.BlockSpec((J, cb, D), lambda i: (0, 0, 0), pipeline_mode=once),
                  pl.BlockSpec((nb, D, CW), lambda i: (0, 0, 0), pipeline_mode=once), pair, g_prev, tapw, vec],
        out_specs=[pair, row, vecd, tapw, vec],
        out_shape=[sds((2, half, S, CW), MXU_DTYPE), sds((S, D), F32), sds((1, D), F32),
                   sds((taps, half * CW), F32), sds((1, half * CW), F32)],
        scratch_shapes=[pltpu.VMEM((ts + halo, cb), F32), pltpu.VMEM((J, ts + halo, cb), F32)],
        semantics=("arbitrary",))(dx3, x2, g_ffn, w_down.reshape(J, cb, D), w_up, gu2, gu2, conv_w, conv_b)
    outs = res if comm is None else res[0]
    outs = [outs[0].reshape(nb, S, CW)] + list(outs[1:])
    return outs if comm is None else (outs, res[1])


def _attn_bwd(dx2, w_o, q, kv, x1, g_xa, w_q, comm=None):
    S, D = x1.shape
    M = kv.shape[0]
    hd = D // XA_HEADS
    scale = hd ** -0.5
    ts = _tile(S, 1024)

    def body(dx2_ref, wo_ref, q_ref, kv_ref, x1_ref, g_ref, wq_ref, dq_ref, dx1_ref, dkv_ref, dg_ref):
        i = pl.program_id(0)

        @pl.when(i == 0)
        def _():
            dkv_ref[...] = jnp.zeros_like(dkv_ref)
            dg_ref[...] = jnp.zeros_like(dg_ref)

        dx2 = dx2_ref[...]
        do = _mm_nt(dx2, wo_ref[...]).astype(MXU_DTYPE)
        for h in range(XA_HEADS):
            cols = pl.ds(h * hd, hd)
            vcols = pl.ds(D + h * hd, hd)
            qh, kh, doh = q_ref[:, cols], kv_ref[:, cols], do[:, h * hd:(h + 1) * hd]
            p = _softmax_rows(_mm_nt(qh, kh) * scale)
            dp = _mm_nt(doh, kv_ref[:, vcols])
            dkv_ref[:, vcols] += _mm_tn(p, doh)
            ds = (p * (dp - jnp.sum(dp * p, axis=-1, keepdims=True)) * scale).astype(MXU_DTYPE)
            dq_ref[:, cols] = _mm(ds, kh).astype(MXU_DTYPE)
            dkv_ref[:, cols] += _mm_tn(ds, qh)
        dh2 = _mm_nt(dq_ref[...], wq_ref[...])
        rinv, xhat = _rms(x1_ref[...])
        dg_ref[...] += _colsum(dh2 * xhat)
        dx1_ref[...] = dx2 + _rms_bwd(rinv, xhat, dh2 * g_ref[...])

    row = pl.BlockSpec((ts, D), lambda i: (i, 0))
    mat = pl.BlockSpec((D, D), lambda i: (0, 0))
    vecd = pl.BlockSpec((1, D), lambda i: (0, 0))
    kvs = pl.BlockSpec((M, 2 * D), lambda i: (0, 0))
    return _pcall(
        comm, body, name="attn_bwd", grid=(S // ts,),
        in_specs=[row, mat, row, kvs, row, vecd, mat],
        out_specs=[row, row, kvs, vecd],
        out_shape=[jax.ShapeDtypeStruct((S, D), MXU_DTYPE), jax.ShapeDtypeStruct((S, D), F32),
                   jax.ShapeDtypeStruct((M, 2 * D), F32), jax.ShapeDtypeStruct((1, D), F32)],
        semantics=("arbitrary",))(dx2, w_o, q, kv, x1, g_xa, w_q)


def _kv_bwd(dkv, w_kv, mem, g, m):
    M, D = mem.shape
    nb, _, C = w_kv.shape

    def body(dkv_ref, w_ref, mem_ref, m_ref, dw_ref, dg_ref):
        dm = jnp.zeros((M, D), F32)
        for j in range(nb):
            dj = dkv_ref[:, pl.ds(j * C, C)].astype(MXU_DTYPE)
            dw_ref[j] = _mm_tn(m_ref[...], dj).astype(dw_ref.dtype)
            dm = dm + _mm_nt(dj, w_ref[j])
        _, xhat = _rms(mem_ref[...])
        dg_ref[...] = _colsum(dm * xhat)

    full = lambda *s: pl.BlockSpec(s, lambda i: (0,) * len(s))
    return pl.pallas_call(
        body, name="kv_bwd", grid=(1,),
        in_specs=[full(M, nb * C), full(nb, D, C), full(M, D), full(M, D)],
        out_specs=[full(nb, D, C), full(1, D)],
        out_shape=[jax.ShapeDtypeStruct((nb, D, C), WIRE_DTYPE), jax.ShapeDtypeStruct((1, D), F32)],
        compiler_params=_params("arbitrary"))(dkv, w_kv, mem, m)


def _conf_bwd(dx1, w_out_c, z, c1, conv_w, ln_g, ln_b, comm=None):
    _, S, C = z.shape
    D = dx1.shape[1]
    ts = _tile(S, 256)
    n = S // ts
    taps = conv_w.shape[0]
    halo = 32
    hb = ts // halo

    def body(dx_ref, wo_ref, za_ref, zb_ref, zah_ref, zbh_ref, c1_ref, cw_ref, g_ref, b_ref,
             dz_ref, dcw_ref, dcb_ref, dlg_ref, dlb_ref, c0buf, dbuf, shifted):
        i = pl.program_id(0)
        r = n - 1 - i

        @pl.when(i == 0)
        def _():
            for ref in (dcw_ref, dcb_ref, dlg_ref, dlb_ref):
                ref[...] = jnp.zeros_like(ref)
            dbuf[pl.ds(ts, halo), :] = jnp.zeros((halo, C), F32)

        za = za_ref[0]
        sb = _sigmoid(zb_ref[0])
        c0buf[pl.ds(0, halo), :] = jnp.where(r > 0, zah_ref[0] * _sigmoid(zbh_ref[0]), 0.0)
        c0buf[pl.ds(halo, ts), :] = za * sb
        dc3 = _mm_nt(dx_ref[...], wo_ref[...])
        rstd, xhat = _layer_norm_stats(c1_ref[...])
        g = g_ref[...]
        c2 = xhat * g + b_ref[...]
        sg = _sigmoid(c2)
        dc2 = dc3 * sg * (1.0 + c2 * (1.0 - sg))
        dlg_ref[...] += _colsum(dc2 * xhat)
        dlb_ref[...] += _colsum(dc2)
        dxh = dc2 * g
        dc1 = rstd * (dxh - jnp.mean(dxh, axis=-1, keepdims=True)
                      - xhat * jnp.mean(dxh * xhat, axis=-1, keepdims=True))
        dcb_ref[...] += _colsum(dc1)
        dbuf[pl.ds(0, ts), :] = dc1
        _shift_copies(shifted, c0buf, ts + halo - 8, up=False)
        _tap_grads8(dcw_ref, dc1, shifted, halo, taps, ts)
        _shift_copies(shifted, dbuf, ts + halo - 8, up=True)
        dc0 = _anticausal_taps8(shifted, cw_ref, taps, ts)
        dz_ref[0] = (dc0 * sb).astype(MXU_DTYPE)
        dz_ref[1] = (dc0 * za * sb * (1.0 - sb)).astype(MXU_DTYPE)
        dbuf[pl.ds(ts, halo), :] = dbuf[pl.ds(0, halo), :]

    vec = pl.BlockSpec((1, C), lambda i: (0, 0))
    tapw = pl.BlockSpec((taps, C), lambda i: (0, 0))
    tile = lambda b: pl.BlockSpec((1, ts, C), lambda i: (b, n - 1 - i, 0))
    prev = lambda b: pl.BlockSpec((1, halo, C), lambda i: (b, jnp.maximum((n - 1 - i) * hb - 1, 0), 0))
    return _pcall(
        comm, body, name="conf_bwd", grid=(n,),
        in_specs=[pl.BlockSpec((ts, D), lambda i: (n - 1 - i, 0)), pl.BlockSpec((C, D), lambda i: (0, 0)),
                  tile(2), tile(3), prev(2), prev(3), pl.BlockSpec((ts, C), lambda i: (n - 1 - i, 0)),
                  tapw, vec, vec],
        out_specs=[pl.BlockSpec((2, ts, C), lambda i: (1, n - 1 - i, 0)), tapw, vec, vec, vec],
        out_shape=[jax.ShapeDtypeStruct((4, S, C), MXU_DTYPE), jax.ShapeDtypeStruct((taps, C), F32),
                   jax.ShapeDtypeStruct((1, C), F32), jax.ShapeDtypeStruct((1, C), F32),
                   jax.ShapeDtypeStruct((1, C), F32)],
        scratch_shapes=[pltpu.VMEM((ts + halo, C), F32), pltpu.VMEM((ts + halo, C), F32),
                        pltpu.VMEM((8, ts + halo - 8, C), F32)],
        semantics=("arbitrary",))(dx1, w_out_c, z, z, z, z, c1, conv_w, ln_g, ln_b)


def _lru_bwd(dx1, w_out_l, z, h, conv_w, conv_b, wa, ba, wx, bx, lam, dz, comm=None):
    _, S, C = z.shape
    D = dx1.shape[1]
    ts = _tile(S, 256)
    n = S // ts
    taps = conv_w.shape[0]
    halo = 8
    hb = ts // halo

    def body(dx_ref, wo_ref, zx_ref, zxh_ref, zg_ref, h_ref, hh_ref, cw_ref, cb_ref, wa_ref, ba_ref,
             wx_ref, bx_ref, lam_ref, dz_in,
             dz_ref, dwa_ref, dwx_ref, dba_ref, dbx_ref, dlam_ref, dcw_ref, dcb_ref,
             xbuf, hbuf, a_s, w_s, dh_s, g_s, dbuf, pc):
        i = pl.program_id(0)
        r = n - 1 - i

        @pl.when(i == 0)
        def _():
            for ref in (dwa_ref, dwx_ref, dba_ref, dbx_ref, dlam_ref, dcw_ref, dcb_ref, pc):
                ref[...] = jnp.zeros_like(ref)
            dbuf[pl.ds(ts, halo), :] = jnp.zeros((halo, C), F32)

        xbuf[pl.ds(0, halo), :] = jnp.where(r > 0, zxh_ref[0], 0.0)
        xbuf[pl.ds(halo, ts), :] = zx_ref[0]
        hbuf[pl.ds(0, halo), :] = jnp.where(r > 0, hh_ref[...], 0.0)
        hbuf[pl.ds(halo, ts), :] = h_ref[...]
        xs = _windows(xbuf, halo, taps, ts)
        xc = _causal_from(xs, cw_ref) + cb_ref[...]
        lam_v = lam_ref[...]
        sp = _softplus_neg(lam_v)
        rg, ig, a, mult, inv_mult = _lru_gates(xc, wa_ref, ba_ref, wx_ref, bx_ref, sp)

        dy = _mm_nt(dx_ref[...], wo_ref[...])
        ge, dge = _gelu(zg_ref[0])
        dh = dy * ge
        dz_ref[1] = (dy * h_ref[...] * dge).astype(MXU_DTYPE)
        a_s[...] = a
        w_s[...] = a * dh
        dh_s[...] = dh
        row = lax.broadcasted_iota(jnp.int32, (8, C), 0)

        def step(kk, carry):
            off = pl.multiple_of((ts // 8 - 1 - kk) * 8, 8)
            av = a_s[pl.ds(off, 8), :]
            wv = w_s[pl.ds(off, 8), :]
            for d in (1, 2, 4):
                m = row < 8 - d
                a_sh = jnp.where(m, pltpu.roll(av, 8 - d, 0), 1.0)
                w_sh = jnp.where(m, pltpu.roll(wv, 8 - d, 0), 0.0)
                wv = wv + av * w_sh
                av = av * a_sh
            pv = wv + av * carry
            g_s[pl.ds(off, 8), :] = dh_s[pl.ds(off, 8), :] + jnp.where(row < 7, pltpu.roll(pv, 7, 0), carry)
            return jnp.broadcast_to(pv[0:1, :], (8, C))

        pc[...] = lax.fori_loop(0, ts // 8, step, pc[...], unroll=4)
        gt = g_s[...]
        da = gt * hbuf[pl.ds(halo - 1, ts), :]
        gm = gt * mult
        dlog_a = da * a - (gt * ig * xc) * (a * a) * inv_mult
        dlam_ref[...] += _colsum(dlog_a * rg) * (RG_C / (1.0 + jnp.exp(lam_v)))
        dpa = (dlog_a * (-RG_C * sp)) * rg * (1.0 - rg)
        dpx = (gm * xc) * ig * (1.0 - ig)
        dba_ref[...] += _colsum(dpa)
        dbx_ref[...] += _colsum(dpx)
        xb = xc.astype(MXU_DTYPE)
        dpab, dpxb = dpa.astype(MXU_DTYPE), dpx.astype(MXU_DTYPE)
        dwa_ref[...] += _mm_tn(xb, dpab)
        dwx_ref[...] += _mm_tn(xb, dpxb)
        dxc = gm * ig + _mm_nt(dpab, wa_ref[...]) + _mm_nt(dpxb, wx_ref[...])
        dcb_ref[...] += _colsum(dxc)
        dbuf[pl.ds(0, ts), :] = dxc
        _tap_grads_from(dcw_ref, dxc, xs)
        dz_ref[0] = _anticausal_taps(dbuf, cw_ref, taps, ts).astype(MXU_DTYPE)
        dbuf[pl.ds(ts, halo), :] = dbuf[pl.ds(0, halo), :]

    vec = pl.BlockSpec((1, C), lambda i: (0, 0))
    mat = pl.BlockSpec((C, C), lambda i: (0, 0))
    tapw = pl.BlockSpec((taps, C), lambda i: (0, 0))
    prev_rows = lambda i: jnp.maximum((n - 1 - i) * hb - 1, 0)
    sds = jax.ShapeDtypeStruct
    return _pcall(
        comm, body, name="lru_bwd", grid=(n,),
        in_specs=[pl.BlockSpec((ts, D), lambda i: (n - 1 - i, 0)), pl.BlockSpec((C, D), lambda i: (0, 0)),
                  pl.BlockSpec((1, ts, C), lambda i: (0, n - 1 - i, 0)),
                  pl.BlockSpec((1, halo, C), lambda i: (0, prev_rows(i), 0)),
                  pl.BlockSpec((1, ts, C), lambda i: (1, n - 1 - i, 0)),
                  pl.BlockSpec((ts, C), lambda i: (n - 1 - i, 0)),
                  pl.BlockSpec((halo, C), lambda i: (prev_rows(i), 0)),
                  tapw, vec, mat, vec, mat, vec, vec, ANY],
        out_specs=[pl.BlockSpec((2, ts, C), lambda i: (0, n - 1 - i, 0)), mat, mat, vec, vec, vec, tapw, vec],
        out_shape=[sds(dz.shape, MXU_DTYPE), sds((C, C), F32), sds((C, C), F32), sds((1, C), F32),
                   sds((1, C), F32), sds((1, C), F32), sds((taps, C), F32), sds((1, C), F32)],
        scratch_shapes=[pltpu.VMEM((ts + halo, C), F32), pltpu.VMEM((ts + halo, C), F32)]
        + [pltpu.VMEM((ts, C), F32)] * 4 + [pltpu.VMEM((ts + halo, C), F32), pltpu.VMEM((8, C), F32)],
        aliases={14: 0},
        semantics=("arbitrary",))(dx1, w_out_l, z, z, z, h, h, conv_w, conv_b, wa, ba, wx, bx, lam, dz)


def _bwd_in(dz, w_in, x, g, dx1):
    S, D = x.shape
    nb, _, C = w_in.shape
    ts = _tile(S, 512)

    def body(dz_ref, w_ref, x_ref, g_ref, dx1_ref, dx_ref, dg_ref):
        i = pl.program_id(0)

        @pl.when(i == 0)
        def _():
            dg_ref[...] = jnp.zeros_like(dg_ref)

        dh = _mm_nt(dz_ref[0], w_ref[0])
        for j in range(1, nb):
            dh = dh + _mm_nt(dz_ref[j], w_ref[j])
        rinv, xhat = _rms(x_ref[...])
        dg_ref[...] += _colsum(dh * xhat)
        dx_ref[...] = dx1_ref[...] + _rms_bwd(rinv, xhat, dh * g_ref[...])

    row = pl.BlockSpec((ts, D), lambda i: (i, 0))
    vecd = pl.BlockSpec((1, D), lambda i: (0, 0))
    return pl.pallas_call(
        body, name="bwd_in", grid=(S // ts,),
        in_specs=[pl.BlockSpec((nb, ts, C), lambda i: (0, i, 0)), pl.BlockSpec((nb, D, C), lambda i: (0, 0, 0)),
                  row, vecd, row],
        out_specs=[row, vecd],
        out_shape=[jax.ShapeDtypeStruct((S, D), F32), jax.ShapeDtypeStruct((1, D), F32)],
        compiler_params=_params("arbitrary"))(dz, w_in, x, g, dx1)


def _wgrad(a, b, name, comm=None):
    na, S, K = a.shape
    nb, _, N = b.shape
    nj = max(na, nb)
    assert min(na, nb) == 1
    ts = _tile(S, 1024)
    ns = S // ts
    grp = max(g for g in range(1, nj + 1) if nj % g == 0 and g * K * N * 4 <= WGRAD_ACC_BYTES)
    ga, gb = (grp if na > 1 else 1), (grp if nb > 1 else 1)

    def body(a_ref, b_ref, o_ref, acc):
        s = pl.program_id(1)

        @pl.when(s == 0)
        def _():
            acc[...] = jnp.zeros_like(acc)

        for k in range(grp):
            acc[k] += _mm_tn(a_ref[k if na > 1 else 0], b_ref[k if nb > 1 else 0])

        @pl.when(s == ns - 1)
        def _():
            o_ref[...] = acc[...].astype(o_ref.dtype)

    res = _pcall(
        comm, body, name=name, grid=(nj // grp, ns),
        in_specs=[pl.BlockSpec((ga, ts, K), (lambda j, s: (j, s, 0)) if na > 1 else (lambda j, s: (0, s, 0))),
                  pl.BlockSpec((gb, ts, N), (lambda j, s: (j, s, 0)) if nb > 1 else (lambda j, s: (0, s, 0)))],
        out_specs=pl.BlockSpec((grp, K, N), lambda j, s: (j, 0, 0)),
        out_shape=jax.ShapeDtypeStruct((nj, K, N), WIRE_DTYPE),
        scratch_shapes=[pltpu.VMEM((grp, K, N), F32)],
        semantics=("parallel", "arbitrary"))(a, b)
    return res[0] if comm is None else (res[0][0], res[1])


def _place():
    x, y, c = lax.axis_index("x"), lax.axis_index("y"), lax.axis_index("c")
    other_chips = [(1 - x, y), (x, 1 - y), (1 - x, 1 - y)]
    return x, y, c, other_chips


def _gather_weights(shards):
    nt = len(shards)

    def body(*refs):
        src, dst = refs[:nt], refs[nt:2 * nt]
        ici_send, ici_recv, d2d_send, d2d_recv, own_send, own_recv = refs[2 * nt:]
        x, y, c, chips = _place()
        mine = 2 * x + y

        def half(t, pc):
            hr = src[t].shape[0] // 2
            return pl.ds(pc * hr, hr)

        def own(t):
            return pltpu.make_async_remote_copy(
                src_ref=src[t], dst_ref=dst[t].at[mine], send_sem=own_send.at[t], recv_sem=own_recv.at[t],
                device_id=(x, y, 1 - c), device_id_type=MESH)

        def ici(t, k, block, to):
            cx, cy = block
            ref = dst[t].at[2 * cx + cy, half(t, c)]
            return pltpu.make_async_remote_copy(
                src_ref=src[t].at[half(t, c)] if to is not None else ref, dst_ref=ref,
                send_sem=ici_send.at[t, k], recv_sem=ici_recv.at[t, k],
                device_id=(*to, c) if to is not None else (x, y, c), device_id_type=MESH)

        def d2d(t, k, block, pc):
            cx, cy = block
            ref = dst[t].at[2 * cx + cy, half(t, pc)]
            return pltpu.make_async_remote_copy(
                src_ref=ref, dst_ref=ref, send_sem=d2d_send.at[t, k], recv_sem=d2d_recv.at[t, k],
                device_id=(x, y, 1 - c), device_id_type=MESH)

        sends = [ici(t, k, (x, y), chip) for t in range(nt) for k, chip in enumerate(chips)]
        sends += [own(t) for t in range(nt)]
        for cp in sends:
            cp.start()
        passed = []
        for t in range(nt):
            for k, chip in enumerate(chips):
                ici(t, k, chip, None).wait_recv()
                fw = d2d(t, k, chip, c)
                fw.start()
                passed.append(fw)
        for t in range(nt):
            own(t).wait_recv()
            for k, chip in enumerate(chips):
                d2d(t, k, chip, 1 - c).wait_recv()
        for cp in sends + passed:
            cp.wait_send()

    return pl.pallas_call(
        body, name="gather_weights",
        in_specs=[ANY] * nt, out_specs=[ANY] * nt,
        out_shape=[jax.ShapeDtypeStruct((N_CHIPS,) + s.shape, s.dtype) for s in shards],
        scratch_shapes=[pltpu.SemaphoreType.DMA((nt, 3))] * 4 + [pltpu.SemaphoreType.DMA((nt,))] * 2,
        compiler_params=pltpu.CompilerParams(has_side_effects=True))(*shards)


def _gather_over_ici(shards):
    nt = len(shards)

    def copies(src, dst, scr, arriving):
        ici_send, ici_recv, own_send, own_recv = scr
        x, y, c, chips = _place()
        out = []
        for t in range(nt):
            hr = src[t].shape[0] // 2
            rows = pl.ds(c * hr, hr)
            for k, (cx, cy) in enumerate(chips):
                block = 2 * cx + cy if arriving else 2 * x + y
                out.append(pltpu.make_async_remote_copy(
                    src_ref=src[t].at[rows], dst_ref=dst[t].at[block, rows],
                    send_sem=ici_send.at[t, k], recv_sem=ici_recv.at[t, k],
                    device_id=(cx, cy, c), device_id_type=MESH))
            out.append(pltpu.make_async_remote_copy(
                src_ref=src[t], dst_ref=dst[t].at[2 * x + y], send_sem=own_send.at[t], recv_sem=own_recv.at[t],
                device_id=(x, y, 1 - c), device_id_type=MESH))
        return out

    def start(src, dst, scr):
        for cp in copies(src, dst, scr, False):
            cp.start()

    def finish(src, dst, scr):
        for cp in copies(src, dst, scr, True):
            cp.wait_recv()
        for cp in copies(src, dst, scr, False):
            cp.wait_send()

    return _Comm(shards, [jax.ShapeDtypeStruct((N_CHIPS,) + s.shape, s.dtype) for s in shards],
                 [pltpu.SemaphoreType.DMA((nt, 3))] * 2 + [pltpu.SemaphoreType.DMA((nt,))] * 2, start, finish)


def _gather_pass_on(bufs):
    nt = len(bufs)

    def passed(dst, scr, t, k, block, pc):
        send, recv = scr
        x, y, c, _ = _place()
        cx, cy = block
        hr = dst[t].shape[1] // 2
        ref = dst[t].at[2 * cx + cy, pl.ds(pc * hr, hr)]
        return pltpu.make_async_remote_copy(src_ref=ref, dst_ref=ref, send_sem=send.at[t, k], recv_sem=recv.at[t, k],
                                            device_id=(x, y, 1 - c), device_id_type=MESH)

    def start(src, dst, scr):
        _, _, c, chips = _place()
        for t in range(nt):
            for k, chip in enumerate(chips):
                passed(dst, scr, t, k, chip, c).start()

    def finish(src, dst, scr):
        _, _, c, chips = _place()
        for t in range(nt):
            for k, chip in enumerate(chips):
                passed(dst, scr, t, k, chip, 1 - c).wait_recv()
        for t in range(nt):
            for k, chip in enumerate(chips):
                passed(dst, scr, t, k, chip, c).wait_send()

    return _Comm(bufs, [jax.ShapeDtypeStruct(b.shape, b.dtype) for b in bufs],
                 [pltpu.SemaphoreType.DMA((nt, 3))] * 2, start, finish, aliases={t: t for t in range(nt)})


def _exchange_halves(grads):
    nt = len(grads)

    def copies(src, dst, scr):
        send, recv = scr
        x, y, c, _ = _place()
        out = []
        for t in range(nt):
            hr = src[t].shape[1] // 2
            out.append(pltpu.make_async_remote_copy(
                src_ref=src[t].at[:, pl.ds((1 - c) * hr, hr)], dst_ref=dst[t],
                send_sem=send.at[t], recv_sem=recv.at[t], device_id=(x, y, 1 - c), device_id_type=MESH))
        return out

    def start(src, dst, scr):
        for cp in copies(src, dst, scr):
            cp.start()

    def finish(src, dst, scr):
        for cp in copies(src, dst, scr):
            cp.wait()

    return _Comm(grads, [jax.ShapeDtypeStruct((g.shape[0], g.shape[1] // 2, g.shape[2]), g.dtype) for g in grads],
                 [pltpu.SemaphoreType.DMA((nt,))] * 2, start, finish)


def _add_halves(grad, other, name):
    nb, R, C = grad.shape
    hr = R // 2
    tr = _tile(hr, 256, 16)
    steps = hr // tr
    c = lax.axis_index("c").astype(jnp.int32).reshape((1,))

    def body(c_ref, a_ref, b_ref, o_ref):
        o_ref[...] = (a_ref[...].astype(F32) + b_ref[...].astype(F32)).astype(o_ref.dtype)

    return pl.pallas_call(
        body, name=name,
        grid_spec=pltpu.PrefetchScalarGridSpec(
            num_scalar_prefetch=1, grid=(nb, steps),
            in_specs=[pl.BlockSpec((1, tr, C), lambda j, i, c_ref: (j, c_ref[0] * steps + i, 0)),
                      pl.BlockSpec((1, tr, C), lambda j, i, c_ref: (j, i, 0))],
            out_specs=pl.BlockSpec((1, tr, C), lambda j, i, c_ref: (j, i, 0))),
        out_shape=jax.ShapeDtypeStruct((nb, hr, C), grad.dtype),
        compiler_params=_params("parallel", "parallel"))(c, grad, other)


def _scatter_chip_sums(parts):
    nt = len(parts)

    def copies(src, dst, scr):
        send, recv = scr
        x, y, c, chips = _place()
        out = []
        for t in range(nt):
            for k, (cx, cy) in enumerate(chips):
                out.append(pltpu.make_async_remote_copy(
                    src_ref=src[t].at[2 * cx + cy], dst_ref=dst[t].at[k],
                    send_sem=send.at[t, k], recv_sem=recv.at[t, k], device_id=(cx, cy, c), device_id_type=MESH))
        return out

    def start(src, dst, scr):
        for cp in copies(src, dst, scr):
            cp.start()

    def finish(src, dst, scr):
        for cp in copies(src, dst, scr):
            cp.wait()

    return _Comm(parts, [jax.ShapeDtypeStruct((3,) + p.shape[1:], p.dtype) for p in parts],
                 [pltpu.SemaphoreType.DMA((nt, 3))] * 2, start, finish)


def _sum_chips(part, recv, name):
    _, hr, C = part.shape
    tr = _tile(hr, 256, 16)
    steps = hr // tr
    where = jnp.stack([2 * lax.axis_index("x") + lax.axis_index("y"), lax.axis_index("c")]).astype(jnp.int32)

    def body(w_ref, a_ref, b_ref, o_ref):
        acc = a_ref[0].astype(F32)
        for k in range(3):
            acc = acc + b_ref[k].astype(F32)
        o_ref[...] = acc

    return pl.pallas_call(
        body, name=name,
        grid_spec=pltpu.PrefetchScalarGridSpec(
            num_scalar_prefetch=1, grid=(steps,),
            in_specs=[pl.BlockSpec((1, tr, C), lambda i, w_ref: (w_ref[0], i, 0)),
                      pl.BlockSpec((3, tr, C), lambda i, w_ref: (0, i, 0))],
            out_specs=pl.BlockSpec((tr, C), lambda i, w_ref: (w_ref[1] * steps + i, 0))),
        out_shape=jax.ShapeDtypeStruct((2 * hr, C), F32),
        compiler_params=_params("parallel"))(where, part, recv)


def _join_halves(bufs):
    nt = len(bufs)

    def swap(dst, scr, t, pc):
        send, recv = scr
        x, y, c, _ = _place()
        hr = dst[t].shape[0] // 2
        rows = dst[t].at[pl.ds(pc * hr, hr)]
        return pltpu.make_async_remote_copy(src_ref=rows, dst_ref=rows, send_sem=send.at[t], recv_sem=recv.at[t],
                                            device_id=(x, y, 1 - c), device_id_type=MESH)

    def start(src, dst, scr):
        c = lax.axis_index("c")
        for t in range(nt):
            swap(dst, scr, t, c).start()

    def finish(src, dst, scr):
        c = lax.axis_index("c")
        for t in range(nt):
            swap(dst, scr, t, 1 - c).wait_recv()
        for t in range(nt):
            swap(dst, scr, t, c).wait_send()

    return _Comm(bufs, [jax.ShapeDtypeStruct(b.shape, b.dtype) for b in bufs],
                 [pltpu.SemaphoreType.DMA((nt,))] * 2, start, finish, aliases={t: t for t in range(nt)})


def _all_reduce_rows(buf, loss_row=None):
    R, L = buf.shape

    def copies(in_ref, gath, send, recv):
        x, y, c, _ = _place()
        out = []
        for k in range(1, N_DEV):
            peer = (x ^ ((k >> 2) & 1), y ^ ((k >> 1) & 1), c ^ (k & 1))
            out.append(pltpu.make_async_remote_copy(
                src_ref=in_ref, dst_ref=gath.at[k], send_sem=send.at[k - 1], recv_sem=recv.at[k - 1],
                device_id=peer, device_id_type=MESH))
        return out

    def start(ins, outs, scr):
        gath, send, recv = scr
        gath[0] = ins[0][...]
        for cp in copies(ins[0], gath, send, recv):
            cp.start()

    def finish(ins, outs, scr):
        gath, send, recv = scr
        for cp in copies(ins[0], gath, send, recv):
            cp.wait()
        x, y, c, _ = _place()
        me = 4 * x + 2 * y + c
        total = gath[me]
        for d in range(1, N_DEV):
            total = total + gath[d ^ me]
        outs[0][...] = total
        if loss_row is not None:
            outs[1][...] = jnp.sum(total[loss_row:loss_row + 1, :], axis=1, keepdims=True)

    out_shape = [jax.ShapeDtypeStruct((R, L), F32)]
    if loss_row is not None:
        out_shape.append(jax.ShapeDtypeStruct((1, 1), F32))
    return _Comm([buf], out_shape,
                 [pltpu.VMEM((N_DEV, R, L), F32), pltpu.SemaphoreType.DMA((N_DEV - 1,)),
                  pltpu.SemaphoreType.DMA((N_DEV - 1,))],
                 start, finish, in_specs=[WHOLE_VMEM], out_specs=[WHOLE_VMEM] * len(out_shape))


def _adamw(w, g, m, v, name):
    R, C = w.shape
    tr = _tile(R, 256)
    c1 = 1.0 - ADAM_B1 ** ADAM_STEP
    c2 = 1.0 - ADAM_B2 ** ADAM_STEP

    def body(w_ref, g_ref, m_ref, v_ref, d_ref, nm_ref, nv_ref):
        gv = g_ref[...]
        nm = ADAM_B1 * m_ref[...] + (1.0 - ADAM_B1) * gv
        nv = ADAM_B2 * v_ref[...] + (1.0 - ADAM_B2) * (gv * gv)
        nm_ref[...] = nm
        nv_ref[...] = nv
        d_ref[...] = -ADAM_LR * ((nm / c1) / (jnp.sqrt(nv / c2) + ADAM_EPS) + ADAM_WD * w_ref[...])

    blk = pl.BlockSpec((tr, C), lambda i: (i, 0))
    return pl.pallas_call(
        body, name=name, grid=(R // tr,), in_specs=[blk] * 4, out_specs=[blk] * 3,
        out_shape=[jax.ShapeDtypeStruct((R, C), F32)] * 3,
        compiler_params=_params("parallel"))(w, g, m, v)


def _pack_rows(arrays):
    rows = []
    for a in arrays:
        flat = a.reshape(-1).astype(F32)
        pad = (-flat.shape[0]) % LANES
        rows.append(jnp.pad(flat, (0, pad)).reshape(-1, LANES))
    buf = jnp.concatenate(rows, axis=0)
    return jnp.pad(buf, ((0, (-buf.shape[0]) % 8), (0, 0)))


def _unpack_rows(buf, shapes):
    out, r = [], 0
    for s in shapes:
        n = math.prod(s)
        nr = -(-n // LANES)
        out.append(buf[r:r + nr].reshape(-1)[:n].reshape(s))
        r += nr
    return out


def _block_diag(w):
    H, a, b = w.shape
    eye = jnp.eye(H, dtype=w.dtype)
    return (eye[:, None, :, None] * w[:, :, None, :]).reshape(H * a, H * b)


def _block_diag_parts(d, H):
    a, b = d.shape[0] // H, d.shape[1] // H
    d4 = d.reshape(H, a, H, b)
    return jnp.stack([d4[h, :, h, :] for h in range(H)])


def _rs_add(names, grads, others):
    return [_add_halves(g, o, "rs_add_halves_" + n) for n, g, o in zip(names, grads, others)]


def _rs_sum(names, parts, recvs):
    return [_sum_chips(p, r, "rs_sum_chips_" + n) for n, p, r in zip(names, parts, recvs)]


def _step(x, mem, target, shards, small, tap_rows, tap_shapes):
    D = x.shape[1]
    nch = N_CHIPS
    p = dict(small)

    (w_in_f,) = _gather_weights([shards['w_in']])
    wf = {}

    def ici(names):
        return _gather_over_ici([shards[n] for n in names])

    ici_a, taps_sum = ici(['w_out', 'w_q']), _all_reduce_rows(tap_rows)
    (z, h1), couts = _fwd_in(x, p['mix_norm_g'], w_in_f, comm=_merge(ici_a, taps_sum))
    bufs_a, (taps,) = _split(couts, ici_a, taps_sum)
    p.update(zip(COL_SHARDED_SMALL, _unpack_rows(taps, tap_shapes)))
    wa_d = _block_diag(p['lru_w_a']).astype(MXU_DTYPE)
    wx_d = _block_diag(p['lru_w_x']).astype(MXU_DTYPE)
    heads = p['lru_w_a'].shape[0]
    pass_a, ici_b = _gather_pass_on(bufs_a), ici(['w_kv', 'w_o'])
    (h, y_lru), couts = _lru_fwd(z, p['lru_conv_w'], p['lru_conv_b'], wa_d, p['lru_b_a'], wx_d, p['lru_b_x'],
                                 p['lru_lambda'], comm=_merge(pass_a, ici_b))
    (wf['w_out'], wf['w_q']), bufs_b = _split(couts, pass_a, ici_b)
    pass_b, ici_c = _gather_pass_on(bufs_b), ici(['w_up'])
    (c1, c3), couts = _conf_fwd(z, p['conf_conv_w'], p['conf_conv_b'], p['conf_ln_g'], p['conf_ln_b'],
                                comm=_merge(pass_b, ici_c))
    (wf['w_kv'], wf['w_o']), bufs_c = _split(couts, pass_b, ici_c)
    w_out2 = wf['w_out'].reshape(2, -1, D)
    w_q = wf['w_q'].reshape(D, D)
    w_o = wf['w_o'].reshape(D, D)
    pass_c, ici_d = _gather_pass_on(bufs_c), ici(['w_down'])
    (x1, h2, q), couts = _fwd_out_q(x, y_lru, c3, w_out2, p['xa_norm_g'], w_q, comm=_merge(pass_c, ici_d))
    (wf['w_up'],), bufs_d = _split(couts, pass_c, ici_d)
    m, kv = _kv_fwd(mem, p['mem_norm_g'], wf['w_kv'])
    (o, x2, h3), (wf['w_down'],) = _attn_fwd(q, kv, x1, w_o, p['ffn_norm_g'], comm=_gather_pass_on(bufs_d))
    gu, act, dx3, loss_lanes, d_final_g = _ffn_fwd(h3, wf['w_up'], p['ffn_conv_w'], p['ffn_conv_b'], wf['w_down'],
                                                   x2, p['final_norm_g'], target)

    dgu, dx2, d_ffn_g, d_ffn_cw, d_ffn_cb = _ffn_bwd(dx3, wf['w_down'], wf['w_up'], gu, x2, p['ffn_norm_g'],
                                                     p['ffn_conv_w'], p['ffn_conv_b'])
    g_down = _wgrad(act, dx3[None], "wgrad_down").reshape(nch, -1, D)
    g_up, other = _wgrad(h3[None], dgu, "wgrad_up", comm=_exchange_halves([g_down]))
    (p_down,) = _rs_add(['w_down'], [g_down], other)
    sc_down, ex_up = _scatter_chip_sums([p_down]), _exchange_halves([g_up])
    (dq, dx1, dkv, d_xa_g), couts = _attn_bwd(dx2, w_o, q, kv, x1, p['xa_norm_g'], w_q, comm=_merge(sc_down, ex_up))
    recv, other = _split(couts, sc_down, ex_up)
    f_down = _rs_sum(['w_down'], [p_down], recv)
    (p_up,) = _rs_add(['w_up'], [g_up], other)
    mid = ['w_o', 'w_q', 'w_kv']
    g_o = _wgrad(o[None], dx2[None], "wgrad_o").reshape(nch, -1, D)
    g_q = _wgrad(h2[None], dq[None], "wgrad_q").reshape(nch, -1, D)
    g_kv, d_mem_g = _kv_bwd(dkv, wf['w_kv'], mem, p['mem_norm_g'], m)
    join_down, sc_up, ex_mid = _join_halves(f_down), _scatter_chip_sums([p_up]), _exchange_halves([g_o, g_q, g_kv])
    (dz_c, d_conf_cw, d_conf_cb, d_ln_g, d_ln_b), couts = _conf_bwd(
        dx1, w_out2[1], z, c1, p['conf_conv_w'], p['conf_ln_g'], p['conf_ln_b'],
        comm=_merge(join_down, sc_up, ex_mid))
    (r_down,), recv, other = _split(couts, join_down, sc_up, ex_mid)
    p_up = [p_up]
    p_mid = _rs_add(mid, [g_o, g_q, g_kv], other)
    join_up, sc_mid = _join_halves(_rs_sum(['w_up'], p_up, recv)), _scatter_chip_sums(p_mid)
    (dz, d_wa, d_wx, d_ba, d_bx, d_lam, d_lru_cw, d_lru_cb), couts = _lru_bwd(
        dx1, w_out2[0], z, h, p['lru_conv_w'], p['lru_conv_b'], wa_d, p['lru_b_a'], wx_d, p['lru_b_x'],
        p['lru_lambda'], dz_c, comm=_merge(join_up, sc_mid))
    (r_up,), recv = _split(couts, join_up, sc_mid)
    f_mid = _rs_sum(mid, p_mid, recv)
    grad_x, d_mix_g = _bwd_in(dz, w_in_f, x, p['mix_norm_g'], dx1)

    small_g = {'mix_norm_g': d_mix_g, 'lru_conv_w': d_lru_cw, 'lru_conv_b': d_lru_cb,
               'lru_w_a': _block_diag_parts(d_wa, heads), 'lru_b_a': d_ba,
               'lru_w_x': _block_diag_parts(d_wx, heads), 'lru_b_x': d_bx, 'lru_lambda': d_lam,
               'conf_conv_w': d_conf_cw, 'conf_conv_b': d_conf_cb, 'conf_ln_g': d_ln_g, 'conf_ln_b': d_ln_b,
               'xa_norm_g': d_xa_g, 'mem_norm_g': d_mem_g, 'ffn_norm_g': d_ffn_g,
               'ffn_conv_w': d_ffn_cw, 'ffn_conv_b': d_ffn_cb, 'final_norm_g': d_final_g}
    names = list(small_g)
    shapes = [small_g[n].shape for n in names]
    join_mid = _join_halves(f_mid)
    small_sum = _all_reduce_rows(_pack_rows([loss_lanes] + [small_g[n] for n in names]), loss_row=0)
    g_in, couts = _wgrad(h1[None], dz, "wgrad_in", comm=_merge(join_mid, small_sum))
    r_mid, (summed, loss) = _split(couts, join_mid, small_sum)
    g_out_l, other = _wgrad(y_lru[None], dx1[None], "wgrad_out_lru", comm=_exchange_halves([g_in]))
    p_in = _rs_add(['w_in'], [g_in], other)
    g_out_c, recv = _wgrad(c3[None], dx1[None], "wgrad_out_conf", comm=_scatter_chip_sums(p_in))
    f_in = _rs_sum(['w_in'], p_in, recv)

    g_out = jnp.concatenate([g_out_l, g_out_c], axis=0).reshape(nch, -1, D)
    p_out = _rs_add(['w_out'], [g_out], _run_comm(_exchange_halves([g_out]), "rs_exchange_last"))
    f_out = _rs_sum(['w_out'], p_out, _run_comm(_scatter_chip_sums(p_out), "rs_scatter_last"))
    r_last = _run_comm(_join_halves(f_out + f_in), "rs_join_last")
    big = dict(zip(['w_down', 'w_up'] + mid + ['w_out', 'w_in'], [r_down, r_up] + r_mid + r_last))
    return grad_x, big, summed, loss, names, [loss_lanes.shape] + shapes


def kernel(x, mem, mix_norm_g, w_in, lru_conv_w, lru_conv_b, lru_w_a, lru_b_a, lru_w_x, lru_b_x, lru_lambda, conf_conv_w, conf_conv_b, conf_ln_g, conf_ln_b, w_out, xa_norm_g, mem_norm_g, w_q, w_kv, w_o, ffn_norm_g, w_up, ffn_conv_w, ffn_conv_b, w_down, final_norm_g, loss_target, m_mix_norm_g, m_w_in, m_lru_conv_w, m_lru_conv_b, m_lru_w_a, m_lru_b_a, m_lru_w_x, m_lru_b_x, m_lru_lambda, m_conf_conv_w, m_conf_conv_b, m_conf_ln_g, m_conf_ln_b, m_w_out, m_xa_norm_g, m_mem_norm_g, m_w_q, m_w_kv, m_w_o, m_ffn_norm_g, m_w_up, m_ffn_conv_w, m_ffn_conv_b, m_w_down, m_final_norm_g, v_mix_norm_g, v_w_in, v_lru_conv_w, v_lru_conv_b, v_lru_w_a, v_lru_b_a, v_lru_w_x, v_lru_b_x, v_lru_lambda, v_conf_conv_w, v_conf_conv_b, v_conf_ln_g, v_conf_ln_b, v_w_out, v_xa_norm_g, v_mem_norm_g, v_w_q, v_w_kv, v_w_o, v_ffn_norm_g, v_w_up, v_ffn_conv_w, v_ffn_conv_b, v_w_down, v_final_norm_g):
    given = dict(locals())
    w = {n: given[n] for n in WEIGHTS}
    mom = {n: given["m_" + n] for n in WEIGHTS}
    var = {n: given["v_" + n] for n in WEIGHTS}
    xi, yi, ci = lax.axis_index("x"), lax.axis_index("y"), lax.axis_index("c")
    chip = 2 * xi + yi

    shards = {n: w[n][0].astype(WIRE_DTYPE) for n in BIG}
    tap_full = []
    for n in COL_SHARDED_SMALL:
        s = w[n][0]
        full = jnp.zeros((s.shape[0], N_CHIPS * s.shape[1]), F32)
        s = jnp.where(ci == 0, s, jnp.zeros_like(s))
        tap_full.append(lax.dynamic_update_slice(full, s, (0, chip * s.shape[1])))
    small = {n: (w[n] if w[n].ndim == 1 else w[n][0]) for n in SMALL if n not in COL_SHARDED_SMALL}
    small = {n: (a.reshape(1, -1) if a.ndim == 1 else a) for n, a in small.items()}

    grad_x, big_g, summed, loss, small_names, packed_shapes = _step(
        x[0], mem[0], loss_target[0], shards, small, _pack_rows(tap_full), [t.shape for t in tap_full])
    small_sum = dict(zip(small_names, _unpack_rows(summed, packed_shapes)[1:]))

    grads = {}
    for n in WEIGHTS:
        if n in BIG:
            g = big_g[n]
        elif n in COL_SHARDED_SMALL:
            width = w[n].shape[-1]
            g = lax.dynamic_slice_in_dim(small_sum[n], chip * width, width, axis=1)
        else:
            g = small_sum[n]
        grads[n] = g.reshape(w[n].shape)

    delta, new_m, new_v = {}, {}, {}
    for n in BIG:
        d, nm, nv = _adamw(w[n][0], grads[n][0], mom[n][0], var[n][0], "adamw_" + n)
        delta[n], new_m[n], new_v[n] = d[None], nm[None], nv[None]
    shapes = [w[n].shape for n in SMALL]
    d, nm, nv = _adamw(_pack_rows([w[n] for n in SMALL]), _pack_rows([grads[n] for n in SMALL]),
                       _pack_rows([mom[n] for n in SMALL]), _pack_rows([var[n] for n in SMALL]), "adamw_small")
    for out, buf in ((delta, d), (new_m, nm), (new_v, nv)):
        out.update(dict(zip(SMALL, _unpack_rows(buf, shapes))))

    return (loss[0, 0], grad_x[None], *[grads[n] for n in WEIGHTS], *[delta[n] for n in WEIGHTS],
            *[new_m[n] for n in WEIGHTS], *[new_v[n] for n in WEIGHTS])
```

```python
import math

import jax
import jax.numpy as jnp
from jax import lax
from jax.experimental import pallas as pl
from jax.experimental.pallas import tpu as pltpu

F32 = jnp.float32
MXU_DTYPE = jnp.bfloat16
WIRE_DTYPE = jnp.bfloat16
EPS = 1e-6
RG_C = 8.0
XA_HEADS = 4
ADAM_LR, ADAM_B1, ADAM_B2, ADAM_EPS, ADAM_WD, ADAM_STEP = 0.001, 0.9, 0.999, 1e-08, 0.01, 10
VMEM_LIMIT_BYTES = 52 * 1024 * 1024
WGRAD_ACC_BYTES = 8 * 1024 * 1024
LANES = 1024
N_CHIPS = 4
N_DEV = 8
MESH = pl.DeviceIdType.MESH
GELU_C = math.sqrt(2.0 / math.pi)
GELU_K = 0.044715

WEIGHTS = ['mix_norm_g', 'w_in', 'lru_conv_w', 'lru_conv_b', 'lru_w_a', 'lru_b_a', 'lru_w_x', 'lru_b_x',
           'lru_lambda', 'conf_conv_w', 'conf_conv_b', 'conf_ln_g', 'conf_ln_b', 'w_out', 'xa_norm_g',
           'mem_norm_g', 'w_q', 'w_kv', 'w_o', 'ffn_norm_g', 'w_up', 'ffn_conv_w', 'ffn_conv_b', 'w_down',
           'final_norm_g']
BIG = ['w_in', 'w_kv', 'w_up', 'w_out', 'w_q', 'w_o', 'w_down']
SMALL = [n for n in WEIGHTS if n not in BIG]
COL_SHARDED_SMALL = ['lru_conv_w', 'conf_conv_w', 'ffn_conv_w']


def _params(*semantics):
    return pltpu.CompilerParams(dimension_semantics=semantics, vmem_limit_bytes=VMEM_LIMIT_BYTES)


ANY = pl.BlockSpec(memory_space=pl.ANY)
WHOLE_VMEM = pl.BlockSpec(memory_space=pltpu.VMEM)


class _Comm:
    def __init__(self, arrays, out_shapes, scratch, start, finish, aliases=None, in_specs=None, out_specs=None):
        self.arrays, self.out_shapes, self.scratch = list(arrays), list(out_shapes), list(scratch)
        self.start, self.finish = start, finish
        self.aliases = dict(aliases or {})
        self.in_specs = list(in_specs) if in_specs is not None else [ANY] * len(self.arrays)
        self.out_specs = list(out_specs) if out_specs is not None else [ANY] * len(self.out_shapes)


def _merge(*comms):
    comms = [c for c in comms if c is not None]
    if not comms:
        return None
    ai = [0]
    for c in comms:
        ai.append(ai[-1] + len(c.arrays))
    oi = [0]
    for c in comms:
        oi.append(oi[-1] + len(c.out_shapes))
    si = [0]
    for c in comms:
        si.append(si[-1] + len(c.scratch))

    def each(which):
        def run(ins, outs, scr):
            for k, c in enumerate(comms):
                getattr(c, which)(ins[ai[k]:ai[k + 1]], outs[oi[k]:oi[k + 1]], scr[si[k]:si[k + 1]])
        return run

    aliases = {ai[k] + i: oi[k] + o for k, c in enumerate(comms) for i, o in c.aliases.items()}
    return _Comm(sum((c.arrays for c in comms), []), sum((c.out_shapes for c in comms), []),
                 sum((c.scratch for c in comms), []), each("start"), each("finish"), aliases,
                 sum((c.in_specs for c in comms), []), sum((c.out_specs for c in comms), []))


def _split(outs, *comms):
    parts, at = [], 0
    for c in comms:
        parts.append(outs[at:at + len(c.out_shapes)])
        at += len(c.out_shapes)
    return parts


def _pcall(comm, body, *, name, grid, in_specs, out_specs, out_shape, semantics, scratch_shapes=(), aliases=None):
    single = not isinstance(out_shape, (list, tuple))
    out_shape = [out_shape] if single else list(out_shape)
    out_specs = [out_specs] if single else list(out_specs)
    in_specs, scratch_shapes = list(in_specs), list(scratch_shapes)
    aliases = dict(aliases or {})

    if comm is None:
        def plain(*args):
            return list(pl.pallas_call(body, name=name, grid=grid, in_specs=in_specs, out_specs=out_specs,
                                       out_shape=out_shape, scratch_shapes=scratch_shapes,
                                       input_output_aliases=aliases,
                                       compiler_params=_params(*semantics))(*args))
        return plain

    def hosted(*args):
        n_in, n_out, n_scr = len(args), len(out_shape), len(scratch_shapes)
        c_in, c_out = len(comm.arrays), len(comm.out_shapes)

        def wrapped(*refs):
            ins, cins = refs[:n_in], refs[n_in:n_in + c_in]
            o0 = n_in + c_in
            outs, couts = refs[o0:o0 + n_out], refs[o0 + n_out:o0 + n_out + c_out]
            s0 = o0 + n_out + c_out
            scr, cscr = refs[s0:s0 + n_scr], refs[s0 + n_scr:]
            first = last = None
            for axis, size in enumerate(grid):
                at_start, at_end = pl.program_id(axis) == 0, pl.program_id(axis) == size - 1
                first = at_start if first is None else first & at_start
                last = at_end if last is None else last & at_end
            if first is None:
                comm.start(cins, couts, cscr)
                body(*ins, *outs, *scr)
                comm.finish(cins, couts, cscr)
                return
            pl.when(first)(lambda: comm.start(cins, couts, cscr))
            body(*ins, *outs, *scr)
            pl.when(last)(lambda: comm.finish(cins, couts, cscr))

        res = pl.pallas_call(
            wrapped, name=name, grid=grid, in_specs=in_specs + comm.in_specs, out_specs=out_specs + comm.out_specs,
            out_shape=out_shape + comm.out_shapes, scratch_shapes=scratch_shapes + comm.scratch,
            input_output_aliases={**aliases, **{n_in + i: n_out + o for i, o in comm.aliases.items()}},
            compiler_params=pltpu.CompilerParams(dimension_semantics=("arbitrary",) * len(grid),
                                                 vmem_limit_bytes=VMEM_LIMIT_BYTES, has_side_effects=True),
        )(*args, *comm.arrays)
        return list(res[:n_out]), list(res[n_out:])

    return hosted


def _run_comm(comm, name):
    return _pcall(comm, lambda: None, name=name, grid=(), in_specs=[], out_specs=[], out_shape=[], semantics=())()[1]


def _tile(n, want, align=8):
    if n <= want:
        return n
    for t in range(want - want % align, 0, -align):
        if n % t == 0:
            return t
    raise ValueError((n, want, align))


def _mm(a, b):
    return jnp.dot(a.astype(MXU_DTYPE), b.astype(MXU_DTYPE), preferred_element_type=F32)


def _mm_nt(a, b):
    return lax.dot_general(a.astype(MXU_DTYPE), b.astype(MXU_DTYPE), (((1,), (1,)), ((), ())),
                           preferred_element_type=F32)


def _mm_tn(a, b):
    return lax.dot_general(a.astype(MXU_DTYPE), b.astype(MXU_DTYPE), (((0,), (0,)), ((), ())),
                           preferred_element_type=F32)


def _sigmoid(v):
    return 0.5 * jnp.tanh(0.5 * v) + 0.5


def _gelu(v):
    v2 = v * v
    t = jnp.tanh(v * (GELU_C + (GELU_C * GELU_K) * v2))
    hv = 0.5 * v
    dt = (1.0 - t * t) * (GELU_C + (3.0 * GELU_C * GELU_K) * v2)
    return hv + hv * t, (0.5 + 0.5 * t) + hv * dt


def _softplus_neg(lam):
    e = jnp.exp(-jnp.abs(lam))
    u = 1.0 + e
    log1p_e = jnp.where(u == 1.0, e, jnp.log(u) * e / jnp.where(u == 1.0, 1.0, u - 1.0))
    return jnp.maximum(-lam, 0.0) + log1p_e


def _rms(xv):
    rinv = lax.rsqrt(jnp.mean(xv * xv, axis=-1, keepdims=True) + EPS)
    return rinv, xv * rinv


def _rms_bwd(rinv, xhat, dxhat):
    return rinv * (dxhat - xhat * jnp.mean(dxhat * xhat, axis=-1, keepdims=True))


def _colsum(v):
    return jnp.sum(v, axis=0, keepdims=True)


def _wrow(w_ref, k, wcols):
    return w_ref[pl.ds(k, 1), :] if wcols is None else w_ref[pl.ds(k, 1), wcols]


def _windows(buf_ref, halo, taps, rows):
    assert taps <= 8 <= halo
    x = buf_ref[pl.ds(halo - 8, rows + 8), :]
    return [x[8:] if s == 0 else pltpu.roll(x, s, 0)[8:] for s in range(taps)]


def _causal_from(xs, w_ref, wcols=None):
    taps = len(xs)
    acc = None
    for s in range(taps):
        term = _wrow(w_ref, taps - 1 - s, wcols) * xs[s]
        acc = term if acc is None else acc + term
    return acc


def _tap_grads_from(dw_ref, dy, xs, wcols=None):
    taps = len(xs)
    for s in range(taps):
        g = _colsum(dy * xs[s])
        if wcols is None:
            dw_ref[pl.ds(taps - 1 - s, 1), :] += g
        else:
            dw_ref[pl.ds(taps - 1 - s, 1), wcols] += g


def _causal_taps(buf_ref, halo, w_ref, taps, rows, wcols=None):
    return _causal_from(_windows(buf_ref, halo, taps, rows), w_ref, wcols)


def _anticausal_taps(buf_ref, w_ref, taps, rows, wcols=None):
    assert taps <= 8
    x = buf_ref[pl.ds(0, rows + 8), :]
    acc = None
    for s in range(taps):
        win = x[:rows] if s == 0 else pltpu.roll(x, rows + 8 - s, 0)[:rows]
        term = _wrow(w_ref, taps - 1 - s, wcols) * win
        acc = term if acc is None else acc + term
    return acc


def _shift_copies(dst_ref, buf_ref, rows, up):
    x = buf_ref[pl.ds(0, rows + 8), :]
    for r in range(8):
        if up:
            dst_ref[r] = x[:rows] if r == 0 else pltpu.roll(x, rows + 8 - r, 0)[:rows]
        else:
            dst_ref[r] = x[8:] if r == 0 else pltpu.roll(x, r, 0)[8:]


def _causal_taps8(sh_ref, halo, w_ref, taps, rows):
    acc = None
    for s in range(taps):
        term = _wrow(w_ref, taps - 1 - s, None) * sh_ref[s % 8, pl.ds(halo - 8 - 8 * (s // 8), rows), :]
        acc = term if acc is None else acc + term
    return acc


def _anticausal_taps8(sh_ref, w_ref, taps, rows):
    acc = None
    for s in range(taps):
        term = _wrow(w_ref, taps - 1 - s, None) * sh_ref[s % 8, pl.ds(8 * (s // 8), rows), :]
        acc = term if acc is None else acc + term
    return acc


def _tap_grads8(dw_ref, dy, sh_ref, halo, taps, rows):
    for s in range(taps):
        dw_ref[pl.ds(taps - 1 - s, 1), :] += _colsum(dy * sh_ref[s % 8, pl.ds(halo - 8 - 8 * (s // 8), rows), :])


def _fwd_in(x, g, w_in, comm=None):
    S, D = x.shape
    nb, _, C = w_in.shape
    ts = _tile(S, 1024)

    def body(x_ref, g_ref, w_ref, z_ref, h_ref):
        _, xhat = _rms(x_ref[...])
        h = (xhat * g_ref[...]).astype(MXU_DTYPE)
        h_ref[...] = h
        for j in range(nb):
            z_ref[j] = jnp.dot(h, w_ref[j], preferred_element_type=F32)

    return _pcall(
        comm, body, name="fwd_in", grid=(S // ts,),
        in_specs=[pl.BlockSpec((ts, D), lambda i: (i, 0)), pl.BlockSpec((1, D), lambda i: (0, 0)),
                  pl.BlockSpec((nb, D, C), lambda i: (0, 0, 0))],
        out_specs=[pl.BlockSpec((nb, ts, C), lambda i: (0, i, 0)), pl.BlockSpec((ts, D), lambda i: (i, 0))],
        out_shape=[jax.ShapeDtypeStruct((nb, S, C), F32), jax.ShapeDtypeStruct((S, D), MXU_DTYPE)],
        semantics=("parallel",))(x, g, w_in)


def _lru_gates(xc, wa_ref, ba_ref, wx_ref, bx_ref, sp):
    xb = xc.astype(MXU_DTYPE)
    r = _sigmoid(jnp.dot(xb, wa_ref[...], preferred_element_type=F32) + ba_ref[...])
    ig = _sigmoid(jnp.dot(xb, wx_ref[...], preferred_element_type=F32) + bx_ref[...])
    log_a = -RG_C * r * sp
    a = jnp.exp(log_a)
    one_minus_a2 = jnp.tanh(-log_a) * (a * a + 1.0)
    inv_mult = lax.rsqrt(one_minus_a2)
    mult = jnp.where(one_minus_a2 > 0.0, one_minus_a2 * inv_mult, 0.0)
    return r, ig, a, mult, inv_mult


def _lru_fwd(z, conv_w, conv_b, wa, ba, wx, bx, lam, comm=None):
    _, S, C = z.shape
    ts = _tile(S, 256)
    taps = conv_w.shape[0]
    halo = 8

    def body(zx_ref, zg_ref, cw_ref, cb_ref, wa_ref, ba_ref, wx_ref, bx_ref, lam_ref,
             h_ref, y_ref, xbuf, a_s, u_s, hc):
        i = pl.program_id(0)

        @pl.when(i == 0)
        def _():
            xbuf[pl.ds(0, halo), :] = jnp.zeros((halo, C), F32)
            hc[...] = jnp.zeros_like(hc)

        xbuf[pl.ds(halo, ts), :] = zx_ref[0]
        xc = _causal_taps(xbuf, halo, cw_ref, taps, ts) + cb_ref[...]
        sp = _softplus_neg(lam_ref[...])
        _, ig, a, mult, _ = _lru_gates(xc, wa_ref, ba_ref, wx_ref, bx_ref, sp)
        a_s[...] = a
        u_s[...] = mult * (ig * xc)
        row = lax.broadcasted_iota(jnp.int32, (8, C), 0)

        def step(k, carry):
            off = pl.multiple_of(k * 8, 8)
            av = a_s[pl.ds(off, 8), :]
            uv = u_s[pl.ds(off, 8), :]
            for d in (1, 2, 4):
                m = row >= d
                a_sh = jnp.where(m, pltpu.roll(av, d, 0), 1.0)
                u_sh = jnp.where(m, pltpu.roll(uv, d, 0), 0.0)
                uv = uv + av * u_sh
                av = av * a_sh
            hv = uv + av * carry
            h_ref[pl.ds(off, 8), :] = hv
            return jnp.broadcast_to(hv[7:8, :], (8, C))

        hc[...] = lax.fori_loop(0, ts // 8, step, hc[...], unroll=4)
        ge, _ = _gelu(zg_ref[0])
        y_ref[...] = (h_ref[...] * ge).astype(MXU_DTYPE)
        xbuf[pl.ds(0, halo), :] = xbuf[pl.ds(ts, halo), :]

    vec = pl.BlockSpec((1, C), lambda i: (0, 0))
    mat = pl.BlockSpec((C, C), lambda i: (0, 0))
    return _pcall(
        comm, body, name="lru_fwd", grid=(S // ts,),
        in_specs=[pl.BlockSpec((1, ts, C), lambda i: (0, i, 0)), pl.BlockSpec((1, ts, C), lambda i: (1, i, 0)),
                  pl.BlockSpec((taps, C), lambda i: (0, 0)), vec, mat, vec, mat, vec, vec],
        out_specs=[pl.BlockSpec((ts, C), lambda i: (i, 0)), pl.BlockSpec((ts, C), lambda i: (i, 0))],
        out_shape=[jax.ShapeDtypeStruct((S, C), F32), jax.ShapeDtypeStruct((S, C), MXU_DTYPE)],
        scratch_shapes=[pltpu.VMEM((ts + halo, C), F32), pltpu.VMEM((ts, C), F32), pltpu.VMEM((ts, C), F32),
                        pltpu.VMEM((8, C), F32)],
        semantics=("arbitrary",))(z, z, conv_w, conv_b, wa, ba, wx, bx, lam)


def _layer_norm_stats(c1):
    mu = jnp.mean(c1, axis=-1, keepdims=True)
    xc = c1 - mu
    rstd = lax.rsqrt(jnp.mean(xc * xc, axis=-1, keepdims=True) + EPS)
    return rstd, xc * rstd


def _conf_fwd(z, conv_w, conv_b, ln_g, ln_b, comm=None):
    _, S, C = z.shape
    ts = _tile(S, 256)
    taps = conv_w.shape[0]
    halo = 32

    def body(za_ref, zb_ref, cw_ref, cb_ref, g_ref, b_ref, c1_ref, c3_ref, cbuf, shifted):
        i = pl.program_id(0)

        @pl.when(i == 0)
        def _():
            cbuf[pl.ds(0, halo), :] = jnp.zeros((halo, C), F32)

        cbuf[pl.ds(halo, ts), :] = za_ref[0] * _sigmoid(zb_ref[0])
        _shift_copies(shifted, cbuf, ts + halo - 8, up=False)
        c1 = _causal_taps8(shifted, halo, cw_ref, taps, ts) + cb_ref[...]
        c1_ref[...] = c1
        _, xhat = _layer_norm_stats(c1)
        c2 = xhat * g_ref[...] + b_ref[...]
        c3_ref[...] = (c2 * _sigmoid(c2)).astype(MXU_DTYPE)
        cbuf[pl.ds(0, halo), :] = cbuf[pl.ds(ts, halo), :]

    vec = pl.BlockSpec((1, C), lambda i: (0, 0))
    return _pcall(
        comm, body, name="conf_fwd", grid=(S // ts,),
        in_specs=[pl.BlockSpec((1, ts, C), lambda i: (2, i, 0)), pl.BlockSpec((1, ts, C), lambda i: (3, i, 0)),
                  pl.BlockSpec((taps, C), lambda i: (0, 0)), vec, vec, vec],
        out_specs=[pl.BlockSpec((ts, C), lambda i: (i, 0)), pl.BlockSpec((ts, C), lambda i: (i, 0))],
        out_shape=[jax.ShapeDtypeStruct((S, C), F32), jax.ShapeDtypeStruct((S, C), MXU_DTYPE)],
        scratch_shapes=[pltpu.VMEM((ts + halo, C), F32), pltpu.VMEM((8, ts + halo - 8, C), F32)],
        semantics=("arbitrary",))(z, z, conv_w, conv_b, ln_g, ln_b)


def _fwd_out_q(x, y_lru, c3, w_out, g_xa, w_q, comm=None):
    S, D = x.shape
    C = y_lru.shape[1]
    ts = _tile(S, 1024)

    def body(x_ref, yl_ref, c3_ref, wo_ref, g_ref, wq_ref, x1_ref, h2_ref, q_ref):
        x1 = (x_ref[...] + jnp.dot(yl_ref[...], wo_ref[0], preferred_element_type=F32)
              + jnp.dot(c3_ref[...], wo_ref[1], preferred_element_type=F32))
        x1_ref[...] = x1
        _, xhat = _rms(x1)
        h2 = (xhat * g_ref[...]).astype(MXU_DTYPE)
        h2_ref[...] = h2
        q_ref[...] = jnp.dot(h2, wq_ref[...], preferred_element_type=F32).astype(MXU_DTYPE)

    row = lambda w: pl.BlockSpec((ts, w), lambda i: (i, 0))
    return _pcall(
        comm, body, name="fwd_out_q", grid=(S // ts,),
        in_specs=[row(D), row(C), row(C), pl.BlockSpec((2, C, D), lambda i: (0, 0, 0)),
                  pl.BlockSpec((1, D), lambda i: (0, 0)), pl.BlockSpec((D, D), lambda i: (0, 0))],
        out_specs=[row(D), row(D), row(D)],
        out_shape=[jax.ShapeDtypeStruct((S, D), F32), jax.ShapeDtypeStruct((S, D), MXU_DTYPE),
                   jax.ShapeDtypeStruct((S, D), MXU_DTYPE)],
        semantics=("parallel",))(x, y_lru, c3, w_out, g_xa, w_q)


def _kv_fwd(mem, g, w_kv):
    M, D = mem.shape
    nb, _, C = w_kv.shape

    def body(mem_ref, g_ref, w_ref, m_ref, kv_ref):
        _, xhat = _rms(mem_ref[...])
        m = (xhat * g_ref[...]).astype(MXU_DTYPE)
        m_ref[...] = m
        for j in range(nb):
            kv_ref[:, pl.ds(j * C, C)] = jnp.dot(m, w_ref[j], preferred_element_type=F32).astype(MXU_DTYPE)

    return pl.pallas_call(
        body, name="kv_fwd", grid=(1,),
        in_specs=[pl.BlockSpec((M, D), lambda i: (0, 0)), pl.BlockSpec((1, D), lambda i: (0, 0)),
                  pl.BlockSpec((nb, D, C), lambda i: (0, 0, 0))],
        out_specs=[pl.BlockSpec((M, D), lambda i: (0, 0)), pl.BlockSpec((M, nb * C), lambda i: (0, 0))],
        out_shape=[jax.ShapeDtypeStruct((M, D), MXU_DTYPE), jax.ShapeDtypeStruct((M, nb * C), MXU_DTYPE)],
        compiler_params=_params("arbitrary"))(mem, g, w_kv)


def _softmax_rows(s):
    e = jnp.exp(s - jnp.max(s, axis=-1, keepdims=True))
    return e / jnp.sum(e, axis=-1, keepdims=True)


def _attn_fwd(q, kv, x1, w_o, g_ffn, comm=None):
    S, D = x1.shape
    M = kv.shape[0]
    hd = D // XA_HEADS
    scale = hd ** -0.5
    ts = _tile(S, 1024)

    def body(q_ref, kv_ref, x1_ref, wo_ref, g_ref, o_ref, x2_ref, h3_ref):
        for h in range(XA_HEADS):
            cols = pl.ds(h * hd, hd)
            p = _softmax_rows(_mm_nt(q_ref[:, cols], kv_ref[:, cols]) * scale)
            o_ref[:, cols] = _mm(p, kv_ref[:, pl.ds(D + h * hd, hd)]).astype(MXU_DTYPE)
        x2 = x1_ref[...] + jnp.dot(o_ref[...], wo_ref[...], preferred_element_type=F32)
        x2_ref[...] = x2
        _, xhat = _rms(x2)
        h3_ref[...] = (xhat * g_ref[...]).astype(MXU_DTYPE)

    row = pl.BlockSpec((ts, D), lambda i: (i, 0))
    return _pcall(
        comm, body, name="attn_fwd", grid=(S // ts,),
        in_specs=[row, pl.BlockSpec((M, 2 * D), lambda i: (0, 0)), row, pl.BlockSpec((D, D), lambda i: (0, 0)),
                  pl.BlockSpec((1, D), lambda i: (0, 0))],
        out_specs=[row, row, row],
        out_shape=[jax.ShapeDtypeStruct((S, D), MXU_DTYPE), jax.ShapeDtypeStruct((S, D), F32),
                   jax.ShapeDtypeStruct((S, D), MXU_DTYPE)],
        semantics=("parallel",))(q, kv, x1, w_o, g_ffn)


def _ffn_fwd(h3, w_up, conv_w, conv_b, w_down, x2, g_final, target, comm=None):
    S, D = h3.shape
    nb, _, CW = w_up.shape
    half = nb // 2
    cb = 768
    per = CW // cb
    J = half * per
    ts = _tile(S, 256)
    taps = conv_w.shape[0]
    halo = 8

    def body(h_ref, wup_ref, cw_ref, cb_ref, wd_ref, x2_ref, gf_ref, t_ref,
             gu_ref, act_ref, dx3_ref, loss_ref, dgf_ref, gbuf):
        i = pl.program_id(0)

        @pl.when(i == 0)
        def _():
            for ref in (loss_ref, dgf_ref, gbuf):
                ref[...] = jnp.zeros_like(ref)

        hv = h_ref[...]
        x3 = x2_ref[...]
        for j in range(J):
            b, cols, wcols = j // per, pl.ds((j % per) * cb, cb), pl.ds(j * cb, cb)
            g = jnp.dot(hv, wup_ref[b, :, cols], preferred_element_type=F32)
            u = jnp.dot(hv, wup_ref[half + b, :, cols], preferred_element_type=F32)
            gu_ref[0, b, :, cols] = g
            gu_ref[1, b, :, cols] = u
            gbuf[j, pl.ds(halo, ts), :] = g
            gc = _causal_taps(gbuf.at[j], halo, cw_ref, taps, ts, wcols=wcols) + cb_ref[:, wcols]
            gbuf[j, pl.ds(0, halo), :] = gbuf[j, pl.ds(ts, halo), :]
            ge, _ = _gelu(gc)
            act = (ge * u).astype(MXU_DTYPE)
            act_ref[j] = act
            x3 = x3 + jnp.dot(act, wd_ref[j], preferred_element_type=F32)
        rinv, xhat = _rms(x3)
        gf = gf_ref[...]
        diff = xhat * gf - t_ref[...]
        loss_ref[...] += _colsum(diff * diff) * (0.5 / D)
        dy = diff * (1.0 / D)
        dgf_ref[...] += _colsum(dy * xhat)
        dx3_ref[...] = _rms_bwd(rinv, xhat, dy * gf)

    row = pl.BlockSpec((ts, D), lambda i: (i, 0))
    vecd = pl.BlockSpec((1, D), lambda i: (0, 0))
    once = pl.Buffered(1)
    sds = jax.ShapeDtypeStruct
    res = _pcall(
        comm, body, name="ffn_fwd", grid=(S // ts,),
        in_specs=[row, pl.BlockSpec((nb, D, CW), lambda i: (0, 0, 0), pipeline_mode=once),
                  pl.BlockSpec((taps, half * CW), lambda i: (0, 0)), pl.BlockSpec((1, half * CW), lambda i: (0, 0)),
                  pl.BlockSpec((J, cb, D), lambda i: (0, 0, 0), pipeline_mode=once), row, vecd, row],
        out_specs=[pl.BlockSpec((2, half, ts, CW), lambda i: (0, 0, i, 0)),
                   pl.BlockSpec((J, ts, cb), lambda i: (0, i, 0)), row, vecd, vecd],
        out_shape=[sds((2, half, S, CW), F32), sds((J, S, cb), MXU_DTYPE), sds((S, D), F32),
                   sds((1, D), F32), sds((1, D), F32)],
        scratch_shapes=[pltpu.VMEM((J, ts + halo, cb), F32)],
        semantics=("arbitrary",))(h3, w_up, conv_w, conv_b, w_down.reshape(J, cb, D), x2, g_final, target)
    outs = res if comm is None else res[0]
    outs = [outs[0].reshape(nb, S, CW)] + list(outs[1:])
    return outs if comm is None else (outs, res[1])


def _ffn_bwd(dx3, w_down, w_up, gu, x2, g_ffn, conv_w, conv_b, comm=None):
    nb, S, CW = gu.shape
    half = nb // 2
    D = dx3.shape[1]
    cb = 768
    per = CW // cb
    J = half * per
    ts = _tile(S, 256)
    n = S // ts
    taps = conv_w.shape[0]
    halo = 8
    hb = ts // halo

    def body(dx_ref, x2_ref, gf_ref, wd_ref, wup_ref, gu_ref, gh_ref, cw_ref, cb_ref,
             dgu_ref, dx2_ref, dgf_ref, dcw_ref, dcb_ref, gbuf, dbuf):
        i = pl.program_id(0)
        r = n - 1 - i

        @pl.when(i == 0)
        def _():
            for ref in (dgf_ref, dcw_ref, dcb_ref, dbuf):
                ref[...] = jnp.zeros_like(ref)

        dx3v = dx_ref[...]
        dxb = dx3v.astype(MXU_DTYPE)
        dacts = [_mm_nt(dxb, wd_ref[j]) for j in range(J)]
        dh = None
        for j in range(J):
            b, cols, wcols = j // per, pl.ds((j % per) * cb, cb), pl.ds(j * cb, cb)
            dact = dacts[j]
            gbuf[pl.ds(0, halo), :] = jnp.where(r > 0, gh_ref[0, b, :, cols], 0.0)
            gbuf[pl.ds(halo, ts), :] = gu_ref[0, b, :, cols]
            gs = _windows(gbuf, halo, taps, ts)
            gc = _causal_from(gs, cw_ref, wcols) + cb_ref[:, wcols]
            ge, dge = _gelu(gc)
            dub = (dact * ge).astype(MXU_DTYPE)
            dgc = dact * gu_ref[1, b, :, cols] * dge
            dcb_ref[:, wcols] += _colsum(dgc)
            dbuf[j, pl.ds(0, ts), :] = dgc
            _tap_grads_from(dcw_ref, dgc, gs, wcols)
            dgb = _anticausal_taps(dbuf.at[j], cw_ref, taps, ts, wcols=wcols).astype(MXU_DTYPE)
            dbuf[j, pl.ds(ts, halo), :] = dbuf[j, pl.ds(0, halo), :]
            dgu_ref[0, b, :, cols] = dgb
            dgu_ref[1, b, :, cols] = dub
            part = _mm_nt(dgb, wup_ref[b, :, cols]) + _mm_nt(dub, wup_ref[half + b, :, cols])
            dh = part if dh is None else dh + part
        rinv, xhat = _rms(x2_ref[...])
        dgf_ref[...] += _colsum(dh * xhat)
        dx2_ref[...] = dx3v + _rms_bwd(rinv, xhat, dh * gf_ref[...])

    gu2 = gu.reshape(2, half, S, CW)
    row = pl.BlockSpec((ts, D), lambda i: (n - 1 - i, 0))
    vecd = pl.BlockSpec((1, D), lambda i: (0, 0))
    pair = pl.BlockSpec((2, half, ts, CW), lambda i: (0, 0, n - 1 - i, 0))
    g_prev = pl.BlockSpec((1, half, halo, CW), lambda i: (0, 0, jnp.maximum((n - 1 - i) * hb - 1, 0), 0))
    tapw = pl.BlockSpec((taps, half * CW), lambda i: (0, 0))
    vec = pl.BlockSpec((1, half * CW), lambda i: (0, 0))
    once = pl.Buffered(1)
    sds = jax.ShapeDtypeStruct
    res = _pcall(
        comm, body, name="ffn_bwd", grid=(n,),
        in_specs=[row, row, vecd, pl.BlockSpec((J, cb, D), lambda i: (0, 0, 0), pipeline_mode=once),
                  pl.BlockSpec((nb, D, CW), lambda i: (0, 0, 0), pipeline_mode=once), pair, g_prev, tapw, vec],
        out_specs=[pair, row, vecd, tapw, vec],
        out_shape=[sds((2, half, S, CW), MXU_DTYPE), sds((S, D), F32), sds((1, D), F32),
                   sds((taps, half * CW), F32), sds((1, half * CW), F32)],
        scratch_shapes=[pltpu.VMEM((ts + halo, cb), F32), pltpu.VMEM((J, ts + halo, cb), F32)],
        semantics=("arbitrary",))(dx3, x2, g_ffn, w_down.reshape(J, cb, D), w_up, gu2, gu2, conv_w, conv_b)
    outs = res if comm is None else res[0]
    outs = [outs[0].reshape(nb, S, CW)] + list(outs[1:])
    return outs if comm is None else (outs, res[1])


def _attn_bwd(dx2, w_o, q, kv, x1, g_xa, w_q, comm=None):
    S, D = x1.shape
    M = kv.shape[0]
    hd = D // XA_HEADS
    scale = hd ** -0.5
    ts = _tile(S, 1024)

    def body(dx2_ref, wo_ref, q_ref, kv_ref, x1_ref, g_ref, wq_ref, dq_ref, dx1_ref, dkv_ref, dg_ref):
        i = pl.program_id(0)

        @pl.when(i == 0)
        def _():
            dkv_ref[...] = jnp.zeros_like(dkv_ref)
            dg_ref[...] = jnp.zeros_like(dg_ref)

        dx2 = dx2_ref[...]
        do = _mm_nt(dx2, wo_ref[...]).astype(MXU_DTYPE)
        for h in range(XA_HEADS):
            cols = pl.ds(h * hd, hd)
            vcols = pl.ds(D + h * hd, hd)
            qh, kh, doh = q_ref[:, cols], kv_ref[:, cols], do[:, h * hd:(h + 1) * hd]
            p = _softmax_rows(_mm_nt(qh, kh) * scale)
            dp = _mm_nt(doh, kv_ref[:, vcols])
            dkv_ref[:, vcols] += _mm_tn(p, doh)
            ds = (p * (dp - jnp.sum(dp * p, axis=-1, keepdims=True)) * scale).astype(MXU_DTYPE)
            dq_ref[:, cols] = _mm(ds, kh).astype(MXU_DTYPE)
            dkv_ref[:, cols] += _mm_tn(ds, qh)
        dh2 = _mm_nt(dq_ref[...], wq_ref[...])
        rinv, xhat = _rms(x1_ref[...])
        dg_ref[...] += _colsum(dh2 * xhat)
        dx1_ref[...] = dx2 + _rms_bwd(rinv, xhat, dh2 * g_ref[...])

    row = pl.BlockSpec((ts, D), lambda i: (i, 0))
    mat = pl.BlockSpec((D, D), lambda i: (0, 0))
    vecd = pl.BlockSpec((1, D), lambda i: (0, 0))
    kvs = pl.BlockSpec((M, 2 * D), lambda i: (0, 0))
    return _pcall(
        comm, body, name="attn_bwd", grid=(S // ts,),
        in_specs=[row, mat, row, kvs, row, vecd, mat],
        out_specs=[row, row, kvs, vecd],
        out_shape=[jax.ShapeDtypeStruct((S, D), MXU_DTYPE), jax.ShapeDtypeStruct((S, D), F32),
                   jax.ShapeDtypeStruct((M, 2 * D), F32), jax.ShapeDtypeStruct((1, D), F32)],
        semantics=("arbitrary",))(dx2, w_o, q, kv, x1, g_xa, w_q)


def _kv_bwd(dkv, w_kv, mem, g, m):
    M, D = mem.shape
    nb, _, C = w_kv.shape

    def body(dkv_ref, w_ref, mem_ref, m_ref, dw_ref, dg_ref):
        dm = jnp.zeros((M, D), F32)
        for j in range(nb):
            dj = dkv_ref[:, pl.ds(j * C, C)].astype(MXU_DTYPE)
            dw_ref[j] = _mm_tn(m_ref[...], dj).astype(dw_ref.dtype)
            dm = dm + _mm_nt(dj, w_ref[j])
        _, xhat = _rms(mem_ref[...])
        dg_ref[...] = _colsum(dm * xhat)

    full = lambda *s: pl.BlockSpec(s, lambda i: (0,) * len(s))
    return pl.pallas_call(
        body, name="kv_bwd", grid=(1,),
        in_specs=[full(M, nb * C), full(nb, D, C), full(M, D), full(M, D)],
        out_specs=[full(nb, D, C), full(1, D)],
        out_shape=[jax.ShapeDtypeStruct((nb, D, C), WIRE_DTYPE), jax.ShapeDtypeStruct((1, D), F32)],
        compiler_params=_params("arbitrary"))(dkv, w_kv, mem, m)


def _conf_bwd(dx1, w_out_c, z, c1, conv_w, ln_g, ln_b, comm=None):
    _, S, C = z.shape
    D = dx1.shape[1]
    ts = _tile(S, 256)
    n = S // ts
    taps = conv_w.shape[0]
    halo = 32
    hb = ts // halo

    def body(dx_ref, wo_ref, za_ref, zb_ref, zah_ref, zbh_ref, c1_ref, cw_ref, g_ref, b_ref,
             dz_ref, dcw_ref, dcb_ref, dlg_ref, dlb_ref, c0buf, dbuf, shifted):
        i = pl.program_id(0)
        r = n - 1 - i

        @pl.when(i == 0)
        def _():
            for ref in (dcw_ref, dcb_ref, dlg_ref, dlb_ref):
                ref[...] = jnp.zeros_like(ref)
            dbuf[pl.ds(ts, halo), :] = jnp.zeros((halo, C), F32)

        za = za_ref[0]
        sb = _sigmoid(zb_ref[0])
        c0buf[pl.ds(0, halo), :] = jnp.where(r > 0, zah_ref[0] * _sigmoid(zbh_ref[0]), 0.0)
        c0buf[pl.ds(halo, ts), :] = za * sb
        dc3 = _mm_nt(dx_ref[...], wo_ref[...])
        rstd, xhat = _layer_norm_stats(c1_ref[...])
        g = g_ref[...]
        c2 = xhat * g + b_ref[...]
        sg = _sigmoid(c2)
        dc2 = dc3 * sg * (1.0 + c2 * (1.0 - sg))
        dlg_ref[...] += _colsum(dc2 * xhat)
        dlb_ref[...] += _colsum(dc2)
        dxh = dc2 * g
        dc1 = rstd * (dxh - jnp.mean(dxh, axis=-1, keepdims=True)
                      - xhat * jnp.mean(dxh * xhat, axis=-1, keepdims=True))
        dcb_ref[...] += _colsum(dc1)
        dbuf[pl.ds(0, ts), :] = dc1
        _shift_copies(shifted, c0buf, ts + halo - 8, up=False)
        _tap_grads8(dcw_ref, dc1, shifted, halo, taps, ts)
        _shift_copies(shifted, dbuf, ts + halo - 8, up=True)
        dc0 = _anticausal_taps8(shifted, cw_ref, taps, ts)
        dz_ref[0] = (dc0 * sb).astype(MXU_DTYPE)
        dz_ref[1] = (dc0 * za * sb * (1.0 - sb)).astype(MXU_DTYPE)
        dbuf[pl.ds(ts, halo), :] = dbuf[pl.ds(0, halo), :]

    vec = pl.BlockSpec((1, C), lambda i: (0, 0))
    tapw = pl.BlockSpec((taps, C), lambda i: (0, 0))
    tile = lambda b: pl.BlockSpec((1, ts, C), lambda i: (b, n - 1 - i, 0))
    prev = lambda b: pl.BlockSpec((1, halo, C), lambda i: (b, jnp.maximum((n - 1 - i) * hb - 1, 0), 0))
    return _pcall(
        comm, body, name="conf_bwd", grid=(n,),
        in_specs=[pl.BlockSpec((ts, D), lambda i: (n - 1 - i, 0)), pl.BlockSpec((C, D), lambda i: (0, 0)),
                  tile(2), tile(3), prev(2), prev(3), pl.BlockSpec((ts, C), lambda i: (n - 1 - i, 0)),
                  tapw, vec, vec],
        out_specs=[pl.BlockSpec((2, ts, C), lambda i: (1, n - 1 - i, 0)), tapw, vec, vec, vec],
        out_shape=[jax.ShapeDtypeStruct((4, S, C), MXU_DTYPE), jax.ShapeDtypeStruct((taps, C), F32),
                   jax.ShapeDtypeStruct((1, C), F32), jax.ShapeDtypeStruct((1, C), F32),
                   jax.ShapeDtypeStruct((1, C), F32)],
        scratch_shapes=[pltpu.VMEM((ts + halo, C), F32), pltpu.VMEM((ts + halo, C), F32),
                        pltpu.VMEM((8, ts + halo - 8, C), F32)],
        semantics=("arbitrary",))(dx1, w_out_c, z, z, z, z, c1, conv_w, ln_g, ln_b)


def _lru_bwd(dx1, w_out_l, z, h, conv_w, conv_b, wa, ba, wx, bx, lam, dz, comm=None):
    _, S, C = z.shape
    D = dx1.shape[1]
    ts = _tile(S, 256)
    n = S // ts
    taps = conv_w.shape[0]
    halo = 8
    hb = ts // halo

    def body(dx_ref, wo_ref, zx_ref, zxh_ref, zg_ref, h_ref, hh_ref, cw_ref, cb_ref, wa_ref, ba_ref,
             wx_ref, bx_ref, lam_ref, dz_in,
             dz_ref, dwa_ref, dwx_ref, dba_ref, dbx_ref, dlam_ref, dcw_ref, dcb_ref,
             xbuf, hbuf, a_s, w_s, dh_s, g_s, dbuf, pc):
        i = pl.program_id(0)
        r = n - 1 - i

        @pl.when(i == 0)
        def _():
            for ref in (dwa_ref, dwx_ref, dba_ref, dbx_ref, dlam_ref, dcw_ref, dcb_ref, pc):
                ref[...] = jnp.zeros_like(ref)
            dbuf[pl.ds(ts, halo), :] = jnp.zeros((halo, C), F32)

        xbuf[pl.ds(0, halo), :] = jnp.where(r > 0, zxh_ref[0], 0.0)
        xbuf[pl.ds(halo, ts), :] = zx_ref[0]
        hbuf[pl.ds(0, halo), :] = jnp.where(r > 0, hh_ref[...], 0.0)
        hbuf[pl.ds(halo, ts), :] = h_ref[...]
        xs = _windows(xbuf, halo, taps, ts)
        xc = _causal_from(xs, cw_ref) + cb_ref[...]
        lam_v = lam_ref[...]
        sp = _softplus_neg(lam_v)
        rg, ig, a, mult, inv_mult = _lru_gates(xc, wa_ref, ba_ref, wx_ref, bx_ref, sp)

        dy = _mm_nt(dx_ref[...], wo_ref[...])
        ge, dge = _gelu(zg_ref[0])
        dh = dy * ge
        dz_ref[1] = (dy * h_ref[...] * dge).astype(MXU_DTYPE)
        a_s[...] = a
        w_s[...] = a * dh
        dh_s[...] = dh
        row = lax.broadcasted_iota(jnp.int32, (8, C), 0)

        def step(kk, carry):
            off = pl.multiple_of((ts // 8 - 1 - kk) * 8, 8)
            av = a_s[pl.ds(off, 8), :]
            wv = w_s[pl.ds(off, 8), :]
            for d in (1, 2, 4):
                m = row < 8 - d
                a_sh = jnp.where(m, pltpu.roll(av, 8 - d, 0), 1.0)
                w_sh = jnp.where(m, pltpu.roll(wv, 8 - d, 0), 0.0)
                wv = wv + av * w_sh
                av = av * a_sh
            pv = wv + av * carry
            g_s[pl.ds(off, 8), :] = dh_s[pl.ds(off, 8), :] + jnp.where(row < 7, pltpu.roll(pv, 7, 0), carry)
            return jnp.broadcast_to(pv[0:1, :], (8, C))

        pc[...] = lax.fori_loop(0, ts // 8, step, pc[...], unroll=4)
        gt = g_s[...]
        da = gt * hbuf[pl.ds(halo - 1, ts), :]
        gm = gt * mult
        dlog_a = da * a - (gt * ig * xc) * (a * a) * inv_mult
        dlam_ref[...] += _colsum(dlog_a * rg) * (RG_C / (1.0 + jnp.exp(lam_v)))
        dpa = (dlog_a * (-RG_C * sp)) * rg * (1.0 - rg)
        dpx = (gm * xc) * ig * (1.0 - ig)
        dba_ref[...] += _colsum(dpa)
        dbx_ref[...] += _colsum(dpx)
        xb = xc.astype(MXU_DTYPE)
        dpab, dpxb = dpa.astype(MXU_DTYPE), dpx.astype(MXU_DTYPE)
        dwa_ref[...] += _mm_tn(xb, dpab)
        dwx_ref[...] += _mm_tn(xb, dpxb)
        dxc = gm * ig + _mm_nt(dpab, wa_ref[...]) + _mm_nt(dpxb, wx_ref[...])
        dcb_ref[...] += _colsum(dxc)
        dbuf[pl.ds(0, ts), :] = dxc
        _tap_grads_from(dcw_ref, dxc, xs)
        dz_ref[0] = _anticausal_taps(dbuf, cw_ref, taps, ts).astype(MXU_DTYPE)
        dbuf[pl.ds(ts, halo), :] = dbuf[pl.ds(0, halo), :]

    vec = pl.BlockSpec((1, C), lambda i: (0, 0))
    mat = pl.BlockSpec((C, C), lambda i: (0, 0))
    tapw = pl.BlockSpec((taps, C), lambda i: (0, 0))
    prev_rows = lambda i: jnp.maximum((n - 1 - i) * hb - 1, 0)
    sds = jax.ShapeDtypeStruct
    return _pcall(
        comm, body, name="lru_bwd", grid=(n,),
        in_specs=[pl.BlockSpec((ts, D), lambda i: (n - 1 - i, 0)), pl.BlockSpec((C, D), lambda i: (0, 0)),
                  pl.BlockSpec((1, ts, C), lambda i: (0, n - 1 - i, 0)),
                  pl.BlockSpec((1, halo, C), lambda i: (0, prev_rows(i), 0)),
                  pl.BlockSpec((1, ts, C), lambda i: (1, n - 1 - i, 0)),
                  pl.BlockSpec((ts, C), lambda i: (n - 1 - i, 0)),
                  pl.BlockSpec((halo, C), lambda i: (prev_rows(i), 0)),
                  tapw, vec, mat, vec, mat, vec, vec, ANY],
        out_specs=[pl.BlockSpec((2, ts, C), lambda i: (0, n - 1 - i, 0)), mat, mat, vec, vec, vec, tapw, vec],
        out_shape=[sds(dz.shape, MXU_DTYPE), sds((C, C), F32), sds((C, C), F32), sds((1, C), F32),
                   sds((1, C), F32), sds((1, C), F32), sds((taps, C), F32), sds((1, C), F32)],
        scratch_shapes=[pltpu.VMEM((ts + halo, C), F32), pltpu.VMEM((ts + halo, C), F32)]
        + [pltpu.VMEM((ts, C), F32)] * 4 + [pltpu.VMEM((ts + halo, C), F32), pltpu.VMEM((8, C), F32)],
        aliases={14: 0},
        semantics=("arbitrary",))(dx1, w_out_l, z, z, z, h, h, conv_w, conv_b, wa, ba, wx, bx, lam, dz)


def _bwd_in(dz, w_in, x, g, dx1):
    S, D = x.shape
    nb, _, C = w_in.shape
    ts = _tile(S, 512)

    def body(dz_ref, w_ref, x_ref, g_ref, dx1_ref, dx_ref, dg_ref):
        i = pl.program_id(0)

        @pl.when(i == 0)
        def _():
            dg_ref[...] = jnp.zeros_like(dg_ref)

        dh = _mm_nt(dz_ref[0], w_ref[0])
        for j in range(1, nb):
            dh = dh + _mm_nt(dz_ref[j], w_ref[j])
        rinv, xhat = _rms(x_ref[...])
        dg_ref[...] += _colsum(dh * xhat)
        dx_ref[...] = dx1_ref[...] + _rms_bwd(rinv, xhat, dh * g_ref[...])

    row = pl.BlockSpec((ts, D), lambda i: (i, 0))
    vecd = pl.BlockSpec((1, D), lambda i: (0, 0))
    return pl.pallas_call(
        body, name="bwd_in", grid=(S // ts,),
        in_specs=[pl.BlockSpec((nb, ts, C), lambda i: (0, i, 0)), pl.BlockSpec((nb, D, C), lambda i: (0, 0, 0)),
                  row, vecd, row],
        out_specs=[row, vecd],
        out_shape=[jax.ShapeDtypeStruct((S, D), F32), jax.ShapeDtypeStruct((1, D), F32)],
        compiler_params=_params("arbitrary"))(dz, w_in, x, g, dx1)


def _wgrad(a, b, name, comm=None):
    na, S, K = a.shape
    nb, _, N = b.shape
    nj = max(na, nb)
    assert min(na, nb) == 1
    ts = _tile(S, 1024)
    ns = S // ts
    grp = max(g for g in range(1, nj + 1) if nj % g == 0 and g * K * N * 4 <= WGRAD_ACC_BYTES)
    ga, gb = (grp if na > 1 else 1), (grp if nb > 1 else 1)

    def body(a_ref, b_ref, o_ref, acc):
        s = pl.program_id(1)

        @pl.when(s == 0)
        def _():
            acc[...] = jnp.zeros_like(acc)

        for k in range(grp):
            acc[k] += _mm_tn(a_ref[k if na > 1 else 0], b_ref[k if nb > 1 else 0])

        @pl.when(s == ns - 1)
        def _():
            o_ref[...] = acc[...].astype(o_ref.dtype)

    res = _pcall(
        comm, body, name=name, grid=(nj // grp, ns),
        in_specs=[pl.BlockSpec((ga, ts, K), (lambda j, s: (j, s, 0)) if na > 1 else (lambda j, s: (0, s, 0))),
                  pl.BlockSpec((gb, ts, N), (lambda j, s: (j, s, 0)) if nb > 1 else (lambda j, s: (0, s, 0)))],
        out_specs=pl.BlockSpec((grp, K, N), lambda j, s: (j, 0, 0)),
        out_shape=jax.ShapeDtypeStruct((nj, K, N), WIRE_DTYPE),
        scratch_shapes=[pltpu.VMEM((grp, K, N), F32)],
        semantics=("parallel", "arbitrary"))(a, b)
    return res[0] if comm is None else (res[0][0], res[1])


def _place():
    x, y, c = lax.axis_index("x"), lax.axis_index("y"), lax.axis_index("c")
    other_chips = [(1 - x, y), (x, 1 - y), (1 - x, 1 - y)]
    return x, y, c, other_chips


def _gather_weights(shards):
    nt = len(shards)

    def body(*refs):
        src, dst = refs[:nt], refs[nt:2 * nt]
        ici_send, ici_recv, d2d_send, d2d_recv, own_send, own_recv = refs[2 * nt:]
        x, y, c, chips = _place()
        mine = 2 * x + y

        def half(t, pc):
            hr = src[t].shape[0] // 2
            return pl.ds(pc * hr, hr)

        def own(t):
            return pltpu.make_async_remote_copy(
                src_ref=src[t], dst_ref=dst[t].at[mine], send_sem=own_send.at[t], recv_sem=own_recv.at[t],
                device_id=(x, y, 1 - c), device_id_type=MESH)

        def ici(t, k, block, to):
            cx, cy = block
            ref = dst[t].at[2 * cx + cy, half(t, c)]
            return pltpu.make_async_remote_copy(
                src_ref=src[t].at[half(t, c)] if to is not None else ref, dst_ref=ref,
                send_sem=ici_send.at[t, k], recv_sem=ici_recv.at[t, k],
                device_id=(*to, c) if to is not None else (x, y, c), device_id_type=MESH)

        def d2d(t, k, block, pc):
            cx, cy = block
            ref = dst[t].at[2 * cx + cy, half(t, pc)]
            return pltpu.make_async_remote_copy(
                src_ref=ref, dst_ref=ref, send_sem=d2d_send.at[t, k], recv_sem=d2d_recv.at[t, k],
                device_id=(x, y, 1 - c), device_id_type=MESH)

        sends = [ici(t, k, (x, y), chip) for t in range(nt) for k, chip in enumerate(chips)]
        sends += [own(t) for t in range(nt)]
        for cp in sends:
            cp.start()
        passed = []
        for t in range(nt):
            for k, chip in enumerate(chips):
                ici(t, k, chip, None).wait_recv()
                fw = d2d(t, k, chip, c)
                fw.start()
                passed.append(fw)
        for t in range(nt):
            own(t).wait_recv()
            for k, chip in enumerate(chips):
                d2d(t, k, chip, 1 - c).wait_recv()
        for cp in sends + passed:
            cp.wait_send()

    return pl.pallas_call(
        body, name="gather_weights",
        in_specs=[ANY] * nt, out_specs=[ANY] * nt,
        out_shape=[jax.ShapeDtypeStruct((N_CHIPS,) + s.shape, s.dtype) for s in shards],
        scratch_shapes=[pltpu.SemaphoreType.DMA((nt, 3))] * 4 + [pltpu.SemaphoreType.DMA((nt,))] * 2,
        compiler_params=pltpu.CompilerParams(has_side_effects=True))(*shards)


def _gather_over_ici(shards):
    nt = len(shards)

    def copies(src, dst, scr, arriving):
        ici_send, ici_recv, own_send, own_recv = scr
        x, y, c, chips = _place()
        out = []
        for t in range(nt):
            hr = src[t].shape[0] // 2
            rows = pl.ds(c * hr, hr)
            for k, (cx, cy) in enumerate(chips):
                block = 2 * cx + cy if arriving else 2 * x + y
                out.append(pltpu.make_async_remote_copy(
                    src_ref=src[t].at[rows], dst_ref=dst[t].at[block, rows],
                    send_sem=ici_send.at[t, k], recv_sem=ici_recv.at[t, k],
                    device_id=(cx, cy, c), device_id_type=MESH))
            out.append(pltpu.make_async_remote_copy(
                src_ref=src[t], dst_ref=dst[t].at[2 * x + y], send_sem=own_send.at[t], recv_sem=own_recv.at[t],
                device_id=(x, y, 1 - c), device_id_type=MESH))
        return out

    def start(src, dst, scr):
        for cp in copies(src, dst, scr, False):
            cp.start()

    def finish(src, dst, scr):
        for cp in copies(src, dst, scr, True):
            cp.wait_recv()
        for cp in copies(src, dst, scr, False):
            cp.wait_send()

    return _Comm(shards, [jax.ShapeDtypeStruct((N_CHIPS,) + s.shape, s.dtype) for s in shards],
                 [pltpu.SemaphoreType.DMA((nt, 3))] * 2 + [pltpu.SemaphoreType.DMA((nt,))] * 2, start, finish)


def _gather_pass_on(bufs):
    nt = len(bufs)

    def passed(dst, scr, t, k, block, pc):
        send, recv = scr
        x, y, c, _ = _place()
        cx, cy = block
        hr = dst[t].shape[1] // 2
        ref = dst[t].at[2 * cx + cy, pl.ds(pc * hr, hr)]
        return pltpu.make_async_remote_copy(src_ref=ref, dst_ref=ref, send_sem=send.at[t, k], recv_sem=recv.at[t, k],
                                            device_id=(x, y, 1 - c), device_id_type=MESH)

    def start(src, dst, scr):
        _, _, c, chips = _place()
        for t in range(nt):
            for k, chip in enumerate(chips):
                passed(dst, scr, t, k, chip, c).start()

    def finish(src, dst, scr):
        _, _, c, chips = _place()
        for t in range(nt):
            for k, chip in enumerate(chips):
                passed(dst, scr, t, k, chip, 1 - c).wait_recv()
        for t in range(nt):
            for k, chip in enumerate(chips):
                passed(dst, scr, t, k, chip, c).wait_send()

    return _Comm(bufs, [jax.ShapeDtypeStruct(b.shape, b.dtype) for b in bufs],
                 [pltpu.SemaphoreType.DMA((nt, 3))] * 2, start, finish, aliases={t: t for t in range(nt)})


def _exchange_halves(grads):
    nt = len(grads)

    def copies(src, dst, scr):
        send, recv = scr
        x, y, c, _ = _place()
        out = []
        for t in range(nt):
            hr = src[t].shape[1] // 2
            out.append(pltpu.make_async_remote_copy(
                src_ref=src[t].at[:, pl.ds((1 - c) * hr, hr)], dst_ref=dst[t],
                send_sem=send.at[t], recv_sem=recv.at[t], device_id=(x, y, 1 - c), device_id_type=MESH))
        return out

    def start(src, dst, scr):
        for cp in copies(src, dst, scr):
            cp.start()

    def finish(src, dst, scr):
        for cp in copies(src, dst, scr):
            cp.wait()

    return _Comm(grads, [jax.ShapeDtypeStruct((g.shape[0], g.shape[1] // 2, g.shape[2]), g.dtype) for g in grads],
                 [pltpu.SemaphoreType.DMA((nt,))] * 2, start, finish)


def _add_halves(grad, other, name):
    nb, R, C = grad.shape
    hr = R // 2
    tr = _tile(hr, 256, 16)
    steps = hr // tr
    c = lax.axis_index("c").astype(jnp.int32).reshape((1,))

    def body(c_ref, a_ref, b_ref, o_ref):
        o_ref[...] = (a_ref[...].astype(F32) + b_ref[...].astype(F32)).astype(o_ref.dtype)

    return pl.pallas_call(
        body, name=name,
        grid_spec=pltpu.PrefetchScalarGridSpec(
            num_scalar_prefetch=1, grid=(nb, steps),
            in_specs=[pl.BlockSpec((1, tr, C), lambda j, i, c_ref: (j, c_ref[0] * steps + i, 0)),
                      pl.BlockSpec((1, tr, C), lambda j, i, c_ref: (j, i, 0))],
            out_specs=pl.BlockSpec((1, tr, C), lambda j, i, c_ref: (j, i, 0))),
        out_shape=jax.ShapeDtypeStruct((nb, hr, C), grad.dtype),
        compiler_params=_params("parallel", "parallel"))(c, grad, other)


def _scatter_chip_sums(parts):
    nt = len(parts)

    def copies(src, dst, scr):
        send, recv = scr
        x, y, c, chips = _place()
        out = []
        for t in range(nt):
            for k, (cx, cy) in enumerate(chips):
                out.append(pltpu.make_async_remote_copy(
                    src_ref=src[t].at[2 * cx + cy], dst_ref=dst[t].at[k],
                    send_sem=send.at[t, k], recv_sem=recv.at[t, k], device_id=(cx, cy, c), device_id_type=MESH))
        return out

    def start(src, dst, scr):
        for cp in copies(src, dst, scr):
            cp.start()

    def finish(src, dst, scr):
        for cp in copies(src, dst, scr):
            cp.wait()

    return _Comm(parts, [jax.ShapeDtypeStruct((3,) + p.shape[1:], p.dtype) for p in parts],
                 [pltpu.SemaphoreType.DMA((nt, 3))] * 2, start, finish)


def _sum_chips(part, recv, name):
    _, hr, C = part.shape
    tr = _tile(hr, 256, 16)
    steps = hr // tr
    where = jnp.stack([2 * lax.axis_index("x") + lax.axis_index("y"), lax.axis_index("c")]).astype(jnp.int32)

    def body(w_ref, a_ref, b_ref, o_ref):
        acc = a_ref[0].astype(F32)
        for k in range(3):
            acc = acc + b_ref[k].astype(F32)
        o_ref[...] = acc

    return pl.pallas_call(
        body, name=name,
        grid_spec=pltpu.PrefetchScalarGridSpec(
            num_scalar_prefetch=1, grid=(steps,),
            in_specs=[pl.BlockSpec((1, tr, C), lambda i, w_ref: (w_ref[0], i, 0)),
                      pl.BlockSpec((3, tr, C), lambda i, w_ref: (0, i, 0))],
            out_specs=pl.BlockSpec((tr, C), lambda i, w_ref: (w_ref[1] * steps + i, 0))),
        out_shape=jax.ShapeDtypeStruct((2 * hr, C), F32),
        compiler_params=_params("parallel"))(where, part, recv)


def _join_halves(bufs):
    nt = len(bufs)

    def swap(dst, scr, t, pc):
        send, recv = scr
        x, y, c, _ = _place()
        hr = dst[t].shape[0] // 2
        rows = dst[t].at[pl.ds(pc * hr, hr)]
        return pltpu.make_async_remote_copy(src_ref=rows, dst_ref=rows, send_sem=send.at[t], recv_sem=recv.at[t],
                                            device_id=(x, y, 1 - c), device_id_type=MESH)

    def start(src, dst, scr):
        c = lax.axis_index("c")
        for t in range(nt):
            swap(dst, scr, t, c).start()

    def finish(src, dst, scr):
        c = lax.axis_index("c")
        for t in range(nt):
            swap(dst, scr, t, 1 - c).wait_recv()
        for t in range(nt):
            swap(dst, scr, t, c).wait_send()

    return _Comm(bufs, [jax.ShapeDtypeStruct(b.shape, b.dtype) for b in bufs],
                 [pltpu.SemaphoreType.DMA((nt,))] * 2, start, finish, aliases={t: t for t in range(nt)})


def _all_reduce_rows(buf, loss_row=None):
    R, L = buf.shape

    def copies(in_ref, gath, send, recv):
        x, y, c, _ = _place()
        out = []
        for k in range(1, N_DEV):
            peer = (x ^ ((k >> 2) & 1), y ^ ((k >> 1) & 1), c ^ (k & 1))
            out.append(pltpu.make_async_remote_copy(
                src_ref=in_ref, dst_ref=gath.at[k], send_sem=send.at[k - 1], recv_sem=recv.at[k - 1],
                device_id=peer, device_id_type=MESH))
        return out

    def start(ins, outs, scr):
        gath, send, recv = scr
        gath[0] = ins[0][...]
        for cp in copies(ins[0], gath, send, recv):
            cp.start()

    def finish(ins, outs, scr):
        gath, send, recv = scr
        for cp in copies(ins[0], gath, send, recv):
            cp.wait()
        x, y, c, _ = _place()
        me = 4 * x + 2 * y + c
        total = gath[me]
        for d in range(1, N_DEV):
            total = total + gath[d ^ me]
        outs[0][...] = total
        if loss_row is not None:
            outs[1][...] = jnp.sum(total[loss_row:loss_row + 1, :], axis=1, keepdims=True)

    out_shape = [jax.ShapeDtypeStruct((R, L), F32)]
    if loss_row is not None:
        out_shape.append(jax.ShapeDtypeStruct((1, 1), F32))
    return _Comm([buf], out_shape,
                 [pltpu.VMEM((N_DEV, R, L), F32), pltpu.SemaphoreType.DMA((N_DEV - 1,)),
                  pltpu.SemaphoreType.DMA((N_DEV - 1,))],
                 start, finish, in_specs=[WHOLE_VMEM], out_specs=[WHOLE_VMEM] * len(out_shape))


def _adamw_update(w_ref, g_ref, m_ref, v_ref, d_ref, nm_ref, nv_ref):
    gv = g_ref[...]
    nm = ADAM_B1 * m_ref[...] + (1.0 - ADAM_B1) * gv
    nv = ADAM_B2 * v_ref[...] + (1.0 - ADAM_B2) * (gv * gv)
    nm_ref[...] = nm
    nv_ref[...] = nv
    m_hat = nm / (1.0 - ADAM_B1 ** ADAM_STEP)
    v_hat = nv / (1.0 - ADAM_B2 ** ADAM_STEP)
    d_ref[...] = -ADAM_LR * (m_hat / (jnp.sqrt(v_hat) + ADAM_EPS) + ADAM_WD * w_ref[...])


def _adamw(w, g, m, v, name):
    R, C = w.shape
    tr = _tile(R, 256)
    blk = pl.BlockSpec((tr, C), lambda i: (i, 0))
    return pl.pallas_call(
        _adamw_update, name=name, grid=(R // tr,), in_specs=[blk] * 4, out_specs=[blk] * 3,
        out_shape=[jax.ShapeDtypeStruct((R, C), F32)] * 3,
        compiler_params=_params("parallel"))(w, g, m, v)


def _adamw_many(ws, gs, ms, vs, name):
    n = len(ws)

    def body(*refs):
        for k in range(n):
            _adamw_update(*[refs[part * n + k] for part in range(7)])

    shapes = [jax.ShapeDtypeStruct(w.shape, F32) for w in ws]
    outs = pl.pallas_call(
        body, name=name, in_specs=[WHOLE_VMEM] * (4 * n), out_specs=[WHOLE_VMEM] * (3 * n), out_shape=shapes * 3,
        compiler_params=pltpu.CompilerParams(vmem_limit_bytes=VMEM_LIMIT_BYTES))(*ws, *gs, *ms, *vs)
    return outs[:n], outs[n:2 * n], outs[2 * n:]


def _pack_rows(arrays):
    rows = []
    for a in arrays:
        flat = a.reshape(-1).astype(F32)
        pad = (-flat.shape[0]) % LANES
        rows.append(jnp.pad(flat, (0, pad)).reshape(-1, LANES))
    buf = jnp.concatenate(rows, axis=0)
    return jnp.pad(buf, ((0, (-buf.shape[0]) % 8), (0, 0)))


def _unpack_rows(buf, shapes):
    out, r = [], 0
    for s in shapes:
        n = math.prod(s)
        nr = -(-n // LANES)
        out.append(buf[r:r + nr].reshape(-1)[:n].reshape(s))
        r += nr
    return out


def _block_diag(w):
    H, a, b = w.shape
    eye = jnp.eye(H, dtype=w.dtype)
    return (eye[:, None, :, None] * w[:, :, None, :]).reshape(H * a, H * b)


def _block_diag_parts(d, H):
    a, b = d.shape[0] // H, d.shape[1] // H
    d4 = d.reshape(H, a, H, b)
    return jnp.stack([d4[h, :, h, :] for h in range(H)])


def _rs_add(names, grads, others):
    return [_add_halves(g, o, "rs_add_halves_" + n) for n, g, o in zip(names, grads, others)]


def _rs_sum(names, parts, recvs):
    return [_sum_chips(p, r, "rs_sum_chips_" + n) for n, p, r in zip(names, parts, recvs)]


def _step(x, mem, target, shards, small, tap_rows, tap_shapes):
    D = x.shape[1]
    nch = N_CHIPS
    p = dict(small)

    (w_in_f,) = _gather_weights([shards['w_in']])
    wf = {}

    def ici(names):
        return _gather_over_ici([shards[n] for n in names])

    ici_a, taps_sum = ici(['w_out', 'w_q']), _all_reduce_rows(tap_rows)
    (z, h1), couts = _fwd_in(x, p['mix_norm_g'], w_in_f, comm=_merge(ici_a, taps_sum))
    bufs_a, (taps,) = _split(couts, ici_a, taps_sum)
    p.update(zip(COL_SHARDED_SMALL, _unpack_rows(taps, tap_shapes)))
    wa_d = _block_diag(p['lru_w_a']).astype(MXU_DTYPE)
    wx_d = _block_diag(p['lru_w_x']).astype(MXU_DTYPE)
    heads = p['lru_w_a'].shape[0]
    pass_a, ici_b = _gather_pass_on(bufs_a), ici(['w_kv', 'w_o'])
    (h, y_lru), couts = _lru_fwd(z, p['lru_conv_w'], p['lru_conv_b'], wa_d, p['lru_b_a'], wx_d, p['lru_b_x'],
                                 p['lru_lambda'], comm=_merge(pass_a, ici_b))
    (wf['w_out'], wf['w_q']), bufs_b = _split(couts, pass_a, ici_b)
    pass_b, ici_c = _gather_pass_on(bufs_b), ici(['w_up'])
    (c1, c3), couts = _conf_fwd(z, p['conf_conv_w'], p['conf_conv_b'], p['conf_ln_g'], p['conf_ln_b'],
                                comm=_merge(pass_b, ici_c))
    (wf['w_kv'], wf['w_o']), bufs_c = _split(couts, pass_b, ici_c)
    w_out2 = wf['w_out'].reshape(2, -1, D)
    w_q = wf['w_q'].reshape(D, D)
    w_o = wf['w_o'].reshape(D, D)
    pass_c, ici_d = _gather_pass_on(bufs_c), ici(['w_down'])
    (x1, h2, q), couts = _fwd_out_q(x, y_lru, c3, w_out2, p['xa_norm_g'], w_q, comm=_merge(pass_c, ici_d))
    (wf['w_up'],), bufs_d = _split(couts, pass_c, ici_d)
    m, kv = _kv_fwd(mem, p['mem_norm_g'], wf['w_kv'])
    (o, x2, h3), (wf['w_down'],) = _attn_fwd(q, kv, x1, w_o, p['ffn_norm_g'], comm=_gather_pass_on(bufs_d))
    gu, act, dx3, loss_lanes, d_final_g = _ffn_fwd(h3, wf['w_up'], p['ffn_conv_w'], p['ffn_conv_b'], wf['w_down'],
                                                   x2, p['final_norm_g'], target)

    dgu, dx2, d_ffn_g, d_ffn_cw, d_ffn_cb = _ffn_bwd(dx3, wf['w_down'], wf['w_up'], gu, x2, p['ffn_norm_g'],
                                                     p['ffn_conv_w'], p['ffn_conv_b'])
    g_down = _wgrad(act, dx3[None], "wgrad_down").reshape(nch, -1, D)
    g_up, other = _wgrad(h3[None], dgu, "wgrad_up", comm=_exchange_halves([g_down]))
    (p_down,) = _rs_add(['w_down'], [g_down], other)
    sc_down, ex_up = _scatter_chip_sums([p_down]), _exchange_halves([g_up])
    (dq, dx1, dkv, d_xa_g), couts = _attn_bwd(dx2, w_o, q, kv, x1, p['xa_norm_g'], w_q, comm=_merge(sc_down, ex_up))
    recv, other = _split(couts, sc_down, ex_up)
    f_down = _rs_sum(['w_down'], [p_down], recv)
    (p_up,) = _rs_add(['w_up'], [g_up], other)
    mid = ['w_o', 'w_q', 'w_kv']
    g_o = _wgrad(o[None], dx2[None], "wgrad_o").reshape(nch, -1, D)
    g_q = _wgrad(h2[None], dq[None], "wgrad_q").reshape(nch, -1, D)
    g_kv, d_mem_g = _kv_bwd(dkv, wf['w_kv'], mem, p['mem_norm_g'], m)
    join_down, sc_up, ex_mid = _join_halves(f_down), _scatter_chip_sums([p_up]), _exchange_halves([g_o, g_q, g_kv])
    (dz_c, d_conf_cw, d_conf_cb, d_ln_g, d_ln_b), couts = _conf_bwd(
        dx1, w_out2[1], z, c1, p['conf_conv_w'], p['conf_ln_g'], p['conf_ln_b'],
        comm=_merge(join_down, sc_up, ex_mid))
    (r_down,), recv, other = _split(couts, join_down, sc_up, ex_mid)
    p_up = [p_up]
    p_mid = _rs_add(mid, [g_o, g_q, g_kv], other)
    join_up, sc_mid = _join_halves(_rs_sum(['w_up'], p_up, recv)), _scatter_chip_sums(p_mid)
    (dz, d_wa, d_wx, d_ba, d_bx, d_lam, d_lru_cw, d_lru_cb), couts = _lru_bwd(
        dx1, w_out2[0], z, h, p['lru_conv_w'], p['lru_conv_b'], wa_d, p['lru_b_a'], wx_d, p['lru_b_x'],
        p['lru_lambda'], dz_c, comm=_merge(join_up, sc_mid))
    (r_up,), recv = _split(couts, join_up, sc_mid)
    f_mid = _rs_sum(mid, p_mid, recv)
    grad_x, d_mix_g = _bwd_in(dz, w_in_f, x, p['mix_norm_g'], dx1)

    small_g = {'mix_norm_g': d_mix_g, 'lru_conv_w': d_lru_cw, 'lru_conv_b': d_lru_cb,
               'lru_w_a': _block_diag_parts(d_wa, heads), 'lru_b_a': d_ba,
               'lru_w_x': _block_diag_parts(d_wx, heads), 'lru_b_x': d_bx, 'lru_lambda': d_lam,
               'conf_conv_w': d_conf_cw, 'conf_conv_b': d_conf_cb, 'conf_ln_g': d_ln_g, 'conf_ln_b': d_ln_b,
               'xa_norm_g': d_xa_g, 'mem_norm_g': d_mem_g, 'ffn_norm_g': d_ffn_g,
               'ffn_conv_w': d_ffn_cw, 'ffn_conv_b': d_ffn_cb, 'final_norm_g': d_final_g}
    names = list(small_g)
    shapes = [small_g[n].shape for n in names]
    join_mid = _join_halves(f_mid)
    small_sum = _all_reduce_rows(_pack_rows([loss_lanes] + [small_g[n] for n in names]), loss_row=0)
    g_in, couts = _wgrad(h1[None], dz, "wgrad_in", comm=_merge(join_mid, small_sum))
    r_mid, (summed, loss) = _split(couts, join_mid, small_sum)
    g_out_l, other = _wgrad(y_lru[None], dx1[None], "wgrad_out_lru", comm=_exchange_halves([g_in]))
    p_in = _rs_add(['w_in'], [g_in], other)
    g_out_c, recv = _wgrad(c3[None], dx1[None], "wgrad_out_conf", comm=_scatter_chip_sums(p_in))
    f_in = _rs_sum(['w_in'], p_in, recv)

    g_out = jnp.concatenate([g_out_l, g_out_c], axis=0).reshape(nch, -1, D)
    p_out = _rs_add(['w_out'], [g_out], _run_comm(_exchange_halves([g_out]), "rs_exchange_last"))
    f_out = _rs_sum(['w_out'], p_out, _run_comm(_scatter_chip_sums(p_out), "rs_scatter_last"))
    r_last = _run_comm(_join_halves(f_out + f_in), "rs_join_last")
    big = dict(zip(['w_down', 'w_up'] + mid + ['w_out', 'w_in'], [r_down, r_up] + r_mid + r_last))
    return grad_x, big, summed, loss, names, [loss_lanes.shape] + shapes


def kernel(x, mem, mix_norm_g, w_in, lru_conv_w, lru_conv_b, lru_w_a, lru_b_a, lru_w_x, lru_b_x, lru_lambda, conf_conv_w, conf_conv_b, conf_ln_g, conf_ln_b, w_out, xa_norm_g, mem_norm_g, w_q, w_kv, w_o, ffn_norm_g, w_up, ffn_conv_w, ffn_conv_b, w_down, final_norm_g, loss_target, m_mix_norm_g, m_w_in, m_lru_conv_w, m_lru_conv_b, m_lru_w_a, m_lru_b_a, m_lru_w_x, m_lru_b_x, m_lru_lambda, m_conf_conv_w, m_conf_conv_b, m_conf_ln_g, m_conf_ln_b, m_w_out, m_xa_norm_g, m_mem_norm_g, m_w_q, m_w_kv, m_w_o, m_ffn_norm_g, m_w_up, m_ffn_conv_w, m_ffn_conv_b, m_w_down, m_final_norm_g, v_mix_norm_g, v_w_in, v_lru_conv_w, v_lru_conv_b, v_lru_w_a, v_lru_b_a, v_lru_w_x, v_lru_b_x, v_lru_lambda, v_conf_conv_w, v_conf_conv_b, v_conf_ln_g, v_conf_ln_b, v_w_out, v_xa_norm_g, v_mem_norm_g, v_w_q, v_w_kv, v_w_o, v_ffn_norm_g, v_w_up, v_ffn_conv_w, v_ffn_conv_b, v_w_down, v_final_norm_g):
    given = dict(locals())
    w = {n: given[n] for n in WEIGHTS}
    mom = {n: given["m_" + n] for n in WEIGHTS}
    var = {n: given["v_" + n] for n in WEIGHTS}
    xi, yi, ci = lax.axis_index("x"), lax.axis_index("y"), lax.axis_index("c")
    chip = 2 * xi + yi

    shards = {n: w[n][0].astype(WIRE_DTYPE) for n in BIG}
    tap_full = []
    for n in COL_SHARDED_SMALL:
        s = w[n][0]
        full = jnp.zeros((s.shape[0], N_CHIPS * s.shape[1]), F32)
        s = jnp.where(ci == 0, s, jnp.zeros_like(s))
        tap_full.append(lax.dynamic_update_slice(full, s, (0, chip * s.shape[1])))
    small = {n: (w[n] if w[n].ndim == 1 else w[n][0]) for n in SMALL if n not in COL_SHARDED_SMALL}
    small = {n: (a.reshape(1, -1) if a.ndim == 1 else a) for n, a in small.items()}

    grad_x, big_g, summed, loss, small_names, packed_shapes = _step(
        x[0], mem[0], loss_target[0], shards, small, _pack_rows(tap_full), [t.shape for t in tap_full])
    small_sum = dict(zip(small_names, _unpack_rows(summed, packed_shapes)[1:]))

    grads = {}
    for n in WEIGHTS:
        if n in BIG:
            g = big_g[n]
        elif n in COL_SHARDED_SMALL:
            width = w[n].shape[-1]
            g = lax.dynamic_slice_in_dim(small_sum[n], chip * width, width, axis=1)
        else:
            g = small_sum[n]
        grads[n] = g.reshape(w[n].shape)

    delta, new_m, new_v = {}, {}, {}
    for n in BIG:
        d, nm, nv = _adamw(w[n][0], grads[n][0], mom[n][0], var[n][0], "adamw_" + n)
        delta[n], new_m[n], new_v[n] = d[None], nm[None], nv[None]
    flat = lambda a: a.reshape(-1, a.shape[-1])
    outs = _adamw_many(*[[flat(src[n]) for n in SMALL] for src in (w, grads, mom, var)], "adamw_small")
    for out, arrays in zip((delta, new_m, new_v), outs):
        out.update({n: a.reshape(w[n].shape) for n, a in zip(SMALL, arrays)})

    return (loss[0, 0], grad_x[None], *[grads[n] for n in WEIGHTS], *[delta[n] for n in WEIGHTS],
            *[new_m[n] for n in WEIGHTS], *[new_v[n] for n in WEIGHTS])
```

```python
import math

import jax
import jax.numpy as jnp
from jax import lax
from jax.experimental import pallas as pl
from jax.experimental.pallas import tpu as pltpu

F32 = jnp.float32
MXU_DTYPE = jnp.bfloat16
WIRE_DTYPE = jnp.bfloat16
EPS = 1e-6
RG_C = 8.0
XA_HEADS = 4
ADAM_LR, ADAM_B1, ADAM_B2, ADAM_EPS, ADAM_WD, ADAM_STEP = 0.001, 0.9, 0.999, 1e-08, 0.01, 10
VMEM_LIMIT_BYTES = 52 * 1024 * 1024
WGRAD_ACC_BYTES = 8 * 1024 * 1024
LANES = 1024
N_CHIPS = 4
N_DEV = 8
MESH = pl.DeviceIdType.MESH
GELU_C = math.sqrt(2.0 / math.pi)
GELU_K = 0.044715

WEIGHTS = ['mix_norm_g', 'w_in', 'lru_conv_w', 'lru_conv_b', 'lru_w_a', 'lru_b_a', 'lru_w_x', 'lru_b_x',
           'lru_lambda', 'conf_conv_w', 'conf_conv_b', 'conf_ln_g', 'conf_ln_b', 'w_out', 'xa_norm_g',
           'mem_norm_g', 'w_q', 'w_kv', 'w_o', 'ffn_norm_g', 'w_up', 'ffn_conv_w', 'ffn_conv_b', 'w_down',
           'final_norm_g']
BIG = ['w_in', 'w_kv', 'w_up', 'w_out', 'w_q', 'w_o', 'w_down']
SMALL = [n for n in WEIGHTS if n not in BIG]
COL_SHARDED_SMALL = ['lru_conv_w', 'conf_conv_w', 'ffn_conv_w']


def _params(*semantics):
    return pltpu.CompilerParams(dimension_semantics=semantics, vmem_limit_bytes=VMEM_LIMIT_BYTES)


ANY = pl.BlockSpec(memory_space=pl.ANY)
WHOLE_VMEM = pl.BlockSpec(memory_space=pltpu.VMEM)


class _Comm:
    def __init__(self, arrays, out_shapes, scratch, start, finish, aliases=None, in_specs=None, out_specs=None):
        self.arrays, self.out_shapes, self.scratch = list(arrays), list(out_shapes), list(scratch)
        self.start, self.finish = start, finish
        self.aliases = dict(aliases or {})
        self.in_specs = list(in_specs) if in_specs is not None else [ANY] * len(self.arrays)
        self.out_specs = list(out_specs) if out_specs is not None else [ANY] * len(self.out_shapes)


def _merge(*comms):
    comms = [c for c in comms if c is not None]
    if not comms:
        return None
    ai = [0]
    for c in comms:
        ai.append(ai[-1] + len(c.arrays))
    oi = [0]
    for c in comms:
        oi.append(oi[-1] + len(c.out_shapes))
    si = [0]
    for c in comms:
        si.append(si[-1] + len(c.scratch))

    def each(which):
        def run(ins, outs, scr):
            for k, c in enumerate(comms):
                getattr(c, which)(ins[ai[k]:ai[k + 1]], outs[oi[k]:oi[k + 1]], scr[si[k]:si[k + 1]])
        return run

    aliases = {ai[k] + i: oi[k] + o for k, c in enumerate(comms) for i, o in c.aliases.items()}
    return _Comm(sum((c.arrays for c in comms), []), sum((c.out_shapes for c in comms), []),
                 sum((c.scratch for c in comms), []), each("start"), each("finish"), aliases,
                 sum((c.in_specs for c in comms), []), sum((c.out_specs for c in comms), []))


def _split(outs, *comms):
    parts, at = [], 0
    for c in comms:
        parts.append(outs[at:at + len(c.out_shapes)])
        at += len(c.out_shapes)
    return parts


def _pcall(comm, body, *, name, grid, in_specs, out_specs, out_shape, semantics, scratch_shapes=(), aliases=None):
    single = not isinstance(out_shape, (list, tuple))
    out_shape = [out_shape] if single else list(out_shape)
    out_specs = [out_specs] if single else list(out_specs)
    in_specs, scratch_shapes = list(in_specs), list(scratch_shapes)
    aliases = dict(aliases or {})

    if comm is None:
        def plain(*args):
            return list(pl.pallas_call(body, name=name, grid=grid, in_specs=in_specs, out_specs=out_specs,
                                       out_shape=out_shape, scratch_shapes=scratch_shapes,
                                       input_output_aliases=aliases,
                                       compiler_params=_params(*semantics))(*args))
        return plain

    def hosted(*args):
        n_in, n_out, n_scr = len(args), len(out_shape), len(scratch_shapes)
        c_in, c_out = len(comm.arrays), len(comm.out_shapes)

        def wrapped(*refs):
            ins, cins = refs[:n_in], refs[n_in:n_in + c_in]
            o0 = n_in + c_in
            outs, couts = refs[o0:o0 + n_out], refs[o0 + n_out:o0 + n_out + c_out]
            s0 = o0 + n_out + c_out
            scr, cscr = refs[s0:s0 + n_scr], refs[s0 + n_scr:]
            first = last = None
            for axis, size in enumerate(grid):
                at_start, at_end = pl.program_id(axis) == 0, pl.program_id(axis) == size - 1
                first = at_start if first is None else first & at_start
                last = at_end if last is None else last & at_end
            if first is None:
                comm.start(cins, couts, cscr)
                body(*ins, *outs, *scr)
                comm.finish(cins, couts, cscr)
                return
            pl.when(first)(lambda: comm.start(cins, couts, cscr))
            body(*ins, *outs, *scr)
            pl.when(last)(lambda: comm.finish(cins, couts, cscr))

        res = pl.pallas_call(
            wrapped, name=name, grid=grid, in_specs=in_specs + comm.in_specs, out_specs=out_specs + comm.out_specs,
            out_shape=out_shape + comm.out_shapes, scratch_shapes=scratch_shapes + comm.scratch,
            input_output_aliases={**aliases, **{n_in + i: n_out + o for i, o in comm.aliases.items()}},
            compiler_params=pltpu.CompilerParams(dimension_semantics=("arbitrary",) * len(grid),
                                                 vmem_limit_bytes=VMEM_LIMIT_BYTES, has_side_effects=True),
        )(*args, *comm.arrays)
        return list(res[:n_out]), list(res[n_out:])

    return hosted


def _run_comm(comm, name):
    return _pcall(comm, lambda: None, name=name, grid=(), in_specs=[], out_specs=[], out_shape=[], semantics=())()[1]


def _tile(n, want, align=8):
    if n <= want:
        return n
    for t in range(want - want % align, 0, -align):
        if n % t == 0:
            return t
    raise ValueError((n, want, align))


def _mm(a, b):
    return jnp.dot(a.astype(MXU_DTYPE), b.astype(MXU_DTYPE), preferred_element_type=F32)


def _mm_nt(a, b):
    return lax.dot_general(a.astype(MXU_DTYPE), b.astype(MXU_DTYPE), (((1,), (1,)), ((), ())),
                           preferred_element_type=F32)


def _mm_tn(a, b):
    return lax.dot_general(a.astype(MXU_DTYPE), b.astype(MXU_DTYPE), (((0,), (0,)), ((), ())),
                           preferred_element_type=F32)


def _sigmoid(v):
    return 0.5 * jnp.tanh(0.5 * v) + 0.5


def _gelu(v):
    v2 = v * v
    t = jnp.tanh(v * (GELU_C + (GELU_C * GELU_K) * v2))
    hv = 0.5 * v
    dt = (1.0 - t * t) * (GELU_C + (3.0 * GELU_C * GELU_K) * v2)
    return hv + hv * t, (0.5 + 0.5 * t) + hv * dt


def _softplus_neg(lam):
    e = jnp.exp(-jnp.abs(lam))
    u = 1.0 + e
    log1p_e = jnp.where(u == 1.0, e, jnp.log(u) * e / jnp.where(u == 1.0, 1.0, u - 1.0))
    return jnp.maximum(-lam, 0.0) + log1p_e


def _rms(xv):
    rinv = lax.rsqrt(jnp.mean(xv * xv, axis=-1, keepdims=True) + EPS)
    return rinv, xv * rinv


def _rms_bwd(rinv, xhat, dxhat):
    return rinv * (dxhat - xhat * jnp.mean(dxhat * xhat, axis=-1, keepdims=True))


def _colsum(v):
    return jnp.sum(v, axis=0, keepdims=True)


def _wrow(w_ref, k, wcols):
    return w_ref[pl.ds(k, 1), :] if wcols is None else w_ref[pl.ds(k, 1), wcols]


def _windows(buf_ref, halo, taps, rows):
    assert taps <= 8 <= halo
    x = buf_ref[pl.ds(halo - 8, rows + 8), :]
    return [x[8:] if s == 0 else pltpu.roll(x, s, 0)[8:] for s in range(taps)]


def _causal_from(xs, w_ref, wcols=None):
    taps = len(xs)
    acc = None
    for s in range(taps):
        term = _wrow(w_ref, taps - 1 - s, wcols) * xs[s]
        acc = term if acc is None else acc + term
    return acc


def _tap_grads_from(dw_ref, dy, xs, wcols=None):
    taps = len(xs)
    for s in range(taps):
        g = _colsum(dy * xs[s])
        if wcols is None:
            dw_ref[pl.ds(taps - 1 - s, 1), :] += g
        else:
            dw_ref[pl.ds(taps - 1 - s, 1), wcols] += g


def _causal_taps(buf_ref, halo, w_ref, taps, rows, wcols=None):
    return _causal_from(_windows(buf_ref, halo, taps, rows), w_ref, wcols)


def _anticausal_taps(buf_ref, w_ref, taps, rows, wcols=None):
    assert taps <= 8
    x = buf_ref[pl.ds(0, rows + 8), :]
    acc = None
    for s in range(taps):
        win = x[:rows] if s == 0 else pltpu.roll(x, rows + 8 - s, 0)[:rows]
        term = _wrow(w_ref, taps - 1 - s, wcols) * win
        acc = term if acc is None else acc + term
    return acc


def _shift_copies(dst_ref, buf_ref, rows, up):
    x = buf_ref[pl.ds(0, rows + 8), :]
    for r in range(8):
        if up:
            dst_ref[r] = x[:rows] if r == 0 else pltpu.roll(x, rows + 8 - r, 0)[:rows]
        else:
            dst_ref[r] = x[8:] if r == 0 else pltpu.roll(x, r, 0)[8:]


def _causal_taps8(sh_ref, halo, w_ref, taps, rows):
    acc = None
    for s in range(taps):
        term = _wrow(w_ref, taps - 1 - s, None) * sh_ref[s % 8, pl.ds(halo - 8 - 8 * (s // 8), rows), :]
        acc = term if acc is None else acc + term
    return acc


def _anticausal_taps8(sh_ref, w_ref, taps, rows):
    acc = None
    for s in range(taps):
        term = _wrow(w_ref, taps - 1 - s, None) * sh_ref[s % 8, pl.ds(8 * (s // 8), rows), :]
        acc = term if acc is None else acc + term
    return acc


def _tap_grads8(dw_ref, dy, sh_ref, halo, taps, rows):
    for s in range(taps):
        dw_ref[pl.ds(taps - 1 - s, 1), :] += _colsum(dy * sh_ref[s % 8, pl.ds(halo - 8 - 8 * (s // 8), rows), :])


def _fwd_in(x, g, w_in, comm=None):
    S, D = x.shape
    nb, _, C = w_in.shape
    ts = _tile(S, 1024)

    halves = 2 if ts % 32 == 0 else 1
    hr = ts // halves

    def body(x_ref, g_ref, w_ref, z_ref, h_ref):
        def norm(k):
            rows = pl.ds(k * hr, hr)
            _, xhat = _rms(x_ref[rows, :])
            h = (xhat * g_ref[...]).astype(MXU_DTYPE)
            h_ref[rows, :] = h
            return h

        h_next = norm(0)
        for k in range(halves):
            rows, h = pl.ds(k * hr, hr), h_next
            if k + 1 < halves:
                h_next = norm(k + 1)
            for j in range(nb):
                z_ref[j, rows, :] = jnp.dot(h, w_ref[j], preferred_element_type=F32)

    return _pcall(
        comm, body, name="fwd_in", grid=(S // ts,),
        in_specs=[pl.BlockSpec((ts, D), lambda i: (i, 0)), pl.BlockSpec((1, D), lambda i: (0, 0)),
                  pl.BlockSpec((nb, D, C), lambda i: (0, 0, 0))],
        out_specs=[pl.BlockSpec((nb, ts, C), lambda i: (0, i, 0)), pl.BlockSpec((ts, D), lambda i: (i, 0))],
        out_shape=[jax.ShapeDtypeStruct((nb, S, C), F32), jax.ShapeDtypeStruct((S, D), MXU_DTYPE)],
        semantics=("parallel",))(x, g, w_in)


def _lru_gates(xc, wa_ref, ba_ref, wx_ref, bx_ref, sp):
    xb = xc.astype(MXU_DTYPE)
    r = _sigmoid(jnp.dot(xb, wa_ref[...], preferred_element_type=F32) + ba_ref[...])
    ig = _sigmoid(jnp.dot(xb, wx_ref[...], preferred_element_type=F32) + bx_ref[...])
    log_a = -RG_C * r * sp
    a = jnp.exp(log_a)
    one_minus_a2 = jnp.tanh(-log_a) * (a * a + 1.0)
    inv_mult = lax.rsqrt(one_minus_a2)
    mult = jnp.where(one_minus_a2 > 0.0, one_minus_a2 * inv_mult, 0.0)
    return r, ig, a, mult, inv_mult


def _lru_fwd(z, conv_w, conv_b, wa, ba, wx, bx, lam, comm=None):
    _, S, C = z.shape
    ts = _tile(S, 256)
    taps = conv_w.shape[0]
    halo = 8

    def body(zx_ref, zg_ref, cw_ref, cb_ref, wa_ref, ba_ref, wx_ref, bx_ref, lam_ref,
             h_ref, y_ref, xbuf, a_s, u_s, hc):
        i = pl.program_id(0)

        @pl.when(i == 0)
        def _():
            xbuf[pl.ds(0, halo), :] = jnp.zeros((halo, C), F32)
            hc[...] = jnp.zeros_like(hc)

        xbuf[pl.ds(halo, ts), :] = zx_ref[0]
        xc = _causal_taps(xbuf, halo, cw_ref, taps, ts) + cb_ref[...]
        sp = _softplus_neg(lam_ref[...])
        _, ig, a, mult, _ = _lru_gates(xc, wa_ref, ba_ref, wx_ref, bx_ref, sp)
        a_s[...] = a
        u_s[...] = mult * (ig * xc)
        row = lax.broadcasted_iota(jnp.int32, (8, C), 0)

        def step(k, carry):
            off = pl.multiple_of(k * 8, 8)
            av = a_s[pl.ds(off, 8), :]
            uv = u_s[pl.ds(off, 8), :]
            for d in (1, 2, 4):
                m = row >= d
                a_sh = jnp.where(m, pltpu.roll(av, d, 0), 1.0)
                u_sh = jnp.where(m, pltpu.roll(uv, d, 0), 0.0)
                uv = uv + av * u_sh
                av = av * a_sh
            hv = uv + av * carry
            h_ref[pl.ds(off, 8), :] = hv
            return jnp.broadcast_to(hv[7:8, :], (8, C))

        hc[...] = lax.fori_loop(0, ts // 8, step, hc[...], unroll=4)
        ge, _ = _gelu(zg_ref[0])
        y_ref[...] = (h_ref[...] * ge).astype(MXU_DTYPE)
        xbuf[pl.ds(0, halo), :] = xbuf[pl.ds(ts, halo), :]

    vec = pl.BlockSpec((1, C), lambda i: (0, 0))
    mat = pl.BlockSpec((C, C), lambda i: (0, 0))
    return _pcall(
        comm, body, name="lru_fwd", grid=(S // ts,),
        in_specs=[pl.BlockSpec((1, ts, C), lambda i: (0, i, 0)), pl.BlockSpec((1, ts, C), lambda i: (1, i, 0)),
                  pl.BlockSpec((taps, C), lambda i: (0, 0)), vec, mat, vec, mat, vec, vec],
        out_specs=[pl.BlockSpec((ts, C), lambda i: (i, 0)), pl.BlockSpec((ts, C), lambda i: (i, 0))],
        out_shape=[jax.ShapeDtypeStruct((S, C), F32), jax.ShapeDtypeStruct((S, C), MXU_DTYPE)],
        scratch_shapes=[pltpu.VMEM((ts + halo, C), F32), pltpu.VMEM((ts, C), F32), pltpu.VMEM((ts, C), F32),
                        pltpu.VMEM((8, C), F32)],
        semantics=("arbitrary",))(z, z, conv_w, conv_b, wa, ba, wx, bx, lam)


def _layer_norm_stats(c1):
    mu = jnp.mean(c1, axis=-1, keepdims=True)
    xc = c1 - mu
    rstd = lax.rsqrt(jnp.mean(xc * xc, axis=-1, keepdims=True) + EPS)
    return rstd, xc * rstd


def _conf_fwd(z, conv_w, conv_b, ln_g, ln_b, comm=None):
    _, S, C = z.shape
    ts = _tile(S, 256)
    taps = conv_w.shape[0]
    halo = 32

    def body(za_ref, zb_ref, cw_ref, cb_ref, g_ref, b_ref, c1_ref, c3_ref, cbuf, shifted):
        i = pl.program_id(0)

        @pl.when(i == 0)
        def _():
            cbuf[pl.ds(0, halo), :] = jnp.zeros((halo, C), F32)

        cbuf[pl.ds(halo, ts), :] = za_ref[0] * _sigmoid(zb_ref[0])
        _shift_copies(shifted, cbuf, ts + halo - 8, up=False)
        c1 = _causal_taps8(shifted, halo, cw_ref, taps, ts) + cb_ref[...]
        c1_ref[...] = c1
        _, xhat = _layer_norm_stats(c1)
        c2 = xhat * g_ref[...] + b_ref[...]
        c3_ref[...] = (c2 * _sigmoid(c2)).astype(MXU_DTYPE)
        cbuf[pl.ds(0, halo), :] = cbuf[pl.ds(ts, halo), :]

    vec = pl.BlockSpec((1, C), lambda i: (0, 0))
    return _pcall(
        comm, body, name="conf_fwd", grid=(S // ts,),
        in_specs=[pl.BlockSpec((1, ts, C), lambda i: (2, i, 0)), pl.BlockSpec((1, ts, C), lambda i: (3, i, 0)),
                  pl.BlockSpec((taps, C), lambda i: (0, 0)), vec, vec, vec],
        out_specs=[pl.BlockSpec((ts, C), lambda i: (i, 0)), pl.BlockSpec((ts, C), lambda i: (i, 0))],
        out_shape=[jax.ShapeDtypeStruct((S, C), F32), jax.ShapeDtypeStruct((S, C), MXU_DTYPE)],
        scratch_shapes=[pltpu.VMEM((ts + halo, C), F32), pltpu.VMEM((8, ts + halo - 8, C), F32)],
        semantics=("arbitrary",))(z, z, conv_w, conv_b, ln_g, ln_b)


def _fwd_out_q(x, y_lru, c3, w_out, g_xa, w_q, comm=None):
    S, D = x.shape
    C = y_lru.shape[1]
    ts = _tile(S, 1024)

    halves = 2 if ts % 32 == 0 else 1
    hr = ts // halves

    def body(x_ref, yl_ref, c3_ref, wo_ref, g_ref, wq_ref, x1_ref, h2_ref, q_ref):
        def mixed(k):
            rows = pl.ds(k * hr, hr)
            return (jnp.dot(yl_ref[rows, :], wo_ref[0], preferred_element_type=F32)
                    + jnp.dot(c3_ref[rows, :], wo_ref[1], preferred_element_type=F32))

        y_next = mixed(0)
        for k in range(halves):
            rows, y = pl.ds(k * hr, hr), y_next
            if k + 1 < halves:
                y_next = mixed(k + 1)
            x1 = x_ref[rows, :] + y
            x1_ref[rows, :] = x1
            _, xhat = _rms(x1)
            h2 = (xhat * g_ref[...]).astype(MXU_DTYPE)
            h2_ref[rows, :] = h2
            q_ref[rows, :] = jnp.dot(h2, wq_ref[...], preferred_element_type=F32).astype(MXU_DTYPE)

    row = lambda w: pl.BlockSpec((ts, w), lambda i: (i, 0))
    return _pcall(
        comm, body, name="fwd_out_q", grid=(S // ts,),
        in_specs=[row(D), row(C), row(C), pl.BlockSpec((2, C, D), lambda i: (0, 0, 0)),
                  pl.BlockSpec((1, D), lambda i: (0, 0)), pl.BlockSpec((D, D), lambda i: (0, 0))],
        out_specs=[row(D), row(D), row(D)],
        out_shape=[jax.ShapeDtypeStruct((S, D), F32), jax.ShapeDtypeStruct((S, D), MXU_DTYPE),
                   jax.ShapeDtypeStruct((S, D), MXU_DTYPE)],
        semantics=("parallel",))(x, y_lru, c3, w_out, g_xa, w_q)


def _kv_fwd(mem, g, w_kv):
    M, D = mem.shape
    nb, _, C = w_kv.shape

    def body(mem_ref, g_ref, w_ref, m_ref, kv_ref):
        _, xhat = _rms(mem_ref[...])
        m = (xhat * g_ref[...]).astype(MXU_DTYPE)
        m_ref[...] = m
        for j in range(nb):
            kv_ref[:, pl.ds(j * C, C)] = jnp.dot(m, w_ref[j], preferred_element_type=F32).astype(MXU_DTYPE)

    return pl.pallas_call(
        body, name="kv_fwd", grid=(1,),
        in_specs=[pl.BlockSpec((M, D), lambda i: (0, 0)), pl.BlockSpec((1, D), lambda i: (0, 0)),
                  pl.BlockSpec((nb, D, C), lambda i: (0, 0, 0))],
        out_specs=[pl.BlockSpec((M, D), lambda i: (0, 0)), pl.BlockSpec((M, nb * C), lambda i: (0, 0))],
        out_shape=[jax.ShapeDtypeStruct((M, D), MXU_DTYPE), jax.ShapeDtypeStruct((M, nb * C), MXU_DTYPE)],
        compiler_params=_params("arbitrary"))(mem, g, w_kv)


def _softmax_rows(s):
    e = jnp.exp(s - jnp.max(s, axis=-1, keepdims=True))
    return e / jnp.sum(e, axis=-1, keepdims=True)


def _attn_fwd(q, kv, x1, w_o, g_ffn, comm=None):
    S, D = x1.shape
    M = kv.shape[0]
    hd = D // XA_HEADS
    scale = hd ** -0.5
    ts = _tile(S, 1024)

    def body(q_ref, kv_ref, x1_ref, wo_ref, g_ref, o_ref, x2_ref, h3_ref):
        def scores(h):
            cols = pl.ds(h * hd, hd)
            return _mm_nt(q_ref[:, cols], kv_ref[:, cols]) * scale

        s_next = scores(0)
        for h in range(XA_HEADS):
            s = s_next
            if h + 1 < XA_HEADS:
                s_next = scores(h + 1)
            p = _softmax_rows(s)
            o_ref[:, pl.ds(h * hd, hd)] = _mm(p, kv_ref[:, pl.ds(D + h * hd, hd)]).astype(MXU_DTYPE)
        x2 = x1_ref[...] + jnp.dot(o_ref[...], wo_ref[...], preferred_element_type=F32)
        x2_ref[...] = x2
        _, xhat = _rms(x2)
        h3_ref[...] = (xhat * g_ref[...]).astype(MXU_DTYPE)

    row = pl.BlockSpec((ts, D), lambda i: (i, 0))
    return _pcall(
        comm, body, name="attn_fwd", grid=(S // ts,),
        in_specs=[row, pl.BlockSpec((M, 2 * D), lambda i: (0, 0)), row, pl.BlockSpec((D, D), lambda i: (0, 0)),
                  pl.BlockSpec((1, D), lambda i: (0, 0))],
        out_specs=[row, row, row],
        out_shape=[jax.ShapeDtypeStruct((S, D), MXU_DTYPE), jax.ShapeDtypeStruct((S, D), F32),
                   jax.ShapeDtypeStruct((S, D), MXU_DTYPE)],
        semantics=("parallel",))(q, kv, x1, w_o, g_ffn)


def _ffn_fwd(h3, w_up, conv_w, conv_b, w_down, x2, g_final, target, comm=None):
    S, D = h3.shape
    nb, _, CW = w_up.shape
    half = nb // 2
    cb = 768
    per = CW // cb
    J = half * per
    ts = _tile(S, 256)
    taps = conv_w.shape[0]
    halo = 8

    def body(h_ref, wup_ref, cw_ref, cb_ref, wd_ref, x2_ref, gf_ref, t_ref,
             gu_ref, act_ref, dx3_ref, loss_ref, dgf_ref, gbuf):
        i = pl.program_id(0)

        @pl.when(i == 0)
        def _():
            for ref in (loss_ref, dgf_ref, gbuf):
                ref[...] = jnp.zeros_like(ref)

        hv = h_ref[...]
        x3 = x2_ref[...]
        def up(j):
            b, cols = j // per, pl.ds((j % per) * cb, cb)
            return (jnp.dot(hv, wup_ref[b, :, cols], preferred_element_type=F32),
                    jnp.dot(hv, wup_ref[half + b, :, cols], preferred_element_type=F32))

        ahead = up(0)
        for j in range(J):
            b, cols, wcols = j // per, pl.ds((j % per) * cb, cb), pl.ds(j * cb, cb)
            g, u = ahead
            if j + 1 < J:
                ahead = up(j + 1)
            gu_ref[0, b, :, cols] = g
            gu_ref[1, b, :, cols] = u
            gbuf[j, pl.ds(halo, ts), :] = g
            gc = _causal_taps(gbuf.at[j], halo, cw_ref, taps, ts, wcols=wcols) + cb_ref[:, wcols]
            gbuf[j, pl.ds(0, halo), :] = gbuf[j, pl.ds(ts, halo), :]
            ge, _ = _gelu(gc)
            act = (ge * u).astype(MXU_DTYPE)
            act_ref[j] = act
            x3 = x3 + jnp.dot(act, wd_ref[j], preferred_element_type=F32)
        rinv, xhat = _rms(x3)
        gf = gf_ref[...]
        diff = xhat * gf - t_ref[...]
        loss_ref[...] += _colsum(diff * diff) * (0.5 / D)
        dy = diff * (1.0 / D)
        dgf_ref[...] += _colsum(dy * xhat)
        dx3_ref[...] = _rms_bwd(rinv, xhat, dy * gf)

    row = pl.BlockSpec((ts, D), lambda i: (i, 0))
    vecd = pl.BlockSpec((1, D), lambda i: (0, 0))
    once = pl.Buffered(1)
    sds = jax.ShapeDtypeStruct
    res = _pcall(
        comm, body, name="ffn_fwd", grid=(S // ts,),
        in_specs=[row, pl.BlockSpec((nb, D, CW), lambda i: (0, 0, 0), pipeline_mode=once),
                  pl.BlockSpec((taps, half * CW), lambda i: (0, 0)), pl.BlockSpec((1, half * CW), lambda i: (0, 0)),
                  pl.BlockSpec((J, cb, D), lambda i: (0, 0, 0), pipeline_mode=once), row, vecd, row],
        out_specs=[pl.BlockSpec((2, half, ts, CW), lambda i: (0, 0, i, 0)),
                   pl.BlockSpec((J, ts, cb), lambda i: (0, i, 0)), row, vecd, vecd],
        out_shape=[sds((2, half, S, CW), F32), sds((J, S, cb), MXU_DTYPE), sds((S, D), F32),
                   sds((1, D), F32), sds((1, D), F32)],
        scratch_shapes=[pltpu.VMEM((J, ts + halo, cb), F32)],
        semantics=("arbitrary",))(h3, w_up, conv_w, conv_b, w_down.reshape(J, cb, D), x2, g_final, target)
    outs = res if comm is None else res[0]
    outs = [outs[0].reshape(nb, S, CW)] + list(outs[1:])
    return outs if comm is None else (outs, res[1])


def _ffn_bwd(dx3, w_down, w_up, gu, x2, g_ffn, conv_w, conv_b, comm=None):
    nb, S, CW = gu.shape
    half = nb // 2
    D = dx3.shape[1]
    cb = 768
    per = CW // cb
    J = half * per
    ts = _tile(S, 256)
    n = S // ts
    taps = conv_w.shape[0]
    halo = 8
    hb = ts // halo

    def body(dx_ref, x2_ref, gf_ref, wd_ref, wup_ref, gu_ref, gh_ref, cw_ref, cb_ref,
             dgu_ref, dx2_ref, dgf_ref, dcw_ref, dcb_ref, gbuf, dbuf):
        i = pl.program_id(0)
        r = n - 1 - i

        @pl.when(i == 0)
        def _():
            for ref in (dgf_ref, dcw_ref, dcb_ref, dbuf):
                ref[...] = jnp.zeros_like(ref)

        dx3v = dx_ref[...]
        dxb = dx3v.astype(MXU_DTYPE)
        dacts = [_mm_nt(dxb, wd_ref[j]) for j in range(J)]
        dh = None
        for j in range(J):
            b, cols, wcols = j // per, pl.ds((j % per) * cb, cb), pl.ds(j * cb, cb)
            dact = dacts[j]
            gbuf[pl.ds(0, halo), :] = jnp.where(r > 0, gh_ref[0, b, :, cols], 0.0)
            gbuf[pl.ds(halo, ts), :] = gu_ref[0, b, :, cols]
            gs = _windows(gbuf, halo, taps, ts)
            gc = _causal_from(gs, cw_ref, wcols) + cb_ref[:, wcols]
            ge, dge = _gelu(gc)
            dub = (dact * ge).astype(MXU_DTYPE)
            dgc = dact * gu_ref[1, b, :, cols] * dge
            dcb_ref[:, wcols] += _colsum(dgc)
            dbuf[j, pl.ds(0, ts), :] = dgc
            _tap_grads_from(dcw_ref, dgc, gs, wcols)
            dgb = _anticausal_taps(dbuf.at[j], cw_ref, taps, ts, wcols=wcols).astype(MXU_DTYPE)
            dbuf[j, pl.ds(ts, halo), :] = dbuf[j, pl.ds(0, halo), :]
            dgu_ref[0, b, :, cols] = dgb
            dgu_ref[1, b, :, cols] = dub
            part = _mm_nt(dgb, wup_ref[b, :, cols]) + _mm_nt(dub, wup_ref[half + b, :, cols])
            dh = part if dh is None else dh + part
        rinv, xhat = _rms(x2_ref[...])
        dgf_ref[...] += _colsum(dh * xhat)
        dx2_ref[...] = dx3v + _rms_bwd(rinv, xhat, dh * gf_ref[...])

    gu2 = gu.reshape(2, half, S, CW)
    row = pl.BlockSpec((ts, D), lambda i: (n - 1 - i, 0))
    vecd = pl.BlockSpec((1, D), lambda i: (0, 0))
    pair = pl.BlockSpec((2, half, ts, CW), lambda i: (0, 0, n - 1 - i, 0))
    g_prev = pl.BlockSpec((1, half, halo, CW), lambda i: (0, 0, jnp.maximum((n - 1 - i) * hb - 1, 0), 0))
    tapw = pl.BlockSpec((taps, half * CW), lambda i: (0, 0))
    vec = pl.BlockSpec((1, half * CW), lambda i: (0, 0))
    once = pl.Buffered(1)
    sds = jax.ShapeDtypeStruct
    res = _pcall(
        comm, body, name="ffn_bwd", grid=(n,),
        in_specs=[row, row, vecd, pl.BlockSpec((J, cb, D), lambda i: (0, 0, 0), pipeline_mode=once),
                  pl.BlockSpec((nb, D, CW), lambda i: (0, 0, 0), pipeline_mode=once), pair, g_prev, tapw, vec],
        out_specs=[pair, row, vecd, tapw, vec],
        out_shape=[sds((2, half, S, CW), MXU_DTYPE), sds((S, D), F32), sds((1, D), F32),
                   sds((taps, half * CW), F32), sds((1, half * CW), F32)],
        scratch_shapes=[pltpu.VMEM((ts + halo, cb), F32), pltpu.VMEM((J, ts + halo, cb), F32)],
        semantics=("arbitrary",))(dx3, x2, g_ffn, w_down.reshape(J, cb, D), w_up, gu2, gu2, conv_w, conv_b)
    outs = res if comm is None else res[0]
    outs = [outs[0].reshape(nb, S, CW)] + list(outs[1:])
    return outs if comm is None else (outs, res[1])


def _attn_bwd(dx2, w_o, q, kv, x1, g_xa, w_q, comm=None):
    S, D = x1.shape
    M = kv.shape[0]
    hd = D // XA_HEADS
    scale = hd ** -0.5
    ts = _tile(S, 1024)

    def body(dx2_ref, wo_ref, q_ref, kv_ref, x1_ref, g_ref, wq_ref, dq_ref, dx1_ref, dkv_ref, dg_ref):
        i = pl.program_id(0)

        @pl.when(i == 0)
        def _():
            dkv_ref[...] = jnp.zeros_like(dkv_ref)
            dg_ref[...] = jnp.zeros_like(dg_ref)

        dx2 = dx2_ref[...]
        do = _mm_nt(dx2, wo_ref[...]).astype(MXU_DTYPE)
        def scores(h):
            cols = pl.ds(h * hd, hd)
            doh = do[:, h * hd:(h + 1) * hd]
            return (_mm_nt(q_ref[:, cols], kv_ref[:, cols]) * scale,
                    _mm_nt(doh, kv_ref[:, pl.ds(D + h * hd, hd)]), doh)

        ahead = scores(0)
        for h in range(XA_HEADS):
            cols = pl.ds(h * hd, hd)
            vcols = pl.ds(D + h * hd, hd)
            s, dp, doh = ahead
            if h + 1 < XA_HEADS:
                ahead = scores(h + 1)
            p = _softmax_rows(s)
            ds = (p * (dp - jnp.sum(dp * p, axis=-1, keepdims=True)) * scale).astype(MXU_DTYPE)
            dkv_ref[:, vcols] += _mm_tn(p, doh)
            dq_ref[:, cols] = _mm(ds, kv_ref[:, cols]).astype(MXU_DTYPE)
            dkv_ref[:, cols] += _mm_tn(ds, q_ref[:, cols])
        dh2 = _mm_nt(dq_ref[...], wq_ref[...])
        rinv, xhat = _rms(x1_ref[...])
        dg_ref[...] += _colsum(dh2 * xhat)
        dx1_ref[...] = dx2 + _rms_bwd(rinv, xhat, dh2 * g_ref[...])

    row = pl.BlockSpec((ts, D), lambda i: (i, 0))
    mat = pl.BlockSpec((D, D), lambda i: (0, 0))
    vecd = pl.BlockSpec((1, D), lambda i: (0, 0))
    kvs = pl.BlockSpec((M, 2 * D), lambda i: (0, 0))
    return _pcall(
        comm, body, name="attn_bwd", grid=(S // ts,),
        in_specs=[row, mat, row, kvs, row, vecd, mat],
        out_specs=[row, row, kvs, vecd],
        out_shape=[jax.ShapeDtypeStruct((S, D), MXU_DTYPE), jax.ShapeDtypeStruct((S, D), F32),
                   jax.ShapeDtypeStruct((M, 2 * D), F32), jax.ShapeDtypeStruct((1, D), F32)],
        semantics=("arbitrary",))(dx2, w_o, q, kv, x1, g_xa, w_q)


def _kv_bwd(dkv, w_kv, mem, g, m):
    M, D = mem.shape
    nb, _, C = w_kv.shape

    def body(dkv_ref, w_ref, mem_ref, m_ref, dw_ref, dg_ref):
        dm = jnp.zeros((M, D), F32)
        for j in range(nb):
            dj = dkv_ref[:, pl.ds(j * C, C)].astype(MXU_DTYPE)
            dw_ref[j] = _mm_tn(m_ref[...], dj).astype(dw_ref.dtype)
            dm = dm + _mm_nt(dj, w_ref[j])
        _, xhat = _rms(mem_ref[...])
        dg_ref[...] = _colsum(dm * xhat)

    full = lambda *s: pl.BlockSpec(s, lambda i: (0,) * len(s))
    return pl.pallas_call(
        body, name="kv_bwd", grid=(1,),
        in_specs=[full(M, nb * C), full(nb, D, C), full(M, D), full(M, D)],
        out_specs=[full(nb, D, C), full(1, D)],
        out_shape=[jax.ShapeDtypeStruct((nb, D, C), WIRE_DTYPE), jax.ShapeDtypeStruct((1, D), F32)],
        compiler_params=_params("arbitrary"))(dkv, w_kv, mem, m)


def _conf_bwd(dx1, w_out_c, z, c1, conv_w, ln_g, ln_b, comm=None):
    _, S, C = z.shape
    D = dx1.shape[1]
    ts = _tile(S, 256)
    n = S // ts
    taps = conv_w.shape[0]
    halo = 32
    hb = ts // halo

    def body(dx_ref, wo_ref, za_ref, zb_ref, zah_ref, zbh_ref, c1_ref, cw_ref, g_ref, b_ref,
             dz_ref, dcw_ref, dcb_ref, dlg_ref, dlb_ref, c0buf, dbuf, shifted):
        i = pl.program_id(0)
        r = n - 1 - i

        @pl.when(i == 0)
        def _():
            for ref in (dcw_ref, dcb_ref, dlg_ref, dlb_ref):
                ref[...] = jnp.zeros_like(ref)
            dbuf[pl.ds(ts, halo), :] = jnp.zeros((halo, C), F32)

        za = za_ref[0]
        sb = _sigmoid(zb_ref[0])
        c0buf[pl.ds(0, halo), :] = jnp.where(r > 0, zah_ref[0] * _sigmoid(zbh_ref[0]), 0.0)
        c0buf[pl.ds(halo, ts), :] = za * sb
        dc3 = _mm_nt(dx_ref[...], wo_ref[...])
        rstd, xhat = _layer_norm_stats(c1_ref[...])
        g = g_ref[...]
        c2 = xhat * g + b_ref[...]
        sg = _sigmoid(c2)
        dc2 = dc3 * sg * (1.0 + c2 * (1.0 - sg))
        dlg_ref[...] += _colsum(dc2 * xhat)
        dlb_ref[...] += _colsum(dc2)
        dxh = dc2 * g
        dc1 = rstd * (dxh - jnp.mean(dxh, axis=-1, keepdims=True)
                      - xhat * jnp.mean(dxh * xhat, axis=-1, keepdims=True))
        dcb_ref[...] += _colsum(dc1)
        dbuf[pl.ds(0, ts), :] = dc1
        _shift_copies(shifted, c0buf, ts + halo - 8, up=False)
        _tap_grads8(dcw_ref, dc1, shifted, halo, taps, ts)
        _shift_copies(shifted, dbuf, ts + halo - 8, up=True)
        dc0 = _anticausal_taps8(shifted, cw_ref, taps, ts)
        dz_ref[0] = (dc0 * sb).astype(MXU_DTYPE)
        dz_ref[1] = (dc0 * za * sb * (1.0 - sb)).astype(MXU_DTYPE)
        dbuf[pl.ds(ts, halo), :] = dbuf[pl.ds(0, halo), :]

    vec = pl.BlockSpec((1, C), lambda i: (0, 0))
    tapw = pl.BlockSpec((taps, C), lambda i: (0, 0))
    tile = lambda b: pl.BlockSpec((1, ts, C), lambda i: (b, n - 1 - i, 0))
    prev = lambda b: pl.BlockSpec((1, halo, C), lambda i: (b, jnp.maximum((n - 1 - i) * hb - 1, 0), 0))
    return _pcall(
        comm, body, name="conf_bwd", grid=(n,),
        in_specs=[pl.BlockSpec((ts, D), lambda i: (n - 1 - i, 0)), pl.BlockSpec((C, D), lambda i: (0, 0)),
                  tile(2), tile(3), prev(2), prev(3), pl.BlockSpec((ts, C), lambda i: (n - 1 - i, 0)),
                  tapw, vec, vec],
        out_specs=[pl.BlockSpec((2, ts, C), lambda i: (1, n - 1 - i, 0)), tapw, vec, vec, vec],
        out_shape=[jax.ShapeDtypeStruct((4, S, C), MXU_DTYPE), jax.ShapeDtypeStruct((taps, C), F32),
                   jax.ShapeDtypeStruct((1, C), F32), jax.ShapeDtypeStruct((1, C), F32),
                   jax.ShapeDtypeStruct((1, C), F32)],
        scratch_shapes=[pltpu.VMEM((ts + halo, C), F32), pltpu.VMEM((ts + halo, C), F32),
                        pltpu.VMEM((8, ts + halo - 8, C), F32)],
        semantics=("arbitrary",))(dx1, w_out_c, z, z, z, z, c1, conv_w, ln_g, ln_b)


def _lru_bwd(dx1, w_out_l, z, h, conv_w, conv_b, wa, ba, wx, bx, lam, dz, comm=None):
    _, S, C = z.shape
    D = dx1.shape[1]
    ts = _tile(S, 256)
    n = S // ts
    taps = conv_w.shape[0]
    halo = 8
    hb = ts // halo

    def body(dx_ref, wo_ref, zx_ref, zxh_ref, zg_ref, h_ref, hh_ref, cw_ref, cb_ref, wa_ref, ba_ref,
             wx_ref, bx_ref, lam_ref, dz_in,
             dz_ref, dwa_ref, dwx_ref, dba_ref, dbx_ref, dlam_ref, dcw_ref, dcb_ref,
             xbuf, hbuf, a_s, w_s, dh_s, g_s, dbuf, pc):
        i = pl.program_id(0)
        r = n - 1 - i

        @pl.when(i == 0)
        def _():
            for ref in (dwa_ref, dwx_ref, dba_ref, dbx_ref, dlam_ref, dcw_ref, dcb_ref, pc):
                ref[...] = jnp.zeros_like(ref)
            dbuf[pl.ds(ts, halo), :] = jnp.zeros((halo, C), F32)

        xbuf[pl.ds(0, halo), :] = jnp.where(r > 0, zxh_ref[0], 0.0)
        xbuf[pl.ds(halo, ts), :] = zx_ref[0]
        hbuf[pl.ds(0, halo), :] = jnp.where(r > 0, hh_ref[...], 0.0)
        hbuf[pl.ds(halo, ts), :] = h_ref[...]
        xs = _windows(xbuf, halo, taps, ts)
        xc = _causal_from(xs, cw_ref) + cb_ref[...]
        lam_v = lam_ref[...]
        sp = _softplus_neg(lam_v)
        rg, ig, a, mult, inv_mult = _lru_gates(xc, wa_ref, ba_ref, wx_ref, bx_ref, sp)

        dy = _mm_nt(dx_ref[...], wo_ref[...])
        ge, dge = _gelu(zg_ref[0])
        dh = dy * ge
        dz_ref[1] = (dy * h_ref[...] * dge).astype(MXU_DTYPE)
        a_s[...] = a
        w_s[...] = a * dh
        dh_s[...] = dh
        row = lax.broadcasted_iota(jnp.int32, (8, C), 0)

        def step(kk, carry):
            off = pl.multiple_of((ts // 8 - 1 - kk) * 8, 8)
            av = a_s[pl.ds(off, 8), :]
            wv = w_s[pl.ds(off, 8), :]
            for d in (1, 2, 4):
                m = row < 8 - d
                a_sh = jnp.where(m, pltpu.roll(av, 8 - d, 0), 1.0)
                w_sh = jnp.where(m, pltpu.roll(wv, 8 - d, 0), 0.0)
                wv = wv + av * w_sh
                av = av * a_sh
            pv = wv + av * carry
            g_s[pl.ds(off, 8), :] = dh_s[pl.ds(off, 8), :] + jnp.where(row < 7, pltpu.roll(pv, 7, 0), carry)
            return jnp.broadcast_to(pv[0:1, :], (8, C))

        pc[...] = lax.fori_loop(0, ts // 8, step, pc[...], unroll=4)
        gt = g_s[...]
        da = gt * hbuf[pl.ds(halo - 1, ts), :]
        gm = gt * mult
        dlog_a = da * a - (gt * ig * xc) * (a * a) * inv_mult
        dlam_ref[...] += _colsum(dlog_a * rg) * (RG_C / (1.0 + jnp.exp(lam_v)))
        dpa = (dlog_a * (-RG_C * sp)) * rg * (1.0 - rg)
        dpx = (gm * xc) * ig * (1.0 - ig)
        dba_ref[...] += _colsum(dpa)
        dbx_ref[...] += _colsum(dpx)
        xb = xc.astype(MXU_DTYPE)
        dpab, dpxb = dpa.astype(MXU_DTYPE), dpx.astype(MXU_DTYPE)
        dwa_ref[...] += _mm_tn(xb, dpab)
        dwx_ref[...] += _mm_tn(xb, dpxb)
        dxc = gm * ig + _mm_nt(dpab, wa_ref[...]) + _mm_nt(dpxb, wx_ref[...])
        dcb_ref[...] += _colsum(dxc)
        dbuf[pl.ds(0, ts), :] = dxc
        _tap_grads_from(dcw_ref, dxc, xs)
        dz_ref[0] = _anticausal_taps(dbuf, cw_ref, taps, ts).astype(MXU_DTYPE)
        dbuf[pl.ds(ts, halo), :] = dbuf[pl.ds(0, halo), :]

    vec = pl.BlockSpec((1, C), lambda i: (0, 0))
    mat = pl.BlockSpec((C, C), lambda i: (0, 0))
    tapw = pl.BlockSpec((taps, C), lambda i: (0, 0))
    prev_rows = lambda i: jnp.maximum((n - 1 - i) * hb - 1, 0)
    sds = jax.ShapeDtypeStruct
    return _pcall(
        comm, body, name="lru_bwd", grid=(n,),
        in_specs=[pl.BlockSpec((ts, D), lambda i: (n - 1 - i, 0)), pl.BlockSpec((C, D), lambda i: (0, 0)),
                  pl.BlockSpec((1, ts, C), lambda i: (0, n - 1 - i, 0)),
                  pl.BlockSpec((1, halo, C), lambda i: (0, prev_rows(i), 0)),
                  pl.BlockSpec((1, ts, C), lambda i: (1, n - 1 - i, 0)),
                  pl.BlockSpec((ts, C), lambda i: (n - 1 - i, 0)),
                  pl.BlockSpec((halo, C), lambda i: (prev_rows(i), 0)),
                  tapw, vec, mat, vec, mat, vec, vec, ANY],
        out_specs=[pl.BlockSpec((2, ts, C), lambda i: (0, n - 1 - i, 0)), mat, mat, vec, vec, vec, tapw, vec],
        out_shape=[sds(dz.shape, MXU_DTYPE), sds((C, C), F32), sds((C, C), F32), sds((1, C), F32),
                   sds((1, C), F32), sds((1, C), F32), sds((taps, C), F32), sds((1, C), F32)],
        scratch_shapes=[pltpu.VMEM((ts + halo, C), F32), pltpu.VMEM((ts + halo, C), F32)]
        + [pltpu.VMEM((ts, C), F32)] * 4 + [pltpu.VMEM((ts + halo, C), F32), pltpu.VMEM((8, C), F32)],
        aliases={14: 0},
        semantics=("arbitrary",))(dx1, w_out_l, z, z, z, h, h, conv_w, conv_b, wa, ba, wx, bx, lam, dz)


def _bwd_in(dz, w_in, x, g, dx1):
    S, D = x.shape
    nb, _, C = w_in.shape
    ts = _tile(S, 512)
    halves = 2 if ts % 32 == 0 else 1
    hr = ts // halves

    def body(dz_ref, w_ref, x_ref, g_ref, dx1_ref, dx_ref, dg_ref):
        i = pl.program_id(0)

        @pl.when(i == 0)
        def _():
            dg_ref[...] = jnp.zeros_like(dg_ref)

        def grad_h(k):
            rows = pl.ds(k * hr, hr)
            dh = _mm_nt(dz_ref[0, rows, :], w_ref[0])
            for j in range(1, nb):
                dh = dh + _mm_nt(dz_ref[j, rows, :], w_ref[j])
            return dh

        ahead = grad_h(0)
        for k in range(halves):
            rows, dh = pl.ds(k * hr, hr), ahead
            if k + 1 < halves:
                ahead = grad_h(k + 1)
            rinv, xhat = _rms(x_ref[rows, :])
            dg_ref[...] += _colsum(dh * xhat)
            dx_ref[rows, :] = dx1_ref[rows, :] + _rms_bwd(rinv, xhat, dh * g_ref[...])

    row = pl.BlockSpec((ts, D), lambda i: (i, 0))
    vecd = pl.BlockSpec((1, D), lambda i: (0, 0))
    return pl.pallas_call(
        body, name="bwd_in", grid=(S // ts,),
        in_specs=[pl.BlockSpec((nb, ts, C), lambda i: (0, i, 0)), pl.BlockSpec((nb, D, C), lambda i: (0, 0, 0)),
                  row, vecd, row],
        out_specs=[row, vecd],
        out_shape=[jax.ShapeDtypeStruct((S, D), F32), jax.ShapeDtypeStruct((1, D), F32)],
        compiler_params=_params("arbitrary"))(dz, w_in, x, g, dx1)


def _wgrad(a, b, name, comm=None):
    na, S, K = a.shape
    nb, _, N = b.shape
    nj = max(na, nb)
    assert min(na, nb) == 1
    ts = _tile(S, 1024)
    ns = S // ts
    grp = max(g for g in range(1, nj + 1) if nj % g == 0 and g * K * N * 4 <= WGRAD_ACC_BYTES)
    ga, gb = (grp if na > 1 else 1), (grp if nb > 1 else 1)

    def body(a_ref, b_ref, o_ref, acc):
        s = pl.program_id(1)

        @pl.when(s == 0)
        def _():
            acc[...] = jnp.zeros_like(acc)

        for k in range(grp):
            acc[k] += _mm_tn(a_ref[k if na > 1 else 0], b_ref[k if nb > 1 else 0])

        @pl.when(s == ns - 1)
        def _():
            o_ref[...] = acc[...].astype(o_ref.dtype)

    res = _pcall(
        comm, body, name=name, grid=(nj // grp, ns),
        in_specs=[pl.BlockSpec((ga, ts, K), (lambda j, s: (j, s, 0)) if na > 1 else (lambda j, s: (0, s, 0))),
                  pl.BlockSpec((gb, ts, N), (lambda j, s: (j, s, 0)) if nb > 1 else (lambda j, s: (0, s, 0)))],
        out_specs=pl.BlockSpec((grp, K, N), lambda j, s: (j, 0, 0)),
        out_shape=jax.ShapeDtypeStruct((nj, K, N), WIRE_DTYPE),
        scratch_shapes=[pltpu.VMEM((grp, K, N), F32)],
        semantics=("parallel", "arbitrary"))(a, b)
    return res[0] if comm is None else (res[0][0], res[1])


def _place():
    x, y, c = lax.axis_index("x"), lax.axis_index("y"), lax.axis_index("c")
    other_chips = [(1 - x, y), (x, 1 - y), (1 - x, 1 - y)]
    return x, y, c, other_chips


def _gather_weights(shards):
    nt = len(shards)

    def body(*refs):
        src, dst = refs[:nt], refs[nt:2 * nt]
        ici_send, ici_recv, d2d_send, d2d_recv, own_send, own_recv = refs[2 * nt:]
        x, y, c, chips = _place()
        mine = 2 * x + y

        def half(t, pc):
            hr = src[t].shape[0] // 2
            return pl.ds(pc * hr, hr)

        def own(t):
            return pltpu.make_async_remote_copy(
                src_ref=src[t], dst_ref=dst[t].at[mine], send_sem=own_send.at[t], recv_sem=own_recv.at[t],
                device_id=(x, y, 1 - c), device_id_type=MESH)

        def ici(t, k, block, to):
            cx, cy = block
            ref = dst[t].at[2 * cx + cy, half(t, c)]
            return pltpu.make_async_remote_copy(
                src_ref=src[t].at[half(t, c)] if to is not None else ref, dst_ref=ref,
                send_sem=ici_send.at[t, k], recv_sem=ici_recv.at[t, k],
                device_id=(*to, c) if to is not None else (x, y, c), device_id_type=MESH)

        def d2d(t, k, block, pc):
            cx, cy = block
            ref = dst[t].at[2 * cx + cy, half(t, pc)]
            return pltpu.make_async_remote_copy(
                src_ref=ref, dst_ref=ref, send_sem=d2d_send.at[t, k], recv_sem=d2d_recv.at[t, k],
                device_id=(x, y, 1 - c), device_id_type=MESH)

        sends = [ici(t, k, (x, y), chip) for t in range(nt) for k, chip in enumerate(chips)]
        sends += [own(t) for t in range(nt)]
        for cp in sends:
            cp.start()
        passed = []
        for t in range(nt):
            for k, chip in enumerate(chips):
                ici(t, k, chip, None).wait_recv()
                fw = d2d(t, k, chip, c)
                fw.start()
                passed.append(fw)
        for t in range(nt):
            own(t).wait_recv()
            for k, chip in enumerate(chips):
                d2d(t, k, chip, 1 - c).wait_recv()
        for cp in sends + passed:
            cp.wait_send()

    return pl.pallas_call(
        body, name="gather_weights",
        in_specs=[ANY] * nt, out_specs=[ANY] * nt,
        out_shape=[jax.ShapeDtypeStruct((N_CHIPS,) + s.shape, s.dtype) for s in shards],
        scratch_shapes=[pltpu.SemaphoreType.DMA((nt, 3))] * 4 + [pltpu.SemaphoreType.DMA((nt,))] * 2,
        compiler_params=pltpu.CompilerParams(has_side_effects=True))(*shards)


def _gather_over_ici(shards):
    nt = len(shards)

    def copies(src, dst, scr, arriving):
        ici_send, ici_recv, own_send, own_recv = scr
        x, y, c, chips = _place()
        out = []
        for t in range(nt):
            hr = src[t].shape[0] // 2
            rows = pl.ds(c * hr, hr)
            for k, (cx, cy) in enumerate(chips):
                block = 2 * cx + cy if arriving else 2 * x + y
                out.append(pltpu.make_async_remote_copy(
                    src_ref=src[t].at[rows], dst_ref=dst[t].at[block, rows],
                    send_sem=ici_send.at[t, k], recv_sem=ici_recv.at[t, k],
                    device_id=(cx, cy, c), device_id_type=MESH))
            out.append(pltpu.make_async_remote_copy(
                src_ref=src[t], dst_ref=dst[t].at[2 * x + y], send_sem=own_send.at[t], recv_sem=own_recv.at[t],
                device_id=(x, y, 1 - c), device_id_type=MESH))
        return out

    def start(src, dst, scr):
        for cp in copies(src, dst, scr, False):
            cp.start()

    def finish(src, dst, scr):
        for cp in copies(src, dst, scr, True):
            cp.wait_recv()
        for cp in copies(src, dst, scr, False):
            cp.wait_send()

    return _Comm(shards, [jax.ShapeDtypeStruct((N_CHIPS,) + s.shape, s.dtype) for s in shards],
                 [pltpu.SemaphoreType.DMA((nt, 3))] * 2 + [pltpu.SemaphoreType.DMA((nt,))] * 2, start, finish)


def _gather_pass_on(bufs):
    nt = len(bufs)

    def passed(dst, scr, t, k, block, pc):
        send, recv = scr
        x, y, c, _ = _place()
        cx, cy = block
        hr = dst[t].shape[1] // 2
        ref = dst[t].at[2 * cx + cy, pl.ds(pc * hr, hr)]
        return pltpu.make_async_remote_copy(src_ref=ref, dst_ref=ref, send_sem=send.at[t, k], recv_sem=recv.at[t, k],
                                            device_id=(x, y, 1 - c), device_id_type=MESH)

    def start(src, dst, scr):
        _, _, c, chips = _place()
        for t in range(nt):
            for k, chip in enumerate(chips):
                passed(dst, scr, t, k, chip, c).start()

    def finish(src, dst, scr):
        _, _, c, chips = _place()
        for t in range(nt):
            for k, chip in enumerate(chips):
                passed(dst, scr, t, k, chip, 1 - c).wait_recv()
        for t in range(nt):
            for k, chip in enumerate(chips):
                passed(dst, scr, t, k, chip, c).wait_send()

    return _Comm(bufs, [jax.ShapeDtypeStruct(b.shape, b.dtype) for b in bufs],
                 [pltpu.SemaphoreType.DMA((nt, 3))] * 2, start, finish, aliases={t: t for t in range(nt)})


def _exchange_halves(grads):
    nt = len(grads)

    def copies(src, dst, scr):
        send, recv = scr
        x, y, c, _ = _place()
        out = []
        for t in range(nt):
            hr = src[t].shape[1] // 2
            out.append(pltpu.make_async_remote_copy(
                src_ref=src[t].at[:, pl.ds((1 - c) * hr, hr)], dst_ref=dst[t],
                send_sem=send.at[t], recv_sem=recv.at[t], device_id=(x, y, 1 - c), device_id_type=MESH))
        return out

    def start(src, dst, scr):
        for cp in copies(src, dst, scr):
            cp.start()

    def finish(src, dst, scr):
        for cp in copies(src, dst, scr):
            cp.wait()

    return _Comm(grads, [jax.ShapeDtypeStruct((g.shape[0], g.shape[1] // 2, g.shape[2]), g.dtype) for g in grads],
                 [pltpu.SemaphoreType.DMA((nt,))] * 2, start, finish)


def _add_halves(grad, other, name):
    nb, R, C = grad.shape
    hr = R // 2
    tr = _tile(hr, 256, 16)
    steps = hr // tr
    c = lax.axis_index("c").astype(jnp.int32).reshape((1,))

    def body(c_ref, a_ref, b_ref, o_ref):
        o_ref[...] = (a_ref[...].astype(F32) + b_ref[...].astype(F32)).astype(o_ref.dtype)

    return pl.pallas_call(
        body, name=name,
        grid_spec=pltpu.PrefetchScalarGridSpec(
            num_scalar_prefetch=1, grid=(nb, steps),
            in_specs=[pl.BlockSpec((1, tr, C), lambda j, i, c_ref: (j, c_ref[0] * steps + i, 0)),
                      pl.BlockSpec((1, tr, C), lambda j, i, c_ref: (j, i, 0))],
            out_specs=pl.BlockSpec((1, tr, C), lambda j, i, c_ref: (j, i, 0))),
        out_shape=jax.ShapeDtypeStruct((nb, hr, C), grad.dtype),
        compiler_params=_params("parallel", "parallel"))(c, grad, other)


def _scatter_chip_sums(parts):
    nt = len(parts)

    def copies(src, dst, scr):
        send, recv = scr
        x, y, c, chips = _place()
        out = []
        for t in range(nt):
            for k, (cx, cy) in enumerate(chips):
                out.append(pltpu.make_async_remote_copy(
                    src_ref=src[t].at[2 * cx + cy], dst_ref=dst[t].at[k],
                    send_sem=send.at[t, k], recv_sem=recv.at[t, k], device_id=(cx, cy, c), device_id_type=MESH))
        return out

    def start(src, dst, scr):
        for cp in copies(src, dst, scr):
            cp.start()

    def finish(src, dst, scr):
        for cp in copies(src, dst, scr):
            cp.wait()

    return _Comm(parts, [jax.ShapeDtypeStruct((3,) + p.shape[1:], p.dtype) for p in parts],
                 [pltpu.SemaphoreType.DMA((nt, 3))] * 2, start, finish)


def _sum_chips(part, recv, name):
    _, hr, C = part.shape
    tr = _tile(hr, 256, 16)
    steps = hr // tr
    where = jnp.stack([2 * lax.axis_index("x") + lax.axis_index("y"), lax.axis_index("c")]).astype(jnp.int32)

    def body(w_ref, a_ref, b_ref, o_ref):
        acc = a_ref[0].astype(F32)
        for k in range(3):
            acc = acc + b_ref[k].astype(F32)
        o_ref[...] = acc

    return pl.pallas_call(
        body, name=name,
        grid_spec=pltpu.PrefetchScalarGridSpec(
            num_scalar_prefetch=1, grid=(steps,),
            in_specs=[pl.BlockSpec((1, tr, C), lambda i, w_ref: (w_ref[0], i, 0)),
                      pl.BlockSpec((3, tr, C), lambda i, w_ref: (0, i, 0))],
            out_specs=pl.BlockSpec((tr, C), lambda i, w_ref: (w_ref[1] * steps + i, 0))),
        out_shape=jax.ShapeDtypeStruct((2 * hr, C), F32),
        compiler_params=_params("parallel"))(where, part, recv)


def _join_halves(bufs):
    nt = len(bufs)

    def swap(dst, scr, t, pc):
        send, recv = scr
        x, y, c, _ = _place()
        hr = dst[t].shape[0] // 2
        rows = dst[t].at[pl.ds(pc * hr, hr)]
        return pltpu.make_async_remote_copy(src_ref=rows, dst_ref=rows, send_sem=send.at[t], recv_sem=recv.at[t],
                                            device_id=(x, y, 1 - c), device_id_type=MESH)

    def start(src, dst, scr):
        c = lax.axis_index("c")
        for t in range(nt):
            swap(dst, scr, t, c).start()

    def finish(src, dst, scr):
        c = lax.axis_index("c")
        for t in range(nt):
            swap(dst, scr, t, 1 - c).wait_recv()
        for t in range(nt):
            swap(dst, scr, t, c).wait_send()

    return _Comm(bufs, [jax.ShapeDtypeStruct(b.shape, b.dtype) for b in bufs],
                 [pltpu.SemaphoreType.DMA((nt,))] * 2, start, finish, aliases={t: t for t in range(nt)})


def _all_reduce_rows(buf, loss_row=None):
    R, L = buf.shape

    def copies(in_ref, gath, send, recv):
        x, y, c, _ = _place()
        out = []
        for k in range(1, N_DEV):
            peer = (x ^ ((k >> 2) & 1), y ^ ((k >> 1) & 1), c ^ (k & 1))
            out.append(pltpu.make_async_remote_copy(
                src_ref=in_ref, dst_ref=gath.at[k], send_sem=send.at[k - 1], recv_sem=recv.at[k - 1],
                device_id=peer, device_id_type=MESH))
        return out

    def start(ins, outs, scr):
        gath, send, recv = scr
        gath[0] = ins[0][...]
        for cp in copies(ins[0], gath, send, recv):
            cp.start()

    def finish(ins, outs, scr):
        gath, send, recv = scr
        for cp in copies(ins[0], gath, send, recv):
            cp.wait()
        x, y, c, _ = _place()
        me = 4 * x + 2 * y + c
        total = gath[me]
        for d in range(1, N_DEV):
            total = total + gath[d ^ me]
        outs[0][...] = total
        if loss_row is not None:
            outs[1][...] = jnp.sum(total[loss_row:loss_row + 1, :], axis=1, keepdims=True)

    out_shape = [jax.ShapeDtypeStruct((R, L), F32)]
    if loss_row is not None:
        out_shape.append(jax.ShapeDtypeStruct((1, 1), F32))
    return _Comm([buf], out_shape,
                 [pltpu.VMEM((N_DEV, R, L), F32), pltpu.SemaphoreType.DMA((N_DEV - 1,)),
                  pltpu.SemaphoreType.DMA((N_DEV - 1,))],
                 start, finish, in_specs=[WHOLE_VMEM], out_specs=[WHOLE_VMEM] * len(out_shape))


def _adamw_update(w_ref, g_ref, m_ref, v_ref, d_ref, nm_ref, nv_ref):
    gv = g_ref[...]
    nm = ADAM_B1 * m_ref[...] + (1.0 - ADAM_B1) * gv
    nv = ADAM_B2 * v_ref[...] + (1.0 - ADAM_B2) * (gv * gv)
    nm_ref[...] = nm
    nv_ref[...] = nv
    m_hat = nm / (1.0 - ADAM_B1 ** ADAM_STEP)
    v_hat = nv / (1.0 - ADAM_B2 ** ADAM_STEP)
    d_ref[...] = -ADAM_LR * (m_hat / (jnp.sqrt(v_hat) + ADAM_EPS) + ADAM_WD * w_ref[...])


def _adamw(w, g, m, v, name):
    R, C = w.shape
    tr = _tile(R, 256)
    blk = pl.BlockSpec((tr, C), lambda i: (i, 0))
    return pl.pallas_call(
        _adamw_update, name=name, grid=(R // tr,), in_specs=[blk] * 4, out_specs=[blk] * 3,
        out_shape=[jax.ShapeDtypeStruct((R, C), F32)] * 3,
        compiler_params=_params("parallel"))(w, g, m, v)


def _adamw_many(ws, gs, ms, vs, name):
    n = len(ws)

    def body(*refs):
        for k in range(n):
            _adamw_update(*[refs[part * n + k] for part in range(7)])

    shapes = [jax.ShapeDtypeStruct(w.shape, F32) for w in ws]
    outs = pl.pallas_call(
        body, name=name, in_specs=[WHOLE_VMEM] * (4 * n), out_specs=[WHOLE_VMEM] * (3 * n), out_shape=shapes * 3,
        compiler_params=pltpu.CompilerParams(vmem_limit_bytes=VMEM_LIMIT_BYTES))(*ws, *gs, *ms, *vs)
    return outs[:n], outs[n:2 * n], outs[2 * n:]


def _pack_rows(arrays):
    rows = []
    for a in arrays:
        flat = a.reshape(-1).astype(F32)
        pad = (-flat.shape[0]) % LANES
        rows.append(jnp.pad(flat, (0, pad)).reshape(-1, LANES))
    buf = jnp.concatenate(rows, axis=0)
    return jnp.pad(buf, ((0, (-buf.shape[0]) % 8), (0, 0)))


def _unpack_rows(buf, shapes):
    out, r = [], 0
    for s in shapes:
        n = math.prod(s)
        nr = -(-n // LANES)
        out.append(buf[r:r + nr].reshape(-1)[:n].reshape(s))
        r += nr
    return out


def _block_diag(w):
    H, a, b = w.shape
    eye = jnp.eye(H, dtype=w.dtype)
    return (eye[:, None, :, None] * w[:, :, None, :]).reshape(H * a, H * b)


def _block_diag_parts(d, H):
    a, b = d.shape[0] // H, d.shape[1] // H
    d4 = d.reshape(H, a, H, b)
    return jnp.stack([d4[h, :, h, :] for h in range(H)])


def _rs_add(names, grads, others):
    return [_add_halves(g, o, "rs_add_halves_" + n) for n, g, o in zip(names, grads, others)]


def _rs_sum(names, parts, recvs):
    return [_sum_chips(p, r, "rs_sum_chips_" + n) for n, p, r in zip(names, parts, recvs)]


def _step(x, mem, target, shards, small, tap_rows, tap_shapes):
    D = x.shape[1]
    nch = N_CHIPS
    p = dict(small)

    (w_in_f,) = _gather_weights([shards['w_in']])
    wf = {}

    def ici(names):
        return _gather_over_ici([shards[n] for n in names])

    ici_a, taps_sum = ici(['w_out', 'w_q']), _all_reduce_rows(tap_rows)
    (z, h1), couts = _fwd_in(x, p['mix_norm_g'], w_in_f, comm=_merge(ici_a, taps_sum))
    bufs_a, (taps,) = _split(couts, ici_a, taps_sum)
    p.update(zip(COL_SHARDED_SMALL, _unpack_rows(taps, tap_shapes)))
    wa_d = _block_diag(p['lru_w_a']).astype(MXU_DTYPE)
    wx_d = _block_diag(p['lru_w_x']).astype(MXU_DTYPE)
    heads = p['lru_w_a'].shape[0]
    pass_a, ici_b = _gather_pass_on(bufs_a), ici(['w_kv', 'w_o'])
    (h, y_lru), couts = _lru_fwd(z, p['lru_conv_w'], p['lru_conv_b'], wa_d, p['lru_b_a'], wx_d, p['lru_b_x'],
                                 p['lru_lambda'], comm=_merge(pass_a, ici_b))
    (wf['w_out'], wf['w_q']), bufs_b = _split(couts, pass_a, ici_b)
    pass_b, ici_c = _gather_pass_on(bufs_b), ici(['w_up'])
    (c1, c3), couts = _conf_fwd(z, p['conf_conv_w'], p['conf_conv_b'], p['conf_ln_g'], p['conf_ln_b'],
                                comm=_merge(pass_b, ici_c))
    (wf['w_kv'], wf['w_o']), bufs_c = _split(couts, pass_b, ici_c)
    w_out2 = wf['w_out'].reshape(2, -1, D)
    w_q = wf['w_q'].reshape(D, D)
    w_o = wf['w_o'].reshape(D, D)
    pass_c, ici_d = _gather_pass_on(bufs_c), ici(['w_down'])
    (x1, h2, q), couts = _fwd_out_q(x, y_lru, c3, w_out2, p['xa_norm_g'], w_q, comm=_merge(pass_c, ici_d))
    (wf['w_up'],), bufs_d = _split(couts, pass_c, ici_d)
    m, kv = _kv_fwd(mem, p['mem_norm_g'], wf['w_kv'])
    (o, x2, h3), (wf['w_down'],) = _attn_fwd(q, kv, x1, w_o, p['ffn_norm_g'], comm=_gather_pass_on(bufs_d))
    gu, act, dx3, loss_lanes, d_final_g = _ffn_fwd(h3, wf['w_up'], p['ffn_conv_w'], p['ffn_conv_b'], wf['w_down'],
                                                   x2, p['final_norm_g'], target)

    dgu, dx2, d_ffn_g, d_ffn_cw, d_ffn_cb = _ffn_bwd(dx3, wf['w_down'], wf['w_up'], gu, x2, p['ffn_norm_g'],
                                                     p['ffn_conv_w'], p['ffn_conv_b'])
    g_down = _wgrad(act, dx3[None], "wgrad_down").reshape(nch, -1, D)
    g_up, other = _wgrad(h3[None], dgu, "wgrad_up", comm=_exchange_halves([g_down]))
    (p_down,) = _rs_add(['w_down'], [g_down], other)
    sc_down, ex_up = _scatter_chip_sums([p_down]), _exchange_halves([g_up])
    (dq, dx1, dkv, d_xa_g), couts = _attn_bwd(dx2, w_o, q, kv, x1, p['xa_norm_g'], w_q, comm=_merge(sc_down, ex_up))
    recv, other = _split(couts, sc_down, ex_up)
    f_down = _rs_sum(['w_down'], [p_down], recv)
    (p_up,) = _rs_add(['w_up'], [g_up], other)
    mid = ['w_o', 'w_q', 'w_kv']
    g_o = _wgrad(o[None], dx2[None], "wgrad_o").reshape(nch, -1, D)
    g_q = _wgrad(h2[None], dq[None], "wgrad_q").reshape(nch, -1, D)
    g_kv, d_mem_g = _kv_bwd(dkv, wf['w_kv'], mem, p['mem_norm_g'], m)
    join_down, sc_up, ex_mid = _join_halves(f_down), _scatter_chip_sums([p_up]), _exchange_halves([g_o, g_q, g_kv])
    (dz_c, d_conf_cw, d_conf_cb, d_ln_g, d_ln_b), couts = _conf_bwd(
        dx1, w_out2[1], z, c1, p['conf_conv_w'], p['conf_ln_g'], p['conf_ln_b'],
        comm=_merge(join_down, sc_up, ex_mid))
    (r_down,), recv, other = _split(couts, join_down, sc_up, ex_mid)
    p_up = [p_up]
    p_mid = _rs_add(mid, [g_o, g_q, g_kv], other)
    join_up, sc_mid = _join_halves(_rs_sum(['w_up'], p_up, recv)), _scatter_chip_sums(p_mid)
    (dz, d_wa, d_wx, d_ba, d_bx, d_lam, d_lru_cw, d_lru_cb), couts = _lru_bwd(
        dx1, w_out2[0], z, h, p['lru_conv_w'], p['lru_conv_b'], wa_d, p['lru_b_a'], wx_d, p['lru_b_x'],
        p['lru_lambda'], dz_c, comm=_merge(join_up, sc_mid))
    (r_up,), recv = _split(couts, join_up, sc_mid)
    f_mid = _rs_sum(mid, p_mid, recv)
    grad_x, d_mix_g = _bwd_in(dz, w_in_f, x, p['mix_norm_g'], dx1)

    small_g = {'mix_norm_g': d_mix_g, 'lru_conv_w': d_lru_cw, 'lru_conv_b': d_lru_cb,
               'lru_w_a': _block_diag_parts(d_wa, heads), 'lru_b_a': d_ba,
               'lru_w_x': _block_diag_parts(d_wx, heads), 'lru_b_x': d_bx, 'lru_lambda': d_lam,
               'conf_conv_w': d_conf_cw, 'conf_conv_b': d_conf_cb, 'conf_ln_g': d_ln_g, 'conf_ln_b': d_ln_b,
               'xa_norm_g': d_xa_g, 'mem_norm_g': d_mem_g, 'ffn_norm_g': d_ffn_g,
               'ffn_conv_w': d_ffn_cw, 'ffn_conv_b': d_ffn_cb, 'final_norm_g': d_final_g}
    names = list(small_g)
    shapes = [small_g[n].shape for n in names]
    join_mid = _join_halves(f_mid)
    small_sum = _all_reduce_rows(_pack_rows([loss_lanes] + [small_g[n] for n in names]), loss_row=0)
    g_in, couts = _wgrad(h1[None], dz, "wgrad_in", comm=_merge(join_mid, small_sum))
    r_mid, (summed, loss) = _split(couts, join_mid, small_sum)
    g_out_l, other = _wgrad(y_lru[None], dx1[None], "wgrad_out_lru", comm=_exchange_halves([g_in]))
    p_in = _rs_add(['w_in'], [g_in], other)
    g_out_c, recv = _wgrad(c3[None], dx1[None], "wgrad_out_conf", comm=_scatter_chip_sums(p_in))
    f_in = _rs_sum(['w_in'], p_in, recv)

    g_out = jnp.concatenate([g_out_l, g_out_c], axis=0).reshape(nch, -1, D)
    p_out = _rs_add(['w_out'], [g_out], _run_comm(_exchange_halves([g_out]), "rs_exchange_last"))
    f_out = _rs_sum(['w_out'], p_out, _run_comm(_scatter_chip_sums(p_out), "rs_scatter_last"))
    r_last = _run_comm(_join_halves(f_out + f_in), "rs_join_last")
    big = dict(zip(['w_down', 'w_up'] + mid + ['w_out', 'w_in'], [r_down, r_up] + r_mid + r_last))
    return grad_x, big, summed, loss, names, [loss_lanes.shape] + shapes


def kernel(x, mem, mix_norm_g, w_in, lru_conv_w, lru_conv_b, lru_w_a, lru_b_a, lru_w_x, lru_b_x, lru_lambda, conf_conv_w, conf_conv_b, conf_ln_g, conf_ln_b, w_out, xa_norm_g, mem_norm_g, w_q, w_kv, w_o, ffn_norm_g, w_up, ffn_conv_w, ffn_conv_b, w_down, final_norm_g, loss_target, m_mix_norm_g, m_w_in, m_lru_conv_w, m_lru_conv_b, m_lru_w_a, m_lru_b_a, m_lru_w_x, m_lru_b_x, m_lru_lambda, m_conf_conv_w, m_conf_conv_b, m_conf_ln_g, m_conf_ln_b, m_w_out, m_xa_norm_g, m_mem_norm_g, m_w_q, m_w_kv, m_w_o, m_ffn_norm_g, m_w_up, m_ffn_conv_w, m_ffn_conv_b, m_w_down, m_final_norm_g, v_mix_norm_g, v_w_in, v_lru_conv_w, v_lru_conv_b, v_lru_w_a, v_lru_b_a, v_lru_w_x, v_lru_b_x, v_lru_lambda, v_conf_conv_w, v_conf_conv_b, v_conf_ln_g, v_conf_ln_b, v_w_out, v_xa_norm_g, v_mem_norm_g, v_w_q, v_w_kv, v_w_o, v_ffn_norm_g, v_w_up, v_ffn_conv_w, v_ffn_conv_b, v_w_down, v_final_norm_g):
    given = dict(locals())
    w = {n: given[n] for n in WEIGHTS}
    mom = {n: given["m_" + n] for n in WEIGHTS}
    var = {n: given["v_" + n] for n in WEIGHTS}
    xi, yi, ci = lax.axis_index("x"), lax.axis_index("y"), lax.axis_index("c")
    chip = 2 * xi + yi

    shards = {n: w[n][0].astype(WIRE_DTYPE) for n in BIG}
    tap_full = []
    for n in COL_SHARDED_SMALL:
        s = w[n][0]
        full = jnp.zeros((s.shape[0], N_CHIPS * s.shape[1]), F32)
        s = jnp.where(ci == 0, s, jnp.zeros_like(s))
        tap_full.append(lax.dynamic_update_slice(full, s, (0, chip * s.shape[1])))
    small = {n: (w[n] if w[n].ndim == 1 else w[n][0]) for n in SMALL if n not in COL_SHARDED_SMALL}
    small = {n: (a.reshape(1, -1) if a.ndim == 1 else a) for n, a in small.items()}

    grad_x, big_g, summed, loss, small_names, packed_shapes = _step(
        x[0], mem[0], loss_target[0], shards, small, _pack_rows(tap_full), [t.shape for t in tap_full])
    small_sum = dict(zip(small_names, _unpack_rows(summed, packed_shapes)[1:]))

    grads = {}
    for n in WEIGHTS:
        if n in BIG:
            g = big_g[n]
        elif n in COL_SHARDED_SMALL:
            width = w[n].shape[-1]
            g = lax.dynamic_slice_in_dim(small_sum[n], chip * width, width, axis=1)
        else:
            g = small_sum[n]
        grads[n] = g.reshape(w[n].shape)

    delta, new_m, new_v = {}, {}, {}
    for n in BIG:
        d, nm, nv = _adamw(w[n][0], grads[n][0], mom[n][0], var[n][0], "adamw_" + n)
        delta[n], new_m[n], new_v[n] = d[None], nm[None], nv[None]
    flat = lambda a: a.reshape(-1, a.shape[-1])
    outs = _adamw_many(*[[flat(src[n]) for n in SMALL] for src in (w, grads, mom, var)], "adamw_small")
    for out, arrays in zip((delta, new_m, new_v), outs):
        out.update({n: a.reshape(w[n].shape) for n, a in zip(SMALL, arrays)})

    return (loss[0, 0], grad_x[None], *[grads[n] for n in WEIGHTS], *[delta[n] for n in WEIGHTS],
            *[new_m[n] for n in WEIGHTS], *[new_v[n] for n in WEIGHTS])
```

```python
import math

import jax
import jax.numpy as jnp
from jax import lax
from jax.experimental import pallas as pl
from jax.experimental.pallas import tpu as pltpu

F32 = jnp.float32
MXU_DTYPE = jnp.bfloat16
WIRE_DTYPE = jnp.bfloat16
EPS = 1e-6
RG_C = 8.0
XA_HEADS = 4
ADAM_LR, ADAM_B1, ADAM_B2, ADAM_EPS, ADAM_WD, ADAM_STEP = 0.001, 0.9, 0.999, 1e-08, 0.01, 10
VMEM_LIMIT_BYTES = 52 * 1024 * 1024
WGRAD_ACC_BYTES = 8 * 1024 * 1024
LANES = 1024
N_CHIPS = 4
N_DEV = 8
MESH = pl.DeviceIdType.MESH
GELU_C = math.sqrt(2.0 / math.pi)
GELU_K = 0.044715

WEIGHTS = ['mix_norm_g', 'w_in', 'lru_conv_w', 'lru_conv_b', 'lru_w_a', 'lru_b_a', 'lru_w_x', 'lru_b_x',
           'lru_lambda', 'conf_conv_w', 'conf_conv_b', 'conf_ln_g', 'conf_ln_b', 'w_out', 'xa_norm_g',
           'mem_norm_g', 'w_q', 'w_kv', 'w_o', 'ffn_norm_g', 'w_up', 'ffn_conv_w', 'ffn_conv_b', 'w_down',
           'final_norm_g']
BIG = ['w_in', 'w_kv', 'w_up', 'w_out', 'w_q', 'w_o', 'w_down']
SMALL = [n for n in WEIGHTS if n not in BIG]
COL_SHARDED_SMALL = ['lru_conv_w', 'conf_conv_w', 'ffn_conv_w']


def _params(*semantics):
    return pltpu.CompilerParams(dimension_semantics=semantics, vmem_limit_bytes=VMEM_LIMIT_BYTES)


ANY = pl.BlockSpec(memory_space=pl.ANY)
WHOLE_VMEM = pl.BlockSpec(memory_space=pltpu.VMEM)


class _Comm:
    def __init__(self, arrays, out_shapes, scratch, start, finish, aliases=None, in_specs=None, out_specs=None):
        self.arrays, self.out_shapes, self.scratch = list(arrays), list(out_shapes), list(scratch)
        self.start, self.finish = start, finish
        self.aliases = dict(aliases or {})
        self.in_specs = list(in_specs) if in_specs is not None else [ANY] * len(self.arrays)
        self.out_specs = list(out_specs) if out_specs is not None else [ANY] * len(self.out_shapes)


def _merge(*comms):
    comms = [c for c in comms if c is not None]
    if not comms:
        return None
    ai = [0]
    for c in comms:
        ai.append(ai[-1] + len(c.arrays))
    oi = [0]
    for c in comms:
        oi.append(oi[-1] + len(c.out_shapes))
    si = [0]
    for c in comms:
        si.append(si[-1] + len(c.scratch))

    def each(which):
        def run(ins, outs, scr):
            for k, c in enumerate(comms):
                getattr(c, which)(ins[ai[k]:ai[k + 1]], outs[oi[k]:oi[k + 1]], scr[si[k]:si[k + 1]])
        return run

    aliases = {ai[k] + i: oi[k] + o for k, c in enumerate(comms) for i, o in c.aliases.items()}
    return _Comm(sum((c.arrays for c in comms), []), sum((c.out_shapes for c in comms), []),
                 sum((c.scratch for c in comms), []), each("start"), each("finish"), aliases,
                 sum((c.in_specs for c in comms), []), sum((c.out_specs for c in comms), []))


def _split(outs, *comms):
    parts, at = [], 0
    for c in comms:
        parts.append(outs[at:at + len(c.out_shapes)])
        at += len(c.out_shapes)
    return parts


def _pcall(comm, body, *, name, grid, in_specs, out_specs, out_shape, semantics, scratch_shapes=(), aliases=None):
    single = not isinstance(out_shape, (list, tuple))
    out_shape = [out_shape] if single else list(out_shape)
    out_specs = [out_specs] if single else list(out_specs)
    in_specs, scratch_shapes = list(in_specs), list(scratch_shapes)
    aliases = dict(aliases or {})

    if comm is None:
        def plain(*args):
            return list(pl.pallas_call(body, name=name, grid=grid, in_specs=in_specs, out_specs=out_specs,
                                       out_shape=out_shape, scratch_shapes=scratch_shapes,
                                       input_output_aliases=aliases,
                                       compiler_params=_params(*semantics))(*args))
        return plain

    def hosted(*args):
        n_in, n_out, n_scr = len(args), len(out_shape), len(scratch_shapes)
        c_in, c_out = len(comm.arrays), len(comm.out_shapes)

        def wrapped(*refs):
            ins, cins = refs[:n_in], refs[n_in:n_in + c_in]
            o0 = n_in + c_in
            outs, couts = refs[o0:o0 + n_out], refs[o0 + n_out:o0 + n_out + c_out]
            s0 = o0 + n_out + c_out
            scr, cscr = refs[s0:s0 + n_scr], refs[s0 + n_scr:]
            first = last = None
            for axis, size in enumerate(grid):
                at_start, at_end = pl.program_id(axis) == 0, pl.program_id(axis) == size - 1
                first = at_start if first is None else first & at_start
                last = at_end if last is None else last & at_end
            if first is None:
                comm.start(cins, couts, cscr)
                body(*ins, *outs, *scr)
                comm.finish(cins, couts, cscr)
                return
            pl.when(first)(lambda: comm.start(cins, couts, cscr))
            body(*ins, *outs, *scr)
            pl.when(last)(lambda: comm.finish(cins, couts, cscr))

        res = pl.pallas_call(
            wrapped, name=name, grid=grid, in_specs=in_specs + comm.in_specs, out_specs=out_specs + comm.out_specs,
            out_shape=out_shape + comm.out_shapes, scratch_shapes=scratch_shapes + comm.scratch,
            input_output_aliases={**aliases, **{n_in + i: n_out + o for i, o in comm.aliases.items()}},
            compiler_params=pltpu.CompilerParams(dimension_semantics=("arbitrary",) * len(grid),
                                                 vmem_limit_bytes=VMEM_LIMIT_BYTES, has_side_effects=True),
        )(*args, *comm.arrays)
        return list(res[:n_out]), list(res[n_out:])

    return hosted


def _run_comm(comm, name):
    return _pcall(comm, lambda: None, name=name, grid=(), in_specs=[], out_specs=[], out_shape=[], semantics=())()[1]


def _tile(n, want, align=8):
    if n <= want:
        return n
    for t in range(want - want % align, 0, -align):
        if n % t == 0:
            return t
    raise ValueError((n, want, align))


def _mm(a, b):
    return jnp.dot(a.astype(MXU_DTYPE), b.astype(MXU_DTYPE), preferred_element_type=F32)


def _mm_nt(a, b):
    return lax.dot_general(a.astype(MXU_DTYPE), b.astype(MXU_DTYPE), (((1,), (1,)), ((), ())),
                           preferred_element_type=F32)


def _mm_tn(a, b):
    return lax.dot_general(a.astype(MXU_DTYPE), b.astype(MXU_DTYPE), (((0,), (0,)), ((), ())),
                           preferred_element_type=F32)


def _sigmoid(v):
    return 0.5 * jnp.tanh(0.5 * v) + 0.5


def _gelu(v):
    v2 = v * v
    t = jnp.tanh(v * (GELU_C + (GELU_C * GELU_K) * v2))
    hv = 0.5 * v
    dt = (1.0 - t * t) * (GELU_C + (3.0 * GELU_C * GELU_K) * v2)
    return hv + hv * t, (0.5 + 0.5 * t) + hv * dt


def _softplus_neg(lam):
    e = jnp.exp(-jnp.abs(lam))
    u = 1.0 + e
    log1p_e = jnp.where(u == 1.0, e, jnp.log(u) * e / jnp.where(u == 1.0, 1.0, u - 1.0))
    return jnp.maximum(-lam, 0.0) + log1p_e


def _rms(xv):
    rinv = lax.rsqrt(jnp.mean(xv * xv, axis=-1, keepdims=True) + EPS)
    return rinv, xv * rinv


def _rms_bwd(rinv, xhat, dxhat):
    return rinv * (dxhat - xhat * jnp.mean(dxhat * xhat, axis=-1, keepdims=True))


def _colsum(v):
    return jnp.sum(v, axis=0, keepdims=True)


def _wrow(w_ref, k, wcols):
    return w_ref[pl.ds(k, 1), :] if wcols is None else w_ref[pl.ds(k, 1), wcols]


def _windows(buf_ref, halo, taps, rows):
    assert taps <= 8 <= halo
    x = buf_ref[pl.ds(halo - 8, rows + 8), :]
    return [x[8:] if s == 0 else pltpu.roll(x, s, 0)[8:] for s in range(taps)]


def _causal_from(xs, w_ref, wcols=None):
    taps = len(xs)
    acc = None
    for s in range(taps):
        term = _wrow(w_ref, taps - 1 - s, wcols) * xs[s]
        acc = term if acc is None else acc + term
    return acc


def _tap_grads_from(dw_ref, dy, xs, wcols=None):
    taps = len(xs)
    for s in range(taps):
        g = _colsum(dy * xs[s])
        if wcols is None:
            dw_ref[pl.ds(taps - 1 - s, 1), :] += g
        else:
            dw_ref[pl.ds(taps - 1 - s, 1), wcols] += g


def _causal_taps(buf_ref, halo, w_ref, taps, rows, wcols=None):
    return _causal_from(_windows(buf_ref, halo, taps, rows), w_ref, wcols)


def _anticausal_taps(buf_ref, w_ref, taps, rows, wcols=None):
    assert taps <= 8
    x = buf_ref[pl.ds(0, rows + 8), :]
    acc = None
    for s in range(taps):
        win = x[:rows] if s == 0 else pltpu.roll(x, rows + 8 - s, 0)[:rows]
        term = _wrow(w_ref, taps - 1 - s, wcols) * win
        acc = term if acc is None else acc + term
    return acc


def _shift_copies(dst_ref, buf_ref, rows, up):
    x = buf_ref[pl.ds(0, rows + 8), :]
    for r in range(8):
        if up:
            dst_ref[r] = x[:rows] if r == 0 else pltpu.roll(x, rows + 8 - r, 0)[:rows]
        else:
            dst_ref[r] = x[8:] if r == 0 else pltpu.roll(x, r, 0)[8:]


def _causal_taps8(sh_ref, halo, w_ref, taps, rows):
    acc = None
    for s in range(taps):
        term = _wrow(w_ref, taps - 1 - s, None) * sh_ref[s % 8, pl.ds(halo - 8 - 8 * (s // 8), rows), :]
        acc = term if acc is None else acc + term
    return acc


def _anticausal_taps8(sh_ref, w_ref, taps, rows):
    acc = None
    for s in range(taps):
        term = _wrow(w_ref, taps - 1 - s, None) * sh_ref[s % 8, pl.ds(8 * (s // 8), rows), :]
        acc = term if acc is None else acc + term
    return acc


def _tap_grads8(dw_ref, dy, sh_ref, halo, taps, rows):
    for s in range(taps):
        dw_ref[pl.ds(taps - 1 - s, 1), :] += _colsum(dy * sh_ref[s % 8, pl.ds(halo - 8 - 8 * (s // 8), rows), :])


def _fwd_in(x, g, w_in, comm=None):
    S, D = x.shape
    nb, _, C = w_in.shape
    ts = _tile(S, 1024)

    halves = 2 if ts % 32 == 0 else 1
    hr = ts // halves

    def body(x_ref, g_ref, w_ref, z_ref, h_ref):
        def norm(k):
            rows = pl.ds(k * hr, hr)
            _, xhat = _rms(x_ref[rows, :])
            h = (xhat * g_ref[...]).astype(MXU_DTYPE)
            h_ref[rows, :] = h
            return h

        h_next = norm(0)
        for k in range(halves):
            rows, h = pl.ds(k * hr, hr), h_next
            if k + 1 < halves:
                h_next = norm(k + 1)
            for j in range(nb):
                z_ref[j, rows, :] = jnp.dot(h, w_ref[j], preferred_element_type=F32)

    return _pcall(
        comm, body, name="fwd_in", grid=(S // ts,),
        in_specs=[pl.BlockSpec((ts, D), lambda i: (i, 0)), pl.BlockSpec((1, D), lambda i: (0, 0)),
                  pl.BlockSpec((nb, D, C), lambda i: (0, 0, 0))],
        out_specs=[pl.BlockSpec((nb, ts, C), lambda i: (0, i, 0)), pl.BlockSpec((ts, D), lambda i: (i, 0))],
        out_shape=[jax.ShapeDtypeStruct((nb, S, C), F32), jax.ShapeDtypeStruct((S, D), MXU_DTYPE)],
        semantics=("parallel",))(x, g, w_in)


def _lru_gates(xc, wa_ref, ba_ref, wx_ref, bx_ref, sp):
    xb = xc.astype(MXU_DTYPE)
    r = _sigmoid(jnp.dot(xb, wa_ref[...], preferred_element_type=F32) + ba_ref[...])
    ig = _sigmoid(jnp.dot(xb, wx_ref[...], preferred_element_type=F32) + bx_ref[...])
    log_a = -RG_C * r * sp
    a = jnp.exp(log_a)
    one_minus_a2 = jnp.tanh(-log_a) * (a * a + 1.0)
    inv_mult = lax.rsqrt(one_minus_a2)
    mult = jnp.where(one_minus_a2 > 0.0, one_minus_a2 * inv_mult, 0.0)
    return r, ig, a, mult, inv_mult


def _lru_fwd(z, conv_w, conv_b, wa, ba, wx, bx, lam, comm=None):
    _, S, C = z.shape
    ts = _tile(S, 256)
    taps = conv_w.shape[0]
    halo = 8

    def body(zx_ref, zg_ref, cw_ref, cb_ref, wa_ref, ba_ref, wx_ref, bx_ref, lam_ref,
             h_ref, y_ref, xbuf, a_s, u_s, hc):
        i = pl.program_id(0)

        @pl.when(i == 0)
        def _():
            xbuf[pl.ds(0, halo), :] = jnp.zeros((halo, C), F32)
            hc[...] = jnp.zeros_like(hc)

        xbuf[pl.ds(halo, ts), :] = zx_ref[0]
        xc = _causal_taps(xbuf, halo, cw_ref, taps, ts) + cb_ref[...]
        sp = _softplus_neg(lam_ref[...])
        _, ig, a, mult, _ = _lru_gates(xc, wa_ref, ba_ref, wx_ref, bx_ref, sp)
        a_s[...] = a
        u_s[...] = mult * (ig * xc)
        row = lax.broadcasted_iota(jnp.int32, (8, C), 0)

        def step(k, carry):
            off = pl.multiple_of(k * 8, 8)
            av = a_s[pl.ds(off, 8), :]
            uv = u_s[pl.ds(off, 8), :]
            for d in (1, 2, 4):
                m = row >= d
                a_sh = jnp.where(m, pltpu.roll(av, d, 0), 1.0)
                u_sh = jnp.where(m, pltpu.roll(uv, d, 0), 0.0)
                uv = uv + av * u_sh
                av = av * a_sh
            hv = uv + av * carry
            h_ref[pl.ds(off, 8), :] = hv
            return jnp.broadcast_to(hv[7:8, :], (8, C))

        hc[...] = lax.fori_loop(0, ts // 8, step, hc[...], unroll=4)
        ge, _ = _gelu(zg_ref[0])
        y_ref[...] = (h_ref[...] * ge).astype(MXU_DTYPE)
        xbuf[pl.ds(0, halo), :] = xbuf[pl.ds(ts, halo), :]

    vec = pl.BlockSpec((1, C), lambda i: (0, 0))
    mat = pl.BlockSpec((C, C), lambda i: (0, 0))
    return _pcall(
        comm, body, name="lru_fwd", grid=(S // ts,),
        in_specs=[pl.BlockSpec((1, ts, C), lambda i: (0, i, 0)), pl.BlockSpec((1, ts, C), lambda i: (1, i, 0)),
                  pl.BlockSpec((taps, C), lambda i: (0, 0)), vec, mat, vec, mat, vec, vec],
        out_specs=[pl.BlockSpec((ts, C), lambda i: (i, 0)), pl.BlockSpec((ts, C), lambda i: (i, 0))],
        out_shape=[jax.ShapeDtypeStruct((S, C), F32), jax.ShapeDtypeStruct((S, C), MXU_DTYPE)],
        scratch_shapes=[pltpu.VMEM((ts + halo, C), F32), pltpu.VMEM((ts, C), F32), pltpu.VMEM((ts, C), F32),
                        pltpu.VMEM((8, C), F32)],
        semantics=("arbitrary",))(z, z, conv_w, conv_b, wa, ba, wx, bx, lam)


def _layer_norm_stats(c1):
    mu = jnp.mean(c1, axis=-1, keepdims=True)
    xc = c1 - mu
    rstd = lax.rsqrt(jnp.mean(xc * xc, axis=-1, keepdims=True) + EPS)
    return rstd, xc * rstd


def _conf_fwd(z, conv_w, conv_b, ln_g, ln_b, comm=None):
    _, S, C = z.shape
    ts = _tile(S, 256)
    taps = conv_w.shape[0]
    halo = 32

    def body(za_ref, zb_ref, cw_ref, cb_ref, g_ref, b_ref, c1_ref, c3_ref, cbuf, shifted):
        i = pl.program_id(0)

        @pl.when(i == 0)
        def _():
            cbuf[pl.ds(0, halo), :] = jnp.zeros((halo, C), F32)

        cbuf[pl.ds(halo, ts), :] = za_ref[0] * _sigmoid(zb_ref[0])
        _shift_copies(shifted, cbuf, ts + halo - 8, up=False)
        c1 = _causal_taps8(shifted, halo, cw_ref, taps, ts) + cb_ref[...]
        c1_ref[...] = c1
        _, xhat = _layer_norm_stats(c1)
        c2 = xhat * g_ref[...] + b_ref[...]
        c3_ref[...] = (c2 * _sigmoid(c2)).astype(MXU_DTYPE)
        cbuf[pl.ds(0, halo), :] = cbuf[pl.ds(ts, halo), :]

    vec = pl.BlockSpec((1, C), lambda i: (0, 0))
    return _pcall(
        comm, body, name="conf_fwd", grid=(S // ts,),
        in_specs=[pl.BlockSpec((1, ts, C), lambda i: (2, i, 0)), pl.BlockSpec((1, ts, C), lambda i: (3, i, 0)),
                  pl.BlockSpec((taps, C), lambda i: (0, 0)), vec, vec, vec],
        out_specs=[pl.BlockSpec((ts, C), lambda i: (i, 0)), pl.BlockSpec((ts, C), lambda i: (i, 0))],
        out_shape=[jax.ShapeDtypeStruct((S, C), F32), jax.ShapeDtypeStruct((S, C), MXU_DTYPE)],
        scratch_shapes=[pltpu.VMEM((ts + halo, C), F32), pltpu.VMEM((8, ts + halo - 8, C), F32)],
        semantics=("arbitrary",))(z, z, conv_w, conv_b, ln_g, ln_b)


def _fwd_out_q(x, y_lru, c3, w_out, g_xa, w_q, comm=None):
    S, D = x.shape
    C = y_lru.shape[1]
    ts = _tile(S, 1024)

    halves = 2 if ts % 32 == 0 else 1
    hr = ts // halves

    def body(x_ref, yl_ref, c3_ref, wo_ref, g_ref, wq_ref, x1_ref, h2_ref, q_ref):
        def mixed(k):
            rows = pl.ds(k * hr, hr)
            return (jnp.dot(yl_ref[rows, :], wo_ref[0], preferred_element_type=F32)
                    + jnp.dot(c3_ref[rows, :], wo_ref[1], preferred_element_type=F32))

        y_next = mixed(0)
        for k in range(halves):
            rows, y = pl.ds(k * hr, hr), y_next
            if k + 1 < halves:
                y_next = mixed(k + 1)
            x1 = x_ref[rows, :] + y
            x1_ref[rows, :] = x1
            _, xhat = _rms(x1)
            h2 = (xhat * g_ref[...]).astype(MXU_DTYPE)
            h2_ref[rows, :] = h2
            q_ref[rows, :] = jnp.dot(h2, wq_ref[...], preferred_element_type=F32).astype(MXU_DTYPE)

    row = lambda w: pl.BlockSpec((ts, w), lambda i: (i, 0))
    return _pcall(
        comm, body, name="fwd_out_q", grid=(S // ts,),
        in_specs=[row(D), row(C), row(C), pl.BlockSpec((2, C, D), lambda i: (0, 0, 0)),
                  pl.BlockSpec((1, D), lambda i: (0, 0)), pl.BlockSpec((D, D), lambda i: (0, 0))],
        out_specs=[row(D), row(D), row(D)],
        out_shape=[jax.ShapeDtypeStruct((S, D), F32), jax.ShapeDtypeStruct((S, D), MXU_DTYPE),
                   jax.ShapeDtypeStruct((S, D), MXU_DTYPE)],
        semantics=("parallel",))(x, y_lru, c3, w_out, g_xa, w_q)


def _kv_fwd(mem, g, w_kv):
    M, D = mem.shape
    nb, _, C = w_kv.shape

    def body(mem_ref, g_ref, w_ref, m_ref, kv_ref):
        _, xhat = _rms(mem_ref[...])
        m = (xhat * g_ref[...]).astype(MXU_DTYPE)
        m_ref[...] = m
        for j in range(nb):
            kv_ref[:, pl.ds(j * C, C)] = jnp.dot(m, w_ref[j], preferred_element_type=F32).astype(MXU_DTYPE)

    return pl.pallas_call(
        body, name="kv_fwd", grid=(1,),
        in_specs=[pl.BlockSpec((M, D), lambda i: (0, 0)), pl.BlockSpec((1, D), lambda i: (0, 0)),
                  pl.BlockSpec((nb, D, C), lambda i: (0, 0, 0))],
        out_specs=[pl.BlockSpec((M, D), lambda i: (0, 0)), pl.BlockSpec((M, nb * C), lambda i: (0, 0))],
        out_shape=[jax.ShapeDtypeStruct((M, D), MXU_DTYPE), jax.ShapeDtypeStruct((M, nb * C), MXU_DTYPE)],
        compiler_params=_params("arbitrary"))(mem, g, w_kv)


def _softmax_rows(s):
    e = jnp.exp(s - jnp.max(s, axis=-1, keepdims=True))
    return e / jnp.sum(e, axis=-1, keepdims=True)


def _attn_fwd(q, kv, x1, w_o, g_ffn, comm=None):
    S, D = x1.shape
    M = kv.shape[0]
    hd = D // XA_HEADS
    scale = hd ** -0.5
    ts = _tile(S, 1024)

    def body(q_ref, kv_ref, x1_ref, wo_ref, g_ref, o_ref, x2_ref, h3_ref):
        def scores(h):
            cols = pl.ds(h * hd, hd)
            return _mm_nt(q_ref[:, cols], kv_ref[:, cols]) * scale

        s_next = scores(0)
        for h in range(XA_HEADS):
            s = s_next
            if h + 1 < XA_HEADS:
                s_next = scores(h + 1)
            p = _softmax_rows(s)
            o_ref[:, pl.ds(h * hd, hd)] = _mm(p, kv_ref[:, pl.ds(D + h * hd, hd)]).astype(MXU_DTYPE)
        x2 = x1_ref[...] + jnp.dot(o_ref[...], wo_ref[...], preferred_element_type=F32)
        x2_ref[...] = x2
        _, xhat = _rms(x2)
        h3_ref[...] = (xhat * g_ref[...]).astype(MXU_DTYPE)

    row = pl.BlockSpec((ts, D), lambda i: (i, 0))
    return _pcall(
        comm, body, name="attn_fwd", grid=(S // ts,),
        in_specs=[row, pl.BlockSpec((M, 2 * D), lambda i: (0, 0)), row, pl.BlockSpec((D, D), lambda i: (0, 0)),
                  pl.BlockSpec((1, D), lambda i: (0, 0))],
        out_specs=[row, row, row],
        out_shape=[jax.ShapeDtypeStruct((S, D), MXU_DTYPE), jax.ShapeDtypeStruct((S, D), F32),
                   jax.ShapeDtypeStruct((S, D), MXU_DTYPE)],
        semantics=("parallel",))(q, kv, x1, w_o, g_ffn)


def _ffn_fwd(h3, w_up, conv_w, conv_b, w_down, x2, g_final, target, comm=None):
    S, D = h3.shape
    nb, _, CW = w_up.shape
    half = nb // 2
    cb = 768
    per = CW // cb
    J = half * per
    ts = _tile(S, 256)
    taps = conv_w.shape[0]
    halo = 8

    def body(h_ref, wup_ref, cw_ref, cb_ref, wd_ref, x2_ref, gf_ref, t_ref,
             gu_ref, act_ref, dx3_ref, loss_ref, dgf_ref, gbuf):
        i = pl.program_id(0)

        @pl.when(i == 0)
        def _():
            for ref in (loss_ref, dgf_ref, gbuf):
                ref[...] = jnp.zeros_like(ref)

        hv = h_ref[...]
        x3 = x2_ref[...]
        def up(j):
            b, cols = j // per, pl.ds((j % per) * cb, cb)
            return (jnp.dot(hv, wup_ref[b, :, cols], preferred_element_type=F32),
                    jnp.dot(hv, wup_ref[half + b, :, cols], preferred_element_type=F32))

        ahead = up(0)
        for j in range(J):
            b, cols, wcols = j // per, pl.ds((j % per) * cb, cb), pl.ds(j * cb, cb)
            g, u = ahead
            if j + 1 < J:
                ahead = up(j + 1)
            gu_ref[0, b, :, cols] = g
            gu_ref[1, b, :, cols] = u
            gbuf[j, pl.ds(halo, ts), :] = g
            gc = _causal_taps(gbuf.at[j], halo, cw_ref, taps, ts, wcols=wcols) + cb_ref[:, wcols]
            gbuf[j, pl.ds(0, halo), :] = gbuf[j, pl.ds(ts, halo), :]
            ge, _ = _gelu(gc)
            act = (ge * u).astype(MXU_DTYPE)
            act_ref[j] = act
            x3 = x3 + jnp.dot(act, wd_ref[j], preferred_element_type=F32)
        rinv, xhat = _rms(x3)
        gf = gf_ref[...]
        diff = xhat * gf - t_ref[...]
        loss_ref[...] += _colsum(diff * diff) * (0.5 / D)
        dy = diff * (1.0 / D)
        dgf_ref[...] += _colsum(dy * xhat)
        dx3_ref[...] = _rms_bwd(rinv, xhat, dy * gf)

    row = pl.BlockSpec((ts, D), lambda i: (i, 0))
    vecd = pl.BlockSpec((1, D), lambda i: (0, 0))
    once = pl.Buffered(1)
    sds = jax.ShapeDtypeStruct
    res = _pcall(
        comm, body, name="ffn_fwd", grid=(S // ts,),
        in_specs=[row, pl.BlockSpec((nb, D, CW), lambda i: (0, 0, 0), pipeline_mode=once),
                  pl.BlockSpec((taps, half * CW), lambda i: (0, 0)), pl.BlockSpec((1, half * CW), lambda i: (0, 0)),
                  pl.BlockSpec((J, cb, D), lambda i: (0, 0, 0), pipeline_mode=once), row, vecd, row],
        out_specs=[pl.BlockSpec((2, half, ts, CW), lambda i: (0, 0, i, 0)),
                   pl.BlockSpec((J, ts, cb), lambda i: (0, i, 0)), row, vecd, vecd],
        out_shape=[sds((2, half, S, CW), F32), sds((J, S, cb), MXU_DTYPE), sds((S, D), F32),
                   sds((1, D), F32), sds((1, D), F32)],
        scratch_shapes=[pltpu.VMEM((J, ts + halo, cb), F32)],
        semantics=("arbitrary",))(h3, w_up, conv_w, conv_b, w_down.reshape(J, cb, D), x2, g_final, target)
    outs = res if comm is None else res[0]
    outs = [outs[0].reshape(nb, S, CW)] + list(outs[1:])
    return outs if comm is None else (outs, res[1])


def _ffn_bwd(dx3, w_down, w_up, gu, x2, g_ffn, conv_w, conv_b, comm=None):
    nb, S, CW = gu.shape
    half = nb // 2
    D = dx3.shape[1]
    cb = 768
    per = CW // cb
    J = half * per
    ts = _tile(S, 256)
    n = S // ts
    taps = conv_w.shape[0]
    halo = 8
    hb = ts // halo

    def body(dx_ref, x2_ref, gf_ref, wd_ref, wup_ref, gu_ref, gh_ref, cw_ref, cb_ref,
             dgu_ref, dx2_ref, dgf_ref, dcw_ref, dcb_ref, gbuf, dbuf):
        i = pl.program_id(0)
        r = n - 1 - i

        @pl.when(i == 0)
        def _():
            for ref in (dgf_ref, dcw_ref, dcb_ref, dbuf):
                ref[...] = jnp.zeros_like(ref)

        dx3v = dx_ref[...]
        dxb = dx3v.astype(MXU_DTYPE)
        dacts = [_mm_nt(dxb, wd_ref[j]) for j in range(J)]
        dh = None
        for j in range(J):
            b, cols, wcols = j // per, pl.ds((j % per) * cb, cb), pl.ds(j * cb, cb)
            dact = dacts[j]
            gbuf[pl.ds(0, halo), :] = jnp.where(r > 0, gh_ref[0, b, :, cols], 0.0)
            gbuf[pl.ds(halo, ts), :] = gu_ref[0, b, :, cols]
            gs = _windows(gbuf, halo, taps, ts)
            gc = _causal_from(gs, cw_ref, wcols) + cb_ref[:, wcols]
            ge, dge = _gelu(gc)
            dub = (dact * ge).astype(MXU_DTYPE)
            dgc = dact * gu_ref[1, b, :, cols] * dge
            dcb_ref[:, wcols] += _colsum(dgc)
            dbuf[j, pl.ds(0, ts), :] = dgc
            _tap_grads_from(dcw_ref, dgc, gs, wcols)
            dgb = _anticausal_taps(dbuf.at[j], cw_ref, taps, ts, wcols=wcols).astype(MXU_DTYPE)
            dbuf[j, pl.ds(ts, halo), :] = dbuf[j, pl.ds(0, halo), :]
            dgu_ref[0, b, :, cols] = dgb
            dgu_ref[1, b, :, cols] = dub
            part = _mm_nt(dgb, wup_ref[b, :, cols]) + _mm_nt(dub, wup_ref[half + b, :, cols])
            dh = part if dh is None else dh + part
        rinv, xhat = _rms(x2_ref[...])
        dgf_ref[...] += _colsum(dh * xhat)
        dx2_ref[...] = dx3v + _rms_bwd(rinv, xhat, dh * gf_ref[...])

    gu2 = gu.reshape(2, half, S, CW)
    row = pl.BlockSpec((ts, D), lambda i: (n - 1 - i, 0))
    vecd = pl.BlockSpec((1, D), lambda i: (0, 0))
    pair = pl.BlockSpec((2, half, ts, CW), lambda i: (0, 0, n - 1 - i, 0))
    g_prev = pl.BlockSpec((1, half, halo, CW), lambda i: (0, 0, jnp.maximum((n - 1 - i) * hb - 1, 0), 0))
    tapw = pl.BlockSpec((taps, half * CW), lambda i: (0, 0))
    vec = pl.BlockSpec((1, half * CW), lambda i: (0, 0))
    once = pl.Buffered(1)
    sds = jax.ShapeDtypeStruct
    res = _pcall(
        comm, body, name="ffn_bwd", grid=(n,),
        in_specs=[row, row, vecd, pl.BlockSpec((J, cb, D), lambda i: (0, 0, 0), pipeline_mode=once),
                  pl.BlockSpec((nb, D, CW), lambda i: (0, 0, 0), pipeline_mode=once), pair, g_prev, tapw, vec],
        out_specs=[pair, row, vecd, tapw, vec],
        out_shape=[sds((2, half, S, CW), MXU_DTYPE), sds((S, D), F32), sds((1, D), F32),
                   sds((taps, half * CW), F32), sds((1, half * CW), F32)],
        scratch_shapes=[pltpu.VMEM((ts + halo, cb), F32), pltpu.VMEM((J, ts + halo, cb), F32)],
        semantics=("arbitrary",))(dx3, x2, g_ffn, w_down.reshape(J, cb, D), w_up, gu2, gu2, conv_w, conv_b)
    outs = res if comm is None else res[0]
    outs = [outs[0].reshape(nb, S, CW)] + list(outs[1:])
    return outs if comm is None else (outs, res[1])


def _attn_bwd(dx2, w_o, q, kv, x1, g_xa, w_q, comm=None):
    S, D = x1.shape
    M = kv.shape[0]
    hd = D // XA_HEADS
    scale = hd ** -0.5
    ts = _tile(S, 1024)

    def body(dx2_ref, wo_ref, q_ref, kv_ref, x1_ref, g_ref, wq_ref, dq_ref, dx1_ref, dkv_ref, dg_ref):
        i = pl.program_id(0)

        @pl.when(i == 0)
        def _():
            dkv_ref[...] = jnp.zeros_like(dkv_ref)
            dg_ref[...] = jnp.zeros_like(dg_ref)

        dx2 = dx2_ref[...]
        do = _mm_nt(dx2, wo_ref[...]).astype(MXU_DTYPE)
        def scores(h):
            cols = pl.ds(h * hd, hd)
            doh = do[:, h * hd:(h + 1) * hd]
            return (_mm_nt(q_ref[:, cols], kv_ref[:, cols]) * scale,
                    _mm_nt(doh, kv_ref[:, pl.ds(D + h * hd, hd)]), doh)

        ahead = scores(0)
        for h in range(XA_HEADS):
            cols = pl.ds(h * hd, hd)
            vcols = pl.ds(D + h * hd, hd)
            s, dp, doh = ahead
            if h + 1 < XA_HEADS:
                ahead = scores(h + 1)
            p = _softmax_rows(s)
            ds = (p * (dp - jnp.sum(dp * p, axis=-1, keepdims=True)) * scale).astype(MXU_DTYPE)
            dkv_ref[:, vcols] += _mm_tn(p, doh)
            dq_ref[:, cols] = _mm(ds, kv_ref[:, cols]).astype(MXU_DTYPE)
            dkv_ref[:, cols] += _mm_tn(ds, q_ref[:, cols])
        dh2 = _mm_nt(dq_ref[...], wq_ref[...])
        rinv, xhat = _rms(x1_ref[...])
        dg_ref[...] += _colsum(dh2 * xhat)
        dx1_ref[...] = dx2 + _rms_bwd(rinv, xhat, dh2 * g_ref[...])

    row = pl.BlockSpec((ts, D), lambda i: (i, 0))
    mat = pl.BlockSpec((D, D), lambda i: (0, 0))
    vecd = pl.BlockSpec((1, D), lambda i: (0, 0))
    kvs = pl.BlockSpec((M, 2 * D), lambda i: (0, 0))
    return _pcall(
        comm, body, name="attn_bwd", grid=(S // ts,),
        in_specs=[row, mat, row, kvs, row, vecd, mat],
        out_specs=[row, row, kvs, vecd],
        out_shape=[jax.ShapeDtypeStruct((S, D), MXU_DTYPE), jax.ShapeDtypeStruct((S, D), F32),
                   jax.ShapeDtypeStruct((M, 2 * D), F32), jax.ShapeDtypeStruct((1, D), F32)],
        semantics=("arbitrary",))(dx2, w_o, q, kv, x1, g_xa, w_q)


def _kv_bwd(dkv, w_kv, mem, g, m):
    M, D = mem.shape
    nb, _, C = w_kv.shape

    def body(dkv_ref, w_ref, mem_ref, m_ref, dw_ref, dg_ref):
        dm = jnp.zeros((M, D), F32)
        for j in range(nb):
            dj = dkv_ref[:, pl.ds(j * C, C)].astype(MXU_DTYPE)
            dw_ref[j] = _mm_tn(m_ref[...], dj).astype(dw_ref.dtype)
            dm = dm + _mm_nt(dj, w_ref[j])
        _, xhat = _rms(mem_ref[...])
        dg_ref[...] = _colsum(dm * xhat)

    full = lambda *s: pl.BlockSpec(s, lambda i: (0,) * len(s))
    return pl.pallas_call(
        body, name="kv_bwd", grid=(1,),
        in_specs=[full(M, nb * C), full(nb, D, C), full(M, D), full(M, D)],
        out_specs=[full(nb, D, C), full(1, D)],
        out_shape=[jax.ShapeDtypeStruct((nb, D, C), WIRE_DTYPE), jax.ShapeDtypeStruct((1, D), F32)],
        compiler_params=_params("arbitrary"))(dkv, w_kv, mem, m)


def _conf_bwd(dx1, w_out_c, z, c1, conv_w, ln_g, ln_b, comm=None):
    _, S, C = z.shape
    D = dx1.shape[1]
    ts = _tile(S, 256)
    n = S // ts
    taps = conv_w.shape[0]
    halo = 32
    hb = ts // halo

    def body(dx_ref, wo_ref, za_ref, zb_ref, zah_ref, zbh_ref, c1_ref, cw_ref, g_ref, b_ref,
             dz_ref, dcw_ref, dcb_ref, dlg_ref, dlb_ref, c0buf, dbuf, shifted):
        i = pl.program_id(0)
        r = n - 1 - i

        @pl.when(i == 0)
        def _():
            for ref in (dcw_ref, dcb_ref, dlg_ref, dlb_ref):
                ref[...] = jnp.zeros_like(ref)
            dbuf[pl.ds(ts, halo), :] = jnp.zeros((halo, C), F32)

        za = za_ref[0]
        sb = _sigmoid(zb_ref[0])
        c0buf[pl.ds(0, halo), :] = jnp.where(r > 0, zah_ref[0] * _sigmoid(zbh_ref[0]), 0.0)
        c0buf[pl.ds(halo, ts), :] = za * sb
        dc3 = _mm_nt(dx_ref[...], wo_ref[...])
        rstd, xhat = _layer_norm_stats(c1_ref[...])
        g = g_ref[...]
        c2 = xhat * g + b_ref[...]
        sg = _sigmoid(c2)
        dc2 = dc3 * sg * (1.0 + c2 * (1.0 - sg))
        dlg_ref[...] += _colsum(dc2 * xhat)
        dlb_ref[...] += _colsum(dc2)
        dxh = dc2 * g
        dc1 = rstd * (dxh - jnp.mean(dxh, axis=-1, keepdims=True)
                      - xhat * jnp.mean(dxh * xhat, axis=-1, keepdims=True))
        dcb_ref[...] += _colsum(dc1)
        dbuf[pl.ds(0, ts), :] = dc1
        _shift_copies(shifted, c0buf, ts + halo - 8, up=False)
        _tap_grads8(dcw_ref, dc1, shifted, halo, taps, ts)
        _shift_copies(shifted, dbuf, ts + halo - 8, up=True)
        dc0 = _anticausal_taps8(shifted, cw_ref, taps, ts)
        dz_ref[0] = (dc0 * sb).astype(MXU_DTYPE)
        dz_ref[1] = (dc0 * za * sb * (1.0 - sb)).astype(MXU_DTYPE)
        dbuf[pl.ds(ts, halo), :] = dbuf[pl.ds(0, halo), :]

    vec = pl.BlockSpec((1, C), lambda i: (0, 0))
    tapw = pl.BlockSpec((taps, C), lambda i: (0, 0))
    tile = lambda b: pl.BlockSpec((1, ts, C), lambda i: (b, n - 1 - i, 0))
    prev = lambda b: pl.BlockSpec((1, halo, C), lambda i: (b, jnp.maximum((n - 1 - i) * hb - 1, 0), 0))
    return _pcall(
        comm, body, name="conf_bwd", grid=(n,),
        in_specs=[pl.BlockSpec((ts, D), lambda i: (n - 1 - i, 0)), pl.BlockSpec((C, D), lambda i: (0, 0)),
                  tile(2), tile(3), prev(2), prev(3), pl.BlockSpec((ts, C), lambda i: (n - 1 - i, 0)),
                  tapw, vec, vec],
        out_specs=[pl.BlockSpec((2, ts, C), lambda i: (1, n - 1 - i, 0)), tapw, vec, vec, vec],
        out_shape=[jax.ShapeDtypeStruct((4, S, C), MXU_DTYPE), jax.ShapeDtypeStruct((taps, C), F32),
                   jax.ShapeDtypeStruct((1, C), F32), jax.ShapeDtypeStruct((1, C), F32),
                   jax.ShapeDtypeStruct((1, C), F32)],
        scratch_shapes=[pltpu.VMEM((ts + halo, C), F32), pltpu.VMEM((ts + halo, C), F32),
                        pltpu.VMEM((8, ts + halo - 8, C), F32)],
        semantics=("arbitrary",))(dx1, w_out_c, z, z, z, z, c1, conv_w, ln_g, ln_b)


def _lru_bwd(dx1, w_out_l, z, h, conv_w, conv_b, wa, ba, wx, bx, lam, dz, comm=None):
    _, S, C = z.shape
    D = dx1.shape[1]
    ts = _tile(S, 256)
    n = S // ts
    taps = conv_w.shape[0]
    halo = 8
    hb = ts // halo

    def body(dx_ref, wo_ref, zx_ref, zxh_ref, zg_ref, h_ref, hh_ref, cw_ref, cb_ref, wa_ref, ba_ref,
             wx_ref, bx_ref, lam_ref, dz_in,
             dz_ref, dwa_ref, dwx_ref, dba_ref, dbx_ref, dlam_ref, dcw_ref, dcb_ref,
             xbuf, hbuf, a_s, w_s, dh_s, g_s, dbuf, pc):
        i = pl.program_id(0)
        r = n - 1 - i

        @pl.when(i == 0)
        def _():
            for ref in (dwa_ref, dwx_ref, dba_ref, dbx_ref, dlam_ref, dcw_ref, dcb_ref, pc):
                ref[...] = jnp.zeros_like(ref)
            dbuf[pl.ds(ts, halo), :] = jnp.zeros((halo, C), F32)

        xbuf[pl.ds(0, halo), :] = jnp.where(r > 0, zxh_ref[0], 0.0)
        xbuf[pl.ds(halo, ts), :] = zx_ref[0]
        hbuf[pl.ds(0, halo), :] = jnp.where(r > 0, hh_ref[...], 0.0)
        hbuf[pl.ds(halo, ts), :] = h_ref[...]
        xs = _windows(xbuf, halo, taps, ts)
        xc = _causal_from(xs, cw_ref) + cb_ref[...]
        lam_v = lam_ref[...]
        sp = _softplus_neg(lam_v)
        rg, ig, a, mult, inv_mult = _lru_gates(xc, wa_ref, ba_ref, wx_ref, bx_ref, sp)

        dy = _mm_nt(dx_ref[...], wo_ref[...])
        ge, dge = _gelu(zg_ref[0])
        dh = dy * ge
        dz_ref[1] = (dy * h_ref[...] * dge).astype(MXU_DTYPE)
        a_s[...] = a
        w_s[...] = a * dh
        dh_s[...] = dh
        row = lax.broadcasted_iota(jnp.int32, (8, C), 0)

        def step(kk, carry):
            off = pl.multiple_of((ts // 8 - 1 - kk) * 8, 8)
            av = a_s[pl.ds(off, 8), :]
            wv = w_s[pl.ds(off, 8), :]
            for d in (1, 2, 4):
                m = row < 8 - d
                a_sh = jnp.where(m, pltpu.roll(av, 8 - d, 0), 1.0)
                w_sh = jnp.where(m, pltpu.roll(wv, 8 - d, 0), 0.0)
                wv = wv + av * w_sh
                av = av * a_sh
            pv = wv + av * carry
            g_s[pl.ds(off, 8), :] = dh_s[pl.ds(off, 8), :] + jnp.where(row < 7, pltpu.roll(pv, 7, 0), carry)
            return jnp.broadcast_to(pv[0:1, :], (8, C))

        pc[...] = lax.fori_loop(0, ts // 8, step, pc[...], unroll=4)
        gt = g_s[...]
        da = gt * hbuf[pl.ds(halo - 1, ts), :]
        gm = gt * mult
        dlog_a = da * a - (gt * ig * xc) * (a * a) * inv_mult
        dlam_ref[...] += _colsum(dlog_a * rg) * (RG_C / (1.0 + jnp.exp(lam_v)))
        dpa = (dlog_a * (-RG_C * sp)) * rg * (1.0 - rg)
        dpx = (gm * xc) * ig * (1.0 - ig)
        dba_ref[...] += _colsum(dpa)
        dbx_ref[...] += _colsum(dpx)
        xb = xc.astype(MXU_DTYPE)
        dpab, dpxb = dpa.astype(MXU_DTYPE), dpx.astype(MXU_DTYPE)
        dwa_ref[...] += _mm_tn(xb, dpab)
        dwx_ref[...] += _mm_tn(xb, dpxb)
        dxc = gm * ig + _mm_nt(dpab, wa_ref[...]) + _mm_nt(dpxb, wx_ref[...])
        dcb_ref[...] += _colsum(dxc)
        dbuf[pl.ds(0, ts), :] = dxc
        _tap_grads_from(dcw_ref, dxc, xs)
        dz_ref[0] = _anticausal_taps(dbuf, cw_ref, taps, ts).astype(MXU_DTYPE)
        dbuf[pl.ds(ts, halo), :] = dbuf[pl.ds(0, halo), :]

    vec = pl.BlockSpec((1, C), lambda i: (0, 0))
    mat = pl.BlockSpec((C, C), lambda i: (0, 0))
    tapw = pl.BlockSpec((taps, C), lambda i: (0, 0))
    prev_rows = lambda i: jnp.maximum((n - 1 - i) * hb - 1, 0)
    sds = jax.ShapeDtypeStruct
    return _pcall(
        comm, body, name="lru_bwd", grid=(n,),
        in_specs=[pl.BlockSpec((ts, D), lambda i: (n - 1 - i, 0)), pl.BlockSpec((C, D), lambda i: (0, 0)),
                  pl.BlockSpec((1, ts, C), lambda i: (0, n - 1 - i, 0)),
                  pl.BlockSpec((1, halo, C), lambda i: (0, prev_rows(i), 0)),
                  pl.BlockSpec((1, ts, C), lambda i: (1, n - 1 - i, 0)),
                  pl.BlockSpec((ts, C), lambda i: (n - 1 - i, 0)),
                  pl.BlockSpec((halo, C), lambda i: (prev_rows(i), 0)),
                  tapw, vec, mat, vec, mat, vec, vec, ANY],
        out_specs=[pl.BlockSpec((2, ts, C), lambda i: (0, n - 1 - i, 0)), mat, mat, vec, vec, vec, tapw, vec],
        out_shape=[sds(dz.shape, MXU_DTYPE), sds((C, C), F32), sds((C, C), F32), sds((1, C), F32),
                   sds((1, C), F32), sds((1, C), F32), sds((taps, C), F32), sds((1, C), F32)],
        scratch_shapes=[pltpu.VMEM((ts + halo, C), F32), pltpu.VMEM((ts + halo, C), F32)]
        + [pltpu.VMEM((ts, C), F32)] * 4 + [pltpu.VMEM((ts + halo, C), F32), pltpu.VMEM((8, C), F32)],
        aliases={14: 0},
        semantics=("arbitrary",))(dx1, w_out_l, z, z, z, h, h, conv_w, conv_b, wa, ba, wx, bx, lam, dz)


def _bwd_in(dz, w_in, x, g, dx1):
    S, D = x.shape
    nb, _, C = w_in.shape
    ts = _tile(S, 512)
    halves = 2 if ts % 32 == 0 else 1
    hr = ts // halves

    def body(dz_ref, w_ref, x_ref, g_ref, dx1_ref, dx_ref, dg_ref):
        i = pl.program_id(0)

        @pl.when(i == 0)
        def _():
            dg_ref[...] = jnp.zeros_like(dg_ref)

        def grad_h(k):
            rows = pl.ds(k * hr, hr)
            dh = _mm_nt(dz_ref[0, rows, :], w_ref[0])
            for j in range(1, nb):
                dh = dh + _mm_nt(dz_ref[j, rows, :], w_ref[j])
            return dh

        ahead = grad_h(0)
        for k in range(halves):
            rows, dh = pl.ds(k * hr, hr), ahead
            if k + 1 < halves:
                ahead = grad_h(k + 1)
            rinv, xhat = _rms(x_ref[rows, :])
            dg_ref[...] += _colsum(dh * xhat)
            dx_ref[rows, :] = dx1_ref[rows, :] + _rms_bwd(rinv, xhat, dh * g_ref[...])

    row = pl.BlockSpec((ts, D), lambda i: (i, 0))
    vecd = pl.BlockSpec((1, D), lambda i: (0, 0))
    return pl.pallas_call(
        body, name="bwd_in", grid=(S // ts,),
        in_specs=[pl.BlockSpec((nb, ts, C), lambda i: (0, i, 0)), pl.BlockSpec((nb, D, C), lambda i: (0, 0, 0)),
                  row, vecd, row],
        out_specs=[row, vecd],
        out_shape=[jax.ShapeDtypeStruct((S, D), F32), jax.ShapeDtypeStruct((1, D), F32)],
        compiler_params=_params("arbitrary"))(dz, w_in, x, g, dx1)


def _wgrad(a, b, name, comm=None):
    na, S, K = a.shape
    nb, _, N = b.shape
    nj = max(na, nb)
    assert min(na, nb) == 1
    ts = _tile(S, 1024)
    ns = S // ts
    grp = max(g for g in range(1, nj + 1) if nj % g == 0 and g * K * N * 4 <= WGRAD_ACC_BYTES)
    ga, gb = (grp if na > 1 else 1), (grp if nb > 1 else 1)

    def body(a_ref, b_ref, o_ref, acc):
        s = pl.program_id(1)

        @pl.when(s == 0)
        def _():
            acc[...] = jnp.zeros_like(acc)

        for k in range(grp):
            acc[k] += _mm_tn(a_ref[k if na > 1 else 0], b_ref[k if nb > 1 else 0])

        @pl.when(s == ns - 1)
        def _():
            o_ref[...] = acc[...].astype(o_ref.dtype)

    res = _pcall(
        comm, body, name=name, grid=(nj // grp, ns),
        in_specs=[pl.BlockSpec((ga, ts, K), (lambda j, s: (j, s, 0)) if na > 1 else (lambda j, s: (0, s, 0))),
                  pl.BlockSpec((gb, ts, N), (lambda j, s: (j, s, 0)) if nb > 1 else (lambda j, s: (0, s, 0)))],
        out_specs=pl.BlockSpec((grp, K, N), lambda j, s: (j, 0, 0)),
        out_shape=jax.ShapeDtypeStruct((nj, K, N), WIRE_DTYPE),
        scratch_shapes=[pltpu.VMEM((grp, K, N), F32)],
        semantics=("parallel", "arbitrary"))(a, b)
    return res[0] if comm is None else (res[0][0], res[1])


def _place():
    x, y, c = lax.axis_index("x"), lax.axis_index("y"), lax.axis_index("c")
    other_chips = [(1 - x, y), (x, 1 - y), (1 - x, 1 - y)]
    return x, y, c, other_chips


def _gather_weights(shards):
    nt = len(shards)

    def body(*refs):
        src, dst = refs[:nt], refs[nt:2 * nt]
        ici_send, ici_recv, d2d_send, d2d_recv, own_send, own_recv = refs[2 * nt:]
        x, y, c, chips = _place()
        mine = 2 * x + y

        def half(t, pc):
            hr = src[t].shape[0] // 2
            return pl.ds(pc * hr, hr)

        def own(t):
            return pltpu.make_async_remote_copy(
                src_ref=src[t], dst_ref=dst[t].at[mine], send_sem=own_send.at[t], recv_sem=own_recv.at[t],
                device_id=(x, y, 1 - c), device_id_type=MESH)

        def ici(t, k, block, to):
            cx, cy = block
            ref = dst[t].at[2 * cx + cy, half(t, c)]
            return pltpu.make_async_remote_copy(
                src_ref=src[t].at[half(t, c)] if to is not None else ref, dst_ref=ref,
                send_sem=ici_send.at[t, k], recv_sem=ici_recv.at[t, k],
                device_id=(*to, c) if to is not None else (x, y, c), device_id_type=MESH)

        def d2d(t, k, block, pc):
            cx, cy = block
            ref = dst[t].at[2 * cx + cy, half(t, pc)]
            return pltpu.make_async_remote_copy(
                src_ref=ref, dst_ref=ref, send_sem=d2d_send.at[t, k], recv_sem=d2d_recv.at[t, k],
                device_id=(x, y, 1 - c), device_id_type=MESH)

        sends = [ici(t, k, (x, y), chip) for t in range(nt) for k, chip in enumerate(chips)]
        sends += [own(t) for t in range(nt)]
        for cp in sends:
            cp.start()
        passed = []
        for t in range(nt):
            for k, chip in enumerate(chips):
                ici(t, k, chip, None).wait_recv()
                fw = d2d(t, k, chip, c)
                fw.start()
                passed.append(fw)
        for t in range(nt):
            own(t).wait_recv()
            for k, chip in enumerate(chips):
                d2d(t, k, chip, 1 - c).wait_recv()
        for cp in sends + passed:
            cp.wait_send()

    return pl.pallas_call(
        body, name="gather_weights",
        in_specs=[ANY] * nt, out_specs=[ANY] * nt,
        out_shape=[jax.ShapeDtypeStruct((N_CHIPS,) + s.shape, s.dtype) for s in shards],
        scratch_shapes=[pltpu.SemaphoreType.DMA((nt, 3))] * 4 + [pltpu.SemaphoreType.DMA((nt,))] * 2,
        compiler_params=pltpu.CompilerParams(has_side_effects=True))(*shards)


def _gather_over_ici(shards):
    nt = len(shards)

    def copies(src, dst, scr, arriving):
        ici_send, ici_recv, own_send, own_recv = scr
        x, y, c, chips = _place()
        out = []
        for t in range(nt):
            hr = src[t].shape[0] // 2
            rows = pl.ds(c * hr, hr)
            for k, (cx, cy) in enumerate(chips):
                block = 2 * cx + cy if arriving else 2 * x + y
                out.append(pltpu.make_async_remote_copy(
                    src_ref=src[t].at[rows], dst_ref=dst[t].at[block, rows],
                    send_sem=ici_send.at[t, k], recv_sem=ici_recv.at[t, k],
                    device_id=(cx, cy, c), device_id_type=MESH))
            out.append(pltpu.make_async_remote_copy(
                src_ref=src[t], dst_ref=dst[t].at[2 * x + y], send_sem=own_send.at[t], recv_sem=own_recv.at[t],
                device_id=(x, y, 1 - c), device_id_type=MESH))
        return out

    def start(src, dst, scr):
        for cp in copies(src, dst, scr, False):
            cp.start()

    def finish(src, dst, scr):
        for cp in copies(src, dst, scr, True):
            cp.wait_recv()
        for cp in copies(src, dst, scr, False):
            cp.wait_send()

    return _Comm(shards, [jax.ShapeDtypeStruct((N_CHIPS,) + s.shape, s.dtype) for s in shards],
                 [pltpu.SemaphoreType.DMA((nt, 3))] * 2 + [pltpu.SemaphoreType.DMA((nt,))] * 2, start, finish)


def _gather_pass_on(bufs):
    nt = len(bufs)

    def passed(dst, scr, t, k, block, pc):
        send, recv = scr
        x, y, c, _ = _place()
        cx, cy = block
        hr = dst[t].shape[1] // 2
        ref = dst[t].at[2 * cx + cy, pl.ds(pc * hr, hr)]
        return pltpu.make_async_remote_copy(src_ref=ref, dst_ref=ref, send_sem=send.at[t, k], recv_sem=recv.at[t, k],
                                            device_id=(x, y, 1 - c), device_id_type=MESH)

    def start(src, dst, scr):
        _, _, c, chips = _place()
        for t in range(nt):
            for k, chip in enumerate(chips):
                passed(dst, scr, t, k, chip, c).start()

    def finish(src, dst, scr):
        _, _, c, chips = _place()
        for t in range(nt):
            for k, chip in enumerate(chips):
                passed(dst, scr, t, k, chip, 1 - c).wait_recv()
        for t in range(nt):
            for k, chip in enumerate(chips):
                passed(dst, scr, t, k, chip, c).wait_send()

    return _Comm(bufs, [jax.ShapeDtypeStruct(b.shape, b.dtype) for b in bufs],
                 [pltpu.SemaphoreType.DMA((nt, 3))] * 2, start, finish, aliases={t: t for t in range(nt)})


def _exchange_halves(grads):
    nt = len(grads)

    def copies(src, dst, scr):
        send, recv = scr
        x, y, c, _ = _place()
        out = []
        for t in range(nt):
            hr = src[t].shape[1] // 2
            out.append(pltpu.make_async_remote_copy(
                src_ref=src[t].at[:, pl.ds((1 - c) * hr, hr)], dst_ref=dst[t],
                send_sem=send.at[t], recv_sem=recv.at[t], device_id=(x, y, 1 - c), device_id_type=MESH))
        return out

    def start(src, dst, scr):
        for cp in copies(src, dst, scr):
            cp.start()

    def finish(src, dst, scr):
        for cp in copies(src, dst, scr):
            cp.wait()

    return _Comm(grads, [jax.ShapeDtypeStruct((g.shape[0], g.shape[1] // 2, g.shape[2]), g.dtype) for g in grads],
                 [pltpu.SemaphoreType.DMA((nt,))] * 2, start, finish)


def _add_halves(grad, other, name):
    nb, R, C = grad.shape
    hr = R // 2
    tr = _tile(hr, 256, 16)
    steps = hr // tr
    c = lax.axis_index("c").astype(jnp.int32).reshape((1,))

    def body(c_ref, a_ref, b_ref, o_ref):
        o_ref[...] = (a_ref[...].astype(F32) + b_ref[...].astype(F32)).astype(o_ref.dtype)

    return pl.pallas_call(
        body, name=name,
        grid_spec=pltpu.PrefetchScalarGridSpec(
            num_scalar_prefetch=1, grid=(nb, steps),
            in_specs=[pl.BlockSpec((1, tr, C), lambda j, i, c_ref: (j, c_ref[0] * steps + i, 0)),
                      pl.BlockSpec((1, tr, C), lambda j, i, c_ref: (j, i, 0))],
            out_specs=pl.BlockSpec((1, tr, C), lambda j, i, c_ref: (j, i, 0))),
        out_shape=jax.ShapeDtypeStruct((nb, hr, C), grad.dtype),
        compiler_params=_params("parallel", "parallel"))(c, grad, other)


def _scatter_chip_sums(parts):
    nt = len(parts)

    def copies(src, dst, scr):
        send, recv = scr
        x, y, c, chips = _place()
        out = []
        for t in range(nt):
            for k, (cx, cy) in enumerate(chips):
                out.append(pltpu.make_async_remote_copy(
                    src_ref=src[t].at[2 * cx + cy], dst_ref=dst[t].at[k],
                    send_sem=send.at[t, k], recv_sem=recv.at[t, k], device_id=(cx, cy, c), device_id_type=MESH))
        return out

    def start(src, dst, scr):
        for cp in copies(src, dst, scr):
            cp.start()

    def finish(src, dst, scr):
        for cp in copies(src, dst, scr):
            cp.wait()

    return _Comm(parts, [jax.ShapeDtypeStruct((3,) + p.shape[1:], p.dtype) for p in parts],
                 [pltpu.SemaphoreType.DMA((nt, 3))] * 2, start, finish)


def _sum_chips(part, recv, name):
    _, hr, C = part.shape
    tr = _tile(hr, 256, 16)
    steps = hr // tr
    where = jnp.stack([2 * lax.axis_index("x") + lax.axis_index("y"), lax.axis_index("c")]).astype(jnp.int32)

    def body(w_ref, a_ref, b_ref, o_ref):
        acc = a_ref[0].astype(F32)
        for k in range(3):
            acc = acc + b_ref[k].astype(F32)
        o_ref[...] = acc

    return pl.pallas_call(
        body, name=name,
        grid_spec=pltpu.PrefetchScalarGridSpec(
            num_scalar_prefetch=1, grid=(steps,),
            in_specs=[pl.BlockSpec((1, tr, C), lambda i, w_ref: (w_ref[0], i, 0)),
                      pl.BlockSpec((3, tr, C), lambda i, w_ref: (0, i, 0))],
            out_specs=pl.BlockSpec((tr, C), lambda i, w_ref: (w_ref[1] * steps + i, 0))),
        out_shape=jax.ShapeDtypeStruct((2 * hr, C), F32),
        compiler_params=_params("parallel"))(where, part, recv)


def _join_halves(bufs):
    nt = len(bufs)

    def swap(dst, scr, t, pc):
        send, recv = scr
        x, y, c, _ = _place()
        hr = dst[t].shape[0] // 2
        rows = dst[t].at[pl.ds(pc * hr, hr)]
        return pltpu.make_async_remote_copy(src_ref=rows, dst_ref=rows, send_sem=send.at[t], recv_sem=recv.at[t],
                                            device_id=(x, y, 1 - c), device_id_type=MESH)

    def start(src, dst, scr):
        c = lax.axis_index("c")
        for t in range(nt):
            swap(dst, scr, t, c).start()

    def finish(src, dst, scr):
        c = lax.axis_index("c")
        for t in range(nt):
            swap(dst, scr, t, 1 - c).wait_recv()
        for t in range(nt):
            swap(dst, scr, t, c).wait_send()

    return _Comm(bufs, [jax.ShapeDtypeStruct(b.shape, b.dtype) for b in bufs],
                 [pltpu.SemaphoreType.DMA((nt,))] * 2, start, finish, aliases={t: t for t in range(nt)})


def _reduce_scatter_in_vmem(g):
    nb, R, C = g.shape
    hr = R // 2

    def run(ins, outs, scr):
        (g_ref,), (out_ref,) = ins, outs
        other, part, got, send, recv = scr
        x, y, c, chips = _place()
        sibling = (x, y, 1 - c)
        my_rows = pl.ds(pl.multiple_of(c * hr, hr), hr)
        their_rows = pl.ds(pl.multiple_of((1 - c) * hr, hr), hr)
        swap = pltpu.make_async_remote_copy(src_ref=g_ref.at[:, their_rows], dst_ref=other, send_sem=send.at[0],
                                            recv_sem=recv.at[0], device_id=sibling, device_id_type=MESH)
        swap.start()
        swap.wait()
        part[...] = (g_ref[:, my_rows, :].astype(F32) + other[...].astype(F32)).astype(part.dtype)
        to_owner = [pltpu.make_async_remote_copy(src_ref=part.at[2 * cx + cy], dst_ref=got.at[k],
                                                 send_sem=send.at[1 + k], recv_sem=recv.at[1 + k],
                                                 device_id=(cx, cy, c), device_id_type=MESH)
                    for k, (cx, cy) in enumerate(chips)]
        for cp in to_owner:
            cp.start()
        for cp in to_owner:
            cp.wait()
        total = part[2 * x + y].astype(F32)
        for k in range(3):
            total = total + got[k].astype(F32)
        out_ref[my_rows, :] = total

        def join(rows):
            return pltpu.make_async_remote_copy(src_ref=out_ref.at[rows], dst_ref=out_ref.at[rows], send_sem=send.at[4],
                                                recv_sem=recv.at[4], device_id=sibling, device_id_type=MESH)

        join(my_rows).start()
        join(their_rows).wait_recv()
        join(my_rows).wait_send()

    return _Comm([g], [jax.ShapeDtypeStruct((R, C), F32)],
                 [pltpu.VMEM((nb, hr, C), g.dtype), pltpu.VMEM((nb, hr, C), g.dtype), pltpu.VMEM((3, hr, C), g.dtype),
                  pltpu.SemaphoreType.DMA((5,)), pltpu.SemaphoreType.DMA((5,))],
                 run, lambda ins, outs, scr: None, in_specs=[WHOLE_VMEM], out_specs=[WHOLE_VMEM])


def _all_reduce_rows(buf, loss_row=None):
    R, L = buf.shape

    def copies(in_ref, gath, send, recv):
        x, y, c, _ = _place()
        out = []
        for k in range(1, N_DEV):
            peer = (x ^ ((k >> 2) & 1), y ^ ((k >> 1) & 1), c ^ (k & 1))
            out.append(pltpu.make_async_remote_copy(
                src_ref=in_ref, dst_ref=gath.at[k], send_sem=send.at[k - 1], recv_sem=recv.at[k - 1],
                device_id=peer, device_id_type=MESH))
        return out

    def start(ins, outs, scr):
        gath, send, recv = scr
        gath[0] = ins[0][...]
        for cp in copies(ins[0], gath, send, recv):
            cp.start()

    def finish(ins, outs, scr):
        gath, send, recv = scr
        for cp in copies(ins[0], gath, send, recv):
            cp.wait()
        x, y, c, _ = _place()
        me = 4 * x + 2 * y + c
        total = gath[me]
        for d in range(1, N_DEV):
            total = total + gath[d ^ me]
        outs[0][...] = total
        if loss_row is not None:
            outs[1][...] = jnp.sum(total[loss_row:loss_row + 1, :], axis=1, keepdims=True)

    out_shape = [jax.ShapeDtypeStruct((R, L), F32)]
    if loss_row is not None:
        out_shape.append(jax.ShapeDtypeStruct((1, 1), F32))
    return _Comm([buf], out_shape,
                 [pltpu.VMEM((N_DEV, R, L), F32), pltpu.SemaphoreType.DMA((N_DEV - 1,)),
                  pltpu.SemaphoreType.DMA((N_DEV - 1,))],
                 start, finish, in_specs=[WHOLE_VMEM], out_specs=[WHOLE_VMEM] * len(out_shape))


def _adamw_update(w_ref, g_ref, m_ref, v_ref, d_ref, nm_ref, nv_ref):
    gv = g_ref[...]
    nm = ADAM_B1 * m_ref[...] + (1.0 - ADAM_B1) * gv
    nv = ADAM_B2 * v_ref[...] + (1.0 - ADAM_B2) * (gv * gv)
    nm_ref[...] = nm
    nv_ref[...] = nv
    m_hat = nm / (1.0 - ADAM_B1 ** ADAM_STEP)
    v_hat = nv / (1.0 - ADAM_B2 ** ADAM_STEP)
    d_ref[...] = -ADAM_LR * (m_hat / (jnp.sqrt(v_hat) + ADAM_EPS) + ADAM_WD * w_ref[...])


def _adamw(w, g, m, v, name):
    R, C = w.shape
    tr = _tile(R, 256)

    def body(w_ref, g_ref, m_ref, v_ref, d_ref, nm_ref, nv_ref, g_out):
        _adamw_update(w_ref, g_ref, m_ref, v_ref, d_ref, nm_ref, nv_ref)
        g_out[...] = g_ref[...]

    blk = pl.BlockSpec((tr, C), lambda i: (i, 0))
    return pl.pallas_call(
        body, name=name, grid=(R // tr,), in_specs=[blk] * 4, out_specs=[blk] * 4,
        out_shape=[jax.ShapeDtypeStruct((R, C), F32)] * 4,
        compiler_params=_params("parallel"))(w, g, m, v)


def _adamw_many(ws, gs, ms, vs, name):
    n = len(ws)

    def body(*refs):
        for k in range(n):
            _adamw_update(*[refs[part * n + k] for part in range(7)])

    shapes = [jax.ShapeDtypeStruct(w.shape, F32) for w in ws]
    outs = pl.pallas_call(
        body, name=name, in_specs=[WHOLE_VMEM] * (4 * n), out_specs=[WHOLE_VMEM] * (3 * n), out_shape=shapes * 3,
        compiler_params=pltpu.CompilerParams(vmem_limit_bytes=VMEM_LIMIT_BYTES))(*ws, *gs, *ms, *vs)
    return outs[:n], outs[n:2 * n], outs[2 * n:]


def _pack_rows(arrays):
    rows = []
    for a in arrays:
        flat = a.reshape(-1).astype(F32)
        pad = (-flat.shape[0]) % LANES
        rows.append(jnp.pad(flat, (0, pad)).reshape(-1, LANES))
    buf = jnp.concatenate(rows, axis=0)
    return jnp.pad(buf, ((0, (-buf.shape[0]) % 8), (0, 0)))


def _unpack_rows(buf, shapes):
    out, r = [], 0
    for s in shapes:
        n = math.prod(s)
        nr = -(-n // LANES)
        out.append(buf[r:r + nr].reshape(-1)[:n].reshape(s))
        r += nr
    return out


def _block_diag(w):
    H, a, b = w.shape
    eye = jnp.eye(H, dtype=w.dtype)
    return (eye[:, None, :, None] * w[:, :, None, :]).reshape(H * a, H * b)


def _block_diag_parts(d, H):
    a, b = d.shape[0] // H, d.shape[1] // H
    d4 = d.reshape(H, a, H, b)
    return jnp.stack([d4[h, :, h, :] for h in range(H)])


def _rs_add(names, grads, others):
    return [_add_halves(g, o, "rs_add_halves_" + n) for n, g, o in zip(names, grads, others)]


def _rs_sum(names, parts, recvs):
    return [_sum_chips(p, r, "rs_sum_chips_" + n) for n, p, r in zip(names, parts, recvs)]


def _step(x, mem, target, shards, small, tap_rows, tap_shapes):
    D = x.shape[1]
    nch = N_CHIPS
    p = dict(small)

    (w_in_f,) = _gather_weights([shards['w_in']])
    wf = {}

    def ici(names):
        return _gather_over_ici([shards[n] for n in names])

    ici_a, taps_sum = ici(['w_out', 'w_q']), _all_reduce_rows(tap_rows)
    (z, h1), couts = _fwd_in(x, p['mix_norm_g'], w_in_f, comm=_merge(ici_a, taps_sum))
    bufs_a, (taps,) = _split(couts, ici_a, taps_sum)
    p.update(zip(COL_SHARDED_SMALL, _unpack_rows(taps, tap_shapes)))
    wa_d = _block_diag(p['lru_w_a']).astype(MXU_DTYPE)
    wx_d = _block_diag(p['lru_w_x']).astype(MXU_DTYPE)
    heads = p['lru_w_a'].shape[0]
    pass_a, ici_b = _gather_pass_on(bufs_a), ici(['w_kv', 'w_o'])
    (h, y_lru), couts = _lru_fwd(z, p['lru_conv_w'], p['lru_conv_b'], wa_d, p['lru_b_a'], wx_d, p['lru_b_x'],
                                 p['lru_lambda'], comm=_merge(pass_a, ici_b))
    (wf['w_out'], wf['w_q']), bufs_b = _split(couts, pass_a, ici_b)
    pass_b, ici_c = _gather_pass_on(bufs_b), ici(['w_up'])
    (c1, c3), couts = _conf_fwd(z, p['conf_conv_w'], p['conf_conv_b'], p['conf_ln_g'], p['conf_ln_b'],
                                comm=_merge(pass_b, ici_c))
    (wf['w_kv'], wf['w_o']), bufs_c = _split(couts, pass_b, ici_c)
    w_out2 = wf['w_out'].reshape(2, -1, D)
    w_q = wf['w_q'].reshape(D, D)
    w_o = wf['w_o'].reshape(D, D)
    pass_c, ici_d = _gather_pass_on(bufs_c), ici(['w_down'])
    (x1, h2, q), couts = _fwd_out_q(x, y_lru, c3, w_out2, p['xa_norm_g'], w_q, comm=_merge(pass_c, ici_d))
    (wf['w_up'],), bufs_d = _split(couts, pass_c, ici_d)
    m, kv = _kv_fwd(mem, p['mem_norm_g'], wf['w_kv'])
    (o, x2, h3), (wf['w_down'],) = _attn_fwd(q, kv, x1, w_o, p['ffn_norm_g'], comm=_gather_pass_on(bufs_d))
    gu, act, dx3, loss_lanes, d_final_g = _ffn_fwd(h3, wf['w_up'], p['ffn_conv_w'], p['ffn_conv_b'], wf['w_down'],
                                                   x2, p['final_norm_g'], target)

    dgu, dx2, d_ffn_g, d_ffn_cw, d_ffn_cb = _ffn_bwd(dx3, wf['w_down'], wf['w_up'], gu, x2, p['ffn_norm_g'],
                                                     p['ffn_conv_w'], p['ffn_conv_b'])
    g_down = _wgrad(act, dx3[None], "wgrad_down").reshape(nch, -1, D)
    g_up, other = _wgrad(h3[None], dgu, "wgrad_up", comm=_exchange_halves([g_down]))
    (p_down,) = _rs_add(['w_down'], [g_down], other)
    sc_down, ex_up = _scatter_chip_sums([p_down]), _exchange_halves([g_up])
    (dq, dx1, dkv, d_xa_g), couts = _attn_bwd(dx2, w_o, q, kv, x1, p['xa_norm_g'], w_q, comm=_merge(sc_down, ex_up))
    recv, other = _split(couts, sc_down, ex_up)
    f_down = _rs_sum(['w_down'], [p_down], recv)
    (p_up,) = _rs_add(['w_up'], [g_up], other)
    mid = ['w_o', 'w_q', 'w_kv']
    g_o = _wgrad(o[None], dx2[None], "wgrad_o").reshape(nch, -1, D)
    g_q = _wgrad(h2[None], dq[None], "wgrad_q").reshape(nch, -1, D)
    g_kv, d_mem_g = _kv_bwd(dkv, wf['w_kv'], mem, p['mem_norm_g'], m)
    join_down, sc_up, ex_mid = _join_halves(f_down), _scatter_chip_sums([p_up]), _exchange_halves([g_o, g_q, g_kv])
    (dz_c, d_conf_cw, d_conf_cb, d_ln_g, d_ln_b), couts = _conf_bwd(
        dx1, w_out2[1], z, c1, p['conf_conv_w'], p['conf_ln_g'], p['conf_ln_b'],
        comm=_merge(join_down, sc_up, ex_mid))
    (r_down,), recv, other = _split(couts, join_down, sc_up, ex_mid)
    p_up = [p_up]
    p_mid = _rs_add(mid, [g_o, g_q, g_kv], other)
    join_up, sc_mid = _join_halves(_rs_sum(['w_up'], p_up, recv)), _scatter_chip_sums(p_mid)
    (dz, d_wa, d_wx, d_ba, d_bx, d_lam, d_lru_cw, d_lru_cb), couts = _lru_bwd(
        dx1, w_out2[0], z, h, p['lru_conv_w'], p['lru_conv_b'], wa_d, p['lru_b_a'], wx_d, p['lru_b_x'],
        p['lru_lambda'], dz_c, comm=_merge(join_up, sc_mid))
    (r_up,), recv = _split(couts, join_up, sc_mid)
    f_mid = _rs_sum(mid, p_mid, recv)
    grad_x, d_mix_g = _bwd_in(dz, w_in_f, x, p['mix_norm_g'], dx1)

    small_g = {'mix_norm_g': d_mix_g, 'lru_conv_w': d_lru_cw, 'lru_conv_b': d_lru_cb,
               'lru_w_a': _block_diag_parts(d_wa, heads), 'lru_b_a': d_ba,
               'lru_w_x': _block_diag_parts(d_wx, heads), 'lru_b_x': d_bx, 'lru_lambda': d_lam,
               'conf_conv_w': d_conf_cw, 'conf_conv_b': d_conf_cb, 'conf_ln_g': d_ln_g, 'conf_ln_b': d_ln_b,
               'xa_norm_g': d_xa_g, 'mem_norm_g': d_mem_g, 'ffn_norm_g': d_ffn_g,
               'ffn_conv_w': d_ffn_cw, 'ffn_conv_b': d_ffn_cb, 'final_norm_g': d_final_g}
    names = list(small_g)
    shapes = [small_g[n].shape for n in names]
    join_mid = _join_halves(f_mid)
    small_sum = _all_reduce_rows(_pack_rows([loss_lanes] + [small_g[n] for n in names]), loss_row=0)
    g_in, couts = _wgrad(h1[None], dz, "wgrad_in", comm=_merge(join_mid, small_sum))
    r_mid, (summed, loss) = _split(couts, join_mid, small_sum)
    g_out_l, other = _wgrad(y_lru[None], dx1[None], "wgrad_out_lru", comm=_exchange_halves([g_in]))
    p_in = _rs_add(['w_in'], [g_in], other)
    g_out_c, recv = _wgrad(c3[None], dx1[None], "wgrad_out_conf", comm=_scatter_chip_sums(p_in))
    f_in = _rs_sum(['w_in'], p_in, recv)

    g_out = jnp.concatenate([g_out_l, g_out_c], axis=0).reshape(nch, -1, D)
    join_in, rs_out = _join_halves(f_in), _reduce_scatter_in_vmem(g_out)
    (r_in,), (r_out,) = _split(_run_comm(_merge(join_in, rs_out), "rs_last"), join_in, rs_out)
    big = dict(zip(['w_down', 'w_up'] + mid + ['w_out', 'w_in'], [r_down, r_up] + r_mid + [r_out, r_in]))
    return grad_x, big, summed, loss, names, [loss_lanes.shape] + shapes


def kernel(x, mem, mix_norm_g, w_in, lru_conv_w, lru_conv_b, lru_w_a, lru_b_a, lru_w_x, lru_b_x, lru_lambda, conf_conv_w, conf_conv_b, conf_ln_g, conf_ln_b, w_out, xa_norm_g, mem_norm_g, w_q, w_kv, w_o, ffn_norm_g, w_up, ffn_conv_w, ffn_conv_b, w_down, final_norm_g, loss_target, m_mix_norm_g, m_w_in, m_lru_conv_w, m_lru_conv_b, m_lru_w_a, m_lru_b_a, m_lru_w_x, m_lru_b_x, m_lru_lambda, m_conf_conv_w, m_conf_conv_b, m_conf_ln_g, m_conf_ln_b, m_w_out, m_xa_norm_g, m_mem_norm_g, m_w_q, m_w_kv, m_w_o, m_ffn_norm_g, m_w_up, m_ffn_conv_w, m_ffn_conv_b, m_w_down, m_final_norm_g, v_mix_norm_g, v_w_in, v_lru_conv_w, v_lru_conv_b, v_lru_w_a, v_lru_b_a, v_lru_w_x, v_lru_b_x, v_lru_lambda, v_conf_conv_w, v_conf_conv_b, v_conf_ln_g, v_conf_ln_b, v_w_out, v_xa_norm_g, v_mem_norm_g, v_w_q, v_w_kv, v_w_o, v_ffn_norm_g, v_w_up, v_ffn_conv_w, v_ffn_conv_b, v_w_down, v_final_norm_g):
    given = dict(locals())
    w = {n: given[n] for n in WEIGHTS}
    mom = {n: given["m_" + n] for n in WEIGHTS}
    var = {n: given["v_" + n] for n in WEIGHTS}
    xi, yi, ci = lax.axis_index("x"), lax.axis_index("y"), lax.axis_index("c")
    chip = 2 * xi + yi

    shards = {n: w[n][0].astype(WIRE_DTYPE) for n in BIG}
    tap_full = []
    for n in COL_SHARDED_SMALL:
        s = w[n][0]
        full = jnp.zeros((s.shape[0], N_CHIPS * s.shape[1]), F32)
        s = jnp.where(ci == 0, s, jnp.zeros_like(s))
        tap_full.append(lax.dynamic_update_slice(full, s, (0, chip * s.shape[1])))
    small = {n: (w[n] if w[n].ndim == 1 else w[n][0]) for n in SMALL if n not in COL_SHARDED_SMALL}
    small = {n: (a.reshape(1, -1) if a.ndim == 1 else a) for n, a in small.items()}

    grad_x, big_g, summed, loss, small_names, packed_shapes = _step(
        x[0], mem[0], loss_target[0], shards, small, _pack_rows(tap_full), [t.shape for t in tap_full])
    small_sum = dict(zip(small_names, _unpack_rows(summed, packed_shapes)[1:]))

    grads = {}
    for n in WEIGHTS:
        if n in BIG:
            g = big_g[n]
        elif n in COL_SHARDED_SMALL:
            width = w[n].shape[-1]
            g = lax.dynamic_slice_in_dim(small_sum[n], chip * width, width, axis=1)
        else:
            g = small_sum[n]
        grads[n] = g.reshape(w[n].shape)

    delta, new_m, new_v = {}, {}, {}
    for n in BIG:
        d, nm, nv, g = _adamw(w[n][0], grads[n][0], mom[n][0], var[n][0], "adamw_" + n)
        delta[n], new_m[n], new_v[n], grads[n] = d[None], nm[None], nv[None], g[None]
    flat = lambda a: a.reshape(-1, a.shape[-1])
    outs = _adamw_many(*[[flat(src[n]) for n in SMALL] for src in (w, grads, mom, var)], "adamw_small")
    for out, arrays in zip((delta, new_m, new_v), outs):
        out.update({n: a.reshape(w[n].shape) for n, a in zip(SMALL, arrays)})

    return (loss[0, 0], grad_x[None], *[grads[n] for n in WEIGHTS], *[delta[n] for n in WEIGHTS],
            *[new_m[n] for n in WEIGHTS], *[new_v[n] for n in WEIGHTS])
```

```python
import math

import jax
import jax.numpy as jnp
from jax import lax
from jax.experimental import pallas as pl
from jax.experimental.pallas import tpu as pltpu

F32 = jnp.float32
MXU_DTYPE = jnp.bfloat16
WIRE_DTYPE = jnp.bfloat16
EPS = 1e-6
RG_C = 8.0
XA_HEADS = 4
ADAM_LR, ADAM_B1, ADAM_B2, ADAM_EPS, ADAM_WD, ADAM_STEP = 0.001, 0.9, 0.999, 1e-08, 0.01, 10
VMEM_LIMIT_BYTES = 52 * 1024 * 1024
WGRAD_ACC_BYTES = 8 * 1024 * 1024
LANES = 1024
N_CHIPS = 4
N_DEV = 8
MESH = pl.DeviceIdType.MESH
GELU_C = math.sqrt(2.0 / math.pi)
GELU_K = 0.044715

WEIGHTS = ['mix_norm_g', 'w_in', 'lru_conv_w', 'lru_conv_b', 'lru_w_a', 'lru_b_a', 'lru_w_x', 'lru_b_x',
           'lru_lambda', 'conf_conv_w', 'conf_conv_b', 'conf_ln_g', 'conf_ln_b', 'w_out', 'xa_norm_g',
           'mem_norm_g', 'w_q', 'w_kv', 'w_o', 'ffn_norm_g', 'w_up', 'ffn_conv_w', 'ffn_conv_b', 'w_down',
           'final_norm_g']
BIG = ['w_in', 'w_kv', 'w_up', 'w_out', 'w_q', 'w_o', 'w_down']
SMALL = [n for n in WEIGHTS if n not in BIG]
COL_SHARDED_SMALL = ['lru_conv_w', 'conf_conv_w', 'ffn_conv_w']


def _params(*semantics):
    return pltpu.CompilerParams(dimension_semantics=semantics, vmem_limit_bytes=VMEM_LIMIT_BYTES)


ANY = pl.BlockSpec(memory_space=pl.ANY)
WHOLE_VMEM = pl.BlockSpec(memory_space=pltpu.VMEM)


class _Comm:
    def __init__(self, arrays, out_shapes, scratch, start, finish, aliases=None, in_specs=None, out_specs=None):
        self.arrays, self.out_shapes, self.scratch = list(arrays), list(out_shapes), list(scratch)
        self.start, self.finish = start, finish
        self.aliases = dict(aliases or {})
        self.in_specs = list(in_specs) if in_specs is not None else [ANY] * len(self.arrays)
        self.out_specs = list(out_specs) if out_specs is not None else [ANY] * len(self.out_shapes)


def _merge(*comms):
    comms = [c for c in comms if c is not None]
    if not comms:
        return None
    ai = [0]
    for c in comms:
        ai.append(ai[-1] + len(c.arrays))
    oi = [0]
    for c in comms:
        oi.append(oi[-1] + len(c.out_shapes))
    si = [0]
    for c in comms:
        si.append(si[-1] + len(c.scratch))

    def each(which):
        def run(ins, outs, scr):
            for k, c in enumerate(comms):
                getattr(c, which)(ins[ai[k]:ai[k + 1]], outs[oi[k]:oi[k + 1]], scr[si[k]:si[k + 1]])
        return run

    aliases = {ai[k] + i: oi[k] + o for k, c in enumerate(comms) for i, o in c.aliases.items()}
    return _Comm(sum((c.arrays for c in comms), []), sum((c.out_shapes for c in comms), []),
                 sum((c.scratch for c in comms), []), each("start"), each("finish"), aliases,
                 sum((c.in_specs for c in comms), []), sum((c.out_specs for c in comms), []))


def _split(outs, *comms):
    parts, at = [], 0
    for c in comms:
        parts.append(outs[at:at + len(c.out_shapes)])
        at += len(c.out_shapes)
    return parts


def _pcall(comm, body, *, name, grid, in_specs, out_specs, out_shape, semantics, scratch_shapes=(), aliases=None):
    single = not isinstance(out_shape, (list, tuple))
    out_shape = [out_shape] if single else list(out_shape)
    out_specs = [out_specs] if single else list(out_specs)
    in_specs, scratch_shapes = list(in_specs), list(scratch_shapes)
    aliases = dict(aliases or {})

    if comm is None:
        def plain(*args):
            return list(pl.pallas_call(body, name=name, grid=grid, in_specs=in_specs, out_specs=out_specs,
                                       out_shape=out_shape, scratch_shapes=scratch_shapes,
                                       input_output_aliases=aliases,
                                       compiler_params=_params(*semantics))(*args))
        return plain

    def hosted(*args):
        n_in, n_out, n_scr = len(args), len(out_shape), len(scratch_shapes)
        c_in, c_out = len(comm.arrays), len(comm.out_shapes)

        def wrapped(*refs):
            ins, cins = refs[:n_in], refs[n_in:n_in + c_in]
            o0 = n_in + c_in
            outs, couts = refs[o0:o0 + n_out], refs[o0 + n_out:o0 + n_out + c_out]
            s0 = o0 + n_out + c_out
            scr, cscr = refs[s0:s0 + n_scr], refs[s0 + n_scr:]
            first = last = None
            for axis, size in enumerate(grid):
                at_start, at_end = pl.program_id(axis) == 0, pl.program_id(axis) == size - 1
                first = at_start if first is None else first & at_start
                last = at_end if last is None else last & at_end
            if first is None:
                comm.start(cins, couts, cscr)
                body(*ins, *outs, *scr)
                comm.finish(cins, couts, cscr)
                return
            pl.when(first)(lambda: comm.start(cins, couts, cscr))
            body(*ins, *outs, *scr)
            pl.when(last)(lambda: comm.finish(cins, couts, cscr))

        res = pl.pallas_call(
            wrapped, name=name, grid=grid, in_specs=in_specs + comm.in_specs, out_specs=out_specs + comm.out_specs,
            out_shape=out_shape + comm.out_shapes, scratch_shapes=scratch_shapes + comm.scratch,
            input_output_aliases={**aliases, **{n_in + i: n_out + o for i, o in comm.aliases.items()}},
            compiler_params=pltpu.CompilerParams(dimension_semantics=("arbitrary",) * len(grid),
                                                 vmem_limit_bytes=VMEM_LIMIT_BYTES, has_side_effects=True),
        )(*args, *comm.arrays)
        return list(res[:n_out]), list(res[n_out:])

    return hosted


def _run_comm(comm, name):
    return _pcall(comm, lambda: None, name=name, grid=(), in_specs=[], out_specs=[], out_shape=[], semantics=())()[1]


def _tile(n, want, align=8):
    if n <= want:
        return n
    for t in range(want - want % align, 0, -align):
        if n % t == 0:
            return t
    raise ValueError((n, want, align))


def _mm(a, b):
    return jnp.dot(a.astype(MXU_DTYPE), b.astype(MXU_DTYPE), preferred_element_type=F32)


def _mm_nt(a, b):
    return lax.dot_general(a.astype(MXU_DTYPE), b.astype(MXU_DTYPE), (((1,), (1,)), ((), ())),
                           preferred_element_type=F32)


def _mm_tn(a, b):
    return lax.dot_general(a.astype(MXU_DTYPE), b.astype(MXU_DTYPE), (((0,), (0,)), ((), ())),
                           preferred_element_type=F32)


def _sigmoid(v):
    return 0.5 * jnp.tanh(0.5 * v) + 0.5


def _gelu(v):
    v2 = v * v
    t = jnp.tanh(v * (GELU_C + (GELU_C * GELU_K) * v2))
    hv = 0.5 * v
    dt = (1.0 - t * t) * (GELU_C + (3.0 * GELU_C * GELU_K) * v2)
    return hv + hv * t, (0.5 + 0.5 * t) + hv * dt


def _softplus_neg(lam):
    e = jnp.exp(-jnp.abs(lam))
    u = 1.0 + e
    log1p_e = jnp.where(u == 1.0, e, jnp.log(u) * e / jnp.where(u == 1.0, 1.0, u - 1.0))
    return jnp.maximum(-lam, 0.0) + log1p_e


def _rms(xv):
    rinv = lax.rsqrt(jnp.mean(xv * xv, axis=-1, keepdims=True) + EPS)
    return rinv, xv * rinv


def _rms_bwd(rinv, xhat, dxhat):
    return rinv * (dxhat - xhat * jnp.mean(dxhat * xhat, axis=-1, keepdims=True))


def _colsum(v):
    return jnp.sum(v, axis=0, keepdims=True)


def _wrow(w_ref, k, wcols):
    return w_ref[pl.ds(k, 1), :] if wcols is None else w_ref[pl.ds(k, 1), wcols]


def _windows(buf_ref, halo, taps, rows):
    assert taps <= 8 <= halo
    x = buf_ref[pl.ds(halo - 8, rows + 8), :]
    return [x[8:] if s == 0 else pltpu.roll(x, s, 0)[8:] for s in range(taps)]


def _causal_from(xs, w_ref, wcols=None):
    taps = len(xs)
    acc = None
    for s in range(taps):
        term = _wrow(w_ref, taps - 1 - s, wcols) * xs[s]
        acc = term if acc is None else acc + term
    return acc


def _tap_grads_from(dw_ref, dy, xs, wcols=None):
    taps = len(xs)
    for s in range(taps):
        g = _colsum(dy * xs[s])
        if wcols is None:
            dw_ref[pl.ds(taps - 1 - s, 1), :] += g
        else:
            dw_ref[pl.ds(taps - 1 - s, 1), wcols] += g


def _causal_taps(buf_ref, halo, w_ref, taps, rows, wcols=None):
    return _causal_from(_windows(buf_ref, halo, taps, rows), w_ref, wcols)


def _anticausal_taps(buf_ref, w_ref, taps, rows, wcols=None):
    assert taps <= 8
    x = buf_ref[pl.ds(0, rows + 8), :]
    acc = None
    for s in range(taps):
        win = x[:rows] if s == 0 else pltpu.roll(x, rows + 8 - s, 0)[:rows]
        term = _wrow(w_ref, taps - 1 - s, wcols) * win
        acc = term if acc is None else acc + term
    return acc


def _shift_copies(dst_ref, buf_ref, rows, up):
    x = buf_ref[pl.ds(0, rows + 8), :]
    for r in range(8):
        if up:
            dst_ref[r] = x[:rows] if r == 0 else pltpu.roll(x, rows + 8 - r, 0)[:rows]
        else:
            dst_ref[r] = x[8:] if r == 0 else pltpu.roll(x, r, 0)[8:]


def _causal_taps8(sh_ref, halo, w_ref, taps, rows):
    acc = None
    for s in range(taps):
        term = _wrow(w_ref, taps - 1 - s, None) * sh_ref[s % 8, pl.ds(halo - 8 - 8 * (s // 8), rows), :]
        acc = term if acc is None else acc + term
    return acc


def _anticausal_taps8(sh_ref, w_ref, taps, rows):
    acc = None
    for s in range(taps):
        term = _wrow(w_ref, taps - 1 - s, None) * sh_ref[s % 8, pl.ds(8 * (s // 8), rows), :]
        acc = term if acc is None else acc + term
    return acc


def _tap_grads8(dw_ref, dy, sh_ref, halo, taps, rows):
    for s in range(taps):
        dw_ref[pl.ds(taps - 1 - s, 1), :] += _colsum(dy * sh_ref[s % 8, pl.ds(halo - 8 - 8 * (s // 8), rows), :])


def _fwd_in(x, g, w_in, comm=None):
    S, D = x.shape
    nb, _, C = w_in.shape
    ts = _tile(S, 1024)

    halves = 2 if ts % 32 == 0 else 1
    hr = ts // halves

    def body(x_ref, g_ref, w_ref, z_ref, h_ref):
        def norm(k):
            rows = pl.ds(k * hr, hr)
            _, xhat = _rms(x_ref[rows, :])
            h = (xhat * g_ref[...]).astype(MXU_DTYPE)
            h_ref[rows, :] = h
            return h

        h_next = norm(0)
        for k in range(halves):
            rows, h = pl.ds(k * hr, hr), h_next
            if k + 1 < halves:
                h_next = norm(k + 1)
            for j in range(nb):
                z_ref[j, rows, :] = jnp.dot(h, w_ref[j], preferred_element_type=F32)

    return _pcall(
        comm, body, name="fwd_in", grid=(S // ts,),
        in_specs=[pl.BlockSpec((ts, D), lambda i: (i, 0)), pl.BlockSpec((1, D), lambda i: (0, 0)),
                  pl.BlockSpec((nb, D, C), lambda i: (0, 0, 0))],
        out_specs=[pl.BlockSpec((nb, ts, C), lambda i: (0, i, 0)), pl.BlockSpec((ts, D), lambda i: (i, 0))],
        out_shape=[jax.ShapeDtypeStruct((nb, S, C), F32), jax.ShapeDtypeStruct((S, D), MXU_DTYPE)],
        semantics=("parallel",))(x, g, w_in)


def _lru_gates(xc, wa_ref, ba_ref, wx_ref, bx_ref, sp):
    xb = xc.astype(MXU_DTYPE)
    r = _sigmoid(jnp.dot(xb, wa_ref[...], preferred_element_type=F32) + ba_ref[...])
    ig = _sigmoid(jnp.dot(xb, wx_ref[...], preferred_element_type=F32) + bx_ref[...])
    log_a = -RG_C * r * sp
    a = jnp.exp(log_a)
    one_minus_a2 = jnp.tanh(-log_a) * (a * a + 1.0)
    inv_mult = lax.rsqrt(one_minus_a2)
    mult = jnp.where(one_minus_a2 > 0.0, one_minus_a2 * inv_mult, 0.0)
    return r, ig, a, mult, inv_mult


def _lru_fwd(z, conv_w, conv_b, wa, ba, wx, bx, lam, comm=None):
    _, S, C = z.shape
    ts = _tile(S, 256)
    taps = conv_w.shape[0]
    halo = 8

    def body(zx_ref, zg_ref, cw_ref, cb_ref, wa_ref, ba_ref, wx_ref, bx_ref, lam_ref,
             h_ref, y_ref, xbuf, a_s, u_s, hc):
        i = pl.program_id(0)

        @pl.when(i == 0)
        def _():
            xbuf[pl.ds(0, halo), :] = jnp.zeros((halo, C), F32)
            hc[...] = jnp.zeros_like(hc)

        xbuf[pl.ds(halo, ts), :] = zx_ref[0]
        xc = _causal_taps(xbuf, halo, cw_ref, taps, ts) + cb_ref[...]
        sp = _softplus_neg(lam_ref[...])
        _, ig, a, mult, _ = _lru_gates(xc, wa_ref, ba_ref, wx_ref, bx_ref, sp)
        a_s[...] = a
        u_s[...] = mult * (ig * xc)
        row = lax.broadcasted_iota(jnp.int32, (8, C), 0)

        def step(k, carry):
            off = pl.multiple_of(k * 8, 8)
            av = a_s[pl.ds(off, 8), :]
            uv = u_s[pl.ds(off, 8), :]
            for d in (1, 2, 4):
                m = row >= d
                a_sh = jnp.where(m, pltpu.roll(av, d, 0), 1.0)
                u_sh = jnp.where(m, pltpu.roll(uv, d, 0), 0.0)
                uv = uv + av * u_sh
                av = av * a_sh
            hv = uv + av * carry
            h_ref[pl.ds(off, 8), :] = hv
            return jnp.broadcast_to(hv[7:8, :], (8, C))

        hc[...] = lax.fori_loop(0, ts // 8, step, hc[...], unroll=4)
        ge, _ = _gelu(zg_ref[0])
        y_ref[...] = (h_ref[...] * ge).astype(MXU_DTYPE)
        xbuf[pl.ds(0, halo), :] = xbuf[pl.ds(ts, halo), :]

    vec = pl.BlockSpec((1, C), lambda i: (0, 0))
    mat = pl.BlockSpec((C, C), lambda i: (0, 0))
    return _pcall(
        comm, body, name="lru_fwd", grid=(S // ts,),
        in_specs=[pl.BlockSpec((1, ts, C), lambda i: (0, i, 0)), pl.BlockSpec((1, ts, C), lambda i: (1, i, 0)),
                  pl.BlockSpec((taps, C), lambda i: (0, 0)), vec, mat, vec, mat, vec, vec],
        out_specs=[pl.BlockSpec((ts, C), lambda i: (i, 0)), pl.BlockSpec((ts, C), lambda i: (i, 0))],
        out_shape=[jax.ShapeDtypeStruct((S, C), F32), jax.ShapeDtypeStruct((S, C), MXU_DTYPE)],
        scratch_shapes=[pltpu.VMEM((ts + halo, C), F32), pltpu.VMEM((ts, C), F32), pltpu.VMEM((ts, C), F32),
                        pltpu.VMEM((8, C), F32)],
        semantics=("arbitrary",))(z, z, conv_w, conv_b, wa, ba, wx, bx, lam)


def _layer_norm_stats(c1):
    mu = jnp.mean(c1, axis=-1, keepdims=True)
    xc = c1 - mu
    rstd = lax.rsqrt(jnp.mean(xc * xc, axis=-1, keepdims=True) + EPS)
    return rstd, xc * rstd


def _conf_fwd(z, conv_w, conv_b, ln_g, ln_b, comm=None):
    _, S, C = z.shape
    ts = _tile(S, 256)
    taps = conv_w.shape[0]
    halo = 32

    def body(za_ref, zb_ref, cw_ref, cb_ref, g_ref, b_ref, c1_ref, c3_ref, cbuf, shifted):
        i = pl.program_id(0)

        @pl.when(i == 0)
        def _():
            cbuf[pl.ds(0, halo), :] = jnp.zeros((halo, C), F32)

        cbuf[pl.ds(halo, ts), :] = za_ref[0] * _sigmoid(zb_ref[0])
        _shift_copies(shifted, cbuf, ts + halo - 8, up=False)
        c1 = _causal_taps8(shifted, halo, cw_ref, taps, ts) + cb_ref[...]
        c1_ref[...] = c1
        _, xhat = _layer_norm_stats(c1)
        c2 = xhat * g_ref[...] + b_ref[...]
        c3_ref[...] = (c2 * _sigmoid(c2)).astype(MXU_DTYPE)
        cbuf[pl.ds(0, halo), :] = cbuf[pl.ds(ts, halo), :]

    vec = pl.BlockSpec((1, C), lambda i: (0, 0))
    return _pcall(
        comm, body, name="conf_fwd", grid=(S // ts,),
        in_specs=[pl.BlockSpec((1, ts, C), lambda i: (2, i, 0)), pl.BlockSpec((1, ts, C), lambda i: (3, i, 0)),
                  pl.BlockSpec((taps, C), lambda i: (0, 0)), vec, vec, vec],
        out_specs=[pl.BlockSpec((ts, C), lambda i: (i, 0)), pl.BlockSpec((ts, C), lambda i: (i, 0))],
        out_shape=[jax.ShapeDtypeStruct((S, C), F32), jax.ShapeDtypeStruct((S, C), MXU_DTYPE)],
        scratch_shapes=[pltpu.VMEM((ts + halo, C), F32), pltpu.VMEM((8, ts + halo - 8, C), F32)],
        semantics=("arbitrary",))(z, z, conv_w, conv_b, ln_g, ln_b)


def _fwd_out_q(x, y_lru, c3, w_out, g_xa, w_q, comm=None):
    S, D = x.shape
    C = y_lru.shape[1]
    ts = _tile(S, 1024)

    halves = 2 if ts % 32 == 0 else 1
    hr = ts // halves

    def body(x_ref, yl_ref, c3_ref, wo_ref, g_ref, wq_ref, x1_ref, h2_ref, q_ref):
        def mixed(k):
            rows = pl.ds(k * hr, hr)
            return (jnp.dot(yl_ref[rows, :], wo_ref[0], preferred_element_type=F32)
                    + jnp.dot(c3_ref[rows, :], wo_ref[1], preferred_element_type=F32))

        y_next = mixed(0)
        for k in range(halves):
            rows, y = pl.ds(k * hr, hr), y_next
            if k + 1 < halves:
                y_next = mixed(k + 1)
            x1 = x_ref[rows, :] + y
            x1_ref[rows, :] = x1
            _, xhat = _rms(x1)
            h2 = (xhat * g_ref[...]).astype(MXU_DTYPE)
            h2_ref[rows, :] = h2
            q_ref[rows, :] = jnp.dot(h2, wq_ref[...], preferred_element_type=F32).astype(MXU_DTYPE)

    row = lambda w: pl.BlockSpec((ts, w), lambda i: (i, 0))
    return _pcall(
        comm, body, name="fwd_out_q", grid=(S // ts,),
        in_specs=[row(D), row(C), row(C), pl.BlockSpec((2, C, D), lambda i: (0, 0, 0)),
                  pl.BlockSpec((1, D), lambda i: (0, 0)), pl.BlockSpec((D, D), lambda i: (0, 0))],
        out_specs=[row(D), row(D), row(D)],
        out_shape=[jax.ShapeDtypeStruct((S, D), F32), jax.ShapeDtypeStruct((S, D), MXU_DTYPE),
                   jax.ShapeDtypeStruct((S, D), MXU_DTYPE)],
        semantics=("parallel",))(x, y_lru, c3, w_out, g_xa, w_q)


def _kv_fwd(mem, g, w_kv):
    M, D = mem.shape
    nb, _, C = w_kv.shape

    def body(mem_ref, g_ref, w_ref, m_ref, kv_ref):
        _, xhat = _rms(mem_ref[...])
        m = (xhat * g_ref[...]).astype(MXU_DTYPE)
        m_ref[...] = m
        for j in range(nb):
            kv_ref[:, pl.ds(j * C, C)] = jnp.dot(m, w_ref[j], preferred_element_type=F32).astype(MXU_DTYPE)

    return pl.pallas_call(
        body, name="kv_fwd", grid=(1,),
        in_specs=[pl.BlockSpec((M, D), lambda i: (0, 0)), pl.BlockSpec((1, D), lambda i: (0, 0)),
                  pl.BlockSpec((nb, D, C), lambda i: (0, 0, 0))],
        out_specs=[pl.BlockSpec((M, D), lambda i: (0, 0)), pl.BlockSpec((M, nb * C), lambda i: (0, 0))],
        out_shape=[jax.ShapeDtypeStruct((M, D), MXU_DTYPE), jax.ShapeDtypeStruct((M, nb * C), MXU_DTYPE)],
        compiler_params=_params("arbitrary"))(mem, g, w_kv)


def _softmax_rows(s):
    e = jnp.exp(s - jnp.max(s, axis=-1, keepdims=True))
    return e / jnp.sum(e, axis=-1, keepdims=True)


def _attn_fwd(q, kv, x1, w_o, g_ffn, comm=None):
    S, D = x1.shape
    M = kv.shape[0]
    hd = D // XA_HEADS
    scale = hd ** -0.5
    ts = _tile(S, 1024)

    def body(q_ref, kv_ref, x1_ref, wo_ref, g_ref, o_ref, x2_ref, h3_ref):
        def scores(h):
            cols = pl.ds(h * hd, hd)
            return _mm_nt(q_ref[:, cols], kv_ref[:, cols]) * scale

        s_next = scores(0)
        for h in range(XA_HEADS):
            s = s_next
            if h + 1 < XA_HEADS:
                s_next = scores(h + 1)
            p = _softmax_rows(s)
            o_ref[:, pl.ds(h * hd, hd)] = _mm(p, kv_ref[:, pl.ds(D + h * hd, hd)]).astype(MXU_DTYPE)
        x2 = x1_ref[...] + jnp.dot(o_ref[...], wo_ref[...], preferred_element_type=F32)
        x2_ref[...] = x2
        _, xhat = _rms(x2)
        h3_ref[...] = (xhat * g_ref[...]).astype(MXU_DTYPE)

    row = pl.BlockSpec((ts, D), lambda i: (i, 0))
    return _pcall(
        comm, body, name="attn_fwd", grid=(S // ts,),
        in_specs=[row, pl.BlockSpec((M, 2 * D), lambda i: (0, 0)), row, pl.BlockSpec((D, D), lambda i: (0, 0)),
                  pl.BlockSpec((1, D), lambda i: (0, 0))],
        out_specs=[row, row, row],
        out_shape=[jax.ShapeDtypeStruct((S, D), MXU_DTYPE), jax.ShapeDtypeStruct((S, D), F32),
                   jax.ShapeDtypeStruct((S, D), MXU_DTYPE)],
        semantics=("parallel",))(q, kv, x1, w_o, g_ffn)


def _ffn_fwd(h3, w_up, conv_w, conv_b, w_down, x2, g_final, target, comm=None):
    S, D = h3.shape
    nb, _, CW = w_up.shape
    half = nb // 2
    cb = 768
    per = CW // cb
    J = half * per
    ts = _tile(S, 256)
    taps = conv_w.shape[0]
    halo = 8

    def body(h_ref, wup_ref, cw_ref, cb_ref, wd_ref, x2_ref, gf_ref, t_ref,
             g_ref, ge_ref, ud_ref, act_ref, dx3_ref, loss_ref, dgf_ref, gbuf):
        i = pl.program_id(0)

        @pl.when(i == 0)
        def _():
            for ref in (loss_ref, dgf_ref, gbuf):
                ref[...] = jnp.zeros_like(ref)

        hv = h_ref[...]
        x3 = x2_ref[...]
        def up(j):
            b, cols = j // per, pl.ds((j % per) * cb, cb)
            return (jnp.dot(hv, wup_ref[b, :, cols], preferred_element_type=F32),
                    jnp.dot(hv, wup_ref[half + b, :, cols], preferred_element_type=F32))

        ahead = up(0)
        for j in range(J):
            b, cols, wcols = j // per, pl.ds((j % per) * cb, cb), pl.ds(j * cb, cb)
            g, u = ahead
            if j + 1 < J:
                ahead = up(j + 1)
            g_ref[b, :, cols] = g
            gbuf[j, pl.ds(halo, ts), :] = g
            gc = _causal_taps(gbuf.at[j], halo, cw_ref, taps, ts, wcols=wcols) + cb_ref[:, wcols]
            gbuf[j, pl.ds(0, halo), :] = gbuf[j, pl.ds(ts, halo), :]
            ge, dge = _gelu(gc)
            ge_ref[b, :, cols] = ge.astype(MXU_DTYPE)
            ud_ref[b, :, cols] = (u * dge).astype(MXU_DTYPE)
            act = (ge * u).astype(MXU_DTYPE)
            act_ref[j] = act
            x3 = x3 + jnp.dot(act, wd_ref[j], preferred_element_type=F32)
        rinv, xhat = _rms(x3)
        gf = gf_ref[...]
        diff = xhat * gf - t_ref[...]
        loss_ref[...] += _colsum(diff * diff) * (0.5 / D)
        dy = diff * (1.0 / D)
        dgf_ref[...] += _colsum(dy * xhat)
        dx3_ref[...] = _rms_bwd(rinv, xhat, dy * gf)

    row = pl.BlockSpec((ts, D), lambda i: (i, 0))
    vecd = pl.BlockSpec((1, D), lambda i: (0, 0))
    wide = pl.BlockSpec((half, ts, CW), lambda i: (0, i, 0))
    once = pl.Buffered(1)
    sds = jax.ShapeDtypeStruct
    return _pcall(
        comm, body, name="ffn_fwd", grid=(S // ts,),
        in_specs=[row, pl.BlockSpec((nb, D, CW), lambda i: (0, 0, 0), pipeline_mode=once),
                  pl.BlockSpec((taps, half * CW), lambda i: (0, 0)), pl.BlockSpec((1, half * CW), lambda i: (0, 0)),
                  pl.BlockSpec((J, cb, D), lambda i: (0, 0, 0), pipeline_mode=once), row, vecd, row],
        out_specs=[wide, wide, wide, pl.BlockSpec((J, ts, cb), lambda i: (0, i, 0)), row, vecd, vecd],
        out_shape=[sds((half, S, CW), F32), sds((half, S, CW), MXU_DTYPE), sds((half, S, CW), MXU_DTYPE),
                   sds((J, S, cb), MXU_DTYPE), sds((S, D), F32), sds((1, D), F32), sds((1, D), F32)],
        scratch_shapes=[pltpu.VMEM((J, ts + halo, cb), F32)],
        semantics=("arbitrary",))(h3, w_up, conv_w, conv_b, w_down.reshape(J, cb, D), x2, g_final, target)


def _ffn_bwd(dx3, w_down, w_up, g, ge, u_dge, x2, g_ffn, conv_w, comm=None):
    half, S, CW = g.shape
    nb = 2 * half
    D = dx3.shape[1]
    cb = 768
    per = CW // cb
    J = half * per
    ts = _tile(S, 256)
    n = S // ts
    taps = conv_w.shape[0]
    halo = 8
    hb = ts // halo

    def body(dx_ref, x2_ref, gf_ref, wd_ref, wup_ref, g_ref, gh_ref, ge_ref, ud_ref, cw_ref,
             dgu_ref, dx2_ref, dgf_ref, dcw_ref, dcb_ref, gbuf, dbuf):
        i = pl.program_id(0)
        r = n - 1 - i

        @pl.when(i == 0)
        def _():
            for ref in (dgf_ref, dcw_ref, dcb_ref, dbuf):
                ref[...] = jnp.zeros_like(ref)

        dx3v = dx_ref[...]
        dxb = dx3v.astype(MXU_DTYPE)
        dacts = [_mm_nt(dxb, wd_ref[j]) for j in range(J)]
        dh = None
        for j in range(J):
            b, cols, wcols = j // per, pl.ds((j % per) * cb, cb), pl.ds(j * cb, cb)
            dact = dacts[j]
            dub = (dact * ge_ref[b, :, cols].astype(F32)).astype(MXU_DTYPE)
            dgc = dact * ud_ref[b, :, cols].astype(F32)
            dcb_ref[:, wcols] += _colsum(dgc)
            dbuf[j, pl.ds(0, ts), :] = dgc
            gbuf[pl.ds(0, halo), :] = jnp.where(r > 0, gh_ref[b, :, cols], 0.0)
            gbuf[pl.ds(halo, ts), :] = g_ref[b, :, cols]
            _tap_grads_from(dcw_ref, dgc, _windows(gbuf, halo, taps, ts), wcols)
            dgb = _anticausal_taps(dbuf.at[j], cw_ref, taps, ts, wcols=wcols).astype(MXU_DTYPE)
            dbuf[j, pl.ds(ts, halo), :] = dbuf[j, pl.ds(0, halo), :]
            dgu_ref[0, b, :, cols] = dgb
            dgu_ref[1, b, :, cols] = dub
            part = _mm_nt(dgb, wup_ref[b, :, cols]) + _mm_nt(dub, wup_ref[half + b, :, cols])
            dh = part if dh is None else dh + part
        rinv, xhat = _rms(x2_ref[...])
        dgf_ref[...] += _colsum(dh * xhat)
        dx2_ref[...] = dx3v + _rms_bwd(rinv, xhat, dh * gf_ref[...])

    row = pl.BlockSpec((ts, D), lambda i: (n - 1 - i, 0))
    vecd = pl.BlockSpec((1, D), lambda i: (0, 0))
    wide = pl.BlockSpec((half, ts, CW), lambda i: (0, n - 1 - i, 0))
    g_prev = pl.BlockSpec((half, halo, CW), lambda i: (0, jnp.maximum((n - 1 - i) * hb - 1, 0), 0))
    tapw = pl.BlockSpec((taps, half * CW), lambda i: (0, 0))
    vec = pl.BlockSpec((1, half * CW), lambda i: (0, 0))
    once = pl.Buffered(1)
    sds = jax.ShapeDtypeStruct
    res = _pcall(
        comm, body, name="ffn_bwd", grid=(n,),
        in_specs=[row, row, vecd, pl.BlockSpec((J, cb, D), lambda i: (0, 0, 0), pipeline_mode=once),
                  pl.BlockSpec((nb, D, CW), lambda i: (0, 0, 0), pipeline_mode=once), wide, g_prev, wide, wide, tapw],
        out_specs=[pl.BlockSpec((2, half, ts, CW), lambda i: (0, 0, n - 1 - i, 0)), row, vecd, tapw, vec],
        out_shape=[sds((2, half, S, CW), MXU_DTYPE), sds((S, D), F32), sds((1, D), F32),
                   sds((taps, half * CW), F32), sds((1, half * CW), F32)],
        scratch_shapes=[pltpu.VMEM((ts + halo, cb), F32), pltpu.VMEM((J, ts + halo, cb), F32)],
        semantics=("arbitrary",))(dx3, x2, g_ffn, w_down.reshape(J, cb, D), w_up, g, g, ge, u_dge, conv_w)
    outs = res if comm is None else res[0]
    outs = [outs[0].reshape(nb, S, CW)] + list(outs[1:])
    return outs if comm is None else (outs, res[1])


def _attn_bwd(dx2, w_o, q, kv, x1, g_xa, w_q, comm=None):
    S, D = x1.shape
    M = kv.shape[0]
    hd = D // XA_HEADS
    scale = hd ** -0.5
    ts = _tile(S, 1024)

    def body(dx2_ref, wo_ref, q_ref, kv_ref, x1_ref, g_ref, wq_ref, dq_ref, dx1_ref, dkv_ref, dg_ref):
        i = pl.program_id(0)

        @pl.when(i == 0)
        def _():
            dkv_ref[...] = jnp.zeros_like(dkv_ref)
            dg_ref[...] = jnp.zeros_like(dg_ref)

        dx2 = dx2_ref[...]
        do = _mm_nt(dx2, wo_ref[...]).astype(MXU_DTYPE)
        def scores(h):
            cols = pl.ds(h * hd, hd)
            doh = do[:, h * hd:(h + 1) * hd]
            return (_mm_nt(q_ref[:, cols], kv_ref[:, cols]) * scale,
                    _mm_nt(doh, kv_ref[:, pl.ds(D + h * hd, hd)]), doh)

        ahead = scores(0)
        for h in range(XA_HEADS):
            cols = pl.ds(h * hd, hd)
            vcols = pl.ds(D + h * hd, hd)
            s, dp, doh = ahead
            if h + 1 < XA_HEADS:
                ahead = scores(h + 1)
            p = _softmax_rows(s)
            ds = (p * (dp - jnp.sum(dp * p, axis=-1, keepdims=True)) * scale).astype(MXU_DTYPE)
            dkv_ref[:, vcols] += _mm_tn(p, doh)
            dq_ref[:, cols] = _mm(ds, kv_ref[:, cols]).astype(MXU_DTYPE)
            dkv_ref[:, cols] += _mm_tn(ds, q_ref[:, cols])
        dh2 = _mm_nt(dq_ref[...], wq_ref[...])
        rinv, xhat = _rms(x1_ref[...])
        dg_ref[...] += _colsum(dh2 * xhat)
        dx1_ref[...] = dx2 + _rms_bwd(rinv, xhat, dh2 * g_ref[...])

    row = pl.BlockSpec((ts, D), lambda i: (i, 0))
    mat = pl.BlockSpec((D, D), lambda i: (0, 0))
    vecd = pl.BlockSpec((1, D), lambda i: (0, 0))
    kvs = pl.BlockSpec((M, 2 * D), lambda i: (0, 0))
    return _pcall(
        comm, body, name="attn_bwd", grid=(S // ts,),
        in_specs=[row, mat, row, kvs, row, vecd, mat],
        out_specs=[row, row, kvs, vecd],
        out_shape=[jax.ShapeDtypeStruct((S, D), MXU_DTYPE), jax.ShapeDtypeStruct((S, D), F32),
                   jax.ShapeDtypeStruct((M, 2 * D), F32), jax.ShapeDtypeStruct((1, D), F32)],
        semantics=("arbitrary",))(dx2, w_o, q, kv, x1, g_xa, w_q)


def _kv_bwd(dkv, w_kv, mem, g, m):
    M, D = mem.shape
    nb, _, C = w_kv.shape

    def body(dkv_ref, w_ref, mem_ref, m_ref, dw_ref, dg_ref):
        dm = jnp.zeros((M, D), F32)
        for j in range(nb):
            dj = dkv_ref[:, pl.ds(j * C, C)].astype(MXU_DTYPE)
            dw_ref[j] = _mm_tn(m_ref[...], dj).astype(dw_ref.dtype)
            dm = dm + _mm_nt(dj, w_ref[j])
        _, xhat = _rms(mem_ref[...])
        dg_ref[...] = _colsum(dm * xhat)

    full = lambda *s: pl.BlockSpec(s, lambda i: (0,) * len(s))
    return pl.pallas_call(
        body, name="kv_bwd", grid=(1,),
        in_specs=[full(M, nb * C), full(nb, D, C), full(M, D), full(M, D)],
        out_specs=[full(nb, D, C), full(1, D)],
        out_shape=[jax.ShapeDtypeStruct((nb, D, C), WIRE_DTYPE), jax.ShapeDtypeStruct((1, D), F32)],
        compiler_params=_params("arbitrary"))(dkv, w_kv, mem, m)


def _conf_bwd(dx1, w_out_c, z, c1, conv_w, ln_g, ln_b, comm=None):
    _, S, C = z.shape
    D = dx1.shape[1]
    ts = _tile(S, 256)
    n = S // ts
    taps = conv_w.shape[0]
    halo = 32
    hb = ts // halo

    def body(dx_ref, wo_ref, za_ref, zb_ref, zah_ref, zbh_ref, c1_ref, cw_ref, g_ref, b_ref,
             dz_ref, dcw_ref, dcb_ref, dlg_ref, dlb_ref, c0buf, dbuf, shifted):
        i = pl.program_id(0)
        r = n - 1 - i

        @pl.when(i == 0)
        def _():
            for ref in (dcw_ref, dcb_ref, dlg_ref, dlb_ref):
                ref[...] = jnp.zeros_like(ref)
            dbuf[pl.ds(ts, halo), :] = jnp.zeros((halo, C), F32)

        za = za_ref[0]
        sb = _sigmoid(zb_ref[0])
        c0buf[pl.ds(0, halo), :] = jnp.where(r > 0, zah_ref[0] * _sigmoid(zbh_ref[0]), 0.0)
        c0buf[pl.ds(halo, ts), :] = za * sb
        dc3 = _mm_nt(dx_ref[...], wo_ref[...])
        rstd, xhat = _layer_norm_stats(c1_ref[...])
        g = g_ref[...]
        c2 = xhat * g + b_ref[...]
        sg = _sigmoid(c2)
        dc2 = dc3 * sg * (1.0 + c2 * (1.0 - sg))
        dlg_ref[...] += _colsum(dc2 * xhat)
        dlb_ref[...] += _colsum(dc2)
        dxh = dc2 * g
        dc1 = rstd * (dxh - jnp.mean(dxh, axis=-1, keepdims=True)
                      - xhat * jnp.mean(dxh * xhat, axis=-1, keepdims=True))
        dcb_ref[...] += _colsum(dc1)
        dbuf[pl.ds(0, ts), :] = dc1
        _shift_copies(shifted, c0buf, ts + halo - 8, up=False)
        _tap_grads8(dcw_ref, dc1, shifted, halo, taps, ts)
        _shift_copies(shifted, dbuf, ts + halo - 8, up=True)
        dc0 = _anticausal_taps8(shifted, cw_ref, taps, ts)
        dz_ref[0] = (dc0 * sb).astype(MXU_DTYPE)
        dz_ref[1] = (dc0 * za * sb * (1.0 - sb)).astype(MXU_DTYPE)
        dbuf[pl.ds(ts, halo), :] = dbuf[pl.ds(0, halo), :]

    vec = pl.BlockSpec((1, C), lambda i: (0, 0))
    tapw = pl.BlockSpec((taps, C), lambda i: (0, 0))
    tile = lambda b: pl.BlockSpec((1, ts, C), lambda i: (b, n - 1 - i, 0))
    prev = lambda b: pl.BlockSpec((1, halo, C), lambda i: (b, jnp.maximum((n - 1 - i) * hb - 1, 0), 0))
    return _pcall(
        comm, body, name="conf_bwd", grid=(n,),
        in_specs=[pl.BlockSpec((ts, D), lambda i: (n - 1 - i, 0)), pl.BlockSpec((C, D), lambda i: (0, 0)),
                  tile(2), tile(3), prev(2), prev(3), pl.BlockSpec((ts, C), lambda i: (n - 1 - i, 0)),
                  tapw, vec, vec],
        out_specs=[pl.BlockSpec((2, ts, C), lambda i: (1, n - 1 - i, 0)), tapw, vec, vec, vec],
        out_shape=[jax.ShapeDtypeStruct((4, S, C), MXU_DTYPE), jax.ShapeDtypeStruct((taps, C), F32),
                   jax.ShapeDtypeStruct((1, C), F32), jax.ShapeDtypeStruct((1, C), F32),
                   jax.ShapeDtypeStruct((1, C), F32)],
        scratch_shapes=[pltpu.VMEM((ts + halo, C), F32), pltpu.VMEM((ts + halo, C), F32),
                        pltpu.VMEM((8, ts + halo - 8, C), F32)],
        semantics=("arbitrary",))(dx1, w_out_c, z, z, z, z, c1, conv_w, ln_g, ln_b)


def _lru_bwd(dx1, w_out_l, z, h, conv_w, conv_b, wa, ba, wx, bx, lam, dz, comm=None):
    _, S, C = z.shape
    D = dx1.shape[1]
    ts = _tile(S, 256)
    n = S // ts
    taps = conv_w.shape[0]
    halo = 8
    hb = ts // halo

    def body(dx_ref, wo_ref, zx_ref, zxh_ref, zg_ref, h_ref, hh_ref, cw_ref, cb_ref, wa_ref, ba_ref,
             wx_ref, bx_ref, lam_ref, dz_in,
             dz_ref, dwa_ref, dwx_ref, dba_ref, dbx_ref, dlam_ref, dcw_ref, dcb_ref,
             xbuf, hbuf, a_s, w_s, dh_s, g_s, dbuf, pc):
        i = pl.program_id(0)
        r = n - 1 - i

        @pl.when(i == 0)
        def _():
            for ref in (dwa_ref, dwx_ref, dba_ref, dbx_ref, dlam_ref, dcw_ref, dcb_ref, pc):
                ref[...] = jnp.zeros_like(ref)
            dbuf[pl.ds(ts, halo), :] = jnp.zeros((halo, C), F32)

        xbuf[pl.ds(0, halo), :] = jnp.where(r > 0, zxh_ref[0], 0.0)
        xbuf[pl.ds(halo, ts), :] = zx_ref[0]
        hbuf[pl.ds(0, halo), :] = jnp.where(r > 0, hh_ref[...], 0.0)
        hbuf[pl.ds(halo, ts), :] = h_ref[...]
        xs = _windows(xbuf, halo, taps, ts)
        xc = _causal_from(xs, cw_ref) + cb_ref[...]
        lam_v = lam_ref[...]
        sp = _softplus_neg(lam_v)
        rg, ig, a, mult, inv_mult = _lru_gates(xc, wa_ref, ba_ref, wx_ref, bx_ref, sp)

        dy = _mm_nt(dx_ref[...], wo_ref[...])
        ge, dge = _gelu(zg_ref[0])
        dh = dy * ge
        dz_ref[1] = (dy * h_ref[...] * dge).astype(MXU_DTYPE)
        a_s[...] = a
        w_s[...] = a * dh
        dh_s[...] = dh
        row = lax.broadcasted_iota(jnp.int32, (8, C), 0)

        def step(kk, carry):
            off = pl.multiple_of((ts // 8 - 1 - kk) * 8, 8)
            av = a_s[pl.ds(off, 8), :]
            wv = w_s[pl.ds(off, 8), :]
            for d in (1, 2, 4):
                m = row < 8 - d
                a_sh = jnp.where(m, pltpu.roll(av, 8 - d, 0), 1.0)
                w_sh = jnp.where(m, pltpu.roll(wv, 8 - d, 0), 0.0)
                wv = wv + av * w_sh
                av = av * a_sh
            pv = wv + av * carry
            g_s[pl.ds(off, 8), :] = dh_s[pl.ds(off, 8), :] + jnp.where(row < 7, pltpu.roll(pv, 7, 0), carry)
            return jnp.broadcast_to(pv[0:1, :], (8, C))

        pc[...] = lax.fori_loop(0, ts // 8, step, pc[...], unroll=4)
        gt = g_s[...]
        da = gt * hbuf[pl.ds(halo - 1, ts), :]
        gm = gt * mult
        dlog_a = da * a - (gt * ig * xc) * (a * a) * inv_mult
        dlam_ref[...] += _colsum(dlog_a * rg) * (RG_C / (1.0 + jnp.exp(lam_v)))
        dpa = (dlog_a * (-RG_C * sp)) * rg * (1.0 - rg)
        dpx = (gm * xc) * ig * (1.0 - ig)
        dba_ref[...] += _colsum(dpa)
        dbx_ref[...] += _colsum(dpx)
        xb = xc.astype(MXU_DTYPE)
        dpab, dpxb = dpa.astype(MXU_DTYPE), dpx.astype(MXU_DTYPE)
        dwa_ref[...] += _mm_tn(xb, dpab)
        dwx_ref[...] += _mm_tn(xb, dpxb)
        dxc = gm * ig + _mm_nt(dpab, wa_ref[...]) + _mm_nt(dpxb, wx_ref[...])
        dcb_ref[...] += _colsum(dxc)
        dbuf[pl.ds(0, ts), :] = dxc
        _tap_grads_from(dcw_ref, dxc, xs)
        dz_ref[0] = _anticausal_taps(dbuf, cw_ref, taps, ts).astype(MXU_DTYPE)
        dbuf[pl.ds(ts, halo), :] = dbuf[pl.ds(0, halo), :]

    vec = pl.BlockSpec((1, C), lambda i: (0, 0))
    mat = pl.BlockSpec((C, C), lambda i: (0, 0))
    tapw = pl.BlockSpec((taps, C), lambda i: (0, 0))
    prev_rows = lambda i: jnp.maximum((n - 1 - i) * hb - 1, 0)
    sds = jax.ShapeDtypeStruct
    return _pcall(
        comm, body, name="lru_bwd", grid=(n,),
        in_specs=[pl.BlockSpec((ts, D), lambda i: (n - 1 - i, 0)), pl.BlockSpec((C, D), lambda i: (0, 0)),
                  pl.BlockSpec((1, ts, C), lambda i: (0, n - 1 - i, 0)),
                  pl.BlockSpec((1, halo, C), lambda i: (0, prev_rows(i), 0)),
                  pl.BlockSpec((1, ts, C), lambda i: (1, n - 1 - i, 0)),
                  pl.BlockSpec((ts, C), lambda i: (n - 1 - i, 0)),
                  pl.BlockSpec((halo, C), lambda i: (prev_rows(i), 0)),
                  tapw, vec, mat, vec, mat, vec, vec, ANY],
        out_specs=[pl.BlockSpec((2, ts, C), lambda i: (0, n - 1 - i, 0)), mat, mat, vec, vec, vec, tapw, vec],
        out_shape=[sds(dz.shape, MXU_DTYPE), sds((C, C), F32), sds((C, C), F32), sds((1, C), F32),
                   sds((1, C), F32), sds((1, C), F32), sds((taps, C), F32), sds((1, C), F32)],
        scratch_shapes=[pltpu.VMEM((ts + halo, C), F32), pltpu.VMEM((ts + halo, C), F32)]
        + [pltpu.VMEM((ts, C), F32)] * 4 + [pltpu.VMEM((ts + halo, C), F32), pltpu.VMEM((8, C), F32)],
        aliases={14: 0},
        semantics=("arbitrary",))(dx1, w_out_l, z, z, z, h, h, conv_w, conv_b, wa, ba, wx, bx, lam, dz)


def _bwd_in(dz, w_in, x, g, dx1):
    S, D = x.shape
    nb, _, C = w_in.shape
    ts = _tile(S, 512)
    halves = 2 if ts % 32 == 0 else 1
    hr = ts // halves

    def body(dz_ref, w_ref, x_ref, g_ref, dx1_ref, dx_ref, dg_ref):
        i = pl.program_id(0)

        @pl.when(i == 0)
        def _():
            dg_ref[...] = jnp.zeros_like(dg_ref)

        def grad_h(k):
            rows = pl.ds(k * hr, hr)
            dh = _mm_nt(dz_ref[0, rows, :], w_ref[0])
            for j in range(1, nb):
                dh = dh + _mm_nt(dz_ref[j, rows, :], w_ref[j])
            return dh

        ahead = grad_h(0)
        for k in range(halves):
            rows, dh = pl.ds(k * hr, hr), ahead
            if k + 1 < halves:
                ahead = grad_h(k + 1)
            rinv, xhat = _rms(x_ref[rows, :])
            dg_ref[...] += _colsum(dh * xhat)
            dx_ref[rows, :] = dx1_ref[rows, :] + _rms_bwd(rinv, xhat, dh * g_ref[...])

    row = pl.BlockSpec((ts, D), lambda i: (i, 0))
    vecd = pl.BlockSpec((1, D), lambda i: (0, 0))
    return pl.pallas_call(
        body, name="bwd_in", grid=(S // ts,),
        in_specs=[pl.BlockSpec((nb, ts, C), lambda i: (0, i, 0)), pl.BlockSpec((nb, D, C), lambda i: (0, 0, 0)),
                  row, vecd, row],
        out_specs=[row, vecd],
        out_shape=[jax.ShapeDtypeStruct((S, D), F32), jax.ShapeDtypeStruct((1, D), F32)],
        compiler_params=_params("arbitrary"))(dz, w_in, x, g, dx1)


def _wgrad(a, b, name, comm=None):
    na, S, K = a.shape
    nb, _, N = b.shape
    nj = max(na, nb)
    assert min(na, nb) == 1
    ts = _tile(S, 1024)
    ns = S // ts
    grp = max(g for g in range(1, nj + 1) if nj % g == 0 and g * K * N * 4 <= WGRAD_ACC_BYTES)
    ga, gb = (grp if na > 1 else 1), (grp if nb > 1 else 1)

    def body(a_ref, b_ref, o_ref, acc):
        s = pl.program_id(1)

        @pl.when(s == 0)
        def _():
            acc[...] = jnp.zeros_like(acc)

        for k in range(grp):
            acc[k] += _mm_tn(a_ref[k if na > 1 else 0], b_ref[k if nb > 1 else 0])

        @pl.when(s == ns - 1)
        def _():
            o_ref[...] = acc[...].astype(o_ref.dtype)

    res = _pcall(
        comm, body, name=name, grid=(nj // grp, ns),
        in_specs=[pl.BlockSpec((ga, ts, K), (lambda j, s: (j, s, 0)) if na > 1 else (lambda j, s: (0, s, 0))),
                  pl.BlockSpec((gb, ts, N), (lambda j, s: (j, s, 0)) if nb > 1 else (lambda j, s: (0, s, 0)))],
        out_specs=pl.BlockSpec((grp, K, N), lambda j, s: (j, 0, 0)),
        out_shape=jax.ShapeDtypeStruct((nj, K, N), WIRE_DTYPE),
        scratch_shapes=[pltpu.VMEM((grp, K, N), F32)],
        semantics=("parallel", "arbitrary"))(a, b)
    return res[0] if comm is None else (res[0][0], res[1])


def _place():
    x, y, c = lax.axis_index("x"), lax.axis_index("y"), lax.axis_index("c")
    other_chips = [(1 - x, y), (x, 1 - y), (1 - x, 1 - y)]
    return x, y, c, other_chips


def _gather_weights(shards):
    nt = len(shards)

    def body(*refs):
        src, dst = refs[:nt], refs[nt:2 * nt]
        ici_send, ici_recv, d2d_send, d2d_recv, own_send, own_recv = refs[2 * nt:]
        x, y, c, chips = _place()
        mine = 2 * x + y

        def half(t, pc):
            hr = src[t].shape[0] // 2
            return pl.ds(pc * hr, hr)

        def own(t):
            return pltpu.make_async_remote_copy(
                src_ref=src[t], dst_ref=dst[t].at[mine], send_sem=own_send.at[t], recv_sem=own_recv.at[t],
                device_id=(x, y, 1 - c), device_id_type=MESH)

        def ici(t, k, block, to):
            cx, cy = block
            ref = dst[t].at[2 * cx + cy, half(t, c)]
            return pltpu.make_async_remote_copy(
                src_ref=src[t].at[half(t, c)] if to is not None else ref, dst_ref=ref,
                send_sem=ici_send.at[t, k], recv_sem=ici_recv.at[t, k],
                device_id=(*to, c) if to is not None else (x, y, c), device_id_type=MESH)

        def d2d(t, k, block, pc):
            cx, cy = block
            ref = dst[t].at[2 * cx + cy, half(t, pc)]
            return pltpu.make_async_remote_copy(
                src_ref=ref, dst_ref=ref, send_sem=d2d_send.at[t, k], recv_sem=d2d_recv.at[t, k],
                device_id=(x, y, 1 - c), device_id_type=MESH)

        sends = [ici(t, k, (x, y), chip) for t in range(nt) for k, chip in enumerate(chips)]
        sends += [own(t) for t in range(nt)]
        for cp in sends:
            cp.start()
        passed = []
        for t in range(nt):
            for k, chip in enumerate(chips):
                ici(t, k, chip, None).wait_recv()
                fw = d2d(t, k, chip, c)
                fw.start()
                passed.append(fw)
        for t in range(nt):
            own(t).wait_recv()
            for k, chip in enumerate(chips):
                d2d(t, k, chip, 1 - c).wait_recv()
        for cp in sends + passed:
            cp.wait_send()

    return pl.pallas_call(
        body, name="gather_weights",
        in_specs=[ANY] * nt, out_specs=[ANY] * nt,
        out_shape=[jax.ShapeDtypeStruct((N_CHIPS,) + s.shape, s.dtype) for s in shards],
        scratch_shapes=[pltpu.SemaphoreType.DMA((nt, 3))] * 4 + [pltpu.SemaphoreType.DMA((nt,))] * 2,
        compiler_params=pltpu.CompilerParams(has_side_effects=True))(*shards)


def _gather_over_ici(shards):
    nt = len(shards)

    def copies(src, dst, scr, arriving):
        ici_send, ici_recv, own_send, own_recv = scr
        x, y, c, chips = _place()
        out = []
        for t in range(nt):
            hr = src[t].shape[0] // 2
            rows = pl.ds(c * hr, hr)
            for k, (cx, cy) in enumerate(chips):
                block = 2 * cx + cy if arriving else 2 * x + y
                out.append(pltpu.make_async_remote_copy(
                    src_ref=src[t].at[rows], dst_ref=dst[t].at[block, rows],
                    send_sem=ici_send.at[t, k], recv_sem=ici_recv.at[t, k],
                    device_id=(cx, cy, c), device_id_type=MESH))
            out.append(pltpu.make_async_remote_copy(
                src_ref=src[t], dst_ref=dst[t].at[2 * x + y], send_sem=own_send.at[t], recv_sem=own_recv.at[t],
                device_id=(x, y, 1 - c), device_id_type=MESH))
        return out

    def start(src, dst, scr):
        for cp in copies(src, dst, scr, False):
            cp.start()

    def finish(src, dst, scr):
        for cp in copies(src, dst, scr, True):
            cp.wait_recv()
        for cp in copies(src, dst, scr, False):
            cp.wait_send()

    return _Comm(shards, [jax.ShapeDtypeStruct((N_CHIPS,) + s.shape, s.dtype) for s in shards],
                 [pltpu.SemaphoreType.DMA((nt, 3))] * 2 + [pltpu.SemaphoreType.DMA((nt,))] * 2, start, finish)


def _gather_pass_on(bufs):
    nt = len(bufs)

    def passed(dst, scr, t, k, block, pc):
        send, recv = scr
        x, y, c, _ = _place()
        cx, cy = block
        hr = dst[t].shape[1] // 2
        ref = dst[t].at[2 * cx + cy, pl.ds(pc * hr, hr)]
        return pltpu.make_async_remote_copy(src_ref=ref, dst_ref=ref, send_sem=send.at[t, k], recv_sem=recv.at[t, k],
                                            device_id=(x, y, 1 - c), device_id_type=MESH)

    def start(src, dst, scr):
        _, _, c, chips = _place()
        for t in range(nt):
            for k, chip in enumerate(chips):
                passed(dst, scr, t, k, chip, c).start()

    def finish(src, dst, scr):
        _, _, c, chips = _place()
        for t in range(nt):
            for k, chip in enumerate(chips):
                passed(dst, scr, t, k, chip, 1 - c).wait_recv()
        for t in range(nt):
            for k, chip in enumerate(chips):
                passed(dst, scr, t, k, chip, c).wait_send()

    return _Comm(bufs, [jax.ShapeDtypeStruct(b.shape, b.dtype) for b in bufs],
                 [pltpu.SemaphoreType.DMA((nt, 3))] * 2, start, finish, aliases={t: t for t in range(nt)})


def _exchange_halves(grads):
    nt = len(grads)

    def copies(src, dst, scr):
        send, recv = scr
        x, y, c, _ = _place()
        out = []
        for t in range(nt):
            hr = src[t].shape[1] // 2
            out.append(pltpu.make_async_remote_copy(
                src_ref=src[t].at[:, pl.ds((1 - c) * hr, hr)], dst_ref=dst[t],
                send_sem=send.at[t], recv_sem=recv.at[t], device_id=(x, y, 1 - c), device_id_type=MESH))
        return out

    def start(src, dst, scr):
        for cp in copies(src, dst, scr):
            cp.start()

    def finish(src, dst, scr):
        for cp in copies(src, dst, scr):
            cp.wait()

    return _Comm(grads, [jax.ShapeDtypeStruct((g.shape[0], g.shape[1] // 2, g.shape[2]), g.dtype) for g in grads],
                 [pltpu.SemaphoreType.DMA((nt,))] * 2, start, finish)


def _add_halves(grad, other, name):
    nb, R, C = grad.shape
    hr = R // 2
    tr = _tile(hr, 256, 16)
    steps = hr // tr
    c = lax.axis_index("c").astype(jnp.int32).reshape((1,))

    def body(c_ref, a_ref, b_ref, o_ref):
        o_ref[...] = (a_ref[...].astype(F32) + b_ref[...].astype(F32)).astype(o_ref.dtype)

    return pl.pallas_call(
        body, name=name,
        grid_spec=pltpu.PrefetchScalarGridSpec(
            num_scalar_prefetch=1, grid=(nb, steps),
            in_specs=[pl.BlockSpec((1, tr, C), lambda j, i, c_ref: (j, c_ref[0] * steps + i, 0)),
                      pl.BlockSpec((1, tr, C), lambda j, i, c_ref: (j, i, 0))],
            out_specs=pl.BlockSpec((1, tr, C), lambda j, i, c_ref: (j, i, 0))),
        out_shape=jax.ShapeDtypeStruct((nb, hr, C), grad.dtype),
        compiler_params=_params("parallel", "parallel"))(c, grad, other)


def _scatter_chip_sums(parts):
    nt = len(parts)

    def copies(src, dst, scr):
        send, recv = scr
        x, y, c, chips = _place()
        out = []
        for t in range(nt):
            for k, (cx, cy) in enumerate(chips):
                out.append(pltpu.make_async_remote_copy(
                    src_ref=src[t].at[2 * cx + cy], dst_ref=dst[t].at[k],
                    send_sem=send.at[t, k], recv_sem=recv.at[t, k], device_id=(cx, cy, c), device_id_type=MESH))
        return out

    def start(src, dst, scr):
        for cp in copies(src, dst, scr):
            cp.start()

    def finish(src, dst, scr):
        for cp in copies(src, dst, scr):
            cp.wait()

    return _Comm(parts, [jax.ShapeDtypeStruct((3,) + p.shape[1:], p.dtype) for p in parts],
                 [pltpu.SemaphoreType.DMA((nt, 3))] * 2, start, finish)


def _sum_chips(part, recv, name):
    _, hr, C = part.shape
    tr = _tile(hr, 256, 16)
    steps = hr // tr
    where = jnp.stack([2 * lax.axis_index("x") + lax.axis_index("y"), lax.axis_index("c")]).astype(jnp.int32)

    def body(w_ref, a_ref, b_ref, o_ref):
        acc = a_ref[0].astype(F32)
        for k in range(3):
            acc = acc + b_ref[k].astype(F32)
        o_ref[...] = acc

    return pl.pallas_call(
        body, name=name,
        grid_spec=pltpu.PrefetchScalarGridSpec(
            num_scalar_prefetch=1, grid=(steps,),
            in_specs=[pl.BlockSpec((1, tr, C), lambda i, w_ref: (w_ref[0], i, 0)),
                      pl.BlockSpec((3, tr, C), lambda i, w_ref: (0, i, 0))],
            out_specs=pl.BlockSpec((tr, C), lambda i, w_ref: (w_ref[1] * steps + i, 0))),
        out_shape=jax.ShapeDtypeStruct((2 * hr, C), F32),
        compiler_params=_params("parallel"))(where, part, recv)


def _join_halves(bufs):
    nt = len(bufs)

    def swap(dst, scr, t, pc):
        send, recv = scr
        x, y, c, _ = _place()
        hr = dst[t].shape[0] // 2
        rows = dst[t].at[pl.ds(pc * hr, hr)]
        return pltpu.make_async_remote_copy(src_ref=rows, dst_ref=rows, send_sem=send.at[t], recv_sem=recv.at[t],
                                            device_id=(x, y, 1 - c), device_id_type=MESH)

    def start(src, dst, scr):
        c = lax.axis_index("c")
        for t in range(nt):
            swap(dst, scr, t, c).start()

    def finish(src, dst, scr):
        c = lax.axis_index("c")
        for t in range(nt):
            swap(dst, scr, t, 1 - c).wait_recv()
        for t in range(nt):
            swap(dst, scr, t, c).wait_send()

    return _Comm(bufs, [jax.ShapeDtypeStruct(b.shape, b.dtype) for b in bufs],
                 [pltpu.SemaphoreType.DMA((nt,))] * 2, start, finish, aliases={t: t for t in range(nt)})


def _reduce_scatter_in_vmem(g):
    nb, R, C = g.shape
    hr = R // 2

    def run(ins, outs, scr):
        (g_ref,), (out_ref,) = ins, outs
        other, part, got, send, recv = scr
        x, y, c, chips = _place()
        sibling = (x, y, 1 - c)
        my_rows = pl.ds(pl.multiple_of(c * hr, hr), hr)
        their_rows = pl.ds(pl.multiple_of((1 - c) * hr, hr), hr)
        swap = pltpu.make_async_remote_copy(src_ref=g_ref.at[:, their_rows], dst_ref=other, send_sem=send.at[0],
                                            recv_sem=recv.at[0], device_id=sibling, device_id_type=MESH)
        swap.start()
        swap.wait()
        part[...] = (g_ref[:, my_rows, :].astype(F32) + other[...].astype(F32)).astype(part.dtype)
        to_owner = [pltpu.make_async_remote_copy(src_ref=part.at[2 * cx + cy], dst_ref=got.at[k],
                                                 send_sem=send.at[1 + k], recv_sem=recv.at[1 + k],
                                                 device_id=(cx, cy, c), device_id_type=MESH)
                    for k, (cx, cy) in enumerate(chips)]
        for cp in to_owner:
            cp.start()
        for cp in to_owner:
            cp.wait()
        total = part[2 * x + y].astype(F32)
        for k in range(3):
            total = total + got[k].astype(F32)
        out_ref[my_rows, :] = total

        def join(rows):
            return pltpu.make_async_remote_copy(src_ref=out_ref.at[rows], dst_ref=out_ref.at[rows], send_sem=send.at[4],
                                                recv_sem=recv.at[4], device_id=sibling, device_id_type=MESH)

        join(my_rows).start()
        join(their_rows).wait_recv()
        join(my_rows).wait_send()

    return _Comm([g], [jax.ShapeDtypeStruct((R, C), F32)],
                 [pltpu.VMEM((nb, hr, C), g.dtype), pltpu.VMEM((nb, hr, C), g.dtype), pltpu.VMEM((3, hr, C), g.dtype),
                  pltpu.SemaphoreType.DMA((5,)), pltpu.SemaphoreType.DMA((5,))],
                 run, lambda ins, outs, scr: None, in_specs=[WHOLE_VMEM], out_specs=[WHOLE_VMEM])


def _all_reduce_rows(buf, loss_row=None):
    R, L = buf.shape

    def copies(in_ref, gath, send, recv):
        x, y, c, _ = _place()
        out = []
        for k in range(1, N_DEV):
            peer = (x ^ ((k >> 2) & 1), y ^ ((k >> 1) & 1), c ^ (k & 1))
            out.append(pltpu.make_async_remote_copy(
                src_ref=in_ref, dst_ref=gath.at[k], send_sem=send.at[k - 1], recv_sem=recv.at[k - 1],
                device_id=peer, device_id_type=MESH))
        return out

    def start(ins, outs, scr):
        gath, send, recv = scr
        gath[0] = ins[0][...]
        for cp in copies(ins[0], gath, send, recv):
            cp.start()

    def finish(ins, outs, scr):
        gath, send, recv = scr
        for cp in copies(ins[0], gath, send, recv):
            cp.wait()
        x, y, c, _ = _place()
        me = 4 * x + 2 * y + c
        total = gath[me]
        for d in range(1, N_DEV):
            total = total + gath[d ^ me]
        outs[0][...] = total
        if loss_row is not None:
            outs[1][...] = jnp.sum(total[loss_row:loss_row + 1, :], axis=1, keepdims=True)

    out_shape = [jax.ShapeDtypeStruct((R, L), F32)]
    if loss_row is not None:
        out_shape.append(jax.ShapeDtypeStruct((1, 1), F32))
    return _Comm([buf], out_shape,
                 [pltpu.VMEM((N_DEV, R, L), F32), pltpu.SemaphoreType.DMA((N_DEV - 1,)),
                  pltpu.SemaphoreType.DMA((N_DEV - 1,))],
                 start, finish, in_specs=[WHOLE_VMEM], out_specs=[WHOLE_VMEM] * len(out_shape))


def _adamw_update(w_ref, g_ref, m_ref, v_ref, d_ref, nm_ref, nv_ref):
    gv = g_ref[...]
    nm = ADAM_B1 * m_ref[...] + (1.0 - ADAM_B1) * gv
    nv = ADAM_B2 * v_ref[...] + (1.0 - ADAM_B2) * (gv * gv)
    nm_ref[...] = nm
    nv_ref[...] = nv
    m_hat = nm / (1.0 - ADAM_B1 ** ADAM_STEP)
    v_hat = nv / (1.0 - ADAM_B2 ** ADAM_STEP)
    d_ref[...] = -ADAM_LR * (m_hat / (jnp.sqrt(v_hat) + ADAM_EPS) + ADAM_WD * w_ref[...])


def _adamw(w, g, m, v, name):
    R, C = w.shape
    tr = _tile(R, 256)

    def body(w_ref, g_ref, m_ref, v_ref, d_ref, nm_ref, nv_ref, g_out):
        _adamw_update(w_ref, g_ref, m_ref, v_ref, d_ref, nm_ref, nv_ref)
        g_out[...] = g_ref[...]

    blk = pl.BlockSpec((tr, C), lambda i: (i, 0))
    return pl.pallas_call(
        body, name=name, grid=(R // tr,), in_specs=[blk] * 4, out_specs=[blk] * 4,
        out_shape=[jax.ShapeDtypeStruct((R, C), F32)] * 4,
        compiler_params=_params("parallel"))(w, g, m, v)


def _adamw_many(ws, gs, ms, vs, name):
    n = len(ws)

    def body(*refs):
        for k in range(n):
            _adamw_update(*[refs[part * n + k] for part in range(7)])

    shapes = [jax.ShapeDtypeStruct(w.shape, F32) for w in ws]
    outs = pl.pallas_call(
        body, name=name, in_specs=[WHOLE_VMEM] * (4 * n), out_specs=[WHOLE_VMEM] * (3 * n), out_shape=shapes * 3,
        compiler_params=pltpu.CompilerParams(vmem_limit_bytes=VMEM_LIMIT_BYTES))(*ws, *gs, *ms, *vs)
    return outs[:n], outs[n:2 * n], outs[2 * n:]


def _pack_rows(arrays):
    rows = []
    for a in arrays:
        flat = a.reshape(-1).astype(F32)
        pad = (-flat.shape[0]) % LANES
        rows.append(jnp.pad(flat, (0, pad)).reshape(-1, LANES))
    buf = jnp.concatenate(rows, axis=0)
    return jnp.pad(buf, ((0, (-buf.shape[0]) % 8), (0, 0)))


def _unpack_rows(buf, shapes):
    out, r = [], 0
    for s in shapes:
        n = math.prod(s)
        nr = -(-n // LANES)
        out.append(buf[r:r + nr].reshape(-1)[:n].reshape(s))
        r += nr
    return out


def _block_diag(w):
    H, a, b = w.shape
    eye = jnp.eye(H, dtype=w.dtype)
    return (eye[:, None, :, None] * w[:, :, None, :]).reshape(H * a, H * b)


def _block_diag_parts(d, H):
    a, b = d.shape[0] // H, d.shape[1] // H
    d4 = d.reshape(H, a, H, b)
    return jnp.stack([d4[h, :, h, :] for h in range(H)])


def _rs_add(names, grads, others):
    return [_add_halves(g, o, "rs_add_halves_" + n) for n, g, o in zip(names, grads, others)]


def _rs_sum(names, parts, recvs):
    return [_sum_chips(p, r, "rs_sum_chips_" + n) for n, p, r in zip(names, parts, recvs)]


def _step(x, mem, target, shards, small, tap_rows, tap_shapes):
    D = x.shape[1]
    nch = N_CHIPS
    p = dict(small)

    (w_in_f,) = _gather_weights([shards['w_in']])
    wf = {}

    def ici(names):
        return _gather_over_ici([shards[n] for n in names])

    ici_a, taps_sum = ici(['w_out', 'w_q']), _all_reduce_rows(tap_rows)
    (z, h1), couts = _fwd_in(x, p['mix_norm_g'], w_in_f, comm=_merge(ici_a, taps_sum))
    bufs_a, (taps,) = _split(couts, ici_a, taps_sum)
    p.update(zip(COL_SHARDED_SMALL, _unpack_rows(taps, tap_shapes)))
    wa_d = _block_diag(p['lru_w_a']).astype(MXU_DTYPE)
    wx_d = _block_diag(p['lru_w_x']).astype(MXU_DTYPE)
    heads = p['lru_w_a'].shape[0]
    pass_a, ici_b = _gather_pass_on(bufs_a), ici(['w_kv', 'w_o'])
    (h, y_lru), couts = _lru_fwd(z, p['lru_conv_w'], p['lru_conv_b'], wa_d, p['lru_b_a'], wx_d, p['lru_b_x'],
                                 p['lru_lambda'], comm=_merge(pass_a, ici_b))
    (wf['w_out'], wf['w_q']), bufs_b = _split(couts, pass_a, ici_b)
    pass_b, ici_c = _gather_pass_on(bufs_b), ici(['w_up'])
    (c1, c3), couts = _conf_fwd(z, p['conf_conv_w'], p['conf_conv_b'], p['conf_ln_g'], p['conf_ln_b'],
                                comm=_merge(pass_b, ici_c))
    (wf['w_kv'], wf['w_o']), bufs_c = _split(couts, pass_b, ici_c)
    w_out2 = wf['w_out'].reshape(2, -1, D)
    w_q = wf['w_q'].reshape(D, D)
    w_o = wf['w_o'].reshape(D, D)
    pass_c, ici_d = _gather_pass_on(bufs_c), ici(['w_down'])
    (x1, h2, q), couts = _fwd_out_q(x, y_lru, c3, w_out2, p['xa_norm_g'], w_q, comm=_merge(pass_c, ici_d))
    (wf['w_up'],), bufs_d = _split(couts, pass_c, ici_d)
    m, kv = _kv_fwd(mem, p['mem_norm_g'], wf['w_kv'])
    (o, x2, h3), (wf['w_down'],) = _attn_fwd(q, kv, x1, w_o, p['ffn_norm_g'], comm=_gather_pass_on(bufs_d))
    ffn_g, ffn_ge, ffn_u_dge, act, dx3, loss_lanes, d_final_g = _ffn_fwd(
        h3, wf['w_up'], p['ffn_conv_w'], p['ffn_conv_b'], wf['w_down'], x2, p['final_norm_g'], target)

    dgu, dx2, d_ffn_g, d_ffn_cw, d_ffn_cb = _ffn_bwd(dx3, wf['w_down'], wf['w_up'], ffn_g, ffn_ge, ffn_u_dge, x2,
                                                     p['ffn_norm_g'], p['ffn_conv_w'])
    g_down = _wgrad(act, dx3[None], "wgrad_down").reshape(nch, -1, D)
    g_up, other = _wgrad(h3[None], dgu, "wgrad_up", comm=_exchange_halves([g_down]))
    (p_down,) = _rs_add(['w_down'], [g_down], other)
    sc_down, ex_up = _scatter_chip_sums([p_down]), _exchange_halves([g_up])
    (dq, dx1, dkv, d_xa_g), couts = _attn_bwd(dx2, w_o, q, kv, x1, p['xa_norm_g'], w_q, comm=_merge(sc_down, ex_up))
    recv, other = _split(couts, sc_down, ex_up)
    f_down = _rs_sum(['w_down'], [p_down], recv)
    (p_up,) = _rs_add(['w_up'], [g_up], other)
    mid = ['w_o', 'w_q', 'w_kv']
    g_o = _wgrad(o[None], dx2[None], "wgrad_o").reshape(nch, -1, D)
    g_q = _wgrad(h2[None], dq[None], "wgrad_q").reshape(nch, -1, D)
    g_kv, d_mem_g = _kv_bwd(dkv, wf['w_kv'], mem, p['mem_norm_g'], m)
    join_down, sc_up, ex_mid = _join_halves(f_down), _scatter_chip_sums([p_up]), _exchange_halves([g_o, g_q, g_kv])
    (dz_c, d_conf_cw, d_conf_cb, d_ln_g, d_ln_b), couts = _conf_bwd(
        dx1, w_out2[1], z, c1, p['conf_conv_w'], p['conf_ln_g'], p['conf_ln_b'],
        comm=_merge(join_down, sc_up, ex_mid))
    (r_down,), recv, other = _split(couts, join_down, sc_up, ex_mid)
    p_up = [p_up]
    p_mid = _rs_add(mid, [g_o, g_q, g_kv], other)
    join_up, sc_mid = _join_halves(_rs_sum(['w_up'], p_up, recv)), _scatter_chip_sums(p_mid)
    (dz, d_wa, d_wx, d_ba, d_bx, d_lam, d_lru_cw, d_lru_cb), couts = _lru_bwd(
        dx1, w_out2[0], z, h, p['lru_conv_w'], p['lru_conv_b'], wa_d, p['lru_b_a'], wx_d, p['lru_b_x'],
        p['lru_lambda'], dz_c, comm=_merge(join_up, sc_mid))
    (r_up,), recv = _split(couts, join_up, sc_mid)
    f_mid = _rs_sum(mid, p_mid, recv)
    grad_x, d_mix_g = _bwd_in(dz, w_in_f, x, p['mix_norm_g'], dx1)

    small_g = {'mix_norm_g': d_mix_g, 'lru_conv_w': d_lru_cw, 'lru_conv_b': d_lru_cb,
               'lru_w_a': _block_diag_parts(d_wa, heads), 'lru_b_a': d_ba,
               'lru_w_x': _block_diag_parts(d_wx, heads), 'lru_b_x': d_bx, 'lru_lambda': d_lam,
               'conf_conv_w': d_conf_cw, 'conf_conv_b': d_conf_cb, 'conf_ln_g': d_ln_g, 'conf_ln_b': d_ln_b,
               'xa_norm_g': d_xa_g, 'mem_norm_g': d_mem_g, 'ffn_norm_g': d_ffn_g,
               'ffn_conv_w': d_ffn_cw, 'ffn_conv_b': d_ffn_cb, 'final_norm_g': d_final_g}
    names = list(small_g)
    shapes = [small_g[n].shape for n in names]
    join_mid = _join_halves(f_mid)
    small_sum = _all_reduce_rows(_pack_rows([loss_lanes] + [small_g[n] for n in names]), loss_row=0)
    g_in, couts = _wgrad(h1[None], dz, "wgrad_in", comm=_merge(join_mid, small_sum))
    r_mid, (summed, loss) = _split(couts, join_mid, small_sum)
    g_out_l, other = _wgrad(y_lru[None], dx1[None], "wgrad_out_lru", comm=_exchange_halves([g_in]))
    p_in = _rs_add(['w_in'], [g_in], other)
    g_out_c, recv = _wgrad(c3[None], dx1[None], "wgrad_out_conf", comm=_scatter_chip_sums(p_in))
    f_in = _rs_sum(['w_in'], p_in, recv)

    g_out = jnp.concatenate([g_out_l, g_out_c], axis=0).reshape(nch, -1, D)
    join_in, rs_out = _join_halves(f_in), _reduce_scatter_in_vmem(g_out)
    (r_in,), (r_out,) = _split(_run_comm(_merge(join_in, rs_out), "rs_last"), join_in, rs_out)
    big = dict(zip(['w_down', 'w_up'] + mid + ['w_out', 'w_in'], [r_down, r_up] + r_mid + [r_out, r_in]))
    return grad_x, big, summed, loss, names, [loss_lanes.shape] + shapes


def kernel(x, mem, mix_norm_g, w_in, lru_conv_w, lru_conv_b, lru_w_a, lru_b_a, lru_w_x, lru_b_x, lru_lambda, conf_conv_w, conf_conv_b, conf_ln_g, conf_ln_b, w_out, xa_norm_g, mem_norm_g, w_q, w_kv, w_o, ffn_norm_g, w_up, ffn_conv_w, ffn_conv_b, w_down, final_norm_g, loss_target, m_mix_norm_g, m_w_in, m_lru_conv_w, m_lru_conv_b, m_lru_w_a, m_lru_b_a, m_lru_w_x, m_lru_b_x, m_lru_lambda, m_conf_conv_w, m_conf_conv_b, m_conf_ln_g, m_conf_ln_b, m_w_out, m_xa_norm_g, m_mem_norm_g, m_w_q, m_w_kv, m_w_o, m_ffn_norm_g, m_w_up, m_ffn_conv_w, m_ffn_conv_b, m_w_down, m_final_norm_g, v_mix_norm_g, v_w_in, v_lru_conv_w, v_lru_conv_b, v_lru_w_a, v_lru_b_a, v_lru_w_x, v_lru_b_x, v_lru_lambda, v_conf_conv_w, v_conf_conv_b, v_conf_ln_g, v_conf_ln_b, v_w_out, v_xa_norm_g, v_mem_norm_g, v_w_q, v_w_kv, v_w_o, v_ffn_norm_g, v_w_up, v_ffn_conv_w, v_ffn_conv_b, v_w_down, v_final_norm_g):
    given = dict(locals())
    w = {n: given[n] for n in WEIGHTS}
    mom = {n: given["m_" + n] for n in WEIGHTS}
    var = {n: given["v_" + n] for n in WEIGHTS}
    xi, yi, ci = lax.axis_index("x"), lax.axis_index("y"), lax.axis_index("c")
    chip = 2 * xi + yi

    shards = {n: w[n][0].astype(WIRE_DTYPE) for n in BIG}
    tap_full = []
    for n in COL_SHARDED_SMALL:
        s = w[n][0]
        full = jnp.zeros((s.shape[0], N_CHIPS * s.shape[1]), F32)
        s = jnp.where(ci == 0, s, jnp.zeros_like(s))
        tap_full.append(lax.dynamic_update_slice(full, s, (0, chip * s.shape[1])))
    small = {n: (w[n] if w[n].ndim == 1 else w[n][0]) for n in SMALL if n not in COL_SHARDED_SMALL}
    small = {n: (a.reshape(1, -1) if a.ndim == 1 else a) for n, a in small.items()}

    grad_x, big_g, summed, loss, small_names, packed_shapes = _step(
        x[0], mem[0], loss_target[0], shards, small, _pack_rows(tap_full), [t.shape for t in tap_full])
    small_sum = dict(zip(small_names, _unpack_rows(summed, packed_shapes)[1:]))

    grads = {}
    for n in WEIGHTS:
        if n in BIG:
            g = big_g[n]
        elif n in COL_SHARDED_SMALL:
            width = w[n].shape[-1]
            g = lax.dynamic_slice_in_dim(small_sum[n], chip * width, width, axis=1)
        else:
            g = small_sum[n]
        grads[n] = g.reshape(w[n].shape)

    delta, new_m, new_v = {}, {}, {}
    for n in BIG:
        d, nm, nv, g = _adamw(w[n][0], grads[n][0], mom[n][0], var[n][0], "adamw_" + n)
        delta[n], new_m[n], new_v[n], grads[n] = d[None], nm[None], nv[None], g[None]
    flat = lambda a: a.reshape(-1, a.shape[-1])
    outs = _adamw_many(*[[flat(src[n]) for n in SMALL] for src in (w, grads, mom, var)], "adamw_small")
    for out, arrays in zip((delta, new_m, new_v), outs):
        out.update({n: a.reshape(w[n].shape) for n, a in zip(SMALL, arrays)})

    return (loss[0, 0], grad_x[None], *[grads[n] for n in WEIGHTS], *[delta[n] for n in WEIGHTS],
            *[new_m[n] for n in WEIGHTS], *[new_v[n] for n in WEIGHTS])
```

```python
import math

import jax
import jax.numpy as jnp
from jax import lax
from jax.experimental import pallas as pl
from jax.experimental.pallas import tpu as pltpu

F32 = jnp.float32
MXU_DTYPE = jnp.bfloat16
WIRE_DTYPE = jnp.bfloat16
EPS = 1e-6
RG_C = 8.0
XA_HEADS = 4
ADAM_LR, ADAM_B1, ADAM_B2, ADAM_EPS, ADAM_WD, ADAM_STEP = 0.001, 0.9, 0.999, 1e-08, 0.01, 10
VMEM_LIMIT_BYTES = 52 * 1024 * 1024
WGRAD_ACC_BYTES = 8 * 1024 * 1024
LANES = 1024
N_CHIPS = 4
N_DEV = 8
MESH = pl.DeviceIdType.MESH
GELU_C = math.sqrt(2.0 / math.pi)
GELU_K = 0.044715

WEIGHTS = ['mix_norm_g', 'w_in', 'lru_conv_w', 'lru_conv_b', 'lru_w_a', 'lru_b_a', 'lru_w_x', 'lru_b_x',
           'lru_lambda', 'conf_conv_w', 'conf_conv_b', 'conf_ln_g', 'conf_ln_b', 'w_out', 'xa_norm_g',
           'mem_norm_g', 'w_q', 'w_kv', 'w_o', 'ffn_norm_g', 'w_up', 'ffn_conv_w', 'ffn_conv_b', 'w_down',
           'final_norm_g']
BIG = ['w_in', 'w_kv', 'w_up', 'w_out', 'w_q', 'w_o', 'w_down']
SMALL = [n for n in WEIGHTS if n not in BIG]
COL_SHARDED_SMALL = ['lru_conv_w', 'conf_conv_w', 'ffn_conv_w']


def _params(*semantics):
    return pltpu.CompilerParams(dimension_semantics=semantics, vmem_limit_bytes=VMEM_LIMIT_BYTES)


ANY = pl.BlockSpec(memory_space=pl.ANY)
WHOLE_VMEM = pl.BlockSpec(memory_space=pltpu.VMEM)


class _Comm:
    def __init__(self, arrays, out_shapes, scratch, start, finish, aliases=None, in_specs=None, out_specs=None):
        self.arrays, self.out_shapes, self.scratch = list(arrays), list(out_shapes), list(scratch)
        self.start, self.finish = start, finish
        self.aliases = dict(aliases or {})
        self.in_specs = list(in_specs) if in_specs is not None else [ANY] * len(self.arrays)
        self.out_specs = list(out_specs) if out_specs is not None else [ANY] * len(self.out_shapes)


def _merge(*comms):
    comms = [c for c in comms if c is not None]
    if not comms:
        return None
    ai = [0]
    for c in comms:
        ai.append(ai[-1] + len(c.arrays))
    oi = [0]
    for c in comms:
        oi.append(oi[-1] + len(c.out_shapes))
    si = [0]
    for c in comms:
        si.append(si[-1] + len(c.scratch))

    def each(which):
        def run(ins, outs, scr):
            for k, c in enumerate(comms):
                getattr(c, which)(ins[ai[k]:ai[k + 1]], outs[oi[k]:oi[k + 1]], scr[si[k]:si[k + 1]])
        return run

    aliases = {ai[k] + i: oi[k] + o for k, c in enumerate(comms) for i, o in c.aliases.items()}
    return _Comm(sum((c.arrays for c in comms), []), sum((c.out_shapes for c in comms), []),
                 sum((c.scratch for c in comms), []), each("start"), each("finish"), aliases,
                 sum((c.in_specs for c in comms), []), sum((c.out_specs for c in comms), []))


def _split(outs, *comms):
    parts, at = [], 0
    for c in comms:
        parts.append(outs[at:at + len(c.out_shapes)])
        at += len(c.out_shapes)
    return parts


def _pcall(comm, body, *, name, grid, in_specs, out_specs, out_shape, semantics, scratch_shapes=(), aliases=None):
    single = not isinstance(out_shape, (list, tuple))
    out_shape = [out_shape] if single else list(out_shape)
    out_specs = [out_specs] if single else list(out_specs)
    in_specs, scratch_shapes = list(in_specs), list(scratch_shapes)
    aliases = dict(aliases or {})

    if comm is None:
        def plain(*args):
            return list(pl.pallas_call(body, name=name, grid=grid, in_specs=in_specs, out_specs=out_specs,
                                       out_shape=out_shape, scratch_shapes=scratch_shapes,
                                       input_output_aliases=aliases,
                                       compiler_params=_params(*semantics))(*args))
        return plain

    def hosted(*args):
        n_in, n_out, n_scr = len(args), len(out_shape), len(scratch_shapes)
        c_in, c_out = len(comm.arrays), len(comm.out_shapes)

        def wrapped(*refs):
            ins, cins = refs[:n_in], refs[n_in:n_in + c_in]
            o0 = n_in + c_in
            outs, couts = refs[o0:o0 + n_out], refs[o0 + n_out:o0 + n_out + c_out]
            s0 = o0 + n_out + c_out
            scr, cscr = refs[s0:s0 + n_scr], refs[s0 + n_scr:]
            first = last = None
            for axis, size in enumerate(grid):
                at_start, at_end = pl.program_id(axis) == 0, pl.program_id(axis) == size - 1
                first = at_start if first is None else first & at_start
                last = at_end if last is None else last & at_end
            if first is None:
                comm.start(cins, couts, cscr)
                body(*ins, *outs, *scr)
                comm.finish(cins, couts, cscr)
                return
            pl.when(first)(lambda: comm.start(cins, couts, cscr))
            body(*ins, *outs, *scr)
            pl.when(last)(lambda: comm.finish(cins, couts, cscr))

        res = pl.pallas_call(
            wrapped, name=name, grid=grid, in_specs=in_specs + comm.in_specs, out_specs=out_specs + comm.out_specs,
            out_shape=out_shape + comm.out_shapes, scratch_shapes=scratch_shapes + comm.scratch,
            input_output_aliases={**aliases, **{n_in + i: n_out + o for i, o in comm.aliases.items()}},
            compiler_params=pltpu.CompilerParams(dimension_semantics=("arbitrary",) * len(grid),
                                                 vmem_limit_bytes=VMEM_LIMIT_BYTES, has_side_effects=True),
        )(*args, *comm.arrays)
        return list(res[:n_out]), list(res[n_out:])

    return hosted


def _run_comm(comm, name):
    return _pcall(comm, lambda: None, name=name, grid=(), in_specs=[], out_specs=[], out_shape=[], semantics=())()[1]


def _tile(n, want, align=8):
    if n <= want:
        return n
    for t in range(want - want % align, 0, -align):
        if n % t == 0:
            return t
    raise ValueError((n, want, align))


def _mm(a, b):
    return jnp.dot(a.astype(MXU_DTYPE), b.astype(MXU_DTYPE), preferred_element_type=F32)


def _mm_nt(a, b):
    return lax.dot_general(a.astype(MXU_DTYPE), b.astype(MXU_DTYPE), (((1,), (1,)), ((), ())),
                           preferred_element_type=F32)


def _mm_tn(a, b):
    return lax.dot_general(a.astype(MXU_DTYPE), b.astype(MXU_DTYPE), (((0,), (0,)), ((), ())),
                           preferred_element_type=F32)


def _sigmoid(v):
    return 0.5 * jnp.tanh(0.5 * v) + 0.5


def _gelu(v):
    v2 = v * v
    t = jnp.tanh(v * (GELU_C + (GELU_C * GELU_K) * v2))
    hv = 0.5 * v
    dt = (1.0 - t * t) * (GELU_C + (3.0 * GELU_C * GELU_K) * v2)
    return hv + hv * t, (0.5 + 0.5 * t) + hv * dt


def _softplus_neg(lam):
    e = jnp.exp(-jnp.abs(lam))
    u = 1.0 + e
    log1p_e = jnp.where(u == 1.0, e, jnp.log(u) * e / jnp.where(u == 1.0, 1.0, u - 1.0))
    return jnp.maximum(-lam, 0.0) + log1p_e


def _rms(xv):
    rinv = lax.rsqrt(jnp.mean(xv * xv, axis=-1, keepdims=True) + EPS)
    return rinv, xv * rinv


def _rms_bwd(rinv, xhat, dxhat):
    return rinv * (dxhat - xhat * jnp.mean(dxhat * xhat, axis=-1, keepdims=True))


def _colsum(v):
    return jnp.sum(v, axis=0, keepdims=True)


def _wrow(w_ref, k, wcols):
    return w_ref[pl.ds(k, 1), :] if wcols is None else w_ref[pl.ds(k, 1), wcols]


def _windows(buf_ref, halo, taps, rows):
    assert taps <= 8 <= halo
    x = buf_ref[pl.ds(halo - 8, rows + 8), :]
    return [x[8:] if s == 0 else pltpu.roll(x, s, 0)[8:] for s in range(taps)]


def _causal_from(xs, w_ref, wcols=None):
    taps = len(xs)
    acc = None
    for s in range(taps):
        term = _wrow(w_ref, taps - 1 - s, wcols) * xs[s]
        acc = term if acc is None else acc + term
    return acc


def _tap_grads_from(dw_ref, dy, xs, wcols=None):
    taps = len(xs)
    for s in range(taps):
        g = _colsum(dy * xs[s])
        if wcols is None:
            dw_ref[pl.ds(taps - 1 - s, 1), :] += g
        else:
            dw_ref[pl.ds(taps - 1 - s, 1), wcols] += g


def _causal_taps(buf_ref, halo, w_ref, taps, rows, wcols=None):
    return _causal_from(_windows(buf_ref, halo, taps, rows), w_ref, wcols)


def _anticausal_taps(buf_ref, w_ref, taps, rows, wcols=None):
    assert taps <= 8
    x = buf_ref[pl.ds(0, rows + 8), :]
    acc = None
    for s in range(taps):
        win = x[:rows] if s == 0 else pltpu.roll(x, rows + 8 - s, 0)[:rows]
        term = _wrow(w_ref, taps - 1 - s, wcols) * win
        acc = term if acc is None else acc + term
    return acc


def _shift_copies(dst_ref, buf_ref, rows, up):
    x = buf_ref[pl.ds(0, rows + 8), :]
    for r in range(8):
        if up:
            dst_ref[r] = x[:rows] if r == 0 else pltpu.roll(x, rows + 8 - r, 0)[:rows]
        else:
            dst_ref[r] = x[8:] if r == 0 else pltpu.roll(x, r, 0)[8:]


def _causal_taps8(sh_ref, halo, w_ref, taps, rows):
    acc = None
    for s in range(taps):
        term = _wrow(w_ref, taps - 1 - s, None) * sh_ref[s % 8, pl.ds(halo - 8 - 8 * (s // 8), rows), :]
        acc = term if acc is None else acc + term
    return acc


def _anticausal_taps8(sh_ref, w_ref, taps, rows):
    acc = None
    for s in range(taps):
        term = _wrow(w_ref, taps - 1 - s, None) * sh_ref[s % 8, pl.ds(8 * (s // 8), rows), :]
        acc = term if acc is None else acc + term
    return acc


def _tap_grads8(dw_ref, dy, sh_ref, halo, taps, rows):
    for s in range(taps):
        dw_ref[pl.ds(taps - 1 - s, 1), :] += _colsum(dy * sh_ref[s % 8, pl.ds(halo - 8 - 8 * (s // 8), rows), :])


def _fwd_in(x, g, w_in, comm=None):
    S, D = x.shape
    nb, _, C = w_in.shape
    ts = _tile(S, 1024)

    halves = 2 if ts % 32 == 0 else 1
    hr = ts // halves

    def body(x_ref, g_ref, w_ref, z_ref, h_ref):
        def norm(k):
            rows = pl.ds(k * hr, hr)
            _, xhat = _rms(x_ref[rows, :])
            h = (xhat * g_ref[...]).astype(MXU_DTYPE)
            h_ref[rows, :] = h
            return h

        h_next = norm(0)
        for k in range(halves):
            rows, h = pl.ds(k * hr, hr), h_next
            if k + 1 < halves:
                h_next = norm(k + 1)
            for j in range(nb):
                z_ref[j, rows, :] = jnp.dot(h, w_ref[j], preferred_element_type=F32)

    return _pcall(
        comm, body, name="fwd_in", grid=(S // ts,),
        in_specs=[pl.BlockSpec((ts, D), lambda i: (i, 0)), pl.BlockSpec((1, D), lambda i: (0, 0)),
                  pl.BlockSpec((nb, D, C), lambda i: (0, 0, 0))],
        out_specs=[pl.BlockSpec((nb, ts, C), lambda i: (0, i, 0)), pl.BlockSpec((ts, D), lambda i: (i, 0))],
        out_shape=[jax.ShapeDtypeStruct((nb, S, C), F32), jax.ShapeDtypeStruct((S, D), MXU_DTYPE)],
        semantics=("parallel",))(x, g, w_in)


def _lru_gates(xc, wa_ref, ba_ref, wx_ref, bx_ref, sp):
    xb = xc.astype(MXU_DTYPE)
    r = _sigmoid(jnp.dot(xb, wa_ref[...], preferred_element_type=F32) + ba_ref[...])
    ig = _sigmoid(jnp.dot(xb, wx_ref[...], preferred_element_type=F32) + bx_ref[...])
    log_a = -RG_C * r * sp
    a = jnp.exp(log_a)
    one_minus_a2 = jnp.tanh(-log_a) * (a * a + 1.0)
    inv_mult = lax.rsqrt(one_minus_a2)
    mult = jnp.where(one_minus_a2 > 0.0, one_minus_a2 * inv_mult, 0.0)
    return r, ig, a, mult, inv_mult


def _lru_fwd(z, conv_w, conv_b, wa, ba, wx, bx, lam, comm=None):
    _, S, C = z.shape
    ts = _tile(S, 512)
    taps = conv_w.shape[0]
    halo = 8

    def body(zx_ref, zg_ref, cw_ref, cb_ref, wa_ref, ba_ref, wx_ref, bx_ref, lam_ref,
             h_ref, y_ref, xbuf, a_s, u_s, hc):
        i = pl.program_id(0)

        @pl.when(i == 0)
        def _():
            xbuf[pl.ds(0, halo), :] = jnp.zeros((halo, C), F32)
            hc[...] = jnp.zeros_like(hc)

        xbuf[pl.ds(halo, ts), :] = zx_ref[0]
        xc = _causal_taps(xbuf, halo, cw_ref, taps, ts) + cb_ref[...]
        sp = _softplus_neg(lam_ref[...])
        _, ig, a, mult, _ = _lru_gates(xc, wa_ref, ba_ref, wx_ref, bx_ref, sp)
        a_s[...] = a
        u_s[...] = mult * (ig * xc)
        row = lax.broadcasted_iota(jnp.int32, (8, C), 0)

        def step(k, carry):
            off = pl.multiple_of(k * 8, 8)
            av = a_s[pl.ds(off, 8), :]
            uv = u_s[pl.ds(off, 8), :]
            for d in (1, 2, 4):
                m = row >= d
                a_sh = jnp.where(m, pltpu.roll(av, d, 0), 1.0)
                u_sh = jnp.where(m, pltpu.roll(uv, d, 0), 0.0)
                uv = uv + av * u_sh
                av = av * a_sh
            hv = uv + av * carry
            h_ref[pl.ds(off, 8), :] = hv
            return jnp.broadcast_to(hv[7:8, :], (8, C))

        hc[...] = lax.fori_loop(0, ts // 8, step, hc[...], unroll=4)
        ge, _ = _gelu(zg_ref[0])
        y_ref[...] = (h_ref[...] * ge).astype(MXU_DTYPE)
        xbuf[pl.ds(0, halo), :] = xbuf[pl.ds(ts, halo), :]

    vec = pl.BlockSpec((1, C), lambda i: (0, 0))
    mat = pl.BlockSpec((C, C), lambda i: (0, 0))
    return _pcall(
        comm, body, name="lru_fwd", grid=(S // ts,),
        in_specs=[pl.BlockSpec((1, ts, C), lambda i: (0, i, 0)), pl.BlockSpec((1, ts, C), lambda i: (1, i, 0)),
                  pl.BlockSpec((taps, C), lambda i: (0, 0)), vec, mat, vec, mat, vec, vec],
        out_specs=[pl.BlockSpec((ts, C), lambda i: (i, 0)), pl.BlockSpec((ts, C), lambda i: (i, 0))],
        out_shape=[jax.ShapeDtypeStruct((S, C), F32), jax.ShapeDtypeStruct((S, C), MXU_DTYPE)],
        scratch_shapes=[pltpu.VMEM((ts + halo, C), F32), pltpu.VMEM((ts, C), F32), pltpu.VMEM((ts, C), F32),
                        pltpu.VMEM((8, C), F32)],
        semantics=("arbitrary",))(z, z, conv_w, conv_b, wa, ba, wx, bx, lam)


def _layer_norm_stats(c1):
    mu = jnp.mean(c1, axis=-1, keepdims=True)
    xc = c1 - mu
    rstd = lax.rsqrt(jnp.mean(xc * xc, axis=-1, keepdims=True) + EPS)
    return rstd, xc * rstd


def _conf_fwd(z, conv_w, conv_b, ln_g, ln_b, comm=None):
    _, S, C = z.shape
    ts = _tile(S, 512)
    taps = conv_w.shape[0]
    halo = 32

    def body(za_ref, zb_ref, cw_ref, cb_ref, g_ref, b_ref, c1_ref, c3_ref, cbuf, shifted):
        i = pl.program_id(0)

        @pl.when(i == 0)
        def _():
            cbuf[pl.ds(0, halo), :] = jnp.zeros((halo, C), F32)

        cbuf[pl.ds(halo, ts), :] = za_ref[0] * _sigmoid(zb_ref[0])
        _shift_copies(shifted, cbuf, ts + halo - 8, up=False)
        c1 = _causal_taps8(shifted, halo, cw_ref, taps, ts) + cb_ref[...]
        c1_ref[...] = c1
        _, xhat = _layer_norm_stats(c1)
        c2 = xhat * g_ref[...] + b_ref[...]
        c3_ref[...] = (c2 * _sigmoid(c2)).astype(MXU_DTYPE)
        cbuf[pl.ds(0, halo), :] = cbuf[pl.ds(ts, halo), :]

    vec = pl.BlockSpec((1, C), lambda i: (0, 0))
    return _pcall(
        comm, body, name="conf_fwd", grid=(S // ts,),
        in_specs=[pl.BlockSpec((1, ts, C), lambda i: (2, i, 0)), pl.BlockSpec((1, ts, C), lambda i: (3, i, 0)),
                  pl.BlockSpec((taps, C), lambda i: (0, 0)), vec, vec, vec],
        out_specs=[pl.BlockSpec((ts, C), lambda i: (i, 0)), pl.BlockSpec((ts, C), lambda i: (i, 0))],
        out_shape=[jax.ShapeDtypeStruct((S, C), F32), jax.ShapeDtypeStruct((S, C), MXU_DTYPE)],
        scratch_shapes=[pltpu.VMEM((ts + halo, C), F32), pltpu.VMEM((8, ts + halo - 8, C), F32)],
        semantics=("arbitrary",))(z, z, conv_w, conv_b, ln_g, ln_b)


def _fwd_out_q(x, y_lru, c3, w_out, g_xa, w_q, comm=None):
    S, D = x.shape
    C = y_lru.shape[1]
    ts = _tile(S, 1024)

    halves = 2 if ts % 32 == 0 else 1
    hr = ts // halves

    def body(x_ref, yl_ref, c3_ref, wo_ref, g_ref, wq_ref, x1_ref, h2_ref, q_ref):
        def mixed(k):
            rows = pl.ds(k * hr, hr)
            return (jnp.dot(yl_ref[rows, :], wo_ref[0], preferred_element_type=F32)
                    + jnp.dot(c3_ref[rows, :], wo_ref[1], preferred_element_type=F32))

        y_next = mixed(0)
        for k in range(halves):
            rows, y = pl.ds(k * hr, hr), y_next
            if k + 1 < halves:
                y_next = mixed(k + 1)
            x1 = x_ref[rows, :] + y
            x1_ref[rows, :] = x1
            _, xhat = _rms(x1)
            h2 = (xhat * g_ref[...]).astype(MXU_DTYPE)
            h2_ref[rows, :] = h2
            q_ref[rows, :] = jnp.dot(h2, wq_ref[...], preferred_element_type=F32).astype(MXU_DTYPE)

    row = lambda w: pl.BlockSpec((ts, w), lambda i: (i, 0))
    return _pcall(
        comm, body, name="fwd_out_q", grid=(S // ts,),
        in_specs=[row(D), row(C), row(C), pl.BlockSpec((2, C, D), lambda i: (0, 0, 0)),
                  pl.BlockSpec((1, D), lambda i: (0, 0)), pl.BlockSpec((D, D), lambda i: (0, 0))],
        out_specs=[row(D), row(D), row(D)],
        out_shape=[jax.ShapeDtypeStruct((S, D), F32), jax.ShapeDtypeStruct((S, D), MXU_DTYPE),
                   jax.ShapeDtypeStruct((S, D), MXU_DTYPE)],
        semantics=("parallel",))(x, y_lru, c3, w_out, g_xa, w_q)


def _kv_fwd(mem, g, w_kv):
    M, D = mem.shape
    nb, _, C = w_kv.shape

    def body(mem_ref, g_ref, w_ref, m_ref, kv_ref):
        _, xhat = _rms(mem_ref[...])
        m = (xhat * g_ref[...]).astype(MXU_DTYPE)
        m_ref[...] = m
        for j in range(nb):
            kv_ref[:, pl.ds(j * C, C)] = jnp.dot(m, w_ref[j], preferred_element_type=F32).astype(MXU_DTYPE)

    return pl.pallas_call(
        body, name="kv_fwd", grid=(1,),
        in_specs=[pl.BlockSpec((M, D), lambda i: (0, 0)), pl.BlockSpec((1, D), lambda i: (0, 0)),
                  pl.BlockSpec((nb, D, C), lambda i: (0, 0, 0))],
        out_specs=[pl.BlockSpec((M, D), lambda i: (0, 0)), pl.BlockSpec((M, nb * C), lambda i: (0, 0))],
        out_shape=[jax.ShapeDtypeStruct((M, D), MXU_DTYPE), jax.ShapeDtypeStruct((M, nb * C), MXU_DTYPE)],
        compiler_params=_params("arbitrary"))(mem, g, w_kv)


def _softmax_rows(s):
    e = jnp.exp(s - jnp.max(s, axis=-1, keepdims=True))
    return e / jnp.sum(e, axis=-1, keepdims=True)


def _attn_fwd(q, kv, x1, w_o, g_ffn, comm=None):
    S, D = x1.shape
    M = kv.shape[0]
    hd = D // XA_HEADS
    scale = hd ** -0.5
    ts = _tile(S, 1024)

    def body(q_ref, kv_ref, x1_ref, wo_ref, g_ref, o_ref, x2_ref, h3_ref):
        def scores(h):
            cols = pl.ds(h * hd, hd)
            return _mm_nt(q_ref[:, cols], kv_ref[:, cols]) * scale

        s_next = scores(0)
        for h in range(XA_HEADS):
            s = s_next
            if h + 1 < XA_HEADS:
                s_next = scores(h + 1)
            p = _softmax_rows(s)
            o_ref[:, pl.ds(h * hd, hd)] = _mm(p, kv_ref[:, pl.ds(D + h * hd, hd)]).astype(MXU_DTYPE)
        x2 = x1_ref[...] + jnp.dot(o_ref[...], wo_ref[...], preferred_element_type=F32)
        x2_ref[...] = x2
        _, xhat = _rms(x2)
        h3_ref[...] = (xhat * g_ref[...]).astype(MXU_DTYPE)

    row = pl.BlockSpec((ts, D), lambda i: (i, 0))
    return _pcall(
        comm, body, name="attn_fwd", grid=(S // ts,),
        in_specs=[row, pl.BlockSpec((M, 2 * D), lambda i: (0, 0)), row, pl.BlockSpec((D, D), lambda i: (0, 0)),
                  pl.BlockSpec((1, D), lambda i: (0, 0))],
        out_specs=[row, row, row],
        out_shape=[jax.ShapeDtypeStruct((S, D), MXU_DTYPE), jax.ShapeDtypeStruct((S, D), F32),
                   jax.ShapeDtypeStruct((S, D), MXU_DTYPE)],
        semantics=("parallel",))(q, kv, x1, w_o, g_ffn)


def _ffn_fwd(h3, w_up, conv_w, conv_b, w_down, x2, g_final, target, comm=None):
    S, D = h3.shape
    nb, _, CW = w_up.shape
    half = nb // 2
    cb = 768
    per = CW // cb
    J = half * per
    ts = _tile(S, 256)
    taps = conv_w.shape[0]
    halo = 8

    def body(h_ref, wup_ref, cw_ref, cb_ref, wd_ref, x2_ref, gf_ref, t_ref,
             gu_ref, act_ref, dx3_ref, loss_ref, dgf_ref, gbuf):
        i = pl.program_id(0)

        @pl.when(i == 0)
        def _():
            for ref in (loss_ref, dgf_ref, gbuf):
                ref[...] = jnp.zeros_like(ref)

        hv = h_ref[...]
        x3 = x2_ref[...]
        def up(j):
            b, cols = j // per, pl.ds((j % per) * cb, cb)
            return (jnp.dot(hv, wup_ref[b, :, cols], preferred_element_type=F32),
                    jnp.dot(hv, wup_ref[half + b, :, cols], preferred_element_type=F32))

        ahead = up(0)
        for j in range(J):
            b, cols, wcols = j // per, pl.ds((j % per) * cb, cb), pl.ds(j * cb, cb)
            g, u = ahead
            if j + 1 < J:
                ahead = up(j + 1)
            gu_ref[0, b, :, cols] = g
            gu_ref[1, b, :, cols] = u
            gbuf[j, pl.ds(halo, ts), :] = g
            gc = _causal_taps(gbuf.at[j], halo, cw_ref, taps, ts, wcols=wcols) + cb_ref[:, wcols]
            gbuf[j, pl.ds(0, halo), :] = gbuf[j, pl.ds(ts, halo), :]
            ge, _ = _gelu(gc)
            act = (ge * u).astype(MXU_DTYPE)
            act_ref[j] = act
            x3 = x3 + jnp.dot(act, wd_ref[j], preferred_element_type=F32)
        rinv, xhat = _rms(x3)
        gf = gf_ref[...]
        diff = xhat * gf - t_ref[...]
        loss_ref[...] += _colsum(diff * diff) * (0.5 / D)
        dy = diff * (1.0 / D)
        dgf_ref[...] += _colsum(dy * xhat)
        dx3_ref[...] = _rms_bwd(rinv, xhat, dy * gf)

    row = pl.BlockSpec((ts, D), lambda i: (i, 0))
    vecd = pl.BlockSpec((1, D), lambda i: (0, 0))
    once = pl.Buffered(1)
    sds = jax.ShapeDtypeStruct
    res = _pcall(
        comm, body, name="ffn_fwd", grid=(S // ts,),
        in_specs=[row, pl.BlockSpec((nb, D, CW), lambda i: (0, 0, 0), pipeline_mode=once),
                  pl.BlockSpec((taps, half * CW), lambda i: (0, 0)), pl.BlockSpec((1, half * CW), lambda i: (0, 0)),
                  pl.BlockSpec((J, cb, D), lambda i: (0, 0, 0), pipeline_mode=once), row, vecd, row],
        out_specs=[pl.BlockSpec((2, half, ts, CW), lambda i: (0, 0, i, 0)),
                   pl.BlockSpec((J, ts, cb), lambda i: (0, i, 0)), row, vecd, vecd],
        out_shape=[sds((2, half, S, CW), F32), sds((J, S, cb), MXU_DTYPE), sds((S, D), F32),
                   sds((1, D), F32), sds((1, D), F32)],
        scratch_shapes=[pltpu.VMEM((J, ts + halo, cb), F32)],
        semantics=("arbitrary",))(h3, w_up, conv_w, conv_b, w_down.reshape(J, cb, D), x2, g_final, target)
    outs = res if comm is None else res[0]
    outs = [outs[0].reshape(nb, S, CW)] + list(outs[1:])
    return outs if comm is None else (outs, res[1])


def _ffn_bwd(dx3, w_down, w_up, gu, x2, g_ffn, conv_w, conv_b, comm=None):
    nb, S, CW = gu.shape
    half = nb // 2
    D = dx3.shape[1]
    cb = 768
    per = CW // cb
    J = half * per
    ts = _tile(S, 256)
    n = S // ts
    taps = conv_w.shape[0]
    halo = 8
    hb = ts // halo

    def body(dx_ref, x2_ref, gf_ref, wd_ref, wup_ref, gu_ref, gh_ref, cw_ref, cb_ref,
             dgu_ref, dx2_ref, dgf_ref, dcw_ref, dcb_ref, gbuf, dbuf):
        i = pl.program_id(0)
        r = n - 1 - i

        @pl.when(i == 0)
        def _():
            for ref in (dgf_ref, dcw_ref, dcb_ref, dbuf):
                ref[...] = jnp.zeros_like(ref)

        dx3v = dx_ref[...]
        dxb = dx3v.astype(MXU_DTYPE)
        dacts = [_mm_nt(dxb, wd_ref[j]) for j in range(J)]
        dh = None
        for j in range(J):
            b, cols, wcols = j // per, pl.ds((j % per) * cb, cb), pl.ds(j * cb, cb)
            dact = dacts[j]
            gbuf[pl.ds(0, halo), :] = jnp.where(r > 0, gh_ref[0, b, :, cols], 0.0)
            gbuf[pl.ds(halo, ts), :] = gu_ref[0, b, :, cols]
            gs = _windows(gbuf, halo, taps, ts)
            gc = _causal_from(gs, cw_ref, wcols) + cb_ref[:, wcols]
            ge, dge = _gelu(gc)
            dub = (dact * ge).astype(MXU_DTYPE)
            dgc = dact * gu_ref[1, b, :, cols] * dge
            dcb_ref[:, wcols] += _colsum(dgc)
            dbuf[j, pl.ds(0, ts), :] = dgc
            _tap_grads_from(dcw_ref, dgc, gs, wcols)
            dgb = _anticausal_taps(dbuf.at[j], cw_ref, taps, ts, wcols=wcols).astype(MXU_DTYPE)
            dbuf[j, pl.ds(ts, halo), :] = dbuf[j, pl.ds(0, halo), :]
            dgu_ref[0, b, :, cols] = dgb
            dgu_ref[1, b, :, cols] = dub
            part = _mm_nt(dgb, wup_ref[b, :, cols]) + _mm_nt(dub, wup_ref[half + b, :, cols])
            dh = part if dh is None else dh + part
        rinv, xhat = _rms(x2_ref[...])
        dgf_ref[...] += _colsum(dh * xhat)
        dx2_ref[...] = dx3v + _rms_bwd(rinv, xhat, dh * gf_ref[...])

    gu2 = gu.reshape(2, half, S, CW)
    row = pl.BlockSpec((ts, D), lambda i: (n - 1 - i, 0))
    vecd = pl.BlockSpec((1, D), lambda i: (0, 0))
    pair = pl.BlockSpec((2, half, ts, CW), lambda i: (0, 0, n - 1 - i, 0))
    g_prev = pl.BlockSpec((1, half, halo, CW), lambda i: (0, 0, jnp.maximum((n - 1 - i) * hb - 1, 0), 0))
    tapw = pl.BlockSpec((taps, half * CW), lambda i: (0, 0))
    vec = pl.BlockSpec((1, half * CW), lambda i: (0, 0))
    once = pl.Buffered(1)
    sds = jax.ShapeDtypeStruct
    res = _pcall(
        comm, body, name="ffn_bwd", grid=(n,),
        in_specs=[row, row, vecd, pl.BlockSpec((J, cb, D), lambda i: (0, 0, 0), pipeline_mode=once),
                  pl.BlockSpec((nb, D, CW), lambda i: (0, 0, 0), pipeline_mode=once), pair, g_prev, tapw, vec],
        out_specs=[pair, row, vecd, tapw, vec],
        out_shape=[sds((2, half, S, CW), MXU_DTYPE), sds((S, D), F32), sds((1, D), F32),
                   sds((taps, half * CW), F32), sds((1, half * CW), F32)],
        scratch_shapes=[pltpu.VMEM((ts + halo, cb), F32), pltpu.VMEM((J, ts + halo, cb), F32)],
        semantics=("arbitrary",))(dx3, x2, g_ffn, w_down.reshape(J, cb, D), w_up, gu2, gu2, conv_w, conv_b)
    outs = res if comm is None else res[0]
    outs = [outs[0].reshape(nb, S, CW)] + list(outs[1:])
    return outs if comm is None else (outs, res[1])


def _attn_bwd(dx2, w_o, q, kv, x1, g_xa, w_q, comm=None):
    S, D = x1.shape
    M = kv.shape[0]
    hd = D // XA_HEADS
    scale = hd ** -0.5
    ts = _tile(S, 1024)

    def body(dx2_ref, wo_ref, q_ref, kv_ref, x1_ref, g_ref, wq_ref, dq_ref, dx1_ref, dkv_ref, dg_ref):
        i = pl.program_id(0)

        @pl.when(i == 0)
        def _():
            dkv_ref[...] = jnp.zeros_like(dkv_ref)
            dg_ref[...] = jnp.zeros_like(dg_ref)

        dx2 = dx2_ref[...]
        do = _mm_nt(dx2, wo_ref[...]).astype(MXU_DTYPE)
        def scores(h):
            cols = pl.ds(h * hd, hd)
            doh = do[:, h * hd:(h + 1) * hd]
            return (_mm_nt(q_ref[:, cols], kv_ref[:, cols]) * scale,
                    _mm_nt(doh, kv_ref[:, pl.ds(D + h * hd, hd)]), doh)

        ahead = scores(0)
        for h in range(XA_HEADS):
            cols = pl.ds(h * hd, hd)
            vcols = pl.ds(D + h * hd, hd)
            s, dp, doh = ahead
            if h + 1 < XA_HEADS:
                ahead = scores(h + 1)
            p = _softmax_rows(s)
            ds = (p * (dp - jnp.sum(dp * p, axis=-1, keepdims=True)) * scale).astype(MXU_DTYPE)
            dkv_ref[:, vcols] += _mm_tn(p, doh)
            dq_ref[:, cols] = _mm(ds, kv_ref[:, cols]).astype(MXU_DTYPE)
            dkv_ref[:, cols] += _mm_tn(ds, q_ref[:, cols])
        dh2 = _mm_nt(dq_ref[...], wq_ref[...])
        rinv, xhat = _rms(x1_ref[...])
        dg_ref[...] += _colsum(dh2 * xhat)
        dx1_ref[...] = dx2 + _rms_bwd(rinv, xhat, dh2 * g_ref[...])

    row = pl.BlockSpec((ts, D), lambda i: (i, 0))
    mat = pl.BlockSpec((D, D), lambda i: (0, 0))
    vecd = pl.BlockSpec((1, D), lambda i: (0, 0))
    kvs = pl.BlockSpec((M, 2 * D), lambda i: (0, 0))
    return _pcall(
        comm, body, name="attn_bwd", grid=(S // ts,),
        in_specs=[row, mat, row, kvs, row, vecd, mat],
        out_specs=[row, row, kvs, vecd],
        out_shape=[jax.ShapeDtypeStruct((S, D), MXU_DTYPE), jax.ShapeDtypeStruct((S, D), F32),
                   jax.ShapeDtypeStruct((M, 2 * D), F32), jax.ShapeDtypeStruct((1, D), F32)],
        semantics=("arbitrary",))(dx2, w_o, q, kv, x1, g_xa, w_q)


def _kv_bwd(dkv, w_kv, mem, g, m):
    M, D = mem.shape
    nb, _, C = w_kv.shape

    def body(dkv_ref, w_ref, mem_ref, m_ref, dw_ref, dg_ref):
        dm = jnp.zeros((M, D), F32)
        for j in range(nb):
            dj = dkv_ref[:, pl.ds(j * C, C)].astype(MXU_DTYPE)
            dw_ref[j] = _mm_tn(m_ref[...], dj).astype(dw_ref.dtype)
            dm = dm + _mm_nt(dj, w_ref[j])
        _, xhat = _rms(mem_ref[...])
        dg_ref[...] = _colsum(dm * xhat)

    full = lambda *s: pl.BlockSpec(s, lambda i: (0,) * len(s))
    return pl.pallas_call(
        body, name="kv_bwd", grid=(1,),
        in_specs=[full(M, nb * C), full(nb, D, C), full(M, D), full(M, D)],
        out_specs=[full(nb, D, C), full(1, D)],
        out_shape=[jax.ShapeDtypeStruct((nb, D, C), WIRE_DTYPE), jax.ShapeDtypeStruct((1, D), F32)],
        compiler_params=_params("arbitrary"))(dkv, w_kv, mem, m)


def _conf_bwd(dx1, w_out_c, z, c1, conv_w, ln_g, ln_b, comm=None):
    _, S, C = z.shape
    D = dx1.shape[1]
    ts = _tile(S, 512)
    n = S // ts
    taps = conv_w.shape[0]
    halo = 32
    hb = ts // halo

    def body(dx_ref, wo_ref, za_ref, zb_ref, zah_ref, zbh_ref, c1_ref, cw_ref, g_ref, b_ref,
             dz_ref, dcw_ref, dcb_ref, dlg_ref, dlb_ref, c0buf, dbuf, shifted):
        i = pl.program_id(0)
        r = n - 1 - i

        @pl.when(i == 0)
        def _():
            for ref in (dcw_ref, dcb_ref, dlg_ref, dlb_ref):
                ref[...] = jnp.zeros_like(ref)
            dbuf[pl.ds(ts, halo), :] = jnp.zeros((halo, C), F32)

        za = za_ref[0]
        sb = _sigmoid(zb_ref[0])
        c0buf[pl.ds(0, halo), :] = jnp.where(r > 0, zah_ref[0] * _sigmoid(zbh_ref[0]), 0.0)
        c0buf[pl.ds(halo, ts), :] = za * sb
        dc3 = _mm_nt(dx_ref[...], wo_ref[...])
        rstd, xhat = _layer_norm_stats(c1_ref[...])
        g = g_ref[...]
        c2 = xhat * g + b_ref[...]
        sg = _sigmoid(c2)
        dc2 = dc3 * sg * (1.0 + c2 * (1.0 - sg))
        dlg_ref[...] += _colsum(dc2 * xhat)
        dlb_ref[...] += _colsum(dc2)
        dxh = dc2 * g
        dc1 = rstd * (dxh - jnp.mean(dxh, axis=-1, keepdims=True)
                      - xhat * jnp.mean(dxh * xhat, axis=-1, keepdims=True))
        dcb_ref[...] += _colsum(dc1)
        dbuf[pl.ds(0, ts), :] = dc1
        _shift_copies(shifted, c0buf, ts + halo - 8, up=False)
        _tap_grads8(dcw_ref, dc1, shifted, halo, taps, ts)
        _shift_copies(shifted, dbuf, ts + halo - 8, up=True)
        dc0 = _anticausal_taps8(shifted, cw_ref, taps, ts)
        dz_ref[0] = (dc0 * sb).astype(MXU_DTYPE)
        dz_ref[1] = (dc0 * za * sb * (1.0 - sb)).astype(MXU_DTYPE)
        dbuf[pl.ds(ts, halo), :] = dbuf[pl.ds(0, halo), :]

    vec = pl.BlockSpec((1, C), lambda i: (0, 0))
    tapw = pl.BlockSpec((taps, C), lambda i: (0, 0))
    tile = lambda b: pl.BlockSpec((1, ts, C), lambda i: (b, n - 1 - i, 0))
    prev = lambda b: pl.BlockSpec((1, halo, C), lambda i: (b, jnp.maximum((n - 1 - i) * hb - 1, 0), 0))
    return _pcall(
        comm, body, name="conf_bwd", grid=(n,),
        in_specs=[pl.BlockSpec((ts, D), lambda i: (n - 1 - i, 0)), pl.BlockSpec((C, D), lambda i: (0, 0)),
                  tile(2), tile(3), prev(2), prev(3), pl.BlockSpec((ts, C), lambda i: (n - 1 - i, 0)),
                  tapw, vec, vec],
        out_specs=[pl.BlockSpec((2, ts, C), lambda i: (1, n - 1 - i, 0)), tapw, vec, vec, vec],
        out_shape=[jax.ShapeDtypeStruct((4, S, C), MXU_DTYPE), jax.ShapeDtypeStruct((taps, C), F32),
                   jax.ShapeDtypeStruct((1, C), F32), jax.ShapeDtypeStruct((1, C), F32),
                   jax.ShapeDtypeStruct((1, C), F32)],
        scratch_shapes=[pltpu.VMEM((ts + halo, C), F32), pltpu.VMEM((ts + halo, C), F32),
                        pltpu.VMEM((8, ts + halo - 8, C), F32)],
        semantics=("arbitrary",))(dx1, w_out_c, z, z, z, z, c1, conv_w, ln_g, ln_b)


def _lru_bwd(dx1, w_out_l, z, h, conv_w, conv_b, wa, ba, wx, bx, lam, dz, comm=None):
    _, S, C = z.shape
    D = dx1.shape[1]
    ts = _tile(S, 512)
    n = S // ts
    taps = conv_w.shape[0]
    halo = 8
    hb = ts // halo

    def body(dx_ref, wo_ref, zx_ref, zxh_ref, zg_ref, h_ref, hh_ref, cw_ref, cb_ref, wa_ref, ba_ref,
             wx_ref, bx_ref, lam_ref, dz_in,
             dz_ref, dwa_ref, dwx_ref, dba_ref, dbx_ref, dlam_ref, dcw_ref, dcb_ref,
             xbuf, hbuf, a_s, w_s, dh_s, g_s, dbuf, pc):
        i = pl.program_id(0)
        r = n - 1 - i

        @pl.when(i == 0)
        def _():
            for ref in (dwa_ref, dwx_ref, dba_ref, dbx_ref, dlam_ref, dcw_ref, dcb_ref, pc):
                ref[...] = jnp.zeros_like(ref)
            dbuf[pl.ds(ts, halo), :] = jnp.zeros((halo, C), F32)

        xbuf[pl.ds(0, halo), :] = jnp.where(r > 0, zxh_ref[0], 0.0)
        xbuf[pl.ds(halo, ts), :] = zx_ref[0]
        hbuf[pl.ds(0, halo), :] = jnp.where(r > 0, hh_ref[...], 0.0)
        hbuf[pl.ds(halo, ts), :] = h_ref[...]
        xs = _windows(xbuf, halo, taps, ts)
        xc = _causal_from(xs, cw_ref) + cb_ref[...]
        lam_v = lam_ref[...]
        sp = _softplus_neg(lam_v)
        rg, ig, a, mult, inv_mult = _lru_gates(xc, wa_ref, ba_ref, wx_ref, bx_ref, sp)

        dy = _mm_nt(dx_ref[...], wo_ref[...])
        ge, dge = _gelu(zg_ref[0])
        dh = dy * ge
        dz_ref[1] = (dy * h_ref[...] * dge).astype(MXU_DTYPE)
        a_s[...] = a
        w_s[...] = a * dh
        dh_s[...] = dh
        row = lax.broadcasted_iota(jnp.int32, (8, C), 0)

        def step(kk, carry):
            off = pl.multiple_of((ts // 8 - 1 - kk) * 8, 8)
            av = a_s[pl.ds(off, 8), :]
            wv = w_s[pl.ds(off, 8), :]
            for d in (1, 2, 4):
                m = row < 8 - d
                a_sh = jnp.where(m, pltpu.roll(av, 8 - d, 0), 1.0)
                w_sh = jnp.where(m, pltpu.roll(wv, 8 - d, 0), 0.0)
                wv = wv + av * w_sh
                av = av * a_sh
            pv = wv + av * carry
            g_s[pl.ds(off, 8), :] = dh_s[pl.ds(off, 8), :] + jnp.where(row < 7, pltpu.roll(pv, 7, 0), carry)
            return jnp.broadcast_to(pv[0:1, :], (8, C))

        pc[...] = lax.fori_loop(0, ts // 8, step, pc[...], unroll=4)
        gt = g_s[...]
        da = gt * hbuf[pl.ds(halo - 1, ts), :]
        gm = gt * mult
        dlog_a = da * a - (gt * ig * xc) * (a * a) * inv_mult
        dlam_ref[...] += _colsum(dlog_a * rg) * (RG_C / (1.0 + jnp.exp(lam_v)))
        dpa = (dlog_a * (-RG_C * sp)) * rg * (1.0 - rg)
        dpx = (gm * xc) * ig * (1.0 - ig)
        dba_ref[...] += _colsum(dpa)
        dbx_ref[...] += _colsum(dpx)
        xb = xc.astype(MXU_DTYPE)
        dpab, dpxb = dpa.astype(MXU_DTYPE), dpx.astype(MXU_DTYPE)
        dwa_ref[...] += _mm_tn(xb, dpab)
        dwx_ref[...] += _mm_tn(xb, dpxb)
        dxc = gm * ig + _mm_nt(dpab, wa_ref[...]) + _mm_nt(dpxb, wx_ref[...])
        dcb_ref[...] += _colsum(dxc)
        dbuf[pl.ds(0, ts), :] = dxc
        _tap_grads_from(dcw_ref, dxc, xs)
        dz_ref[0] = _anticausal_taps(dbuf, cw_ref, taps, ts).astype(MXU_DTYPE)
        dbuf[pl.ds(ts, halo), :] = dbuf[pl.ds(0, halo), :]

    vec = pl.BlockSpec((1, C), lambda i: (0, 0))
    mat = pl.BlockSpec((C, C), lambda i: (0, 0))
    tapw = pl.BlockSpec((taps, C), lambda i: (0, 0))
    prev_rows = lambda i: jnp.maximum((n - 1 - i) * hb - 1, 0)
    sds = jax.ShapeDtypeStruct
    return _pcall(
        comm, body, name="lru_bwd", grid=(n,),
        in_specs=[pl.BlockSpec((ts, D), lambda i: (n - 1 - i, 0)), pl.BlockSpec((C, D), lambda i: (0, 0)),
                  pl.BlockSpec((1, ts, C), lambda i: (0, n - 1 - i, 0)),
                  pl.BlockSpec((1, halo, C), lambda i: (0, prev_rows(i), 0)),
                  pl.BlockSpec((1, ts, C), lambda i: (1, n - 1 - i, 0)),
                  pl.BlockSpec((ts, C), lambda i: (n - 1 - i, 0)),
                  pl.BlockSpec((halo, C), lambda i: (prev_rows(i), 0)),
                  tapw, vec, mat, vec, mat, vec, vec, ANY],
        out_specs=[pl.BlockSpec((2, ts, C), lambda i: (0, n - 1 - i, 0)), mat, mat, vec, vec, vec, tapw, vec],
        out_shape=[sds(dz.shape, MXU_DTYPE), sds((C, C), F32), sds((C, C), F32), sds((1, C), F32),
                   sds((1, C), F32), sds((1, C), F32), sds((taps, C), F32), sds((1, C), F32)],
        scratch_shapes=[pltpu.VMEM((ts + halo, C), F32), pltpu.VMEM((ts + halo, C), F32)]
        + [pltpu.VMEM((ts, C), F32)] * 4 + [pltpu.VMEM((ts + halo, C), F32), pltpu.VMEM((8, C), F32)],
        aliases={14: 0},
        semantics=("arbitrary",))(dx1, w_out_l, z, z, z, h, h, conv_w, conv_b, wa, ba, wx, bx, lam, dz)


def _bwd_in(dz, w_in, x, g, dx1):
    S, D = x.shape
    nb, _, C = w_in.shape
    ts = _tile(S, 512)
    halves = 2 if ts % 32 == 0 else 1
    hr = ts // halves

    def body(dz_ref, w_ref, x_ref, g_ref, dx1_ref, dx_ref, dg_ref):
        i = pl.program_id(0)

        @pl.when(i == 0)
        def _():
            dg_ref[...] = jnp.zeros_like(dg_ref)

        def grad_h(k):
            rows = pl.ds(k * hr, hr)
            dh = _mm_nt(dz_ref[0, rows, :], w_ref[0])
            for j in range(1, nb):
                dh = dh + _mm_nt(dz_ref[j, rows, :], w_ref[j])
            return dh

        ahead = grad_h(0)
        for k in range(halves):
            rows, dh = pl.ds(k * hr, hr), ahead
            if k + 1 < halves:
                ahead = grad_h(k + 1)
            rinv, xhat = _rms(x_ref[rows, :])
            dg_ref[...] += _colsum(dh * xhat)
            dx_ref[rows, :] = dx1_ref[rows, :] + _rms_bwd(rinv, xhat, dh * g_ref[...])

    row = pl.BlockSpec((ts, D), lambda i: (i, 0))
    vecd = pl.BlockSpec((1, D), lambda i: (0, 0))
    return pl.pallas_call(
        body, name="bwd_in", grid=(S // ts,),
        in_specs=[pl.BlockSpec((nb, ts, C), lambda i: (0, i, 0)), pl.BlockSpec((nb, D, C), lambda i: (0, 0, 0)),
                  row, vecd, row],
        out_specs=[row, vecd],
        out_shape=[jax.ShapeDtypeStruct((S, D), F32), jax.ShapeDtypeStruct((1, D), F32)],
        compiler_params=_params("arbitrary"))(dz, w_in, x, g, dx1)


def _wgrad(a, b, name, comm=None):
    na, S, K = a.shape
    nb, _, N = b.shape
    nj = max(na, nb)
    assert min(na, nb) == 1
    ts = _tile(S, 1024)
    ns = S // ts
    grp = max(g for g in range(1, nj + 1) if nj % g == 0 and g * K * N * 4 <= WGRAD_ACC_BYTES)
    ga, gb = (grp if na > 1 else 1), (grp if nb > 1 else 1)

    def body(a_ref, b_ref, o_ref, acc):
        s = pl.program_id(1)

        @pl.when(s == 0)
        def _():
            acc[...] = jnp.zeros_like(acc)

        for k in range(grp):
            acc[k] += _mm_tn(a_ref[k if na > 1 else 0], b_ref[k if nb > 1 else 0])

        @pl.when(s == ns - 1)
        def _():
            o_ref[...] = acc[...].astype(o_ref.dtype)

    res = _pcall(
        comm, body, name=name, grid=(nj // grp, ns),
        in_specs=[pl.BlockSpec((ga, ts, K), (lambda j, s: (j, s, 0)) if na > 1 else (lambda j, s: (0, s, 0))),
                  pl.BlockSpec((gb, ts, N), (lambda j, s: (j, s, 0)) if nb > 1 else (lambda j, s: (0, s, 0)))],
        out_specs=pl.BlockSpec((grp, K, N), lambda j, s: (j, 0, 0)),
        out_shape=jax.ShapeDtypeStruct((nj, K, N), WIRE_DTYPE),
        scratch_shapes=[pltpu.VMEM((grp, K, N), F32)],
        semantics=("parallel", "arbitrary"))(a, b)
    return res[0] if comm is None else (res[0][0], res[1])


def _place():
    x, y, c = lax.axis_index("x"), lax.axis_index("y"), lax.axis_index("c")
    other_chips = [(1 - x, y), (x, 1 - y), (1 - x, 1 - y)]
    return x, y, c, other_chips


def _gather_weights(shards):
    nt = len(shards)

    def body(*refs):
        src, dst = refs[:nt], refs[nt:2 * nt]
        ici_send, ici_recv, d2d_send, d2d_recv, own_send, own_recv = refs[2 * nt:]
        x, y, c, chips = _place()
        mine = 2 * x + y

        def half(t, pc):
            hr = src[t].shape[0] // 2
            return pl.ds(pc * hr, hr)

        def own(t):
            return pltpu.make_async_remote_copy(
                src_ref=src[t], dst_ref=dst[t].at[mine], send_sem=own_send.at[t], recv_sem=own_recv.at[t],
                device_id=(x, y, 1 - c), device_id_type=MESH)

        def ici(t, k, block, to):
            cx, cy = block
            ref = dst[t].at[2 * cx + cy, half(t, c)]
            return pltpu.make_async_remote_copy(
                src_ref=src[t].at[half(t, c)] if to is not None else ref, dst_ref=ref,
                send_sem=ici_send.at[t, k], recv_sem=ici_recv.at[t, k],
                device_id=(*to, c) if to is not None else (x, y, c), device_id_type=MESH)

        def d2d(t, k, block, pc):
            cx, cy = block
            ref = dst[t].at[2 * cx + cy, half(t, pc)]
            return pltpu.make_async_remote_copy(
                src_ref=ref, dst_ref=ref, send_sem=d2d_send.at[t, k], recv_sem=d2d_recv.at[t, k],
                device_id=(x, y, 1 - c), device_id_type=MESH)

        sends = [ici(t, k, (x, y), chip) for t in range(nt) for k, chip in enumerate(chips)]
        sends += [own(t) for t in range(nt)]
        for cp in sends:
            cp.start()
        passed = []
        for t in range(nt):
            for k, chip in enumerate(chips):
                ici(t, k, chip, None).wait_recv()
                fw = d2d(t, k, chip, c)
                fw.start()
                passed.append(fw)
        for t in range(nt):
            own(t).wait_recv()
            for k, chip in enumerate(chips):
                d2d(t, k, chip, 1 - c).wait_recv()
        for cp in sends + passed:
            cp.wait_send()

    return pl.pallas_call(
        body, name="gather_weights",
        in_specs=[ANY] * nt, out_specs=[ANY] * nt,
        out_shape=[jax.ShapeDtypeStruct((N_CHIPS,) + s.shape, s.dtype) for s in shards],
        scratch_shapes=[pltpu.SemaphoreType.DMA((nt, 3))] * 4 + [pltpu.SemaphoreType.DMA((nt,))] * 2,
        compiler_params=pltpu.CompilerParams(has_side_effects=True))(*shards)


def _gather_over_ici(shards):
    nt = len(shards)

    def copies(src, dst, scr, arriving):
        ici_send, ici_recv, own_send, own_recv = scr
        x, y, c, chips = _place()
        out = []
        for t in range(nt):
            hr = src[t].shape[0] // 2
            rows = pl.ds(c * hr, hr)
            for k, (cx, cy) in enumerate(chips):
                block = 2 * cx + cy if arriving else 2 * x + y
                out.append(pltpu.make_async_remote_copy(
                    src_ref=src[t].at[rows], dst_ref=dst[t].at[block, rows],
                    send_sem=ici_send.at[t, k], recv_sem=ici_recv.at[t, k],
                    device_id=(cx, cy, c), device_id_type=MESH))
            out.append(pltpu.make_async_remote_copy(
                src_ref=src[t], dst_ref=dst[t].at[2 * x + y], send_sem=own_send.at[t], recv_sem=own_recv.at[t],
                device_id=(x, y, 1 - c), device_id_type=MESH))
        return out

    def start(src, dst, scr):
        for cp in copies(src, dst, scr, False):
            cp.start()

    def finish(src, dst, scr):
        for cp in copies(src, dst, scr, True):
            cp.wait_recv()
        for cp in copies(src, dst, scr, False):
            cp.wait_send()

    return _Comm(shards, [jax.ShapeDtypeStruct((N_CHIPS,) + s.shape, s.dtype) for s in shards],
                 [pltpu.SemaphoreType.DMA((nt, 3))] * 2 + [pltpu.SemaphoreType.DMA((nt,))] * 2, start, finish)


def _gather_pass_on(bufs):
    nt = len(bufs)

    def passed(dst, scr, t, k, block, pc):
        send, recv = scr
        x, y, c, _ = _place()
        cx, cy = block
        hr = dst[t].shape[1] // 2
        ref = dst[t].at[2 * cx + cy, pl.ds(pc * hr, hr)]
        return pltpu.make_async_remote_copy(src_ref=ref, dst_ref=ref, send_sem=send.at[t, k], recv_sem=recv.at[t, k],
                                            device_id=(x, y, 1 - c), device_id_type=MESH)

    def start(src, dst, scr):
        _, _, c, chips = _place()
        for t in range(nt):
            for k, chip in enumerate(chips):
                passed(dst, scr, t, k, chip, c).start()

    def finish(src, dst, scr):
        _, _, c, chips = _place()
        for t in range(nt):
            for k, chip in enumerate(chips):
                passed(dst, scr, t, k, chip, 1 - c).wait_recv()
        for t in range(nt):
            for k, chip in enumerate(chips):
                passed(dst, scr, t, k, chip, c).wait_send()

    return _Comm(bufs, [jax.ShapeDtypeStruct(b.shape, b.dtype) for b in bufs],
                 [pltpu.SemaphoreType.DMA((nt, 3))] * 2, start, finish, aliases={t: t for t in range(nt)})


def _exchange_halves(grads):
    nt = len(grads)

    def copies(src, dst, scr):
        send, recv = scr
        x, y, c, _ = _place()
        out = []
        for t in range(nt):
            hr = src[t].shape[1] // 2
            out.append(pltpu.make_async_remote_copy(
                src_ref=src[t].at[:, pl.ds((1 - c) * hr, hr)], dst_ref=dst[t],
                send_sem=send.at[t], recv_sem=recv.at[t], device_id=(x, y, 1 - c), device_id_type=MESH))
        return out

    def start(src, dst, scr):
        for cp in copies(src, dst, scr):
            cp.start()

    def finish(src, dst, scr):
        for cp in copies(src, dst, scr):
            cp.wait()

    return _Comm(grads, [jax.ShapeDtypeStruct((g.shape[0], g.shape[1] // 2, g.shape[2]), g.dtype) for g in grads],
                 [pltpu.SemaphoreType.DMA((nt,))] * 2, start, finish)


def _add_halves(grad, other, name):
    nb, R, C = grad.shape
    hr = R // 2
    tr = _tile(hr, 256, 16)
    steps = hr // tr
    c = lax.axis_index("c").astype(jnp.int32).reshape((1,))

    def body(c_ref, a_ref, b_ref, o_ref):
        o_ref[...] = (a_ref[...].astype(F32) + b_ref[...].astype(F32)).astype(o_ref.dtype)

    return pl.pallas_call(
        body, name=name,
        grid_spec=pltpu.PrefetchScalarGridSpec(
            num_scalar_prefetch=1, grid=(nb, steps),
            in_specs=[pl.BlockSpec((1, tr, C), lambda j, i, c_ref: (j, c_ref[0] * steps + i, 0)),
                      pl.BlockSpec((1, tr, C), lambda j, i, c_ref: (j, i, 0))],
            out_specs=pl.BlockSpec((1, tr, C), lambda j, i, c_ref: (j, i, 0))),
        out_shape=jax.ShapeDtypeStruct((nb, hr, C), grad.dtype),
        compiler_params=_params("parallel", "parallel"))(c, grad, other)


def _scatter_chip_sums(parts):
    nt = len(parts)

    def copies(src, dst, scr):
        send, recv = scr
        x, y, c, chips = _place()
        out = []
        for t in range(nt):
            for k, (cx, cy) in enumerate(chips):
                out.append(pltpu.make_async_remote_copy(
                    src_ref=src[t].at[2 * cx + cy], dst_ref=dst[t].at[k],
                    send_sem=send.at[t, k], recv_sem=recv.at[t, k], device_id=(cx, cy, c), device_id_type=MESH))
        return out

    def start(src, dst, scr):
        for cp in copies(src, dst, scr):
            cp.start()

    def finish(src, dst, scr):
        for cp in copies(src, dst, scr):
            cp.wait()

    return _Comm(parts, [jax.ShapeDtypeStruct((3,) + p.shape[1:], p.dtype) for p in parts],
                 [pltpu.SemaphoreType.DMA((nt, 3))] * 2, start, finish)


def _sum_chips(part, recv, name):
    _, hr, C = part.shape
    tr = _tile(hr, 256, 16)
    steps = hr // tr
    where = jnp.stack([2 * lax.axis_index("x") + lax.axis_index("y"), lax.axis_index("c")]).astype(jnp.int32)

    def body(w_ref, a_ref, b_ref, o_ref):
        acc = a_ref[0].astype(F32)
        for k in range(3):
            acc = acc + b_ref[k].astype(F32)
        o_ref[...] = acc

    return pl.pallas_call(
        body, name=name,
        grid_spec=pltpu.PrefetchScalarGridSpec(
            num_scalar_prefetch=1, grid=(steps,),
            in_specs=[pl.BlockSpec((1, tr, C), lambda i, w_ref: (w_ref[0], i, 0)),
                      pl.BlockSpec((3, tr, C), lambda i, w_ref: (0, i, 0))],
            out_specs=pl.BlockSpec((tr, C), lambda i, w_ref: (w_ref[1] * steps + i, 0))),
        out_shape=jax.ShapeDtypeStruct((2 * hr, C), F32),
        compiler_params=_params("parallel"))(where, part, recv)


def _join_halves(bufs):
    nt = len(bufs)

    def swap(dst, scr, t, pc):
        send, recv = scr
        x, y, c, _ = _place()
        hr = dst[t].shape[0] // 2
        rows = dst[t].at[pl.ds(pc * hr, hr)]
        return pltpu.make_async_remote_copy(src_ref=rows, dst_ref=rows, send_sem=send.at[t], recv_sem=recv.at[t],
                                            device_id=(x, y, 1 - c), device_id_type=MESH)

    def start(src, dst, scr):
        c = lax.axis_index("c")
        for t in range(nt):
            swap(dst, scr, t, c).start()

    def finish(src, dst, scr):
        c = lax.axis_index("c")
        for t in range(nt):
            swap(dst, scr, t, 1 - c).wait_recv()
        for t in range(nt):
            swap(dst, scr, t, c).wait_send()

    return _Comm(bufs, [jax.ShapeDtypeStruct(b.shape, b.dtype) for b in bufs],
                 [pltpu.SemaphoreType.DMA((nt,))] * 2, start, finish, aliases={t: t for t in range(nt)})


def _reduce_scatter_in_vmem(g):
    nb, R, C = g.shape
    hr = R // 2

    def run(ins, outs, scr):
        (g_ref,), (out_ref,) = ins, outs
        other, part, got, send, recv = scr
        x, y, c, chips = _place()
        sibling = (x, y, 1 - c)
        my_rows = pl.ds(pl.multiple_of(c * hr, hr), hr)
        their_rows = pl.ds(pl.multiple_of((1 - c) * hr, hr), hr)
        swap = pltpu.make_async_remote_copy(src_ref=g_ref.at[:, their_rows], dst_ref=other, send_sem=send.at[0],
                                            recv_sem=recv.at[0], device_id=sibling, device_id_type=MESH)
        swap.start()
        swap.wait()
        part[...] = (g_ref[:, my_rows, :].astype(F32) + other[...].astype(F32)).astype(part.dtype)
        to_owner = [pltpu.make_async_remote_copy(src_ref=part.at[2 * cx + cy], dst_ref=got.at[k],
                                                 send_sem=send.at[1 + k], recv_sem=recv.at[1 + k],
                                                 device_id=(cx, cy, c), device_id_type=MESH)
                    for k, (cx, cy) in enumerate(chips)]
        for cp in to_owner:
            cp.start()
        for cp in to_owner:
            cp.wait()
        total = part[2 * x + y].astype(F32)
        for k in range(3):
            total = total + got[k].astype(F32)
        out_ref[my_rows, :] = total

        def join(rows):
            return pltpu.make_async_remote_copy(src_ref=out_ref.at[rows], dst_ref=out_ref.at[rows], send_sem=send.at[4],
                                                recv_sem=recv.at[4], device_id=sibling, device_id_type=MESH)

        join(my_rows).start()
        join(their_rows).wait_recv()
        join(my_rows).wait_send()

    return _Comm([g], [jax.ShapeDtypeStruct((R, C), F32)],
                 [pltpu.VMEM((nb, hr, C), g.dtype), pltpu.VMEM((nb, hr, C), g.dtype), pltpu.VMEM((3, hr, C), g.dtype),
                  pltpu.SemaphoreType.DMA((5,)), pltpu.SemaphoreType.DMA((5,))],
                 run, lambda ins, outs, scr: None, in_specs=[WHOLE_VMEM], out_specs=[WHOLE_VMEM])


def _all_reduce_rows(buf, loss_row=None):
    R, L = buf.shape

    def copies(in_ref, gath, send, recv):
        x, y, c, _ = _place()
        out = []
        for k in range(1, N_DEV):
            peer = (x ^ ((k >> 2) & 1), y ^ ((k >> 1) & 1), c ^ (k & 1))
            out.append(pltpu.make_async_remote_copy(
                src_ref=in_ref, dst_ref=gath.at[k], send_sem=send.at[k - 1], recv_sem=recv.at[k - 1],
                device_id=peer, device_id_type=MESH))
        return out

    def start(ins, outs, scr):
        gath, send, recv = scr
        gath[0] = ins[0][...]
        for cp in copies(ins[0], gath, send, recv):
            cp.start()

    def finish(ins, outs, scr):
        gath, send, recv = scr
        for cp in copies(ins[0], gath, send, recv):
            cp.wait()
        x, y, c, _ = _place()
        me = 4 * x + 2 * y + c
        total = gath[me]
        for d in range(1, N_DEV):
            total = total + gath[d ^ me]
        outs[0][...] = total
        if loss_row is not None:
            outs[1][...] = jnp.sum(total[loss_row:loss_row + 1, :], axis=1, keepdims=True)

    out_shape = [jax.ShapeDtypeStruct((R, L), F32)]
    if loss_row is not None:
        out_shape.append(jax.ShapeDtypeStruct((1, 1), F32))
    return _Comm([buf], out_shape,
                 [pltpu.VMEM((N_DEV, R, L), F32), pltpu.SemaphoreType.DMA((N_DEV - 1,)),
                  pltpu.SemaphoreType.DMA((N_DEV - 1,))],
                 start, finish, in_specs=[WHOLE_VMEM], out_specs=[WHOLE_VMEM] * len(out_shape))


def _adamw_update(w_ref, g_ref, m_ref, v_ref, d_ref, nm_ref, nv_ref):
    gv = g_ref[...]
    nm = ADAM_B1 * m_ref[...] + (1.0 - ADAM_B1) * gv
    nv = ADAM_B2 * v_ref[...] + (1.0 - ADAM_B2) * (gv * gv)
    nm_ref[...] = nm
    nv_ref[...] = nv
    m_hat = nm / (1.0 - ADAM_B1 ** ADAM_STEP)
    v_hat = nv / (1.0 - ADAM_B2 ** ADAM_STEP)
    d_ref[...] = -ADAM_LR * (m_hat / (jnp.sqrt(v_hat) + ADAM_EPS) + ADAM_WD * w_ref[...])


def _adamw(w, g, m, v, name):
    R, C = w.shape
    tr = _tile(R, 256)

    def body(w_ref, g_ref, m_ref, v_ref, d_ref, nm_ref, nv_ref, g_out):
        _adamw_update(w_ref, g_ref, m_ref, v_ref, d_ref, nm_ref, nv_ref)
        g_out[...] = g_ref[...]

    blk = pl.BlockSpec((tr, C), lambda i: (i, 0))
    return pl.pallas_call(
        body, name=name, grid=(R // tr,), in_specs=[blk] * 4, out_specs=[blk] * 4,
        out_shape=[jax.ShapeDtypeStruct((R, C), F32)] * 4,
        compiler_params=_params("parallel"))(w, g, m, v)


def _adamw_many(ws, gs, ms, vs, name):
    n = len(ws)

    def body(*refs):
        for k in range(n):
            _adamw_update(*[refs[part * n + k] for part in range(7)])

    shapes = [jax.ShapeDtypeStruct(w.shape, F32) for w in ws]
    outs = pl.pallas_call(
        body, name=name, in_specs=[WHOLE_VMEM] * (4 * n), out_specs=[WHOLE_VMEM] * (3 * n), out_shape=shapes * 3,
        compiler_params=pltpu.CompilerParams(vmem_limit_bytes=VMEM_LIMIT_BYTES))(*ws, *gs, *ms, *vs)
    return outs[:n], outs[n:2 * n], outs[2 * n:]


def _pack_rows(arrays):
    rows = []
    for a in arrays:
        flat = a.reshape(-1).astype(F32)
        pad = (-flat.shape[0]) % LANES
        rows.append(jnp.pad(flat, (0, pad)).reshape(-1, LANES))
    buf = jnp.concatenate(rows, axis=0)
    return jnp.pad(buf, ((0, (-buf.shape[0]) % 8), (0, 0)))


def _unpack_rows(buf, shapes):
    out, r = [], 0
    for s in shapes:
        n = math.prod(s)
        nr = -(-n // LANES)
        out.append(buf[r:r + nr].reshape(-1)[:n].reshape(s))
        r += nr
    return out


def _block_diag(w):
    H, a, b = w.shape
    eye = jnp.eye(H, dtype=w.dtype)
    return (eye[:, None, :, None] * w[:, :, None, :]).reshape(H * a, H * b)


def _block_diag_parts(d, H):
    a, b = d.shape[0] // H, d.shape[1] // H
    d4 = d.reshape(H, a, H, b)
    return jnp.stack([d4[h, :, h, :] for h in range(H)])


def _rs_add(names, grads, others):
    return [_add_halves(g, o, "rs_add_halves_" + n) for n, g, o in zip(names, grads, others)]


def _rs_sum(names, parts, recvs):
    return [_sum_chips(p, r, "rs_sum_chips_" + n) for n, p, r in zip(names, parts, recvs)]


def _step(x, mem, target, shards, small, tap_rows, tap_shapes):
    D = x.shape[1]
    nch = N_CHIPS
    p = dict(small)

    (w_in_f,) = _gather_weights([shards['w_in']])
    wf = {}

    def ici(names):
        return _gather_over_ici([shards[n] for n in names])

    ici_a, taps_sum = ici(['w_out', 'w_q']), _all_reduce_rows(tap_rows)
    (z, h1), couts = _fwd_in(x, p['mix_norm_g'], w_in_f, comm=_merge(ici_a, taps_sum))
    bufs_a, (taps,) = _split(couts, ici_a, taps_sum)
    p.update(zip(COL_SHARDED_SMALL, _unpack_rows(taps, tap_shapes)))
    wa_d = _block_diag(p['lru_w_a']).astype(MXU_DTYPE)
    wx_d = _block_diag(p['lru_w_x']).astype(MXU_DTYPE)
    heads = p['lru_w_a'].shape[0]
    pass_a, ici_b = _gather_pass_on(bufs_a), ici(['w_kv', 'w_o'])
    (h, y_lru), couts = _lru_fwd(z, p['lru_conv_w'], p['lru_conv_b'], wa_d, p['lru_b_a'], wx_d, p['lru_b_x'],
                                 p['lru_lambda'], comm=_merge(pass_a, ici_b))
    (wf['w_out'], wf['w_q']), bufs_b = _split(couts, pass_a, ici_b)
    pass_b, ici_c = _gather_pass_on(bufs_b), ici(['w_up'])
    (c1, c3), couts = _conf_fwd(z, p['conf_conv_w'], p['conf_conv_b'], p['conf_ln_g'], p['conf_ln_b'],
                                comm=_merge(pass_b, ici_c))
    (wf['w_kv'], wf['w_o']), bufs_c = _split(couts, pass_b, ici_c)
    w_out2 = wf['w_out'].reshape(2, -1, D)
    w_q = wf['w_q'].reshape(D, D)
    w_o = wf['w_o'].reshape(D, D)
    pass_c, ici_d = _gather_pass_on(bufs_c), ici(['w_down'])
    (x1, h2, q), couts = _fwd_out_q(x, y_lru, c3, w_out2, p['xa_norm_g'], w_q, comm=_merge(pass_c, ici_d))
    (wf['w_up'],), bufs_d = _split(couts, pass_c, ici_d)
    m, kv = _kv_fwd(mem, p['mem_norm_g'], wf['w_kv'])
    (o, x2, h3), (wf['w_down'],) = _attn_fwd(q, kv, x1, w_o, p['ffn_norm_g'], comm=_gather_pass_on(bufs_d))
    gu, act, dx3, loss_lanes, d_final_g = _ffn_fwd(h3, wf['w_up'], p['ffn_conv_w'], p['ffn_conv_b'], wf['w_down'],
                                                   x2, p['final_norm_g'], target)

    dgu, dx2, d_ffn_g, d_ffn_cw, d_ffn_cb = _ffn_bwd(dx3, wf['w_down'], wf['w_up'], gu, x2, p['ffn_norm_g'],
                                                     p['ffn_conv_w'], p['ffn_conv_b'])
    g_down = _wgrad(act, dx3[None], "wgrad_down").reshape(nch, -1, D)
    g_up, other = _wgrad(h3[None], dgu, "wgrad_up", comm=_exchange_halves([g_down]))
    (p_down,) = _rs_add(['w_down'], [g_down], other)
    sc_down, ex_up = _scatter_chip_sums([p_down]), _exchange_halves([g_up])
    (dq, dx1, dkv, d_xa_g), couts = _attn_bwd(dx2, w_o, q, kv, x1, p['xa_norm_g'], w_q, comm=_merge(sc_down, ex_up))
    recv, other = _split(couts, sc_down, ex_up)
    f_down = _rs_sum(['w_down'], [p_down], recv)
    (p_up,) = _rs_add(['w_up'], [g_up], other)
    mid = ['w_o', 'w_q', 'w_kv']
    g_o = _wgrad(o[None], dx2[None], "wgrad_o").reshape(nch, -1, D)
    g_q = _wgrad(h2[None], dq[None], "wgrad_q").reshape(nch, -1, D)
    g_kv, d_mem_g = _kv_bwd(dkv, wf['w_kv'], mem, p['mem_norm_g'], m)
    join_down, sc_up, ex_mid = _join_halves(f_down), _scatter_chip_sums([p_up]), _exchange_halves([g_o, g_q, g_kv])
    (dz_c, d_conf_cw, d_conf_cb, d_ln_g, d_ln_b), couts = _conf_bwd(
        dx1, w_out2[1], z, c1, p['conf_conv_w'], p['conf_ln_g'], p['conf_ln_b'],
        comm=_merge(join_down, sc_up, ex_mid))
    (r_down,), recv, other = _split(couts, join_down, sc_up, ex_mid)
    p_up = [p_up]
    p_mid = _rs_add(mid, [g_o, g_q, g_kv], other)
    join_up, sc_mid = _join_halves(_rs_sum(['w_up'], p_up, recv)), _scatter_chip_sums(p_mid)
    (dz, d_wa, d_wx, d_ba, d_bx, d_lam, d_lru_cw, d_lru_cb), couts = _lru_bwd(
        dx1, w_out2[0], z, h, p['lru_conv_w'], p['lru_conv_b'], wa_d, p['lru_b_a'], wx_d, p['lru_b_x'],
        p['lru_lambda'], dz_c, comm=_merge(join_up, sc_mid))
    (r_up,), recv = _split(couts, join_up, sc_mid)
    f_mid = _rs_sum(mid, p_mid, recv)
    grad_x, d_mix_g = _bwd_in(dz, w_in_f, x, p['mix_norm_g'], dx1)

    small_g = {'mix_norm_g': d_mix_g, 'lru_conv_w': d_lru_cw, 'lru_conv_b': d_lru_cb,
               'lru_w_a': _block_diag_parts(d_wa, heads), 'lru_b_a': d_ba,
               'lru_w_x': _block_diag_parts(d_wx, heads), 'lru_b_x': d_bx, 'lru_lambda': d_lam,
               'conf_conv_w': d_conf_cw, 'conf_conv_b': d_conf_cb, 'conf_ln_g': d_ln_g, 'conf_ln_b': d_ln_b,
               'xa_norm_g': d_xa_g, 'mem_norm_g': d_mem_g, 'ffn_norm_g': d_ffn_g,
               'ffn_conv_w': d_ffn_cw, 'ffn_conv_b': d_ffn_cb, 'final_norm_g': d_final_g}
    names = list(small_g)
    shapes = [small_g[n].shape for n in names]
    join_mid = _join_halves(f_mid)
    small_sum = _all_reduce_rows(_pack_rows([loss_lanes] + [small_g[n] for n in names]), loss_row=0)
    g_in, couts = _wgrad(h1[None], dz, "wgrad_in", comm=_merge(join_mid, small_sum))
    r_mid, (summed, loss) = _split(couts, join_mid, small_sum)
    g_out_l, other = _wgrad(y_lru[None], dx1[None], "wgrad_out_lru", comm=_exchange_halves([g_in]))
    p_in = _rs_add(['w_in'], [g_in], other)
    g_out_c, recv = _wgrad(c3[None], dx1[None], "wgrad_out_conf", comm=_scatter_chip_sums(p_in))
    f_in = _rs_sum(['w_in'], p_in, recv)

    g_out = jnp.concatenate([g_out_l, g_out_c], axis=0).reshape(nch, -1, D)
    join_in, rs_out = _join_halves(f_in), _reduce_scatter_in_vmem(g_out)
    (r_in,), (r_out,) = _split(_run_comm(_merge(join_in, rs_out), "rs_last"), join_in, rs_out)
    big = dict(zip(['w_down', 'w_up'] + mid + ['w_out', 'w_in'], [r_down, r_up] + r_mid + [r_out, r_in]))
    return grad_x, big, summed, loss, names, [loss_lanes.shape] + shapes


def kernel(x, mem, mix_norm_g, w_in, lru_conv_w, lru_conv_b, lru_w_a, lru_b_a, lru_w_x, lru_b_x, lru_lambda, conf_conv_w, conf_conv_b, conf_ln_g, conf_ln_b, w_out, xa_norm_g, mem_norm_g, w_q, w_kv, w_o, ffn_norm_g, w_up, ffn_conv_w, ffn_conv_b, w_down, final_norm_g, loss_target, m_mix_norm_g, m_w_in, m_lru_conv_w, m_lru_conv_b, m_lru_w_a, m_lru_b_a, m_lru_w_x, m_lru_b_x, m_lru_lambda, m_conf_conv_w, m_conf_conv_b, m_conf_ln_g, m_conf_ln_b, m_w_out, m_xa_norm_g, m_mem_norm_g, m_w_q, m_w_kv, m_w_o, m_ffn_norm_g, m_w_up, m_ffn_conv_w, m_ffn_conv_b, m_w_down, m_final_norm_g, v_mix_norm_g, v_w_in, v_lru_conv_w, v_lru_conv_b, v_lru_w_a, v_lru_b_a, v_lru_w_x, v_lru_b_x, v_lru_lambda, v_conf_conv_w, v_conf_conv_b, v_conf_ln_g, v_conf_ln_b, v_w_out, v_xa_norm_g, v_mem_norm_g, v_w_q, v_w_kv, v_w_o, v_ffn_norm_g, v_w_up, v_ffn_conv_w, v_ffn_conv_b, v_w_down, v_final_norm_g):
    given = dict(locals())
    w = {n: given[n] for n in WEIGHTS}
    mom = {n: given["m_" + n] for n in WEIGHTS}
    var = {n: given["v_" + n] for n in WEIGHTS}
    xi, yi, ci = lax.axis_index("x"), lax.axis_index("y"), lax.axis_index("c")
    chip = 2 * xi + yi

    shards = {n: w[n][0].astype(WIRE_DTYPE) for n in BIG}
    tap_full = []
    for n in COL_SHARDED_SMALL:
        s = w[n][0]
        full = jnp.zeros((s.shape[0], N_CHIPS * s.shape[1]), F32)
        s = jnp.where(ci == 0, s, jnp.zeros_like(s))
        tap_full.append(lax.dynamic_update_slice(full, s, (0, chip * s.shape[1])))
    small = {n: (w[n] if w[n].ndim == 1 else w[n][0]) for n in SMALL if n not in COL_SHARDED_SMALL}
    small = {n: (a.reshape(1, -1) if a.ndim == 1 else a) for n, a in small.items()}

    grad_x, big_g, summed, loss, small_names, packed_shapes = _step(
        x[0], mem[0], loss_target[0], shards, small, _pack_rows(tap_full), [t.shape for t in tap_full])
    small_sum = dict(zip(small_names, _unpack_rows(summed, packed_shapes)[1:]))

    grads = {}
    for n in WEIGHTS:
        if n in BIG:
            g = big_g[n]
        elif n in COL_SHARDED_SMALL:
            width = w[n].shape[-1]
            g = lax.dynamic_slice_in_dim(small_sum[n], chip * width, width, axis=1)
        else:
            g = small_sum[n]
        grads[n] = g.reshape(w[n].shape)

    delta, new_m, new_v = {}, {}, {}
    for n in BIG:
        d, nm, nv, g = _adamw(w[n][0], grads[n][0], mom[n][0], var[n][0], "adamw_" + n)
        delta[n], new_m[n], new_v[n], grads[n] = d[None], nm[None], nv[None], g[None]
    flat = lambda a: a.reshape(-1, a.shape[-1])
    outs = _adamw_many(*[[flat(src[n]) for n in SMALL] for src in (w, grads, mom, var)], "adamw_small")
    for out, arrays in zip((delta, new_m, new_v), outs):
        out.update({n: a.reshape(w[n].shape) for n, a in zip(SMALL, arrays)})

    return (loss[0, 0], grad_x[None], *[grads[n] for n in WEIGHTS], *[delta[n] for n in WEIGHTS],
            *[new_m[n] for n in WEIGHTS], *[new_v[n] for n in WEIGHTS])
```

```python
import math

import jax
import jax.numpy as jnp
from jax import lax
from jax.experimental import pallas as pl
from jax.experimental.pallas import tpu as pltpu

F32 = jnp.float32
MXU_DTYPE = jnp.bfloat16
WIRE_DTYPE = jnp.bfloat16
EPS = 1e-6
RG_C = 8.0
XA_HEADS = 4
ADAM_LR, ADAM_B1, ADAM_B2, ADAM_EPS, ADAM_WD, ADAM_STEP = 0.001, 0.9, 0.999, 1e-08, 0.01, 10
VMEM_LIMIT_BYTES = 52 * 1024 * 1024
WGRAD_ACC_BYTES = 8 * 1024 * 1024
LANES = 1024
N_CHIPS = 4
N_DEV = 8
MESH = pl.DeviceIdType.MESH
GELU_C = math.sqrt(2.0 / math.pi)
GELU_K = 0.044715

WEIGHTS = ['mix_norm_g', 'w_in', 'lru_conv_w', 'lru_conv_b', 'lru_w_a', 'lru_b_a', 'lru_w_x', 'lru_b_x',
           'lru_lambda', 'conf_conv_w', 'conf_conv_b', 'conf_ln_g', 'conf_ln_b', 'w_out', 'xa_norm_g',
           'mem_norm_g', 'w_q', 'w_kv', 'w_o', 'ffn_norm_g', 'w_up', 'ffn_conv_w', 'ffn_conv_b', 'w_down',
           'final_norm_g']
BIG = ['w_in', 'w_kv', 'w_up', 'w_out', 'w_q', 'w_o', 'w_down']
SMALL = [n for n in WEIGHTS if n not in BIG]
COL_SHARDED_SMALL = ['lru_conv_w', 'conf_conv_w', 'ffn_conv_w']


def _params(*semantics):
    return pltpu.CompilerParams(dimension_semantics=semantics, vmem_limit_bytes=VMEM_LIMIT_BYTES)


ANY = pl.BlockSpec(memory_space=pl.ANY)
WHOLE_VMEM = pl.BlockSpec(memory_space=pltpu.VMEM)


class _Comm:
    def __init__(self, arrays, out_shapes, scratch, start, finish, aliases=None, in_specs=None, out_specs=None):
        self.arrays, self.out_shapes, self.scratch = list(arrays), list(out_shapes), list(scratch)
        self.start, self.finish = start, finish
        self.aliases = dict(aliases or {})
        self.in_specs = list(in_specs) if in_specs is not None else [ANY] * len(self.arrays)
        self.out_specs = list(out_specs) if out_specs is not None else [ANY] * len(self.out_shapes)


def _merge(*comms):
    comms = [c for c in comms if c is not None]
    if not comms:
        return None
    ai = [0]
    for c in comms:
        ai.append(ai[-1] + len(c.arrays))
    oi = [0]
    for c in comms:
        oi.append(oi[-1] + len(c.out_shapes))
    si = [0]
    for c in comms:
        si.append(si[-1] + len(c.scratch))

    def each(which):
        def run(ins, outs, scr):
            for k, c in enumerate(comms):
                getattr(c, which)(ins[ai[k]:ai[k + 1]], outs[oi[k]:oi[k + 1]], scr[si[k]:si[k + 1]])
        return run

    aliases = {ai[k] + i: oi[k] + o for k, c in enumerate(comms) for i, o in c.aliases.items()}
    return _Comm(sum((c.arrays for c in comms), []), sum((c.out_shapes for c in comms), []),
                 sum((c.scratch for c in comms), []), each("start"), each("finish"), aliases,
                 sum((c.in_specs for c in comms), []), sum((c.out_specs for c in comms), []))


def _split(outs, *comms):
    parts, at = [], 0
    for c in comms:
        parts.append(outs[at:at + len(c.out_shapes)])
        at += len(c.out_shapes)
    return parts


def _pcall(comm, body, *, name, grid, in_specs, out_specs, out_shape, semantics, scratch_shapes=(), aliases=None):
    single = not isinstance(out_shape, (list, tuple))
    out_shape = [out_shape] if single else list(out_shape)
    out_specs = [out_specs] if single else list(out_specs)
    in_specs, scratch_shapes = list(in_specs), list(scratch_shapes)
    aliases = dict(aliases or {})

    if comm is None:
        def plain(*args):
            return list(pl.pallas_call(body, name=name, grid=grid, in_specs=in_specs, out_specs=out_specs,
                                       out_shape=out_shape, scratch_shapes=scratch_shapes,
                                       input_output_aliases=aliases,
                                       compiler_params=_params(*semantics))(*args))
        return plain

    def hosted(*args):
        n_in, n_out, n_scr = len(args), len(out_shape), len(scratch_shapes)
        c_in, c_out = len(comm.arrays), len(comm.out_shapes)

        def wrapped(*refs):
            ins, cins = refs[:n_in], refs[n_in:n_in + c_in]
            o0 = n_in + c_in
            outs, couts = refs[o0:o0 + n_out], refs[o0 + n_out:o0 + n_out + c_out]
            s0 = o0 + n_out + c_out
            scr, cscr = refs[s0:s0 + n_scr], refs[s0 + n_scr:]
            first = last = None
            for axis, size in enumerate(grid):
                at_start, at_end = pl.program_id(axis) == 0, pl.program_id(axis) == size - 1
                first = at_start if first is None else first & at_start
                last = at_end if last is None else last & at_end
            if first is None:
                comm.start(cins, couts, cscr)
                body(*ins, *outs, *scr)
                comm.finish(cins, couts, cscr)
                return
            pl.when(first)(lambda: comm.start(cins, couts, cscr))
            body(*ins, *outs, *scr)
            pl.when(last)(lambda: comm.finish(cins, couts, cscr))

        res = pl.pallas_call(
            wrapped, name=name, grid=grid, in_specs=in_specs + comm.in_specs, out_specs=out_specs + comm.out_specs,
            out_shape=out_shape + comm.out_shapes, scratch_shapes=scratch_shapes + comm.scratch,
            input_output_aliases={**aliases, **{n_in + i: n_out + o for i, o in comm.aliases.items()}},
            compiler_params=pltpu.CompilerParams(dimension_semantics=("arbitrary",) * len(grid),
                                                 vmem_limit_bytes=VMEM_LIMIT_BYTES, has_side_effects=True),
        )(*args, *comm.arrays)
        return list(res[:n_out]), list(res[n_out:])

    return hosted


def _run_comm(comm, name):
    return _pcall(comm, lambda: None, name=name, grid=(), in_specs=[], out_specs=[], out_shape=[], semantics=())()[1]


def _tile(n, want, align=8):
    if n <= want:
        return n
    for t in range(want - want % align, 0, -align):
        if n % t == 0:
            return t
    raise ValueError((n, want, align))


def _mm(a, b):
    return jnp.dot(a.astype(MXU_DTYPE), b.astype(MXU_DTYPE), preferred_element_type=F32)


def _mm_nt(a, b):
    return lax.dot_general(a.astype(MXU_DTYPE), b.astype(MXU_DTYPE), (((1,), (1,)), ((), ())),
                           preferred_element_type=F32)


def _mm_tn(a, b):
    return lax.dot_general(a.astype(MXU_DTYPE), b.astype(MXU_DTYPE), (((0,), (0,)), ((), ())),
                           preferred_element_type=F32)


def _sigmoid(v):
    return 0.5 * jnp.tanh(0.5 * v) + 0.5


def _gelu(v):
    v2 = v * v
    t = jnp.tanh(v * (GELU_C + (GELU_C * GELU_K) * v2))
    hv = 0.5 * v
    dt = (1.0 - t * t) * (GELU_C + (3.0 * GELU_C * GELU_K) * v2)
    return hv + hv * t, (0.5 + 0.5 * t) + hv * dt


def _softplus_neg(lam):
    e = jnp.exp(-jnp.abs(lam))
    u = 1.0 + e
    log1p_e = jnp.where(u == 1.0, e, jnp.log(u) * e / jnp.where(u == 1.0, 1.0, u - 1.0))
    return jnp.maximum(-lam, 0.0) + log1p_e


def _rms(xv):
    rinv = lax.rsqrt(jnp.mean(xv * xv, axis=-1, keepdims=True) + EPS)
    return rinv, xv * rinv


def _rms_bwd(rinv, xhat, dxhat):
    return rinv * (dxhat - xhat * jnp.mean(dxhat * xhat, axis=-1, keepdims=True))


def _colsum(v):
    return jnp.sum(v, axis=0, keepdims=True)


def _wrow(w_ref, k, wcols):
    return w_ref[pl.ds(k, 1), :] if wcols is None else w_ref[pl.ds(k, 1), wcols]


def _windows(buf_ref, halo, taps, rows):
    assert taps <= 8 <= halo
    x = buf_ref[pl.ds(halo - 8, rows + 8), :]
    return [x[8:] if s == 0 else pltpu.roll(x, s, 0)[8:] for s in range(taps)]


def _causal_from(xs, w_ref, wcols=None):
    taps = len(xs)
    acc = None
    for s in range(taps):
        term = _wrow(w_ref, taps - 1 - s, wcols) * xs[s]
        acc = term if acc is None else acc + term
    return acc


def _tap_grads_from(dw_ref, dy, xs, wcols=None):
    taps = len(xs)
    for s in range(taps):
        g = _colsum(dy * xs[s])
        if wcols is None:
            dw_ref[pl.ds(taps - 1 - s, 1), :] += g
        else:
            dw_ref[pl.ds(taps - 1 - s, 1), wcols] += g


def _causal_taps(buf_ref, halo, w_ref, taps, rows, wcols=None):
    return _causal_from(_windows(buf_ref, halo, taps, rows), w_ref, wcols)


def _anticausal_taps(buf_ref, w_ref, taps, rows, wcols=None):
    assert taps <= 8
    x = buf_ref[pl.ds(0, rows + 8), :]
    acc = None
    for s in range(taps):
        win = x[:rows] if s == 0 else pltpu.roll(x, rows + 8 - s, 0)[:rows]
        term = _wrow(w_ref, taps - 1 - s, wcols) * win
        acc = term if acc is None else acc + term
    return acc


def _shift_copies(dst_ref, buf_ref, rows, up):
    x = buf_ref[pl.ds(0, rows + 8), :]
    for r in range(8):
        if up:
            dst_ref[r] = x[:rows] if r == 0 else pltpu.roll(x, rows + 8 - r, 0)[:rows]
        else:
            dst_ref[r] = x[8:] if r == 0 else pltpu.roll(x, r, 0)[8:]


def _causal_taps8(sh_ref, halo, w_ref, taps, rows):
    acc = None
    for s in range(taps):
        term = _wrow(w_ref, taps - 1 - s, None) * sh_ref[s % 8, pl.ds(halo - 8 - 8 * (s // 8), rows), :]
        acc = term if acc is None else acc + term
    return acc


def _anticausal_taps8(sh_ref, w_ref, taps, rows):
    acc = None
    for s in range(taps):
        term = _wrow(w_ref, taps - 1 - s, None) * sh_ref[s % 8, pl.ds(8 * (s // 8), rows), :]
        acc = term if acc is None else acc + term
    return acc


def _tap_grads8(dw_ref, dy, sh_ref, halo, taps, rows):
    for s in range(taps):
        dw_ref[pl.ds(taps - 1 - s, 1), :] += _colsum(dy * sh_ref[s % 8, pl.ds(halo - 8 - 8 * (s // 8), rows), :])


def _fwd_in(x, g, w_in, comm=None):
    S, D = x.shape
    nb, _, C = w_in.shape
    ts = _tile(S, 1024)

    halves = 2 if ts % 32 == 0 else 1
    hr = ts // halves

    def body(x_ref, g_ref, w_ref, z_ref, h_ref):
        def norm(k):
            rows = pl.ds(k * hr, hr)
            _, xhat = _rms(x_ref[rows, :])
            h = (xhat * g_ref[...]).astype(MXU_DTYPE)
            h_ref[rows, :] = h
            return h

        h_next = norm(0)
        for k in range(halves):
            rows, h = pl.ds(k * hr, hr), h_next
            if k + 1 < halves:
                h_next = norm(k + 1)
            for j in range(nb):
                z_ref[j, rows, :] = jnp.dot(h, w_ref[j], preferred_element_type=F32)

    return _pcall(
        comm, body, name="fwd_in", grid=(S // ts,),
        in_specs=[pl.BlockSpec((ts, D), lambda i: (i, 0)), pl.BlockSpec((1, D), lambda i: (0, 0)),
                  pl.BlockSpec((nb, D, C), lambda i: (0, 0, 0))],
        out_specs=[pl.BlockSpec((nb, ts, C), lambda i: (0, i, 0)), pl.BlockSpec((ts, D), lambda i: (i, 0))],
        out_shape=[jax.ShapeDtypeStruct((nb, S, C), F32), jax.ShapeDtypeStruct((S, D), MXU_DTYPE)],
        semantics=("parallel",))(x, g, w_in)


def _lru_gates(xc, wa_ref, ba_ref, wx_ref, bx_ref, sp):
    xb = xc.astype(MXU_DTYPE)
    r = _sigmoid(jnp.dot(xb, wa_ref[...], preferred_element_type=F32) + ba_ref[...])
    ig = _sigmoid(jnp.dot(xb, wx_ref[...], preferred_element_type=F32) + bx_ref[...])
    log_a = -RG_C * r * sp
    a = jnp.exp(log_a)
    one_minus_a2 = jnp.tanh(-log_a) * (a * a + 1.0)
    inv_mult = lax.rsqrt(one_minus_a2)
    mult = jnp.where(one_minus_a2 > 0.0, one_minus_a2 * inv_mult, 0.0)
    return r, ig, a, mult, inv_mult


def _lru_fwd(z, conv_w, conv_b, wa, ba, wx, bx, lam, comm=None):
    _, S, C = z.shape
    ts = _tile(S, 512)
    taps = conv_w.shape[0]
    halo = 8

    def body(zx_ref, zg_ref, cw_ref, cb_ref, wa_ref, ba_ref, wx_ref, bx_ref, lam_ref,
             h_ref, y_ref, xbuf, a_s, u_s, hc):
        i = pl.program_id(0)

        @pl.when(i == 0)
        def _():
            xbuf[pl.ds(0, halo), :] = jnp.zeros((halo, C), F32)
            hc[...] = jnp.zeros_like(hc)

        xbuf[pl.ds(halo, ts), :] = zx_ref[0]
        xc = _causal_taps(xbuf, halo, cw_ref, taps, ts) + cb_ref[...]
        sp = _softplus_neg(lam_ref[...])
        _, ig, a, mult, _ = _lru_gates(xc, wa_ref, ba_ref, wx_ref, bx_ref, sp)
        a_s[...] = a
        u_s[...] = mult * (ig * xc)
        row = lax.broadcasted_iota(jnp.int32, (8, C), 0)

        def step(k, carry):
            off = pl.multiple_of(k * 8, 8)
            av = a_s[pl.ds(off, 8), :]
            uv = u_s[pl.ds(off, 8), :]
            for d in (1, 2, 4):
                m = row >= d
                a_sh = jnp.where(m, pltpu.roll(av, d, 0), 1.0)
                u_sh = jnp.where(m, pltpu.roll(uv, d, 0), 0.0)
                uv = uv + av * u_sh
                av = av * a_sh
            hv = uv + av * carry
            h_ref[pl.ds(off, 8), :] = hv
            return jnp.broadcast_to(hv[7:8, :], (8, C))

        hc[...] = lax.fori_loop(0, ts // 8, step, hc[...])
        ge, _ = _gelu(zg_ref[0])
        y_ref[...] = (h_ref[...] * ge).astype(MXU_DTYPE)
        xbuf[pl.ds(0, halo), :] = xbuf[pl.ds(ts, halo), :]

    vec = pl.BlockSpec((1, C), lambda i: (0, 0))
    mat = pl.BlockSpec((C, C), lambda i: (0, 0))
    return _pcall(
        comm, body, name="lru_fwd", grid=(S // ts,),
        in_specs=[pl.BlockSpec((1, ts, C), lambda i: (0, i, 0)), pl.BlockSpec((1, ts, C), lambda i: (1, i, 0)),
                  pl.BlockSpec((taps, C), lambda i: (0, 0)), vec, mat, vec, mat, vec, vec],
        out_specs=[pl.BlockSpec((ts, C), lambda i: (i, 0)), pl.BlockSpec((ts, C), lambda i: (i, 0))],
        out_shape=[jax.ShapeDtypeStruct((S, C), F32), jax.ShapeDtypeStruct((S, C), MXU_DTYPE)],
        scratch_shapes=[pltpu.VMEM((ts + halo, C), F32), pltpu.VMEM((ts, C), F32), pltpu.VMEM((ts, C), F32),
                        pltpu.VMEM((8, C), F32)],
        semantics=("arbitrary",))(z, z, conv_w, conv_b, wa, ba, wx, bx, lam)


def _layer_norm_stats(c1):
    mu = jnp.mean(c1, axis=-1, keepdims=True)
    xc = c1 - mu
    rstd = lax.rsqrt(jnp.mean(xc * xc, axis=-1, keepdims=True) + EPS)
    return rstd, xc * rstd


def _conf_fwd(z, conv_w, conv_b, ln_g, ln_b, comm=None):
    _, S, C = z.shape
    ts = _tile(S, 512)
    taps = conv_w.shape[0]
    halo = 32

    def body(za_ref, zb_ref, cw_ref, cb_ref, g_ref, b_ref, c1_ref, c3_ref, cbuf, shifted):
        i = pl.program_id(0)

        @pl.when(i == 0)
        def _():
            cbuf[pl.ds(0, halo), :] = jnp.zeros((halo, C), F32)

        cbuf[pl.ds(halo, ts), :] = za_ref[0] * _sigmoid(zb_ref[0])
        _shift_copies(shifted, cbuf, ts + halo - 8, up=False)
        c1 = _causal_taps8(shifted, halo, cw_ref, taps, ts) + cb_ref[...]
        c1_ref[...] = c1
        _, xhat = _layer_norm_stats(c1)
        c2 = xhat * g_ref[...] + b_ref[...]
        c3_ref[...] = (c2 * _sigmoid(c2)).astype(MXU_DTYPE)
        cbuf[pl.ds(0, halo), :] = cbuf[pl.ds(ts, halo), :]

    vec = pl.BlockSpec((1, C), lambda i: (0, 0))
    return _pcall(
        comm, body, name="conf_fwd", grid=(S // ts,),
        in_specs=[pl.BlockSpec((1, ts, C), lambda i: (2, i, 0)), pl.BlockSpec((1, ts, C), lambda i: (3, i, 0)),
                  pl.BlockSpec((taps, C), lambda i: (0, 0)), vec, vec, vec],
        out_specs=[pl.BlockSpec((ts, C), lambda i: (i, 0)), pl.BlockSpec((ts, C), lambda i: (i, 0))],
        out_shape=[jax.ShapeDtypeStruct((S, C), F32), jax.ShapeDtypeStruct((S, C), MXU_DTYPE)],
        scratch_shapes=[pltpu.VMEM((ts + halo, C), F32), pltpu.VMEM((8, ts + halo - 8, C), F32)],
        semantics=("arbitrary",))(z, z, conv_w, conv_b, ln_g, ln_b)


def _fwd_out_q(x, y_lru, c3, w_out, g_xa, w_q, comm=None):
    S, D = x.shape
    C = y_lru.shape[1]
    ts = _tile(S, 1024)

    halves = 2 if ts % 32 == 0 else 1
    hr = ts // halves

    def body(x_ref, yl_ref, c3_ref, wo_ref, g_ref, wq_ref, x1_ref, h2_ref, q_ref):
        def mixed(k):
            rows = pl.ds(k * hr, hr)
            return (jnp.dot(yl_ref[rows, :], wo_ref[0], preferred_element_type=F32)
                    + jnp.dot(c3_ref[rows, :], wo_ref[1], preferred_element_type=F32))

        y_next = mixed(0)
        for k in range(halves):
            rows, y = pl.ds(k * hr, hr), y_next
            if k + 1 < halves:
                y_next = mixed(k + 1)
            x1 = x_ref[rows, :] + y
            x1_ref[rows, :] = x1
            _, xhat = _rms(x1)
            h2 = (xhat * g_ref[...]).astype(MXU_DTYPE)
            h2_ref[rows, :] = h2
            q_ref[rows, :] = jnp.dot(h2, wq_ref[...], preferred_element_type=F32).astype(MXU_DTYPE)

    row = lambda w: pl.BlockSpec((ts, w), lambda i: (i, 0))
    return _pcall(
        comm, body, name="fwd_out_q", grid=(S // ts,),
        in_specs=[row(D), row(C), row(C), pl.BlockSpec((2, C, D), lambda i: (0, 0, 0)),
                  pl.BlockSpec((1, D), lambda i: (0, 0)), pl.BlockSpec((D, D), lambda i: (0, 0))],
        out_specs=[row(D), row(D), row(D)],
        out_shape=[jax.ShapeDtypeStruct((S, D), F32), jax.ShapeDtypeStruct((S, D), MXU_DTYPE),
                   jax.ShapeDtypeStruct((S, D), MXU_DTYPE)],
        semantics=("parallel",))(x, y_lru, c3, w_out, g_xa, w_q)


def _kv_fwd(mem, g, w_kv):
    M, D = mem.shape
    nb, _, C = w_kv.shape

    def body(mem_ref, g_ref, w_ref, m_ref, kv_ref):
        _, xhat = _rms(mem_ref[...])
        m = (xhat * g_ref[...]).astype(MXU_DTYPE)
        m_ref[...] = m
        for j in range(nb):
            kv_ref[:, pl.ds(j * C, C)] = jnp.dot(m, w_ref[j], preferred_element_type=F32).astype(MXU_DTYPE)

    return pl.pallas_call(
        body, name="kv_fwd", grid=(1,),
        in_specs=[pl.BlockSpec((M, D), lambda i: (0, 0)), pl.BlockSpec((1, D), lambda i: (0, 0)),
                  pl.BlockSpec((nb, D, C), lambda i: (0, 0, 0))],
        out_specs=[pl.BlockSpec((M, D), lambda i: (0, 0)), pl.BlockSpec((M, nb * C), lambda i: (0, 0))],
        out_shape=[jax.ShapeDtypeStruct((M, D), MXU_DTYPE), jax.ShapeDtypeStruct((M, nb * C), MXU_DTYPE)],
        compiler_params=_params("arbitrary"))(mem, g, w_kv)


def _softmax_rows(s):
    e = jnp.exp(s - jnp.max(s, axis=-1, keepdims=True))
    return e / jnp.sum(e, axis=-1, keepdims=True)


def _attn_fwd(q, kv, x1, w_o, g_ffn, comm=None):
    S, D = x1.shape
    M = kv.shape[0]
    hd = D // XA_HEADS
    scale = hd ** -0.5
    ts = _tile(S, 1024)

    def body(q_ref, kv_ref, x1_ref, wo_ref, g_ref, o_ref, x2_ref, h3_ref):
        def scores(h):
            cols = pl.ds(h * hd, hd)
            return _mm_nt(q_ref[:, cols], kv_ref[:, cols]) * scale

        s_next = scores(0)
        for h in range(XA_HEADS):
            s = s_next
            if h + 1 < XA_HEADS:
                s_next = scores(h + 1)
            p = _softmax_rows(s)
            o_ref[:, pl.ds(h * hd, hd)] = _mm(p, kv_ref[:, pl.ds(D + h * hd, hd)]).astype(MXU_DTYPE)
        x2 = x1_ref[...] + jnp.dot(o_ref[...], wo_ref[...], preferred_element_type=F32)
        x2_ref[...] = x2
        _, xhat = _rms(x2)
        h3_ref[...] = (xhat * g_ref[...]).astype(MXU_DTYPE)

    row = pl.BlockSpec((ts, D), lambda i: (i, 0))
    return _pcall(
        comm, body, name="attn_fwd", grid=(S // ts,),
        in_specs=[row, pl.BlockSpec((M, 2 * D), lambda i: (0, 0)), row, pl.BlockSpec((D, D), lambda i: (0, 0)),
                  pl.BlockSpec((1, D), lambda i: (0, 0))],
        out_specs=[row, row, row],
        out_shape=[jax.ShapeDtypeStruct((S, D), MXU_DTYPE), jax.ShapeDtypeStruct((S, D), F32),
                   jax.ShapeDtypeStruct((S, D), MXU_DTYPE)],
        semantics=("parallel",))(q, kv, x1, w_o, g_ffn)


def _ffn_fwd(h3, w_up, conv_w, conv_b, w_down, x2, g_final, target, comm=None):
    S, D = h3.shape
    nb, _, CW = w_up.shape
    half = nb // 2
    cb = 1536
    per = CW // cb
    J = half * per
    ts = _tile(S, 256)
    taps = conv_w.shape[0]
    halo = 8

    def body(h_ref, wup_ref, cw_ref, cb_ref, wd_ref, x2_ref, gf_ref, t_ref,
             gu_ref, act_ref, dx3_ref, loss_ref, dgf_ref, gbuf):
        i = pl.program_id(0)

        @pl.when(i == 0)
        def _():
            for ref in (loss_ref, dgf_ref, gbuf):
                ref[...] = jnp.zeros_like(ref)

        hv = h_ref[...]
        x3 = x2_ref[...]
        def up(j):
            b, cols = j // per, pl.ds((j % per) * cb, cb)
            return (jnp.dot(hv, wup_ref[b, :, cols], preferred_element_type=F32),
                    jnp.dot(hv, wup_ref[half + b, :, cols], preferred_element_type=F32))

        ahead = up(0)
        for j in range(J):
            b, cols, wcols = j // per, pl.ds((j % per) * cb, cb), pl.ds(j * cb, cb)
            g, u = ahead
            if j + 1 < J:
                ahead = up(j + 1)
            gu_ref[0, b, :, cols] = g
            gu_ref[1, b, :, cols] = u
            gbuf[j, pl.ds(halo, ts), :] = g
            gc = _causal_taps(gbuf.at[j], halo, cw_ref, taps, ts, wcols=wcols) + cb_ref[:, wcols]
            gbuf[j, pl.ds(0, halo), :] = gbuf[j, pl.ds(ts, halo), :]
            ge, _ = _gelu(gc)
            act = (ge * u).astype(MXU_DTYPE)
            act_ref[j] = act
            x3 = x3 + jnp.dot(act, wd_ref[j], preferred_element_type=F32)
        rinv, xhat = _rms(x3)
        gf = gf_ref[...]
        diff = xhat * gf - t_ref[...]
        loss_ref[...] += _colsum(diff * diff) * (0.5 / D)
        dy = diff * (1.0 / D)
        dgf_ref[...] += _colsum(dy * xhat)
        dx3_ref[...] = _rms_bwd(rinv, xhat, dy * gf)

    row = pl.BlockSpec((ts, D), lambda i: (i, 0))
    vecd = pl.BlockSpec((1, D), lambda i: (0, 0))
    once = pl.Buffered(1)
    sds = jax.ShapeDtypeStruct
    res = _pcall(
        comm, body, name="ffn_fwd", grid=(S // ts,),
        in_specs=[row, pl.BlockSpec((nb, D, CW), lambda i: (0, 0, 0), pipeline_mode=once),
                  pl.BlockSpec((taps, half * CW), lambda i: (0, 0)), pl.BlockSpec((1, half * CW), lambda i: (0, 0)),
                  pl.BlockSpec((J, cb, D), lambda i: (0, 0, 0), pipeline_mode=once), row, vecd, row],
        out_specs=[pl.BlockSpec((2, half, ts, CW), lambda i: (0, 0, i, 0)),
                   pl.BlockSpec((J, ts, cb), lambda i: (0, i, 0)), row, vecd, vecd],
        out_shape=[sds((2, half, S, CW), F32), sds((J, S, cb), MXU_DTYPE), sds((S, D), F32),
                   sds((1, D), F32), sds((1, D), F32)],
        scratch_shapes=[pltpu.VMEM((J, ts + halo, cb), F32)],
        semantics=("arbitrary",))(h3, w_up, conv_w, conv_b, w_down.reshape(J, cb, D), x2, g_final, target)
    outs = res if comm is None else res[0]
    outs = [outs[0].reshape(nb, S, CW)] + list(outs[1:])
    return outs if comm is None else (outs, res[1])


def _ffn_bwd(dx3, w_down, w_up, gu, x2, g_ffn, conv_w, conv_b, comm=None):
    nb, S, CW = gu.shape
    half = nb // 2
    D = dx3.shape[1]
    cb = 1536
    per = CW // cb
    J = half * per
    ts = _tile(S, 256)
    n = S // ts
    taps = conv_w.shape[0]
    halo = 8
    hb = ts // halo

    def body(dx_ref, x2_ref, gf_ref, wd_ref, wup_ref, gu_ref, gh_ref, cw_ref, cb_ref,
             dgu_ref, dx2_ref, dgf_ref, dcw_ref, dcb_ref, gbuf, dbuf):
        i = pl.program_id(0)
        r = n - 1 - i

        @pl.when(i == 0)
        def _():
            for ref in (dgf_ref, dcw_ref, dcb_ref, dbuf):
                ref[...] = jnp.zeros_like(ref)

        dx3v = dx_ref[...]
        dxb = dx3v.astype(MXU_DTYPE)
        dacts = [_mm_nt(dxb, wd_ref[j]) for j in range(J)]
        dh = None
        for j in range(J):
            b, cols, wcols = j // per, pl.ds((j % per) * cb, cb), pl.ds(j * cb, cb)
            dact = dacts[j]
            gbuf[pl.ds(0, halo), :] = jnp.where(r > 0, gh_ref[0, b, :, cols], 0.0)
            gbuf[pl.ds(halo, ts), :] = gu_ref[0, b, :, cols]
            gs = _windows(gbuf, halo, taps, ts)
            gc = _causal_from(gs, cw_ref, wcols) + cb_ref[:, wcols]
            ge, dge = _gelu(gc)
            dub = (dact * ge).astype(MXU_DTYPE)
            dgc = dact * gu_ref[1, b, :, cols] * dge
            dcb_ref[:, wcols] += _colsum(dgc)
            dbuf[j, pl.ds(0, ts), :] = dgc
            _tap_grads_from(dcw_ref, dgc, gs, wcols)
            dgb = _anticausal_taps(dbuf.at[j], cw_ref, taps, ts, wcols=wcols).astype(MXU_DTYPE)
            dbuf[j, pl.ds(ts, halo), :] = dbuf[j, pl.ds(0, halo), :]
            dgu_ref[0, b, :, cols] = dgb
            dgu_ref[1, b, :, cols] = dub
            part = _mm_nt(dgb, wup_ref[b, :, cols]) + _mm_nt(dub, wup_ref[half + b, :, cols])
            dh = part if dh is None else dh + part
        rinv, xhat = _rms(x2_ref[...])
        dgf_ref[...] += _colsum(dh * xhat)
        dx2_ref[...] = dx3v + _rms_bwd(rinv, xhat, dh * gf_ref[...])

    gu2 = gu.reshape(2, half, S, CW)
    row = pl.BlockSpec((ts, D), lambda i: (n - 1 - i, 0))
    vecd = pl.BlockSpec((1, D), lambda i: (0, 0))
    pair = pl.BlockSpec((2, half, ts, CW), lambda i: (0, 0, n - 1 - i, 0))
    g_prev = pl.BlockSpec((1, half, halo, CW), lambda i: (0, 0, jnp.maximum((n - 1 - i) * hb - 1, 0), 0))
    tapw = pl.BlockSpec((taps, half * CW), lambda i: (0, 0))
    vec = pl.BlockSpec((1, half * CW), lambda i: (0, 0))
    once = pl.Buffered(1)
    sds = jax.ShapeDtypeStruct
    res = _pcall(
        comm, body, name="ffn_bwd", grid=(n,),
        in_specs=[row, row, vecd, pl.BlockSpec((J, cb, D), lambda i: (0, 0, 0), pipeline_mode=once),
                  pl.BlockSpec((nb, D, CW), lambda i: (0, 0, 0), pipeline_mode=once), pair, g_prev, tapw, vec],
        out_specs=[pair, row, vecd, tapw, vec],
        out_shape=[sds((2, half, S, CW), MXU_DTYPE), sds((S, D), F32), sds((1, D), F32),
                   sds((taps, half * CW), F32), sds((1, half * CW), F32)],
        scratch_shapes=[pltpu.VMEM((ts + halo, cb), F32), pltpu.VMEM((J, ts + halo, cb), F32)],
        semantics=("arbitrary",))(dx3, x2, g_ffn, w_down.reshape(J, cb, D), w_up, gu2, gu2, conv_w, conv_b)
    outs = res if comm is None else res[0]
    outs = [outs[0].reshape(nb, S, CW)] + list(outs[1:])
    return outs if comm is None else (outs, res[1])


def _attn_bwd(dx2, w_o, q, kv, x1, g_xa, w_q, comm=None):
    S, D = x1.shape
    M = kv.shape[0]
    hd = D // XA_HEADS
    scale = hd ** -0.5
    ts = _tile(S, 1024)

    def body(dx2_ref, wo_ref, q_ref, kv_ref, x1_ref, g_ref, wq_ref, dq_ref, dx1_ref, dkv_ref, dg_ref):
        i = pl.program_id(0)

        @pl.when(i == 0)
        def _():
            dkv_ref[...] = jnp.zeros_like(dkv_ref)
            dg_ref[...] = jnp.zeros_like(dg_ref)

        dx2 = dx2_ref[...]
        do = _mm_nt(dx2, wo_ref[...]).astype(MXU_DTYPE)
        def scores(h):
            cols = pl.ds(h * hd, hd)
            doh = do[:, h * hd:(h + 1) * hd]
            return (_mm_nt(q_ref[:, cols], kv_ref[:, cols]) * scale,
                    _mm_nt(doh, kv_ref[:, pl.ds(D + h * hd, hd)]), doh)

        ahead = scores(0)
        for h in range(XA_HEADS):
            cols = pl.ds(h * hd, hd)
            vcols = pl.ds(D + h * hd, hd)
            s, dp, doh = ahead
            if h + 1 < XA_HEADS:
                ahead = scores(h + 1)
            p = _softmax_rows(s)
            ds = (p * (dp - jnp.sum(dp * p, axis=-1, keepdims=True)) * scale).astype(MXU_DTYPE)
            dkv_ref[:, vcols] += _mm_tn(p, doh)
            dq_ref[:, cols] = _mm(ds, kv_ref[:, cols]).astype(MXU_DTYPE)
            dkv_ref[:, cols] += _mm_tn(ds, q_ref[:, cols])
        dh2 = _mm_nt(dq_ref[...], wq_ref[...])
        rinv, xhat = _rms(x1_ref[...])
        dg_ref[...] += _colsum(dh2 * xhat)
        dx1_ref[...] = dx2 + _rms_bwd(rinv, xhat, dh2 * g_ref[...])

    row = pl.BlockSpec((ts, D), lambda i: (i, 0))
    mat = pl.BlockSpec((D, D), lambda i: (0, 0))
    vecd = pl.BlockSpec((1, D), lambda i: (0, 0))
    kvs = pl.BlockSpec((M, 2 * D), lambda i: (0, 0))
    return _pcall(
        comm, body, name="attn_bwd", grid=(S // ts,),
        in_specs=[row, mat, row, kvs, row, vecd, mat],
        out_specs=[row, row, kvs, vecd],
        out_shape=[jax.ShapeDtypeStruct((S, D), MXU_DTYPE), jax.ShapeDtypeStruct((S, D), F32),
                   jax.ShapeDtypeStruct((M, 2 * D), F32), jax.ShapeDtypeStruct((1, D), F32)],
        semantics=("arbitrary",))(dx2, w_o, q, kv, x1, g_xa, w_q)


def _kv_bwd(dkv, w_kv, mem, g, m):
    M, D = mem.shape
    nb, _, C = w_kv.shape

    def body(dkv_ref, w_ref, mem_ref, m_ref, dw_ref, dg_ref):
        dm = jnp.zeros((M, D), F32)
        for j in range(nb):
            dj = dkv_ref[:, pl.ds(j * C, C)].astype(MXU_DTYPE)
            dw_ref[j] = _mm_tn(m_ref[...], dj).astype(dw_ref.dtype)
            dm = dm + _mm_nt(dj, w_ref[j])
        _, xhat = _rms(mem_ref[...])
        dg_ref[...] = _colsum(dm * xhat)

    full = lambda *s: pl.BlockSpec(s, lambda i: (0,) * len(s))
    return pl.pallas_call(
        body, name="kv_bwd", grid=(1,),
        in_specs=[full(M, nb * C), full(nb, D, C), full(M, D), full(M, D)],
        out_specs=[full(nb, D, C), full(1, D)],
        out_shape=[jax.ShapeDtypeStruct((nb, D, C), WIRE_DTYPE), jax.ShapeDtypeStruct((1, D), F32)],
        compiler_params=_params("arbitrary"))(dkv, w_kv, mem, m)


def _conf_bwd(dx1, w_out_c, z, c1, conv_w, ln_g, ln_b, comm=None):
    _, S, C = z.shape
    D = dx1.shape[1]
    ts = _tile(S, 512)
    n = S // ts
    taps = conv_w.shape[0]
    halo = 32
    hb = ts // halo

    def body(dx_ref, wo_ref, za_ref, zb_ref, zah_ref, zbh_ref, c1_ref, cw_ref, g_ref, b_ref,
             dz_ref, dcw_ref, dcb_ref, dlg_ref, dlb_ref, c0buf, dbuf, shifted):
        i = pl.program_id(0)
        r = n - 1 - i

        @pl.when(i == 0)
        def _():
            for ref in (dcw_ref, dcb_ref, dlg_ref, dlb_ref):
                ref[...] = jnp.zeros_like(ref)
            dbuf[pl.ds(ts, halo), :] = jnp.zeros((halo, C), F32)

        za = za_ref[0]
        sb = _sigmoid(zb_ref[0])
        c0buf[pl.ds(0, halo), :] = jnp.where(r > 0, zah_ref[0] * _sigmoid(zbh_ref[0]), 0.0)
        c0buf[pl.ds(halo, ts), :] = za * sb
        dc3 = _mm_nt(dx_ref[...], wo_ref[...])
        rstd, xhat = _layer_norm_stats(c1_ref[...])
        g = g_ref[...]
        c2 = xhat * g + b_ref[...]
        sg = _sigmoid(c2)
        dc2 = dc3 * sg * (1.0 + c2 * (1.0 - sg))
        dlg_ref[...] += _colsum(dc2 * xhat)
        dlb_ref[...] += _colsum(dc2)
        dxh = dc2 * g
        dc1 = rstd * (dxh - jnp.mean(dxh, axis=-1, keepdims=True)
                      - xhat * jnp.mean(dxh * xhat, axis=-1, keepdims=True))
        dcb_ref[...] += _colsum(dc1)
        dbuf[pl.ds(0, ts), :] = dc1
        _shift_copies(shifted, c0buf, ts + halo - 8, up=False)
        _tap_grads8(dcw_ref, dc1, shifted, halo, taps, ts)
        _shift_copies(shifted, dbuf, ts + halo - 8, up=True)
        dc0 = _anticausal_taps8(shifted, cw_ref, taps, ts)
        dz_ref[0] = (dc0 * sb).astype(MXU_DTYPE)
        dz_ref[1] = (dc0 * za * sb * (1.0 - sb)).astype(MXU_DTYPE)
        dbuf[pl.ds(ts, halo), :] = dbuf[pl.ds(0, halo), :]

    vec = pl.BlockSpec((1, C), lambda i: (0, 0))
    tapw = pl.BlockSpec((taps, C), lambda i: (0, 0))
    tile = lambda b: pl.BlockSpec((1, ts, C), lambda i: (b, n - 1 - i, 0))
    prev = lambda b: pl.BlockSpec((1, halo, C), lambda i: (b, jnp.maximum((n - 1 - i) * hb - 1, 0), 0))
    return _pcall(
        comm, body, name="conf_bwd", grid=(n,),
        in_specs=[pl.BlockSpec((ts, D), lambda i: (n - 1 - i, 0)), pl.BlockSpec((C, D), lambda i: (0, 0)),
                  tile(2), tile(3), prev(2), prev(3), pl.BlockSpec((ts, C), lambda i: (n - 1 - i, 0)),
                  tapw, vec, vec],
        out_specs=[pl.BlockSpec((2, ts, C), lambda i: (1, n - 1 - i, 0)), tapw, vec, vec, vec],
        out_shape=[jax.ShapeDtypeStruct((4, S, C), MXU_DTYPE), jax.ShapeDtypeStruct((taps, C), F32),
                   jax.ShapeDtypeStruct((1, C), F32), jax.ShapeDtypeStruct((1, C), F32),
                   jax.ShapeDtypeStruct((1, C), F32)],
        scratch_shapes=[pltpu.VMEM((ts + halo, C), F32), pltpu.VMEM((ts + halo, C), F32),
                        pltpu.VMEM((8, ts + halo - 8, C), F32)],
        semantics=("arbitrary",))(dx1, w_out_c, z, z, z, z, c1, conv_w, ln_g, ln_b)


def _lru_bwd(dx1, w_out_l, z, h, conv_w, conv_b, wa, ba, wx, bx, lam, dz, comm=None):
    _, S, C = z.shape
    D = dx1.shape[1]
    ts = _tile(S, 512)
    n = S // ts
    taps = conv_w.shape[0]
    halo = 8
    hb = ts // halo

    def body(dx_ref, wo_ref, zx_ref, zxh_ref, zg_ref, h_ref, hh_ref, cw_ref, cb_ref, wa_ref, ba_ref,
             wx_ref, bx_ref, lam_ref, dz_in,
             dz_ref, dwa_ref, dwx_ref, dba_ref, dbx_ref, dlam_ref, dcw_ref, dcb_ref,
             xbuf, hbuf, a_s, w_s, dh_s, g_s, dbuf, pc):
        i = pl.program_id(0)
        r = n - 1 - i

        @pl.when(i == 0)
        def _():
            for ref in (dwa_ref, dwx_ref, dba_ref, dbx_ref, dlam_ref, dcw_ref, dcb_ref, pc):
                ref[...] = jnp.zeros_like(ref)
            dbuf[pl.ds(ts, halo), :] = jnp.zeros((halo, C), F32)

        xbuf[pl.ds(0, halo), :] = jnp.where(r > 0, zxh_ref[0], 0.0)
        xbuf[pl.ds(halo, ts), :] = zx_ref[0]
        hbuf[pl.ds(0, halo), :] = jnp.where(r > 0, hh_ref[...], 0.0)
        hbuf[pl.ds(halo, ts), :] = h_ref[...]
        xs = _windows(xbuf, halo, taps, ts)
        xc = _causal_from(xs, cw_ref) + cb_ref[...]
        lam_v = lam_ref[...]
        sp = _softplus_neg(lam_v)
        rg, ig, a, mult, inv_mult = _lru_gates(xc, wa_ref, ba_ref, wx_ref, bx_ref, sp)

        dy = _mm_nt(dx_ref[...], wo_ref[...])
        ge, dge = _gelu(zg_ref[0])
        dh = dy * ge
        dz_ref[1] = (dy * h_ref[...] * dge).astype(MXU_DTYPE)
        a_s[...] = a
        w_s[...] = a * dh
        dh_s[...] = dh
        row = lax.broadcasted_iota(jnp.int32, (8, C), 0)

        def step(kk, carry):
            off = pl.multiple_of((ts // 8 - 1 - kk) * 8, 8)
            av = a_s[pl.ds(off, 8), :]
            wv = w_s[pl.ds(off, 8), :]
            for d in (1, 2, 4):
                m = row < 8 - d
                a_sh = jnp.where(m, pltpu.roll(av, 8 - d, 0), 1.0)
                w_sh = jnp.where(m, pltpu.roll(wv, 8 - d, 0), 0.0)
                wv = wv + av * w_sh
                av = av * a_sh
            pv = wv + av * carry
            g_s[pl.ds(off, 8), :] = dh_s[pl.ds(off, 8), :] + jnp.where(row < 7, pltpu.roll(pv, 7, 0), carry)
            return jnp.broadcast_to(pv[0:1, :], (8, C))

        pc[...] = lax.fori_loop(0, ts // 8, step, pc[...])
        gt = g_s[...]
        da = gt * hbuf[pl.ds(halo - 1, ts), :]
        gm = gt * mult
        dlog_a = da * a - (gt * ig * xc) * (a * a) * inv_mult
        dlam_ref[...] += _colsum(dlog_a * rg) * (RG_C / (1.0 + jnp.exp(lam_v)))
        dpa = (dlog_a * (-RG_C * sp)) * rg * (1.0 - rg)
        dpx = (gm * xc) * ig * (1.0 - ig)
        dba_ref[...] += _colsum(dpa)
        dbx_ref[...] += _colsum(dpx)
        xb = xc.astype(MXU_DTYPE)
        dpab, dpxb = dpa.astype(MXU_DTYPE), dpx.astype(MXU_DTYPE)
        dwa_ref[...] += _mm_tn(xb, dpab)
        dwx_ref[...] += _mm_tn(xb, dpxb)
        dxc = gm * ig + _mm_nt(dpab, wa_ref[...]) + _mm_nt(dpxb, wx_ref[...])
        dcb_ref[...] += _colsum(dxc)
        dbuf[pl.ds(0, ts), :] = dxc
        _tap_grads_from(dcw_ref, dxc, xs)
        dz_ref[0] = _anticausal_taps(dbuf, cw_ref, taps, ts).astype(MXU_DTYPE)
        dbuf[pl.ds(ts, halo), :] = dbuf[pl.ds(0, halo), :]

    vec = pl.BlockSpec((1, C), lambda i: (0, 0))
    mat = pl.BlockSpec((C, C), lambda i: (0, 0))
    tapw = pl.BlockSpec((taps, C), lambda i: (0, 0))
    prev_rows = lambda i: jnp.maximum((n - 1 - i) * hb - 1, 0)
    sds = jax.ShapeDtypeStruct
    return _pcall(
        comm, body, name="lru_bwd", grid=(n,),
        in_specs=[pl.BlockSpec((ts, D), lambda i: (n - 1 - i, 0)), pl.BlockSpec((C, D), lambda i: (0, 0)),
                  pl.BlockSpec((1, ts, C), lambda i: (0, n - 1 - i, 0)),
                  pl.BlockSpec((1, halo, C), lambda i: (0, prev_rows(i), 0)),
                  pl.BlockSpec((1, ts, C), lambda i: (1, n - 1 - i, 0)),
                  pl.BlockSpec((ts, C), lambda i: (n - 1 - i, 0)),
                  pl.BlockSpec((halo, C), lambda i: (prev_rows(i), 0)),
                  tapw, vec, mat, vec, mat, vec, vec, ANY],
        out_specs=[pl.BlockSpec((2, ts, C), lambda i: (0, n - 1 - i, 0)), mat, mat, vec, vec, vec, tapw, vec],
        out_shape=[sds(dz.shape, MXU_DTYPE), sds((C, C), F32), sds((C, C), F32), sds((1, C), F32),
                   sds((1, C), F32), sds((1, C), F32), sds((taps, C), F32), sds((1, C), F32)],
        scratch_shapes=[pltpu.VMEM((ts + halo, C), F32), pltpu.VMEM((ts + halo, C), F32)]
        + [pltpu.VMEM((ts, C), F32)] * 4 + [pltpu.VMEM((ts + halo, C), F32), pltpu.VMEM((8, C), F32)],
        aliases={14: 0},
        semantics=("arbitrary",))(dx1, w_out_l, z, z, z, h, h, conv_w, conv_b, wa, ba, wx, bx, lam, dz)


def _bwd_in(dz, w_in, x, g, dx1):
    S, D = x.shape
    nb, _, C = w_in.shape
    ts = _tile(S, 512)
    halves = 2 if ts % 32 == 0 else 1
    hr = ts // halves

    def body(dz_ref, w_ref, x_ref, g_ref, dx1_ref, dx_ref, dg_ref):
        i = pl.program_id(0)

        @pl.when(i == 0)
        def _():
            dg_ref[...] = jnp.zeros_like(dg_ref)

        def grad_h(k):
            rows = pl.ds(k * hr, hr)
            dh = _mm_nt(dz_ref[0, rows, :], w_ref[0])
            for j in range(1, nb):
                dh = dh + _mm_nt(dz_ref[j, rows, :], w_ref[j])
            return dh

        ahead = grad_h(0)
        for k in range(halves):
            rows, dh = pl.ds(k * hr, hr), ahead
            if k + 1 < halves:
                ahead = grad_h(k + 1)
            rinv, xhat = _rms(x_ref[rows, :])
            dg_ref[...] += _colsum(dh * xhat)
            dx_ref[rows, :] = dx1_ref[rows, :] + _rms_bwd(rinv, xhat, dh * g_ref[...])

    row = pl.BlockSpec((ts, D), lambda i: (i, 0))
    vecd = pl.BlockSpec((1, D), lambda i: (0, 0))
    return pl.pallas_call(
        body, name="bwd_in", grid=(S // ts,),
        in_specs=[pl.BlockSpec((nb, ts, C), lambda i: (0, i, 0)), pl.BlockSpec((nb, D, C), lambda i: (0, 0, 0)),
                  row, vecd, row],
        out_specs=[row, vecd],
        out_shape=[jax.ShapeDtypeStruct((S, D), F32), jax.ShapeDtypeStruct((1, D), F32)],
        compiler_params=_params("arbitrary"))(dz, w_in, x, g, dx1)


def _wgrad(a, b, name, comm=None):
    na, S, K = a.shape
    nb, _, N = b.shape
    nj = max(na, nb)
    assert min(na, nb) == 1
    ts = _tile(S, 1024)
    ns = S // ts
    grp = max(g for g in range(1, nj + 1) if nj % g == 0 and g * K * N * 4 <= WGRAD_ACC_BYTES)
    ga, gb = (grp if na > 1 else 1), (grp if nb > 1 else 1)

    def body(a_ref, b_ref, o_ref, acc):
        s = pl.program_id(1)

        @pl.when(s == 0)
        def _():
            acc[...] = jnp.zeros_like(acc)

        for k in range(grp):
            acc[k] += _mm_tn(a_ref[k if na > 1 else 0], b_ref[k if nb > 1 else 0])

        @pl.when(s == ns - 1)
        def _():
            o_ref[...] = acc[...].astype(o_ref.dtype)

    res = _pcall(
        comm, body, name=name, grid=(nj // grp, ns),
        in_specs=[pl.BlockSpec((ga, ts, K), (lambda j, s: (j, s, 0)) if na > 1 else (lambda j, s: (0, s, 0))),
                  pl.BlockSpec((gb, ts, N), (lambda j, s: (j, s, 0)) if nb > 1 else (lambda j, s: (0, s, 0)))],
        out_specs=pl.BlockSpec((grp, K, N), lambda j, s: (j, 0, 0)),
        out_shape=jax.ShapeDtypeStruct((nj, K, N), WIRE_DTYPE),
        scratch_shapes=[pltpu.VMEM((grp, K, N), F32)],
        semantics=("parallel", "arbitrary"))(a, b)
    return res[0] if comm is None else (res[0][0], res[1])


def _place():
    x, y, c = lax.axis_index("x"), lax.axis_index("y"), lax.axis_index("c")
    other_chips = [(1 - x, y), (x, 1 - y), (1 - x, 1 - y)]
    return x, y, c, other_chips


def _gather_weights(shards):
    nt = len(shards)

    def body(*refs):
        src, dst = refs[:nt], refs[nt:2 * nt]
        ici_send, ici_recv, d2d_send, d2d_recv, own_send, own_recv = refs[2 * nt:]
        x, y, c, chips = _place()
        mine = 2 * x + y

        def half(t, pc):
            hr = src[t].shape[0] // 2
            return pl.ds(pc * hr, hr)

        def own(t):
            return pltpu.make_async_remote_copy(
                src_ref=src[t], dst_ref=dst[t].at[mine], send_sem=own_send.at[t], recv_sem=own_recv.at[t],
                device_id=(x, y, 1 - c), device_id_type=MESH)

        def ici(t, k, block, to):
            cx, cy = block
            ref = dst[t].at[2 * cx + cy, half(t, c)]
            return pltpu.make_async_remote_copy(
                src_ref=src[t].at[half(t, c)] if to is not None else ref, dst_ref=ref,
                send_sem=ici_send.at[t, k], recv_sem=ici_recv.at[t, k],
                device_id=(*to, c) if to is not None else (x, y, c), device_id_type=MESH)

        def d2d(t, k, block, pc):
            cx, cy = block
            ref = dst[t].at[2 * cx + cy, half(t, pc)]
            return pltpu.make_async_remote_copy(
                src_ref=ref, dst_ref=ref, send_sem=d2d_send.at[t, k], recv_sem=d2d_recv.at[t, k],
                device_id=(x, y, 1 - c), device_id_type=MESH)

        sends = [ici(t, k, (x, y), chip) for t in range(nt) for k, chip in enumerate(chips)]
        sends += [own(t) for t in range(nt)]
        for cp in sends:
            cp.start()
        passed = []
        for t in range(nt):
            for k, chip in enumerate(chips):
                ici(t, k, chip, None).wait_recv()
                fw = d2d(t, k, chip, c)
                fw.start()
                passed.append(fw)
        for t in range(nt):
            own(t).wait_recv()
            for k, chip in enumerate(chips):
                d2d(t, k, chip, 1 - c).wait_recv()
        for cp in sends + passed:
            cp.wait_send()

    return pl.pallas_call(
        body, name="gather_weights",
        in_specs=[ANY] * nt, out_specs=[ANY] * nt,
        out_shape=[jax.ShapeDtypeStruct((N_CHIPS,) + s.shape, s.dtype) for s in shards],
        scratch_shapes=[pltpu.SemaphoreType.DMA((nt, 3))] * 4 + [pltpu.SemaphoreType.DMA((nt,))] * 2,
        compiler_params=pltpu.CompilerParams(has_side_effects=True))(*shards)


def _gather_over_ici(shards):
    nt = len(shards)

    def copies(src, dst, scr, arriving):
        ici_send, ici_recv, own_send, own_recv = scr
        x, y, c, chips = _place()
        out = []
        for t in range(nt):
            hr = src[t].shape[0] // 2
            rows = pl.ds(c * hr, hr)
            for k, (cx, cy) in enumerate(chips):
                block = 2 * cx + cy if arriving else 2 * x + y
                out.append(pltpu.make_async_remote_copy(
                    src_ref=src[t].at[rows], dst_ref=dst[t].at[block, rows],
                    send_sem=ici_send.at[t, k], recv_sem=ici_recv.at[t, k],
                    device_id=(cx, cy, c), device_id_type=MESH))
            out.append(pltpu.make_async_remote_copy(
                src_ref=src[t], dst_ref=dst[t].at[2 * x + y], send_sem=own_send.at[t], recv_sem=own_recv.at[t],
                device_id=(x, y, 1 - c), device_id_type=MESH))
        return out

    def start(src, dst, scr):
        for cp in copies(src, dst, scr, False):
            cp.start()

    def finish(src, dst, scr):
        for cp in copies(src, dst, scr, True):
            cp.wait_recv()
        for cp in copies(src, dst, scr, False):
            cp.wait_send()

    return _Comm(shards, [jax.ShapeDtypeStruct((N_CHIPS,) + s.shape, s.dtype) for s in shards],
                 [pltpu.SemaphoreType.DMA((nt, 3))] * 2 + [pltpu.SemaphoreType.DMA((nt,))] * 2, start, finish)


def _gather_pass_on(bufs):
    nt = len(bufs)

    def passed(dst, scr, t, k, block, pc):
        send, recv = scr
        x, y, c, _ = _place()
        cx, cy = block
        hr = dst[t].shape[1] // 2
        ref = dst[t].at[2 * cx + cy, pl.ds(pc * hr, hr)]
        return pltpu.make_async_remote_copy(src_ref=ref, dst_ref=ref, send_sem=send.at[t, k], recv_sem=recv.at[t, k],
                                            device_id=(x, y, 1 - c), device_id_type=MESH)

    def start(src, dst, scr):
        _, _, c, chips = _place()
        for t in range(nt):
            for k, chip in enumerate(chips):
                passed(dst, scr, t, k, chip, c).start()

    def finish(src, dst, scr):
        _, _, c, chips = _place()
        for t in range(nt):
            for k, chip in enumerate(chips):
                passed(dst, scr, t, k, chip, 1 - c).wait_recv()
        for t in range(nt):
            for k, chip in enumerate(chips):
                passed(dst, scr, t, k, chip, c).wait_send()

    return _Comm(bufs, [jax.ShapeDtypeStruct(b.shape, b.dtype) for b in bufs],
                 [pltpu.SemaphoreType.DMA((nt, 3))] * 2, start, finish, aliases={t: t for t in range(nt)})


def _exchange_halves(grads):
    nt = len(grads)

    def copies(src, dst, scr):
        send, recv = scr
        x, y, c, _ = _place()
        out = []
        for t in range(nt):
            hr = src[t].shape[1] // 2
            out.append(pltpu.make_async_remote_copy(
                src_ref=src[t].at[:, pl.ds((1 - c) * hr, hr)], dst_ref=dst[t],
                send_sem=send.at[t], recv_sem=recv.at[t], device_id=(x, y, 1 - c), device_id_type=MESH))
        return out

    def start(src, dst, scr):
        for cp in copies(src, dst, scr):
            cp.start()

    def finish(src, dst, scr):
        for cp in copies(src, dst, scr):
            cp.wait()

    return _Comm(grads, [jax.ShapeDtypeStruct((g.shape[0], g.shape[1] // 2, g.shape[2]), g.dtype) for g in grads],
                 [pltpu.SemaphoreType.DMA((nt,))] * 2, start, finish)


def _add_halves(grad, other, name):
    nb, R, C = grad.shape
    hr = R // 2
    tr = _tile(hr, 256, 16)
    steps = hr // tr
    c = lax.axis_index("c").astype(jnp.int32).reshape((1,))

    def body(c_ref, a_ref, b_ref, o_ref):
        o_ref[...] = (a_ref[...].astype(F32) + b_ref[...].astype(F32)).astype(o_ref.dtype)

    return pl.pallas_call(
        body, name=name,
        grid_spec=pltpu.PrefetchScalarGridSpec(
            num_scalar_prefetch=1, grid=(nb, steps),
            in_specs=[pl.BlockSpec((1, tr, C), lambda j, i, c_ref: (j, c_ref[0] * steps + i, 0)),
                      pl.BlockSpec((1, tr, C), lambda j, i, c_ref: (j, i, 0))],
            out_specs=pl.BlockSpec((1, tr, C), lambda j, i, c_ref: (j, i, 0))),
        out_shape=jax.ShapeDtypeStruct((nb, hr, C), grad.dtype),
        compiler_params=_params("parallel", "parallel"))(c, grad, other)


def _scatter_chip_sums(parts):
    nt = len(parts)

    def copies(src, dst, scr):
        send, recv = scr
        x, y, c, chips = _place()
        out = []
        for t in range(nt):
            for k, (cx, cy) in enumerate(chips):
                out.append(pltpu.make_async_remote_copy(
                    src_ref=src[t].at[2 * cx + cy], dst_ref=dst[t].at[k],
                    send_sem=send.at[t, k], recv_sem=recv.at[t, k], device_id=(cx, cy, c), device_id_type=MESH))
        return out

    def start(src, dst, scr):
        for cp in copies(src, dst, scr):
            cp.start()

    def finish(src, dst, scr):
        for cp in copies(src, dst, scr):
            cp.wait()

    return _Comm(parts, [jax.ShapeDtypeStruct((3,) + p.shape[1:], p.dtype) for p in parts],
                 [pltpu.SemaphoreType.DMA((nt, 3))] * 2, start, finish)


def _sum_chips(part, recv, name):
    _, hr, C = part.shape
    tr = _tile(hr, 256, 16)
    steps = hr // tr
    where = jnp.stack([2 * lax.axis_index("x") + lax.axis_index("y"), lax.axis_index("c")]).astype(jnp.int32)

    def body(w_ref, a_ref, b_ref, o_ref):
        acc = a_ref[0].astype(F32)
        for k in range(3):
            acc = acc + b_ref[k].astype(F32)
        o_ref[...] = acc

    return pl.pallas_call(
        body, name=name,
        grid_spec=pltpu.PrefetchScalarGridSpec(
            num_scalar_prefetch=1, grid=(steps,),
            in_specs=[pl.BlockSpec((1, tr, C), lambda i, w_ref: (w_ref[0], i, 0)),
                      pl.BlockSpec((3, tr, C), lambda i, w_ref: (0, i, 0))],
            out_specs=pl.BlockSpec((tr, C), lambda i, w_ref: (w_ref[1] * steps + i, 0))),
        out_shape=jax.ShapeDtypeStruct((2 * hr, C), F32),
        compiler_params=_params("parallel"))(where, part, recv)


def _join_halves(bufs):
    nt = len(bufs)

    def swap(dst, scr, t, pc):
        send, recv = scr
        x, y, c, _ = _place()
        hr = dst[t].shape[0] // 2
        rows = dst[t].at[pl.ds(pc * hr, hr)]
        return pltpu.make_async_remote_copy(src_ref=rows, dst_ref=rows, send_sem=send.at[t], recv_sem=recv.at[t],
                                            device_id=(x, y, 1 - c), device_id_type=MESH)

    def start(src, dst, scr):
        c = lax.axis_index("c")
        for t in range(nt):
            swap(dst, scr, t, c).start()

    def finish(src, dst, scr):
        c = lax.axis_index("c")
        for t in range(nt):
            swap(dst, scr, t, 1 - c).wait_recv()
        for t in range(nt):
            swap(dst, scr, t, c).wait_send()

    return _Comm(bufs, [jax.ShapeDtypeStruct(b.shape, b.dtype) for b in bufs],
                 [pltpu.SemaphoreType.DMA((nt,))] * 2, start, finish, aliases={t: t for t in range(nt)})


def _reduce_scatter_in_vmem(g):
    nb, R, C = g.shape
    hr = R // 2

    def run(ins, outs, scr):
        (g_ref,), (out_ref,) = ins, outs
        other, part, got, send, recv = scr
        x, y, c, chips = _place()
        sibling = (x, y, 1 - c)
        my_rows = pl.ds(pl.multiple_of(c * hr, hr), hr)
        their_rows = pl.ds(pl.multiple_of((1 - c) * hr, hr), hr)
        swap = pltpu.make_async_remote_copy(src_ref=g_ref.at[:, their_rows], dst_ref=other, send_sem=send.at[0],
                                            recv_sem=recv.at[0], device_id=sibling, device_id_type=MESH)
        swap.start()
        swap.wait()
        part[...] = (g_ref[:, my_rows, :].astype(F32) + other[...].astype(F32)).astype(part.dtype)
        to_owner = [pltpu.make_async_remote_copy(src_ref=part.at[2 * cx + cy], dst_ref=got.at[k],
                                                 send_sem=send.at[1 + k], recv_sem=recv.at[1 + k],
                                                 device_id=(cx, cy, c), device_id_type=MESH)
                    for k, (cx, cy) in enumerate(chips)]
        for cp in to_owner:
            cp.start()
        for cp in to_owner:
            cp.wait()
        total = part[2 * x + y].astype(F32)
        for k in range(3):
            total = total + got[k].astype(F32)
        out_ref[my_rows, :] = total

        def join(rows):
            return pltpu.make_async_remote_copy(src_ref=out_ref.at[rows], dst_ref=out_ref.at[rows], send_sem=send.at[4],
                                                recv_sem=recv.at[4], device_id=sibling, device_id_type=MESH)

        join(my_rows).start()
        join(their_rows).wait_recv()
        join(my_rows).wait_send()

    return _Comm([g], [jax.ShapeDtypeStruct((R, C), F32)],
                 [pltpu.VMEM((nb, hr, C), g.dtype), pltpu.VMEM((nb, hr, C), g.dtype), pltpu.VMEM((3, hr, C), g.dtype),
                  pltpu.SemaphoreType.DMA((5,)), pltpu.SemaphoreType.DMA((5,))],
                 run, lambda ins, outs, scr: None, in_specs=[WHOLE_VMEM], out_specs=[WHOLE_VMEM])


def _all_reduce_rows(buf, loss_row=None):
    R, L = buf.shape

    def copies(in_ref, gath, send, recv):
        x, y, c, _ = _place()
        out = []
        for k in range(1, N_DEV):
            peer = (x ^ ((k >> 2) & 1), y ^ ((k >> 1) & 1), c ^ (k & 1))
            out.append(pltpu.make_async_remote_copy(
                src_ref=in_ref, dst_ref=gath.at[k], send_sem=send.at[k - 1], recv_sem=recv.at[k - 1],
                device_id=peer, device_id_type=MESH))
        return out

    def start(ins, outs, scr):
        gath, send, recv = scr
        gath[0] = ins[0][...]
        for cp in copies(ins[0], gath, send, recv):
            cp.start()

    def finish(ins, outs, scr):
        gath, send, recv = scr
        for cp in copies(ins[0], gath, send, recv):
            cp.wait()
        x, y, c, _ = _place()
        me = 4 * x + 2 * y + c
        total = gath[me]
        for d in range(1, N_DEV):
            total = total + gath[d ^ me]
        outs[0][...] = total
        if loss_row is not None:
            outs[1][...] = jnp.sum(total[loss_row:loss_row + 1, :], axis=1, keepdims=True)

    out_shape = [jax.ShapeDtypeStruct((R, L), F32)]
    if loss_row is not None:
        out_shape.append(jax.ShapeDtypeStruct((1, 1), F32))
    return _Comm([buf], out_shape,
                 [pltpu.VMEM((N_DEV, R, L), F32), pltpu.SemaphoreType.DMA((N_DEV - 1,)),
                  pltpu.SemaphoreType.DMA((N_DEV - 1,))],
                 start, finish, in_specs=[WHOLE_VMEM], out_specs=[WHOLE_VMEM] * len(out_shape))


def _adamw_update(w_ref, g_ref, m_ref, v_ref, d_ref, nm_ref, nv_ref):
    gv = g_ref[...]
    nm = ADAM_B1 * m_ref[...] + (1.0 - ADAM_B1) * gv
    nv = ADAM_B2 * v_ref[...] + (1.0 - ADAM_B2) * (gv * gv)
    nm_ref[...] = nm
    nv_ref[...] = nv
    m_hat = nm / (1.0 - ADAM_B1 ** ADAM_STEP)
    v_hat = nv / (1.0 - ADAM_B2 ** ADAM_STEP)
    d_ref[...] = -ADAM_LR * (m_hat / (jnp.sqrt(v_hat) + ADAM_EPS) + ADAM_WD * w_ref[...])


def _adamw(w, g, m, v, name):
    R, C = w.shape
    tr = _tile(R, 256)

    def body(w_ref, g_ref, m_ref, v_ref, d_ref, nm_ref, nv_ref, g_out):
        _adamw_update(w_ref, g_ref, m_ref, v_ref, d_ref, nm_ref, nv_ref)
        g_out[...] = g_ref[...]

    blk = pl.BlockSpec((tr, C), lambda i: (i, 0))
    return pl.pallas_call(
        body, name=name, grid=(R // tr,), in_specs=[blk] * 4, out_specs=[blk] * 4,
        out_shape=[jax.ShapeDtypeStruct((R, C), F32)] * 4,
        compiler_params=_params("parallel"))(w, g, m, v)


def _adamw_many(ws, gs, ms, vs, name):
    n = len(ws)

    def body(*refs):
        for k in range(n):
            _adamw_update(*[refs[part * n + k] for part in range(7)])

    shapes = [jax.ShapeDtypeStruct(w.shape, F32) for w in ws]
    outs = pl.pallas_call(
        body, name=name, in_specs=[WHOLE_VMEM] * (4 * n), out_specs=[WHOLE_VMEM] * (3 * n), out_shape=shapes * 3,
        compiler_params=pltpu.CompilerParams(vmem_limit_bytes=VMEM_LIMIT_BYTES))(*ws, *gs, *ms, *vs)
    return outs[:n], outs[n:2 * n], outs[2 * n:]


def _pack_rows(arrays):
    rows = []
    for a in arrays:
        flat = a.reshape(-1).astype(F32)
        pad = (-flat.shape[0]) % LANES
        rows.append(jnp.pad(flat, (0, pad)).reshape(-1, LANES))
    buf = jnp.concatenate(rows, axis=0)
    return jnp.pad(buf, ((0, (-buf.shape[0]) % 8), (0, 0)))


def _unpack_rows(buf, shapes):
    out, r = [], 0
    for s in shapes:
        n = math.prod(s)
        nr = -(-n // LANES)
        out.append(buf[r:r + nr].reshape(-1)[:n].reshape(s))
        r += nr
    return out


def _block_diag(w):
    H, a, b = w.shape
    eye = jnp.eye(H, dtype=w.dtype)
    return (eye[:, None, :, None] * w[:, :, None, :]).reshape(H * a, H * b)


def _block_diag_parts(d, H):
    a, b = d.shape[0] // H, d.shape[1] // H
    d4 = d.reshape(H, a, H, b)
    return jnp.stack([d4[h, :, h, :] for h in range(H)])


def _rs_add(names, grads, others):
    return [_add_halves(g, o, "rs_add_halves_" + n) for n, g, o in zip(names, grads, others)]


def _rs_sum(names, parts, recvs):
    return [_sum_chips(p, r, "rs_sum_chips_" + n) for n, p, r in zip(names, parts, recvs)]


def _step(x, mem, target, shards, small, tap_rows, tap_shapes):
    D = x.shape[1]
    nch = N_CHIPS
    p = dict(small)

    (w_in_f,) = _gather_weights([shards['w_in']])
    wf = {}

    def ici(names):
        return _gather_over_ici([shards[n] for n in names])

    ici_a, taps_sum = ici(['w_out', 'w_q']), _all_reduce_rows(tap_rows)
    (z, h1), couts = _fwd_in(x, p['mix_norm_g'], w_in_f, comm=_merge(ici_a, taps_sum))
    bufs_a, (taps,) = _split(couts, ici_a, taps_sum)
    p.update(zip(COL_SHARDED_SMALL, _unpack_rows(taps, tap_shapes)))
    wa_d = _block_diag(p['lru_w_a']).astype(MXU_DTYPE)
    wx_d = _block_diag(p['lru_w_x']).astype(MXU_DTYPE)
    heads = p['lru_w_a'].shape[0]
    pass_a, ici_b = _gather_pass_on(bufs_a), ici(['w_kv', 'w_o'])
    (h, y_lru), couts = _lru_fwd(z, p['lru_conv_w'], p['lru_conv_b'], wa_d, p['lru_b_a'], wx_d, p['lru_b_x'],
                                 p['lru_lambda'], comm=_merge(pass_a, ici_b))
    (wf['w_out'], wf['w_q']), bufs_b = _split(couts, pass_a, ici_b)
    pass_b, ici_c = _gather_pass_on(bufs_b), ici(['w_up'])
    (c1, c3), couts = _conf_fwd(z, p['conf_conv_w'], p['conf_conv_b'], p['conf_ln_g'], p['conf_ln_b'],
                                comm=_merge(pass_b, ici_c))
    (wf['w_kv'], wf['w_o']), bufs_c = _split(couts, pass_b, ici_c)
    w_out2 = wf['w_out'].reshape(2, -1, D)
    w_q = wf['w_q'].reshape(D, D)
    w_o = wf['w_o'].reshape(D, D)
    pass_c, ici_d = _gather_pass_on(bufs_c), ici(['w_down'])
    (x1, h2, q), couts = _fwd_out_q(x, y_lru, c3, w_out2, p['xa_norm_g'], w_q, comm=_merge(pass_c, ici_d))
    (wf['w_up'],), bufs_d = _split(couts, pass_c, ici_d)
    m, kv = _kv_fwd(mem, p['mem_norm_g'], wf['w_kv'])
    (o, x2, h3), (wf['w_down'],) = _attn_fwd(q, kv, x1, w_o, p['ffn_norm_g'], comm=_gather_pass_on(bufs_d))
    gu, act, dx3, loss_lanes, d_final_g = _ffn_fwd(h3, wf['w_up'], p['ffn_conv_w'], p['ffn_conv_b'], wf['w_down'],
                                                   x2, p['final_norm_g'], target)

    dgu, dx2, d_ffn_g, d_ffn_cw, d_ffn_cb = _ffn_bwd(dx3, wf['w_down'], wf['w_up'], gu, x2, p['ffn_norm_g'],
                                                     p['ffn_conv_w'], p['ffn_conv_b'])
    g_down = _wgrad(act, dx3[None], "wgrad_down").reshape(nch, -1, D)
    g_up, other = _wgrad(h3[None], dgu, "wgrad_up", comm=_exchange_halves([g_down]))
    (p_down,) = _rs_add(['w_down'], [g_down], other)
    sc_down, ex_up = _scatter_chip_sums([p_down]), _exchange_halves([g_up])
    (dq, dx1, dkv, d_xa_g), couts = _attn_bwd(dx2, w_o, q, kv, x1, p['xa_norm_g'], w_q, comm=_merge(sc_down, ex_up))
    recv, other = _split(couts, sc_down, ex_up)
    f_down = _rs_sum(['w_down'], [p_down], recv)
    (p_up,) = _rs_add(['w_up'], [g_up], other)
    mid = ['w_o', 'w_q', 'w_kv']
    g_o = _wgrad(o[None], dx2[None], "wgrad_o").reshape(nch, -1, D)
    g_q = _wgrad(h2[None], dq[None], "wgrad_q").reshape(nch, -1, D)
    g_kv, d_mem_g = _kv_bwd(dkv, wf['w_kv'], mem, p['mem_norm_g'], m)
    join_down, sc_up, ex_mid = _join_halves(f_down), _scatter_chip_sums([p_up]), _exchange_halves([g_o, g_q, g_kv])
    (dz_c, d_conf_cw, d_conf_cb, d_ln_g, d_ln_b), couts = _conf_bwd(
        dx1, w_out2[1], z, c1, p['conf_conv_w'], p['conf_ln_g'], p['conf_ln_b'],
        comm=_merge(join_down, sc_up, ex_mid))
    (r_down,), recv, other = _split(couts, join_down, sc_up, ex_mid)
    p_up = [p_up]
    p_mid = _rs_add(mid, [g_o, g_q, g_kv], other)
    join_up, sc_mid = _join_halves(_rs_sum(['w_up'], p_up, recv)), _scatter_chip_sums(p_mid)
    (dz, d_wa, d_wx, d_ba, d_bx, d_lam, d_lru_cw, d_lru_cb), couts = _lru_bwd(
        dx1, w_out2[0], z, h, p['lru_conv_w'], p['lru_conv_b'], wa_d, p['lru_b_a'], wx_d, p['lru_b_x'],
        p['lru_lambda'], dz_c, comm=_merge(join_up, sc_mid))
    (r_up,), recv = _split(couts, join_up, sc_mid)
    f_mid = _rs_sum(mid, p_mid, recv)
    grad_x, d_mix_g = _bwd_in(dz, w_in_f, x, p['mix_norm_g'], dx1)

    small_g = {'mix_norm_g': d_mix_g, 'lru_conv_w': d_lru_cw, 'lru_conv_b': d_lru_cb,
               'lru_w_a': _block_diag_parts(d_wa, heads), 'lru_b_a': d_ba,
               'lru_w_x': _block_diag_parts(d_wx, heads), 'lru_b_x': d_bx, 'lru_lambda': d_lam,
               'conf_conv_w': d_conf_cw, 'conf_conv_b': d_conf_cb, 'conf_ln_g': d_ln_g, 'conf_ln_b': d_ln_b,
               'xa_norm_g': d_xa_g, 'mem_norm_g': d_mem_g, 'ffn_norm_g': d_ffn_g,
               'ffn_conv_w': d_ffn_cw, 'ffn_conv_b': d_ffn_cb, 'final_norm_g': d_final_g}
    names = list(small_g)
    shapes = [small_g[n].shape for n in names]
    join_mid = _join_halves(f_mid)
    small_sum = _all_reduce_rows(_pack_rows([loss_lanes] + [small_g[n] for n in names]), loss_row=0)
    g_in, couts = _wgrad(h1[None], dz, "wgrad_in", comm=_merge(join_mid, small_sum))
    r_mid, (summed, loss) = _split(couts, join_mid, small_sum)
    g_out_l, other = _wgrad(y_lru[None], dx1[None], "wgrad_out_lru", comm=_exchange_halves([g_in]))
    p_in = _rs_add(['w_in'], [g_in], other)
    g_out_c, recv = _wgrad(c3[None], dx1[None], "wgrad_out_conf", comm=_scatter_chip_sums(p_in))
    f_in = _rs_sum(['w_in'], p_in, recv)

    g_out = jnp.concatenate([g_out_l, g_out_c], axis=0).reshape(nch, -1, D)
    join_in, rs_out = _join_halves(f_in), _reduce_scatter_in_vmem(g_out)
    (r_in,), (r_out,) = _split(_run_comm(_merge(join_in, rs_out), "rs_last"), join_in, rs_out)
    big = dict(zip(['w_down', 'w_up'] + mid + ['w_out', 'w_in'], [r_down, r_up] + r_mid + [r_out, r_in]))
    return grad_x, big, summed, loss, names, [loss_lanes.shape] + shapes


def kernel(x, mem, mix_norm_g, w_in, lru_conv_w, lru_conv_b, lru_w_a, lru_b_a, lru_w_x, lru_b_x, lru_lambda, conf_conv_w, conf_conv_b, conf_ln_g, conf_ln_b, w_out, xa_norm_g, mem_norm_g, w_q, w_kv, w_o, ffn_norm_g, w_up, ffn_conv_w, ffn_conv_b, w_down, final_norm_g, loss_target, m_mix_norm_g, m_w_in, m_lru_conv_w, m_lru_conv_b, m_lru_w_a, m_lru_b_a, m_lru_w_x, m_lru_b_x, m_lru_lambda, m_conf_conv_w, m_conf_conv_b, m_conf_ln_g, m_conf_ln_b, m_w_out, m_xa_norm_g, m_mem_norm_g, m_w_q, m_w_kv, m_w_o, m_ffn_norm_g, m_w_up, m_ffn_conv_w, m_ffn_conv_b, m_w_down, m_final_norm_g, v_mix_norm_g, v_w_in, v_lru_conv_w, v_lru_conv_b, v_lru_w_a, v_lru_b_a, v_lru_w_x, v_lru_b_x, v_lru_lambda, v_conf_conv_w, v_conf_conv_b, v_conf_ln_g, v_conf_ln_b, v_w_out, v_xa_norm_g, v_mem_norm_g, v_w_q, v_w_kv, v_w_o, v_ffn_norm_g, v_w_up, v_ffn_conv_w, v_ffn_conv_b, v_w_down, v_final_norm_g):
    given = dict(locals())
    w = {n: given[n] for n in WEIGHTS}
    mom = {n: given["m_" + n] for n in WEIGHTS}
    var = {n: given["v_" + n] for n in WEIGHTS}
    xi, yi, ci = lax.axis_index("x"), lax.axis_index("y"), lax.axis_index("c")
    chip = 2 * xi + yi

    shards = {n: w[n][0].astype(WIRE_DTYPE) for n in BIG}
    tap_full = []
    for n in COL_SHARDED_SMALL:
        s = w[n][0]
        full = jnp.zeros((s.shape[0], N_CHIPS * s.shape[1]), F32)
        s = jnp.where(ci == 0, s, jnp.zeros_like(s))
        tap_full.append(lax.dynamic_update_slice(full, s, (0, chip * s.shape[1])))
    small = {n: (w[n] if w[n].ndim == 1 else w[n][0]) for n in SMALL if n not in COL_SHARDED_SMALL}
    small = {n: (a.reshape(1, -1) if a.ndim == 1 else a) for n, a in small.items()}

    grad_x, big_g, summed, loss, small_names, packed_shapes = _step(
        x[0], mem[0], loss_target[0], shards, small, _pack_rows(tap_full), [t.shape for t in tap_full])
    small_sum = dict(zip(small_names, _unpack_rows(summed, packed_shapes)[1:]))

    grads = {}
    for n in WEIGHTS:
        if n in BIG:
            g = big_g[n]
        elif n in COL_SHARDED_SMALL:
            width = w[n].shape[-1]
            g = lax.dynamic_slice_in_dim(small_sum[n], chip * width, width, axis=1)
        else:
            g = small_sum[n]
        grads[n] = g.reshape(w[n].shape)

    delta, new_m, new_v = {}, {}, {}
    for n in BIG:
        d, nm, nv, g = _adamw(w[n][0], grads[n][0], mom[n][0], var[n][0], "adamw_" + n)
        delta[n], new_m[n], new_v[n], grads[n] = d[None], nm[None], nv[None], g[None]
    flat = lambda a: a.reshape(-1, a.shape[-1])
    outs = _adamw_many(*[[flat(src[n]) for n in SMALL] for src in (w, grads, mom, var)], "adamw_small")
    for out, arrays in zip((delta, new_m, new_v), outs):
        out.update({n: a.reshape(w[n].shape) for n, a in zip(SMALL, arrays)})

    return (loss[0, 0], grad_x[None], *[grads[n] for n in WEIGHTS], *[delta[n] for n in WEIGHTS],
            *[new_m[n] for n in WEIGHTS], *[new_v[n] for n in WEIGHTS])
```

```python
import math

import jax
import jax.numpy as jnp
from jax import lax
from jax.experimental import pallas as pl
from jax.experimental.pallas import tpu as pltpu

F32 = jnp.float32
MXU_DTYPE = jnp.bfloat16
WIRE_DTYPE = jnp.bfloat16
EPS = 1e-6
RG_C = 8.0
XA_HEADS = 4
ADAM_LR, ADAM_B1, ADAM_B2, ADAM_EPS, ADAM_WD, ADAM_STEP = 0.001, 0.9, 0.999, 1e-08, 0.01, 10
VMEM_LIMIT_BYTES = 52 * 1024 * 1024
WGRAD_ACC_BYTES = 8 * 1024 * 1024
LANES = 1024
N_CHIPS = 4
N_DEV = 8
MESH = pl.DeviceIdType.MESH
GELU_C = math.sqrt(2.0 / math.pi)
GELU_K = 0.044715

WEIGHTS = ['mix_norm_g', 'w_in', 'lru_conv_w', 'lru_conv_b', 'lru_w_a', 'lru_b_a', 'lru_w_x', 'lru_b_x',
           'lru_lambda', 'conf_conv_w', 'conf_conv_b', 'conf_ln_g', 'conf_ln_b', 'w_out', 'xa_norm_g',
           'mem_norm_g', 'w_q', 'w_kv', 'w_o', 'ffn_norm_g', 'w_up', 'ffn_conv_w', 'ffn_conv_b', 'w_down',
           'final_norm_g']
BIG = ['w_in', 'w_kv', 'w_up', 'w_out', 'w_q', 'w_o', 'w_down']
SMALL = [n for n in WEIGHTS if n not in BIG]
COL_SHARDED_SMALL = ['lru_conv_w', 'conf_conv_w', 'ffn_conv_w']


def _params(*semantics):
    return pltpu.CompilerParams(dimension_semantics=semantics, vmem_limit_bytes=VMEM_LIMIT_BYTES)


ANY = pl.BlockSpec(memory_space=pl.ANY)
WHOLE_VMEM = pl.BlockSpec(memory_space=pltpu.VMEM)


class _Comm:
    def __init__(self, arrays, out_shapes, scratch, start, finish, aliases=None, in_specs=None, out_specs=None):
        self.arrays, self.out_shapes, self.scratch = list(arrays), list(out_shapes), list(scratch)
        self.start, self.finish = start, finish
        self.aliases = dict(aliases or {})
        self.in_specs = list(in_specs) if in_specs is not None else [ANY] * len(self.arrays)
        self.out_specs = list(out_specs) if out_specs is not None else [ANY] * len(self.out_shapes)


def _merge(*comms):
    comms = [c for c in comms if c is not None]
    if not comms:
        return None
    ai = [0]
    for c in comms:
        ai.append(ai[-1] + len(c.arrays))
    oi = [0]
    for c in comms:
        oi.append(oi[-1] + len(c.out_shapes))
    si = [0]
    for c in comms:
        si.append(si[-1] + len(c.scratch))

    def each(which):
        def run(ins, outs, scr):
            for k, c in enumerate(comms):
                getattr(c, which)(ins[ai[k]:ai[k + 1]], outs[oi[k]:oi[k + 1]], scr[si[k]:si[k + 1]])
        return run

    aliases = {ai[k] + i: oi[k] + o for k, c in enumerate(comms) for i, o in c.aliases.items()}
    return _Comm(sum((c.arrays for c in comms), []), sum((c.out_shapes for c in comms), []),
                 sum((c.scratch for c in comms), []), each("start"), each("finish"), aliases,
                 sum((c.in_specs for c in comms), []), sum((c.out_specs for c in comms), []))


def _split(outs, *comms):
    parts, at = [], 0
    for c in comms:
        parts.append(outs[at:at + len(c.out_shapes)])
        at += len(c.out_shapes)
    return parts


def _pcall(comm, body, *, name, grid, in_specs, out_specs, out_shape, semantics, scratch_shapes=(), aliases=None):
    single = not isinstance(out_shape, (list, tuple))
    out_shape = [out_shape] if single else list(out_shape)
    out_specs = [out_specs] if single else list(out_specs)
    in_specs, scratch_shapes = list(in_specs), list(scratch_shapes)
    aliases = dict(aliases or {})

    if comm is None:
        def plain(*args):
            return list(pl.pallas_call(body, name=name, grid=grid, in_specs=in_specs, out_specs=out_specs,
                                       out_shape=out_shape, scratch_shapes=scratch_shapes,
                                       input_output_aliases=aliases,
                                       compiler_params=_params(*semantics))(*args))
        return plain

    def hosted(*args):
        n_in, n_out, n_scr = len(args), len(out_shape), len(scratch_shapes)
        c_in, c_out = len(comm.arrays), len(comm.out_shapes)

        def wrapped(*refs):
            ins, cins = refs[:n_in], refs[n_in:n_in + c_in]
            o0 = n_in + c_in
            outs, couts = refs[o0:o0 + n_out], refs[o0 + n_out:o0 + n_out + c_out]
            s0 = o0 + n_out + c_out
            scr, cscr = refs[s0:s0 + n_scr], refs[s0 + n_scr:]
            first = last = None
            for axis, size in enumerate(grid):
                at_start, at_end = pl.program_id(axis) == 0, pl.program_id(axis) == size - 1
                first = at_start if first is None else first & at_start
                last = at_end if last is None else last & at_end
            if first is None:
                comm.start(cins, couts, cscr)
                body(*ins, *outs, *scr)
                comm.finish(cins, couts, cscr)
                return
            pl.when(first)(lambda: comm.start(cins, couts, cscr))
            body(*ins, *outs, *scr)
            pl.when(last)(lambda: comm.finish(cins, couts, cscr))

        res = pl.pallas_call(
            wrapped, name=name, grid=grid, in_specs=in_specs + comm.in_specs, out_specs=out_specs + comm.out_specs,
            out_shape=out_shape + comm.out_shapes, scratch_shapes=scratch_shapes + comm.scratch,
            input_output_aliases={**aliases, **{n_in + i: n_out + o for i, o in comm.aliases.items()}},
            compiler_params=pltpu.CompilerParams(dimension_semantics=("arbitrary",) * len(grid),
                                                 vmem_limit_bytes=VMEM_LIMIT_BYTES, has_side_effects=True),
        )(*args, *comm.arrays)
        return list(res[:n_out]), list(res[n_out:])

    return hosted


def _run_comm(comm, name):
    return _pcall(comm, lambda: None, name=name, grid=(), in_specs=[], out_specs=[], out_shape=[], semantics=())()[1]


def _tile(n, want, align=8):
    if n <= want:
        return n
    for t in range(want - want % align, 0, -align):
        if n % t == 0:
            return t
    raise ValueError((n, want, align))


def _mm(a, b):
    return jnp.dot(a.astype(MXU_DTYPE), b.astype(MXU_DTYPE), preferred_element_type=F32)


def _mm_nt(a, b):
    return lax.dot_general(a.astype(MXU_DTYPE), b.astype(MXU_DTYPE), (((1,), (1,)), ((), ())),
                           preferred_element_type=F32)


def _mm_tn(a, b):
    return lax.dot_general(a.astype(MXU_DTYPE), b.astype(MXU_DTYPE), (((0,), (0,)), ((), ())),
                           preferred_element_type=F32)


def _sigmoid(v):
    return 0.5 * jnp.tanh(0.5 * v) + 0.5


def _gelu(v):
    v2 = v * v
    t = jnp.tanh(v * (GELU_C + (GELU_C * GELU_K) * v2))
    hv = 0.5 * v
    dt = (1.0 - t * t) * (GELU_C + (3.0 * GELU_C * GELU_K) * v2)
    return hv + hv * t, (0.5 + 0.5 * t) + hv * dt


def _softplus_neg(lam):
    e = jnp.exp(-jnp.abs(lam))
    u = 1.0 + e
    log1p_e = jnp.where(u == 1.0, e, jnp.log(u) * e / jnp.where(u == 1.0, 1.0, u - 1.0))
    return jnp.maximum(-lam, 0.0) + log1p_e


def _rms(xv):
    rinv = lax.rsqrt(jnp.mean(xv * xv, axis=-1, keepdims=True) + EPS)
    return rinv, xv * rinv


def _rms_bwd(rinv, xhat, dxhat):
    return rinv * (dxhat - xhat * jnp.mean(dxhat * xhat, axis=-1, keepdims=True))


def _colsum(v):
    return jnp.sum(v, axis=0, keepdims=True)


def _wrow(w_ref, k, wcols):
    return w_ref[pl.ds(k, 1), :] if wcols is None else w_ref[pl.ds(k, 1), wcols]


def _windows(buf_ref, halo, taps, rows):
    assert taps <= 8 <= halo
    x = buf_ref[pl.ds(halo - 8, rows + 8), :]
    return [x[8:] if s == 0 else pltpu.roll(x, s, 0)[8:] for s in range(taps)]


def _causal_from(xs, w_ref, wcols=None):
    taps = len(xs)
    acc = None
    for s in range(taps):
        term = _wrow(w_ref, taps - 1 - s, wcols) * xs[s]
        acc = term if acc is None else acc + term
    return acc


def _tap_grads_from(dw_ref, dy, xs, wcols=None):
    taps = len(xs)
    for s in range(taps):
        g = _colsum(dy * xs[s])
        if wcols is None:
            dw_ref[pl.ds(taps - 1 - s, 1), :] += g
        else:
            dw_ref[pl.ds(taps - 1 - s, 1), wcols] += g


def _causal_taps(buf_ref, halo, w_ref, taps, rows, wcols=None):
    return _causal_from(_windows(buf_ref, halo, taps, rows), w_ref, wcols)


def _anticausal_taps(buf_ref, w_ref, taps, rows, wcols=None):
    assert taps <= 8
    x = buf_ref[pl.ds(0, rows + 8), :]
    acc = None
    for s in range(taps):
        win = x[:rows] if s == 0 else pltpu.roll(x, rows + 8 - s, 0)[:rows]
        term = _wrow(w_ref, taps - 1 - s, wcols) * win
        acc = term if acc is None else acc + term
    return acc


def _shift_copies(dst_ref, buf_ref, rows, up):
    x = buf_ref[pl.ds(0, rows + 8), :]
    for r in range(8):
        if up:
            dst_ref[r] = x[:rows] if r == 0 else pltpu.roll(x, rows + 8 - r, 0)[:rows]
        else:
            dst_ref[r] = x[8:] if r == 0 else pltpu.roll(x, r, 0)[8:]


def _causal_taps8(sh_ref, halo, w_ref, taps, rows):
    acc = None
    for s in range(taps):
        term = _wrow(w_ref, taps - 1 - s, None) * sh_ref[s % 8, pl.ds(halo - 8 - 8 * (s // 8), rows), :]
        acc = term if acc is None else acc + term
    return acc


def _anticausal_taps8(sh_ref, w_ref, taps, rows):
    acc = None
    for s in range(taps):
        term = _wrow(w_ref, taps - 1 - s, None) * sh_ref[s % 8, pl.ds(8 * (s // 8), rows), :]
        acc = term if acc is None else acc + term
    return acc


def _tap_grads8(dw_ref, dy, sh_ref, halo, taps, rows):
    for s in range(taps):
        dw_ref[pl.ds(taps - 1 - s, 1), :] += _colsum(dy * sh_ref[s % 8, pl.ds(halo - 8 - 8 * (s // 8), rows), :])


def _fwd_in(x, g, w_in, comm=None):
    S, D = x.shape
    nb, _, C = w_in.shape
    ts = _tile(S, 1024)

    halves = 2 if ts % 32 == 0 else 1
    hr = ts // halves

    def body(x_ref, g_ref, w_ref, z_ref, h_ref):
        def norm(k):
            rows = pl.ds(k * hr, hr)
            _, xhat = _rms(x_ref[rows, :])
            h = (xhat * g_ref[...]).astype(MXU_DTYPE)
            h_ref[rows, :] = h
            return h

        h_next = norm(0)
        for k in range(halves):
            rows, h = pl.ds(k * hr, hr), h_next
            if k + 1 < halves:
                h_next = norm(k + 1)
            for j in range(nb):
                z_ref[j, rows, :] = jnp.dot(h, w_ref[j], preferred_element_type=F32)

    return _pcall(
        comm, body, name="fwd_in", grid=(S // ts,),
        in_specs=[pl.BlockSpec((ts, D), lambda i: (i, 0)), pl.BlockSpec((1, D), lambda i: (0, 0)),
                  pl.BlockSpec((nb, D, C), lambda i: (0, 0, 0))],
        out_specs=[pl.BlockSpec((nb, ts, C), lambda i: (0, i, 0)), pl.BlockSpec((ts, D), lambda i: (i, 0))],
        out_shape=[jax.ShapeDtypeStruct((nb, S, C), F32), jax.ShapeDtypeStruct((S, D), MXU_DTYPE)],
        semantics=("parallel",))(x, g, w_in)


def _lru_gates(xc, wa_ref, ba_ref, wx_ref, bx_ref, sp):
    xb = xc.astype(MXU_DTYPE)
    r = _sigmoid(jnp.dot(xb, wa_ref[...], preferred_element_type=F32) + ba_ref[...])
    ig = _sigmoid(jnp.dot(xb, wx_ref[...], preferred_element_type=F32) + bx_ref[...])
    log_a = -RG_C * r * sp
    a = jnp.exp(log_a)
    one_minus_a2 = jnp.tanh(-log_a) * (a * a + 1.0)
    inv_mult = lax.rsqrt(one_minus_a2)
    mult = jnp.where(one_minus_a2 > 0.0, one_minus_a2 * inv_mult, 0.0)
    return r, ig, a, mult, inv_mult


def _lru_fwd(z, conv_w, conv_b, wa, ba, wx, bx, lam, comm=None):
    _, S, C = z.shape
    ts = _tile(S, 512)
    taps = conv_w.shape[0]
    halo = 8

    def body(zx_ref, zg_ref, cw_ref, cb_ref, wa_ref, ba_ref, wx_ref, bx_ref, lam_ref,
             h_ref, y_ref, xbuf, a_s, u_s, hc):
        i = pl.program_id(0)

        @pl.when(i == 0)
        def _():
            xbuf[pl.ds(0, halo), :] = jnp.zeros((halo, C), F32)
            hc[...] = jnp.zeros_like(hc)

        xbuf[pl.ds(halo, ts), :] = zx_ref[0]
        xc = _causal_taps(xbuf, halo, cw_ref, taps, ts) + cb_ref[...]
        sp = _softplus_neg(lam_ref[...])
        _, ig, a, mult, _ = _lru_gates(xc, wa_ref, ba_ref, wx_ref, bx_ref, sp)
        a_s[...] = a
        u_s[...] = mult * (ig * xc)
        row = lax.broadcasted_iota(jnp.int32, (8, C), 0)

        def step(k, carry):
            off = pl.multiple_of(k * 8, 8)
            av = a_s[pl.ds(off, 8), :]
            uv = u_s[pl.ds(off, 8), :]
            for d in (1, 2, 4):
                m = row >= d
                a_sh = jnp.where(m, pltpu.roll(av, d, 0), 1.0)
                u_sh = jnp.where(m, pltpu.roll(uv, d, 0), 0.0)
                uv = uv + av * u_sh
                av = av * a_sh
            hv = uv + av * carry
            h_ref[pl.ds(off, 8), :] = hv
            return jnp.broadcast_to(hv[7:8, :], (8, C))

        hc[...] = lax.fori_loop(0, ts // 8, step, hc[...])
        ge, _ = _gelu(zg_ref[0])
        y_ref[...] = (h_ref[...] * ge).astype(MXU_DTYPE)
        xbuf[pl.ds(0, halo), :] = xbuf[pl.ds(ts, halo), :]

    vec = pl.BlockSpec((1, C), lambda i: (0, 0))
    mat = pl.BlockSpec((C, C), lambda i: (0, 0))
    return _pcall(
        comm, body, name="lru_fwd", grid=(S // ts,),
        in_specs=[pl.BlockSpec((1, ts, C), lambda i: (0, i, 0)), pl.BlockSpec((1, ts, C), lambda i: (1, i, 0)),
                  pl.BlockSpec((taps, C), lambda i: (0, 0)), vec, mat, vec, mat, vec, vec],
        out_specs=[pl.BlockSpec((ts, C), lambda i: (i, 0)), pl.BlockSpec((ts, C), lambda i: (i, 0))],
        out_shape=[jax.ShapeDtypeStruct((S, C), F32), jax.ShapeDtypeStruct((S, C), MXU_DTYPE)],
        scratch_shapes=[pltpu.VMEM((ts + halo, C), F32), pltpu.VMEM((ts, C), F32), pltpu.VMEM((ts, C), F32),
                        pltpu.VMEM((8, C), F32)],
        semantics=("arbitrary",))(z, z, conv_w, conv_b, wa, ba, wx, bx, lam)


def _layer_norm_stats(c1):
    mu = jnp.mean(c1, axis=-1, keepdims=True)
    xc = c1 - mu
    rstd = lax.rsqrt(jnp.mean(xc * xc, axis=-1, keepdims=True) + EPS)
    return rstd, xc * rstd


def _conf_fwd(z, conv_w, conv_b, ln_g, ln_b, comm=None):
    _, S, C = z.shape
    ts = _tile(S, 512)
    taps = conv_w.shape[0]
    halo = 32

    def body(za_ref, zb_ref, cw_ref, cb_ref, g_ref, b_ref, c1_ref, c3_ref, cbuf, shifted):
        i = pl.program_id(0)

        @pl.when(i == 0)
        def _():
            cbuf[pl.ds(0, halo), :] = jnp.zeros((halo, C), F32)

        cbuf[pl.ds(halo, ts), :] = za_ref[0] * _sigmoid(zb_ref[0])
        _shift_copies(shifted, cbuf, ts + halo - 8, up=False)
        c1 = _causal_taps8(shifted, halo, cw_ref, taps, ts) + cb_ref[...]
        c1_ref[...] = c1
        _, xhat = _layer_norm_stats(c1)
        c2 = xhat * g_ref[...] + b_ref[...]
        c3_ref[...] = (c2 * _sigmoid(c2)).astype(MXU_DTYPE)
        cbuf[pl.ds(0, halo), :] = cbuf[pl.ds(ts, halo), :]

    vec = pl.BlockSpec((1, C), lambda i: (0, 0))
    return _pcall(
        comm, body, name="conf_fwd", grid=(S // ts,),
        in_specs=[pl.BlockSpec((1, ts, C), lambda i: (2, i, 0)), pl.BlockSpec((1, ts, C), lambda i: (3, i, 0)),
                  pl.BlockSpec((taps, C), lambda i: (0, 0)), vec, vec, vec],
        out_specs=[pl.BlockSpec((ts, C), lambda i: (i, 0)), pl.BlockSpec((ts, C), lambda i: (i, 0))],
        out_shape=[jax.ShapeDtypeStruct((S, C), F32), jax.ShapeDtypeStruct((S, C), MXU_DTYPE)],
        scratch_shapes=[pltpu.VMEM((ts + halo, C), F32), pltpu.VMEM((8, ts + halo - 8, C), F32)],
        semantics=("arbitrary",))(z, z, conv_w, conv_b, ln_g, ln_b)


def _fwd_out_q(x, y_lru, c3, w_out, g_xa, w_q, comm=None):
    S, D = x.shape
    C = y_lru.shape[1]
    ts = _tile(S, 1024)

    halves = 2 if ts % 32 == 0 else 1
    hr = ts // halves

    def body(x_ref, yl_ref, c3_ref, wo_ref, g_ref, wq_ref, x1_ref, h2_ref, q_ref):
        def mixed(k):
            rows = pl.ds(k * hr, hr)
            return (jnp.dot(yl_ref[rows, :], wo_ref[0], preferred_element_type=F32)
                    + jnp.dot(c3_ref[rows, :], wo_ref[1], preferred_element_type=F32))

        y_next = mixed(0)
        for k in range(halves):
            rows, y = pl.ds(k * hr, hr), y_next
            if k + 1 < halves:
                y_next = mixed(k + 1)
            x1 = x_ref[rows, :] + y
            x1_ref[rows, :] = x1
            _, xhat = _rms(x1)
            h2 = (xhat * g_ref[...]).astype(MXU_DTYPE)
            h2_ref[rows, :] = h2
            q_ref[rows, :] = jnp.dot(h2, wq_ref[...], preferred_element_type=F32).astype(MXU_DTYPE)

    row = lambda w: pl.BlockSpec((ts, w), lambda i: (i, 0))
    return _pcall(
        comm, body, name="fwd_out_q", grid=(S // ts,),
        in_specs=[row(D), row(C), row(C), pl.BlockSpec((2, C, D), lambda i: (0, 0, 0)),
                  pl.BlockSpec((1, D), lambda i: (0, 0)), pl.BlockSpec((D, D), lambda i: (0, 0))],
        out_specs=[row(D), row(D), row(D)],
        out_shape=[jax.ShapeDtypeStruct((S, D), F32), jax.ShapeDtypeStruct((S, D), MXU_DTYPE),
                   jax.ShapeDtypeStruct((S, D), MXU_DTYPE)],
        semantics=("parallel",))(x, y_lru, c3, w_out, g_xa, w_q)


def _kv_fwd(mem, g, w_kv):
    M, D = mem.shape
    nb, _, C = w_kv.shape

    def body(mem_ref, g_ref, w_ref, m_ref, kv_ref):
        _, xhat = _rms(mem_ref[...])
        m = (xhat * g_ref[...]).astype(MXU_DTYPE)
        m_ref[...] = m
        for j in range(nb):
            kv_ref[:, pl.ds(j * C, C)] = jnp.dot(m, w_ref[j], preferred_element_type=F32).astype(MXU_DTYPE)

    return pl.pallas_call(
        body, name="kv_fwd", grid=(1,),
        in_specs=[pl.BlockSpec((M, D), lambda i: (0, 0)), pl.BlockSpec((1, D), lambda i: (0, 0)),
                  pl.BlockSpec((nb, D, C), lambda i: (0, 0, 0))],
        out_specs=[pl.BlockSpec((M, D), lambda i: (0, 0)), pl.BlockSpec((M, nb * C), lambda i: (0, 0))],
        out_shape=[jax.ShapeDtypeStruct((M, D), MXU_DTYPE), jax.ShapeDtypeStruct((M, nb * C), MXU_DTYPE)],
        compiler_params=_params("arbitrary"))(mem, g, w_kv)


def _softmax_rows(s):
    e = jnp.exp(s - jnp.max(s, axis=-1, keepdims=True))
    return e / jnp.sum(e, axis=-1, keepdims=True)


def _attn_fwd(q, kv, x1, w_o, g_ffn, comm=None):
    S, D = x1.shape
    M = kv.shape[0]
    hd = D // XA_HEADS
    scale = hd ** -0.5
    ts = _tile(S, 1024)

    def body(q_ref, kv_ref, x1_ref, wo_ref, g_ref, o_ref, x2_ref, h3_ref):
        def scores(h):
            cols = pl.ds(h * hd, hd)
            return _mm_nt(q_ref[:, cols], kv_ref[:, cols]) * scale

        s_next = scores(0)
        for h in range(XA_HEADS):
            s = s_next
            if h + 1 < XA_HEADS:
                s_next = scores(h + 1)
            p = _softmax_rows(s)
            o_ref[:, pl.ds(h * hd, hd)] = _mm(p, kv_ref[:, pl.ds(D + h * hd, hd)]).astype(MXU_DTYPE)
        x2 = x1_ref[...] + jnp.dot(o_ref[...], wo_ref[...], preferred_element_type=F32)
        x2_ref[...] = x2
        _, xhat = _rms(x2)
        h3_ref[...] = (xhat * g_ref[...]).astype(MXU_DTYPE)

    row = pl.BlockSpec((ts, D), lambda i: (i, 0))
    return _pcall(
        comm, body, name="attn_fwd", grid=(S // ts,),
        in_specs=[row, pl.BlockSpec((M, 2 * D), lambda i: (0, 0)), row, pl.BlockSpec((D, D), lambda i: (0, 0)),
                  pl.BlockSpec((1, D), lambda i: (0, 0))],
        out_specs=[row, row, row],
        out_shape=[jax.ShapeDtypeStruct((S, D), MXU_DTYPE), jax.ShapeDtypeStruct((S, D), F32),
                   jax.ShapeDtypeStruct((S, D), MXU_DTYPE)],
        semantics=("parallel",))(q, kv, x1, w_o, g_ffn)


def _ffn_fwd(h3, w_up, conv_w, conv_b, w_down, x2, g_final, target, comm=None):
    S, D = h3.shape
    nb, _, CW = w_up.shape
    half = nb // 2
    cb = 1536
    per = CW // cb
    J = half * per
    ts = _tile(S, 256)
    taps = conv_w.shape[0]
    halo = 8

    def body(h_ref, wup_ref, cw_ref, cb_ref, wd_ref, x2_ref, gf_ref, t_ref,
             gu_ref, act_ref, dx3_ref, loss_ref, dgf_ref, gbuf):
        i = pl.program_id(0)

        @pl.when(i == 0)
        def _():
            for ref in (loss_ref, dgf_ref, gbuf):
                ref[...] = jnp.zeros_like(ref)

        hv = h_ref[...]
        x3 = x2_ref[...]
        def up(j):
            b, cols = j // per, pl.ds((j % per) * cb, cb)
            return (jnp.dot(hv, wup_ref[b, :, cols], preferred_element_type=F32),
                    jnp.dot(hv, wup_ref[half + b, :, cols], preferred_element_type=F32))

        ahead = up(0)
        for j in range(J):
            b, cols, wcols = j // per, pl.ds((j % per) * cb, cb), pl.ds(j * cb, cb)
            g, u = ahead
            if j + 1 < J:
                ahead = up(j + 1)
            gu_ref[0, b, :, cols] = g
            gu_ref[1, b, :, cols] = u
            gbuf[j, pl.ds(halo, ts), :] = g
            gc = _causal_taps(gbuf.at[j], halo, cw_ref, taps, ts, wcols=wcols) + cb_ref[:, wcols]
            gbuf[j, pl.ds(0, halo), :] = gbuf[j, pl.ds(ts, halo), :]
            ge, _ = _gelu(gc)
            act = (ge * u).astype(MXU_DTYPE)
            act_ref[j] = act
            x3 = x3 + jnp.dot(act, wd_ref[j], preferred_element_type=F32)
        rinv, xhat = _rms(x3)
        gf = gf_ref[...]
        diff = xhat * gf - t_ref[...]
        loss_ref[...] += _colsum(diff * diff) * (0.5 / D)
        dy = diff * (1.0 / D)
        dgf_ref[...] += _colsum(dy * xhat)
        dx3_ref[...] = _rms_bwd(rinv, xhat, dy * gf)

    row = pl.BlockSpec((ts, D), lambda i: (i, 0))
    vecd = pl.BlockSpec((1, D), lambda i: (0, 0))
    once = pl.Buffered(1)
    sds = jax.ShapeDtypeStruct
    res = _pcall(
        comm, body, name="ffn_fwd", grid=(S // ts,),
        in_specs=[row, pl.BlockSpec((nb, D, CW), lambda i: (0, 0, 0), pipeline_mode=once),
                  pl.BlockSpec((taps, half * CW), lambda i: (0, 0)), pl.BlockSpec((1, half * CW), lambda i: (0, 0)),
                  pl.BlockSpec((J, cb, D), lambda i: (0, 0, 0), pipeline_mode=once), row, vecd, row],
        out_specs=[pl.BlockSpec((2, half, ts, CW), lambda i: (0, 0, i, 0)),
                   pl.BlockSpec((J, ts, cb), lambda i: (0, i, 0)), row, vecd, vecd],
        out_shape=[sds((2, half, S, CW), F32), sds((J, S, cb), MXU_DTYPE), sds((S, D), F32),
                   sds((1, D), F32), sds((1, D), F32)],
        scratch_shapes=[pltpu.VMEM((J, ts + halo, cb), F32)],
        semantics=("arbitrary",))(h3, w_up, conv_w, conv_b, w_down.reshape(J, cb, D), x2, g_final, target)
    outs = res if comm is None else res[0]
    outs = [outs[0].reshape(nb, S, CW)] + list(outs[1:])
    return outs if comm is None else (outs, res[1])


def _ffn_bwd(dx3, w_down, w_up, gu, x2, g_ffn, conv_w, conv_b, comm=None):
    nb, S, CW = gu.shape
    half = nb // 2
    D = dx3.shape[1]
    cb = 1536
    per = CW // cb
    J = half * per
    ts = _tile(S, 256)
    n = S // ts
    taps = conv_w.shape[0]
    halo = 8
    hb = ts // halo

    def body(dx_ref, x2_ref, gf_ref, wd_ref, wup_ref, gu_ref, gh_ref, cw_ref, cb_ref,
             dgu_ref, dx2_ref, dgf_ref, dcw_ref, dcb_ref, gbuf, dbuf):
        i = pl.program_id(0)
        r = n - 1 - i

        @pl.when(i == 0)
        def _():
            for ref in (dgf_ref, dcw_ref, dcb_ref, dbuf):
                ref[...] = jnp.zeros_like(ref)

        dx3v = dx_ref[...]
        dxb = dx3v.astype(MXU_DTYPE)
        dacts = [_mm_nt(dxb, wd_ref[j]) for j in range(J)]
        dh = None
        for j in range(J):
            b, cols, wcols = j // per, pl.ds((j % per) * cb, cb), pl.ds(j * cb, cb)
            dact = dacts[j]
            gbuf[pl.ds(0, halo), :] = jnp.where(r > 0, gh_ref[0, b, :, cols], 0.0)
            gbuf[pl.ds(halo, ts), :] = gu_ref[0, b, :, cols]
            gs = _windows(gbuf, halo, taps, ts)
            gc = _causal_from(gs, cw_ref, wcols) + cb_ref[:, wcols]
            ge, dge = _gelu(gc)
            dub = (dact * ge).astype(MXU_DTYPE)
            dgc = dact * gu_ref[1, b, :, cols] * dge
            dcb_ref[:, wcols] += _colsum(dgc)
            dbuf[j, pl.ds(0, ts), :] = dgc
            _tap_grads_from(dcw_ref, dgc, gs, wcols)
            dgb = _anticausal_taps(dbuf.at[j], cw_ref, taps, ts, wcols=wcols).astype(MXU_DTYPE)
            dbuf[j, pl.ds(ts, halo), :] = dbuf[j, pl.ds(0, halo), :]
            dgu_ref[0, b, :, cols] = dgb
            dgu_ref[1, b, :, cols] = dub
            part = _mm_nt(dgb, wup_ref[b, :, cols]) + _mm_nt(dub, wup_ref[half + b, :, cols])
            dh = part if dh is None else dh + part
        rinv, xhat = _rms(x2_ref[...])
        dgf_ref[...] += _colsum(dh * xhat)
        dx2_ref[...] = dx3v + _rms_bwd(rinv, xhat, dh * gf_ref[...])

    gu2 = gu.reshape(2, half, S, CW)
    row = pl.BlockSpec((ts, D), lambda i: (n - 1 - i, 0))
    vecd = pl.BlockSpec((1, D), lambda i: (0, 0))
    pair = pl.BlockSpec((2, half, ts, CW), lambda i: (0, 0, n - 1 - i, 0))
    g_prev = pl.BlockSpec((1, half, halo, CW), lambda i: (0, 0, jnp.maximum((n - 1 - i) * hb - 1, 0), 0))
    tapw = pl.BlockSpec((taps, half * CW), lambda i: (0, 0))
    vec = pl.BlockSpec((1, half * CW), lambda i: (0, 0))
    once = pl.Buffered(1)
    sds = jax.ShapeDtypeStruct
    res = _pcall(
        comm, body, name="ffn_bwd", grid=(n,),
        in_specs=[row, row, vecd, pl.BlockSpec((J, cb, D), lambda i: (0, 0, 0), pipeline_mode=once),
                  pl.BlockSpec((nb, D, CW), lambda i: (0, 0, 0), pipeline_mode=once), pair, g_prev, tapw, vec],
        out_specs=[pair, row, vecd, tapw, vec],
        out_shape=[sds((2, half, S, CW), MXU_DTYPE), sds((S, D), F32), sds((1, D), F32),
                   sds((taps, half * CW), F32), sds((1, half * CW), F32)],
        scratch_shapes=[pltpu.VMEM((ts + halo, cb), F32), pltpu.VMEM((J, ts + halo, cb), F32)],
        semantics=("arbitrary",))(dx3, x2, g_ffn, w_down.reshape(J, cb, D), w_up, gu2, gu2, conv_w, conv_b)
    outs = res if comm is None else res[0]
    outs = [outs[0].reshape(nb, S, CW)] + list(outs[1:])
    return outs if comm is None else (outs, res[1])


def _attn_bwd(dx2, w_o, q, kv, x1, g_xa, w_q, comm=None):
    S, D = x1.shape
    M = kv.shape[0]
    hd = D // XA_HEADS
    scale = hd ** -0.5
    ts = _tile(S, 1024)

    def body(dx2_ref, wo_ref, q_ref, kv_ref, x1_ref, g_ref, wq_ref, dq_ref, dx1_ref, dkv_ref, dg_ref):
        i = pl.program_id(0)

        @pl.when(i == 0)
        def _():
            dkv_ref[...] = jnp.zeros_like(dkv_ref)
            dg_ref[...] = jnp.zeros_like(dg_ref)

        dx2 = dx2_ref[...]
        do = _mm_nt(dx2, wo_ref[...]).astype(MXU_DTYPE)
        def scores(h):
            cols = pl.ds(h * hd, hd)
            doh = do[:, h * hd:(h + 1) * hd]
            return (_mm_nt(q_ref[:, cols], kv_ref[:, cols]) * scale,
                    _mm_nt(doh, kv_ref[:, pl.ds(D + h * hd, hd)]), doh)

        ahead = scores(0)
        for h in range(XA_HEADS):
            cols = pl.ds(h * hd, hd)
            vcols = pl.ds(D + h * hd, hd)
            s, dp, doh = ahead
            if h + 1 < XA_HEADS:
                ahead = scores(h + 1)
            p = _softmax_rows(s)
            ds = (p * (dp - jnp.sum(dp * p, axis=-1, keepdims=True)) * scale).astype(MXU_DTYPE)
            dkv_ref[:, vcols] += _mm_tn(p, doh)
            dq_ref[:, cols] = _mm(ds, kv_ref[:, cols]).astype(MXU_DTYPE)
            dkv_ref[:, cols] += _mm_tn(ds, q_ref[:, cols])
        dh2 = _mm_nt(dq_ref[...], wq_ref[...])
        rinv, xhat = _rms(x1_ref[...])
        dg_ref[...] += _colsum(dh2 * xhat)
        dx1_ref[...] = dx2 + _rms_bwd(rinv, xhat, dh2 * g_ref[...])

    row = pl.BlockSpec((ts, D), lambda i: (i, 0))
    mat = pl.BlockSpec((D, D), lambda i: (0, 0))
    vecd = pl.BlockSpec((1, D), lambda i: (0, 0))
    kvs = pl.BlockSpec((M, 2 * D), lambda i: (0, 0))
    return _pcall(
        comm, body, name="attn_bwd", grid=(S // ts,),
        in_specs=[row, mat, row, kvs, row, vecd, mat],
        out_specs=[row, row, kvs, vecd],
        out_shape=[jax.ShapeDtypeStruct((S, D), MXU_DTYPE), jax.ShapeDtypeStruct((S, D), F32),
                   jax.ShapeDtypeStruct((M, 2 * D), F32), jax.ShapeDtypeStruct((1, D), F32)],
        semantics=("arbitrary",))(dx2, w_o, q, kv, x1, g_xa, w_q)


def _kv_bwd(dkv, w_kv, mem, g, m):
    M, D = mem.shape
    nb, _, C = w_kv.shape

    def body(dkv_ref, w_ref, mem_ref, m_ref, dw_ref, dg_ref):
        dm = jnp.zeros((M, D), F32)
        for j in range(nb):
            dj = dkv_ref[:, pl.ds(j * C, C)].astype(MXU_DTYPE)
            dw_ref[j] = _mm_tn(m_ref[...], dj).astype(dw_ref.dtype)
            dm = dm + _mm_nt(dj, w_ref[j])
        _, xhat = _rms(mem_ref[...])
        dg_ref[...] = _colsum(dm * xhat)

    full = lambda *s: pl.BlockSpec(s, lambda i: (0,) * len(s))
    return pl.pallas_call(
        body, name="kv_bwd", grid=(1,),
        in_specs=[full(M, nb * C), full(nb, D, C), full(M, D), full(M, D)],
        out_specs=[full(nb, D, C), full(1, D)],
        out_shape=[jax.ShapeDtypeStruct((nb, D, C), WIRE_DTYPE), jax.ShapeDtypeStruct((1, D), F32)],
        compiler_params=_params("arbitrary"))(dkv, w_kv, mem, m)


def _conf_bwd(dx1, w_out_c, z, c1, conv_w, ln_g, ln_b, comm=None):
    _, S, C = z.shape
    D = dx1.shape[1]
    ts = _tile(S, 512)
    n = S // ts
    taps = conv_w.shape[0]
    halo = 32
    hb = ts // halo

    def body(dx_ref, wo_ref, za_ref, zb_ref, zah_ref, zbh_ref, c1_ref, cw_ref, g_ref, b_ref,
             dz_ref, dcw_ref, dcb_ref, dlg_ref, dlb_ref, c0buf, dbuf, shifted):
        i = pl.program_id(0)
        r = n - 1 - i

        @pl.when(i == 0)
        def _():
            for ref in (dcw_ref, dcb_ref, dlg_ref, dlb_ref):
                ref[...] = jnp.zeros_like(ref)
            dbuf[pl.ds(ts, halo), :] = jnp.zeros((halo, C), F32)

        za = za_ref[0]
        sb = _sigmoid(zb_ref[0])
        c0buf[pl.ds(0, halo), :] = jnp.where(r > 0, zah_ref[0] * _sigmoid(zbh_ref[0]), 0.0)
        c0buf[pl.ds(halo, ts), :] = za * sb
        dc3 = _mm_nt(dx_ref[...], wo_ref[...])
        rstd, xhat = _layer_norm_stats(c1_ref[...])
        g = g_ref[...]
        c2 = xhat * g + b_ref[...]
        sg = _sigmoid(c2)
        dc2 = dc3 * sg * (1.0 + c2 * (1.0 - sg))
        dlg_ref[...] += _colsum(dc2 * xhat)
        dlb_ref[...] += _colsum(dc2)
        dxh = dc2 * g
        dc1 = rstd * (dxh - jnp.mean(dxh, axis=-1, keepdims=True)
                      - xhat * jnp.mean(dxh * xhat, axis=-1, keepdims=True))
        dcb_ref[...] += _colsum(dc1)
        dbuf[pl.ds(0, ts), :] = dc1
        _shift_copies(shifted, c0buf, ts + halo - 8, up=False)
        _tap_grads8(dcw_ref, dc1, shifted, halo, taps, ts)
        _shift_copies(shifted, dbuf, ts + halo - 8, up=True)
        dc0 = _anticausal_taps8(shifted, cw_ref, taps, ts)
        dz_ref[0] = (dc0 * sb).astype(MXU_DTYPE)
        dz_ref[1] = (dc0 * za * sb * (1.0 - sb)).astype(MXU_DTYPE)
        dbuf[pl.ds(ts, halo), :] = dbuf[pl.ds(0, halo), :]

    vec = pl.BlockSpec((1, C), lambda i: (0, 0))
    tapw = pl.BlockSpec((taps, C), lambda i: (0, 0))
    tile = lambda b: pl.BlockSpec((1, ts, C), lambda i: (b, n - 1 - i, 0))
    prev = lambda b: pl.BlockSpec((1, halo, C), lambda i: (b, jnp.maximum((n - 1 - i) * hb - 1, 0), 0))
    return _pcall(
        comm, body, name="conf_bwd", grid=(n,),
        in_specs=[pl.BlockSpec((ts, D), lambda i: (n - 1 - i, 0)), pl.BlockSpec((C, D), lambda i: (0, 0)),
                  tile(2), tile(3), prev(2), prev(3), pl.BlockSpec((ts, C), lambda i: (n - 1 - i, 0)),
                  tapw, vec, vec],
        out_specs=[pl.BlockSpec((2, ts, C), lambda i: (1, n - 1 - i, 0)), tapw, vec, vec, vec],
        out_shape=[jax.ShapeDtypeStruct((4, S, C), MXU_DTYPE), jax.ShapeDtypeStruct((taps, C), F32),
                   jax.ShapeDtypeStruct((1, C), F32), jax.ShapeDtypeStruct((1, C), F32),
                   jax.ShapeDtypeStruct((1, C), F32)],
        scratch_shapes=[pltpu.VMEM((ts + halo, C), F32), pltpu.VMEM((ts + halo, C), F32),
                        pltpu.VMEM((8, ts + halo - 8, C), F32)],
        semantics=("arbitrary",))(dx1, w_out_c, z, z, z, z, c1, conv_w, ln_g, ln_b)


def _lru_bwd(dx1, w_out_l, z, h, conv_w, conv_b, wa, ba, wx, bx, lam, dz, comm=None):
    _, S, C = z.shape
    D = dx1.shape[1]
    ts = _tile(S, 512)
    n = S // ts
    taps = conv_w.shape[0]
    halo = 8
    hb = ts // halo

    def body(dx_ref, wo_ref, zx_ref, zxh_ref, zg_ref, h_ref, hh_ref, cw_ref, cb_ref, wa_ref, ba_ref,
             wx_ref, bx_ref, lam_ref, dz_in,
             dz_ref, dwa_ref, dwx_ref, dba_ref, dbx_ref, dlam_ref, dcw_ref, dcb_ref,
             xbuf, hbuf, a_s, w_s, dh_s, g_s, dbuf, pc):
        i = pl.program_id(0)
        r = n - 1 - i

        @pl.when(i == 0)
        def _():
            for ref in (dwa_ref, dwx_ref, dba_ref, dbx_ref, dlam_ref, dcw_ref, dcb_ref, pc):
                ref[...] = jnp.zeros_like(ref)
            dbuf[pl.ds(ts, halo), :] = jnp.zeros((halo, C), F32)

        xbuf[pl.ds(0, halo), :] = jnp.where(r > 0, zxh_ref[0], 0.0)
        xbuf[pl.ds(halo, ts), :] = zx_ref[0]
        hbuf[pl.ds(0, halo), :] = jnp.where(r > 0, hh_ref[...], 0.0)
        hbuf[pl.ds(halo, ts), :] = h_ref[...]
        xs = _windows(xbuf, halo, taps, ts)
        xc = _causal_from(xs, cw_ref) + cb_ref[...]
        lam_v = lam_ref[...]
        sp = _softplus_neg(lam_v)
        rg, ig, a, mult, inv_mult = _lru_gates(xc, wa_ref, ba_ref, wx_ref, bx_ref, sp)

        dy = _mm_nt(dx_ref[...], wo_ref[...])
        ge, dge = _gelu(zg_ref[0])
        dh = dy * ge
        dz_ref[1] = (dy * h_ref[...] * dge).astype(MXU_DTYPE)
        a_s[...] = a
        w_s[...] = a * dh
        dh_s[...] = dh
        row = lax.broadcasted_iota(jnp.int32, (8, C), 0)

        def step(kk, carry):
            off = pl.multiple_of((ts // 8 - 1 - kk) * 8, 8)
            av = a_s[pl.ds(off, 8), :]
            wv = w_s[pl.ds(off, 8), :]
            for d in (1, 2, 4):
                m = row < 8 - d
                a_sh = jnp.where(m, pltpu.roll(av, 8 - d, 0), 1.0)
                w_sh = jnp.where(m, pltpu.roll(wv, 8 - d, 0), 0.0)
                wv = wv + av * w_sh
                av = av * a_sh
            pv = wv + av * carry
            g_s[pl.ds(off, 8), :] = dh_s[pl.ds(off, 8), :] + jnp.where(row < 7, pltpu.roll(pv, 7, 0), carry)
            return jnp.broadcast_to(pv[0:1, :], (8, C))

        pc[...] = lax.fori_loop(0, ts // 8, step, pc[...])
        gt = g_s[...]
        da = gt * hbuf[pl.ds(halo - 1, ts), :]
        gm = gt * mult
        dlog_a = da * a - (gt * ig * xc) * (a * a) * inv_mult
        dlam_ref[...] += _colsum(dlog_a * rg) * (RG_C / (1.0 + jnp.exp(lam_v)))
        dpa = (dlog_a * (-RG_C * sp)) * rg * (1.0 - rg)
        dpx = (gm * xc) * ig * (1.0 - ig)
        dba_ref[...] += _colsum(dpa)
        dbx_ref[...] += _colsum(dpx)
        xb = xc.astype(MXU_DTYPE)
        dpab, dpxb = dpa.astype(MXU_DTYPE), dpx.astype(MXU_DTYPE)
        dwa_ref[...] += _mm_tn(xb, dpab)
        dwx_ref[...] += _mm_tn(xb, dpxb)
        dxc = gm * ig + _mm_nt(dpab, wa_ref[...]) + _mm_nt(dpxb, wx_ref[...])
        dcb_ref[...] += _colsum(dxc)
        dbuf[pl.ds(0, ts), :] = dxc
        _tap_grads_from(dcw_ref, dxc, xs)
        dz_ref[0] = _anticausal_taps(dbuf, cw_ref, taps, ts).astype(MXU_DTYPE)
        dbuf[pl.ds(ts, halo), :] = dbuf[pl.ds(0, halo), :]

    vec = pl.BlockSpec((1, C), lambda i: (0, 0))
    mat = pl.BlockSpec((C, C), lambda i: (0, 0))
    tapw = pl.BlockSpec((taps, C), lambda i: (0, 0))
    prev_rows = lambda i: jnp.maximum((n - 1 - i) * hb - 1, 0)
    sds = jax.ShapeDtypeStruct
    return _pcall(
        comm, body, name="lru_bwd", grid=(n,),
        in_specs=[pl.BlockSpec((ts, D), lambda i: (n - 1 - i, 0)), pl.BlockSpec((C, D), lambda i: (0, 0)),
                  pl.BlockSpec((1, ts, C), lambda i: (0, n - 1 - i, 0)),
                  pl.BlockSpec((1, halo, C), lambda i: (0, prev_rows(i), 0)),
                  pl.BlockSpec((1, ts, C), lambda i: (1, n - 1 - i, 0)),
                  pl.BlockSpec((ts, C), lambda i: (n - 1 - i, 0)),
                  pl.BlockSpec((halo, C), lambda i: (prev_rows(i), 0)),
                  tapw, vec, mat, vec, mat, vec, vec, ANY],
        out_specs=[pl.BlockSpec((2, ts, C), lambda i: (0, n - 1 - i, 0)), mat, mat, vec, vec, vec, tapw, vec],
        out_shape=[sds(dz.shape, MXU_DTYPE), sds((C, C), F32), sds((C, C), F32), sds((1, C), F32),
                   sds((1, C), F32), sds((1, C), F32), sds((taps, C), F32), sds((1, C), F32)],
        scratch_shapes=[pltpu.VMEM((ts + halo, C), F32), pltpu.VMEM((ts + halo, C), F32)]
        + [pltpu.VMEM((ts, C), F32)] * 4 + [pltpu.VMEM((ts + halo, C), F32), pltpu.VMEM((8, C), F32)],
        aliases={14: 0},
        semantics=("arbitrary",))(dx1, w_out_l, z, z, z, h, h, conv_w, conv_b, wa, ba, wx, bx, lam, dz)


def _bwd_in(dz, w_in, x, g, dx1):
    S, D = x.shape
    nb, _, C = w_in.shape
    ts = _tile(S, 512)
    halves = 2 if ts % 32 == 0 else 1
    hr = ts // halves

    def body(dz_ref, w_ref, x_ref, g_ref, dx1_ref, dx_ref, dg_ref):
        i = pl.program_id(0)

        @pl.when(i == 0)
        def _():
            dg_ref[...] = jnp.zeros_like(dg_ref)

        def grad_h(k):
            rows = pl.ds(k * hr, hr)
            dh = _mm_nt(dz_ref[0, rows, :], w_ref[0])
            for j in range(1, nb):
                dh = dh + _mm_nt(dz_ref[j, rows, :], w_ref[j])
            return dh

        ahead = grad_h(0)
        for k in range(halves):
            rows, dh = pl.ds(k * hr, hr), ahead
            if k + 1 < halves:
                ahead = grad_h(k + 1)
            rinv, xhat = _rms(x_ref[rows, :])
            dg_ref[...] += _colsum(dh * xhat)
            dx_ref[rows, :] = dx1_ref[rows, :] + _rms_bwd(rinv, xhat, dh * g_ref[...])

    row = pl.BlockSpec((ts, D), lambda i: (i, 0))
    vecd = pl.BlockSpec((1, D), lambda i: (0, 0))
    return pl.pallas_call(
        body, name="bwd_in", grid=(S // ts,),
        in_specs=[pl.BlockSpec((nb, ts, C), lambda i: (0, i, 0)), pl.BlockSpec((nb, D, C), lambda i: (0, 0, 0)),
                  row, vecd, row],
        out_specs=[row, vecd],
        out_shape=[jax.ShapeDtypeStruct((S, D), F32), jax.ShapeDtypeStruct((1, D), F32)],
        compiler_params=_params("arbitrary"))(dz, w_in, x, g, dx1)


def _wgrad(a, b, name, comm=None):
    na, S, K = a.shape
    nb, _, N = b.shape
    nj = max(na, nb)
    assert min(na, nb) == 1
    ts = _tile(S, 2048)
    ns = S // ts
    grp = max(g for g in range(1, nj + 1) if nj % g == 0 and g * K * N * 4 <= WGRAD_ACC_BYTES)
    ga, gb = (grp if na > 1 else 1), (grp if nb > 1 else 1)

    def body(a_ref, b_ref, o_ref, acc):
        s = pl.program_id(1)

        @pl.when(s == 0)
        def _():
            acc[...] = jnp.zeros_like(acc)

        for k in range(grp):
            acc[k] += _mm_tn(a_ref[k if na > 1 else 0], b_ref[k if nb > 1 else 0])

        @pl.when(s == ns - 1)
        def _():
            o_ref[...] = acc[...].astype(o_ref.dtype)

    res = _pcall(
        comm, body, name=name, grid=(nj // grp, ns),
        in_specs=[pl.BlockSpec((ga, ts, K), (lambda j, s: (j, s, 0)) if na > 1 else (lambda j, s: (0, s, 0))),
                  pl.BlockSpec((gb, ts, N), (lambda j, s: (j, s, 0)) if nb > 1 else (lambda j, s: (0, s, 0)))],
        out_specs=pl.BlockSpec((grp, K, N), lambda j, s: (j, 0, 0)),
        out_shape=jax.ShapeDtypeStruct((nj, K, N), WIRE_DTYPE),
        scratch_shapes=[pltpu.VMEM((grp, K, N), F32)],
        semantics=("parallel", "arbitrary"))(a, b)
    return res[0] if comm is None else (res[0][0], res[1])


def _place():
    x, y, c = lax.axis_index("x"), lax.axis_index("y"), lax.axis_index("c")
    other_chips = [(1 - x, y), (x, 1 - y), (1 - x, 1 - y)]
    return x, y, c, other_chips


def _gather_weights(shards):
    nt = len(shards)

    def body(*refs):
        src, dst = refs[:nt], refs[nt:2 * nt]
        ici_send, ici_recv, d2d_send, d2d_recv, own_send, own_recv = refs[2 * nt:]
        x, y, c, chips = _place()
        mine = 2 * x + y

        def half(t, pc):
            hr = src[t].shape[0] // 2
            return pl.ds(pc * hr, hr)

        def own(t):
            return pltpu.make_async_remote_copy(
                src_ref=src[t], dst_ref=dst[t].at[mine], send_sem=own_send.at[t], recv_sem=own_recv.at[t],
                device_id=(x, y, 1 - c), device_id_type=MESH)

        def ici(t, k, block, to):
            cx, cy = block
            ref = dst[t].at[2 * cx + cy, half(t, c)]
            return pltpu.make_async_remote_copy(
                src_ref=src[t].at[half(t, c)] if to is not None else ref, dst_ref=ref,
                send_sem=ici_send.at[t, k], recv_sem=ici_recv.at[t, k],
                device_id=(*to, c) if to is not None else (x, y, c), device_id_type=MESH)

        def d2d(t, k, block, pc):
            cx, cy = block
            ref = dst[t].at[2 * cx + cy, half(t, pc)]
            return pltpu.make_async_remote_copy(
                src_ref=ref, dst_ref=ref, send_sem=d2d_send.at[t, k], recv_sem=d2d_recv.at[t, k],
                device_id=(x, y, 1 - c), device_id_type=MESH)

        sends = [ici(t, k, (x, y), chip) for t in range(nt) for k, chip in enumerate(chips)]
        sends += [own(t) for t in range(nt)]
        for cp in sends:
            cp.start()
        passed = []
        for t in range(nt):
            for k, chip in enumerate(chips):
                ici(t, k, chip, None).wait_recv()
                fw = d2d(t, k, chip, c)
                fw.start()
                passed.append(fw)
        for t in range(nt):
            own(t).wait_recv()
            for k, chip in enumerate(chips):
                d2d(t, k, chip, 1 - c).wait_recv()
        for cp in sends + passed:
            cp.wait_send()

    return pl.pallas_call(
        body, name="gather_weights",
        in_specs=[ANY] * nt, out_specs=[ANY] * nt,
        out_shape=[jax.ShapeDtypeStruct((N_CHIPS,) + s.shape, s.dtype) for s in shards],
        scratch_shapes=[pltpu.SemaphoreType.DMA((nt, 3))] * 4 + [pltpu.SemaphoreType.DMA((nt,))] * 2,
        compiler_params=pltpu.CompilerParams(has_side_effects=True))(*shards)


def _gather_over_ici(shards):
    nt = len(shards)

    def copies(src, dst, scr, arriving):
        ici_send, ici_recv, own_send, own_recv = scr
        x, y, c, chips = _place()
        out = []
        for t in range(nt):
            hr = src[t].shape[0] // 2
            rows = pl.ds(c * hr, hr)
            for k, (cx, cy) in enumerate(chips):
                block = 2 * cx + cy if arriving else 2 * x + y
                out.append(pltpu.make_async_remote_copy(
                    src_ref=src[t].at[rows], dst_ref=dst[t].at[block, rows],
                    send_sem=ici_send.at[t, k], recv_sem=ici_recv.at[t, k],
                    device_id=(cx, cy, c), device_id_type=MESH))
            out.append(pltpu.make_async_remote_copy(
                src_ref=src[t], dst_ref=dst[t].at[2 * x + y], send_sem=own_send.at[t], recv_sem=own_recv.at[t],
                device_id=(x, y, 1 - c), device_id_type=MESH))
        return out

    def start(src, dst, scr):
        for cp in copies(src, dst, scr, False):
            cp.start()

    def finish(src, dst, scr):
        for cp in copies(src, dst, scr, True):
            cp.wait_recv()
        for cp in copies(src, dst, scr, False):
            cp.wait_send()

    return _Comm(shards, [jax.ShapeDtypeStruct((N_CHIPS,) + s.shape, s.dtype) for s in shards],
                 [pltpu.SemaphoreType.DMA((nt, 3))] * 2 + [pltpu.SemaphoreType.DMA((nt,))] * 2, start, finish)


def _gather_pass_on(bufs):
    nt = len(bufs)

    def passed(dst, scr, t, k, block, pc):
        send, recv = scr
        x, y, c, _ = _place()
        cx, cy = block
        hr = dst[t].shape[1] // 2
        ref = dst[t].at[2 * cx + cy, pl.ds(pc * hr, hr)]
        return pltpu.make_async_remote_copy(src_ref=ref, dst_ref=ref, send_sem=send.at[t, k], recv_sem=recv.at[t, k],
                                            device_id=(x, y, 1 - c), device_id_type=MESH)

    def start(src, dst, scr):
        _, _, c, chips = _place()
        for t in range(nt):
            for k, chip in enumerate(chips):
                passed(dst, scr, t, k, chip, c).start()

    def finish(src, dst, scr):
        _, _, c, chips = _place()
        for t in range(nt):
            for k, chip in enumerate(chips):
                passed(dst, scr, t, k, chip, 1 - c).wait_recv()
        for t in range(nt):
            for k, chip in enumerate(chips):
                passed(dst, scr, t, k, chip, c).wait_send()

    return _Comm(bufs, [jax.ShapeDtypeStruct(b.shape, b.dtype) for b in bufs],
                 [pltpu.SemaphoreType.DMA((nt, 3))] * 2, start, finish, aliases={t: t for t in range(nt)})


def _exchange_halves(grads):
    nt = len(grads)

    def copies(src, dst, scr):
        send, recv = scr
        x, y, c, _ = _place()
        out = []
        for t in range(nt):
            hr = src[t].shape[1] // 2
            out.append(pltpu.make_async_remote_copy(
                src_ref=src[t].at[:, pl.ds((1 - c) * hr, hr)], dst_ref=dst[t],
                send_sem=send.at[t], recv_sem=recv.at[t], device_id=(x, y, 1 - c), device_id_type=MESH))
        return out

    def start(src, dst, scr):
        for cp in copies(src, dst, scr):
            cp.start()

    def finish(src, dst, scr):
        for cp in copies(src, dst, scr):
            cp.wait()

    return _Comm(grads, [jax.ShapeDtypeStruct((g.shape[0], g.shape[1] // 2, g.shape[2]), g.dtype) for g in grads],
                 [pltpu.SemaphoreType.DMA((nt,))] * 2, start, finish)


def _add_halves(grad, other, name):
    nb, R, C = grad.shape
    hr = R // 2
    tr = _tile(hr, 256, 16)
    steps = hr // tr
    c = lax.axis_index("c").astype(jnp.int32).reshape((1,))

    def body(c_ref, a_ref, b_ref, o_ref):
        o_ref[...] = (a_ref[...].astype(F32) + b_ref[...].astype(F32)).astype(o_ref.dtype)

    return pl.pallas_call(
        body, name=name,
        grid_spec=pltpu.PrefetchScalarGridSpec(
            num_scalar_prefetch=1, grid=(nb, steps),
            in_specs=[pl.BlockSpec((1, tr, C), lambda j, i, c_ref: (j, c_ref[0] * steps + i, 0)),
                      pl.BlockSpec((1, tr, C), lambda j, i, c_ref: (j, i, 0))],
            out_specs=pl.BlockSpec((1, tr, C), lambda j, i, c_ref: (j, i, 0))),
        out_shape=jax.ShapeDtypeStruct((nb, hr, C), grad.dtype),
        compiler_params=_params("parallel", "parallel"))(c, grad, other)


def _scatter_chip_sums(parts):
    nt = len(parts)

    def copies(src, dst, scr):
        send, recv = scr
        x, y, c, chips = _place()
        out = []
        for t in range(nt):
            for k, (cx, cy) in enumerate(chips):
                out.append(pltpu.make_async_remote_copy(
                    src_ref=src[t].at[2 * cx + cy], dst_ref=dst[t].at[k],
                    send_sem=send.at[t, k], recv_sem=recv.at[t, k], device_id=(cx, cy, c), device_id_type=MESH))
        return out

    def start(src, dst, scr):
        for cp in copies(src, dst, scr):
            cp.start()

    def finish(src, dst, scr):
        for cp in copies(src, dst, scr):
            cp.wait()

    return _Comm(parts, [jax.ShapeDtypeStruct((3,) + p.shape[1:], p.dtype) for p in parts],
                 [pltpu.SemaphoreType.DMA((nt, 3))] * 2, start, finish)


def _sum_chips(part, recv, name):
    _, hr, C = part.shape
    tr = _tile(hr, 256, 16)
    steps = hr // tr
    where = jnp.stack([2 * lax.axis_index("x") + lax.axis_index("y"), lax.axis_index("c")]).astype(jnp.int32)

    def body(w_ref, a_ref, b_ref, o_ref):
        acc = a_ref[0].astype(F32)
        for k in range(3):
            acc = acc + b_ref[k].astype(F32)
        o_ref[...] = acc

    return pl.pallas_call(
        body, name=name,
        grid_spec=pltpu.PrefetchScalarGridSpec(
            num_scalar_prefetch=1, grid=(steps,),
            in_specs=[pl.BlockSpec((1, tr, C), lambda i, w_ref: (w_ref[0], i, 0)),
                      pl.BlockSpec((3, tr, C), lambda i, w_ref: (0, i, 0))],
            out_specs=pl.BlockSpec((tr, C), lambda i, w_ref: (w_ref[1] * steps + i, 0))),
        out_shape=jax.ShapeDtypeStruct((2 * hr, C), F32),
        compiler_params=_params("parallel"))(where, part, recv)


def _join_halves(bufs):
    nt = len(bufs)

    def swap(dst, scr, t, pc):
        send, recv = scr
        x, y, c, _ = _place()
        hr = dst[t].shape[0] // 2
        rows = dst[t].at[pl.ds(pc * hr, hr)]
        return pltpu.make_async_remote_copy(src_ref=rows, dst_ref=rows, send_sem=send.at[t], recv_sem=recv.at[t],
                                            device_id=(x, y, 1 - c), device_id_type=MESH)

    def start(src, dst, scr):
        c = lax.axis_index("c")
        for t in range(nt):
            swap(dst, scr, t, c).start()

    def finish(src, dst, scr):
        c = lax.axis_index("c")
        for t in range(nt):
            swap(dst, scr, t, 1 - c).wait_recv()
        for t in range(nt):
            swap(dst, scr, t, c).wait_send()

    return _Comm(bufs, [jax.ShapeDtypeStruct(b.shape, b.dtype) for b in bufs],
                 [pltpu.SemaphoreType.DMA((nt,))] * 2, start, finish, aliases={t: t for t in range(nt)})


def _reduce_scatter_in_vmem(g):
    nb, R, C = g.shape
    hr = R // 2

    def run(ins, outs, scr):
        (g_ref,), (out_ref,) = ins, outs
        other, part, got, send, recv = scr
        x, y, c, chips = _place()
        sibling = (x, y, 1 - c)
        my_rows = pl.ds(pl.multiple_of(c * hr, hr), hr)
        their_rows = pl.ds(pl.multiple_of((1 - c) * hr, hr), hr)
        swap = pltpu.make_async_remote_copy(src_ref=g_ref.at[:, their_rows], dst_ref=other, send_sem=send.at[0],
                                            recv_sem=recv.at[0], device_id=sibling, device_id_type=MESH)
        swap.start()
        swap.wait()
        part[...] = (g_ref[:, my_rows, :].astype(F32) + other[...].astype(F32)).astype(part.dtype)
        to_owner = [pltpu.make_async_remote_copy(src_ref=part.at[2 * cx + cy], dst_ref=got.at[k],
                                                 send_sem=send.at[1 + k], recv_sem=recv.at[1 + k],
                                                 device_id=(cx, cy, c), device_id_type=MESH)
                    for k, (cx, cy) in enumerate(chips)]
        for cp in to_owner:
            cp.start()
        for cp in to_owner:
            cp.wait()
        total = part[2 * x + y].astype(F32)
        for k in range(3):
            total = total + got[k].astype(F32)
        out_ref[my_rows, :] = total

        def join(rows):
            return pltpu.make_async_remote_copy(src_ref=out_ref.at[rows], dst_ref=out_ref.at[rows], send_sem=send.at[4],
                                                recv_sem=recv.at[4], device_id=sibling, device_id_type=MESH)

        join(my_rows).start()
        join(their_rows).wait_recv()
        join(my_rows).wait_send()

    return _Comm([g], [jax.ShapeDtypeStruct((R, C), F32)],
                 [pltpu.VMEM((nb, hr, C), g.dtype), pltpu.VMEM((nb, hr, C), g.dtype), pltpu.VMEM((3, hr, C), g.dtype),
                  pltpu.SemaphoreType.DMA((5,)), pltpu.SemaphoreType.DMA((5,))],
                 run, lambda ins, outs, scr: None, in_specs=[WHOLE_VMEM], out_specs=[WHOLE_VMEM])


def _all_reduce_rows(buf, loss_row=None):
    R, L = buf.shape

    def copies(in_ref, gath, send, recv):
        x, y, c, _ = _place()
        out = []
        for k in range(1, N_DEV):
            peer = (x ^ ((k >> 2) & 1), y ^ ((k >> 1) & 1), c ^ (k & 1))
            out.append(pltpu.make_async_remote_copy(
                src_ref=in_ref, dst_ref=gath.at[k], send_sem=send.at[k - 1], recv_sem=recv.at[k - 1],
                device_id=peer, device_id_type=MESH))
        return out

    def start(ins, outs, scr):
        gath, send, recv = scr
        gath[0] = ins[0][...]
        for cp in copies(ins[0], gath, send, recv):
            cp.start()

    def finish(ins, outs, scr):
        gath, send, recv = scr
        for cp in copies(ins[0], gath, send, recv):
            cp.wait()
        x, y, c, _ = _place()
        me = 4 * x + 2 * y + c
        total = gath[me]
        for d in range(1, N_DEV):
            total = total + gath[d ^ me]
        outs[0][...] = total
        if loss_row is not None:
            outs[1][...] = jnp.sum(total[loss_row:loss_row + 1, :], axis=1, keepdims=True)

    out_shape = [jax.ShapeDtypeStruct((R, L), F32)]
    if loss_row is not None:
        out_shape.append(jax.ShapeDtypeStruct((1, 1), F32))
    return _Comm([buf], out_shape,
                 [pltpu.VMEM((N_DEV, R, L), F32), pltpu.SemaphoreType.DMA((N_DEV - 1,)),
                  pltpu.SemaphoreType.DMA((N_DEV - 1,))],
                 start, finish, in_specs=[WHOLE_VMEM], out_specs=[WHOLE_VMEM] * len(out_shape))


def _adamw_update(w_ref, g_ref, m_ref, v_ref, d_ref, nm_ref, nv_ref):
    gv = g_ref[...]
    nm = ADAM_B1 * m_ref[...] + (1.0 - ADAM_B1) * gv
    nv = ADAM_B2 * v_ref[...] + (1.0 - ADAM_B2) * (gv * gv)
    nm_ref[...] = nm
    nv_ref[...] = nv
    m_hat = nm / (1.0 - ADAM_B1 ** ADAM_STEP)
    v_hat = nv / (1.0 - ADAM_B2 ** ADAM_STEP)
    d_ref[...] = -ADAM_LR * (m_hat / (jnp.sqrt(v_hat) + ADAM_EPS) + ADAM_WD * w_ref[...])


def _adamw(w, g, m, v, name):
    R, C = w.shape
    tr = _tile(R, 256)

    def body(w_ref, g_ref, m_ref, v_ref, d_ref, nm_ref, nv_ref, g_out):
        _adamw_update(w_ref, g_ref, m_ref, v_ref, d_ref, nm_ref, nv_ref)
        g_out[...] = g_ref[...]

    blk = pl.BlockSpec((tr, C), lambda i: (i, 0))
    return pl.pallas_call(
        body, name=name, grid=(R // tr,), in_specs=[blk] * 4, out_specs=[blk] * 4,
        out_shape=[jax.ShapeDtypeStruct((R, C), F32)] * 4,
        compiler_params=_params("parallel"))(w, g, m, v)


def _adamw_many(ws, gs, ms, vs, name):
    n = len(ws)

    def body(*refs):
        for k in range(n):
            _adamw_update(*[refs[part * n + k] for part in range(7)])

    shapes = [jax.ShapeDtypeStruct(w.shape, F32) for w in ws]
    outs = pl.pallas_call(
        body, name=name, in_specs=[WHOLE_VMEM] * (4 * n), out_specs=[WHOLE_VMEM] * (3 * n), out_shape=shapes * 3,
        compiler_params=pltpu.CompilerParams(vmem_limit_bytes=VMEM_LIMIT_BYTES))(*ws, *gs, *ms, *vs)
    return outs[:n], outs[n:2 * n], outs[2 * n:]


def _pack_rows(arrays):
    rows = []
    for a in arrays:
        flat = a.reshape(-1).astype(F32)
        pad = (-flat.shape[0]) % LANES
        rows.append(jnp.pad(flat, (0, pad)).reshape(-1, LANES))
    buf = jnp.concatenate(rows, axis=0)
    return jnp.pad(buf, ((0, (-buf.shape[0]) % 8), (0, 0)))


def _unpack_rows(buf, shapes):
    out, r = [], 0
    for s in shapes:
        n = math.prod(s)
        nr = -(-n // LANES)
        out.append(buf[r:r + nr].reshape(-1)[:n].reshape(s))
        r += nr
    return out


def _block_diag(w):
    H, a, b = w.shape
    eye = jnp.eye(H, dtype=w.dtype)
    return (eye[:, None, :, None] * w[:, :, None, :]).reshape(H * a, H * b)


def _block_diag_parts(d, H):
    a, b = d.shape[0] // H, d.shape[1] // H
    d4 = d.reshape(H, a, H, b)
    return jnp.stack([d4[h, :, h, :] for h in range(H)])


def _rs_add(names, grads, others):
    return [_add_halves(g, o, "rs_add_halves_" + n) for n, g, o in zip(names, grads, others)]


def _rs_sum(names, parts, recvs):
    return [_sum_chips(p, r, "rs_sum_chips_" + n) for n, p, r in zip(names, parts, recvs)]


def _step(x, mem, target, shards, small, tap_rows, tap_shapes):
    D = x.shape[1]
    nch = N_CHIPS
    p = dict(small)

    (w_in_f,) = _gather_weights([shards['w_in']])
    wf = {}

    def ici(names):
        return _gather_over_ici([shards[n] for n in names])

    ici_a, taps_sum = ici(['w_out', 'w_q']), _all_reduce_rows(tap_rows)
    (z, h1), couts = _fwd_in(x, p['mix_norm_g'], w_in_f, comm=_merge(ici_a, taps_sum))
    bufs_a, (taps,) = _split(couts, ici_a, taps_sum)
    p.update(zip(COL_SHARDED_SMALL, _unpack_rows(taps, tap_shapes)))
    wa_d = _block_diag(p['lru_w_a']).astype(MXU_DTYPE)
    wx_d = _block_diag(p['lru_w_x']).astype(MXU_DTYPE)
    heads = p['lru_w_a'].shape[0]
    pass_a, ici_b = _gather_pass_on(bufs_a), ici(['w_kv', 'w_o'])
    (h, y_lru), couts = _lru_fwd(z, p['lru_conv_w'], p['lru_conv_b'], wa_d, p['lru_b_a'], wx_d, p['lru_b_x'],
                                 p['lru_lambda'], comm=_merge(pass_a, ici_b))
    (wf['w_out'], wf['w_q']), bufs_b = _split(couts, pass_a, ici_b)
    pass_b, ici_c = _gather_pass_on(bufs_b), ici(['w_up'])
    (c1, c3), couts = _conf_fwd(z, p['conf_conv_w'], p['conf_conv_b'], p['conf_ln_g'], p['conf_ln_b'],
                                comm=_merge(pass_b, ici_c))
    (wf['w_kv'], wf['w_o']), bufs_c = _split(couts, pass_b, ici_c)
    w_out2 = wf['w_out'].reshape(2, -1, D)
    w_q = wf['w_q'].reshape(D, D)
    w_o = wf['w_o'].reshape(D, D)
    pass_c, ici_d = _gather_pass_on(bufs_c), ici(['w_down'])
    (x1, h2, q), couts = _fwd_out_q(x, y_lru, c3, w_out2, p['xa_norm_g'], w_q, comm=_merge(pass_c, ici_d))
    (wf['w_up'],), bufs_d = _split(couts, pass_c, ici_d)
    m, kv = _kv_fwd(mem, p['mem_norm_g'], wf['w_kv'])
    (o, x2, h3), (wf['w_down'],) = _attn_fwd(q, kv, x1, w_o, p['ffn_norm_g'], comm=_gather_pass_on(bufs_d))
    gu, act, dx3, loss_lanes, d_final_g = _ffn_fwd(h3, wf['w_up'], p['ffn_conv_w'], p['ffn_conv_b'], wf['w_down'],
                                                   x2, p['final_norm_g'], target)

    dgu, dx2, d_ffn_g, d_ffn_cw, d_ffn_cb = _ffn_bwd(dx3, wf['w_down'], wf['w_up'], gu, x2, p['ffn_norm_g'],
                                                     p['ffn_conv_w'], p['ffn_conv_b'])
    g_down = _wgrad(act, dx3[None], "wgrad_down").reshape(nch, -1, D)
    g_up, other = _wgrad(h3[None], dgu, "wgrad_up", comm=_exchange_halves([g_down]))
    (p_down,) = _rs_add(['w_down'], [g_down], other)
    sc_down, ex_up = _scatter_chip_sums([p_down]), _exchange_halves([g_up])
    (dq, dx1, dkv, d_xa_g), couts = _attn_bwd(dx2, w_o, q, kv, x1, p['xa_norm_g'], w_q, comm=_merge(sc_down, ex_up))
    recv, other = _split(couts, sc_down, ex_up)
    f_down = _rs_sum(['w_down'], [p_down], recv)
    (p_up,) = _rs_add(['w_up'], [g_up], other)
    mid = ['w_o', 'w_q', 'w_kv']
    g_o = _wgrad(o[None], dx2[None], "wgrad_o").reshape(nch, -1, D)
    g_q = _wgrad(h2[None], dq[None], "wgrad_q").reshape(nch, -1, D)
    g_kv, d_mem_g = _kv_bwd(dkv, wf['w_kv'], mem, p['mem_norm_g'], m)
    join_down, sc_up, ex_mid = _join_halves(f_down), _scatter_chip_sums([p_up]), _exchange_halves([g_o, g_q, g_kv])
    (dz_c, d_conf_cw, d_conf_cb, d_ln_g, d_ln_b), couts = _conf_bwd(
        dx1, w_out2[1], z, c1, p['conf_conv_w'], p['conf_ln_g'], p['conf_ln_b'],
        comm=_merge(join_down, sc_up, ex_mid))
    (r_down,), recv, other = _split(couts, join_down, sc_up, ex_mid)
    p_up = [p_up]
    p_mid = _rs_add(mid, [g_o, g_q, g_kv], other)
    join_up, sc_mid = _join_halves(_rs_sum(['w_up'], p_up, recv)), _scatter_chip_sums(p_mid)
    (dz, d_wa, d_wx, d_ba, d_bx, d_lam, d_lru_cw, d_lru_cb), couts = _lru_bwd(
        dx1, w_out2[0], z, h, p['lru_conv_w'], p['lru_conv_b'], wa_d, p['lru_b_a'], wx_d, p['lru_b_x'],
        p['lru_lambda'], dz_c, comm=_merge(join_up, sc_mid))
    (r_up,), recv = _split(couts, join_up, sc_mid)
    f_mid = _rs_sum(mid, p_mid, recv)
    grad_x, d_mix_g = _bwd_in(dz, w_in_f, x, p['mix_norm_g'], dx1)

    small_g = {'mix_norm_g': d_mix_g, 'lru_conv_w': d_lru_cw, 'lru_conv_b': d_lru_cb,
               'lru_w_a': _block_diag_parts(d_wa, heads), 'lru_b_a': d_ba,
               'lru_w_x': _block_diag_parts(d_wx, heads), 'lru_b_x': d_bx, 'lru_lambda': d_lam,
               'conf_conv_w': d_conf_cw, 'conf_conv_b': d_conf_cb, 'conf_ln_g': d_ln_g, 'conf_ln_b': d_ln_b,
               'xa_norm_g': d_xa_g, 'mem_norm_g': d_mem_g, 'ffn_norm_g': d_ffn_g,
               'ffn_conv_w': d_ffn_cw, 'ffn_conv_b': d_ffn_cb, 'final_norm_g': d_final_g}
    names = list(small_g)
    shapes = [small_g[n].shape for n in names]
    join_mid = _join_halves(f_mid)
    small_sum = _all_reduce_rows(_pack_rows([loss_lanes] + [small_g[n] for n in names]), loss_row=0)
    g_in, couts = _wgrad(h1[None], dz, "wgrad_in", comm=_merge(join_mid, small_sum))
    r_mid, (summed, loss) = _split(couts, join_mid, small_sum)
    g_out_l, other = _wgrad(y_lru[None], dx1[None], "wgrad_out_lru", comm=_exchange_halves([g_in]))
    p_in = _rs_add(['w_in'], [g_in], other)
    g_out_c, recv = _wgrad(c3[None], dx1[None], "wgrad_out_conf", comm=_scatter_chip_sums(p_in))
    f_in = _rs_sum(['w_in'], p_in, recv)

    g_out = jnp.concatenate([g_out_l, g_out_c], axis=0).reshape(nch, -1, D)
    join_in, rs_out = _join_halves(f_in), _reduce_scatter_in_vmem(g_out)
    (r_in,), (r_out,) = _split(_run_comm(_merge(join_in, rs_out), "rs_last"), join_in, rs_out)
    big = dict(zip(['w_down', 'w_up'] + mid + ['w_out', 'w_in'], [r_down, r_up] + r_mid + [r_out, r_in]))
    return grad_x, big, summed, loss, names, [loss_lanes.shape] + shapes


def kernel(x, mem, mix_norm_g, w_in, lru_conv_w, lru_conv_b, lru_w_a, lru_b_a, lru_w_x, lru_b_x, lru_lambda, conf_conv_w, conf_conv_b, conf_ln_g, conf_ln_b, w_out, xa_norm_g, mem_norm_g, w_q, w_kv, w_o, ffn_norm_g, w_up, ffn_conv_w, ffn_conv_b, w_down, final_norm_g, loss_target, m_mix_norm_g, m_w_in, m_lru_conv_w, m_lru_conv_b, m_lru_w_a, m_lru_b_a, m_lru_w_x, m_lru_b_x, m_lru_lambda, m_conf_conv_w, m_conf_conv_b, m_conf_ln_g, m_conf_ln_b, m_w_out, m_xa_norm_g, m_mem_norm_g, m_w_q, m_w_kv, m_w_o, m_ffn_norm_g, m_w_up, m_ffn_conv_w, m_ffn_conv_b, m_w_down, m_final_norm_g, v_mix_norm_g, v_w_in, v_lru_conv_w, v_lru_conv_b, v_lru_w_a, v_lru_b_a, v_lru_w_x, v_lru_b_x, v_lru_lambda, v_conf_conv_w, v_conf_conv_b, v_conf_ln_g, v_conf_ln_b, v_w_out, v_xa_norm_g, v_mem_norm_g, v_w_q, v_w_kv, v_w_o, v_ffn_norm_g, v_w_up, v_ffn_conv_w, v_ffn_conv_b, v_w_down, v_final_norm_g):
    given = dict(locals())
    w = {n: given[n] for n in WEIGHTS}
    mom = {n: given["m_" + n] for n in WEIGHTS}
    var = {n: given["v_" + n] for n in WEIGHTS}
    xi, yi, ci = lax.axis_index("x"), lax.axis_index("y"), lax.axis_index("c")
    chip = 2 * xi + yi

    shards = {n: w[n][0].astype(WIRE_DTYPE) for n in BIG}
    tap_full = []
    for n in COL_SHARDED_SMALL:
        s = w[n][0]
        full = jnp.zeros((s.shape[0], N_CHIPS * s.shape[1]), F32)
        s = jnp.where(ci == 0, s, jnp.zeros_like(s))
        tap_full.append(lax.dynamic_update_slice(full, s, (0, chip * s.shape[1])))
    small = {n: (w[n] if w[n].ndim == 1 else w[n][0]) for n in SMALL if n not in COL_SHARDED_SMALL}
    small = {n: (a.reshape(1, -1) if a.ndim == 1 else a) for n, a in small.items()}

    grad_x, big_g, summed, loss, small_names, packed_shapes = _step(
        x[0], mem[0], loss_target[0], shards, small, _pack_rows(tap_full), [t.shape for t in tap_full])
    small_sum = dict(zip(small_names, _unpack_rows(summed, packed_shapes)[1:]))

    grads = {}
    for n in WEIGHTS:
        if n in BIG:
            g = big_g[n]
        elif n in COL_SHARDED_SMALL:
            width = w[n].shape[-1]
            g = lax.dynamic_slice_in_dim(small_sum[n], chip * width, width, axis=1)
        else:
            g = small_sum[n]
        grads[n] = g.reshape(w[n].shape)

    delta, new_m, new_v = {}, {}, {}
    for n in BIG:
        d, nm, nv, g = _adamw(w[n][0], grads[n][0], mom[n][0], var[n][0], "adamw_" + n)
        delta[n], new_m[n], new_v[n], grads[n] = d[None], nm[None], nv[None], g[None]
    flat = lambda a: a.reshape(-1, a.shape[-1])
    outs = _adamw_many(*[[flat(src[n]) for n in SMALL] for src in (w, grads, mom, var)], "adamw_small")
    for out, arrays in zip((delta, new_m, new_v), outs):
        out.update({n: a.reshape(w[n].shape) for n, a in zip(SMALL, arrays)})

    return (loss[0, 0], grad_x[None], *[grads[n] for n in WEIGHTS], *[delta[n] for n in WEIGHTS],
            *[new_m[n] for n in WEIGHTS], *[new_v[n] for n in WEIGHTS])
```

```python
import math

import jax
import jax.numpy as jnp
from jax import lax
from jax.experimental import pallas as pl
from jax.experimental.pallas import tpu as pltpu

F32 = jnp.float32
MXU_DTYPE = jnp.bfloat16
WIRE_DTYPE = jnp.bfloat16
EPS = 1e-6
RG_C = 8.0
XA_HEADS = 4
ADAM_LR, ADAM_B1, ADAM_B2, ADAM_EPS, ADAM_WD, ADAM_STEP = 0.001, 0.9, 0.999, 1e-08, 0.01, 10
VMEM_LIMIT_BYTES = 52 * 1024 * 1024
WGRAD_ACC_BYTES = 8 * 1024 * 1024
LANES = 1024
N_CHIPS = 4
N_DEV = 8
MESH = pl.DeviceIdType.MESH
GELU_C = math.sqrt(2.0 / math.pi)
GELU_K = 0.044715

WEIGHTS = ['mix_norm_g', 'w_in', 'lru_conv_w', 'lru_conv_b', 'lru_w_a', 'lru_b_a', 'lru_w_x', 'lru_b_x',
           'lru_lambda', 'conf_conv_w', 'conf_conv_b', 'conf_ln_g', 'conf_ln_b', 'w_out', 'xa_norm_g',
           'mem_norm_g', 'w_q', 'w_kv', 'w_o', 'ffn_norm_g', 'w_up', 'ffn_conv_w', 'ffn_conv_b', 'w_down',
           'final_norm_g']
BIG = ['w_in', 'w_kv', 'w_up', 'w_out', 'w_q', 'w_o', 'w_down']
SMALL = [n for n in WEIGHTS if n not in BIG]
COL_SHARDED_SMALL = ['lru_conv_w', 'conf_conv_w', 'ffn_conv_w']


def _params(*semantics):
    return pltpu.CompilerParams(dimension_semantics=semantics, vmem_limit_bytes=VMEM_LIMIT_BYTES)


ANY = pl.BlockSpec(memory_space=pl.ANY)
WHOLE_VMEM = pl.BlockSpec(memory_space=pltpu.VMEM)


class _Comm:
    def __init__(self, arrays, out_shapes, scratch, start, finish, aliases=None, in_specs=None, out_specs=None):
        self.arrays, self.out_shapes, self.scratch = list(arrays), list(out_shapes), list(scratch)
        self.start, self.finish = start, finish
        self.aliases = dict(aliases or {})
        self.in_specs = list(in_specs) if in_specs is not None else [ANY] * len(self.arrays)
        self.out_specs = list(out_specs) if out_specs is not None else [ANY] * len(self.out_shapes)


def _merge(*comms):
    comms = [c for c in comms if c is not None]
    if not comms:
        return None
    ai = [0]
    for c in comms:
        ai.append(ai[-1] + len(c.arrays))
    oi = [0]
    for c in comms:
        oi.append(oi[-1] + len(c.out_shapes))
    si = [0]
    for c in comms:
        si.append(si[-1] + len(c.scratch))

    def each(which):
        def run(ins, outs, scr):
            for k, c in enumerate(comms):
                getattr(c, which)(ins[ai[k]:ai[k + 1]], outs[oi[k]:oi[k + 1]], scr[si[k]:si[k + 1]])
        return run

    aliases = {ai[k] + i: oi[k] + o for k, c in enumerate(comms) for i, o in c.aliases.items()}
    return _Comm(sum((c.arrays for c in comms), []), sum((c.out_shapes for c in comms), []),
                 sum((c.scratch for c in comms), []), each("start"), each("finish"), aliases,
                 sum((c.in_specs for c in comms), []), sum((c.out_specs for c in comms), []))


def _split(outs, *comms):
    parts, at = [], 0
    for c in comms:
        parts.append(outs[at:at + len(c.out_shapes)])
        at += len(c.out_shapes)
    return parts


def _pcall(comm, body, *, name, grid, in_specs, out_specs, out_shape, semantics, scratch_shapes=(), aliases=None):
    single = not isinstance(out_shape, (list, tuple))
    out_shape = [out_shape] if single else list(out_shape)
    out_specs = [out_specs] if single else list(out_specs)
    in_specs, scratch_shapes = list(in_specs), list(scratch_shapes)
    aliases = dict(aliases or {})

    if comm is None:
        def plain(*args):
            return list(pl.pallas_call(body, name=name, grid=grid, in_specs=in_specs, out_specs=out_specs,
                                       out_shape=out_shape, scratch_shapes=scratch_shapes,
                                       input_output_aliases=aliases,
                                       compiler_params=_params(*semantics))(*args))
        return plain

    def hosted(*args):
        n_in, n_out, n_scr = len(args), len(out_shape), len(scratch_shapes)
        c_in, c_out = len(comm.arrays), len(comm.out_shapes)

        def wrapped(*refs):
            ins, cins = refs[:n_in], refs[n_in:n_in + c_in]
            o0 = n_in + c_in
            outs, couts = refs[o0:o0 + n_out], refs[o0 + n_out:o0 + n_out + c_out]
            s0 = o0 + n_out + c_out
            scr, cscr = refs[s0:s0 + n_scr], refs[s0 + n_scr:]
            first = last = None
            for axis, size in enumerate(grid):
                at_start, at_end = pl.program_id(axis) == 0, pl.program_id(axis) == size - 1
                first = at_start if first is None else first & at_start
                last = at_end if last is None else last & at_end
            if first is None:
                comm.start(cins, couts, cscr)
                body(*ins, *outs, *scr)
                comm.finish(cins, couts, cscr)
                return
            pl.when(first)(lambda: comm.start(cins, couts, cscr))
            body(*ins, *outs, *scr)
            pl.when(last)(lambda: comm.finish(cins, couts, cscr))

        res = pl.pallas_call(
            wrapped, name=name, grid=grid, in_specs=in_specs + comm.in_specs, out_specs=out_specs + comm.out_specs,
            out_shape=out_shape + comm.out_shapes, scratch_shapes=scratch_shapes + comm.scratch,
            input_output_aliases={**aliases, **{n_in + i: n_out + o for i, o in comm.aliases.items()}},
            compiler_params=pltpu.CompilerParams(dimension_semantics=("arbitrary",) * len(grid),
                                                 vmem_limit_bytes=VMEM_LIMIT_BYTES, has_side_effects=True),
        )(*args, *comm.arrays)
        return list(res[:n_out]), list(res[n_out:])

    return hosted


def _run_comm(comm, name):
    return _pcall(comm, lambda: None, name=name, grid=(), in_specs=[], out_specs=[], out_shape=[], semantics=())()[1]


def _tile(n, want, align=8):
    if n <= want:
        return n
    for t in range(want - want % align, 0, -align):
        if n % t == 0:
            return t
    raise ValueError((n, want, align))


def _mm(a, b):
    return jnp.dot(a.astype(MXU_DTYPE), b.astype(MXU_DTYPE), preferred_element_type=F32)


def _mm_nt(a, b):
    return lax.dot_general(a.astype(MXU_DTYPE), b.astype(MXU_DTYPE), (((1,), (1,)), ((), ())),
                           preferred_element_type=F32)


def _mm_tn(a, b):
    return lax.dot_general(a.astype(MXU_DTYPE), b.astype(MXU_DTYPE), (((0,), (0,)), ((), ())),
                           preferred_element_type=F32)


def _sigmoid(v):
    return 0.5 * jnp.tanh(0.5 * v) + 0.5


def _gelu(v):
    v2 = v * v
    t = jnp.tanh(v * (GELU_C + (GELU_C * GELU_K) * v2))
    hv = 0.5 * v
    dt = (1.0 - t * t) * (GELU_C + (3.0 * GELU_C * GELU_K) * v2)
    return hv + hv * t, (0.5 + 0.5 * t) + hv * dt


def _softplus_neg(lam):
    e = jnp.exp(-jnp.abs(lam))
    u = 1.0 + e
    log1p_e = jnp.where(u == 1.0, e, jnp.log(u) * e / jnp.where(u == 1.0, 1.0, u - 1.0))
    return jnp.maximum(-lam, 0.0) + log1p_e


def _rms(xv):
    rinv = lax.rsqrt(jnp.mean(xv * xv, axis=-1, keepdims=True) + EPS)
    return rinv, xv * rinv


def _rms_bwd(rinv, xhat, dxhat):
    return rinv * (dxhat - xhat * jnp.mean(dxhat * xhat, axis=-1, keepdims=True))


def _colsum(v):
    return jnp.sum(v, axis=0, keepdims=True)


def _wrow(w_ref, k, wcols):
    return w_ref[pl.ds(k, 1), :] if wcols is None else w_ref[pl.ds(k, 1), wcols]


def _windows(buf_ref, halo, taps, rows):
    assert taps <= 8 <= halo
    x = buf_ref[pl.ds(halo - 8, rows + 8), :]
    return [x[8:] if s == 0 else pltpu.roll(x, s, 0)[8:] for s in range(taps)]


def _causal_from(xs, w_ref, wcols=None):
    taps = len(xs)
    acc = None
    for s in range(taps):
        term = _wrow(w_ref, taps - 1 - s, wcols) * xs[s]
        acc = term if acc is None else acc + term
    return acc


def _tap_grads_from(dw_ref, dy, xs, wcols=None):
    taps = len(xs)
    for s in range(taps):
        g = _colsum(dy * xs[s])
        if wcols is None:
            dw_ref[pl.ds(taps - 1 - s, 1), :] += g
        else:
            dw_ref[pl.ds(taps - 1 - s, 1), wcols] += g


def _causal_taps(buf_ref, halo, w_ref, taps, rows, wcols=None):
    return _causal_from(_windows(buf_ref, halo, taps, rows), w_ref, wcols)


def _anticausal_taps(buf_ref, w_ref, taps, rows, wcols=None):
    assert taps <= 8
    x = buf_ref[pl.ds(0, rows + 8), :]
    acc = None
    for s in range(taps):
        win = x[:rows] if s == 0 else pltpu.roll(x, rows + 8 - s, 0)[:rows]
        term = _wrow(w_ref, taps - 1 - s, wcols) * win
        acc = term if acc is None else acc + term
    return acc


def _shift_copies(dst_ref, buf_ref, rows, up):
    x = buf_ref[pl.ds(0, rows + 8), :]
    for r in range(8):
        if up:
            dst_ref[r] = x[:rows] if r == 0 else pltpu.roll(x, rows + 8 - r, 0)[:rows]
        else:
            dst_ref[r] = x[8:] if r == 0 else pltpu.roll(x, r, 0)[8:]


def _causal_taps8(sh_ref, halo, w_ref, taps, rows):
    acc = None
    for s in range(taps):
        term = _wrow(w_ref, taps - 1 - s, None) * sh_ref[s % 8, pl.ds(halo - 8 - 8 * (s // 8), rows), :]
        acc = term if acc is None else acc + term
    return acc


def _anticausal_taps8(sh_ref, w_ref, taps, rows):
    acc = None
    for s in range(taps):
        term = _wrow(w_ref, taps - 1 - s, None) * sh_ref[s % 8, pl.ds(8 * (s // 8), rows), :]
        acc = term if acc is None else acc + term
    return acc


def _tap_grads8(dw_ref, dy, sh_ref, halo, taps, rows):
    for s in range(taps):
        dw_ref[pl.ds(taps - 1 - s, 1), :] += _colsum(dy * sh_ref[s % 8, pl.ds(halo - 8 - 8 * (s // 8), rows), :])


def _fwd_in(x, g, w_in, comm=None):
    S, D = x.shape
    nb, _, C = w_in.shape
    ts = _tile(S, 1024)

    halves = 2 if ts % 32 == 0 else 1
    hr = ts // halves

    def body(x_ref, g_ref, w_ref, z_ref, h_ref):
        def norm(k):
            rows = pl.ds(k * hr, hr)
            _, xhat = _rms(x_ref[rows, :])
            h = (xhat * g_ref[...]).astype(MXU_DTYPE)
            h_ref[rows, :] = h
            return h

        h_next = norm(0)
        for k in range(halves):
            rows, h = pl.ds(k * hr, hr), h_next
            if k + 1 < halves:
                h_next = norm(k + 1)
            for j in range(nb):
                z_ref[j, rows, :] = jnp.dot(h, w_ref[j], preferred_element_type=F32)

    return _pcall(
        comm, body, name="fwd_in", grid=(S // ts,),
        in_specs=[pl.BlockSpec((ts, D), lambda i: (i, 0)), pl.BlockSpec((1, D), lambda i: (0, 0)),
                  pl.BlockSpec((nb, D, C), lambda i: (0, 0, 0))],
        out_specs=[pl.BlockSpec((nb, ts, C), lambda i: (0, i, 0)), pl.BlockSpec((ts, D), lambda i: (i, 0))],
        out_shape=[jax.ShapeDtypeStruct((nb, S, C), F32), jax.ShapeDtypeStruct((S, D), MXU_DTYPE)],
        semantics=("parallel",))(x, g, w_in)


def _lru_gates(xc, wa_ref, ba_ref, wx_ref, bx_ref, sp):
    xb = xc.astype(MXU_DTYPE)
    r = _sigmoid(jnp.dot(xb, wa_ref[...], preferred_element_type=F32) + ba_ref[...])
    ig = _sigmoid(jnp.dot(xb, wx_ref[...], preferred_element_type=F32) + bx_ref[...])
    log_a = -RG_C * r * sp
    a = jnp.exp(log_a)
    one_minus_a2 = jnp.tanh(-log_a) * (a * a + 1.0)
    inv_mult = jnp.where(one_minus_a2 > 0.0, lax.rsqrt(one_minus_a2), 0.0)
    mult = one_minus_a2 * inv_mult
    return r, ig, a, mult, inv_mult


def _lru_fwd(z, conv_w, conv_b, wa, ba, wx, bx, lam, comm=None):
    _, S, C = z.shape
    ts = _tile(S, 512)
    taps = conv_w.shape[0]
    halo = 8

    def body(zx_ref, zg_ref, cw_ref, cb_ref, wa_ref, ba_ref, wx_ref, bx_ref, lam_ref,
             h_ref, y_ref, xbuf, a_s, u_s, hc):
        i = pl.program_id(0)

        @pl.when(i == 0)
        def _():
            xbuf[pl.ds(0, halo), :] = jnp.zeros((halo, C), F32)
            hc[...] = jnp.zeros_like(hc)

        xbuf[pl.ds(halo, ts), :] = zx_ref[0]
        xc = _causal_taps(xbuf, halo, cw_ref, taps, ts) + cb_ref[...]
        sp = _softplus_neg(lam_ref[...])
        _, ig, a, mult, _ = _lru_gates(xc, wa_ref, ba_ref, wx_ref, bx_ref, sp)
        a_s[...] = a
        u_s[...] = mult * (ig * xc)
        row = lax.broadcasted_iota(jnp.int32, (8, C), 0)

        def step(k, carry):
            off = pl.multiple_of(k * 8, 8)
            av = a_s[pl.ds(off, 8), :]
            uv = u_s[pl.ds(off, 8), :]
            for d in (1, 2, 4):
                m = row >= d
                a_sh = jnp.where(m, pltpu.roll(av, d, 0), 1.0)
                u_sh = jnp.where(m, pltpu.roll(uv, d, 0), 0.0)
                uv = uv + av * u_sh
                av = av * a_sh
            hv = uv + av * carry
            h_ref[pl.ds(off, 8), :] = hv
            return jnp.broadcast_to(hv[7:8, :], (8, C))

        hc[...] = lax.fori_loop(0, ts // 8, step, hc[...])
        ge, _ = _gelu(zg_ref[0])
        y_ref[...] = (h_ref[...] * ge).astype(MXU_DTYPE)
        xbuf[pl.ds(0, halo), :] = xbuf[pl.ds(ts, halo), :]

    vec = pl.BlockSpec((1, C), lambda i: (0, 0))
    mat = pl.BlockSpec((C, C), lambda i: (0, 0))
    return _pcall(
        comm, body, name="lru_fwd", grid=(S // ts,),
        in_specs=[pl.BlockSpec((1, ts, C), lambda i: (0, i, 0)), pl.BlockSpec((1, ts, C), lambda i: (1, i, 0)),
                  pl.BlockSpec((taps, C), lambda i: (0, 0)), vec, mat, vec, mat, vec, vec],
        out_specs=[pl.BlockSpec((ts, C), lambda i: (i, 0)), pl.BlockSpec((ts, C), lambda i: (i, 0))],
        out_shape=[jax.ShapeDtypeStruct((S, C), F32), jax.ShapeDtypeStruct((S, C), MXU_DTYPE)],
        scratch_shapes=[pltpu.VMEM((ts + halo, C), F32), pltpu.VMEM((ts, C), F32), pltpu.VMEM((ts, C), F32),
                        pltpu.VMEM((8, C), F32)],
        semantics=("arbitrary",))(z, z, conv_w, conv_b, wa, ba, wx, bx, lam)


def _layer_norm_stats(c1):
    mu = jnp.mean(c1, axis=-1, keepdims=True)
    xc = c1 - mu
    rstd = lax.rsqrt(jnp.mean(xc * xc, axis=-1, keepdims=True) + EPS)
    return rstd, xc * rstd


def _conf_fwd(z, conv_w, conv_b, ln_g, ln_b, comm=None):
    _, S, C = z.shape
    ts = _tile(S, 512)
    taps = conv_w.shape[0]
    halo = 32

    def body(za_ref, zb_ref, cw_ref, cb_ref, g_ref, b_ref, c1_ref, c3_ref, cbuf, shifted):
        i = pl.program_id(0)

        @pl.when(i == 0)
        def _():
            cbuf[pl.ds(0, halo), :] = jnp.zeros((halo, C), F32)

        cbuf[pl.ds(halo, ts), :] = za_ref[0] * _sigmoid(zb_ref[0])
        _shift_copies(shifted, cbuf, ts + halo - 8, up=False)
        c1 = _causal_taps8(shifted, halo, cw_ref, taps, ts) + cb_ref[...]
        c1_ref[...] = c1
        _, xhat = _layer_norm_stats(c1)
        c2 = xhat * g_ref[...] + b_ref[...]
        c3_ref[...] = (c2 * _sigmoid(c2)).astype(MXU_DTYPE)
        cbuf[pl.ds(0, halo), :] = cbuf[pl.ds(ts, halo), :]

    vec = pl.BlockSpec((1, C), lambda i: (0, 0))
    return _pcall(
        comm, body, name="conf_fwd", grid=(S // ts,),
        in_specs=[pl.BlockSpec((1, ts, C), lambda i: (2, i, 0)), pl.BlockSpec((1, ts, C), lambda i: (3, i, 0)),
                  pl.BlockSpec((taps, C), lambda i: (0, 0)), vec, vec, vec],
        out_specs=[pl.BlockSpec((ts, C), lambda i: (i, 0)), pl.BlockSpec((ts, C), lambda i: (i, 0))],
        out_shape=[jax.ShapeDtypeStruct((S, C), F32), jax.ShapeDtypeStruct((S, C), MXU_DTYPE)],
        scratch_shapes=[pltpu.VMEM((ts + halo, C), F32), pltpu.VMEM((8, ts + halo - 8, C), F32)],
        semantics=("arbitrary",))(z, z, conv_w, conv_b, ln_g, ln_b)


def _fwd_out_q(x, y_lru, c3, w_out, g_xa, w_q, comm=None):
    S, D = x.shape
    C = y_lru.shape[1]
    ts = _tile(S, 1024)

    halves = 2 if ts % 32 == 0 else 1
    hr = ts // halves

    def body(x_ref, yl_ref, c3_ref, wo_ref, g_ref, wq_ref, x1_ref, h2_ref, q_ref):
        def mixed(k):
            rows = pl.ds(k * hr, hr)
            return (jnp.dot(yl_ref[rows, :], wo_ref[0], preferred_element_type=F32)
                    + jnp.dot(c3_ref[rows, :], wo_ref[1], preferred_element_type=F32))

        y_next = mixed(0)
        for k in range(halves):
            rows, y = pl.ds(k * hr, hr), y_next
            if k + 1 < halves:
                y_next = mixed(k + 1)
            x1 = x_ref[rows, :] + y
            x1_ref[rows, :] = x1
            _, xhat = _rms(x1)
            h2 = (xhat * g_ref[...]).astype(MXU_DTYPE)
            h2_ref[rows, :] = h2
            q_ref[rows, :] = jnp.dot(h2, wq_ref[...], preferred_element_type=F32).astype(MXU_DTYPE)

    row = lambda w: pl.BlockSpec((ts, w), lambda i: (i, 0))
    return _pcall(
        comm, body, name="fwd_out_q", grid=(S // ts,),
        in_specs=[row(D), row(C), row(C), pl.BlockSpec((2, C, D), lambda i: (0, 0, 0)),
                  pl.BlockSpec((1, D), lambda i: (0, 0)), pl.BlockSpec((D, D), lambda i: (0, 0))],
        out_specs=[row(D), row(D), row(D)],
        out_shape=[jax.ShapeDtypeStruct((S, D), F32), jax.ShapeDtypeStruct((S, D), MXU_DTYPE),
                   jax.ShapeDtypeStruct((S, D), MXU_DTYPE)],
        semantics=("parallel",))(x, y_lru, c3, w_out, g_xa, w_q)


def _kv_fwd(mem, g, w_kv):
    M, D = mem.shape
    nb, _, C = w_kv.shape

    def body(mem_ref, g_ref, w_ref, m_ref, kv_ref):
        _, xhat = _rms(mem_ref[...])
        m = (xhat * g_ref[...]).astype(MXU_DTYPE)
        m_ref[...] = m
        for j in range(nb):
            kv_ref[:, pl.ds(j * C, C)] = jnp.dot(m, w_ref[j], preferred_element_type=F32).astype(MXU_DTYPE)

    return pl.pallas_call(
        body, name="kv_fwd", grid=(1,),
        in_specs=[pl.BlockSpec((M, D), lambda i: (0, 0)), pl.BlockSpec((1, D), lambda i: (0, 0)),
                  pl.BlockSpec((nb, D, C), lambda i: (0, 0, 0))],
        out_specs=[pl.BlockSpec((M, D), lambda i: (0, 0)), pl.BlockSpec((M, nb * C), lambda i: (0, 0))],
        out_shape=[jax.ShapeDtypeStruct((M, D), MXU_DTYPE), jax.ShapeDtypeStruct((M, nb * C), MXU_DTYPE)],
        compiler_params=_params("arbitrary"))(mem, g, w_kv)


def _softmax_rows(s):
    e = jnp.exp(s - jnp.max(s, axis=-1, keepdims=True))
    return e / jnp.sum(e, axis=-1, keepdims=True)


def _attn_fwd(q, kv, x1, w_o, g_ffn, comm=None):
    S, D = x1.shape
    M = kv.shape[0]
    hd = D // XA_HEADS
    scale = hd ** -0.5
    ts = _tile(S, 1024)

    def body(q_ref, kv_ref, x1_ref, wo_ref, g_ref, o_ref, x2_ref, h3_ref):
        def scores(h):
            cols = pl.ds(h * hd, hd)
            return _mm_nt(q_ref[:, cols], kv_ref[:, cols]) * scale

        s_next = scores(0)
        for h in range(XA_HEADS):
            s = s_next
            if h + 1 < XA_HEADS:
                s_next = scores(h + 1)
            p = _softmax_rows(s)
            o_ref[:, pl.ds(h * hd, hd)] = _mm(p, kv_ref[:, pl.ds(D + h * hd, hd)]).astype(MXU_DTYPE)
        x2 = x1_ref[...] + jnp.dot(o_ref[...], wo_ref[...], preferred_element_type=F32)
        x2_ref[...] = x2
        _, xhat = _rms(x2)
        h3_ref[...] = (xhat * g_ref[...]).astype(MXU_DTYPE)

    row = pl.BlockSpec((ts, D), lambda i: (i, 0))
    return _pcall(
        comm, body, name="attn_fwd", grid=(S // ts,),
        in_specs=[row, pl.BlockSpec((M, 2 * D), lambda i: (0, 0)), row, pl.BlockSpec((D, D), lambda i: (0, 0)),
                  pl.BlockSpec((1, D), lambda i: (0, 0))],
        out_specs=[row, row, row],
        out_shape=[jax.ShapeDtypeStruct((S, D), MXU_DTYPE), jax.ShapeDtypeStruct((S, D), F32),
                   jax.ShapeDtypeStruct((S, D), MXU_DTYPE)],
        semantics=("parallel",))(q, kv, x1, w_o, g_ffn)


def _ffn_fwd(h3, w_up, conv_w, conv_b, w_down, x2, g_final, target, comm=None):
    S, D = h3.shape
    nb, _, CW = w_up.shape
    half = nb // 2
    cb = 1536
    per = CW // cb
    J = half * per
    ts = _tile(S, 256)
    taps = conv_w.shape[0]
    halo = 8

    def body(h_ref, wup_ref, cw_ref, cb_ref, wd_ref, x2_ref, gf_ref, t_ref,
             gu_ref, act_ref, dx3_ref, loss_ref, dgf_ref, gbuf):
        i = pl.program_id(0)

        @pl.when(i == 0)
        def _():
            for ref in (loss_ref, dgf_ref, gbuf):
                ref[...] = jnp.zeros_like(ref)

        hv = h_ref[...]
        x3 = x2_ref[...]
        def up(j):
            b, cols = j // per, pl.ds((j % per) * cb, cb)
            return (jnp.dot(hv, wup_ref[b, :, cols], preferred_element_type=F32),
                    jnp.dot(hv, wup_ref[half + b, :, cols], preferred_element_type=F32))

        ahead = up(0)
        for j in range(J):
            b, cols, wcols = j // per, pl.ds((j % per) * cb, cb), pl.ds(j * cb, cb)
            g, u = ahead
            if j + 1 < J:
                ahead = up(j + 1)
            gu_ref[0, b, :, cols] = g
            gu_ref[1, b, :, cols] = u
            gbuf[j, pl.ds(halo, ts), :] = g
            gc = _causal_taps(gbuf.at[j], halo, cw_ref, taps, ts, wcols=wcols) + cb_ref[:, wcols]
            gbuf[j, pl.ds(0, halo), :] = gbuf[j, pl.ds(ts, halo), :]
            ge, _ = _gelu(gc)
            act = (ge * u).astype(MXU_DTYPE)
            act_ref[j] = act
            x3 = x3 + jnp.dot(act, wd_ref[j], preferred_element_type=F32)
        rinv, xhat = _rms(x3)
        gf = gf_ref[...]
        diff = xhat * gf - t_ref[...]
        loss_ref[...] += _colsum(diff * diff) * (0.5 / D)
        dy = diff * (1.0 / D)
        dgf_ref[...] += _colsum(dy * xhat)
        dx3_ref[...] = _rms_bwd(rinv, xhat, dy * gf)

    row = pl.BlockSpec((ts, D), lambda i: (i, 0))
    vecd = pl.BlockSpec((1, D), lambda i: (0, 0))
    once = pl.Buffered(1)
    sds = jax.ShapeDtypeStruct
    res = _pcall(
        comm, body, name="ffn_fwd", grid=(S // ts,),
        in_specs=[row, pl.BlockSpec((nb, D, CW), lambda i: (0, 0, 0), pipeline_mode=once),
                  pl.BlockSpec((taps, half * CW), lambda i: (0, 0)), pl.BlockSpec((1, half * CW), lambda i: (0, 0)),
                  pl.BlockSpec((J, cb, D), lambda i: (0, 0, 0), pipeline_mode=once), row, vecd, row],
        out_specs=[pl.BlockSpec((2, half, ts, CW), lambda i: (0, 0, i, 0)),
                   pl.BlockSpec((J, ts, cb), lambda i: (0, i, 0)), row, vecd, vecd],
        out_shape=[sds((2, half, S, CW), F32), sds((J, S, cb), MXU_DTYPE), sds((S, D), F32),
                   sds((1, D), F32), sds((1, D), F32)],
        scratch_shapes=[pltpu.VMEM((J, ts + halo, cb), F32)],
        semantics=("arbitrary",))(h3, w_up, conv_w, conv_b, w_down.reshape(J, cb, D), x2, g_final, target)
    outs = res if comm is None else res[0]
    outs = [outs[0].reshape(nb, S, CW)] + list(outs[1:])
    return outs if comm is None else (outs, res[1])


def _ffn_bwd(dx3, w_down, w_up, gu, x2, g_ffn, conv_w, conv_b, comm=None):
    nb, S, CW = gu.shape
    half = nb // 2
    D = dx3.shape[1]
    cb = 1536
    per = CW // cb
    J = half * per
    ts = _tile(S, 256)
    n = S // ts
    taps = conv_w.shape[0]
    halo = 8
    hb = ts // halo

    def body(dx_ref, x2_ref, gf_ref, wd_ref, wup_ref, gu_ref, gh_ref, cw_ref, cb_ref,
             dgu_ref, dx2_ref, dgf_ref, dcw_ref, dcb_ref, gbuf, dbuf):
        i = pl.program_id(0)
        r = n - 1 - i

        @pl.when(i == 0)
        def _():
            for ref in (dgf_ref, dcw_ref, dcb_ref, dbuf):
                ref[...] = jnp.zeros_like(ref)

        dx3v = dx_ref[...]
        dxb = dx3v.astype(MXU_DTYPE)
        dacts = [_mm_nt(dxb, wd_ref[j]) for j in range(J)]
        dh = None
        for j in range(J):
            b, cols, wcols = j // per, pl.ds((j % per) * cb, cb), pl.ds(j * cb, cb)
            dact = dacts[j]
            gbuf[pl.ds(0, halo), :] = jnp.where(r > 0, gh_ref[0, b, :, cols], 0.0)
            gbuf[pl.ds(halo, ts), :] = gu_ref[0, b, :, cols]
            gs = _windows(gbuf, halo, taps, ts)
            gc = _causal_from(gs, cw_ref, wcols) + cb_ref[:, wcols]
            ge, dge = _gelu(gc)
            dub = (dact * ge).astype(MXU_DTYPE)
            dgc = dact * gu_ref[1, b, :, cols] * dge
            dcb_ref[:, wcols] += _colsum(dgc)
            dbuf[j, pl.ds(0, ts), :] = dgc
            _tap_grads_from(dcw_ref, dgc, gs, wcols)
            dgb = _anticausal_taps(dbuf.at[j], cw_ref, taps, ts, wcols=wcols).astype(MXU_DTYPE)
            dbuf[j, pl.ds(ts, halo), :] = dbuf[j, pl.ds(0, halo), :]
            dgu_ref[0, b, :, cols] = dgb
            dgu_ref[1, b, :, cols] = dub
            part = _mm_nt(dgb, wup_ref[b, :, cols]) + _mm_nt(dub, wup_ref[half + b, :, cols])
            dh = part if dh is None else dh + part
        rinv, xhat = _rms(x2_ref[...])
        dgf_ref[...] += _colsum(dh * xhat)
        dx2_ref[...] = dx3v + _rms_bwd(rinv, xhat, dh * gf_ref[...])

    gu2 = gu.reshape(2, half, S, CW)
    row = pl.BlockSpec((ts, D), lambda i: (n - 1 - i, 0))
    vecd = pl.BlockSpec((1, D), lambda i: (0, 0))
    pair = pl.BlockSpec((2, half, ts, CW), lambda i: (0, 0, n - 1 - i, 0))
    g_prev = pl.BlockSpec((1, half, halo, CW), lambda i: (0, 0, jnp.maximum((n - 1 - i) * hb - 1, 0), 0))
    tapw = pl.BlockSpec((taps, half * CW), lambda i: (0, 0))
    vec = pl.BlockSpec((1, half * CW), lambda i: (0, 0))
    once = pl.Buffered(1)
    sds = jax.ShapeDtypeStruct
    res = _pcall(
        comm, body, name="ffn_bwd", grid=(n,),
        in_specs=[row, row, vecd, pl.BlockSpec((J, cb, D), lambda i: (0, 0, 0), pipeline_mode=once),
                  pl.BlockSpec((nb, D, CW), lambda i: (0, 0, 0), pipeline_mode=once), pair, g_prev, tapw, vec],
        out_specs=[pair, row, vecd, tapw, vec],
        out_shape=[sds((2, half, S, CW), MXU_DTYPE), sds((S, D), F32), sds((1, D), F32),
                   sds((taps, half * CW), F32), sds((1, half * CW), F32)],
        scratch_shapes=[pltpu.VMEM((ts + halo, cb), F32), pltpu.VMEM((J, ts + halo, cb), F32)],
        semantics=("arbitrary",))(dx3, x2, g_ffn, w_down.reshape(J, cb, D), w_up, gu2, gu2, conv_w, conv_b)
    outs = res if comm is None else res[0]
    outs = [outs[0].reshape(nb, S, CW)] + list(outs[1:])
    return outs if comm is None else (outs, res[1])


def _attn_bwd(dx2, w_o, q, kv, x1, g_xa, w_q, comm=None):
    S, D = x1.shape
    M = kv.shape[0]
    hd = D // XA_HEADS
    scale = hd ** -0.5
    ts = _tile(S, 1024)

    def body(dx2_ref, wo_ref, q_ref, kv_ref, x1_ref, g_ref, wq_ref, dq_ref, dx1_ref, dkv_ref, dg_ref):
        i = pl.program_id(0)

        @pl.when(i == 0)
        def _():
            dkv_ref[...] = jnp.zeros_like(dkv_ref)
            dg_ref[...] = jnp.zeros_like(dg_ref)

        dx2 = dx2_ref[...]
        do = _mm_nt(dx2, wo_ref[...]).astype(MXU_DTYPE)
        def scores(h):
            cols = pl.ds(h * hd, hd)
            doh = do[:, h * hd:(h + 1) * hd]
            return (_mm_nt(q_ref[:, cols], kv_ref[:, cols]) * scale,
                    _mm_nt(doh, kv_ref[:, pl.ds(D + h * hd, hd)]), doh)

        ahead = scores(0)
        for h in range(XA_HEADS):
            cols = pl.ds(h * hd, hd)
            vcols = pl.ds(D + h * hd, hd)
            s, dp, doh = ahead
            if h + 1 < XA_HEADS:
                ahead = scores(h + 1)
            p = _softmax_rows(s)
            ds = (p * (dp - jnp.sum(dp * p, axis=-1, keepdims=True)) * scale).astype(MXU_DTYPE)
            dkv_ref[:, vcols] += _mm_tn(p, doh)
            dq_ref[:, cols] = _mm(ds, kv_ref[:, cols]).astype(MXU_DTYPE)
            dkv_ref[:, cols] += _mm_tn(ds, q_ref[:, cols])
        dh2 = _mm_nt(dq_ref[...], wq_ref[...])
        rinv, xhat = _rms(x1_ref[...])
        dg_ref[...] += _colsum(dh2 * xhat)
        dx1_ref[...] = dx2 + _rms_bwd(rinv, xhat, dh2 * g_ref[...])

    row = pl.BlockSpec((ts, D), lambda i: (i, 0))
    mat = pl.BlockSpec((D, D), lambda i: (0, 0))
    vecd = pl.BlockSpec((1, D), lambda i: (0, 0))
    kvs = pl.BlockSpec((M, 2 * D), lambda i: (0, 0))
    return _pcall(
        comm, body, name="attn_bwd", grid=(S // ts,),
        in_specs=[row, mat, row, kvs, row, vecd, mat],
        out_specs=[row, row, kvs, vecd],
        out_shape=[jax.ShapeDtypeStruct((S, D), MXU_DTYPE), jax.ShapeDtypeStruct((S, D), F32),
                   jax.ShapeDtypeStruct((M, 2 * D), F32), jax.ShapeDtypeStruct((1, D), F32)],
        semantics=("arbitrary",))(dx2, w_o, q, kv, x1, g_xa, w_q)


def _kv_bwd(dkv, w_kv, mem, g, m):
    M, D = mem.shape
    nb, _, C = w_kv.shape

    def body(dkv_ref, w_ref, mem_ref, m_ref, dw_ref, dg_ref):
        dm = jnp.zeros((M, D), F32)
        for j in range(nb):
            dj = dkv_ref[:, pl.ds(j * C, C)].astype(MXU_DTYPE)
            dw_ref[j] = _mm_tn(m_ref[...], dj).astype(dw_ref.dtype)
            dm = dm + _mm_nt(dj, w_ref[j])
        _, xhat = _rms(mem_ref[...])
        dg_ref[...] = _colsum(dm * xhat)

    full = lambda *s: pl.BlockSpec(s, lambda i: (0,) * len(s))
    return pl.pallas_call(
        body, name="kv_bwd", grid=(1,),
        in_specs=[full(M, nb * C), full(nb, D, C), full(M, D), full(M, D)],
        out_specs=[full(nb, D, C), full(1, D)],
        out_shape=[jax.ShapeDtypeStruct((nb, D, C), WIRE_DTYPE), jax.ShapeDtypeStruct((1, D), F32)],
        compiler_params=_params("arbitrary"))(dkv, w_kv, mem, m)


def _conf_bwd(dx1, w_out_c, z, c1, conv_w, ln_g, ln_b, comm=None):
    _, S, C = z.shape
    D = dx1.shape[1]
    ts = _tile(S, 512)
    n = S // ts
    taps = conv_w.shape[0]
    halo = 32
    hb = ts // halo

    def body(dx_ref, wo_ref, za_ref, zb_ref, zah_ref, zbh_ref, c1_ref, cw_ref, g_ref, b_ref,
             dz_ref, dcw_ref, dcb_ref, dlg_ref, dlb_ref, c0buf, dbuf, shifted):
        i = pl.program_id(0)
        r = n - 1 - i

        @pl.when(i == 0)
        def _():
            for ref in (dcw_ref, dcb_ref, dlg_ref, dlb_ref):
                ref[...] = jnp.zeros_like(ref)
            dbuf[pl.ds(ts, halo), :] = jnp.zeros((halo, C), F32)

        za = za_ref[0]
        sb = _sigmoid(zb_ref[0])
        c0buf[pl.ds(0, halo), :] = jnp.where(r > 0, zah_ref[0] * _sigmoid(zbh_ref[0]), 0.0)
        c0buf[pl.ds(halo, ts), :] = za * sb
        dc3 = _mm_nt(dx_ref[...], wo_ref[...])
        rstd, xhat = _layer_norm_stats(c1_ref[...])
        g = g_ref[...]
        c2 = xhat * g + b_ref[...]
        sg = _sigmoid(c2)
        dc2 = dc3 * sg * (1.0 + c2 * (1.0 - sg))
        dlg_ref[...] += _colsum(dc2 * xhat)
        dlb_ref[...] += _colsum(dc2)
        dxh = dc2 * g
        dc1 = rstd * (dxh - jnp.mean(dxh, axis=-1, keepdims=True)
                      - xhat * jnp.mean(dxh * xhat, axis=-1, keepdims=True))
        dcb_ref[...] += _colsum(dc1)
        dbuf[pl.ds(0, ts), :] = dc1
        _shift_copies(shifted, c0buf, ts + halo - 8, up=False)
        _tap_grads8(dcw_ref, dc1, shifted, halo, taps, ts)
        _shift_copies(shifted, dbuf, ts + halo - 8, up=True)
        dc0 = _anticausal_taps8(shifted, cw_ref, taps, ts)
        dz_ref[0] = (dc0 * sb).astype(MXU_DTYPE)
        dz_ref[1] = (dc0 * za * sb * (1.0 - sb)).astype(MXU_DTYPE)
        dbuf[pl.ds(ts, halo), :] = dbuf[pl.ds(0, halo), :]

    vec = pl.BlockSpec((1, C), lambda i: (0, 0))
    tapw = pl.BlockSpec((taps, C), lambda i: (0, 0))
    tile = lambda b: pl.BlockSpec((1, ts, C), lambda i: (b, n - 1 - i, 0))
    prev = lambda b: pl.BlockSpec((1, halo, C), lambda i: (b, jnp.maximum((n - 1 - i) * hb - 1, 0), 0))
    return _pcall(
        comm, body, name="conf_bwd", grid=(n,),
        in_specs=[pl.BlockSpec((ts, D), lambda i: (n - 1 - i, 0)), pl.BlockSpec((C, D), lambda i: (0, 0)),
                  tile(2), tile(3), prev(2), prev(3), pl.BlockSpec((ts, C), lambda i: (n - 1 - i, 0)),
                  tapw, vec, vec],
        out_specs=[pl.BlockSpec((2, ts, C), lambda i: (1, n - 1 - i, 0)), tapw, vec, vec, vec],
        out_shape=[jax.ShapeDtypeStruct((4, S, C), MXU_DTYPE), jax.ShapeDtypeStruct((taps, C), F32),
                   jax.ShapeDtypeStruct((1, C), F32), jax.ShapeDtypeStruct((1, C), F32),
                   jax.ShapeDtypeStruct((1, C), F32)],
        scratch_shapes=[pltpu.VMEM((ts + halo, C), F32), pltpu.VMEM((ts + halo, C), F32),
                        pltpu.VMEM((8, ts + halo - 8, C), F32)],
        semantics=("arbitrary",))(dx1, w_out_c, z, z, z, z, c1, conv_w, ln_g, ln_b)


def _lru_bwd(dx1, w_out_l, z, h, conv_w, conv_b, wa, ba, wx, bx, lam, dz, comm=None):
    _, S, C = z.shape
    D = dx1.shape[1]
    ts = _tile(S, 512)
    n = S // ts
    taps = conv_w.shape[0]
    halo = 8
    hb = ts // halo

    def body(dx_ref, wo_ref, zx_ref, zxh_ref, zg_ref, h_ref, hh_ref, cw_ref, cb_ref, wa_ref, ba_ref,
             wx_ref, bx_ref, lam_ref, dz_in,
             dz_ref, dwa_ref, dwx_ref, dba_ref, dbx_ref, dlam_ref, dcw_ref, dcb_ref,
             xbuf, hbuf, a_s, w_s, dh_s, g_s, dbuf, pc):
        i = pl.program_id(0)
        r = n - 1 - i

        @pl.when(i == 0)
        def _():
            for ref in (dwa_ref, dwx_ref, dba_ref, dbx_ref, dlam_ref, dcw_ref, dcb_ref, pc):
                ref[...] = jnp.zeros_like(ref)
            dbuf[pl.ds(ts, halo), :] = jnp.zeros((halo, C), F32)

        xbuf[pl.ds(0, halo), :] = jnp.where(r > 0, zxh_ref[0], 0.0)
        xbuf[pl.ds(halo, ts), :] = zx_ref[0]
        hbuf[pl.ds(0, halo), :] = jnp.where(r > 0, hh_ref[...], 0.0)
        hbuf[pl.ds(halo, ts), :] = h_ref[...]
        xs = _windows(xbuf, halo, taps, ts)
        xc = _causal_from(xs, cw_ref) + cb_ref[...]
        lam_v = lam_ref[...]
        sp = _softplus_neg(lam_v)
        rg, ig, a, mult, inv_mult = _lru_gates(xc, wa_ref, ba_ref, wx_ref, bx_ref, sp)

        dy = _mm_nt(dx_ref[...], wo_ref[...])
        ge, dge = _gelu(zg_ref[0])
        dh = dy * ge
        dz_ref[1] = (dy * h_ref[...] * dge).astype(MXU_DTYPE)
        a_s[...] = a
        w_s[...] = a * dh
        dh_s[...] = dh
        row = lax.broadcasted_iota(jnp.int32, (8, C), 0)

        def step(kk, carry):
            off = pl.multiple_of((ts // 8 - 1 - kk) * 8, 8)
            av = a_s[pl.ds(off, 8), :]
            wv = w_s[pl.ds(off, 8), :]
            for d in (1, 2, 4):
                m = row < 8 - d
                a_sh = jnp.where(m, pltpu.roll(av, 8 - d, 0), 1.0)
                w_sh = jnp.where(m, pltpu.roll(wv, 8 - d, 0), 0.0)
                wv = wv + av * w_sh
                av = av * a_sh
            pv = wv + av * carry
            g_s[pl.ds(off, 8), :] = dh_s[pl.ds(off, 8), :] + jnp.where(row < 7, pltpu.roll(pv, 7, 0), carry)
            return jnp.broadcast_to(pv[0:1, :], (8, C))

        pc[...] = lax.fori_loop(0, ts // 8, step, pc[...])
        gt = g_s[...]
        da = gt * hbuf[pl.ds(halo - 1, ts), :]
        gm = gt * mult
        dlog_a = da * a - (gt * ig * xc) * (a * a) * inv_mult
        dlam_ref[...] += _colsum(dlog_a * rg) * (RG_C / (1.0 + jnp.exp(lam_v)))
        dpa = (dlog_a * (-RG_C * sp)) * rg * (1.0 - rg)
        dpx = (gm * xc) * ig * (1.0 - ig)
        dba_ref[...] += _colsum(dpa)
        dbx_ref[...] += _colsum(dpx)
        xb = xc.astype(MXU_DTYPE)
        dpab, dpxb = dpa.astype(MXU_DTYPE), dpx.astype(MXU_DTYPE)
        dwa_ref[...] += _mm_tn(xb, dpab)
        dwx_ref[...] += _mm_tn(xb, dpxb)
        dxc = gm * ig + _mm_nt(dpab, wa_ref[...]) + _mm_nt(dpxb, wx_ref[...])
        dcb_ref[...] += _colsum(dxc)
        dbuf[pl.ds(0, ts), :] = dxc
        _tap_grads_from(dcw_ref, dxc, xs)
        dz_ref[0] = _anticausal_taps(dbuf, cw_ref, taps, ts).astype(MXU_DTYPE)
        dbuf[pl.ds(ts, halo), :] = dbuf[pl.ds(0, halo), :]

    vec = pl.BlockSpec((1, C), lambda i: (0, 0))
    mat = pl.BlockSpec((C, C), lambda i: (0, 0))
    tapw = pl.BlockSpec((taps, C), lambda i: (0, 0))
    prev_rows = lambda i: jnp.maximum((n - 1 - i) * hb - 1, 0)
    sds = jax.ShapeDtypeStruct
    return _pcall(
        comm, body, name="lru_bwd", grid=(n,),
        in_specs=[pl.BlockSpec((ts, D), lambda i: (n - 1 - i, 0)), pl.BlockSpec((C, D), lambda i: (0, 0)),
                  pl.BlockSpec((1, ts, C), lambda i: (0, n - 1 - i, 0)),
                  pl.BlockSpec((1, halo, C), lambda i: (0, prev_rows(i), 0)),
                  pl.BlockSpec((1, ts, C), lambda i: (1, n - 1 - i, 0)),
                  pl.BlockSpec((ts, C), lambda i: (n - 1 - i, 0)),
                  pl.BlockSpec((halo, C), lambda i: (prev_rows(i), 0)),
                  tapw, vec, mat, vec, mat, vec, vec, ANY],
        out_specs=[pl.BlockSpec((2, ts, C), lambda i: (0, n - 1 - i, 0)), mat, mat, vec, vec, vec, tapw, vec],
        out_shape=[sds(dz.shape, MXU_DTYPE), sds((C, C), F32), sds((C, C), F32), sds((1, C), F32),
                   sds((1, C), F32), sds((1, C), F32), sds((taps, C), F32), sds((1, C), F32)],
        scratch_shapes=[pltpu.VMEM((ts + halo, C), F32), pltpu.VMEM((ts + halo, C), F32)]
        + [pltpu.VMEM((ts, C), F32)] * 4 + [pltpu.VMEM((ts + halo, C), F32), pltpu.VMEM((8, C), F32)],
        aliases={14: 0},
        semantics=("arbitrary",))(dx1, w_out_l, z, z, z, h, h, conv_w, conv_b, wa, ba, wx, bx, lam, dz)


def _bwd_in(dz, w_in, x, g, dx1):
    S, D = x.shape
    nb, _, C = w_in.shape
    ts = _tile(S, 512)
    halves = 2 if ts % 32 == 0 else 1
    hr = ts // halves

    def body(dz_ref, w_ref, x_ref, g_ref, dx1_ref, dx_ref, dg_ref):
        i = pl.program_id(0)

        @pl.when(i == 0)
        def _():
            dg_ref[...] = jnp.zeros_like(dg_ref)

        def grad_h(k):
            rows = pl.ds(k * hr, hr)
            dh = _mm_nt(dz_ref[0, rows, :], w_ref[0])
            for j in range(1, nb):
                dh = dh + _mm_nt(dz_ref[j, rows, :], w_ref[j])
            return dh

        ahead = grad_h(0)
        for k in range(halves):
            rows, dh = pl.ds(k * hr, hr), ahead
            if k + 1 < halves:
                ahead = grad_h(k + 1)
            rinv, xhat = _rms(x_ref[rows, :])
            dg_ref[...] += _colsum(dh * xhat)
            dx_ref[rows, :] = dx1_ref[rows, :] + _rms_bwd(rinv, xhat, dh * g_ref[...])

    row = pl.BlockSpec((ts, D), lambda i: (i, 0))
    vecd = pl.BlockSpec((1, D), lambda i: (0, 0))
    return pl.pallas_call(
        body, name="bwd_in", grid=(S // ts,),
        in_specs=[pl.BlockSpec((nb, ts, C), lambda i: (0, i, 0)), pl.BlockSpec((nb, D, C), lambda i: (0, 0, 0)),
                  row, vecd, row],
        out_specs=[row, vecd],
        out_shape=[jax.ShapeDtypeStruct((S, D), F32), jax.ShapeDtypeStruct((1, D), F32)],
        compiler_params=_params("arbitrary"))(dz, w_in, x, g, dx1)


def _wgrad(a, b, name, comm=None):
    na, S, K = a.shape
    nb, _, N = b.shape
    nj = max(na, nb)
    assert min(na, nb) == 1
    ts = _tile(S, 2048)
    ns = S // ts
    grp = max(g for g in range(1, nj + 1) if nj % g == 0 and g * K * N * 4 <= WGRAD_ACC_BYTES)
    ga, gb = (grp if na > 1 else 1), (grp if nb > 1 else 1)

    def body(a_ref, b_ref, o_ref, acc):
        s = pl.program_id(1)

        @pl.when(s == 0)
        def _():
            acc[...] = jnp.zeros_like(acc)

        for k in range(grp):
            acc[k] += _mm_tn(a_ref[k if na > 1 else 0], b_ref[k if nb > 1 else 0])

        @pl.when(s == ns - 1)
        def _():
            o_ref[...] = acc[...].astype(o_ref.dtype)

    res = _pcall(
        comm, body, name=name, grid=(nj // grp, ns),
        in_specs=[pl.BlockSpec((ga, ts, K), (lambda j, s: (j, s, 0)) if na > 1 else (lambda j, s: (0, s, 0))),
                  pl.BlockSpec((gb, ts, N), (lambda j, s: (j, s, 0)) if nb > 1 else (lambda j, s: (0, s, 0)))],
        out_specs=pl.BlockSpec((grp, K, N), lambda j, s: (j, 0, 0)),
        out_shape=jax.ShapeDtypeStruct((nj, K, N), WIRE_DTYPE),
        scratch_shapes=[pltpu.VMEM((grp, K, N), F32)],
        semantics=("parallel", "arbitrary"))(a, b)
    return res[0] if comm is None else (res[0][0], res[1])


def _place():
    x, y, c = lax.axis_index("x"), lax.axis_index("y"), lax.axis_index("c")
    other_chips = [(1 - x, y), (x, 1 - y), (1 - x, 1 - y)]
    return x, y, c, other_chips


def _gather_weights(shards):
    nt = len(shards)

    def body(*refs):
        src, dst = refs[:nt], refs[nt:2 * nt]
        ici_send, ici_recv, d2d_send, d2d_recv, own_send, own_recv = refs[2 * nt:]
        x, y, c, chips = _place()
        mine = 2 * x + y

        def half(t, pc):
            hr = src[t].shape[0] // 2
            return pl.ds(pc * hr, hr)

        def own(t):
            return pltpu.make_async_remote_copy(
                src_ref=src[t], dst_ref=dst[t].at[mine], send_sem=own_send.at[t], recv_sem=own_recv.at[t],
                device_id=(x, y, 1 - c), device_id_type=MESH)

        def ici(t, k, block, to):
            cx, cy = block
            ref = dst[t].at[2 * cx + cy, half(t, c)]
            return pltpu.make_async_remote_copy(
                src_ref=src[t].at[half(t, c)] if to is not None else ref, dst_ref=ref,
                send_sem=ici_send.at[t, k], recv_sem=ici_recv.at[t, k],
                device_id=(*to, c) if to is not None else (x, y, c), device_id_type=MESH)

        def d2d(t, k, block, pc):
            cx, cy = block
            ref = dst[t].at[2 * cx + cy, half(t, pc)]
            return pltpu.make_async_remote_copy(
                src_ref=ref, dst_ref=ref, send_sem=d2d_send.at[t, k], recv_sem=d2d_recv.at[t, k],
                device_id=(x, y, 1 - c), device_id_type=MESH)

        sends = [ici(t, k, (x, y), chip) for t in range(nt) for k, chip in enumerate(chips)]
        sends += [own(t) for t in range(nt)]
        for cp in sends:
            cp.start()
        passed = []
        for t in range(nt):
            for k, chip in enumerate(chips):
                ici(t, k, chip, None).wait_recv()
                fw = d2d(t, k, chip, c)
                fw.start()
                passed.append(fw)
        for t in range(nt):
            own(t).wait_recv()
            for k, chip in enumerate(chips):
                d2d(t, k, chip, 1 - c).wait_recv()
        for cp in sends + passed:
            cp.wait_send()

    return pl.pallas_call(
        body, name="gather_weights",
        in_specs=[ANY] * nt, out_specs=[ANY] * nt,
        out_shape=[jax.ShapeDtypeStruct((N_CHIPS,) + s.shape, s.dtype) for s in shards],
        scratch_shapes=[pltpu.SemaphoreType.DMA((nt, 3))] * 4 + [pltpu.SemaphoreType.DMA((nt,))] * 2,
        compiler_params=pltpu.CompilerParams(has_side_effects=True))(*shards)


def _gather_over_ici(shards):
    nt = len(shards)

    def copies(src, dst, scr, arriving):
        ici_send, ici_recv, own_send, own_recv = scr
        x, y, c, chips = _place()
        out = []
        for t in range(nt):
            hr = src[t].shape[0] // 2
            rows = pl.ds(c * hr, hr)
            for k, (cx, cy) in enumerate(chips):
                block = 2 * cx + cy if arriving else 2 * x + y
                out.append(pltpu.make_async_remote_copy(
                    src_ref=src[t].at[rows], dst_ref=dst[t].at[block, rows],
                    send_sem=ici_send.at[t, k], recv_sem=ici_recv.at[t, k],
                    device_id=(cx, cy, c), device_id_type=MESH))
            out.append(pltpu.make_async_remote_copy(
                src_ref=src[t], dst_ref=dst[t].at[2 * x + y], send_sem=own_send.at[t], recv_sem=own_recv.at[t],
                device_id=(x, y, 1 - c), device_id_type=MESH))
        return out

    def start(src, dst, scr):
        for cp in copies(src, dst, scr, False):
            cp.start()

    def finish(src, dst, scr):
        for cp in copies(src, dst, scr, True):
            cp.wait_recv()
        for cp in copies(src, dst, scr, False):
            cp.wait_send()

    return _Comm(shards, [jax.ShapeDtypeStruct((N_CHIPS,) + s.shape, s.dtype) for s in shards],
                 [pltpu.SemaphoreType.DMA((nt, 3))] * 2 + [pltpu.SemaphoreType.DMA((nt,))] * 2, start, finish)


def _gather_pass_on(bufs):
    nt = len(bufs)

    def passed(dst, scr, t, k, block, pc):
        send, recv = scr
        x, y, c, _ = _place()
        cx, cy = block
        hr = dst[t].shape[1] // 2
        ref = dst[t].at[2 * cx + cy, pl.ds(pc * hr, hr)]
        return pltpu.make_async_remote_copy(src_ref=ref, dst_ref=ref, send_sem=send.at[t, k], recv_sem=recv.at[t, k],
                                            device_id=(x, y, 1 - c), device_id_type=MESH)

    def start(src, dst, scr):
        _, _, c, chips = _place()
        for t in range(nt):
            for k, chip in enumerate(chips):
                passed(dst, scr, t, k, chip, c).start()

    def finish(src, dst, scr):
        _, _, c, chips = _place()
        for t in range(nt):
            for k, chip in enumerate(chips):
                passed(dst, scr, t, k, chip, 1 - c).wait_recv()
        for t in range(nt):
            for k, chip in enumerate(chips):
                passed(dst, scr, t, k, chip, c).wait_send()

    return _Comm(bufs, [jax.ShapeDtypeStruct(b.shape, b.dtype) for b in bufs],
                 [pltpu.SemaphoreType.DMA((nt, 3))] * 2, start, finish, aliases={t: t for t in range(nt)})


def _exchange_halves(grads):
    nt = len(grads)

    def copies(src, dst, scr):
        send, recv = scr
        x, y, c, _ = _place()
        out = []
        for t in range(nt):
            hr = src[t].shape[1] // 2
            out.append(pltpu.make_async_remote_copy(
                src_ref=src[t].at[:, pl.ds((1 - c) * hr, hr)], dst_ref=dst[t],
                send_sem=send.at[t], recv_sem=recv.at[t], device_id=(x, y, 1 - c), device_id_type=MESH))
        return out

    def start(src, dst, scr):
        for cp in copies(src, dst, scr):
            cp.start()

    def finish(src, dst, scr):
        for cp in copies(src, dst, scr):
            cp.wait()

    return _Comm(grads, [jax.ShapeDtypeStruct((g.shape[0], g.shape[1] // 2, g.shape[2]), g.dtype) for g in grads],
                 [pltpu.SemaphoreType.DMA((nt,))] * 2, start, finish)


def _add_halves(grad, other, name):
    nb, R, C = grad.shape
    hr = R // 2
    tr = _tile(hr, 256, 16)
    steps = hr // tr
    c = lax.axis_index("c").astype(jnp.int32).reshape((1,))

    def body(c_ref, a_ref, b_ref, o_ref):
        o_ref[...] = (a_ref[...].astype(F32) + b_ref[...].astype(F32)).astype(o_ref.dtype)

    return pl.pallas_call(
        body, name=name,
        grid_spec=pltpu.PrefetchScalarGridSpec(
            num_scalar_prefetch=1, grid=(nb, steps),
            in_specs=[pl.BlockSpec((1, tr, C), lambda j, i, c_ref: (j, c_ref[0] * steps + i, 0)),
                      pl.BlockSpec((1, tr, C), lambda j, i, c_ref: (j, i, 0))],
            out_specs=pl.BlockSpec((1, tr, C), lambda j, i, c_ref: (j, i, 0))),
        out_shape=jax.ShapeDtypeStruct((nb, hr, C), grad.dtype),
        compiler_params=_params("parallel", "parallel"))(c, grad, other)


def _scatter_chip_sums(parts):
    nt = len(parts)

    def copies(src, dst, scr):
        send, recv = scr
        x, y, c, chips = _place()
        out = []
        for t in range(nt):
            for k, (cx, cy) in enumerate(chips):
                out.append(pltpu.make_async_remote_copy(
                    src_ref=src[t].at[2 * cx + cy], dst_ref=dst[t].at[k],
                    send_sem=send.at[t, k], recv_sem=recv.at[t, k], device_id=(cx, cy, c), device_id_type=MESH))
        return out

    def start(src, dst, scr):
        for cp in copies(src, dst, scr):
            cp.start()

    def finish(src, dst, scr):
        for cp in copies(src, dst, scr):
            cp.wait()

    return _Comm(parts, [jax.ShapeDtypeStruct((3,) + p.shape[1:], p.dtype) for p in parts],
                 [pltpu.SemaphoreType.DMA((nt, 3))] * 2, start, finish)


def _sum_chips(part, recv, name):
    _, hr, C = part.shape
    tr = _tile(hr, 256, 16)
    steps = hr // tr
    where = jnp.stack([2 * lax.axis_index("x") + lax.axis_index("y"), lax.axis_index("c")]).astype(jnp.int32)

    def body(w_ref, a_ref, b_ref, o_ref):
        acc = a_ref[0].astype(F32)
        for k in range(3):
            acc = acc + b_ref[k].astype(F32)
        o_ref[...] = acc

    return pl.pallas_call(
        body, name=name,
        grid_spec=pltpu.PrefetchScalarGridSpec(
            num_scalar_prefetch=1, grid=(steps,),
            in_specs=[pl.BlockSpec((1, tr, C), lambda i, w_ref: (w_ref[0], i, 0)),
                      pl.BlockSpec((3, tr, C), lambda i, w_ref: (0, i, 0))],
            out_specs=pl.BlockSpec((tr, C), lambda i, w_ref: (w_ref[1] * steps + i, 0))),
        out_shape=jax.ShapeDtypeStruct((2 * hr, C), F32),
        compiler_params=_params("parallel"))(where, part, recv)


def _join_halves(bufs):
    nt = len(bufs)

    def swap(dst, scr, t, pc):
        send, recv = scr
        x, y, c, _ = _place()
        hr = dst[t].shape[0] // 2
        rows = dst[t].at[pl.ds(pc * hr, hr)]
        return pltpu.make_async_remote_copy(src_ref=rows, dst_ref=rows, send_sem=send.at[t], recv_sem=recv.at[t],
                                            device_id=(x, y, 1 - c), device_id_type=MESH)

    def start(src, dst, scr):
        c = lax.axis_index("c")
        for t in range(nt):
            swap(dst, scr, t, c).start()

    def finish(src, dst, scr):
        c = lax.axis_index("c")
        for t in range(nt):
            swap(dst, scr, t, 1 - c).wait_recv()
        for t in range(nt):
            swap(dst, scr, t, c).wait_send()

    return _Comm(bufs, [jax.ShapeDtypeStruct(b.shape, b.dtype) for b in bufs],
                 [pltpu.SemaphoreType.DMA((nt,))] * 2, start, finish, aliases={t: t for t in range(nt)})


def _reduce_scatter_in_vmem(g):
    nb, R, C = g.shape
    hr = R // 2

    def run(ins, outs, scr):
        (g_ref,), (out_ref,) = ins, outs
        other, part, got, send, recv = scr
        x, y, c, chips = _place()
        sibling = (x, y, 1 - c)
        my_rows = pl.ds(pl.multiple_of(c * hr, hr), hr)
        their_rows = pl.ds(pl.multiple_of((1 - c) * hr, hr), hr)
        swap = pltpu.make_async_remote_copy(src_ref=g_ref.at[:, their_rows], dst_ref=other, send_sem=send.at[0],
                                            recv_sem=recv.at[0], device_id=sibling, device_id_type=MESH)
        swap.start()
        swap.wait()
        part[...] = (g_ref[:, my_rows, :].astype(F32) + other[...].astype(F32)).astype(part.dtype)
        to_owner = [pltpu.make_async_remote_copy(src_ref=part.at[2 * cx + cy], dst_ref=got.at[k],
                                                 send_sem=send.at[1 + k], recv_sem=recv.at[1 + k],
                                                 device_id=(cx, cy, c), device_id_type=MESH)
                    for k, (cx, cy) in enumerate(chips)]
        for cp in to_owner:
            cp.start()
        for cp in to_owner:
            cp.wait()
        total = part[2 * x + y].astype(F32)
        for k in range(3):
            total = total + got[k].astype(F32)
        out_ref[my_rows, :] = total

        def join(rows):
            return pltpu.make_async_remote_copy(src_ref=out_ref.at[rows], dst_ref=out_ref.at[rows], send_sem=send.at[4],
                                                recv_sem=recv.at[4], device_id=sibling, device_id_type=MESH)

        join(my_rows).start()
        join(their_rows).wait_recv()
        join(my_rows).wait_send()

    return _Comm([g], [jax.ShapeDtypeStruct((R, C), F32)],
                 [pltpu.VMEM((nb, hr, C), g.dtype), pltpu.VMEM((nb, hr, C), g.dtype), pltpu.VMEM((3, hr, C), g.dtype),
                  pltpu.SemaphoreType.DMA((5,)), pltpu.SemaphoreType.DMA((5,))],
                 run, lambda ins, outs, scr: None, in_specs=[WHOLE_VMEM], out_specs=[WHOLE_VMEM])


def _all_reduce_rows(buf, loss_row=None):
    R, L = buf.shape

    def copies(in_ref, gath, send, recv):
        x, y, c, _ = _place()
        out = []
        for k in range(1, N_DEV):
            peer = (x ^ ((k >> 2) & 1), y ^ ((k >> 1) & 1), c ^ (k & 1))
            out.append(pltpu.make_async_remote_copy(
                src_ref=in_ref, dst_ref=gath.at[k], send_sem=send.at[k - 1], recv_sem=recv.at[k - 1],
                device_id=peer, device_id_type=MESH))
        return out

    def start(ins, outs, scr):
        gath, send, recv = scr
        gath[0] = ins[0][...]
        for cp in copies(ins[0], gath, send, recv):
            cp.start()

    def finish(ins, outs, scr):
        gath, send, recv = scr
        for cp in copies(ins[0], gath, send, recv):
            cp.wait()
        x, y, c, _ = _place()
        me = 4 * x + 2 * y + c
        total = gath[me]
        for d in range(1, N_DEV):
            total = total + gath[d ^ me]
        outs[0][...] = total
        if loss_row is not None:
            outs[1][...] = jnp.sum(total[loss_row:loss_row + 1, :], axis=1, keepdims=True)

    out_shape = [jax.ShapeDtypeStruct((R, L), F32)]
    if loss_row is not None:
        out_shape.append(jax.ShapeDtypeStruct((1, 1), F32))
    return _Comm([buf], out_shape,
                 [pltpu.VMEM((N_DEV, R, L), F32), pltpu.SemaphoreType.DMA((N_DEV - 1,)),
                  pltpu.SemaphoreType.DMA((N_DEV - 1,))],
                 start, finish, in_specs=[WHOLE_VMEM], out_specs=[WHOLE_VMEM] * len(out_shape))


def _adamw_update(w_ref, g_ref, m_ref, v_ref, d_ref, nm_ref, nv_ref):
    gv = g_ref[...]
    nm = ADAM_B1 * m_ref[...] + (1.0 - ADAM_B1) * gv
    nv = ADAM_B2 * v_ref[...] + (1.0 - ADAM_B2) * (gv * gv)
    nm_ref[...] = nm
    nv_ref[...] = nv
    m_hat = nm / (1.0 - ADAM_B1 ** ADAM_STEP)
    v_hat = nv / (1.0 - ADAM_B2 ** ADAM_STEP)
    d_ref[...] = -ADAM_LR * (m_hat / (jnp.sqrt(v_hat) + ADAM_EPS) + ADAM_WD * w_ref[...])


def _adamw(w, g, m, v, name):
    R, C = w.shape
    tr = _tile(R, 256)

    def body(w_ref, g_ref, m_ref, v_ref, d_ref, nm_ref, nv_ref, g_out):
        _adamw_update(w_ref, g_ref, m_ref, v_ref, d_ref, nm_ref, nv_ref)
        g_out[...] = g_ref[...]

    blk = pl.BlockSpec((tr, C), lambda i: (i, 0))
    return pl.pallas_call(
        body, name=name, grid=(R // tr,), in_specs=[blk] * 4, out_specs=[blk] * 4,
        out_shape=[jax.ShapeDtypeStruct((R, C), F32)] * 4,
        compiler_params=_params("parallel"))(w, g, m, v)


def _adamw_many(ws, gs, ms, vs, name):
    n = len(ws)

    def body(*refs):
        for k in range(n):
            _adamw_update(*[refs[part * n + k] for part in range(7)])

    shapes = [jax.ShapeDtypeStruct(w.shape, F32) for w in ws]
    outs = pl.pallas_call(
        body, name=name, in_specs=[WHOLE_VMEM] * (4 * n), out_specs=[WHOLE_VMEM] * (3 * n), out_shape=shapes * 3,
        compiler_params=pltpu.CompilerParams(vmem_limit_bytes=VMEM_LIMIT_BYTES))(*ws, *gs, *ms, *vs)
    return outs[:n], outs[n:2 * n], outs[2 * n:]


def _pack_rows(arrays):
    rows = []
    for a in arrays:
        flat = a.reshape(-1).astype(F32)
        pad = (-flat.shape[0]) % LANES
        rows.append(jnp.pad(flat, (0, pad)).reshape(-1, LANES))
    buf = jnp.concatenate(rows, axis=0)
    return jnp.pad(buf, ((0, (-buf.shape[0]) % 8), (0, 0)))


def _unpack_rows(buf, shapes):
    out, r = [], 0
    for s in shapes:
        n = math.prod(s)
        nr = -(-n // LANES)
        out.append(buf[r:r + nr].reshape(-1)[:n].reshape(s))
        r += nr
    return out


def _block_diag(w):
    H, a, b = w.shape
    eye = jnp.eye(H, dtype=w.dtype)
    return (eye[:, None, :, None] * w[:, :, None, :]).reshape(H * a, H * b)


def _block_diag_parts(d, H):
    a, b = d.shape[0] // H, d.shape[1] // H
    d4 = d.reshape(H, a, H, b)
    return jnp.stack([d4[h, :, h, :] for h in range(H)])


def _rs_add(names, grads, others):
    return [_add_halves(g, o, "rs_add_halves_" + n) for n, g, o in zip(names, grads, others)]


def _rs_sum(names, parts, recvs):
    return [_sum_chips(p, r, "rs_sum_chips_" + n) for n, p, r in zip(names, parts, recvs)]


def _step(x, mem, target, shards, small, tap_rows, tap_shapes):
    D = x.shape[1]
    nch = N_CHIPS
    p = dict(small)

    (w_in_f,) = _gather_weights([shards['w_in']])
    wf = {}

    def ici(names):
        return _gather_over_ici([shards[n] for n in names])

    ici_a, taps_sum = ici(['w_out', 'w_q']), _all_reduce_rows(tap_rows)
    (z, h1), couts = _fwd_in(x, p['mix_norm_g'], w_in_f, comm=_merge(ici_a, taps_sum))
    bufs_a, (taps,) = _split(couts, ici_a, taps_sum)
    p.update(zip(COL_SHARDED_SMALL, _unpack_rows(taps, tap_shapes)))
    wa_d = _block_diag(p['lru_w_a']).astype(MXU_DTYPE)
    wx_d = _block_diag(p['lru_w_x']).astype(MXU_DTYPE)
    heads = p['lru_w_a'].shape[0]
    pass_a, ici_b = _gather_pass_on(bufs_a), ici(['w_kv', 'w_o'])
    (h, y_lru), couts = _lru_fwd(z, p['lru_conv_w'], p['lru_conv_b'], wa_d, p['lru_b_a'], wx_d, p['lru_b_x'],
                                 p['lru_lambda'], comm=_merge(pass_a, ici_b))
    (wf['w_out'], wf['w_q']), bufs_b = _split(couts, pass_a, ici_b)
    pass_b, ici_c = _gather_pass_on(bufs_b), ici(['w_up'])
    (c1, c3), couts = _conf_fwd(z, p['conf_conv_w'], p['conf_conv_b'], p['conf_ln_g'], p['conf_ln_b'],
                                comm=_merge(pass_b, ici_c))
    (wf['w_kv'], wf['w_o']), bufs_c = _split(couts, pass_b, ici_c)
    w_out2 = wf['w_out'].reshape(2, -1, D)
    w_q = wf['w_q'].reshape(D, D)
    w_o = wf['w_o'].reshape(D, D)
    pass_c, ici_d = _gather_pass_on(bufs_c), ici(['w_down'])
    (x1, h2, q), couts = _fwd_out_q(x, y_lru, c3, w_out2, p['xa_norm_g'], w_q, comm=_merge(pass_c, ici_d))
    (wf['w_up'],), bufs_d = _split(couts, pass_c, ici_d)
    m, kv = _kv_fwd(mem, p['mem_norm_g'], wf['w_kv'])
    (o, x2, h3), (wf['w_down'],) = _attn_fwd(q, kv, x1, w_o, p['ffn_norm_g'], comm=_gather_pass_on(bufs_d))
    gu, act, dx3, loss_lanes, d_final_g = _ffn_fwd(h3, wf['w_up'], p['ffn_conv_w'], p['ffn_conv_b'], wf['w_down'],
                                                   x2, p['final_norm_g'], target)

    dgu, dx2, d_ffn_g, d_ffn_cw, d_ffn_cb = _ffn_bwd(dx3, wf['w_down'], wf['w_up'], gu, x2, p['ffn_norm_g'],
                                                     p['ffn_conv_w'], p['ffn_conv_b'])
    g_down = _wgrad(act, dx3[None], "wgrad_down").reshape(nch, -1, D)
    g_up, other = _wgrad(h3[None], dgu, "wgrad_up", comm=_exchange_halves([g_down]))
    (p_down,) = _rs_add(['w_down'], [g_down], other)
    sc_down, ex_up = _scatter_chip_sums([p_down]), _exchange_halves([g_up])
    (dq, dx1, dkv, d_xa_g), couts = _attn_bwd(dx2, w_o, q, kv, x1, p['xa_norm_g'], w_q, comm=_merge(sc_down, ex_up))
    recv, other = _split(couts, sc_down, ex_up)
    f_down = _rs_sum(['w_down'], [p_down], recv)
    (p_up,) = _rs_add(['w_up'], [g_up], other)
    mid = ['w_o', 'w_q', 'w_kv']
    g_o = _wgrad(o[None], dx2[None], "wgrad_o").reshape(nch, -1, D)
    g_q = _wgrad(h2[None], dq[None], "wgrad_q").reshape(nch, -1, D)
    g_kv, d_mem_g = _kv_bwd(dkv, wf['w_kv'], mem, p['mem_norm_g'], m)
    join_down, sc_up, ex_mid = _join_halves(f_down), _scatter_chip_sums([p_up]), _exchange_halves([g_o, g_q, g_kv])
    (dz_c, d_conf_cw, d_conf_cb, d_ln_g, d_ln_b), couts = _conf_bwd(
        dx1, w_out2[1], z, c1, p['conf_conv_w'], p['conf_ln_g'], p['conf_ln_b'],
        comm=_merge(join_down, sc_up, ex_mid))
    (r_down,), recv, other = _split(couts, join_down, sc_up, ex_mid)
    p_up = [p_up]
    p_mid = _rs_add(mid, [g_o, g_q, g_kv], other)
    join_up, sc_mid = _join_halves(_rs_sum(['w_up'], p_up, recv)), _scatter_chip_sums(p_mid)
    (dz, d_wa, d_wx, d_ba, d_bx, d_lam, d_lru_cw, d_lru_cb), couts = _lru_bwd(
        dx1, w_out2[0], z, h, p['lru_conv_w'], p['lru_conv_b'], wa_d, p['lru_b_a'], wx_d, p['lru_b_x'],
        p['lru_lambda'], dz_c, comm=_merge(join_up, sc_mid))
    (r_up,), recv = _split(couts, join_up, sc_mid)
    f_mid = _rs_sum(mid, p_mid, recv)
    grad_x, d_mix_g = _bwd_in(dz, w_in_f, x, p['mix_norm_g'], dx1)

    small_g = {'mix_norm_g': d_mix_g, 'lru_conv_w': d_lru_cw, 'lru_conv_b': d_lru_cb,
               'lru_w_a': _block_diag_parts(d_wa, heads), 'lru_b_a': d_ba,
               'lru_w_x': _block_diag_parts(d_wx, heads), 'lru_b_x': d_bx, 'lru_lambda': d_lam,
               'conf_conv_w': d_conf_cw, 'conf_conv_b': d_conf_cb, 'conf_ln_g': d_ln_g, 'conf_ln_b': d_ln_b,
               'xa_norm_g': d_xa_g, 'mem_norm_g': d_mem_g, 'ffn_norm_g': d_ffn_g,
               'ffn_conv_w': d_ffn_cw, 'ffn_conv_b': d_ffn_cb, 'final_norm_g': d_final_g}
    names = list(small_g)
    shapes = [small_g[n].shape for n in names]
    join_mid = _join_halves(f_mid)
    small_sum = _all_reduce_rows(_pack_rows([loss_lanes] + [small_g[n] for n in names]), loss_row=0)
    g_in, couts = _wgrad(h1[None], dz, "wgrad_in", comm=_merge(join_mid, small_sum))
    r_mid, (summed, loss) = _split(couts, join_mid, small_sum)
    g_out_l, other = _wgrad(y_lru[None], dx1[None], "wgrad_out_lru", comm=_exchange_halves([g_in]))
    p_in = _rs_add(['w_in'], [g_in], other)
    g_out_c, recv = _wgrad(c3[None], dx1[None], "wgrad_out_conf", comm=_scatter_chip_sums(p_in))
    f_in = _rs_sum(['w_in'], p_in, recv)

    g_out = jnp.concatenate([g_out_l, g_out_c], axis=0).reshape(nch, -1, D)
    join_in, rs_out = _join_halves(f_in), _reduce_scatter_in_vmem(g_out)
    (r_in,), (r_out,) = _split(_run_comm(_merge(join_in, rs_out), "rs_last"), join_in, rs_out)
    big = dict(zip(['w_down', 'w_up'] + mid + ['w_out', 'w_in'], [r_down, r_up] + r_mid + [r_out, r_in]))
    return grad_x, big, summed, loss, names, [loss_lanes.shape] + shapes


def kernel(x, mem, mix_norm_g, w_in, lru_conv_w, lru_conv_b, lru_w_a, lru_b_a, lru_w_x, lru_b_x, lru_lambda, conf_conv_w, conf_conv_b, conf_ln_g, conf_ln_b, w_out, xa_norm_g, mem_norm_g, w_q, w_kv, w_o, ffn_norm_g, w_up, ffn_conv_w, ffn_conv_b, w_down, final_norm_g, loss_target, m_mix_norm_g, m_w_in, m_lru_conv_w, m_lru_conv_b, m_lru_w_a, m_lru_b_a, m_lru_w_x, m_lru_b_x, m_lru_lambda, m_conf_conv_w, m_conf_conv_b, m_conf_ln_g, m_conf_ln_b, m_w_out, m_xa_norm_g, m_mem_norm_g, m_w_q, m_w_kv, m_w_o, m_ffn_norm_g, m_w_up, m_ffn_conv_w, m_ffn_conv_b, m_w_down, m_final_norm_g, v_mix_norm_g, v_w_in, v_lru_conv_w, v_lru_conv_b, v_lru_w_a, v_lru_b_a, v_lru_w_x, v_lru_b_x, v_lru_lambda, v_conf_conv_w, v_conf_conv_b, v_conf_ln_g, v_conf_ln_b, v_w_out, v_xa_norm_g, v_mem_norm_g, v_w_q, v_w_kv, v_w_o, v_ffn_norm_g, v_w_up, v_ffn_conv_w, v_ffn_conv_b, v_w_down, v_final_norm_g):
    given = dict(locals())
    w = {n: given[n] for n in WEIGHTS}
    mom = {n: given["m_" + n] for n in WEIGHTS}
    var = {n: given["v_" + n] for n in WEIGHTS}
    xi, yi, ci = lax.axis_index("x"), lax.axis_index("y"), lax.axis_index("c")
    chip = 2 * xi + yi

    shards = {n: w[n][0].astype(WIRE_DTYPE) for n in BIG}
    tap_full = []
    for n in COL_SHARDED_SMALL:
        s = w[n][0]
        full = jnp.zeros((s.shape[0], N_CHIPS * s.shape[1]), F32)
        s = jnp.where(ci == 0, s, jnp.zeros_like(s))
        tap_full.append(lax.dynamic_update_slice(full, s, (0, chip * s.shape[1])))
    small = {n: (w[n] if w[n].ndim == 1 else w[n][0]) for n in SMALL if n not in COL_SHARDED_SMALL}
    small = {n: (a.reshape(1, -1) if a.ndim == 1 else a) for n, a in small.items()}

    grad_x, big_g, summed, loss, small_names, packed_shapes = _step(
        x[0], mem[0], loss_target[0], shards, small, _pack_rows(tap_full), [t.shape for t in tap_full])
    small_sum = dict(zip(small_names, _unpack_rows(summed, packed_shapes)[1:]))

    grads = {}
    for n in WEIGHTS:
        if n in BIG:
            g = big_g[n]
        elif n in COL_SHARDED_SMALL:
            width = w[n].shape[-1]
            g = lax.dynamic_slice_in_dim(small_sum[n], chip * width, width, axis=1)
        else:
            g = small_sum[n]
        grads[n] = g.reshape(w[n].shape)

    delta, new_m, new_v = {}, {}, {}
    for n in BIG:
        d, nm, nv, g = _adamw(w[n][0], grads[n][0], mom[n][0], var[n][0], "adamw_" + n)
        delta[n], new_m[n], new_v[n], grads[n] = d[None], nm[None], nv[None], g[None]
    flat = lambda a: a.reshape(-1, a.shape[-1])
    outs = _adamw_many(*[[flat(src[n]) for n in SMALL] for src in (w, grads, mom, var)], "adamw_small")
    for out, arrays in zip((delta, new_m, new_v), outs):
        out.update({n: a.reshape(w[n].shape) for n, a in zip(SMALL, arrays)})

    return (loss[0, 0], grad_x[None], *[grads[n] for n in WEIGHTS], *[delta[n] for n in WEIGHTS],
            *[new_m[n] for n in WEIGHTS], *[new_v[n] for n in WEIGHTS])
```
